```python
import math
import jax, jax.numpy as jnp
from jax import lax
import numpy as np

D_MODEL = 2048
BATCH = 2
SEQ = 4096
DEPTH = 1
DEC_BATCH = 32
DEC_SEQ = 8
PAST_LEN = 8192
PAGE_SIZE = 128

D_ATT = D_MODEL // 2
ATT_HEADS = 8
ATT_HD = D_ATT // ATT_HEADS
DILATIONS = ((128, 1), (512, 4), (2048, 16))
WINDOW = 2048
Q_BLOCK = 128
D_SSM = D_MODEL - D_ATT
SSM_GROUP_CH = 16
N_SSM_GROUPS = D_SSM // SSM_GROUP_CH
SSM_STATE = 64
D_MIX = D_ATT + D_SSM
N_MEM = 256
MEM_HEADS = 4
MEM_HD = D_MODEL // MEM_HEADS
N_EXPERT_GROUPS = 4
EXPERTS_PER_GROUP = 8
N_EXPERTS = N_EXPERT_GROUPS * EXPERTS_PER_GROUP
TOP_K_INNER = 2
D_EXPERT = D_MODEL // 4
DEEPNORM_ALPHA = (2.0 * DEPTH) ** 0.25
DEEPNORM_BETA = (8.0 * DEPTH) ** -0.25
LN_EPS = 1e-5
RMS_EPS = 1e-6
DT_MIN = 0.001
DT_MAX = 0.1

kernel_name = 'hymba_s5_dilated_attn_hmoe_decode_step'


def _layernorm(x, g, b):
    xf = x.astype(jnp.float32)
    mu = jnp.mean(xf, -1, keepdims=True)
    var = jnp.mean(jnp.square(xf - mu), -1, keepdims=True)
    y = (xf - mu) * lax.rsqrt(var + LN_EPS) * g.astype(jnp.float32) + b.astype(jnp.float32)
    return y.astype(x.dtype)


def _rmsnorm(x, g):
    xf = x.astype(jnp.float32)
    return xf * lax.rsqrt(jnp.mean(jnp.square(xf), -1, keepdims=True) + RMS_EPS) * g.astype(jnp.float32)


def _dilated_block(q_blk, q_pos, k_all, v_all, kv_start):
    tk = k_all.shape[1]
    qf = q_blk.astype(jnp.float32) * (ATT_HD ** -0.5)
    outs, lses = [], []
    for w, d in DILATIONS:
        steps = jnp.arange(w // d + 1, dtype=jnp.int32) * d
        kpos = q_pos[:, None] - steps[None, :]
        valid = kpos >= kv_start
        idx = jnp.clip(kpos - kv_start, 0, tk - 1)
        kg = jnp.take(k_all, idx, axis=1).astype(jnp.float32)
        vg = jnp.take(v_all, idx, axis=1).astype(jnp.float32)
        s = jnp.einsum('bqhd,bqkhd->bhqk', qf, kg)
        s = jnp.where(valid[None, None], s, -jnp.inf)
        m = jnp.max(s, axis=-1, keepdims=True)
        p = jnp.exp(s - m)
        den = jnp.sum(p, axis=-1)
        o = jnp.einsum('bhqk,bqkhd->bqhd', p, vg) / jnp.transpose(den, (0, 2, 1))[..., None]
        outs.append(o)
        lses.append(jnp.transpose(m[..., 0] + jnp.log(den), (0, 2, 1)))
    wts = jax.nn.softmax(jnp.stack(lses, 0), axis=0)
    return jnp.sum(wts[..., None] * jnp.stack(outs, 0), axis=0)


def _dilated_attention_prompt(q, k, v):
    bsz, t = q.shape[:2]

    def block(i):
        q0 = i * Q_BLOCK
        qb = lax.dynamic_slice_in_dim(q, q0, Q_BLOCK, axis=1)
        pos = q0 + jnp.arange(Q_BLOCK, dtype=jnp.int32)
        return _dilated_block(qb, pos, k, v, 0)

    out = lax.map(block, jnp.arange(t // Q_BLOCK, dtype=jnp.int32))
    return jnp.moveaxis(out, 0, 1).reshape(bsz, t, ATT_HEADS, ATT_HD)


def _s5(u, h0_re, h0_im, lam_re, lam_im, log_dt, b_re, b_im, c_re, c_im, d_skip):
    f32 = jnp.float32
    bsz, t, _ = u.shape
    uf = u.astype(f32).reshape(bsz, t, N_SSM_GROUPS, SSM_GROUP_CH)
    dt = jnp.exp(log_dt.astype(f32))[:, None]
    lr, li = lam_re.astype(f32), lam_im.astype(f32)
    mag = jnp.exp(lr * dt)
    a_re, a_im = mag * jnp.cos(li * dt), mag * jnp.sin(li * dt)
    den = lr * lr + li * li
    nr, ni = a_re - 1.0, a_im
    f_re, f_im = (nr * lr + ni * li) / den, (ni * lr - nr * li) / den
    br, bi = b_re.astype(f32), b_im.astype(f32)
    bb_re = f_re[..., None] * br - f_im[..., None] * bi
    bb_im = f_re[..., None] * bi + f_im[..., None] * br
    x_re = jnp.einsum('btgc,gpc->btgp', uf, bb_re)
    x_im = jnp.einsum('btgc,gpc->btgp', uf, bb_im)
    h0r, h0i = h0_re.astype(f32), h0_im.astype(f32)
    x_re = x_re.at[:, 0].add(a_re * h0r - a_im * h0i)
    x_im = x_im.at[:, 0].add(a_re * h0i + a_im * h0r)
    ar = jnp.broadcast_to(a_re, x_re.shape)
    ai = jnp.broadcast_to(a_im, x_im.shape)

    def comb(e1, e2):
        a1r, a1i, b1r, b1i = e1
        a2r, a2i, b2r, b2i = e2
        return (a2r * a1r - a2i * a1i, a2r * a1i + a2i * a1r,
                a2r * b1r - a2i * b1i + b2r, a2r * b1i + a2i * b1r + b2i)

    _, _, h_re, h_im = lax.associative_scan(comb, (ar, ai, x_re, x_im), axis=1)
    y = (jnp.einsum('btgp,gcp->btgc', h_re, c_re.astype(f32))
         - jnp.einsum('btgp,gcp->btgc', h_im, c_im.astype(f32))
         + d_skip.astype(f32) * uf)
    return y.reshape(bsz, t, D_SSM), h_re[:, -1], h_im[:, -1]


def _mem_attention(x, mem_k, mem_v, w_mq, w_mo):
    bsz, t, _ = x.shape
    q = (x @ w_mq).reshape(bsz, t, MEM_HEADS, MEM_HD).astype(jnp.float32)
    s = jnp.einsum('bthd,bmhd->bhtm', q, mem_k.astype(jnp.float32)) * (MEM_HD ** -0.5)
    p = jax.nn.softmax(s, axis=-1)
    o = jnp.einsum('bhtm,bmhd->bthd', p, mem_v.astype(jnp.float32)).reshape(bsz, t, D_MODEL)
    return o.astype(x.dtype) @ w_mo


def _hier_moe(x, w_r1, b_r1, w_r2, b_r2, w_gate, w_up, w_down):
    f32 = jnp.float32
    g_logits = (x @ w_r1).astype(f32) + b_r1.astype(f32)
    g_prob = jax.nn.softmax(g_logits, axis=-1)
    g_sel = jnp.argmax(g_logits, axis=-1)
    g_w = jnp.take_along_axis(g_prob, g_sel[:, None], axis=-1)[:, 0]
    e_logits = jnp.einsum('nd,dge->nge', x, w_r2).astype(f32) + b_r2.astype(f32)
    e_sel = jnp.take_along_axis(e_logits, g_sel[:, None, None], axis=1)[:, 0]
    top_v, top_i = lax.top_k(e_sel, TOP_K_INNER)
    top_w = jax.nn.softmax(top_v, axis=-1) * g_w[:, None]
    expert_id = g_sel[:, None] * EXPERTS_PER_GROUP + top_i
    gates = jnp.sum(jax.nn.one_hot(expert_id, N_EXPERTS, dtype=f32) * top_w[..., None], axis=1)
    hg = jnp.einsum('nd,edf->nef', x, w_gate)
    hu = jnp.einsum('nd,edf->nef', x, w_up)
    h = jax.nn.silu(hg) * hu * gates[..., None].astype(hg.dtype)
    return jnp.einsum('nef,efd->nd', h, w_down)


def _layer(x, p, mem_k, mem_v, h0_re, h0_im, win_k=None, win_v=None):
    bsz, t, _ = x.shape
    proj = x @ p['w_in']
    q, k, v, u = jnp.split(proj, [D_ATT, 2 * D_ATT, 3 * D_ATT], axis=-1)
    q = q.reshape(bsz, t, ATT_HEADS, ATT_HD)
    k = k.reshape(bsz, t, ATT_HEADS, ATT_HD)
    v = v.reshape(bsz, t, ATT_HEADS, ATT_HD)
    if win_k is None:
        attn = _dilated_attention_prompt(q, k, v)
        wp = min(WINDOW, t)
        new_k, new_v = k[:, t - wp:], v[:, t - wp:]
    else:
        wb = win_k.shape[1]
        k_all = jnp.concatenate([win_k, k.astype(win_k.dtype)], axis=1)
        v_all = jnp.concatenate([win_v, v.astype(win_v.dtype)], axis=1)
        q_pos = PAST_LEN + jnp.arange(t, dtype=jnp.int32)
        attn = _dilated_block(q, q_pos, k_all, v_all, PAST_LEN - wb)
        new_k, new_v = k, v
    attn = attn.reshape(bsz, t, D_ATT)
    y_ssm, h_re, h_im = _s5(u, h0_re, h0_im, p['ssm_lam_re'], p['ssm_lam_im'], p['ssm_log_dt'],
                            p['ssm_b_re'], p['ssm_b_im'], p['ssm_c_re'], p['ssm_c_im'], p['ssm_d'])
    yg = jax.nn.gelu(y_ssm)
    ssm_out = yg * jax.nn.sigmoid(yg @ p['w_glu'].astype(jnp.float32))
    mixed = jnp.concatenate([_rmsnorm(attn, p['g_attn']), _rmsnorm(ssm_out, p['g_ssm'])], axis=-1)
    mix_out = mixed.astype(x.dtype) @ p['w_out']
    x = _layernorm(DEEPNORM_ALPHA * x + mix_out, p['ln1_g'], p['ln1_b'])
    x = _layernorm(DEEPNORM_ALPHA * x + _mem_attention(x, mem_k, mem_v, p['w_mq'], p['w_mo']),
                   p['ln2_g'], p['ln2_b'])
    moe = _hier_moe(x.reshape(bsz * t, D_MODEL), p['w_r1'], p['b_r1'], p['w_r2'], p['b_r2'],
                    p['w_gate'], p['w_up'], p['w_down']).reshape(bsz, t, D_MODEL)
    x = _layernorm(DEEPNORM_ALPHA * x + moe.astype(x.dtype), p['ln3_g'], p['ln3_b'])
    return x, new_k, new_v, h_re, h_im


def setup_inputs(seed: int = 0) -> dict:
    key = jax.random.key(seed)
    ks = iter(jax.random.split(key, 64))
    f32 = jnp.float32

    def nrm(shape, scale):
        return jax.random.normal(next(ks), shape, f32) * scale

    wb = min(WINDOW, PAST_LEN)
    L, G, P, C = DEPTH, N_SSM_GROUPS, SSM_STATE, SSM_GROUP_CH
    w_in = nrm((L, D_MODEL, 3 * D_ATT + D_SSM), D_MODEL ** -0.5)
    w_in = w_in.at[..., 2 * D_ATT:3 * D_ATT].multiply(DEEPNORM_BETA)
    n_idx = jnp.arange(P, dtype=f32)
    lam_re = -0.5 * jnp.exp(nrm((L, G, P), 0.02))
    lam_im = math.pi * n_idx[None, None, :] + nrm((L, G, P), 0.01)
    log_dt = jax.random.uniform(next(ks), (L, G), f32, math.log(DT_MIN), math.log(DT_MAX))
    return {
        'x_prompt': nrm((BATCH, SEQ, D_MODEL), 1.0),
        'x_sample': nrm((DEC_BATCH, DEC_SEQ, D_MODEL), 1.0),
        'cache_win_k': nrm((L, DEC_BATCH, wb, ATT_HEADS, ATT_HD), 1.0),
        'cache_win_v': nrm((L, DEC_BATCH, wb, ATT_HEADS, ATT_HD), 1.0),
        'state_ssm_re': nrm((L, DEC_BATCH, G, P), 0.5),
        'state_ssm_im': nrm((L, DEC_BATCH, G, P), 0.5),
        'cache_mem_k': nrm((L, DEC_BATCH, N_MEM, MEM_HEADS, MEM_HD), 1.0),
        'cache_mem_v': nrm((L, DEC_BATCH, N_MEM, MEM_HEADS, MEM_HD), 1.0),
        'mem_prompt': nrm((BATCH, N_MEM, D_MODEL), 1.0),
        'w_in': w_in,
        'ssm_lam_re': lam_re,
        'ssm_lam_im': lam_im,
        'ssm_log_dt': log_dt,
        'ssm_b_re': nrm((L, G, P, C), (2.0 * C) ** -0.5),
        'ssm_b_im': nrm((L, G, P, C), (2.0 * C) ** -0.5),
        'ssm_c_re': nrm((L, G, C, P), (2.0 * P) ** -0.5),
        'ssm_c_im': nrm((L, G, C, P), (2.0 * P) ** -0.5),
        'ssm_d': nrm((L, G, C), 1.0),
        'w_glu': nrm((L, D_SSM, D_SSM), D_SSM ** -0.5),
        'g_attn': 1.0 + nrm((L, D_ATT), 0.02),
        'g_ssm': 1.0 + nrm((L, D_SSM), 0.02),
        'w_out': nrm((L, D_MIX, D_MODEL), D_MIX ** -0.5 * DEEPNORM_BETA),
        'ln1_g': 1.0 + nrm((L, D_MODEL), 0.02),
        'ln1_b': nrm((L, D_MODEL), 0.02),
        'w_mq': nrm((L, D_MODEL, D_MODEL), D_MODEL ** -0.5),
        'w_mk': nrm((L, D_MODEL, D_MODEL), D_MODEL ** -0.5),
        'w_mv': nrm((L, D_MODEL, D_MODEL), D_MODEL ** -0.5 * DEEPNORM_BETA),
        'w_mo': nrm((L, D_MODEL, D_MODEL), D_MODEL ** -0.5 * DEEPNORM_BETA),
        'ln2_g': 1.0 + nrm((L, D_MODEL), 0.02),
        'ln2_b': nrm((L, D_MODEL), 0.02),
        'w_r1': nrm((L, D_MODEL, N_EXPERT_GROUPS), D_MODEL ** -0.5),
        'b_r1': nrm((L, N_EXPERT_GROUPS), 0.01),
        'w_r2': nrm((L, D_MODEL, N_EXPERT_GROUPS, EXPERTS_PER_GROUP), D_MODEL ** -0.5),
        'b_r2': nrm((L, N_EXPERT_GROUPS, EXPERTS_PER_GROUP), 0.01),
        'w_gate': nrm((L, N_EXPERTS, D_MODEL, D_EXPERT), D_MODEL ** -0.5),
        'w_up': nrm((L, N_EXPERTS, D_MODEL, D_EXPERT), D_MODEL ** -0.5),
        'w_down': nrm((L, N_EXPERTS, D_EXPERT, D_MODEL), D_EXPERT ** -0.5 * DEEPNORM_BETA),
        'ln3_g': 1.0 + nrm((L, D_MODEL), 0.02),
        'ln3_b': nrm((L, D_MODEL), 0.02),
    }


def reference(x_prompt, x_sample, cache_win_k, cache_win_v, state_ssm_re, state_ssm_im,
              cache_mem_k, cache_mem_v, mem_prompt, w_in, ssm_lam_re, ssm_lam_im, ssm_log_dt,
              ssm_b_re, ssm_b_im, ssm_c_re, ssm_c_im, ssm_d, w_glu, g_attn, g_ssm, w_out,
              ln1_g, ln1_b, w_mq, w_mk, w_mv, w_mo, ln2_g, ln2_b, w_r1, b_r1, w_r2, b_r2,
              w_gate, w_up, w_down, ln3_g, ln3_b):
    yp, ys = x_prompt, x_sample
    bsz = x_prompt.shape[0]
    wkp, wvp, wks, wvs = [], [], [], []
    srp, sip, srs, sis = [], [], [], []
    mkp, mvp = [], []
    for l in range(DEPTH):
        p = {
            'w_in': w_in[l], 'ssm_lam_re': ssm_lam_re[l], 'ssm_lam_im': ssm_lam_im[l],
            'ssm_log_dt': ssm_log_dt[l], 'ssm_b_re': ssm_b_re[l], 'ssm_b_im': ssm_b_im[l],
            'ssm_c_re': ssm_c_re[l], 'ssm_c_im': ssm_c_im[l], 'ssm_d': ssm_d[l], 'w_glu': w_glu[l],
            'g_attn': g_attn[l], 'g_ssm': g_ssm[l], 'w_out': w_out[l],
            'ln1_g': ln1_g[l], 'ln1_b': ln1_b[l], 'w_mq': w_mq[l], 'w_mo': w_mo[l],
            'ln2_g': ln2_g[l], 'ln2_b': ln2_b[l], 'w_r1': w_r1[l], 'b_r1': b_r1[l],
            'w_r2': w_r2[l], 'b_r2': b_r2[l], 'w_gate': w_gate[l], 'w_up': w_up[l],
            'w_down': w_down[l], 'ln3_g': ln3_g[l], 'ln3_b': ln3_b[l],
        }
        mem_k = (mem_prompt @ w_mk[l]).reshape(bsz, N_MEM, MEM_HEADS, MEM_HD)
        mem_v = (mem_prompt @ w_mv[l]).reshape(bsz, N_MEM, MEM_HEADS, MEM_HD)
        h0 = jnp.zeros((bsz, N_SSM_GROUPS, SSM_STATE), jnp.float32)
        yp, kp, vp, hrp, hip = _layer(yp, p, mem_k, mem_v, h0, h0)
        ys, kn, vn, hrs, his = _layer(ys, p, cache_mem_k[l], cache_mem_v[l], state_ssm_re[l],
                                      state_ssm_im[l], cache_win_k[l], cache_win_v[l])
        wkp.append(kp); wvp.append(vp); wks.append(kn); wvs.append(vn)
        srp.append(hrp); sip.append(hip); srs.append(hrs); sis.append(his)
        mkp.append(mem_k); mvp.append(mem_v)
    return (yp, ys, jnp.stack(wkp), jnp.stack(wvp), jnp.stack(wks), jnp.stack(wvs),
            jnp.stack(srp), jnp.stack(sip), jnp.stack(srs), jnp.stack(sis),
            jnp.stack(mkp), jnp.stack(mvp))
```

```python
import functools
import math

import numpy as np
import jax
import jax.numpy as jnp
from jax import lax
from jax.experimental import pallas as pl
from jax.experimental.pallas import tpu as pltpu

F32 = jnp.float32
BF16 = jnp.bfloat16
I32 = jnp.int32

D_MODEL = 2048
PAST_LEN = 8192
D_ATT = D_MODEL // 2
ATT_HEADS = 8
ATT_HD = D_ATT // ATT_HEADS
DILATIONS = ((128, 1), (512, 4), (2048, 16))
D_SSM = D_MODEL - D_ATT
SSM_GROUP_CH = 16
N_SSM_GROUPS = D_SSM // SSM_GROUP_CH
SSM_STATE = 64
N_MEM = 256
MEM_HEADS = 4
MEM_HD = D_MODEL // MEM_HEADS
N_EXPERT_GROUPS = 4
EXPERTS_PER_GROUP = 8
N_EXPERTS = N_EXPERT_GROUPS * EXPERTS_PER_GROUP
D_EXPERT = D_MODEL // 4
DEPTH = 1
DEEPNORM_ALPHA = (2.0 * DEPTH) ** 0.25
LN_EPS = 1e-5
RMS_EPS = 1e-6

LANES = 128
SUBLANES = 8
Q_BLOCK = 128
SSM_LANE_TILE = 128
SSM_GROUPS_PER_TILE = SSM_LANE_TILE // SSM_GROUP_CH
SSM_STATES_PER_TILE = SSM_GROUPS_PER_TILE * SSM_STATE
SSM_SEGMENTS = 8
MOE_TILE = 256
ROUTER_LANES = 128
NEG_INF = float("-inf")


def _cparams(semantics, vmem_mib):
    return pltpu.CompilerParams(dimension_semantics=semantics,
                                vmem_limit_bytes=int(vmem_mib) << 20)


def _layernorm(y, g, b):
    mu = jnp.mean(y, axis=-1, keepdims=True)
    yc = y - mu
    var = jnp.mean(yc * yc, axis=-1, keepdims=True)
    return yc * lax.rsqrt(var + LN_EPS) * g + b


def _rmsnorm(v, g):
    return v * lax.rsqrt(jnp.mean(v * v, axis=-1, keepdims=True) + RMS_EPS) * g


def _dot(a, b):
    return jnp.dot(a, b, preferred_element_type=F32)


def _dot_nt(a, b):
    return lax.dot_general(a, b, (((1,), (1,)), ((), ())), preferred_element_type=F32)


def _mm_body(x_ref, w_ref, o_ref):
    o_ref[...] = _dot(x_ref[...].astype(BF16), w_ref[...]).astype(o_ref.dtype)


def _matmul(x, w, *, tm, tn, name):
    m, k = x.shape
    n = w.shape[1]
    vmem = 2 * (tm * k * x.dtype.itemsize + k * tn * 2 + tm * tn * 4) / 2**20 + 8
    return pl.pallas_call(
        _mm_body,
        out_shape=jax.ShapeDtypeStruct((m, n), F32),
        grid=(n // tn, m // tm),
        in_specs=[pl.BlockSpec((tm, k), lambda j, i: (i, 0)),
                  pl.BlockSpec((k, tn), lambda j, i: (0, j))],
        out_specs=pl.BlockSpec((tm, tn), lambda j, i: (i, j)),
        compiler_params=_cparams(("parallel", "parallel"), vmem),
        name=name,
    )(x, w)


def _attn_prompt_body(q_ref, k_ref, v_ref, o_ref, acc_s, m_s, l_s, *, seq, dilations):
    scale = ATT_HD ** -0.5
    qi = lax.broadcasted_iota(I32, (Q_BLOCK, Q_BLOCK), 0)
    kj = lax.broadcasted_iota(I32, (Q_BLOCK, Q_BLOCK), 1)
    cur_ok = kj <= qi
    prev_ok = kj >= qi

    for br, (_, d) in enumerate(dilations):
        span = d * Q_BLOCK
        nb = seq // span

        def block(t, carry, br=br, d=d, span=span, nb=nb):
            r = t // nb
            ib = t % nb
            start = r + ib * span
            pstart = jnp.maximum(start - span, r)
            rows = pl.ds(start, Q_BLOCK, stride=d)
            prows = pl.ds(pstart, Q_BLOCK, stride=d)
            q = (q_ref[rows, :] * scale).astype(BF16)
            kc = k_ref[rows, :].astype(BF16)
            kp = k_ref[prows, :].astype(BF16)
            sc = jnp.where(cur_ok, _dot_nt(q, kc), NEG_INF)
            sp = jnp.where(jnp.logical_and(prev_ok, ib > 0), _dot_nt(q, kp), NEG_INF)
            m = jnp.maximum(jnp.max(sc, axis=-1, keepdims=True),
                            jnp.max(sp, axis=-1, keepdims=True))
            pc = jnp.exp(sc - m)
            pp = jnp.exp(sp - m)
            l = jnp.sum(pc, axis=-1, keepdims=True) + jnp.sum(pp, axis=-1, keepdims=True)
            acc = (_dot(pc.astype(BF16), v_ref[rows, :].astype(BF16))
                   + _dot(pp.astype(BF16), v_ref[prows, :].astype(BF16)))
            acc_s[br, rows, :] = acc
            m_s[br, rows, :] = jnp.broadcast_to(m, (Q_BLOCK, LANES))
            l_s[br, rows, :] = jnp.broadcast_to(l, (Q_BLOCK, LANES))
            return carry

        lax.fori_loop(0, seq // Q_BLOCK, block, 0)

    chunk = 256
    nbr = len(dilations)

    def merge(c, carry):
        rows = pl.ds(pl.multiple_of(c * chunk, chunk), chunk)
        ms = [m_s[b, rows, :] for b in range(nbr)]
        mx = functools.reduce(jnp.maximum, ms)
        es = [jnp.exp(mi - mx) for mi in ms]
        num = sum(es[b] * acc_s[b, rows, :] for b in range(nbr))
        den = sum(es[b] * l_s[b, rows, :] for b in range(nbr))
        o_ref[rows, :] = num / den
        return carry

    lax.fori_loop(0, seq // chunk, merge, 0)


def _attn_prompt(proj, *, n_batch, seq, dilations=DILATIONS):
    for w, d in dilations:
        assert w // d == Q_BLOCK and seq % (d * Q_BLOCK) == 0
    nbr = len(dilations)
    blk = lambda off: pl.BlockSpec((seq, ATT_HD), lambda b, h, off=off: (b, off + h))
    vmem = (4 * 2 + 3 * nbr) * seq * ATT_HD * 4 / 2**20 + 8
    return pl.pallas_call(
        functools.partial(_attn_prompt_body, seq=seq, dilations=dilations),
        out_shape=jax.ShapeDtypeStruct((n_batch * seq, D_ATT), F32),
        grid=(n_batch, ATT_HEADS),
        in_specs=[blk(0), blk(ATT_HEADS), blk(2 * ATT_HEADS)],
        out_specs=pl.BlockSpec((seq, ATT_HD), lambda b, h: (b, h)),
        scratch_shapes=[pltpu.VMEM((nbr, seq, ATT_HD), F32)] * 3,
        compiler_params=_cparams(("parallel", "parallel"), vmem),
        name="attn_prompt",
    )(proj, proj, proj)


def _sample_key_multiplicity(n_new, n_cache, past_len, dilations):
    d_max = max(d for _, d in dilations)
    tail = max(w for w, d in dilations if d != d_max)
    assert past_len % d_max == 0 and n_cache % d_max == 0 and n_new <= d_max // 2
    assert tail % d_max == 0 and tail <= n_cache
    half = d_max // 2
    n_grid = (n_cache - tail) // d_max
    kv_start = past_len - n_cache
    grid_rows = (np.arange(n_grid)[:, None] * d_max + np.arange(half)[None, :]).reshape(-1)
    tail_rows = n_cache - tail + np.arange(tail)
    new_rows = n_cache + np.arange(n_new)
    qpos = past_len + np.arange(n_new)

    def mult(rows):
        kpos = kv_start + rows
        delta = qpos[:, None] - kpos[None, :]
        c = np.zeros(delta.shape, np.float32)
        for w, d in dilations:
            c += ((delta >= 0) & (delta <= w) & (delta % d == 0) & (kpos[None, :] >= kv_start))
        return c

    fetched = np.zeros(n_cache + n_new, bool)
    fetched[grid_rows] = True
    fetched[tail_rows] = True
    fetched[new_rows] = True
    assert not mult(np.nonzero(~fetched)[0]).any()
    return mult(grid_rows), mult(tail_rows), mult(new_rows), n_grid, tail, half, d_max


def _attn_sample_body(q_ref, kn_ref, vn_ref, kg_ref, kt_ref, vg_ref, vt_ref,
                      cg_ref, ct_ref, cn_ref, o_ref):
    scale = ATT_HD ** -0.5
    cg, ct, cn = cg_ref[...], ct_ref[...], cn_ref[...]
    n_grid_rows = cg.shape[1]
    for h in range(ATT_HEADS):
        sl = slice(h * ATT_HD, (h + 1) * ATT_HD)
        q = (q_ref[:, sl] * scale).astype(BF16)
        kg = kg_ref[:, :, sl].reshape(n_grid_rows, ATT_HD).astype(BF16)
        vg = vg_ref[:, :, sl].reshape(n_grid_rows, ATT_HD).astype(BF16)
        sg = jnp.where(cg > 0, _dot_nt(q, kg), NEG_INF)
        st = jnp.where(ct > 0, _dot_nt(q, kt_ref[:, sl].astype(BF16)), NEG_INF)
        sn = jnp.where(cn > 0, _dot_nt(q, kn_ref[:, sl].astype(BF16)), NEG_INF)
        m = jnp.maximum(jnp.maximum(jnp.max(sg, axis=-1, keepdims=True),
                                    jnp.max(st, axis=-1, keepdims=True)),
                        jnp.max(sn, axis=-1, keepdims=True))
        pg = cg * jnp.exp(sg - m)
        pt = ct * jnp.exp(st - m)
        pn = cn * jnp.exp(sn - m)
        l = (jnp.sum(pg, axis=-1, keepdims=True) + jnp.sum(pt, axis=-1, keepdims=True)
             + jnp.sum(pn, axis=-1, keepdims=True))
        acc = (_dot(pg.astype(BF16), vg) + _dot(pt.astype(BF16), vt_ref[:, sl].astype(BF16))
               + _dot(pn.astype(BF16), vn_ref[:, sl].astype(BF16)))
        o_ref[:, sl] = acc / l


def _attn_sample(proj, win_k, win_v, *, row0, n_seq, n_new, past_len=PAST_LEN,
                 dilations=DILATIONS):
    n_cache = win_k.shape[1]
    cg, ct, cn, n_grid, tail, half, d_max = _sample_key_multiplicity(
        n_new, n_cache, past_len, dilations)
    assert row0 % n_new == 0 and n_new % SUBLANES == 0 and half == SUBLANES
    rb = row0 // n_new
    n_groups = n_cache // d_max
    kgv = win_k.reshape(n_seq, n_groups, d_max, D_ATT)
    vgv = win_v.reshape(n_seq, n_groups, d_max, D_ATT)
    new = lambda off: pl.BlockSpec((n_new, D_ATT), lambda b, off=off: (rb + b, off))
    grid_spec = pl.BlockSpec((None, n_grid, half, D_ATT), lambda b: (b, 0, 0, 0))
    tail_spec = pl.BlockSpec((None, tail, D_ATT), lambda b: (b, n_cache // tail - 1, 0))
    const = lambda a: pl.BlockSpec(a.shape, lambda b: (0, 0))
    vmem = 2 * 2 * (n_grid * half + tail) * D_ATT * 4 / 2**20 + 8
    return pl.pallas_call(
        _attn_sample_body,
        out_shape=jax.ShapeDtypeStruct((n_seq * n_new, D_ATT), F32),
        grid=(n_seq,),
        in_specs=[new(0), new(1), new(2), grid_spec, tail_spec, grid_spec, tail_spec,
                  const(cg), const(ct), const(cn)],
        out_specs=pl.BlockSpec((n_new, D_ATT), lambda b: (b, 0)),
        compiler_params=_cparams(("parallel",), vmem),
        name="attn_sample",
    )(proj, proj, proj, kgv, win_k, vgv, win_v, jnp.asarray(cg), jnp.asarray(ct), jnp.asarray(cn))


def _gelu_tanh(x):
    return 0.5 * x * (1.0 + jnp.tanh(math.sqrt(2.0 / math.pi) * (x + 0.044715 * (x * x * x))))


def _ssm_body(u_ref, bb_ref, cst_ref, a_ref, ap_ref, d_ref, hre_ref, him_ref, *rest,
              tl, npar, nseg, emit_y, exact_in):
    if emit_y:
        y_ref, fre_ref, fim_ref, x_s, h_s = rest
    else:
        fre_ref, fim_ref, x_s, h_s = rest
    ns = SSM_STATES_PER_TILE
    c = pl.program_id(1)

    @pl.when(c == 0)
    def _init():
        if nseg == 1:
            h_s[0] = hre_ref[...]
            h_s[1] = him_ref[...]
        else:
            pr, pi = ap_ref[0:1, :], ap_ref[1:2, :]
            for b in range(npar // nseg):
                sr = jnp.zeros((1, ns), F32)
                si = jnp.zeros((1, ns), F32)
                for j in range(nseg):
                    row = b * nseg + j
                    h_s[0, row:row + 1, :] = sr
                    h_s[1, row:row + 1, :] = si
                    er, ei = hre_ref[row:row + 1, :], him_ref[row:row + 1, :]
                    sr, si = pr * sr - pi * si + er, pr * si + pi * sr + ei

    u = u_ref[...].reshape(tl * npar, SSM_LANE_TILE)
    if exact_in:
        x_s[...] = jnp.dot(u, bb_ref[...], precision=lax.Precision.HIGHEST,
                           preferred_element_type=F32)
    else:
        x_s[...] = _dot(u.astype(BF16), bb_ref[...])

    ar = jnp.broadcast_to(a_ref[0:1, :], (SUBLANES, ns))
    ai = jnp.broadcast_to(a_ref[1:2, :], (SUBLANES, ns))
    ngrp = npar // SUBLANES

    def step(i, carry):
        out = []
        for g in range(ngrp):
            hr, hi = carry[2 * g], carry[2 * g + 1]
            rows = pl.ds(pl.multiple_of(i * npar + g * SUBLANES, SUBLANES), SUBLANES)
            nr = ar * hr - ai * hi + x_s[rows, 0:ns]
            ni = ar * hi + ai * hr + x_s[rows, ns:2 * ns]
            if emit_y:
                x_s[rows, 0:ns] = nr
                x_s[rows, ns:2 * ns] = ni
            out += [nr, ni]
        return tuple(out)

    init = []
    for g in range(ngrp):
        gs = slice(g * SUBLANES, (g + 1) * SUBLANES)
        init += [h_s[0, gs, :], h_s[1, gs, :]]
    fin = lax.fori_loop(0, tl, step, tuple(init))
    for g in range(ngrp):
        gs = slice(g * SUBLANES, (g + 1) * SUBLANES)
        h_s[0, gs, :] = fin[2 * g]
        h_s[1, gs, :] = fin[2 * g + 1]

    if emit_y:
        y = _dot(x_s[...].astype(BF16), cst_ref[...]) + d_ref[...] * u
        y_ref[...] = _gelu_tanh(y).reshape(tl, npar, SSM_LANE_TILE)

    @pl.when(c == pl.num_programs(1) - 1)
    def _fin():
        fre_ref[...] = h_s[0]
        fim_ref[...] = h_s[1]


def _ssm_scan(u3, prm, hin_re, hin_im, *, tl, nseg, emit_y, exact_in):
    n_steps, npar, d_ssm = u3.shape
    ns = SSM_STATES_PER_TILE
    nk = d_ssm // SSM_LANE_TILE
    bb = prm["bb_f32"] if exact_in else prm["bb_bf16"]
    in_specs = [
        pl.BlockSpec((tl, npar, SSM_LANE_TILE), lambda k, c: (c, 0, k)),
        pl.BlockSpec((None, SSM_LANE_TILE, 2 * ns), lambda k, c: (k, 0, 0)),
        pl.BlockSpec((None, 2 * ns, SSM_LANE_TILE), lambda k, c: (k, 0, 0)),
        pl.BlockSpec((None, 2, ns), lambda k, c: (k, 0, 0)),
        pl.BlockSpec((None, 2, ns), lambda k, c: (k, 0, 0)),
        pl.BlockSpec((1, SSM_LANE_TILE), lambda k, c: (0, k)),
        pl.BlockSpec((npar, ns), lambda k, c: (0, k)),
        pl.BlockSpec((npar, ns), lambda k, c: (0, k)),
    ]
    state_shape = jax.ShapeDtypeStruct((npar, nk * ns), F32)
    state_spec = pl.BlockSpec((npar, ns), lambda k, c: (0, k))
    out_shape = [state_shape, state_shape]
    out_specs = [state_spec, state_spec]
    if emit_y:
        out_shape = [jax.ShapeDtypeStruct(u3.shape, F32)] + out_shape
        out_specs = [pl.BlockSpec((tl, npar, SSM_LANE_TILE), lambda k, c: (c, 0, k))] + out_specs
    return pl.pallas_call(
        functools.partial(_ssm_body, tl=tl, npar=npar, nseg=nseg, emit_y=emit_y,
                          exact_in=exact_in),
        out_shape=out_shape,
        grid=(nk, n_steps // tl),
        in_specs=in_specs,
        out_specs=out_specs,
        scratch_shapes=[pltpu.VMEM((tl * npar, 2 * ns), F32), pltpu.VMEM((2, npar, ns), F32)],
        compiler_params=_cparams(("parallel", "arbitrary"), 32),
        name="ssm_scan_y" if emit_y else "ssm_scan_state",
    )(u3, bb, prm["cst"], prm["a"], prm["apow"], prm["d"], hin_re, hin_im)


def _ssm_params(lam_re, lam_im, log_dt, b_re, b_im, c_re, c_im, d_skip, seg_len):
    g, p, c = N_SSM_GROUPS, SSM_STATE, SSM_GROUP_CH
    nk, gt = g // SSM_GROUPS_PER_TILE, SSM_GROUPS_PER_TILE
    dt = jnp.exp(log_dt.astype(F32))[:, None]
    lr, li = lam_re.astype(F32), lam_im.astype(F32)
    mag = jnp.exp(lr * dt)
    a_re, a_im = mag * jnp.cos(li * dt), mag * jnp.sin(li * dt)
    magp = jnp.exp(lr * dt * seg_len)
    p_re, p_im = magp * jnp.cos(li * dt * seg_len), magp * jnp.sin(li * dt * seg_len)
    den = lr * lr + li * li
    nr, ni = a_re - 1.0, a_im
    f_re, f_im = (nr * lr + ni * li) / den, (ni * lr - nr * li) / den
    br, bi = b_re.astype(F32), b_im.astype(F32)
    bb_re = f_re[..., None] * br - f_im[..., None] * bi
    bb_im = f_re[..., None] * bi + f_im[..., None] * br
    eye = jnp.eye(gt, dtype=F32)

    def pack_b(m):
        return jnp.einsum("kgpc,gh->kgchp", m.reshape(nk, gt, p, c), eye).reshape(nk, gt * c, gt * p)

    def pack_c(m):
        return jnp.einsum("kgcp,gh->kgphc", m.reshape(nk, gt, c, p), eye).reshape(nk, gt * p, gt * c)

    bb = jnp.concatenate([pack_b(bb_re), pack_b(bb_im)], axis=2)
    cst = jnp.concatenate([pack_c(c_re.astype(F32)), -pack_c(c_im.astype(F32))], axis=1)
    tile = lambda v: v.reshape(nk, 1, gt * p)
    return {
        "bb_f32": bb, "bb_bf16": bb.astype(BF16), "cst": cst.astype(BF16),
        "a": jnp.concatenate([tile(a_re), tile(a_im)], axis=1),
        "apow": jnp.concatenate([tile(p_re), tile(p_im)], axis=1),
        "d": d_skip.astype(F32).reshape(1, g * c),
    }


def _glu_body(y_ref, w_ref, o_ref):
    yg = y_ref[...]
    z = _dot(yg.astype(BF16), w_ref[...])
    o_ref[...] = yg * (1.0 / (1.0 + jnp.exp(-z)))


def _glu(yg, w, *, tm):
    m, n = yg.shape
    return pl.pallas_call(
        _glu_body,
        out_shape=jax.ShapeDtypeStruct((m, n), F32),
        grid=(m // tm,),
        in_specs=[pl.BlockSpec((tm, n), lambda i: (i, 0)), pl.BlockSpec((n, n), lambda i: (0, 0))],
        out_specs=pl.BlockSpec((tm, n), lambda i: (i, 0)),
        compiler_params=_cparams(("parallel",), 4 * tm * n * 4 / 2**20 + 12),
        name="ssm_glu",
    )(yg, w)


def _mix_body(attn_ref, ssm_ref, ga_ref, gs_ref, w_ref, x_ref, g_ref, b_ref, o_ref):
    a = _rmsnorm(attn_ref[...], ga_ref[...]).astype(BF16)
    s = _rmsnorm(ssm_ref[...], gs_ref[...]).astype(BF16)
    mix = _dot(a, w_ref[0:D_ATT, :]) + _dot(s, w_ref[D_ATT:D_ATT + D_SSM, :])
    o_ref[...] = _layernorm(DEEPNORM_ALPHA * x_ref[...] + mix, g_ref[...], b_ref[...])


def _mix(attn, ssm, ga, gs, w, x, g, b, *, tm):
    m = x.shape[0]
    row = lambda n: pl.BlockSpec((tm, n), lambda i: (i, 0))
    const = lambda a: pl.BlockSpec(a.shape, lambda i: (0, 0))
    return pl.pallas_call(
        _mix_body,
        out_shape=jax.ShapeDtypeStruct((m, D_MODEL), F32),
        grid=(m // tm,),
        in_specs=[row(D_ATT), row(D_SSM), const(ga), const(gs), const(w), row(D_MODEL),
                  const(g), const(b)],
        out_specs=row(D_MODEL),
        compiler_params=_cparams(("parallel",), 6 * tm * D_MODEL * 4 / 2**20 + 24),
        name="mix_out_ln1",
    )(attn, ssm, ga, gs, w, x, g, b)


def _mm_ln_body(a_ref, w_ref, x_ref, g_ref, b_ref, o_ref):
    y = _dot(a_ref[...].astype(BF16), w_ref[...])
    o_ref[...] = _layernorm(DEEPNORM_ALPHA * x_ref[...] + y, g_ref[...], b_ref[...])


def _mm_ln(a, w, x, g, b, *, tm, name):
    m = x.shape[0]
    row = lambda n: pl.BlockSpec((tm, n), lambda i: (i, 0))
    const = lambda v: pl.BlockSpec(v.shape, lambda i: (0, 0))
    return pl.pallas_call(
        _mm_ln_body,
        out_shape=jax.ShapeDtypeStruct((m, D_MODEL), F32),
        grid=(m // tm,),
        in_specs=[row(a.shape[1]), const(w), row(D_MODEL), const(g), const(b)],
        out_specs=row(D_MODEL),
        compiler_params=_cparams(("parallel",), 6 * tm * D_MODEL * 4 / 2**20 + 24),
        name=name,
    )(a, w, x, g, b)


def _memattn_body(q_ref, k_ref, v_ref, o_ref):
    scale = MEM_HD ** -0.5
    for h in range(MEM_HEADS):
        sl = slice(h * MEM_HD, (h + 1) * MEM_HD)
        s = _dot_nt(q_ref[:, sl].astype(BF16), k_ref[:, sl].astype(BF16)) * scale
        m = jnp.max(s, axis=-1, keepdims=True)
        p = jnp.exp(s - m)
        l = jnp.sum(p, axis=-1, keepdims=True)
        o_ref[:, sl] = _dot(p.astype(BF16), v_ref[:, sl].astype(BF16)) / l


def _memattn(q, mem_k, mem_v, *, row0, n_seq, seq, tq, name):
    assert seq % tq == 0 and row0 % tq == 0
    nq = seq // tq
    rb = row0 // tq
    mem_spec = pl.BlockSpec((None, N_MEM, D_MODEL), lambda b, i: (b, 0, 0))
    return pl.pallas_call(
        _memattn_body,
        out_shape=jax.ShapeDtypeStruct((n_seq * seq, D_MODEL), F32),
        grid=(n_seq, nq),
        in_specs=[pl.BlockSpec((tq, D_MODEL), lambda b, i: (rb + b * nq + i, 0)),
                  mem_spec, mem_spec],
        out_specs=pl.BlockSpec((tq, D_MODEL), lambda b, i: (b * nq + i, 0)),
        compiler_params=_cparams(("parallel", "parallel"),
                                 4 * (tq + N_MEM) * D_MODEL * 4 / 2**20 + 8),
        name=name,
    )(q, mem_k, mem_v)


def _router_body(x_ref, w_ref, b_ref, sel_ref, wts_ref, cnt_ref, run_s, *, tm):
    i = pl.program_id(0)

    @pl.when(i == 0)
    def _():
        run_s[...] = jnp.zeros_like(run_s)

    ng, epg = N_EXPERT_GROUPS, EXPERTS_PER_GROUP
    logits = jnp.dot(x_ref[...], w_ref[...], precision=lax.Precision.HIGHEST,
                     preferred_element_type=F32) + b_ref[...]
    lane = lax.broadcasted_iota(I32, (tm, ROUTER_LANES), 1)
    big = ROUTER_LANES

    def first_argmax(vals):
        mx = jnp.max(vals, axis=-1, keepdims=True)
        idx = jnp.min(jnp.where(vals == mx, lane, big), axis=-1, keepdims=True)
        return mx, idx

    gl = jnp.where(lane < ng, logits, NEG_INF)
    gmax, gsel = first_argmax(gl)
    g_w = 1.0 / jnp.sum(jnp.exp(gl - gmax), axis=-1, keepdims=True)
    lo = ng + gsel * epg
    el = jnp.where(jnp.logical_and(lane >= lo, lane < lo + epg), logits, NEG_INF)
    v1, i1 = first_argmax(el)
    v2, i2 = first_argmax(jnp.where(lane == i1, NEG_INF, el))
    e21 = jnp.exp(v2 - v1)
    w1 = g_w / (1.0 + e21)
    w2 = g_w * e21 / (1.0 + e21)

    onehot = jnp.logical_or(lane == i1, lane == i2)
    r = lax.broadcasted_iota(I32, (tm, tm), 0)
    cc = lax.broadcasted_iota(I32, (tm, tm), 1)
    tri = (cc < r).astype(BF16)
    before = _dot(tri, onehot.astype(BF16)) + run_s[...]
    rank1 = jnp.sum(jnp.where(lane == i1, before, 0.0), axis=-1, keepdims=True).astype(I32)
    rank2 = jnp.sum(jnp.where(lane == i2, before, 0.0), axis=-1, keepdims=True).astype(I32)
    run_s[...] = run_s[...] + jnp.sum(onehot.astype(F32), axis=0, keepdims=True)

    sel = jnp.where(lane == 0, i1 - ng, jnp.where(lane == 1, i2 - ng,
                    jnp.where(lane == 2, rank1, jnp.where(lane == 3, rank2, 0))))
    sel_ref[...] = sel
    wts_ref[...] = jnp.where(lane == 0, w1, jnp.where(lane == 1, w2, 0.0))
    cnt_ref[...] = run_s[...].astype(I32)


def _router(x, w, b, *, tm):
    m = x.shape[0]
    row = pl.BlockSpec((tm, ROUTER_LANES), lambda i: (i, 0))
    return pl.pallas_call(
        functools.partial(_router_body, tm=tm),
        out_shape=[jax.ShapeDtypeStruct((m, ROUTER_LANES), I32),
                   jax.ShapeDtypeStruct((m, ROUTER_LANES), F32),
                   jax.ShapeDtypeStruct((1, ROUTER_LANES), I32)],
        grid=(m // tm,),
        in_specs=[pl.BlockSpec((tm, D_MODEL), lambda i: (i, 0)),
                  pl.BlockSpec((D_MODEL, ROUTER_LANES), lambda i: (0, 0)),
                  pl.BlockSpec((1, ROUTER_LANES), lambda i: (0, 0))],
        out_specs=[row, row, pl.BlockSpec((1, ROUTER_LANES), lambda i: (0, 0))],
        scratch_shapes=[pltpu.VMEM((1, ROUTER_LANES), F32)],
        compiler_params=_cparams(("arbitrary",), 16),
        name="moe_router",
    )(x, w, b)


def _moe_body(te_ref, src_ref, nact_ref, x_hbm, wg_ref, wu_ref, wd_ref, o_ref,
              xbuf, sem, wg_s, wu_s, wd_s):
    i = pl.program_id(0)
    nact = nact_ref[0]
    tm = MOE_TILE

    def gather_copy(tile, slot, r):
        idx = src_ref[tile * tm + r]
        return pltpu.make_async_copy(x_hbm.at[pl.ds(idx, 1)], xbuf.at[slot, pl.ds(r, 1)],
                                     sem.at[slot])

    def start_gather(tile, slot):
        def body(r, carry):
            gather_copy(tile, slot, r).start()
            return carry
        lax.fori_loop(0, tm, body, 0)

    @pl.when(i == 0)
    def _():
        start_gather(0, 0)

    @pl.when(i + 1 < nact)
    def _():
        start_gather(i + 1, (i + 1) % 2)

    @pl.when(i < nact)
    def _():
        slot = i % 2
        pltpu.make_async_copy(x_hbm.at[pl.ds(0, tm)], xbuf.at[slot], sem.at[slot]).wait()
        prev = te_ref[jnp.maximum(i - 1, 0)]

        @pl.when(jnp.logical_or(i == 0, te_ref[i] != prev))
        def _():
            wg_s[...] = wg_ref[...].astype(BF16)
            wu_s[...] = wu_ref[...].astype(BF16)
            wd_s[...] = wd_ref[...].astype(BF16)

        x = xbuf[slot].astype(BF16)
        hg = _dot(x, wg_s[...])
        hu = _dot(x, wu_s[...])
        h = hg * (1.0 / (1.0 + jnp.exp(-hg))) * hu
        o_ref[...] = _dot(h.astype(BF16), wd_s[...])

    @pl.when(i >= nact)
    def _():
        o_ref[...] = jnp.zeros_like(o_ref)


def _moe_experts(x, w_gate, w_up, w_down, tile_expert, src, nact, *, n_tiles):
    tm = MOE_TILE
    grid_spec = pltpu.PrefetchScalarGridSpec(
        num_scalar_prefetch=3,
        grid=(n_tiles,),
        in_specs=[pl.BlockSpec(memory_space=pl.ANY),
                  pl.BlockSpec((None, D_MODEL, D_EXPERT), lambda i, te, s, n: (te[i], 0, 0)),
                  pl.BlockSpec((None, D_MODEL, D_EXPERT), lambda i, te, s, n: (te[i], 0, 0)),
                  pl.BlockSpec((None, D_EXPERT, D_MODEL), lambda i, te, s, n: (te[i], 0, 0))],
        out_specs=pl.BlockSpec((tm, D_MODEL), lambda i, te, s, n: (i, 0)),
        scratch_shapes=[pltpu.VMEM((2, tm, D_MODEL), F32),
                        pltpu.SemaphoreType.DMA((2,)),
                        pltpu.VMEM((D_MODEL, D_EXPERT), BF16),
                        pltpu.VMEM((D_MODEL, D_EXPERT), BF16),
                        pltpu.VMEM((D_EXPERT, D_MODEL), BF16)],
    )
    return pl.pallas_call(
        _moe_body,
        out_shape=jax.ShapeDtypeStruct((n_tiles * tm, D_MODEL), F32),
        grid_spec=grid_spec,
        compiler_params=_cparams(("arbitrary",), 48),
        name="moe_experts",
    )(tile_expert, src, nact, x, w_gate, w_up, w_down)


def _combine_body(pos_ref, ys_hbm, wts_ref, x_ref, g_ref, b_ref, o_ref, buf, sem, *, tc):
    i = pl.program_id(0)
    n = pl.num_programs(0)

    def start_gather(tile, slot):
        def body(r, carry):
            for k in range(2):
                idx = pos_ref[(tile * tc + r) * 2 + k]
                pltpu.make_async_copy(ys_hbm.at[pl.ds(idx, 1)], buf.at[slot, k, pl.ds(r, 1)],
                                      sem.at[slot]).start()
            return carry
        lax.fori_loop(0, tc, body, 0)

    @pl.when(i == 0)
    def _():
        start_gather(0, 0)

    @pl.when(i + 1 < n)
    def _():
        start_gather(i + 1, (i + 1) % 2)

    slot = i % 2
    for k in range(2):
        pltpu.make_async_copy(ys_hbm.at[pl.ds(0, tc)], buf.at[slot, k], sem.at[slot]).wait()
    w = wts_ref[...]
    moe = w[:, 0:1] * buf[slot, 0] + w[:, 1:2] * buf[slot, 1]
    o_ref[...] = _layernorm(DEEPNORM_ALPHA * x_ref[...] + moe, g_ref[...], b_ref[...])


def _moe_combine(ys, pos, wts, x, g, b, *, tc):
    m = x.shape[0]
    grid_spec = pltpu.PrefetchScalarGridSpec(
        num_scalar_prefetch=1,
        grid=(m // tc,),
        in_specs=[pl.BlockSpec(memory_space=pl.ANY),
                  pl.BlockSpec((tc, ROUTER_LANES), lambda i, p: (i, 0)),
                  pl.BlockSpec((tc, D_MODEL), lambda i, p: (i, 0)),
                  pl.BlockSpec((1, D_MODEL), lambda i, p: (0, 0)),
                  pl.BlockSpec((1, D_MODEL), lambda i, p: (0, 0))],
        out_specs=pl.BlockSpec((tc, D_MODEL), lambda i, p: (i, 0)),
        scratch_shapes=[pltpu.VMEM((2, 2, tc, D_MODEL), F32), pltpu.SemaphoreType.DMA((2,))],
    )
    return pl.pallas_call(
        functools.partial(_combine_body, tc=tc),
        out_shape=jax.ShapeDtypeStruct((m, D_MODEL), F32),
        grid_spec=grid_spec,
        compiler_params=_cparams(("arbitrary",), 8 * tc * D_MODEL * 4 / 2**20 + 8),
        name="moe_combine_ln3",
    )(pos, ys, wts, x, g, b)


def _moe(x, w_r1, b_r1, w_r2, b_r2, w_gate, w_up, w_down, g, b, *, tm_router, tc):
    n = x.shape[0]
    ng, ne = N_EXPERT_GROUPS, N_EXPERTS
    pad = ROUTER_LANES - ng - ne
    w_r = jnp.concatenate([w_r1, w_r2.reshape(D_MODEL, ne), jnp.zeros((D_MODEL, pad), F32)], axis=1)
    b_r = jnp.concatenate([b_r1, b_r2.reshape(ne), jnp.zeros((pad,), F32)]).reshape(1, ROUTER_LANES)
    sel, wts, cnt = _router(x, w_r, b_r, tm=tm_router)

    tm = MOE_TILE
    n_tiles = (2 * n) // tm + ne
    counts = cnt[0, ng:ng + ne]
    tiles_per = (counts + tm - 1) // tm
    tile_end = jnp.cumsum(tiles_per)
    row_off = (tile_end - tiles_per) * tm
    nact = tile_end[-1]
    ids, ranks = sel[:, 0:2], sel[:, 2:4]
    pos = row_off[ids] + ranks
    tile_ids = jnp.minimum(jnp.arange(n_tiles, dtype=I32), nact - 1)
    tile_expert = jnp.searchsorted(tile_end, tile_ids, side="right").astype(I32)
    token = jnp.broadcast_to(jnp.arange(n, dtype=I32)[:, None], (n, 2))
    src = jnp.zeros((n_tiles * tm,), I32).at[pos.reshape(-1)].set(token.reshape(-1))

    ys = _moe_experts(x, w_gate, w_up, w_down, tile_expert, src, nact.reshape(1).astype(I32),
                      n_tiles=n_tiles)
    return _moe_combine(ys, pos.reshape(-1).astype(I32), wts, x, g, b, tc=tc)


def _row_tile(m, cap):
    best = SUBLANES
    for t in range(SUBLANES, cap + 1, SUBLANES):
        if m % t == 0:
            best = t
    return best


def kernel(x_prompt, x_sample, cache_win_k, cache_win_v, state_ssm_re, state_ssm_im, cache_mem_k, cache_mem_v, mem_prompt, w_in, ssm_lam_re, ssm_lam_im, ssm_log_dt, ssm_b_re, ssm_b_im, ssm_c_re, ssm_c_im, ssm_d, w_glu, g_attn, g_ssm, w_out, ln1_g, ln1_b, w_mq, w_mk, w_mv, w_mo, ln2_g, ln2_b, w_r1, b_r1, w_r2, b_r2, w_gate, w_up, w_down, ln3_g, ln3_b):
    nb, seq, d = x_prompt.shape
    ns, dseq, _ = x_sample.shape
    n_p, n_s = nb * seq, ns * dseq
    n = n_p + n_s
    l = 0
    row2 = lambda v: v[l].reshape(1, -1)

    x_all = jnp.concatenate([x_prompt.reshape(n_p, d), x_sample.reshape(n_s, d)], axis=0)
    tm_big = _row_tile(n, 768)
    tm_ln = _row_tile(n, 384)

    proj = _matmul(x_all, w_in[l].astype(BF16), tm=tm_big, tn=1024, name="proj_in")

    attn_p = _attn_prompt(proj, n_batch=nb, seq=seq)
    wk = cache_win_k[l].reshape(ns, -1, D_ATT)
    wv = cache_win_v[l].reshape(ns, -1, D_ATT)
    attn_s = _attn_sample(proj, wk, wv, row0=n_p, n_seq=ns, n_new=dseq)
    attn = jnp.concatenate([attn_p, attn_s], axis=0)

    seg_len = seq // SSM_SEGMENTS
    prm = _ssm_params(ssm_lam_re[l], ssm_lam_im[l], ssm_log_dt[l], ssm_b_re[l], ssm_b_im[l],
                      ssm_c_re[l], ssm_c_im[l], ssm_d[l], seg_len)
    u_p = proj[:n_p, 3 * D_ATT:].reshape(nb, SSM_SEGMENTS, seg_len, D_SSM)
    u_p = jnp.transpose(u_p, (2, 0, 1, 3)).reshape(seg_len, nb * SSM_SEGMENTS, D_SSM)
    zeros = jnp.zeros((nb * SSM_SEGMENTS, N_SSM_GROUPS * SSM_STATE), F32)
    tl = _row_tile(seg_len, 32)
    end_re, end_im = _ssm_scan(u_p, prm, zeros, zeros, tl=tl, nseg=1, emit_y=False, exact_in=False)
    yg_p, fin_re, fin_im = _ssm_scan(u_p, prm, end_re, end_im, tl=tl, nseg=SSM_SEGMENTS,
                                     emit_y=True, exact_in=False)
    yg_p = jnp.transpose(yg_p.reshape(seg_len, nb, SSM_SEGMENTS, D_SSM), (1, 2, 0, 3))
    last = SSM_SEGMENTS - 1
    ssm_re_p = fin_re.reshape(nb, SSM_SEGMENTS, N_SSM_GROUPS, SSM_STATE)[:, last]
    ssm_im_p = fin_im.reshape(nb, SSM_SEGMENTS, N_SSM_GROUPS, SSM_STATE)[:, last]

    u_s = jnp.transpose(proj[n_p:, 3 * D_ATT:].reshape(ns, dseq, D_SSM), (1, 0, 2))
    h0_re = state_ssm_re[l].reshape(ns, -1)
    h0_im = state_ssm_im[l].reshape(ns, -1)
    yg_s, ssm_re_s, ssm_im_s = _ssm_scan(u_s, prm, h0_re, h0_im, tl=dseq, nseg=1,
                                         emit_y=True, exact_in=True)
    yg_s = jnp.transpose(yg_s, (1, 0, 2))
    yg = jnp.concatenate([yg_p.reshape(n_p, D_SSM), yg_s.reshape(n_s, D_SSM)], axis=0)
    ssm_out = _glu(yg, w_glu[l].astype(BF16), tm=tm_big)

    x1 = _mix(attn, ssm_out, row2(g_attn), row2(g_ssm), w_out[l].astype(BF16), x_all,
              row2(ln1_g), row2(ln1_b), tm=tm_ln)

    mem_rows = mem_prompt.reshape(nb * N_MEM, d)
    mem_k = _matmul(mem_rows, w_mk[l].astype(BF16), tm=nb * N_MEM, tn=1024, name="mem_k")
    mem_v = _matmul(mem_rows, w_mv[l].astype(BF16), tm=nb * N_MEM, tn=1024, name="mem_v")
    q_mem = _matmul(x1, w_mq[l].astype(BF16), tm=tm_big, tn=1024, name="mem_q")
    o_p = _memattn(q_mem, mem_k.reshape(nb, N_MEM, d), mem_v.reshape(nb, N_MEM, d),
                   row0=0, n_seq=nb, seq=seq, tq=_row_tile(seq, 512), name="memattn_prompt")
    o_s = _memattn(q_mem, cache_mem_k[l].reshape(ns, N_MEM, d), cache_mem_v[l].reshape(ns, N_MEM, d),
                   row0=n_p, n_seq=ns, seq=dseq, tq=dseq, name="memattn_sample")
    o_mem = jnp.concatenate([o_p, o_s], axis=0)
    x2 = _mm_ln(o_mem, w_mo[l].astype(BF16), x1, row2(ln2_g), row2(ln2_b), tm=tm_ln,
                name="mem_out_ln2")

    y = _moe(x2, w_r1[l], b_r1[l], w_r2[l], b_r2[l], w_gate[l], w_up[l], w_down[l],
             row2(ln3_g), row2(ln3_b), tm_router=_row_tile(n, 256), tc=_row_tile(n, 128))

    y_p = y[:n_p].reshape(nb, seq, d)
    y_s = y[n_p:].reshape(ns, dseq, d)
    k_p = proj[:n_p, D_ATT:2 * D_ATT].reshape(nb, seq, ATT_HEADS, ATT_HD)
    v_p = proj[:n_p, 2 * D_ATT:3 * D_ATT].reshape(nb, seq, ATT_HEADS, ATT_HD)
    wp = min(max(w for w, _ in DILATIONS), seq)
    k_s = proj[n_p:, D_ATT:2 * D_ATT].reshape(ns, dseq, ATT_HEADS, ATT_HD)
    v_s = proj[n_p:, 2 * D_ATT:3 * D_ATT].reshape(ns, dseq, ATT_HEADS, ATT_HD)
    state = lambda v, b_: v.reshape(1, b_, N_SSM_GROUPS, SSM_STATE)
    return (y_p, y_s, k_p[None, :, seq - wp:], v_p[None, :, seq - wp:], k_s[None], v_s[None],
            state(ssm_re_p, nb), state(ssm_im_p, nb), state(ssm_re_s, ns), state(ssm_im_s, ns),
            mem_k.reshape(1, nb, N_MEM, MEM_HEADS, MEM_HD),
            mem_v.reshape(1, nb, N_MEM, MEM_HEADS, MEM_HD))
```

```python
import functools
import math

import numpy as np
import jax
import jax.numpy as jnp
from jax import lax
from jax.experimental import pallas as pl
from jax.experimental.pallas import tpu as pltpu

F32 = jnp.float32
BF16 = jnp.bfloat16
I32 = jnp.int32

D_MODEL = 2048
PAST_LEN = 8192
D_ATT = D_MODEL // 2
ATT_HEADS = 8
ATT_HD = D_ATT // ATT_HEADS
DILATIONS = ((128, 1), (512, 4), (2048, 16))
D_SSM = D_MODEL - D_ATT
SSM_GROUP_CH = 16
N_SSM_GROUPS = D_SSM // SSM_GROUP_CH
SSM_STATE = 64
N_MEM = 256
MEM_HEADS = 4
MEM_HD = D_MODEL // MEM_HEADS
N_EXPERT_GROUPS = 4
EXPERTS_PER_GROUP = 8
N_EXPERTS = N_EXPERT_GROUPS * EXPERTS_PER_GROUP
D_EXPERT = D_MODEL // 4
DEPTH = 1
DEEPNORM_ALPHA = (2.0 * DEPTH) ** 0.25
LN_EPS = 1e-5
RMS_EPS = 1e-6

LANES = 128
SUBLANES = 8
Q_BLOCK = 128
ATTN_GROUP = 8
SSM_LANE_TILE = 128
SSM_GROUPS_PER_TILE = SSM_LANE_TILE // SSM_GROUP_CH
SSM_STATES_PER_TILE = SSM_GROUPS_PER_TILE * SSM_STATE
SSM_SEGMENTS = 8
MOE_TILE = 256
ROUTER_LANES = 128
NEG_INF = float("-inf")


def _cparams(semantics, vmem_mib):
    return pltpu.CompilerParams(dimension_semantics=semantics,
                                vmem_limit_bytes=int(vmem_mib) << 20)


def _layernorm(y, g, b):
    mu = jnp.mean(y, axis=-1, keepdims=True)
    yc = y - mu
    var = jnp.mean(yc * yc, axis=-1, keepdims=True)
    return yc * lax.rsqrt(var + LN_EPS) * g + b


def _rmsnorm(v, g):
    return v * lax.rsqrt(jnp.mean(v * v, axis=-1, keepdims=True) + RMS_EPS) * g


def _dot(a, b):
    return jnp.dot(a, b, preferred_element_type=F32)


def _dot_nt(a, b):
    return lax.dot_general(a, b, (((1,), (1,)), ((), ())), preferred_element_type=F32)


def _mm_body(x_ref, w_ref, o_ref):
    o_ref[...] = _dot(x_ref[...].astype(BF16), w_ref[...]).astype(o_ref.dtype)


def _matmul(x, w, *, tm, tn, name):
    m, k = x.shape
    n = w.shape[1]
    vmem = 2 * (tm * k * x.dtype.itemsize + k * tn * 2 + tm * tn * 4) / 2**20 + 8
    return pl.pallas_call(
        _mm_body,
        out_shape=jax.ShapeDtypeStruct((m, n), F32),
        grid=(n // tn, m // tm),
        in_specs=[pl.BlockSpec((tm, k), lambda j, i: (i, 0)),
                  pl.BlockSpec((k, tn), lambda j, i: (0, j))],
        out_specs=pl.BlockSpec((tm, tn), lambda j, i: (i, j)),
        compiler_params=_cparams(("parallel", "parallel"), vmem),
        name=name,
    )(x, w)


def _attn_prompt_body(q_ref, k_ref, v_ref, o_ref, kt_s, va_s, on_s, lse_s, *, seq, dilations):
    scale = ATT_HD ** -0.5
    nblk = seq // Q_BLOCK
    qi = lax.broadcasted_iota(I32, (Q_BLOCK, Q_BLOCK), 0)
    kj = lax.broadcasted_iota(I32, (Q_BLOCK, Q_BLOCK), 1)
    cur_ok = kj <= qi
    prev_ok = kj >= qi
    va_s[:, :, ATT_HD:] = jnp.ones((nblk, Q_BLOCK, ATT_HD), BF16)

    for br, (_, d) in enumerate(dilations):
        span = d * Q_BLOCK
        nb = seq // span

        def stream_rows(t, d=d, span=span, nb=nb):
            r = t // nb
            ib = t % nb
            return r, ib, pl.ds(r + ib * span, Q_BLOCK, stride=d)

        def prep(g, carry, stream_rows=stream_rows):
            loaded = []
            for j in range(ATTN_GROUP):
                t = g * ATTN_GROUP + j
                _, _, rows = stream_rows(t)
                loaded.append((t, k_ref[rows, :], v_ref[rows, :]))
            for t, kk, vv in loaded:
                kt_s[t] = jnp.transpose(kk).astype(BF16)
                va_s[t, :, 0:ATT_HD] = vv.astype(BF16)
            return carry

        lax.fori_loop(0, nblk // ATTN_GROUP, prep, 0)

        def group(g, carry, br=br, nb=nb, stream_rows=stream_rows):
            scores = []
            for j in range(ATTN_GROUP):
                t = g * ATTN_GROUP + j
                r, ib, rows = stream_rows(t)
                tp = jnp.maximum(t - 1, r * nb)
                q = (q_ref[rows, :] * scale).astype(BF16)
                s = _dot(q, jnp.concatenate([kt_s[tp], kt_s[t]], axis=1))
                scores.append((t, tp, ib, rows, s))
            probs = []
            for t, tp, ib, rows, s in scores:
                ok = jnp.concatenate([jnp.logical_and(prev_ok, ib > 0), cur_ok], axis=1)
                s = jnp.where(ok, s, NEG_INF)
                m = jnp.max(s, axis=-1, keepdims=True)
                probs.append((t, tp, rows, m, jnp.exp(s - m).astype(BF16)))
            outs = [(rows, m, _dot(p, jnp.concatenate([va_s[tp], va_s[t]], axis=0)))
                    for t, tp, rows, m, p in probs]
            for rows, m, al in outs:
                l = al[:, ATT_HD:]
                on_s[br, rows, :] = al[:, :ATT_HD] / l
                lse_s[br, rows, :] = m + jnp.log(l)
            return carry

        lax.fori_loop(0, nblk // ATTN_GROUP, group, 0)

    chunk = 256
    nbr = len(dilations)

    def merge(c, carry):
        rows = pl.ds(pl.multiple_of(c * chunk, chunk), chunk)
        ls = [lse_s[b, rows, :] for b in range(nbr)]
        mx = functools.reduce(jnp.maximum, ls)
        es = [jnp.exp(li - mx) for li in ls]
        num = sum(es[b] * on_s[b, rows, :] for b in range(nbr))
        o_ref[rows, :] = num / sum(es)
        return carry

    lax.fori_loop(0, seq // chunk, merge, 0)


def _attn_prompt(proj, *, n_batch, seq, dilations=DILATIONS):
    for w, d in dilations:
        assert w // d == Q_BLOCK and seq % (d * Q_BLOCK) == 0
    nbr = len(dilations)
    nblk = seq // Q_BLOCK
    assert nblk % ATTN_GROUP == 0
    blk = lambda off: pl.BlockSpec((seq, ATT_HD), lambda b, h, off=off: (b, off + h))
    vmem = ((4 * 2 + 2 * nbr) * seq * ATT_HD * 4 + 3 * seq * ATT_HD * 2) / 2**20 + 8
    return pl.pallas_call(
        functools.partial(_attn_prompt_body, seq=seq, dilations=dilations),
        out_shape=jax.ShapeDtypeStruct((n_batch * seq, D_ATT), F32),
        grid=(n_batch, ATT_HEADS),
        in_specs=[blk(0), blk(ATT_HEADS), blk(2 * ATT_HEADS)],
        out_specs=pl.BlockSpec((seq, ATT_HD), lambda b, h: (b, h)),
        scratch_shapes=[pltpu.VMEM((nblk, ATT_HD, Q_BLOCK), BF16),
                        pltpu.VMEM((nblk, Q_BLOCK, 2 * ATT_HD), BF16),
                        pltpu.VMEM((nbr, seq, ATT_HD), F32),
                        pltpu.VMEM((nbr, seq, ATT_HD), F32)],
        compiler_params=_cparams(("parallel", "parallel"), vmem),
        name="attn_prompt",
    )(proj, proj, proj)


def _sample_key_multiplicity(n_new, n_cache, past_len, dilations):
    d_max = max(d for _, d in dilations)
    tail = max(w for w, d in dilations if d != d_max)
    assert past_len % d_max == 0 and n_cache % d_max == 0 and n_new <= d_max // 2
    assert tail % d_max == 0 and tail <= n_cache
    half = d_max // 2
    n_grid = (n_cache - tail) // d_max
    kv_start = past_len - n_cache
    grid_rows = (np.arange(n_grid)[:, None] * d_max + np.arange(half)[None, :]).reshape(-1)
    tail_rows = n_cache - tail + np.arange(tail)
    new_rows = n_cache + np.arange(n_new)
    qpos = past_len + np.arange(n_new)

    def mult(rows):
        kpos = kv_start + rows
        delta = qpos[:, None] - kpos[None, :]
        c = np.zeros(delta.shape, np.float32)
        for w, d in dilations:
            c += ((delta >= 0) & (delta <= w) & (delta % d == 0) & (kpos[None, :] >= kv_start))
        return c

    fetched = np.zeros(n_cache + n_new, bool)
    fetched[grid_rows] = True
    fetched[tail_rows] = True
    fetched[new_rows] = True
    assert not mult(np.nonzero(~fetched)[0]).any()
    return mult(grid_rows), mult(tail_rows), mult(new_rows), n_grid, tail, half, d_max


def _attn_sample_body(q_ref, kn_ref, vn_ref, kg_ref, kt_ref, vg_ref, vt_ref,
                      cg_ref, ct_ref, cn_ref, o_ref):
    scale = ATT_HD ** -0.5
    heads = lambda ref: jnp.concatenate(
        [ref[:, h * ATT_HD:(h + 1) * ATT_HD] for h in range(ATT_HEADS)], axis=0)
    q = (heads(q_ref) * scale).astype(BF16)
    kn = heads(kn_ref).astype(BF16)
    vn = heads(vn_ref).astype(BF16)
    flat = lambda ref: ref[...].reshape(-1, ATT_HD).astype(BF16)
    cg, ct, cn = cg_ref[...], ct_ref[...], cn_ref[...]
    sg = jnp.where(cg > 0, _dot_nt(q, flat(kg_ref)), NEG_INF)
    st = jnp.where(ct > 0, _dot_nt(q, flat(kt_ref)), NEG_INF)
    sn = jnp.where(cn > 0, _dot_nt(q, kn), NEG_INF)
    m = jnp.maximum(jnp.maximum(jnp.max(sg, axis=-1, keepdims=True),
                                jnp.max(st, axis=-1, keepdims=True)),
                    jnp.max(sn, axis=-1, keepdims=True))
    pg = cg * jnp.exp(sg - m)
    pt = ct * jnp.exp(st - m)
    pn = cn * jnp.exp(sn - m)
    l = (jnp.sum(pg, axis=-1, keepdims=True) + jnp.sum(pt, axis=-1, keepdims=True)
         + jnp.sum(pn, axis=-1, keepdims=True))
    acc = (_dot(pg.astype(BF16), flat(vg_ref)) + _dot(pt.astype(BF16), flat(vt_ref))
           + _dot(pn.astype(BF16), vn))
    out = acc / l
    n_new = q_ref.shape[0]
    for h in range(ATT_HEADS):
        o_ref[:, h * ATT_HD:(h + 1) * ATT_HD] = out[h * n_new:(h + 1) * n_new, :]


def _attn_sample(proj, win_k, win_v, *, row0, n_seq, n_new, past_len=PAST_LEN,
                 dilations=DILATIONS):
    n_cache = win_k.shape[1]
    cg, ct, cn, n_grid, tail, half, d_max = _sample_key_multiplicity(
        n_new, n_cache, past_len, dilations)
    assert row0 % n_new == 0 and n_new % SUBLANES == 0 and n_cache % tail == 0
    eye = np.eye(ATT_HEADS, dtype=np.float32)
    key_major = lambda c: np.einsum("tk,hg->htkg", c, eye).reshape(ATT_HEADS * n_new, -1)
    head_major = lambda c: np.einsum("tk,hg->htgk", c, eye).reshape(ATT_HEADS * n_new, -1)
    cg, ct, cn = key_major(cg), key_major(ct), head_major(cn)
    rb = row0 // n_new
    n_groups = n_cache // d_max
    kgv = win_k.reshape(n_seq, n_groups, d_max, ATT_HEADS, ATT_HD)
    vgv = win_v.reshape(n_seq, n_groups, d_max, ATT_HEADS, ATT_HD)
    ktv = win_k.reshape(n_seq, n_cache // tail, tail, ATT_HEADS, ATT_HD)
    vtv = win_v.reshape(n_seq, n_cache // tail, tail, ATT_HEADS, ATT_HD)
    new = lambda off: pl.BlockSpec((n_new, D_ATT), lambda b, off=off: (rb + b, off))
    grid_spec = pl.BlockSpec((None, n_grid, half, ATT_HEADS, ATT_HD), lambda b: (b, 0, 0, 0, 0))
    tail_spec = pl.BlockSpec((None, None, tail, ATT_HEADS, ATT_HD),
                             lambda b: (b, n_cache // tail - 1, 0, 0, 0))
    const = lambda a: pl.BlockSpec(a.shape, lambda b: (0, 0))
    vmem = (2 * 2 * (n_grid * half + tail) * D_ATT * 4 + 4 * cg.size * 4 * 3) / 2**20 + 12
    return pl.pallas_call(
        _attn_sample_body,
        out_shape=jax.ShapeDtypeStruct((n_seq * n_new, D_ATT), F32),
        grid=(n_seq,),
        in_specs=[new(0), new(1), new(2), grid_spec, tail_spec, grid_spec, tail_spec,
                  const(cg), const(ct), const(cn)],
        out_specs=pl.BlockSpec((n_new, D_ATT), lambda b: (b, 0)),
        compiler_params=_cparams(("parallel",), vmem),
        name="attn_sample",
    )(proj, proj, proj, kgv, ktv, vgv, vtv, jnp.asarray(cg), jnp.asarray(ct), jnp.asarray(cn))


def _gelu_tanh(x):
    return 0.5 * x * (1.0 + jnp.tanh(math.sqrt(2.0 / math.pi) * (x + 0.044715 * (x * x * x))))


def _ssm_body(u_ref, bb_ref, cst_ref, a_ref, ap_ref, d_ref, hre_ref, him_ref, *rest,
              tl, npar, nseg, emit_y, exact_in):
    if emit_y:
        y_ref, fre_ref, fim_ref, x_s, h_s = rest
    else:
        fre_ref, fim_ref, x_s, h_s = rest
    ns = SSM_STATES_PER_TILE
    c = pl.program_id(1)

    @pl.when(c == 0)
    def _init():
        if nseg == 1:
            h_s[0] = hre_ref[...]
            h_s[1] = him_ref[...]
        else:
            pr, pi = ap_ref[0:1, :], ap_ref[1:2, :]
            for b in range(npar // nseg):
                sr = jnp.zeros((1, ns), F32)
                si = jnp.zeros((1, ns), F32)
                for j in range(nseg):
                    row = b * nseg + j
                    h_s[0, row:row + 1, :] = sr
                    h_s[1, row:row + 1, :] = si
                    er, ei = hre_ref[row:row + 1, :], him_ref[row:row + 1, :]
                    sr, si = pr * sr - pi * si + er, pr * si + pi * sr + ei

    u = u_ref[...].reshape(tl * npar, SSM_LANE_TILE)
    if exact_in:
        x_s[...] = jnp.dot(u, bb_ref[...], precision=lax.Precision.HIGHEST,
                           preferred_element_type=F32)
    else:
        x_s[...] = _dot(u.astype(BF16), bb_ref[...])

    ar = jnp.broadcast_to(a_ref[0:1, :], (SUBLANES, ns))
    ai = jnp.broadcast_to(a_ref[1:2, :], (SUBLANES, ns))
    ngrp = npar // SUBLANES

    def step(i, carry):
        out = []
        for g in range(ngrp):
            hr, hi = carry[2 * g], carry[2 * g + 1]
            rows = pl.ds(pl.multiple_of(i * npar + g * SUBLANES, SUBLANES), SUBLANES)
            nr = ar * hr - ai * hi + x_s[rows, 0:ns]
            ni = ar * hi + ai * hr + x_s[rows, ns:2 * ns]
            if emit_y:
                x_s[rows, 0:ns] = nr
                x_s[rows, ns:2 * ns] = ni
            out += [nr, ni]
        return tuple(out)

    init = []
    for g in range(ngrp):
        gs = slice(g * SUBLANES, (g + 1) * SUBLANES)
        init += [h_s[0, gs, :], h_s[1, gs, :]]
    fin = lax.fori_loop(0, tl, step, tuple(init))
    for g in range(ngrp):
        gs = slice(g * SUBLANES, (g + 1) * SUBLANES)
        h_s[0, gs, :] = fin[2 * g]
        h_s[1, gs, :] = fin[2 * g + 1]

    if emit_y:
        y = _dot(x_s[...].astype(BF16), cst_ref[...]) + d_ref[...] * u
        y_ref[...] = _gelu_tanh(y).reshape(tl, npar, SSM_LANE_TILE)

    @pl.when(c == pl.num_programs(1) - 1)
    def _fin():
        fre_ref[...] = h_s[0]
        fim_ref[...] = h_s[1]


def _ssm_scan(u3, prm, hin_re, hin_im, *, tl, nseg, emit_y, exact_in):
    n_steps, npar, d_ssm = u3.shape
    ns = SSM_STATES_PER_TILE
    nk = d_ssm // SSM_LANE_TILE
    bb = prm["bb_f32"] if exact_in else prm["bb_bf16"]
    in_specs = [
        pl.BlockSpec((tl, npar, SSM_LANE_TILE), lambda k, c: (c, 0, k)),
        pl.BlockSpec((None, SSM_LANE_TILE, 2 * ns), lambda k, c: (k, 0, 0)),
        pl.BlockSpec((None, 2 * ns, SSM_LANE_TILE), lambda k, c: (k, 0, 0)),
        pl.BlockSpec((None, 2, ns), lambda k, c: (k, 0, 0)),
        pl.BlockSpec((None, 2, ns), lambda k, c: (k, 0, 0)),
        pl.BlockSpec((1, SSM_LANE_TILE), lambda k, c: (0, k)),
        pl.BlockSpec((npar, ns), lambda k, c: (0, k)),
        pl.BlockSpec((npar, ns), lambda k, c: (0, k)),
    ]
    state_shape = jax.ShapeDtypeStruct((npar, nk * ns), F32)
    state_spec = pl.BlockSpec((npar, ns), lambda k, c: (0, k))
    out_shape = [state_shape, state_shape]
    out_specs = [state_spec, state_spec]
    if emit_y:
        out_shape = [jax.ShapeDtypeStruct(u3.shape, F32)] + out_shape
        out_specs = [pl.BlockSpec((tl, npar, SSM_LANE_TILE), lambda k, c: (c, 0, k))] + out_specs
    return pl.pallas_call(
        functools.partial(_ssm_body, tl=tl, npar=npar, nseg=nseg, emit_y=emit_y,
                          exact_in=exact_in),
        out_shape=out_shape,
        grid=(nk, n_steps // tl),
        in_specs=in_specs,
        out_specs=out_specs,
        scratch_shapes=[pltpu.VMEM((tl * npar, 2 * ns), F32), pltpu.VMEM((2, npar, ns), F32)],
        compiler_params=_cparams(("parallel", "arbitrary"), 32),
        name="ssm_scan_y" if emit_y else "ssm_scan_state",
    )(u3, bb, prm["cst"], prm["a"], prm["apow"], prm["d"], hin_re, hin_im)


def _ssm_params(lam_re, lam_im, log_dt, b_re, b_im, c_re, c_im, d_skip, seg_len):
    g, p, c = N_SSM_GROUPS, SSM_STATE, SSM_GROUP_CH
    nk, gt = g // SSM_GROUPS_PER_TILE, SSM_GROUPS_PER_TILE
    dt = jnp.exp(log_dt.astype(F32))[:, None]
    lr, li = lam_re.astype(F32), lam_im.astype(F32)
    mag = jnp.exp(lr * dt)
    a_re, a_im = mag * jnp.cos(li * dt), mag * jnp.sin(li * dt)
    magp = jnp.exp(lr * dt * seg_len)
    p_re, p_im = magp * jnp.cos(li * dt * seg_len), magp * jnp.sin(li * dt * seg_len)
    den = lr * lr + li * li
    nr, ni = a_re - 1.0, a_im
    f_re, f_im = (nr * lr + ni * li) / den, (ni * lr - nr * li) / den
    br, bi = b_re.astype(F32), b_im.astype(F32)
    bb_re = f_re[..., None] * br - f_im[..., None] * bi
    bb_im = f_re[..., None] * bi + f_im[..., None] * br
    eye = jnp.eye(gt, dtype=F32)

    def pack_b(m):
        return jnp.einsum("kgpc,gh->kgchp", m.reshape(nk, gt, p, c), eye).reshape(nk, gt * c, gt * p)

    def pack_c(m):
        return jnp.einsum("kgcp,gh->kgphc", m.reshape(nk, gt, c, p), eye).reshape(nk, gt * p, gt * c)

    bb = jnp.concatenate([pack_b(bb_re), pack_b(bb_im)], axis=2)
    cst = jnp.concatenate([pack_c(c_re.astype(F32)), -pack_c(c_im.astype(F32))], axis=1)
    tile = lambda v: v.reshape(nk, 1, gt * p)
    return {
        "bb_f32": bb, "bb_bf16": bb.astype(BF16), "cst": cst.astype(BF16),
        "a": jnp.concatenate([tile(a_re), tile(a_im)], axis=1),
        "apow": jnp.concatenate([tile(p_re), tile(p_im)], axis=1),
        "d": d_skip.astype(F32).reshape(1, g * c),
    }


def _glu_body(y_ref, w_ref, o_ref):
    yg = y_ref[...]
    z = _dot(yg.astype(BF16), w_ref[...])
    o_ref[...] = yg * (1.0 / (1.0 + jnp.exp(-z)))


def _glu(yg, w, *, tm):
    m, n = yg.shape
    return pl.pallas_call(
        _glu_body,
        out_shape=jax.ShapeDtypeStruct((m, n), F32),
        grid=(m // tm,),
        in_specs=[pl.BlockSpec((tm, n), lambda i: (i, 0)), pl.BlockSpec((n, n), lambda i: (0, 0))],
        out_specs=pl.BlockSpec((tm, n), lambda i: (i, 0)),
        compiler_params=_cparams(("parallel",), 4 * tm * n * 4 / 2**20 + 12),
        name="ssm_glu",
    )(yg, w)


def _mix_body(attn_ref, ssm_ref, ga_ref, gs_ref, w_ref, x_ref, g_ref, b_ref, o_ref):
    a = _rmsnorm(attn_ref[...], ga_ref[...]).astype(BF16)
    s = _rmsnorm(ssm_ref[...], gs_ref[...]).astype(BF16)
    mix = _dot(a, w_ref[0:D_ATT, :]) + _dot(s, w_ref[D_ATT:D_ATT + D_SSM, :])
    o_ref[...] = _layernorm(DEEPNORM_ALPHA * x_ref[...] + mix, g_ref[...], b_ref[...])


def _mix(attn, ssm, ga, gs, w, x, g, b, *, tm):
    m = x.shape[0]
    row = lambda n: pl.BlockSpec((tm, n), lambda i: (i, 0))
    const = lambda a: pl.BlockSpec(a.shape, lambda i: (0, 0))
    return pl.pallas_call(
        _mix_body,
        out_shape=jax.ShapeDtypeStruct((m, D_MODEL), F32),
        grid=(m // tm,),
        in_specs=[row(D_ATT), row(D_SSM), const(ga), const(gs), const(w), row(D_MODEL),
                  const(g), const(b)],
        out_specs=row(D_MODEL),
        compiler_params=_cparams(("parallel",), 6 * tm * D_MODEL * 4 / 2**20 + 24),
        name="mix_out_ln1",
    )(attn, ssm, ga, gs, w, x, g, b)


def _mm_ln_body(a_ref, w_ref, x_ref, g_ref, b_ref, o_ref):
    y = _dot(a_ref[...].astype(BF16), w_ref[...])
    o_ref[...] = _layernorm(DEEPNORM_ALPHA * x_ref[...] + y, g_ref[...], b_ref[...])


def _mm_ln(a, w, x, g, b, *, tm, name):
    m = x.shape[0]
    row = lambda n: pl.BlockSpec((tm, n), lambda i: (i, 0))
    const = lambda v: pl.BlockSpec(v.shape, lambda i: (0, 0))
    return pl.pallas_call(
        _mm_ln_body,
        out_shape=jax.ShapeDtypeStruct((m, D_MODEL), F32),
        grid=(m // tm,),
        in_specs=[row(a.shape[1]), const(w), row(D_MODEL), const(g), const(b)],
        out_specs=row(D_MODEL),
        compiler_params=_cparams(("parallel",), 6 * tm * D_MODEL * 4 / 2**20 + 24),
        name=name,
    )(a, w, x, g, b)


def _memattn_body(q_ref, k_ref, v_ref, o_ref):
    scale = MEM_HD ** -0.5
    for h in range(MEM_HEADS):
        sl = slice(h * MEM_HD, (h + 1) * MEM_HD)
        s = _dot_nt(q_ref[:, sl].astype(BF16), k_ref[:, sl].astype(BF16)) * scale
        m = jnp.max(s, axis=-1, keepdims=True)
        p = jnp.exp(s - m)
        l = jnp.sum(p, axis=-1, keepdims=True)
        o_ref[:, sl] = _dot(p.astype(BF16), v_ref[:, sl].astype(BF16)) / l


def _memattn(q, mem_k, mem_v, *, row0, n_seq, seq, tq, name):
    assert seq % tq == 0 and row0 % tq == 0
    nq = seq // tq
    rb = row0 // tq
    mem_spec = pl.BlockSpec((None, N_MEM, D_MODEL), lambda b, i: (b, 0, 0))
    return pl.pallas_call(
        _memattn_body,
        out_shape=jax.ShapeDtypeStruct((n_seq * seq, D_MODEL), F32),
        grid=(n_seq, nq),
        in_specs=[pl.BlockSpec((tq, D_MODEL), lambda b, i: (rb + b * nq + i, 0)),
                  mem_spec, mem_spec],
        out_specs=pl.BlockSpec((tq, D_MODEL), lambda b, i: (b * nq + i, 0)),
        compiler_params=_cparams(("parallel", "parallel"),
                                 4 * (tq + N_MEM) * D_MODEL * 4 / 2**20 + 8),
        name=name,
    )(q, mem_k, mem_v)


def _memattn_heads_body(q_ref, k_ref, v_ref, c_ref, o_ref):
    scale = MEM_HD ** -0.5
    tq = q_ref.shape[0]
    q = jnp.concatenate([q_ref[:, h * MEM_HD:(h + 1) * MEM_HD] for h in range(MEM_HEADS)], axis=0)
    k = k_ref[...].reshape(N_MEM * MEM_HEADS, MEM_HD).astype(BF16)
    v = v_ref[...].reshape(N_MEM * MEM_HEADS, MEM_HD).astype(BF16)
    s = jnp.where(c_ref[...] > 0, _dot_nt(q.astype(BF16), k) * scale, NEG_INF)
    m = jnp.max(s, axis=-1, keepdims=True)
    p = jnp.exp(s - m)
    l = jnp.sum(p, axis=-1, keepdims=True)
    o = _dot(p.astype(BF16), v) / l
    for h in range(MEM_HEADS):
        o_ref[:, h * MEM_HD:(h + 1) * MEM_HD] = o[h * tq:(h + 1) * tq, :]


def _memattn_heads(q, mem_k, mem_v, *, row0, n_seq, seq, name):
    assert row0 % seq == 0 and seq % SUBLANES == 0
    rb = row0 // seq
    same_head = np.kron(np.eye(MEM_HEADS, dtype=np.float32), np.ones((seq, 1), np.float32))
    same_head = np.tile(same_head, (1, N_MEM))
    mem_spec = pl.BlockSpec((None, N_MEM, MEM_HEADS, MEM_HD), lambda b: (b, 0, 0, 0))
    return pl.pallas_call(
        _memattn_heads_body,
        out_shape=jax.ShapeDtypeStruct((n_seq * seq, D_MODEL), F32),
        grid=(n_seq,),
        in_specs=[pl.BlockSpec((seq, D_MODEL), lambda b: (rb + b, 0)), mem_spec, mem_spec,
                  pl.BlockSpec(same_head.shape, lambda b: (0, 0))],
        out_specs=pl.BlockSpec((seq, D_MODEL), lambda b: (b, 0)),
        compiler_params=_cparams(("parallel",), 8 * N_MEM * D_MODEL * 4 / 2**20 + 8),
        name=name,
    )(q, mem_k, mem_v, jnp.asarray(same_head))


def _router_body(x_ref, w_ref, b_ref, sel_ref, wts_ref, cnt_ref, run_s, *, tm):
    i = pl.program_id(0)

    @pl.when(i == 0)
    def _():
        run_s[...] = jnp.zeros_like(run_s)

    ng, epg = N_EXPERT_GROUPS, EXPERTS_PER_GROUP
    logits = jnp.dot(x_ref[...], w_ref[...], precision=lax.Precision.HIGHEST,
                     preferred_element_type=F32) + b_ref[...]
    lane = lax.broadcasted_iota(I32, (tm, ROUTER_LANES), 1)
    big = ROUTER_LANES

    def first_argmax(vals):
        mx = jnp.max(vals, axis=-1, keepdims=True)
        idx = jnp.min(jnp.where(vals == mx, lane, big), axis=-1, keepdims=True)
        return mx, idx

    gl = jnp.where(lane < ng, logits, NEG_INF)
    gmax, gsel = first_argmax(gl)
    g_w = 1.0 / jnp.sum(jnp.exp(gl - gmax), axis=-1, keepdims=True)
    lo = ng + gsel * epg
    el = jnp.where(jnp.logical_and(lane >= lo, lane < lo + epg), logits, NEG_INF)
    v1, i1 = first_argmax(el)
    v2, i2 = first_argmax(jnp.where(lane == i1, NEG_INF, el))
    e21 = jnp.exp(v2 - v1)
    w1 = g_w / (1.0 + e21)
    w2 = g_w * e21 / (1.0 + e21)

    onehot = jnp.logical_or(lane == i1, lane == i2)
    r = lax.broadcasted_iota(I32, (tm, tm), 0)
    cc = lax.broadcasted_iota(I32, (tm, tm), 1)
    tri = (cc < r).astype(BF16)
    before = _dot(tri, onehot.astype(BF16)) + run_s[...]
    rank1 = jnp.sum(jnp.where(lane == i1, before, 0.0), axis=-1, keepdims=True).astype(I32)
    rank2 = jnp.sum(jnp.where(lane == i2, before, 0.0), axis=-1, keepdims=True).astype(I32)
    run_s[...] = run_s[...] + jnp.sum(onehot.astype(F32), axis=0, keepdims=True)

    sel = jnp.where(lane == 0, i1 - ng, jnp.where(lane == 1, i2 - ng,
                    jnp.where(lane == 2, rank1, jnp.where(lane == 3, rank2, 0))))
    sel_ref[...] = sel
    wts_ref[...] = jnp.where(lane == 0, w1, jnp.where(lane == 1, w2, 0.0))
    cnt_ref[...] = run_s[...].astype(I32)


def _router(x, w, b, *, tm):
    m = x.shape[0]
    row = pl.BlockSpec((tm, ROUTER_LANES), lambda i: (i, 0))
    return pl.pallas_call(
        functools.partial(_router_body, tm=tm),
        out_shape=[jax.ShapeDtypeStruct((m, ROUTER_LANES), I32),
                   jax.ShapeDtypeStruct((m, ROUTER_LANES), F32),
                   jax.ShapeDtypeStruct((1, ROUTER_LANES), I32)],
        grid=(m // tm,),
        in_specs=[pl.BlockSpec((tm, D_MODEL), lambda i: (i, 0)),
                  pl.BlockSpec((D_MODEL, ROUTER_LANES), lambda i: (0, 0)),
                  pl.BlockSpec((1, ROUTER_LANES), lambda i: (0, 0))],
        out_specs=[row, row, pl.BlockSpec((1, ROUTER_LANES), lambda i: (0, 0))],
        scratch_shapes=[pltpu.VMEM((1, ROUTER_LANES), F32)],
        compiler_params=_cparams(("arbitrary",), 16),
        name="moe_router",
    )(x, w, b)


def _moe_body(te_ref, src_ref, nact_ref, x_hbm, wg_ref, wu_ref, wd_ref, o_ref,
              xbuf, sem, wg_s, wu_s, wd_s):
    i = pl.program_id(0)
    nact = nact_ref[0]
    tm = MOE_TILE
    slot = i % 2

    def issue_gather(tile, slot_):
        base = tile * tm
        for r in range(tm):
            pltpu.make_async_copy(x_hbm.at[pl.ds(src_ref[base + r], 1)],
                                  xbuf.at[slot_, pl.ds(r, 1)], sem.at[slot_]).start()

    def tile_step(prefetch):
        pltpu.make_async_copy(x_hbm.at[pl.ds(0, tm)], xbuf.at[slot], sem.at[slot]).wait()
        prev = te_ref[jnp.maximum(i - 1, 0)]

        @pl.when(jnp.logical_or(i == 0, te_ref[i] != prev))
        def _():
            wg_s[...] = wg_ref[...].astype(BF16)
            wu_s[...] = wu_ref[...].astype(BF16)
            wd_s[...] = wd_ref[...].astype(BF16)

        if prefetch:
            issue_gather(i + 1, 1 - slot)
        x = xbuf[slot].astype(BF16)
        hg = _dot(x, wg_s[...])
        hu = _dot(x, wu_s[...])
        h = hg * (1.0 / (1.0 + jnp.exp(-hg))) * hu
        o_ref[...] = _dot(h.astype(BF16), wd_s[...])

    @pl.when(i == 0)
    def _():
        issue_gather(0, 0)

    @pl.when(i + 1 < nact)
    def _():
        tile_step(True)

    @pl.when(i + 1 == nact)
    def _():
        tile_step(False)

    @pl.when(i >= nact)
    def _():
        o_ref[...] = jnp.zeros_like(o_ref)


def _moe_experts(x, w_gate, w_up, w_down, tile_expert, src, nact, *, n_tiles):
    tm = MOE_TILE
    grid_spec = pltpu.PrefetchScalarGridSpec(
        num_scalar_prefetch=3,
        grid=(n_tiles,),
        in_specs=[pl.BlockSpec(memory_space=pl.ANY),
                  pl.BlockSpec((None, D_MODEL, D_EXPERT), lambda i, te, s, n: (te[i], 0, 0)),
                  pl.BlockSpec((None, D_MODEL, D_EXPERT), lambda i, te, s, n: (te[i], 0, 0)),
                  pl.BlockSpec((None, D_EXPERT, D_MODEL), lambda i, te, s, n: (te[i], 0, 0))],
        out_specs=pl.BlockSpec((tm, D_MODEL), lambda i, te, s, n: (i, 0)),
        scratch_shapes=[pltpu.VMEM((2, tm, D_MODEL), F32),
                        pltpu.SemaphoreType.DMA((2,)),
                        pltpu.VMEM((D_MODEL, D_EXPERT), BF16),
                        pltpu.VMEM((D_MODEL, D_EXPERT), BF16),
                        pltpu.VMEM((D_EXPERT, D_MODEL), BF16)],
    )
    return pl.pallas_call(
        _moe_body,
        out_shape=jax.ShapeDtypeStruct((n_tiles * tm, D_MODEL), F32),
        grid_spec=grid_spec,
        compiler_params=_cparams(("arbitrary",), 48),
        name="moe_experts",
    )(tile_expert, src, nact, x, w_gate, w_up, w_down)


def _combine_body(pos_ref, ys_hbm, wts_ref, x_ref, g_ref, b_ref, o_ref, buf, sem, *, tc):
    i = pl.program_id(0)
    n = pl.num_programs(0)
    slot = i % 2

    def issue_gather(tile, slot_):
        base = tile * tc * 2
        for r in range(tc):
            for k in range(2):
                pltpu.make_async_copy(ys_hbm.at[pl.ds(pos_ref[base + 2 * r + k], 1)],
                                      buf.at[slot_, k, pl.ds(r, 1)], sem.at[slot_]).start()

    @pl.when(i == 0)
    def _():
        issue_gather(0, 0)

    for k in range(2):
        pltpu.make_async_copy(ys_hbm.at[pl.ds(0, tc)], buf.at[slot, k], sem.at[slot]).wait()

    @pl.when(i + 1 < n)
    def _():
        issue_gather(i + 1, 1 - slot)

    w = wts_ref[...]
    moe = w[:, 0:1] * buf[slot, 0] + w[:, 1:2] * buf[slot, 1]
    o_ref[...] = _layernorm(DEEPNORM_ALPHA * x_ref[...] + moe, g_ref[...], b_ref[...])


def _moe_combine(ys, pos, wts, x, g, b, *, tc):
    m = x.shape[0]
    grid_spec = pltpu.PrefetchScalarGridSpec(
        num_scalar_prefetch=1,
        grid=(m // tc,),
        in_specs=[pl.BlockSpec(memory_space=pl.ANY),
                  pl.BlockSpec((tc, ROUTER_LANES), lambda i, p: (i, 0)),
                  pl.BlockSpec((tc, D_MODEL), lambda i, p: (i, 0)),
                  pl.BlockSpec((1, D_MODEL), lambda i, p: (0, 0)),
                  pl.BlockSpec((1, D_MODEL), lambda i, p: (0, 0))],
        out_specs=pl.BlockSpec((tc, D_MODEL), lambda i, p: (i, 0)),
        scratch_shapes=[pltpu.VMEM((2, 2, tc, D_MODEL), F32), pltpu.SemaphoreType.DMA((2,))],
    )
    return pl.pallas_call(
        functools.partial(_combine_body, tc=tc),
        out_shape=jax.ShapeDtypeStruct((m, D_MODEL), F32),
        grid_spec=grid_spec,
        compiler_params=_cparams(("arbitrary",), 8 * tc * D_MODEL * 4 / 2**20 + 8),
        name="moe_combine_ln3",
    )(pos, ys, wts, x, g, b)


def _moe(x, w_r1, b_r1, w_r2, b_r2, w_gate, w_up, w_down, g, b, *, tm_router, tc):
    n = x.shape[0]
    ng, ne = N_EXPERT_GROUPS, N_EXPERTS
    pad = ROUTER_LANES - ng - ne
    w_r = jnp.concatenate([w_r1, w_r2.reshape(D_MODEL, ne), jnp.zeros((D_MODEL, pad), F32)], axis=1)
    b_r = jnp.concatenate([b_r1, b_r2.reshape(ne), jnp.zeros((pad,), F32)]).reshape(1, ROUTER_LANES)
    sel, wts, cnt = _router(x, w_r, b_r, tm=tm_router)

    tm = MOE_TILE
    n_tiles = (2 * n) // tm + ne
    counts = cnt[0, ng:ng + ne]
    tiles_per = (counts + tm - 1) // tm
    tile_end = jnp.cumsum(tiles_per)
    row_off = (tile_end - tiles_per) * tm
    nact = tile_end[-1]
    ids, ranks = sel[:, 0:2], sel[:, 2:4]
    pos = row_off[ids] + ranks
    tile_ids = jnp.minimum(jnp.arange(n_tiles, dtype=I32), nact - 1)
    tile_expert = jnp.sum((tile_end[None, :] <= tile_ids[:, None]).astype(I32), axis=1)
    token = jnp.broadcast_to(jnp.arange(n, dtype=I32)[:, None], (n, 2))
    src = jnp.zeros((n_tiles * tm,), I32).at[pos.reshape(-1)].set(token.reshape(-1))

    ys = _moe_experts(x, w_gate, w_up, w_down, tile_expert, src, nact.reshape(1).astype(I32),
                      n_tiles=n_tiles)
    return _moe_combine(ys, pos.reshape(-1).astype(I32), wts, x, g, b, tc=tc)


def _row_tile(m, cap):
    best = SUBLANES
    for t in range(SUBLANES, cap + 1, SUBLANES):
        if m % t == 0:
            best = t
    return best


def kernel(x_prompt, x_sample, cache_win_k, cache_win_v, state_ssm_re, state_ssm_im, cache_mem_k, cache_mem_v, mem_prompt, w_in, ssm_lam_re, ssm_lam_im, ssm_log_dt, ssm_b_re, ssm_b_im, ssm_c_re, ssm_c_im, ssm_d, w_glu, g_attn, g_ssm, w_out, ln1_g, ln1_b, w_mq, w_mk, w_mv, w_mo, ln2_g, ln2_b, w_r1, b_r1, w_r2, b_r2, w_gate, w_up, w_down, ln3_g, ln3_b):
    nb, seq, d = x_prompt.shape
    ns, dseq, _ = x_sample.shape
    n_p, n_s = nb * seq, ns * dseq
    n = n_p + n_s
    l = 0
    row2 = lambda v: v[l].reshape(1, -1)

    x_all = jnp.concatenate([x_prompt.reshape(n_p, d), x_sample.reshape(n_s, d)], axis=0)
    tm_big = _row_tile(n, 768)
    tm_ln = _row_tile(n, 384)

    proj = _matmul(x_all, w_in[l].astype(BF16), tm=tm_big, tn=1024, name="proj_in")

    attn_p = _attn_prompt(proj, n_batch=nb, seq=seq)
    attn_s = _attn_sample(proj, cache_win_k[l], cache_win_v[l], row0=n_p, n_seq=ns, n_new=dseq)
    attn = jnp.concatenate([attn_p, attn_s], axis=0)

    seg_len = seq // SSM_SEGMENTS
    prm = _ssm_params(ssm_lam_re[l], ssm_lam_im[l], ssm_log_dt[l], ssm_b_re[l], ssm_b_im[l],
                      ssm_c_re[l], ssm_c_im[l], ssm_d[l], seg_len)
    u_p = proj[:n_p, 3 * D_ATT:].reshape(nb, SSM_SEGMENTS, seg_len, D_SSM)
    u_p = jnp.transpose(u_p, (2, 0, 1, 3)).reshape(seg_len, nb * SSM_SEGMENTS, D_SSM)
    zeros = jnp.zeros((nb * SSM_SEGMENTS, N_SSM_GROUPS * SSM_STATE), F32)
    tl = _row_tile(seg_len, 32)
    end_re, end_im = _ssm_scan(u_p, prm, zeros, zeros, tl=tl, nseg=1, emit_y=False, exact_in=False)
    yg_p, fin_re, fin_im = _ssm_scan(u_p, prm, end_re, end_im, tl=tl, nseg=SSM_SEGMENTS,
                                     emit_y=True, exact_in=False)
    yg_p = jnp.transpose(yg_p.reshape(seg_len, nb, SSM_SEGMENTS, D_SSM), (1, 2, 0, 3))
    last = SSM_SEGMENTS - 1
    ssm_re_p = fin_re.reshape(nb, SSM_SEGMENTS, N_SSM_GROUPS, SSM_STATE)[:, last]
    ssm_im_p = fin_im.reshape(nb, SSM_SEGMENTS, N_SSM_GROUPS, SSM_STATE)[:, last]

    u_s = jnp.transpose(proj[n_p:, 3 * D_ATT:].reshape(ns, dseq, D_SSM), (1, 0, 2))
    h0_re = state_ssm_re[l].reshape(ns, -1)
    h0_im = state_ssm_im[l].reshape(ns, -1)
    yg_s, ssm_re_s, ssm_im_s = _ssm_scan(u_s, prm, h0_re, h0_im, tl=dseq, nseg=1,
                                         emit_y=True, exact_in=True)
    yg_s = jnp.transpose(yg_s, (1, 0, 2))
    yg = jnp.concatenate([yg_p.reshape(n_p, D_SSM), yg_s.reshape(n_s, D_SSM)], axis=0)
    ssm_out = _glu(yg, w_glu[l].astype(BF16), tm=tm_big)

    x1 = _mix(attn, ssm_out, row2(g_attn), row2(g_ssm), w_out[l].astype(BF16), x_all,
              row2(ln1_g), row2(ln1_b), tm=tm_ln)

    mem_rows = mem_prompt.reshape(nb * N_MEM, d)
    mem_k = _matmul(mem_rows, w_mk[l].astype(BF16), tm=nb * N_MEM, tn=1024, name="mem_k")
    mem_v = _matmul(mem_rows, w_mv[l].astype(BF16), tm=nb * N_MEM, tn=1024, name="mem_v")
    q_mem = _matmul(x1, w_mq[l].astype(BF16), tm=tm_big, tn=1024, name="mem_q")
    o_p = _memattn(q_mem, mem_k.reshape(nb, N_MEM, d), mem_v.reshape(nb, N_MEM, d),
                   row0=0, n_seq=nb, seq=seq, tq=_row_tile(seq, 512), name="memattn_prompt")
    o_s = _memattn_heads(q_mem, cache_mem_k[l], cache_mem_v[l], row0=n_p, n_seq=ns, seq=dseq,
                         name="memattn_sample")
    o_mem = jnp.concatenate([o_p, o_s], axis=0)
    x2 = _mm_ln(o_mem, w_mo[l].astype(BF16), x1, row2(ln2_g), row2(ln2_b), tm=tm_ln,
                name="mem_out_ln2")

    y = _moe(x2, w_r1[l], b_r1[l], w_r2[l], b_r2[l], w_gate[l], w_up[l], w_down[l],
             row2(ln3_g), row2(ln3_b), tm_router=_row_tile(n, 256), tc=_row_tile(n, 128))

    y_p = y[:n_p].reshape(nb, seq, d)
    y_s = y[n_p:].reshape(ns, dseq, d)
    k_p = proj[:n_p, D_ATT:2 * D_ATT].reshape(nb, seq, ATT_HEADS, ATT_HD)
    v_p = proj[:n_p, 2 * D_ATT:3 * D_ATT].reshape(nb, seq, ATT_HEADS, ATT_HD)
    wp = min(max(w for w, _ in DILATIONS), seq)
    k_s = proj[n_p:, D_ATT:2 * D_ATT].reshape(ns, dseq, ATT_HEADS, ATT_HD)
    v_s = proj[n_p:, 2 * D_ATT:3 * D_ATT].reshape(ns, dseq, ATT_HEADS, ATT_HD)
    state = lambda v, b_: v.reshape(1, b_, N_SSM_GROUPS, SSM_STATE)
    return (y_p, y_s, k_p[None, :, seq - wp:], v_p[None, :, seq - wp:], k_s[None], v_s[None],
            state(ssm_re_p, nb), state(ssm_im_p, nb), state(ssm_re_s, ns), state(ssm_im_s, ns),
            mem_k.reshape(1, nb, N_MEM, MEM_HEADS, MEM_HD),
            mem_v.reshape(1, nb, N_MEM, MEM_HEADS, MEM_HD))
```

```python
import functools
import math

import numpy as np
import jax
import jax.numpy as jnp
from jax import lax
from jax.experimental import pallas as pl
from jax.experimental.pallas import tpu as pltpu

F32 = jnp.float32
BF16 = jnp.bfloat16
I32 = jnp.int32

D_MODEL = 2048
PAST_LEN = 8192
D_ATT = D_MODEL // 2
ATT_HEADS = 8
ATT_HD = D_ATT // ATT_HEADS
DILATIONS = ((128, 1), (512, 4), (2048, 16))
D_SSM = D_MODEL - D_ATT
SSM_GROUP_CH = 16
N_SSM_GROUPS = D_SSM // SSM_GROUP_CH
SSM_STATE = 64
N_MEM = 256
MEM_HEADS = 4
MEM_HD = D_MODEL // MEM_HEADS
N_EXPERT_GROUPS = 4
EXPERTS_PER_GROUP = 8
N_EXPERTS = N_EXPERT_GROUPS * EXPERTS_PER_GROUP
D_EXPERT = D_MODEL // 4
DEPTH = 1
DEEPNORM_ALPHA = (2.0 * DEPTH) ** 0.25
LN_EPS = 1e-5
RMS_EPS = 1e-6

LANES = 128
SUBLANES = 8
ROW_CHUNKS = D_MODEL // LANES
Q_BLOCK = 128
ATTN_GROUP = 8
SSM_LANE_TILE = 128
SSM_GROUPS_PER_TILE = SSM_LANE_TILE // SSM_GROUP_CH
SSM_STATES_PER_TILE = SSM_GROUPS_PER_TILE * SSM_STATE
SSM_SEGMENTS = 8
MOE_TILE = 256
ROUTER_LANES = 128
NEG_INF = float("-inf")


def _cparams(semantics, vmem_mib):
    return pltpu.CompilerParams(dimension_semantics=semantics,
                                vmem_limit_bytes=int(vmem_mib) << 20)


def _layernorm(y, g, b):
    mu = jnp.mean(y, axis=-1, keepdims=True)
    yc = y - mu
    var = jnp.mean(yc * yc, axis=-1, keepdims=True)
    return yc * lax.rsqrt(var + LN_EPS) * g + b


def _rmsnorm(v, g):
    return v * lax.rsqrt(jnp.mean(v * v, axis=-1, keepdims=True) + RMS_EPS) * g


def _dot(a, b):
    return jnp.dot(a, b, preferred_element_type=F32)


def _dot_nt(a, b):
    return lax.dot_general(a, b, (((1,), (1,)), ((), ())), preferred_element_type=F32)


def _mm_body(x_ref, w_ref, o_ref):
    o_ref[...] = _dot(x_ref[...].astype(BF16), w_ref[...]).astype(o_ref.dtype)


def _matmul(x, w, *, tm, tn, name):
    m, k = x.shape
    n = w.shape[1]
    vmem = 2 * (tm * k * x.dtype.itemsize + k * tn * 2 + tm * tn * 4) / 2**20 + 8
    return pl.pallas_call(
        _mm_body,
        out_shape=jax.ShapeDtypeStruct((m, n), F32),
        grid=(n // tn, m // tm),
        in_specs=[pl.BlockSpec((tm, k), lambda j, i: (i, 0)),
                  pl.BlockSpec((k, tn), lambda j, i: (0, j))],
        out_specs=pl.BlockSpec((tm, tn), lambda j, i: (i, j)),
        compiler_params=_cparams(("parallel", "parallel"), vmem),
        name=name,
    )(x, w)


def _attn_prompt_body(q_ref, k_ref, v_ref, o_ref, kt_s, va_s, on_s, lse_s, *, seq, dilations):
    scale = ATT_HD ** -0.5
    nblk = seq // Q_BLOCK
    qi = lax.broadcasted_iota(I32, (Q_BLOCK, Q_BLOCK), 0)
    kj = lax.broadcasted_iota(I32, (Q_BLOCK, Q_BLOCK), 1)
    cur_ok = kj <= qi
    prev_ok = kj >= qi
    va_s[:, :, ATT_HD:] = jnp.ones((nblk, Q_BLOCK, ATT_HD), BF16)

    for br, (_, d) in enumerate(dilations):
        span = d * Q_BLOCK
        nb = seq // span

        def stream_rows(t, d=d, span=span, nb=nb):
            r = t // nb
            ib = t % nb
            return r, ib, pl.ds(r + ib * span, Q_BLOCK, stride=d)

        def prep(g, carry, stream_rows=stream_rows):
            loaded = []
            for j in range(ATTN_GROUP):
                t = g * ATTN_GROUP + j
                _, _, rows = stream_rows(t)
                loaded.append((t, k_ref[rows, :], v_ref[rows, :]))
            for t, kk, vv in loaded:
                kt_s[t] = jnp.transpose(kk).astype(BF16)
                va_s[t, :, 0:ATT_HD] = vv.astype(BF16)
            return carry

        lax.fori_loop(0, nblk // ATTN_GROUP, prep, 0)

        def group(g, carry, br=br, nb=nb, stream_rows=stream_rows):
            scores = []
            for j in range(ATTN_GROUP):
                t = g * ATTN_GROUP + j
                r, ib, rows = stream_rows(t)
                tp = jnp.maximum(t - 1, r * nb)
                q = (q_ref[rows, :] * scale).astype(BF16)
                s = _dot(q, jnp.concatenate([kt_s[tp], kt_s[t]], axis=1))
                scores.append((t, tp, ib, rows, s))
            probs = []
            for t, tp, ib, rows, s in scores:
                ok = jnp.concatenate([jnp.logical_and(prev_ok, ib > 0), cur_ok], axis=1)
                s = jnp.where(ok, s, NEG_INF)
                m = jnp.max(s, axis=-1, keepdims=True)
                probs.append((t, tp, rows, m, jnp.exp(s - m).astype(BF16)))
            outs = [(rows, m, _dot(p, jnp.concatenate([va_s[tp], va_s[t]], axis=0)))
                    for t, tp, rows, m, p in probs]
            for rows, m, al in outs:
                l = al[:, ATT_HD:]
                on_s[br, rows, :] = al[:, :ATT_HD] / l
                lse_s[br, rows, :] = m + jnp.log(l)
            return carry

        lax.fori_loop(0, nblk // ATTN_GROUP, group, 0)

    chunk = 256
    nbr = len(dilations)

    def merge(c, carry):
        rows = pl.ds(pl.multiple_of(c * chunk, chunk), chunk)
        ls = [lse_s[b, rows, :] for b in range(nbr)]
        mx = functools.reduce(jnp.maximum, ls)
        es = [jnp.exp(li - mx) for li in ls]
        num = sum(es[b] * on_s[b, rows, :] for b in range(nbr))
        o_ref[rows, :] = num / sum(es)
        return carry

    lax.fori_loop(0, seq // chunk, merge, 0)


def _attn_prompt(proj, *, n_batch, seq, dilations=DILATIONS):
    for w, d in dilations:
        assert w // d == Q_BLOCK and seq % (d * Q_BLOCK) == 0
    nbr = len(dilations)
    nblk = seq // Q_BLOCK
    assert nblk % ATTN_GROUP == 0
    blk = lambda off: pl.BlockSpec((seq, ATT_HD), lambda b, h, off=off: (b, off + h))
    vmem = ((4 * 2 + 2 * nbr) * seq * ATT_HD * 4 + 3 * seq * ATT_HD * 2) / 2**20 + 8
    return pl.pallas_call(
        functools.partial(_attn_prompt_body, seq=seq, dilations=dilations),
        out_shape=jax.ShapeDtypeStruct((n_batch * seq, D_ATT), F32),
        grid=(n_batch, ATT_HEADS),
        in_specs=[blk(0), blk(ATT_HEADS), blk(2 * ATT_HEADS)],
        out_specs=pl.BlockSpec((seq, ATT_HD), lambda b, h: (b, h)),
        scratch_shapes=[pltpu.VMEM((nblk, ATT_HD, Q_BLOCK), BF16),
                        pltpu.VMEM((nblk, Q_BLOCK, 2 * ATT_HD), BF16),
                        pltpu.VMEM((nbr, seq, ATT_HD), F32),
                        pltpu.VMEM((nbr, seq, ATT_HD), F32)],
        compiler_params=_cparams(("parallel", "parallel"), vmem),
        name="attn_prompt",
    )(proj, proj, proj)


def _sample_key_multiplicity(n_new, n_cache, past_len, dilations):
    d_max = max(d for _, d in dilations)
    tail = max(w for w, d in dilations if d != d_max)
    assert past_len % d_max == 0 and n_cache % d_max == 0 and n_new <= d_max // 2
    assert tail % d_max == 0 and tail <= n_cache
    half = d_max // 2
    n_grid = (n_cache - tail) // d_max
    kv_start = past_len - n_cache
    grid_rows = (np.arange(n_grid)[:, None] * d_max + np.arange(half)[None, :]).reshape(-1)
    tail_rows = n_cache - tail + np.arange(tail)
    new_rows = n_cache + np.arange(n_new)
    qpos = past_len + np.arange(n_new)

    def mult(rows):
        kpos = kv_start + rows
        delta = qpos[:, None] - kpos[None, :]
        c = np.zeros(delta.shape, np.float32)
        for w, d in dilations:
            c += ((delta >= 0) & (delta <= w) & (delta % d == 0) & (kpos[None, :] >= kv_start))
        return c

    fetched = np.zeros(n_cache + n_new, bool)
    fetched[grid_rows] = True
    fetched[tail_rows] = True
    fetched[new_rows] = True
    assert not mult(np.nonzero(~fetched)[0]).any()
    return mult(grid_rows), mult(tail_rows), mult(new_rows), n_grid, tail, half, d_max


def _attn_sample_body(q_ref, kn_ref, vn_ref, kg_ref, kt_ref, vg_ref, vt_ref,
                      cg_ref, ct_ref, cn_ref, o_ref):
    scale = ATT_HD ** -0.5
    heads = lambda ref: jnp.concatenate(
        [ref[:, h * ATT_HD:(h + 1) * ATT_HD] for h in range(ATT_HEADS)], axis=0)
    q = (heads(q_ref) * scale).astype(BF16)
    kn = heads(kn_ref).astype(BF16)
    vn = heads(vn_ref).astype(BF16)
    flat = lambda ref: ref[...].reshape(-1, ATT_HD).astype(BF16)
    cg, ct, cn = cg_ref[...], ct_ref[...], cn_ref[...]
    sg = jnp.where(cg > 0, _dot_nt(q, flat(kg_ref)), NEG_INF)
    st = jnp.where(ct > 0, _dot_nt(q, flat(kt_ref)), NEG_INF)
    sn = jnp.where(cn > 0, _dot_nt(q, kn), NEG_INF)
    m = jnp.maximum(jnp.maximum(jnp.max(sg, axis=-1, keepdims=True),
                                jnp.max(st, axis=-1, keepdims=True)),
                    jnp.max(sn, axis=-1, keepdims=True))
    pg = cg * jnp.exp(sg - m)
    pt = ct * jnp.exp(st - m)
    pn = cn * jnp.exp(sn - m)
    l = (jnp.sum(pg, axis=-1, keepdims=True) + jnp.sum(pt, axis=-1, keepdims=True)
         + jnp.sum(pn, axis=-1, keepdims=True))
    acc = (_dot(pg.astype(BF16), flat(vg_ref)) + _dot(pt.astype(BF16), flat(vt_ref))
           + _dot(pn.astype(BF16), vn))
    out = acc / l
    n_new = q_ref.shape[0]
    for h in range(ATT_HEADS):
        o_ref[:, h * ATT_HD:(h + 1) * ATT_HD] = out[h * n_new:(h + 1) * n_new, :]


def _attn_sample(proj, win_k, win_v, *, row0, n_seq, n_new, past_len=PAST_LEN,
                 dilations=DILATIONS):
    n_cache = win_k.shape[1]
    cg, ct, cn, n_grid, tail, half, d_max = _sample_key_multiplicity(
        n_new, n_cache, past_len, dilations)
    assert row0 % n_new == 0 and n_new % SUBLANES == 0 and n_cache % tail == 0
    eye = np.eye(ATT_HEADS, dtype=np.float32)
    key_major = lambda c: np.einsum("tk,hg->htkg", c, eye).reshape(ATT_HEADS * n_new, -1)
    head_major = lambda c: np.einsum("tk,hg->htgk", c, eye).reshape(ATT_HEADS * n_new, -1)
    cg, ct, cn = key_major(cg), key_major(ct), head_major(cn)
    rb = row0 // n_new
    n_groups = n_cache // d_max
    kgv = win_k.reshape(n_seq, n_groups, d_max, ATT_HEADS, ATT_HD)
    vgv = win_v.reshape(n_seq, n_groups, d_max, ATT_HEADS, ATT_HD)
    ktv = win_k.reshape(n_seq, n_cache // tail, tail, ATT_HEADS, ATT_HD)
    vtv = win_v.reshape(n_seq, n_cache // tail, tail, ATT_HEADS, ATT_HD)
    new = lambda off: pl.BlockSpec((n_new, D_ATT), lambda b, off=off: (rb + b, off))
    grid_spec = pl.BlockSpec((None, n_grid, half, ATT_HEADS, ATT_HD), lambda b: (b, 0, 0, 0, 0))
    tail_spec = pl.BlockSpec((None, None, tail, ATT_HEADS, ATT_HD),
                             lambda b: (b, n_cache // tail - 1, 0, 0, 0))
    const = lambda a: pl.BlockSpec(a.shape, lambda b: (0, 0))
    vmem = (2 * 2 * (n_grid * half + tail) * D_ATT * 4 + 4 * cg.size * 4 * 3) / 2**20 + 12
    return pl.pallas_call(
        _attn_sample_body,
        out_shape=jax.ShapeDtypeStruct((n_seq * n_new, D_ATT), F32),
        grid=(n_seq,),
        in_specs=[new(0), new(1), new(2), grid_spec, tail_spec, grid_spec, tail_spec,
                  const(cg), const(ct), const(cn)],
        out_specs=pl.BlockSpec((n_new, D_ATT), lambda b: (b, 0)),
        compiler_params=_cparams(("parallel",), vmem),
        name="attn_sample",
    )(proj, proj, proj, kgv, ktv, vgv, vtv, jnp.asarray(cg), jnp.asarray(ct), jnp.asarray(cn))


def _gelu_tanh(x):
    return 0.5 * x * (1.0 + jnp.tanh(math.sqrt(2.0 / math.pi) * (x + 0.044715 * (x * x * x))))


def _ssm_body(u_ref, bb_ref, cst_ref, a_ref, ap_ref, d_ref, hre_ref, him_ref, *rest,
              tl, npar, nseg, emit_y, exact_in):
    if emit_y:
        y_ref, fre_ref, fim_ref, x_s, h_s = rest
    else:
        fre_ref, fim_ref, x_s, h_s = rest
    ns = SSM_STATES_PER_TILE
    c = pl.program_id(1)

    @pl.when(c == 0)
    def _init():
        if nseg == 1:
            h_s[0] = hre_ref[...]
            h_s[1] = him_ref[...]
        else:
            pr, pi = ap_ref[0:1, :], ap_ref[1:2, :]
            for b in range(npar // nseg):
                sr = jnp.zeros((1, ns), F32)
                si = jnp.zeros((1, ns), F32)
                for j in range(nseg):
                    row = b * nseg + j
                    h_s[0, row:row + 1, :] = sr
                    h_s[1, row:row + 1, :] = si
                    er, ei = hre_ref[row:row + 1, :], him_ref[row:row + 1, :]
                    sr, si = pr * sr - pi * si + er, pr * si + pi * sr + ei

    u = u_ref[...].reshape(tl * npar, SSM_LANE_TILE)
    if exact_in:
        x_s[...] = jnp.dot(u, bb_ref[...], precision=lax.Precision.HIGHEST,
                           preferred_element_type=F32)
    else:
        x_s[...] = _dot(u.astype(BF16), bb_ref[...])

    ar = jnp.broadcast_to(a_ref[0:1, :], (SUBLANES, ns))
    ai = jnp.broadcast_to(a_ref[1:2, :], (SUBLANES, ns))
    ngrp = npar // SUBLANES

    def step(i, carry):
        out = []
        for g in range(ngrp):
            hr, hi = carry[2 * g], carry[2 * g + 1]
            rows = pl.ds(pl.multiple_of(i * npar + g * SUBLANES, SUBLANES), SUBLANES)
            nr = ar * hr - ai * hi + x_s[rows, 0:ns]
            ni = ar * hi + ai * hr + x_s[rows, ns:2 * ns]
            if emit_y:
                x_s[rows, 0:ns] = nr
                x_s[rows, ns:2 * ns] = ni
            out += [nr, ni]
        return tuple(out)

    init = []
    for g in range(ngrp):
        gs = slice(g * SUBLANES, (g + 1) * SUBLANES)
        init += [h_s[0, gs, :], h_s[1, gs, :]]
    fin = lax.fori_loop(0, tl, step, tuple(init), unroll=4)
    for g in range(ngrp):
        gs = slice(g * SUBLANES, (g + 1) * SUBLANES)
        h_s[0, gs, :] = fin[2 * g]
        h_s[1, gs, :] = fin[2 * g + 1]

    if emit_y:
        y = _dot(x_s[...].astype(BF16), cst_ref[...]) + d_ref[...] * u
        y_ref[...] = _gelu_tanh(y).reshape(tl, npar, SSM_LANE_TILE)

    @pl.when(c == pl.num_programs(1) - 1)
    def _fin():
        fre_ref[...] = h_s[0]
        fim_ref[...] = h_s[1]


def _ssm_scan(u3, prm, hin_re, hin_im, *, tl, nseg, emit_y, exact_in):
    n_steps, npar, d_ssm = u3.shape
    ns = SSM_STATES_PER_TILE
    nk = d_ssm // SSM_LANE_TILE
    bb = prm["bb_f32"] if exact_in else prm["bb_bf16"]
    in_specs = [
        pl.BlockSpec((tl, npar, SSM_LANE_TILE), lambda k, c: (c, 0, k)),
        pl.BlockSpec((None, SSM_LANE_TILE, 2 * ns), lambda k, c: (k, 0, 0)),
        pl.BlockSpec((None, 2 * ns, SSM_LANE_TILE), lambda k, c: (k, 0, 0)),
        pl.BlockSpec((None, 2, ns), lambda k, c: (k, 0, 0)),
        pl.BlockSpec((None, 2, ns), lambda k, c: (k, 0, 0)),
        pl.BlockSpec((1, SSM_LANE_TILE), lambda k, c: (0, k)),
        pl.BlockSpec((npar, ns), lambda k, c: (0, k)),
        pl.BlockSpec((npar, ns), lambda k, c: (0, k)),
    ]
    state_shape = jax.ShapeDtypeStruct((npar, nk * ns), F32)
    state_spec = pl.BlockSpec((npar, ns), lambda k, c: (0, k))
    out_shape = [state_shape, state_shape]
    out_specs = [state_spec, state_spec]
    if emit_y:
        out_shape = [jax.ShapeDtypeStruct(u3.shape, F32)] + out_shape
        out_specs = [pl.BlockSpec((tl, npar, SSM_LANE_TILE), lambda k, c: (c, 0, k))] + out_specs
    return pl.pallas_call(
        functools.partial(_ssm_body, tl=tl, npar=npar, nseg=nseg, emit_y=emit_y,
                          exact_in=exact_in),
        out_shape=out_shape,
        grid=(nk, n_steps // tl),
        in_specs=in_specs,
        out_specs=out_specs,
        scratch_shapes=[pltpu.VMEM((tl * npar, 2 * ns), F32), pltpu.VMEM((2, npar, ns), F32)],
        compiler_params=_cparams(("parallel", "arbitrary"), 32),
        name="ssm_scan_y" if emit_y else "ssm_scan_state",
    )(u3, bb, prm["cst"], prm["a"], prm["apow"], prm["d"], hin_re, hin_im)


def _ssm_params(lam_re, lam_im, log_dt, b_re, b_im, c_re, c_im, d_skip, seg_len):
    g, p, c = N_SSM_GROUPS, SSM_STATE, SSM_GROUP_CH
    nk, gt = g // SSM_GROUPS_PER_TILE, SSM_GROUPS_PER_TILE
    dt = jnp.exp(log_dt.astype(F32))[:, None]
    lr, li = lam_re.astype(F32), lam_im.astype(F32)
    mag = jnp.exp(lr * dt)
    a_re, a_im = mag * jnp.cos(li * dt), mag * jnp.sin(li * dt)
    magp = jnp.exp(lr * dt * seg_len)
    p_re, p_im = magp * jnp.cos(li * dt * seg_len), magp * jnp.sin(li * dt * seg_len)
    den = lr * lr + li * li
    nr, ni = a_re - 1.0, a_im
    f_re, f_im = (nr * lr + ni * li) / den, (ni * lr - nr * li) / den
    br, bi = b_re.astype(F32), b_im.astype(F32)
    bb_re = f_re[..., None] * br - f_im[..., None] * bi
    bb_im = f_re[..., None] * bi + f_im[..., None] * br
    eye = jnp.eye(gt, dtype=F32)

    def pack_b(m):
        return jnp.einsum("kgpc,gh->kgchp", m.reshape(nk, gt, p, c), eye).reshape(nk, gt * c, gt * p)

    def pack_c(m):
        return jnp.einsum("kgcp,gh->kgphc", m.reshape(nk, gt, c, p), eye).reshape(nk, gt * p, gt * c)

    bb = jnp.concatenate([pack_b(bb_re), pack_b(bb_im)], axis=2)
    cst = jnp.concatenate([pack_c(c_re.astype(F32)), -pack_c(c_im.astype(F32))], axis=1)
    tile = lambda v: v.reshape(nk, 1, gt * p)
    return {
        "bb_f32": bb, "bb_bf16": bb.astype(BF16), "cst": cst.astype(BF16),
        "a": jnp.concatenate([tile(a_re), tile(a_im)], axis=1),
        "apow": jnp.concatenate([tile(p_re), tile(p_im)], axis=1),
        "d": d_skip.astype(F32).reshape(1, g * c),
    }


def _glu_body(y_ref, w_ref, o_ref):
    yg = y_ref[...]
    z = _dot(yg.astype(BF16), w_ref[...])
    o_ref[...] = yg * (1.0 / (1.0 + jnp.exp(-z)))


def _glu(yg, w, *, tm):
    m, n = yg.shape
    return pl.pallas_call(
        _glu_body,
        out_shape=jax.ShapeDtypeStruct((m, n), F32),
        grid=(m // tm,),
        in_specs=[pl.BlockSpec((tm, n), lambda i: (i, 0)), pl.BlockSpec((n, n), lambda i: (0, 0))],
        out_specs=pl.BlockSpec((tm, n), lambda i: (i, 0)),
        compiler_params=_cparams(("parallel",), 4 * tm * n * 4 / 2**20 + 12),
        name="ssm_glu",
    )(yg, w)


def _mix_body(attn_ref, ssm_ref, ga_ref, gs_ref, w_ref, x_ref, g_ref, b_ref, o_ref):
    a = _rmsnorm(attn_ref[...], ga_ref[...]).astype(BF16)
    s = _rmsnorm(ssm_ref[...], gs_ref[...]).astype(BF16)
    mix = _dot(a, w_ref[0:D_ATT, :]) + _dot(s, w_ref[D_ATT:D_ATT + D_SSM, :])
    o_ref[...] = _layernorm(DEEPNORM_ALPHA * x_ref[...] + mix, g_ref[...], b_ref[...])


def _mix(attn, ssm, ga, gs, w, x, g, b, *, tm):
    m = x.shape[0]
    row = lambda n: pl.BlockSpec((tm, n), lambda i: (i, 0))
    const = lambda a: pl.BlockSpec(a.shape, lambda i: (0, 0))
    return pl.pallas_call(
        _mix_body,
        out_shape=jax.ShapeDtypeStruct((m, D_MODEL), F32),
        grid=(m // tm,),
        in_specs=[row(D_ATT), row(D_SSM), const(ga), const(gs), const(w), row(D_MODEL),
                  const(g), const(b)],
        out_specs=row(D_MODEL),
        compiler_params=_cparams(("parallel",), 6 * tm * D_MODEL * 4 / 2**20 + 24),
        name="mix_out_ln1",
    )(attn, ssm, ga, gs, w, x, g, b)


def _mm_ln_body(a_ref, w_ref, x_ref, g_ref, b_ref, o_ref):
    y = _dot(a_ref[...].astype(BF16), w_ref[...])
    o_ref[...] = _layernorm(DEEPNORM_ALPHA * x_ref[...] + y, g_ref[...], b_ref[...])


def _mm_ln(a, w, x, g, b, *, tm, name):
    m = x.shape[0]
    row = lambda n: pl.BlockSpec((tm, n), lambda i: (i, 0))
    const = lambda v: pl.BlockSpec(v.shape, lambda i: (0, 0))
    return pl.pallas_call(
        _mm_ln_body,
        out_shape=jax.ShapeDtypeStruct((m, D_MODEL), F32),
        grid=(m // tm,),
        in_specs=[row(a.shape[1]), const(w), row(D_MODEL), const(g), const(b)],
        out_specs=row(D_MODEL),
        compiler_params=_cparams(("parallel",), 6 * tm * D_MODEL * 4 / 2**20 + 24),
        name=name,
    )(a, w, x, g, b)


def _memattn_body(q_ref, k_ref, v_ref, o_ref):
    scale = MEM_HD ** -0.5
    for h in range(MEM_HEADS):
        sl = slice(h * MEM_HD, (h + 1) * MEM_HD)
        s = _dot_nt(q_ref[:, sl].astype(BF16), k_ref[:, sl].astype(BF16)) * scale
        m = jnp.max(s, axis=-1, keepdims=True)
        p = jnp.exp(s - m)
        l = jnp.sum(p, axis=-1, keepdims=True)
        o_ref[:, sl] = _dot(p.astype(BF16), v_ref[:, sl].astype(BF16)) / l


def _memattn(q, mem_k, mem_v, *, row0, n_seq, seq, tq, name):
    assert seq % tq == 0 and row0 % tq == 0
    nq = seq // tq
    rb = row0 // tq
    mem_spec = pl.BlockSpec((None, N_MEM, D_MODEL), lambda b, i: (b, 0, 0))
    return pl.pallas_call(
        _memattn_body,
        out_shape=jax.ShapeDtypeStruct((n_seq * seq, D_MODEL), F32),
        grid=(n_seq, nq),
        in_specs=[pl.BlockSpec((tq, D_MODEL), lambda b, i: (rb + b * nq + i, 0)),
                  mem_spec, mem_spec],
        out_specs=pl.BlockSpec((tq, D_MODEL), lambda b, i: (b * nq + i, 0)),
        compiler_params=_cparams(("parallel", "parallel"),
                                 4 * (tq + N_MEM) * D_MODEL * 4 / 2**20 + 8),
        name=name,
    )(q, mem_k, mem_v)


def _memattn_heads_body(q_ref, k_ref, v_ref, c_ref, o_ref):
    scale = MEM_HD ** -0.5
    tq = q_ref.shape[0]
    q = jnp.concatenate([q_ref[:, h * MEM_HD:(h + 1) * MEM_HD] for h in range(MEM_HEADS)], axis=0)
    k = k_ref[...].reshape(N_MEM * MEM_HEADS, MEM_HD).astype(BF16)
    v = v_ref[...].reshape(N_MEM * MEM_HEADS, MEM_HD).astype(BF16)
    s = jnp.where(c_ref[...] > 0, _dot_nt(q.astype(BF16), k) * scale, NEG_INF)
    m = jnp.max(s, axis=-1, keepdims=True)
    p = jnp.exp(s - m)
    l = jnp.sum(p, axis=-1, keepdims=True)
    o = _dot(p.astype(BF16), v) / l
    for h in range(MEM_HEADS):
        o_ref[:, h * MEM_HD:(h + 1) * MEM_HD] = o[h * tq:(h + 1) * tq, :]


def _memattn_heads(q, mem_k, mem_v, *, row0, n_seq, seq, name):
    assert row0 % seq == 0 and seq % SUBLANES == 0
    rb = row0 // seq
    same_head = np.kron(np.eye(MEM_HEADS, dtype=np.float32), np.ones((seq, 1), np.float32))
    same_head = np.tile(same_head, (1, N_MEM))
    mem_spec = pl.BlockSpec((None, N_MEM, MEM_HEADS, MEM_HD), lambda b: (b, 0, 0, 0))
    return pl.pallas_call(
        _memattn_heads_body,
        out_shape=jax.ShapeDtypeStruct((n_seq * seq, D_MODEL), F32),
        grid=(n_seq,),
        in_specs=[pl.BlockSpec((seq, D_MODEL), lambda b: (rb + b, 0)), mem_spec, mem_spec,
                  pl.BlockSpec(same_head.shape, lambda b: (0, 0))],
        out_specs=pl.BlockSpec((seq, D_MODEL), lambda b: (b, 0)),
        compiler_params=_cparams(("parallel",), 8 * N_MEM * D_MODEL * 4 / 2**20 + 8),
        name=name,
    )(q, mem_k, mem_v, jnp.asarray(same_head))


def _router_body(x_ref, w_ref, b_ref, sel_ref, wts_ref, cnt_ref, run_s, *, tm):
    i = pl.program_id(0)

    @pl.when(i == 0)
    def _():
        run_s[...] = jnp.zeros_like(run_s)

    ng, epg = N_EXPERT_GROUPS, EXPERTS_PER_GROUP
    logits = jnp.dot(x_ref[...], w_ref[...], precision=lax.Precision.HIGHEST,
                     preferred_element_type=F32) + b_ref[...]
    lane = lax.broadcasted_iota(I32, (tm, ROUTER_LANES), 1)
    big = ROUTER_LANES

    def first_argmax(vals):
        mx = jnp.max(vals, axis=-1, keepdims=True)
        idx = jnp.min(jnp.where(vals == mx, lane, big), axis=-1, keepdims=True)
        return mx, idx

    gl = jnp.where(lane < ng, logits, NEG_INF)
    gmax, gsel = first_argmax(gl)
    g_w = 1.0 / jnp.sum(jnp.exp(gl - gmax), axis=-1, keepdims=True)
    lo = ng + gsel * epg
    el = jnp.where(jnp.logical_and(lane >= lo, lane < lo + epg), logits, NEG_INF)
    v1, i1 = first_argmax(el)
    v2, i2 = first_argmax(jnp.where(lane == i1, NEG_INF, el))
    e21 = jnp.exp(v2 - v1)
    w1 = g_w / (1.0 + e21)
    w2 = g_w * e21 / (1.0 + e21)

    onehot = jnp.logical_or(lane == i1, lane == i2)
    r = lax.broadcasted_iota(I32, (tm, tm), 0)
    cc = lax.broadcasted_iota(I32, (tm, tm), 1)
    tri = (cc < r).astype(BF16)
    before = _dot(tri, onehot.astype(BF16)) + run_s[...]
    rank1 = jnp.sum(jnp.where(lane == i1, before, 0.0), axis=-1, keepdims=True).astype(I32)
    rank2 = jnp.sum(jnp.where(lane == i2, before, 0.0), axis=-1, keepdims=True).astype(I32)
    run_s[...] = run_s[...] + jnp.sum(onehot.astype(F32), axis=0, keepdims=True)

    sel = jnp.where(lane == 0, i1 - ng, jnp.where(lane == 1, i2 - ng,
                    jnp.where(lane == 2, rank1, jnp.where(lane == 3, rank2, 0))))
    sel_ref[...] = sel
    wts_ref[...] = jnp.where(lane == 0, w1, jnp.where(lane == 1, w2, 0.0))
    cnt_ref[...] = run_s[...].astype(I32)


def _router(x, w, b, *, tm):
    m = x.shape[0]
    row = pl.BlockSpec((tm, ROUTER_LANES), lambda i: (i, 0))
    return pl.pallas_call(
        functools.partial(_router_body, tm=tm),
        out_shape=[jax.ShapeDtypeStruct((m, ROUTER_LANES), I32),
                   jax.ShapeDtypeStruct((m, ROUTER_LANES), F32),
                   jax.ShapeDtypeStruct((1, ROUTER_LANES), I32)],
        grid=(m // tm,),
        in_specs=[pl.BlockSpec((tm, D_MODEL), lambda i: (i, 0)),
                  pl.BlockSpec((D_MODEL, ROUTER_LANES), lambda i: (0, 0)),
                  pl.BlockSpec((1, ROUTER_LANES), lambda i: (0, 0))],
        out_specs=[row, row, pl.BlockSpec((1, ROUTER_LANES), lambda i: (0, 0))],
        scratch_shapes=[pltpu.VMEM((1, ROUTER_LANES), F32)],
        compiler_params=_cparams(("arbitrary",), 16),
        name="moe_router",
    )(x, w, b)


def _start_row_gather(src_hbm, idx, buf, r, sem):
    pltpu.make_async_copy(src_hbm.at[idx], buf.at[r // SUBLANES, :, pl.ds(r % SUBLANES, 1), :],
                          sem).start()


def _wait_row_gathers(buf, other, sem):
    pltpu.make_async_copy(other, buf, sem).wait()


def _gathered_tile(buf):
    rows = buf.shape[0] * SUBLANES
    return jnp.concatenate([buf[:, c].reshape(rows, LANES) for c in range(ROW_CHUNKS)], axis=1)


def _store_row_chunks(o_ref, y):
    for c in range(ROW_CHUNKS):
        o_ref[:, c, 0, :] = y[:, c * LANES:(c + 1) * LANES]


def _moe_body(te_ref, src_ref, nact_ref, x_hbm, wg_ref, wu_ref, wd_ref, o_ref,
              xbuf, sem, wg_s, wu_s, wd_s):
    i = pl.program_id(0)
    nact = nact_ref[0]
    tm = MOE_TILE
    slot = i % 2

    def issue_gather(tile, slot_):
        base = tile * tm
        for r in range(tm):
            _start_row_gather(x_hbm, src_ref[base + r], xbuf.at[slot_], r, sem.at[slot_])

    def tile_step(prefetch):
        _wait_row_gathers(xbuf.at[slot], xbuf.at[1 - slot], sem.at[slot])
        prev = te_ref[jnp.maximum(i - 1, 0)]

        @pl.when(jnp.logical_or(i == 0, te_ref[i] != prev))
        def _():
            wg_s[...] = wg_ref[...].astype(BF16)
            wu_s[...] = wu_ref[...].astype(BF16)
            wd_s[...] = wd_ref[...].astype(BF16)

        if prefetch:
            issue_gather(i + 1, 1 - slot)
        x = _gathered_tile(xbuf.at[slot]).astype(BF16)
        hg = _dot(x, wg_s[...])
        hu = _dot(x, wu_s[...])
        h = hg * (1.0 / (1.0 + jnp.exp(-hg))) * hu
        _store_row_chunks(o_ref, _dot(h.astype(BF16), wd_s[...]))

    @pl.when(i == 0)
    def _():
        issue_gather(0, 0)

    @pl.when(i + 1 < nact)
    def _():
        tile_step(True)

    @pl.when(i + 1 == nact)
    def _():
        tile_step(False)

    @pl.when(i >= nact)
    def _():
        o_ref[...] = jnp.zeros_like(o_ref)


def _moe_experts(x_rows, w_gate, w_up, w_down, tile_expert, src, nact, *, n_tiles):
    tm = MOE_TILE
    grid_spec = pltpu.PrefetchScalarGridSpec(
        num_scalar_prefetch=3,
        grid=(n_tiles,),
        in_specs=[pl.BlockSpec(memory_space=pl.ANY),
                  pl.BlockSpec((None, D_MODEL, D_EXPERT), lambda i, te, s, n: (te[i], 0, 0)),
                  pl.BlockSpec((None, D_MODEL, D_EXPERT), lambda i, te, s, n: (te[i], 0, 0)),
                  pl.BlockSpec((None, D_EXPERT, D_MODEL), lambda i, te, s, n: (te[i], 0, 0))],
        out_specs=pl.BlockSpec((tm, ROW_CHUNKS, 1, LANES), lambda i, te, s, n: (i, 0, 0, 0)),
        scratch_shapes=[pltpu.VMEM((2, tm // SUBLANES, ROW_CHUNKS, SUBLANES, LANES), F32),
                        pltpu.SemaphoreType.DMA((2,)),
                        pltpu.VMEM((D_MODEL, D_EXPERT), BF16),
                        pltpu.VMEM((D_MODEL, D_EXPERT), BF16),
                        pltpu.VMEM((D_EXPERT, D_MODEL), BF16)],
    )
    return pl.pallas_call(
        _moe_body,
        out_shape=jax.ShapeDtypeStruct((n_tiles * tm, ROW_CHUNKS, 1, LANES), F32),
        grid_spec=grid_spec,
        compiler_params=_cparams(("arbitrary",), 48),
        name="moe_experts",
    )(tile_expert, src, nact, x_rows, w_gate, w_up, w_down)


def _combine_body(pos_ref, ys_hbm, wts_ref, x_ref, g_ref, b_ref, o_ref, buf, sem, *, tc):
    i = pl.program_id(0)
    n = pl.num_programs(0)
    slot = i % 2

    def issue_gather(tile, slot_):
        base = tile * tc * 2
        for r in range(tc):
            for k in range(2):
                _start_row_gather(ys_hbm, pos_ref[base + 2 * r + k], buf.at[slot_, k], r,
                                  sem.at[slot_])

    @pl.when(i == 0)
    def _():
        issue_gather(0, 0)

    for k in range(2):
        _wait_row_gathers(buf.at[slot, k], buf.at[1 - slot, k], sem.at[slot])

    @pl.when(i + 1 < n)
    def _():
        issue_gather(i + 1, 1 - slot)

    w = wts_ref[...]
    moe = w[:, 0:1] * _gathered_tile(buf.at[slot, 0]) + w[:, 1:2] * _gathered_tile(buf.at[slot, 1])
    o_ref[...] = _layernorm(DEEPNORM_ALPHA * x_ref[...] + moe, g_ref[...], b_ref[...])


def _moe_combine(ys, pos, wts, x, g, b, *, tc):
    m = x.shape[0]
    grid_spec = pltpu.PrefetchScalarGridSpec(
        num_scalar_prefetch=1,
        grid=(m // tc,),
        in_specs=[pl.BlockSpec(memory_space=pl.ANY),
                  pl.BlockSpec((tc, ROUTER_LANES), lambda i, p: (i, 0)),
                  pl.BlockSpec((tc, D_MODEL), lambda i, p: (i, 0)),
                  pl.BlockSpec((1, D_MODEL), lambda i, p: (0, 0)),
                  pl.BlockSpec((1, D_MODEL), lambda i, p: (0, 0))],
        out_specs=pl.BlockSpec((tc, D_MODEL), lambda i, p: (i, 0)),
        scratch_shapes=[pltpu.VMEM((2, 2, tc // SUBLANES, ROW_CHUNKS, SUBLANES, LANES), F32),
                        pltpu.SemaphoreType.DMA((2,))],
    )
    return pl.pallas_call(
        functools.partial(_combine_body, tc=tc),
        out_shape=jax.ShapeDtypeStruct((m, D_MODEL), F32),
        grid_spec=grid_spec,
        compiler_params=_cparams(("arbitrary",), 8 * tc * D_MODEL * 4 / 2**20 + 8),
        name="moe_combine_ln3",
    )(pos, ys, wts, x, g, b)


def _moe(x, w_r1, b_r1, w_r2, b_r2, w_gate, w_up, w_down, g, b, *, tm_router, tc):
    n = x.shape[0]
    ng, ne = N_EXPERT_GROUPS, N_EXPERTS
    pad = ROUTER_LANES - ng - ne
    w_r = jnp.concatenate([w_r1, w_r2.reshape(D_MODEL, ne), jnp.zeros((D_MODEL, pad), F32)], axis=1)
    b_r = jnp.concatenate([b_r1, b_r2.reshape(ne), jnp.zeros((pad,), F32)]).reshape(1, ROUTER_LANES)
    sel, wts, cnt = _router(x, w_r, b_r, tm=tm_router)

    tm = MOE_TILE
    n_tiles = (2 * n) // tm + ne
    counts = cnt[0, ng:ng + ne]
    tiles_per = (counts + tm - 1) // tm
    tile_end = jnp.cumsum(tiles_per)
    row_off = (tile_end - tiles_per) * tm
    nact = tile_end[-1]
    ids, ranks = sel[:, 0:2], sel[:, 2:4]
    pos = row_off[ids] + ranks
    tile_ids = jnp.minimum(jnp.arange(n_tiles, dtype=I32), nact - 1)
    tile_expert = jnp.sum((tile_end[None, :] <= tile_ids[:, None]).astype(I32), axis=1)
    token = jnp.broadcast_to(jnp.arange(n, dtype=I32)[:, None], (n, 2))
    src = jnp.zeros((n_tiles * tm,), I32).at[pos.reshape(-1)].set(token.reshape(-1))

    x_rows = x.reshape(n, ROW_CHUNKS, 1, LANES)
    ys = _moe_experts(x_rows, w_gate, w_up, w_down, tile_expert, src, nact.reshape(1).astype(I32),
                      n_tiles=n_tiles)
    return _moe_combine(ys, pos.reshape(-1).astype(I32), wts, x, g, b, tc=tc)


def _row_tile(m, cap):
    best = SUBLANES
    for t in range(SUBLANES, cap + 1, SUBLANES):
        if m % t == 0:
            best = t
    return best


def kernel(x_prompt, x_sample, cache_win_k, cache_win_v, state_ssm_re, state_ssm_im, cache_mem_k, cache_mem_v, mem_prompt, w_in, ssm_lam_re, ssm_lam_im, ssm_log_dt, ssm_b_re, ssm_b_im, ssm_c_re, ssm_c_im, ssm_d, w_glu, g_attn, g_ssm, w_out, ln1_g, ln1_b, w_mq, w_mk, w_mv, w_mo, ln2_g, ln2_b, w_r1, b_r1, w_r2, b_r2, w_gate, w_up, w_down, ln3_g, ln3_b):
    nb, seq, d = x_prompt.shape
    ns, dseq, _ = x_sample.shape
    n_p, n_s = nb * seq, ns * dseq
    n = n_p + n_s
    l = 0
    row2 = lambda v: v[l].reshape(1, -1)

    x_all = jnp.concatenate([x_prompt.reshape(n_p, d), x_sample.reshape(n_s, d)], axis=0)
    tm_big = _row_tile(n, 768)
    tm_ln = _row_tile(n, 384)

    proj = _matmul(x_all, w_in[l].astype(BF16), tm=tm_big, tn=1024, name="proj_in")

    attn_p = _attn_prompt(proj, n_batch=nb, seq=seq)
    attn_s = _attn_sample(proj, cache_win_k[l], cache_win_v[l], row0=n_p, n_seq=ns, n_new=dseq)
    attn = jnp.concatenate([attn_p, attn_s], axis=0)

    seg_len = seq // SSM_SEGMENTS
    prm = _ssm_params(ssm_lam_re[l], ssm_lam_im[l], ssm_log_dt[l], ssm_b_re[l], ssm_b_im[l],
                      ssm_c_re[l], ssm_c_im[l], ssm_d[l], seg_len)
    u_p = proj[:n_p, 3 * D_ATT:].reshape(nb, SSM_SEGMENTS, seg_len, D_SSM)
    u_p = jnp.transpose(u_p, (2, 0, 1, 3)).reshape(seg_len, nb * SSM_SEGMENTS, D_SSM)
    zeros = jnp.zeros((nb * SSM_SEGMENTS, N_SSM_GROUPS * SSM_STATE), F32)
    tl = _row_tile(seg_len, 32)
    end_re, end_im = _ssm_scan(u_p, prm, zeros, zeros, tl=tl, nseg=1, emit_y=False, exact_in=False)
    yg_p, fin_re, fin_im = _ssm_scan(u_p, prm, end_re, end_im, tl=tl, nseg=SSM_SEGMENTS,
                                     emit_y=True, exact_in=False)
    yg_p = jnp.transpose(yg_p.reshape(seg_len, nb, SSM_SEGMENTS, D_SSM), (1, 2, 0, 3))
    last = SSM_SEGMENTS - 1
    ssm_re_p = fin_re.reshape(nb, SSM_SEGMENTS, N_SSM_GROUPS, SSM_STATE)[:, last]
    ssm_im_p = fin_im.reshape(nb, SSM_SEGMENTS, N_SSM_GROUPS, SSM_STATE)[:, last]

    u_s = jnp.transpose(proj[n_p:, 3 * D_ATT:].reshape(ns, dseq, D_SSM), (1, 0, 2))
    h0_re = state_ssm_re[l].reshape(ns, -1)
    h0_im = state_ssm_im[l].reshape(ns, -1)
    yg_s, ssm_re_s, ssm_im_s = _ssm_scan(u_s, prm, h0_re, h0_im, tl=dseq, nseg=1,
                                         emit_y=True, exact_in=True)
    yg_s = jnp.transpose(yg_s, (1, 0, 2))
    yg = jnp.concatenate([yg_p.reshape(n_p, D_SSM), yg_s.reshape(n_s, D_SSM)], axis=0)
    ssm_out = _glu(yg, w_glu[l].astype(BF16), tm=tm_big)

    x1 = _mix(attn, ssm_out, row2(g_attn), row2(g_ssm), w_out[l].astype(BF16), x_all,
              row2(ln1_g), row2(ln1_b), tm=tm_ln)

    mem_rows = mem_prompt.reshape(nb * N_MEM, d)
    mem_k = _matmul(mem_rows, w_mk[l].astype(BF16), tm=nb * N_MEM, tn=1024, name="mem_k")
    mem_v = _matmul(mem_rows, w_mv[l].astype(BF16), tm=nb * N_MEM, tn=1024, name="mem_v")
    q_mem = _matmul(x1, w_mq[l].astype(BF16), tm=tm_big, tn=1024, name="mem_q")
    o_p = _memattn(q_mem, mem_k.reshape(nb, N_MEM, d), mem_v.reshape(nb, N_MEM, d),
                   row0=0, n_seq=nb, seq=seq, tq=_row_tile(seq, 512), name="memattn_prompt")
    o_s = _memattn_heads(q_mem, cache_mem_k[l], cache_mem_v[l], row0=n_p, n_seq=ns, seq=dseq,
                         name="memattn_sample")
    o_mem = jnp.concatenate([o_p, o_s], axis=0)
    x2 = _mm_ln(o_mem, w_mo[l].astype(BF16), x1, row2(ln2_g), row2(ln2_b), tm=tm_ln,
                name="mem_out_ln2")

    y = _moe(x2, w_r1[l], b_r1[l], w_r2[l], b_r2[l], w_gate[l], w_up[l], w_down[l],
             row2(ln3_g), row2(ln3_b), tm_router=_row_tile(n, 256), tc=_row_tile(n, 128))

    y_p = y[:n_p].reshape(nb, seq, d)
    y_s = y[n_p:].reshape(ns, dseq, d)
    k_p = proj[:n_p, D_ATT:2 * D_ATT].reshape(nb, seq, ATT_HEADS, ATT_HD)
    v_p = proj[:n_p, 2 * D_ATT:3 * D_ATT].reshape(nb, seq, ATT_HEADS, ATT_HD)
    wp = min(max(w for w, _ in DILATIONS), seq)
    k_s = proj[n_p:, D_ATT:2 * D_ATT].reshape(ns, dseq, ATT_HEADS, ATT_HD)
    v_s = proj[n_p:, 2 * D_ATT:3 * D_ATT].reshape(ns, dseq, ATT_HEADS, ATT_HD)
    state = lambda v, b_: v.reshape(1, b_, N_SSM_GROUPS, SSM_STATE)
    return (y_p, y_s, k_p[None, :, seq - wp:], v_p[None, :, seq - wp:], k_s[None], v_s[None],
            state(ssm_re_p, nb), state(ssm_im_p, nb), state(ssm_re_s, ns), state(ssm_im_s, ns),
            mem_k.reshape(1, nb, N_MEM, MEM_HEADS, MEM_HD),
            mem_v.reshape(1, nb, N_MEM, MEM_HEADS, MEM_HD))
```

```python
import functools
import math

import numpy as np
import jax
import jax.numpy as jnp
from jax import lax
from jax.experimental import pallas as pl
from jax.experimental.pallas import tpu as pltpu

F32 = jnp.float32
BF16 = jnp.bfloat16
I32 = jnp.int32

D_MODEL = 2048
PAST_LEN = 8192
D_ATT = D_MODEL // 2
ATT_HEADS = 8
ATT_HD = D_ATT // ATT_HEADS
DILATIONS = ((128, 1), (512, 4), (2048, 16))
D_SSM = D_MODEL - D_ATT
SSM_GROUP_CH = 16
N_SSM_GROUPS = D_SSM // SSM_GROUP_CH
SSM_STATE = 64
N_MEM = 256
MEM_HEADS = 4
MEM_HD = D_MODEL // MEM_HEADS
N_EXPERT_GROUPS = 4
EXPERTS_PER_GROUP = 8
N_EXPERTS = N_EXPERT_GROUPS * EXPERTS_PER_GROUP
D_EXPERT = D_MODEL // 4
DEPTH = 1
DEEPNORM_ALPHA = (2.0 * DEPTH) ** 0.25
LN_EPS = 1e-5
RMS_EPS = 1e-6

LANES = 128
SUBLANES = 8
ROW_CHUNKS = D_MODEL // LANES
ROW_PITCH = ROW_CHUNKS + 1
Q_BLOCK = 128
ATTN_GROUP = 8
SSM_LANE_TILE = 128
SSM_GROUPS_PER_TILE = SSM_LANE_TILE // SSM_GROUP_CH
SSM_STATES_PER_TILE = SSM_GROUPS_PER_TILE * SSM_STATE
SSM_SEGMENTS = 8
MOE_TILE = 256
ROUTER_LANES = 128
NEG_INF = float("-inf")


def _cparams(semantics, vmem_mib):
    return pltpu.CompilerParams(dimension_semantics=semantics,
                                vmem_limit_bytes=int(vmem_mib) << 20)


def _layernorm(y, g, b):
    mu = jnp.mean(y, axis=-1, keepdims=True)
    yc = y - mu
    var = jnp.mean(yc * yc, axis=-1, keepdims=True)
    return yc * lax.rsqrt(var + LN_EPS) * g + b


def _rmsnorm(v, g):
    return v * lax.rsqrt(jnp.mean(v * v, axis=-1, keepdims=True) + RMS_EPS) * g


def _dot(a, b):
    return jnp.dot(a, b, preferred_element_type=F32)


def _dot_nt(a, b):
    return lax.dot_general(a, b, (((1,), (1,)), ((), ())), preferred_element_type=F32)


def _mm_body(x_ref, w_ref, o_ref):
    o_ref[...] = _dot(x_ref[...].astype(BF16), w_ref[...]).astype(o_ref.dtype)


def _matmul(x, w, *, tm, tn, name):
    m, k = x.shape
    n = w.shape[1]
    vmem = 2 * (tm * k * x.dtype.itemsize + k * tn * 2 + tm * tn * 4) / 2**20 + 8
    return pl.pallas_call(
        _mm_body,
        out_shape=jax.ShapeDtypeStruct((m, n), F32),
        grid=(n // tn, m // tm),
        in_specs=[pl.BlockSpec((tm, k), lambda j, i: (i, 0)),
                  pl.BlockSpec((k, tn), lambda j, i: (0, j))],
        out_specs=pl.BlockSpec((tm, tn), lambda j, i: (i, j)),
        compiler_params=_cparams(("parallel", "parallel"), vmem),
        name=name,
    )(x, w)


def _attn_prompt_body(q_ref, k_ref, v_ref, o_ref, kt_s, va_s, on_s, lse_s, *, seq, dilations):
    scale = ATT_HD ** -0.5
    nblk = seq // Q_BLOCK
    qi = lax.broadcasted_iota(I32, (Q_BLOCK, Q_BLOCK), 0)
    kj = lax.broadcasted_iota(I32, (Q_BLOCK, Q_BLOCK), 1)
    cur_ok = kj <= qi
    prev_ok = kj >= qi
    va_s[:, :, ATT_HD:] = jnp.ones((nblk, Q_BLOCK, ATT_HD), BF16)

    for br, (_, d) in enumerate(dilations):
        span = d * Q_BLOCK
        nb = seq // span

        def stream_rows(t, d=d, span=span, nb=nb):
            r = t // nb
            ib = t % nb
            return r, ib, pl.ds(r + ib * span, Q_BLOCK, stride=d)

        def prep(g, carry, stream_rows=stream_rows):
            loaded = []
            for j in range(ATTN_GROUP):
                t = g * ATTN_GROUP + j
                _, _, rows = stream_rows(t)
                loaded.append((t, k_ref[rows, :], v_ref[rows, :]))
            for t, kk, vv in loaded:
                kt_s[t] = jnp.transpose(kk).astype(BF16)
                va_s[t, :, 0:ATT_HD] = vv.astype(BF16)
            return carry

        lax.fori_loop(0, nblk // ATTN_GROUP, prep, 0)

        def group(g, carry, br=br, nb=nb, stream_rows=stream_rows):
            scores = []
            for j in range(ATTN_GROUP):
                t = g * ATTN_GROUP + j
                r, ib, rows = stream_rows(t)
                tp = jnp.maximum(t - 1, r * nb)
                q = (q_ref[rows, :] * scale).astype(BF16)
                s = _dot(q, jnp.concatenate([kt_s[tp], kt_s[t]], axis=1))
                scores.append((t, tp, ib, rows, s))
            probs = []
            for t, tp, ib, rows, s in scores:
                ok = jnp.concatenate([jnp.logical_and(prev_ok, ib > 0), cur_ok], axis=1)
                s = jnp.where(ok, s, NEG_INF)
                m = jnp.max(s, axis=-1, keepdims=True)
                probs.append((t, tp, rows, m, jnp.exp(s - m).astype(BF16)))
            outs = [(rows, m, _dot(p, jnp.concatenate([va_s[tp], va_s[t]], axis=0)))
                    for t, tp, rows, m, p in probs]
            for rows, m, al in outs:
                l = al[:, ATT_HD:]
                on_s[br, rows, :] = al[:, :ATT_HD] / l
                lse_s[br, rows, :] = m + jnp.log(l)
            return carry

        lax.fori_loop(0, nblk // ATTN_GROUP, group, 0)

    chunk = 256
    nbr = len(dilations)

    def merge(c, carry):
        rows = pl.ds(pl.multiple_of(c * chunk, chunk), chunk)
        ls = [lse_s[b, rows, :] for b in range(nbr)]
        mx = functools.reduce(jnp.maximum, ls)
        es = [jnp.exp(li - mx) for li in ls]
        num = sum(es[b] * on_s[b, rows, :] for b in range(nbr))
        o_ref[rows, :] = num / sum(es)
        return carry

    lax.fori_loop(0, seq // chunk, merge, 0)


def _attn_prompt(proj, *, n_batch, seq, dilations=DILATIONS):
    for w, d in dilations:
        assert w // d == Q_BLOCK and seq % (d * Q_BLOCK) == 0
    nbr = len(dilations)
    nblk = seq // Q_BLOCK
    assert nblk % ATTN_GROUP == 0
    blk = lambda off: pl.BlockSpec((seq, ATT_HD), lambda b, h, off=off: (b, off + h))
    vmem = ((4 * 2 + 2 * nbr) * seq * ATT_HD * 4 + 3 * seq * ATT_HD * 2) / 2**20 + 8
    return pl.pallas_call(
        functools.partial(_attn_prompt_body, seq=seq, dilations=dilations),
        out_shape=jax.ShapeDtypeStruct((n_batch * seq, D_ATT), F32),
        grid=(n_batch, ATT_HEADS),
        in_specs=[blk(0), blk(ATT_HEADS), blk(2 * ATT_HEADS)],
        out_specs=pl.BlockSpec((seq, ATT_HD), lambda b, h: (b, h)),
        scratch_shapes=[pltpu.VMEM((nblk, ATT_HD, Q_BLOCK), BF16),
                        pltpu.VMEM((nblk, Q_BLOCK, 2 * ATT_HD), BF16),
                        pltpu.VMEM((nbr, seq, ATT_HD), F32),
                        pltpu.VMEM((nbr, seq, ATT_HD), F32)],
        compiler_params=_cparams(("parallel", "parallel"), vmem),
        name="attn_prompt",
    )(proj, proj, proj)


def _sample_key_multiplicity(n_new, n_cache, past_len, dilations):
    d_max = max(d for _, d in dilations)
    tail = max(w for w, d in dilations if d != d_max)
    assert past_len % d_max == 0 and n_cache % d_max == 0 and n_new <= d_max // 2
    assert tail % d_max == 0 and tail <= n_cache
    half = d_max // 2
    n_grid = (n_cache - tail) // d_max
    kv_start = past_len - n_cache
    grid_rows = (np.arange(n_grid)[:, None] * d_max + np.arange(half)[None, :]).reshape(-1)
    tail_rows = n_cache - tail + np.arange(tail)
    new_rows = n_cache + np.arange(n_new)
    qpos = past_len + np.arange(n_new)

    def mult(rows):
        kpos = kv_start + rows
        delta = qpos[:, None] - kpos[None, :]
        c = np.zeros(delta.shape, np.float32)
        for w, d in dilations:
            c += ((delta >= 0) & (delta <= w) & (delta % d == 0) & (kpos[None, :] >= kv_start))
        return c

    fetched = np.zeros(n_cache + n_new, bool)
    fetched[grid_rows] = True
    fetched[tail_rows] = True
    fetched[new_rows] = True
    assert not mult(np.nonzero(~fetched)[0]).any()
    return mult(grid_rows), mult(tail_rows), mult(new_rows), n_grid, tail, half, d_max


def _attn_sample_body(q_ref, kn_ref, vn_ref, kg_ref, kt_ref, vg_ref, vt_ref,
                      cg_ref, ct_ref, cn_ref, o_ref):
    scale = ATT_HD ** -0.5
    heads = lambda ref: jnp.concatenate(
        [ref[:, h * ATT_HD:(h + 1) * ATT_HD] for h in range(ATT_HEADS)], axis=0)
    q = (heads(q_ref) * scale).astype(BF16)
    kn = heads(kn_ref).astype(BF16)
    vn = heads(vn_ref).astype(BF16)
    flat = lambda ref: ref[...].reshape(-1, ATT_HD).astype(BF16)
    cg, ct, cn = cg_ref[...], ct_ref[...], cn_ref[...]
    sg = jnp.where(cg > 0, _dot_nt(q, flat(kg_ref)), NEG_INF)
    st = jnp.where(ct > 0, _dot_nt(q, flat(kt_ref)), NEG_INF)
    sn = jnp.where(cn > 0, _dot_nt(q, kn), NEG_INF)
    m = jnp.maximum(jnp.maximum(jnp.max(sg, axis=-1, keepdims=True),
                                jnp.max(st, axis=-1, keepdims=True)),
                    jnp.max(sn, axis=-1, keepdims=True))
    pg = cg * jnp.exp(sg - m)
    pt = ct * jnp.exp(st - m)
    pn = cn * jnp.exp(sn - m)
    l = (jnp.sum(pg, axis=-1, keepdims=True) + jnp.sum(pt, axis=-1, keepdims=True)
         + jnp.sum(pn, axis=-1, keepdims=True))
    acc = (_dot(pg.astype(BF16), flat(vg_ref)) + _dot(pt.astype(BF16), flat(vt_ref))
           + _dot(pn.astype(BF16), vn))
    out = acc / l
    n_new = q_ref.shape[0]
    for h in range(ATT_HEADS):
        o_ref[:, h * ATT_HD:(h + 1) * ATT_HD] = out[h * n_new:(h + 1) * n_new, :]


def _attn_sample(proj, win_k, win_v, *, row0, n_seq, n_new, past_len=PAST_LEN,
                 dilations=DILATIONS):
    n_cache = win_k.shape[1]
    cg, ct, cn, n_grid, tail, half, d_max = _sample_key_multiplicity(
        n_new, n_cache, past_len, dilations)
    assert row0 % n_new == 0 and n_new % SUBLANES == 0 and n_cache % tail == 0
    eye = np.eye(ATT_HEADS, dtype=np.float32)
    key_major = lambda c: np.einsum("tk,hg->htkg", c, eye).reshape(ATT_HEADS * n_new, -1)
    head_major = lambda c: np.einsum("tk,hg->htgk", c, eye).reshape(ATT_HEADS * n_new, -1)
    cg, ct, cn = key_major(cg), key_major(ct), head_major(cn)
    rb = row0 // n_new
    n_groups = n_cache // d_max
    kgv = win_k.reshape(n_seq, n_groups, d_max, ATT_HEADS, ATT_HD)
    vgv = win_v.reshape(n_seq, n_groups, d_max, ATT_HEADS, ATT_HD)
    ktv = win_k.reshape(n_seq, n_cache // tail, tail, ATT_HEADS, ATT_HD)
    vtv = win_v.reshape(n_seq, n_cache // tail, tail, ATT_HEADS, ATT_HD)
    new = lambda off: pl.BlockSpec((n_new, D_ATT), lambda b, off=off: (rb + b, off))
    grid_spec = pl.BlockSpec((None, n_grid, half, ATT_HEADS, ATT_HD), lambda b: (b, 0, 0, 0, 0))
    tail_spec = pl.BlockSpec((None, None, tail, ATT_HEADS, ATT_HD),
                             lambda b: (b, n_cache // tail - 1, 0, 0, 0))
    const = lambda a: pl.BlockSpec(a.shape, lambda b: (0, 0))
    vmem = (2 * 2 * (n_grid * half + tail) * D_ATT * 4 + 4 * cg.size * 4 * 3) / 2**20 + 12
    return pl.pallas_call(
        _attn_sample_body,
        out_shape=jax.ShapeDtypeStruct((n_seq * n_new, D_ATT), F32),
        grid=(n_seq,),
        in_specs=[new(0), new(1), new(2), grid_spec, tail_spec, grid_spec, tail_spec,
                  const(cg), const(ct), const(cn)],
        out_specs=pl.BlockSpec((n_new, D_ATT), lambda b: (b, 0)),
        compiler_params=_cparams(("parallel",), vmem),
        name="attn_sample",
    )(proj, proj, proj, kgv, ktv, vgv, vtv, jnp.asarray(cg), jnp.asarray(ct), jnp.asarray(cn))


def _gelu_tanh(x):
    return 0.5 * x * (1.0 + jnp.tanh(math.sqrt(2.0 / math.pi) * (x + 0.044715 * (x * x * x))))


def _ssm_body(u_ref, bb_ref, cst_ref, a_ref, ap_ref, d_ref, hre_ref, him_ref, *rest,
              tl, npar, nseg, emit_y, exact_in):
    if emit_y:
        y_ref, fre_ref, fim_ref, x_s, h_s = rest
    else:
        fre_ref, fim_ref, x_s, h_s = rest
    ns = SSM_STATES_PER_TILE
    c = pl.program_id(1)

    @pl.when(c == 0)
    def _init():
        if nseg == 1:
            h_s[0] = hre_ref[...]
            h_s[1] = him_ref[...]
        else:
            pr, pi = ap_ref[0:1, :], ap_ref[1:2, :]
            for b in range(npar // nseg):
                sr = jnp.zeros((1, ns), F32)
                si = jnp.zeros((1, ns), F32)
                for j in range(nseg):
                    row = b * nseg + j
                    h_s[0, row:row + 1, :] = sr
                    h_s[1, row:row + 1, :] = si
                    er, ei = hre_ref[row:row + 1, :], him_ref[row:row + 1, :]
                    sr, si = pr * sr - pi * si + er, pr * si + pi * sr + ei

    u = u_ref[...].reshape(tl * npar, SSM_LANE_TILE)
    if exact_in:
        x_s[...] = jnp.dot(u, bb_ref[...], precision=lax.Precision.HIGHEST,
                           preferred_element_type=F32)
    else:
        x_s[...] = _dot(u.astype(BF16), bb_ref[...])

    ar = jnp.broadcast_to(a_ref[0:1, :], (SUBLANES, ns))
    ai = jnp.broadcast_to(a_ref[1:2, :], (SUBLANES, ns))
    ngrp = npar // SUBLANES

    def step(i, carry):
        out = []
        for g in range(ngrp):
            hr, hi = carry[2 * g], carry[2 * g + 1]
            rows = pl.ds(pl.multiple_of(i * npar + g * SUBLANES, SUBLANES), SUBLANES)
            nr = ar * hr - ai * hi + x_s[rows, 0:ns]
            ni = ar * hi + ai * hr + x_s[rows, ns:2 * ns]
            if emit_y:
                x_s[rows, 0:ns] = nr
                x_s[rows, ns:2 * ns] = ni
            out += [nr, ni]
        return tuple(out)

    init = []
    for g in range(ngrp):
        gs = slice(g * SUBLANES, (g + 1) * SUBLANES)
        init += [h_s[0, gs, :], h_s[1, gs, :]]
    fin = lax.fori_loop(0, tl, step, tuple(init), unroll=4)
    for g in range(ngrp):
        gs = slice(g * SUBLANES, (g + 1) * SUBLANES)
        h_s[0, gs, :] = fin[2 * g]
        h_s[1, gs, :] = fin[2 * g + 1]

    if emit_y:
        y = _dot(x_s[...].astype(BF16), cst_ref[...]) + d_ref[...] * u
        y_ref[...] = _gelu_tanh(y).reshape(tl, npar, SSM_LANE_TILE)

    @pl.when(c == pl.num_programs(1) - 1)
    def _fin():
        fre_ref[...] = h_s[0]
        fim_ref[...] = h_s[1]


def _ssm_scan(u3, prm, hin_re, hin_im, *, tl, nseg, emit_y, exact_in):
    n_steps, npar, d_ssm = u3.shape
    ns = SSM_STATES_PER_TILE
    nk = d_ssm // SSM_LANE_TILE
    bb = prm["bb_f32"] if exact_in else prm["bb_bf16"]
    in_specs = [
        pl.BlockSpec((tl, npar, SSM_LANE_TILE), lambda k, c: (c, 0, k)),
        pl.BlockSpec((None, SSM_LANE_TILE, 2 * ns), lambda k, c: (k, 0, 0)),
        pl.BlockSpec((None, 2 * ns, SSM_LANE_TILE), lambda k, c: (k, 0, 0)),
        pl.BlockSpec((None, 2, ns), lambda k, c: (k, 0, 0)),
        pl.BlockSpec((None, 2, ns), lambda k, c: (k, 0, 0)),
        pl.BlockSpec((1, SSM_LANE_TILE), lambda k, c: (0, k)),
        pl.BlockSpec((npar, ns), lambda k, c: (0, k)),
        pl.BlockSpec((npar, ns), lambda k, c: (0, k)),
    ]
    state_shape = jax.ShapeDtypeStruct((npar, nk * ns), F32)
    state_spec = pl.BlockSpec((npar, ns), lambda k, c: (0, k))
    out_shape = [state_shape, state_shape]
    out_specs = [state_spec, state_spec]
    if emit_y:
        out_shape = [jax.ShapeDtypeStruct(u3.shape, F32)] + out_shape
        out_specs = [pl.BlockSpec((tl, npar, SSM_LANE_TILE), lambda k, c: (c, 0, k))] + out_specs
    return pl.pallas_call(
        functools.partial(_ssm_body, tl=tl, npar=npar, nseg=nseg, emit_y=emit_y,
                          exact_in=exact_in),
        out_shape=out_shape,
        grid=(nk, n_steps // tl),
        in_specs=in_specs,
        out_specs=out_specs,
        scratch_shapes=[pltpu.VMEM((tl * npar, 2 * ns), F32), pltpu.VMEM((2, npar, ns), F32)],
        compiler_params=_cparams(("parallel", "arbitrary"), 32),
        name="ssm_scan_y" if emit_y else "ssm_scan_state",
    )(u3, bb, prm["cst"], prm["a"], prm["apow"], prm["d"], hin_re, hin_im)


def _ssm_params(lam_re, lam_im, log_dt, b_re, b_im, c_re, c_im, d_skip, seg_len):
    g, p, c = N_SSM_GROUPS, SSM_STATE, SSM_GROUP_CH
    nk, gt = g // SSM_GROUPS_PER_TILE, SSM_GROUPS_PER_TILE
    dt = jnp.exp(log_dt.astype(F32))[:, None]
    lr, li = lam_re.astype(F32), lam_im.astype(F32)
    mag = jnp.exp(lr * dt)
    a_re, a_im = mag * jnp.cos(li * dt), mag * jnp.sin(li * dt)
    magp = jnp.exp(lr * dt * seg_len)
    p_re, p_im = magp * jnp.cos(li * dt * seg_len), magp * jnp.sin(li * dt * seg_len)
    den = lr * lr + li * li
    nr, ni = a_re - 1.0, a_im
    f_re, f_im = (nr * lr + ni * li) / den, (ni * lr - nr * li) / den
    br, bi = b_re.astype(F32), b_im.astype(F32)
    bb_re = f_re[..., None] * br - f_im[..., None] * bi
    bb_im = f_re[..., None] * bi + f_im[..., None] * br
    eye = jnp.eye(gt, dtype=F32)

    def pack_b(m):
        return jnp.einsum("kgpc,gh->kgchp", m.reshape(nk, gt, p, c), eye).reshape(nk, gt * c, gt * p)

    def pack_c(m):
        return jnp.einsum("kgcp,gh->kgphc", m.reshape(nk, gt, c, p), eye).reshape(nk, gt * p, gt * c)

    bb = jnp.concatenate([pack_b(bb_re), pack_b(bb_im)], axis=2)
    cst = jnp.concatenate([pack_c(c_re.astype(F32)), -pack_c(c_im.astype(F32))], axis=1)
    tile = lambda v: v.reshape(nk, 1, gt * p)
    return {
        "bb_f32": bb, "bb_bf16": bb.astype(BF16), "cst": cst.astype(BF16),
        "a": jnp.concatenate([tile(a_re), tile(a_im)], axis=1),
        "apow": jnp.concatenate([tile(p_re), tile(p_im)], axis=1),
        "d": d_skip.astype(F32).reshape(1, g * c),
    }


def _glu_body(y_ref, w_ref, o_ref):
    yg = y_ref[...]
    z = _dot(yg.astype(BF16), w_ref[...])
    o_ref[...] = yg * (1.0 / (1.0 + jnp.exp(-z)))


def _glu(yg, w, *, tm):
    m, n = yg.shape
    return pl.pallas_call(
        _glu_body,
        out_shape=jax.ShapeDtypeStruct((m, n), F32),
        grid=(m // tm,),
        in_specs=[pl.BlockSpec((tm, n), lambda i: (i, 0)), pl.BlockSpec((n, n), lambda i: (0, 0))],
        out_specs=pl.BlockSpec((tm, n), lambda i: (i, 0)),
        compiler_params=_cparams(("parallel",), 4 * tm * n * 4 / 2**20 + 12),
        name="ssm_glu",
    )(yg, w)


def _mix_body(attn_ref, ssm_ref, ga_ref, gs_ref, w_ref, x_ref, g_ref, b_ref, o_ref):
    a = _rmsnorm(attn_ref[...], ga_ref[...]).astype(BF16)
    s = _rmsnorm(ssm_ref[...], gs_ref[...]).astype(BF16)
    mix = _dot(a, w_ref[0:D_ATT, :]) + _dot(s, w_ref[D_ATT:D_ATT + D_SSM, :])
    o_ref[...] = _layernorm(DEEPNORM_ALPHA * x_ref[...] + mix, g_ref[...], b_ref[...])


def _mix(attn, ssm, ga, gs, w, x, g, b, *, tm):
    m = x.shape[0]
    row = lambda n: pl.BlockSpec((tm, n), lambda i: (i, 0))
    const = lambda a: pl.BlockSpec(a.shape, lambda i: (0, 0))
    return pl.pallas_call(
        _mix_body,
        out_shape=jax.ShapeDtypeStruct((m, D_MODEL), F32),
        grid=(m // tm,),
        in_specs=[row(D_ATT), row(D_SSM), const(ga), const(gs), const(w), row(D_MODEL),
                  const(g), const(b)],
        out_specs=row(D_MODEL),
        compiler_params=_cparams(("parallel",), 6 * tm * D_MODEL * 4 / 2**20 + 24),
        name="mix_out_ln1",
    )(attn, ssm, ga, gs, w, x, g, b)


def _store_gatherable(o_ref, y):
    rows = y.shape[0]
    for c in range(ROW_CHUNKS):
        o_ref[pl.ds(c, rows, stride=ROW_PITCH), :] = y[:, c * LANES:(c + 1) * LANES]
    for c in range(ROW_CHUNKS, ROW_PITCH):
        o_ref[pl.ds(c, rows, stride=ROW_PITCH), :] = jnp.zeros((rows, LANES), F32)


def _load_gathered(buf, rows):
    return jnp.concatenate([buf[pl.ds(c, rows, stride=ROW_PITCH), :] for c in range(ROW_CHUNKS)],
                           axis=1)


def _start_row_gather(src_hbm, idx, buf, r, sem):
    pltpu.make_async_copy(src_hbm.at[pl.ds(idx * ROW_PITCH, ROW_CHUNKS), :],
                          buf.at[pl.ds(r * ROW_PITCH, ROW_CHUNKS), :], sem).start()


def _wait_row_gathers(buf, other, rows, sem):
    span = pl.ds(0, rows * ROW_CHUNKS)
    pltpu.make_async_copy(other.at[span, :], buf.at[span, :], sem).wait()


def _mm_ln_body(a_ref, w_ref, x_ref, g_ref, b_ref, o_ref, rows_ref):
    y = _dot(a_ref[...].astype(BF16), w_ref[...])
    out = _layernorm(DEEPNORM_ALPHA * x_ref[...] + y, g_ref[...], b_ref[...])
    o_ref[...] = out
    _store_gatherable(rows_ref, out)


def _mm_ln(a, w, x, g, b, *, tm, name):
    m = x.shape[0]
    row = lambda n: pl.BlockSpec((tm, n), lambda i: (i, 0))
    const = lambda v: pl.BlockSpec(v.shape, lambda i: (0, 0))
    return pl.pallas_call(
        _mm_ln_body,
        out_shape=[jax.ShapeDtypeStruct((m, D_MODEL), F32),
                   jax.ShapeDtypeStruct((m * ROW_PITCH, LANES), F32)],
        grid=(m // tm,),
        in_specs=[row(a.shape[1]), const(w), row(D_MODEL), const(g), const(b)],
        out_specs=[row(D_MODEL), pl.BlockSpec((tm * ROW_PITCH, LANES), lambda i: (i, 0))],
        compiler_params=_cparams(("parallel",), 8 * tm * D_MODEL * 4 / 2**20 + 24),
        name=name,
    )(a, w, x, g, b)


def _memattn_body(q_ref, k_ref, v_ref, o_ref):
    scale = MEM_HD ** -0.5
    for h in range(MEM_HEADS):
        sl = slice(h * MEM_HD, (h + 1) * MEM_HD)
        s = _dot_nt(q_ref[:, sl].astype(BF16), k_ref[:, sl].astype(BF16)) * scale
        m = jnp.max(s, axis=-1, keepdims=True)
        p = jnp.exp(s - m)
        l = jnp.sum(p, axis=-1, keepdims=True)
        o_ref[:, sl] = _dot(p.astype(BF16), v_ref[:, sl].astype(BF16)) / l


def _memattn(q, mem_k, mem_v, *, row0, n_seq, seq, tq, name):
    assert seq % tq == 0 and row0 % tq == 0
    nq = seq // tq
    rb = row0 // tq
    mem_spec = pl.BlockSpec((None, N_MEM, D_MODEL), lambda b, i: (b, 0, 0))
    return pl.pallas_call(
        _memattn_body,
        out_shape=jax.ShapeDtypeStruct((n_seq * seq, D_MODEL), F32),
        grid=(n_seq, nq),
        in_specs=[pl.BlockSpec((tq, D_MODEL), lambda b, i: (rb + b * nq + i, 0)),
                  mem_spec, mem_spec],
        out_specs=pl.BlockSpec((tq, D_MODEL), lambda b, i: (b * nq + i, 0)),
        compiler_params=_cparams(("parallel", "parallel"),
                                 4 * (tq + N_MEM) * D_MODEL * 4 / 2**20 + 8),
        name=name,
    )(q, mem_k, mem_v)


def _memattn_heads_body(q_ref, k_ref, v_ref, c_ref, o_ref):
    scale = MEM_HD ** -0.5
    tq = q_ref.shape[0]
    q = jnp.concatenate([q_ref[:, h * MEM_HD:(h + 1) * MEM_HD] for h in range(MEM_HEADS)], axis=0)
    k = k_ref[...].reshape(N_MEM * MEM_HEADS, MEM_HD).astype(BF16)
    v = v_ref[...].reshape(N_MEM * MEM_HEADS, MEM_HD).astype(BF16)
    s = jnp.where(c_ref[...] > 0, _dot_nt(q.astype(BF16), k) * scale, NEG_INF)
    m = jnp.max(s, axis=-1, keepdims=True)
    p = jnp.exp(s - m)
    l = jnp.sum(p, axis=-1, keepdims=True)
    o = _dot(p.astype(BF16), v) / l
    for h in range(MEM_HEADS):
        o_ref[:, h * MEM_HD:(h + 1) * MEM_HD] = o[h * tq:(h + 1) * tq, :]


def _memattn_heads(q, mem_k, mem_v, *, row0, n_seq, seq, name):
    assert row0 % seq == 0 and seq % SUBLANES == 0
    rb = row0 // seq
    same_head = np.kron(np.eye(MEM_HEADS, dtype=np.float32), np.ones((seq, 1), np.float32))
    same_head = np.tile(same_head, (1, N_MEM))
    mem_spec = pl.BlockSpec((None, N_MEM, MEM_HEADS, MEM_HD), lambda b: (b, 0, 0, 0))
    return pl.pallas_call(
        _memattn_heads_body,
        out_shape=jax.ShapeDtypeStruct((n_seq * seq, D_MODEL), F32),
        grid=(n_seq,),
        in_specs=[pl.BlockSpec((seq, D_MODEL), lambda b: (rb + b, 0)), mem_spec, mem_spec,
                  pl.BlockSpec(same_head.shape, lambda b: (0, 0))],
        out_specs=pl.BlockSpec((seq, D_MODEL), lambda b: (b, 0)),
        compiler_params=_cparams(("parallel",), 8 * N_MEM * D_MODEL * 4 / 2**20 + 8),
        name=name,
    )(q, mem_k, mem_v, jnp.asarray(same_head))


def _router_body(x_ref, w_ref, b_ref, sel_ref, wts_ref, cnt_ref, run_s, *, tm):
    i = pl.program_id(0)

    @pl.when(i == 0)
    def _():
        run_s[...] = jnp.zeros_like(run_s)

    ng, epg = N_EXPERT_GROUPS, EXPERTS_PER_GROUP
    logits = jnp.dot(x_ref[...], w_ref[...], precision=lax.Precision.HIGHEST,
                     preferred_element_type=F32) + b_ref[...]
    lane = lax.broadcasted_iota(I32, (tm, ROUTER_LANES), 1)
    big = ROUTER_LANES

    def first_argmax(vals):
        mx = jnp.max(vals, axis=-1, keepdims=True)
        idx = jnp.min(jnp.where(vals == mx, lane, big), axis=-1, keepdims=True)
        return mx, idx

    gl = jnp.where(lane < ng, logits, NEG_INF)
    gmax, gsel = first_argmax(gl)
    g_w = 1.0 / jnp.sum(jnp.exp(gl - gmax), axis=-1, keepdims=True)
    lo = ng + gsel * epg
    el = jnp.where(jnp.logical_and(lane >= lo, lane < lo + epg), logits, NEG_INF)
    v1, i1 = first_argmax(el)
    v2, i2 = first_argmax(jnp.where(lane == i1, NEG_INF, el))
    e21 = jnp.exp(v2 - v1)
    w1 = g_w / (1.0 + e21)
    w2 = g_w * e21 / (1.0 + e21)

    onehot = jnp.logical_or(lane == i1, lane == i2)
    r = lax.broadcasted_iota(I32, (tm, tm), 0)
    cc = lax.broadcasted_iota(I32, (tm, tm), 1)
    tri = (cc < r).astype(BF16)
    before = _dot(tri, onehot.astype(BF16)) + run_s[...]
    rank1 = jnp.sum(jnp.where(lane == i1, before, 0.0), axis=-1, keepdims=True).astype(I32)
    rank2 = jnp.sum(jnp.where(lane == i2, before, 0.0), axis=-1, keepdims=True).astype(I32)
    run_s[...] = run_s[...] + jnp.sum(onehot.astype(F32), axis=0, keepdims=True)

    sel = jnp.where(lane == 0, i1 - ng, jnp.where(lane == 1, i2 - ng,
                    jnp.where(lane == 2, rank1, jnp.where(lane == 3, rank2, 0))))
    sel_ref[...] = sel
    wts_ref[...] = jnp.where(lane == 0, w1, jnp.where(lane == 1, w2, 0.0))
    cnt_ref[...] = run_s[...].astype(I32)


def _router(x, w, b, *, tm):
    m = x.shape[0]
    row = pl.BlockSpec((tm, ROUTER_LANES), lambda i: (i, 0))
    return pl.pallas_call(
        functools.partial(_router_body, tm=tm),
        out_shape=[jax.ShapeDtypeStruct((m, ROUTER_LANES), I32),
                   jax.ShapeDtypeStruct((m, ROUTER_LANES), F32),
                   jax.ShapeDtypeStruct((1, ROUTER_LANES), I32)],
        grid=(m // tm,),
        in_specs=[pl.BlockSpec((tm, D_MODEL), lambda i: (i, 0)),
                  pl.BlockSpec((D_MODEL, ROUTER_LANES), lambda i: (0, 0)),
                  pl.BlockSpec((1, ROUTER_LANES), lambda i: (0, 0))],
        out_specs=[row, row, pl.BlockSpec((1, ROUTER_LANES), lambda i: (0, 0))],
        scratch_shapes=[pltpu.VMEM((1, ROUTER_LANES), F32)],
        compiler_params=_cparams(("arbitrary",), 16),
        name="moe_router",
    )(x, w, b)


def _moe_body(te_ref, ord_ref, nxt_ref, src_ref, nact_ref, x_hbm, wg_hbm, wu_hbm, wd_hbm, o_ref,
              xbuf, xsem, wg_f, wu_f, wd_f, wsem, wg_s, wu_s, wd_s):
    i = pl.program_id(0)
    nact = nact_ref[0]
    tm = MOE_TILE
    slot = i % 2

    def weight_copies(expert, ws):
        return [pltpu.make_async_copy(hbm.at[expert], stage.at[ws], wsem.at[ws])
                for hbm, stage in ((wg_hbm, wg_f), (wu_hbm, wu_f), (wd_hbm, wd_f))]

    def issue_gather(tile, slot_):
        base = tile * tm
        for r in range(tm):
            _start_row_gather(x_hbm, src_ref[base + r], xbuf.at[slot_], r, xsem.at[slot_])

    def tile_step(prefetch):
        prev = te_ref[jnp.maximum(i - 1, 0)]

        @pl.when(jnp.logical_or(i == 0, te_ref[i] != prev))
        def _():
            ws = ord_ref[i] % 2
            for cp in weight_copies(te_ref[i], ws):
                cp.wait()
            wg_s[...] = wg_f[ws].astype(BF16)
            wu_s[...] = wu_f[ws].astype(BF16)
            wd_s[...] = wd_f[ws].astype(BF16)

            @pl.when(nxt_ref[i] >= 0)
            def _():
                for cp in weight_copies(nxt_ref[i], 1 - ws):
                    cp.start()

        _wait_row_gathers(xbuf.at[slot], xbuf.at[1 - slot], tm, xsem.at[slot])
        if prefetch:
            issue_gather(i + 1, 1 - slot)
        x = _load_gathered(xbuf.at[slot], tm).astype(BF16)
        hg = _dot(x, wg_s[...])
        hu = _dot(x, wu_s[...])
        h = hg * (1.0 / (1.0 + jnp.exp(-hg))) * hu
        _store_gatherable(o_ref, _dot(h.astype(BF16), wd_s[...]))

    @pl.when(i == 0)
    def _():
        for cp in weight_copies(te_ref[0], 0):
            cp.start()
        issue_gather(0, 0)

    @pl.when(i + 1 < nact)
    def _():
        tile_step(True)

    @pl.when(i + 1 == nact)
    def _():
        tile_step(False)

    @pl.when(i >= nact)
    def _():
        o_ref[...] = jnp.zeros_like(o_ref)


def _moe_experts(x_rows, w_gate, w_up, w_down, tile_expert, tile_ord, tile_next, src, nact, *,
                 n_tiles):
    tm = MOE_TILE
    out_map = lambda i, te, od, nx, s, n: (i, 0)
    any_spec = pl.BlockSpec(memory_space=pl.ANY)
    grid_spec = pltpu.PrefetchScalarGridSpec(
        num_scalar_prefetch=5,
        grid=(n_tiles,),
        in_specs=[any_spec, any_spec, any_spec, any_spec],
        out_specs=pl.BlockSpec((tm * ROW_PITCH, LANES), out_map),
        scratch_shapes=[pltpu.VMEM((2, tm * ROW_PITCH, LANES), F32),
                        pltpu.SemaphoreType.DMA((2,)),
                        pltpu.VMEM((2, D_MODEL, D_EXPERT), F32),
                        pltpu.VMEM((2, D_MODEL, D_EXPERT), F32),
                        pltpu.VMEM((2, D_EXPERT, D_MODEL), F32),
                        pltpu.SemaphoreType.DMA((2,)),
                        pltpu.VMEM((D_MODEL, D_EXPERT), BF16),
                        pltpu.VMEM((D_MODEL, D_EXPERT), BF16),
                        pltpu.VMEM((D_EXPERT, D_MODEL), BF16)],
    )
    return pl.pallas_call(
        _moe_body,
        out_shape=jax.ShapeDtypeStruct((n_tiles * tm * ROW_PITCH, LANES), F32),
        grid_spec=grid_spec,
        compiler_params=_cparams(("arbitrary",), 48),
        name="moe_experts",
    )(tile_expert, tile_ord, tile_next, src, nact, x_rows, w_gate, w_up, w_down)


def _combine_body(pos_ref, ys_hbm, wts_ref, x_ref, g_ref, b_ref, o_ref, buf, sem, *, tc):
    i = pl.program_id(0)
    n = pl.num_programs(0)
    slot = i % 2

    def issue_gather(tile, slot_):
        base = tile * tc * 2
        for r in range(tc):
            for k in range(2):
                _start_row_gather(ys_hbm, pos_ref[base + 2 * r + k], buf.at[slot_, k], r,
                                  sem.at[slot_])

    @pl.when(i == 0)
    def _():
        issue_gather(0, 0)

    for k in range(2):
        _wait_row_gathers(buf.at[slot, k], buf.at[1 - slot, k], tc, sem.at[slot])

    @pl.when(i + 1 < n)
    def _():
        issue_gather(i + 1, 1 - slot)

    w = wts_ref[...]
    moe = (w[:, 0:1] * _load_gathered(buf.at[slot, 0], tc)
           + w[:, 1:2] * _load_gathered(buf.at[slot, 1], tc))
    o_ref[...] = _layernorm(DEEPNORM_ALPHA * x_ref[...] + moe, g_ref[...], b_ref[...])


def _moe_combine(ys, pos, wts, x, g, b, *, tc):
    m = x.shape[0]
    grid_spec = pltpu.PrefetchScalarGridSpec(
        num_scalar_prefetch=1,
        grid=(m // tc,),
        in_specs=[pl.BlockSpec(memory_space=pl.ANY),
                  pl.BlockSpec((tc, ROUTER_LANES), lambda i, p: (i, 0)),
                  pl.BlockSpec((tc, D_MODEL), lambda i, p: (i, 0)),
                  pl.BlockSpec((1, D_MODEL), lambda i, p: (0, 0)),
                  pl.BlockSpec((1, D_MODEL), lambda i, p: (0, 0))],
        out_specs=pl.BlockSpec((tc, D_MODEL), lambda i, p: (i, 0)),
        scratch_shapes=[pltpu.VMEM((2, 2, tc * ROW_PITCH, LANES), F32),
                        pltpu.SemaphoreType.DMA((2,))],
    )
    return pl.pallas_call(
        functools.partial(_combine_body, tc=tc),
        out_shape=jax.ShapeDtypeStruct((m, D_MODEL), F32),
        grid_spec=grid_spec,
        compiler_params=_cparams(("arbitrary",), 8 * tc * D_MODEL * 4 / 2**20 + 8),
        name="moe_combine_ln3",
    )(pos, ys, wts, x, g, b)


def _moe(x, x_rows, w_r1, b_r1, w_r2, b_r2, w_gate, w_up, w_down, g, b, *, tm_router, tc):
    n = x.shape[0]
    ng, ne = N_EXPERT_GROUPS, N_EXPERTS
    pad = ROUTER_LANES - ng - ne
    w_r = jnp.concatenate([w_r1, w_r2.reshape(D_MODEL, ne), jnp.zeros((D_MODEL, pad), F32)], axis=1)
    b_r = jnp.concatenate([b_r1, b_r2.reshape(ne), jnp.zeros((pad,), F32)]).reshape(1, ROUTER_LANES)
    sel, wts, cnt = _router(x, w_r, b_r, tm=tm_router)

    tm = MOE_TILE
    n_tiles = (2 * n) // tm + ne
    counts = cnt[0, ng:ng + ne]
    tiles_per = (counts + tm - 1) // tm
    tile_end = jnp.cumsum(tiles_per)
    row_off = (tile_end - tiles_per) * tm
    nact = tile_end[-1]
    ids, ranks = sel[:, 0:2], sel[:, 2:4]
    pos = row_off[ids] + ranks
    tile_ids = jnp.minimum(jnp.arange(n_tiles, dtype=I32), nact - 1)
    tile_expert = jnp.sum((tile_end[None, :] <= tile_ids[:, None]).astype(I32), axis=1)
    token = jnp.broadcast_to(jnp.arange(n, dtype=I32)[:, None], (n, 2))
    src = jnp.zeros((n_tiles * tm,), I32).at[pos.reshape(-1)].set(token.reshape(-1))
    used = tiles_per > 0
    eid = jnp.arange(ne, dtype=I32)
    ordinal = jnp.cumsum(used.astype(I32)) - 1
    later = jnp.where(jnp.logical_and(used[None, :], eid[None, :] > eid[:, None]), eid[None, :], ne)
    nxt = jnp.min(later, axis=1)
    nxt = jnp.where(nxt == ne, -1, nxt)

    ys = _moe_experts(x_rows, w_gate, w_up, w_down, tile_expert, ordinal[tile_expert],
                      nxt[tile_expert], src, nact.reshape(1).astype(I32), n_tiles=n_tiles)
    return _moe_combine(ys, pos.reshape(-1).astype(I32), wts, x, g, b, tc=tc)


def _row_tile(m, cap):
    best = SUBLANES
    for t in range(SUBLANES, cap + 1, SUBLANES):
        if m % t == 0:
            best = t
    return best


def kernel(x_prompt, x_sample, cache_win_k, cache_win_v, state_ssm_re, state_ssm_im, cache_mem_k, cache_mem_v, mem_prompt, w_in, ssm_lam_re, ssm_lam_im, ssm_log_dt, ssm_b_re, ssm_b_im, ssm_c_re, ssm_c_im, ssm_d, w_glu, g_attn, g_ssm, w_out, ln1_g, ln1_b, w_mq, w_mk, w_mv, w_mo, ln2_g, ln2_b, w_r1, b_r1, w_r2, b_r2, w_gate, w_up, w_down, ln3_g, ln3_b):
    nb, seq, d = x_prompt.shape
    ns, dseq, _ = x_sample.shape
    n_p, n_s = nb * seq, ns * dseq
    n = n_p + n_s
    l = 0
    row2 = lambda v: v[l].reshape(1, -1)

    x_all = jnp.concatenate([x_prompt.reshape(n_p, d), x_sample.reshape(n_s, d)], axis=0)
    tm_big = _row_tile(n, 768)
    tm_ln = _row_tile(n, 384)

    proj = _matmul(x_all, w_in[l].astype(BF16), tm=tm_big, tn=1024, name="proj_in")

    attn_p = _attn_prompt(proj, n_batch=nb, seq=seq)
    attn_s = _attn_sample(proj, cache_win_k[l], cache_win_v[l], row0=n_p, n_seq=ns, n_new=dseq)
    attn = jnp.concatenate([attn_p, attn_s], axis=0)

    seg_len = seq // SSM_SEGMENTS
    prm = _ssm_params(ssm_lam_re[l], ssm_lam_im[l], ssm_log_dt[l], ssm_b_re[l], ssm_b_im[l],
                      ssm_c_re[l], ssm_c_im[l], ssm_d[l], seg_len)
    u_p = proj[:n_p, 3 * D_ATT:].reshape(nb, SSM_SEGMENTS, seg_len, D_SSM)
    u_p = jnp.transpose(u_p, (2, 0, 1, 3)).reshape(seg_len, nb * SSM_SEGMENTS, D_SSM)
    zeros = jnp.zeros((nb * SSM_SEGMENTS, N_SSM_GROUPS * SSM_STATE), F32)
    tl = _row_tile(seg_len, 32)
    end_re, end_im = _ssm_scan(u_p, prm, zeros, zeros, tl=tl, nseg=1, emit_y=False, exact_in=False)
    yg_p, fin_re, fin_im = _ssm_scan(u_p, prm, end_re, end_im, tl=tl, nseg=SSM_SEGMENTS,
                                     emit_y=True, exact_in=False)
    yg_p = jnp.transpose(yg_p.reshape(seg_len, nb, SSM_SEGMENTS, D_SSM), (1, 2, 0, 3))
    last = SSM_SEGMENTS - 1
    ssm_re_p = fin_re.reshape(nb, SSM_SEGMENTS, N_SSM_GROUPS, SSM_STATE)[:, last]
    ssm_im_p = fin_im.reshape(nb, SSM_SEGMENTS, N_SSM_GROUPS, SSM_STATE)[:, last]

    u_s = jnp.transpose(proj[n_p:, 3 * D_ATT:].reshape(ns, dseq, D_SSM), (1, 0, 2))
    h0_re = state_ssm_re[l].reshape(ns, -1)
    h0_im = state_ssm_im[l].reshape(ns, -1)
    yg_s, ssm_re_s, ssm_im_s = _ssm_scan(u_s, prm, h0_re, h0_im, tl=dseq, nseg=1,
                                         emit_y=True, exact_in=True)
    yg_s = jnp.transpose(yg_s, (1, 0, 2))
    yg = jnp.concatenate([yg_p.reshape(n_p, D_SSM), yg_s.reshape(n_s, D_SSM)], axis=0)
    ssm_out = _glu(yg, w_glu[l].astype(BF16), tm=tm_big)

    x1 = _mix(attn, ssm_out, row2(g_attn), row2(g_ssm), w_out[l].astype(BF16), x_all,
              row2(ln1_g), row2(ln1_b), tm=tm_ln)

    mem_rows = mem_prompt.reshape(nb * N_MEM, d)
    mem_k = _matmul(mem_rows, w_mk[l].astype(BF16), tm=nb * N_MEM, tn=1024, name="mem_k")
    mem_v = _matmul(mem_rows, w_mv[l].astype(BF16), tm=nb * N_MEM, tn=1024, name="mem_v")
    q_mem = _matmul(x1, w_mq[l].astype(BF16), tm=tm_big, tn=1024, name="mem_q")
    o_p = _memattn(q_mem, mem_k.reshape(nb, N_MEM, d), mem_v.reshape(nb, N_MEM, d),
                   row0=0, n_seq=nb, seq=seq, tq=_row_tile(seq, 512), name="memattn_prompt")
    o_s = _memattn_heads(q_mem, cache_mem_k[l], cache_mem_v[l], row0=n_p, n_seq=ns, seq=dseq,
                         name="memattn_sample")
    o_mem = jnp.concatenate([o_p, o_s], axis=0)
    x2, x2_rows = _mm_ln(o_mem, w_mo[l].astype(BF16), x1, row2(ln2_g), row2(ln2_b), tm=tm_ln,
                         name="mem_out_ln2")

    y = _moe(x2, x2_rows, w_r1[l], b_r1[l], w_r2[l], b_r2[l], w_gate[l], w_up[l], w_down[l],
             row2(ln3_g), row2(ln3_b), tm_router=_row_tile(n, 256), tc=_row_tile(n, 128))

    y_p = y[:n_p].reshape(nb, seq, d)
    y_s = y[n_p:].reshape(ns, dseq, d)
    k_p = proj[:n_p, D_ATT:2 * D_ATT].reshape(nb, seq, ATT_HEADS, ATT_HD)
    v_p = proj[:n_p, 2 * D_ATT:3 * D_ATT].reshape(nb, seq, ATT_HEADS, ATT_HD)
    wp = min(max(w for w, _ in DILATIONS), seq)
    k_s = proj[n_p:, D_ATT:2 * D_ATT].reshape(ns, dseq, ATT_HEADS, ATT_HD)
    v_s = proj[n_p:, 2 * D_ATT:3 * D_ATT].reshape(ns, dseq, ATT_HEADS, ATT_HD)
    state = lambda v, b_: v.reshape(1, b_, N_SSM_GROUPS, SSM_STATE)
    return (y_p, y_s, k_p[None, :, seq - wp:], v_p[None, :, seq - wp:], k_s[None], v_s[None],
            state(ssm_re_p, nb), state(ssm_im_p, nb), state(ssm_re_s, ns), state(ssm_im_s, ns),
            mem_k.reshape(1, nb, N_MEM, MEM_HEADS, MEM_HD),
            mem_v.reshape(1, nb, N_MEM, MEM_HEADS, MEM_HD))
```

```python
import functools
import math

import numpy as np
import jax
import jax.numpy as jnp
from jax import lax
from jax.experimental import pallas as pl
from jax.experimental.pallas import tpu as pltpu

F32 = jnp.float32
BF16 = jnp.bfloat16
I32 = jnp.int32

D_MODEL = 2048
PAST_LEN = 8192
D_ATT = D_MODEL // 2
ATT_HEADS = 8
ATT_HD = D_ATT // ATT_HEADS
DILATIONS = ((128, 1), (512, 4), (2048, 16))
D_SSM = D_MODEL - D_ATT
SSM_GROUP_CH = 16
N_SSM_GROUPS = D_SSM // SSM_GROUP_CH
SSM_STATE = 64
N_MEM = 256
MEM_HEADS = 4
MEM_HD = D_MODEL // MEM_HEADS
N_EXPERT_GROUPS = 4
EXPERTS_PER_GROUP = 8
N_EXPERTS = N_EXPERT_GROUPS * EXPERTS_PER_GROUP
D_EXPERT = D_MODEL // 4
DEPTH = 1
DEEPNORM_ALPHA = (2.0 * DEPTH) ** 0.25
LN_EPS = 1e-5
RMS_EPS = 1e-6

LANES = 128
SUBLANES = 8
ROW_CHUNKS = D_MODEL // LANES
ROW_PITCH = ROW_CHUNKS + 1
Q_BLOCK = 128
ATTN_GROUP = 8
SSM_LANE_TILE = 128
SSM_GROUPS_PER_TILE = SSM_LANE_TILE // SSM_GROUP_CH
SSM_STATES_PER_TILE = SSM_GROUPS_PER_TILE * SSM_STATE
SSM_SEGMENTS = 8
MOE_TILE = 256
GATHER_SLOTS = 3
ROUTER_LANES = 128
NEG_INF = float("-inf")


def _cparams(semantics, vmem_mib):
    return pltpu.CompilerParams(dimension_semantics=semantics,
                                vmem_limit_bytes=int(vmem_mib) << 20)


def _layernorm(y, g, b):
    mu = jnp.mean(y, axis=-1, keepdims=True)
    yc = y - mu
    var = jnp.mean(yc * yc, axis=-1, keepdims=True)
    return yc * lax.rsqrt(var + LN_EPS) * g + b


def _rmsnorm(v, g):
    return v * lax.rsqrt(jnp.mean(v * v, axis=-1, keepdims=True) + RMS_EPS) * g


def _dot(a, b):
    return jnp.dot(a, b, preferred_element_type=F32)


def _dot_nt(a, b):
    return lax.dot_general(a, b, (((1,), (1,)), ((), ())), preferred_element_type=F32)


def _mm_body(x_ref, w_ref, o_ref):
    o_ref[...] = _dot(x_ref[...].astype(BF16), w_ref[...]).astype(o_ref.dtype)


def _matmul(x, w, *, tm, tn, name):
    m, k = x.shape
    n = w.shape[1]
    vmem = 2 * (tm * k * x.dtype.itemsize + k * tn * 2 + tm * tn * 4) / 2**20 + 8
    return pl.pallas_call(
        _mm_body,
        out_shape=jax.ShapeDtypeStruct((m, n), F32),
        grid=(n // tn, m // tm),
        in_specs=[pl.BlockSpec((tm, k), lambda j, i: (i, 0)),
                  pl.BlockSpec((k, tn), lambda j, i: (0, j))],
        out_specs=pl.BlockSpec((tm, tn), lambda j, i: (i, j)),
        compiler_params=_cparams(("parallel", "parallel"), vmem),
        name=name,
    )(x, w)


def _attn_prompt_body(q_ref, k_ref, v_ref, o_ref, kt_s, va_s, on_s, lse_s, *, seq, dilations):
    scale = ATT_HD ** -0.5
    nblk = seq // Q_BLOCK
    qi = lax.broadcasted_iota(I32, (Q_BLOCK, Q_BLOCK), 0)
    kj = lax.broadcasted_iota(I32, (Q_BLOCK, Q_BLOCK), 1)
    cur_ok = kj <= qi
    prev_ok = kj >= qi
    va_s[:, :, ATT_HD:] = jnp.ones((nblk, Q_BLOCK, ATT_HD), BF16)

    for br, (_, d) in enumerate(dilations):
        span = d * Q_BLOCK
        nb = seq // span

        def stream_rows(t, d=d, span=span, nb=nb):
            r = t // nb
            ib = t % nb
            return r, ib, pl.ds(r + ib * span, Q_BLOCK, stride=d)

        def prep(g, carry, stream_rows=stream_rows):
            loaded = []
            for j in range(ATTN_GROUP):
                t = g * ATTN_GROUP + j
                _, _, rows = stream_rows(t)
                loaded.append((t, k_ref[rows, :], v_ref[rows, :]))
            for t, kk, vv in loaded:
                kt_s[t] = jnp.transpose(kk).astype(BF16)
                va_s[t, :, 0:ATT_HD] = vv.astype(BF16)
            return carry

        lax.fori_loop(0, nblk // ATTN_GROUP, prep, 0)

        def group(g, carry, br=br, nb=nb, stream_rows=stream_rows):
            scores = []
            for j in range(ATTN_GROUP):
                t = g * ATTN_GROUP + j
                r, ib, rows = stream_rows(t)
                tp = jnp.maximum(t - 1, r * nb)
                q = (q_ref[rows, :] * scale).astype(BF16)
                s = _dot(q, jnp.concatenate([kt_s[tp], kt_s[t]], axis=1))
                scores.append((t, tp, ib, rows, s))
            probs = []
            for t, tp, ib, rows, s in scores:
                ok = jnp.concatenate([jnp.logical_and(prev_ok, ib > 0), cur_ok], axis=1)
                s = jnp.where(ok, s, NEG_INF)
                m = jnp.max(s, axis=-1, keepdims=True)
                probs.append((t, tp, rows, m, jnp.exp(s - m).astype(BF16)))
            outs = [(rows, m, _dot(p, jnp.concatenate([va_s[tp], va_s[t]], axis=0)))
                    for t, tp, rows, m, p in probs]
            for rows, m, al in outs:
                l = al[:, ATT_HD:]
                on_s[br, rows, :] = al[:, :ATT_HD] / l
                lse_s[br, rows, :] = m + jnp.log(l)
            return carry

        lax.fori_loop(0, nblk // ATTN_GROUP, group, 0)

    chunk = 256
    nbr = len(dilations)

    def merge(c, carry):
        rows = pl.ds(pl.multiple_of(c * chunk, chunk), chunk)
        ls = [lse_s[b, rows, :] for b in range(nbr)]
        mx = functools.reduce(jnp.maximum, ls)
        es = [jnp.exp(li - mx) for li in ls]
        num = sum(es[b] * on_s[b, rows, :] for b in range(nbr))
        o_ref[rows, :] = num / sum(es)
        return carry

    lax.fori_loop(0, seq // chunk, merge, 0)


def _attn_prompt(proj, *, n_batch, seq, dilations=DILATIONS):
    for w, d in dilations:
        assert w // d == Q_BLOCK and seq % (d * Q_BLOCK) == 0
    nbr = len(dilations)
    nblk = seq // Q_BLOCK
    assert nblk % ATTN_GROUP == 0
    blk = lambda off: pl.BlockSpec((seq, ATT_HD), lambda b, h, off=off: (b, off + h))
    vmem = ((4 * 2 + 2 * nbr) * seq * ATT_HD * 4 + 3 * seq * ATT_HD * 2) / 2**20 + 8
    return pl.pallas_call(
        functools.partial(_attn_prompt_body, seq=seq, dilations=dilations),
        out_shape=jax.ShapeDtypeStruct((n_batch * seq, D_ATT), F32),
        grid=(n_batch, ATT_HEADS),
        in_specs=[blk(0), blk(ATT_HEADS), blk(2 * ATT_HEADS)],
        out_specs=pl.BlockSpec((seq, ATT_HD), lambda b, h: (b, h)),
        scratch_shapes=[pltpu.VMEM((nblk, ATT_HD, Q_BLOCK), BF16),
                        pltpu.VMEM((nblk, Q_BLOCK, 2 * ATT_HD), BF16),
                        pltpu.VMEM((nbr, seq, ATT_HD), F32),
                        pltpu.VMEM((nbr, seq, ATT_HD), F32)],
        compiler_params=_cparams(("parallel", "parallel"), vmem),
        name="attn_prompt",
    )(proj, proj, proj)


def _sample_key_multiplicity(n_new, n_cache, past_len, dilations):
    d_max = max(d for _, d in dilations)
    tail = max(w for w, d in dilations if d != d_max)
    assert past_len % d_max == 0 and n_cache % d_max == 0 and n_new <= d_max // 2
    assert tail % d_max == 0 and tail <= n_cache
    half = d_max // 2
    n_grid = (n_cache - tail) // d_max
    kv_start = past_len - n_cache
    grid_rows = (np.arange(n_grid)[:, None] * d_max + np.arange(half)[None, :]).reshape(-1)
    tail_rows = n_cache - tail + np.arange(tail)
    new_rows = n_cache + np.arange(n_new)
    qpos = past_len + np.arange(n_new)

    def mult(rows):
        kpos = kv_start + rows
        delta = qpos[:, None] - kpos[None, :]
        c = np.zeros(delta.shape, np.float32)
        for w, d in dilations:
            c += ((delta >= 0) & (delta <= w) & (delta % d == 0) & (kpos[None, :] >= kv_start))
        return c

    fetched = np.zeros(n_cache + n_new, bool)
    fetched[grid_rows] = True
    fetched[tail_rows] = True
    fetched[new_rows] = True
    assert not mult(np.nonzero(~fetched)[0]).any()
    return mult(grid_rows), mult(tail_rows), mult(new_rows), n_grid, tail, half, d_max


def _attn_sample_body(q_ref, kn_ref, vn_ref, kg_ref, kt_ref, vg_ref, vt_ref,
                      cg_ref, ct_ref, cn_ref, o_ref):
    scale = ATT_HD ** -0.5
    heads = lambda ref: jnp.concatenate(
        [ref[:, h * ATT_HD:(h + 1) * ATT_HD] for h in range(ATT_HEADS)], axis=0)
    q = (heads(q_ref) * scale).astype(BF16)
    kn = heads(kn_ref).astype(BF16)
    vn = heads(vn_ref).astype(BF16)
    flat = lambda ref: ref[...].reshape(-1, ATT_HD).astype(BF16)
    cg, ct, cn = cg_ref[...], ct_ref[...], cn_ref[...]
    sg = jnp.where(cg > 0, _dot_nt(q, flat(kg_ref)), NEG_INF)
    st = jnp.where(ct > 0, _dot_nt(q, flat(kt_ref)), NEG_INF)
    sn = jnp.where(cn > 0, _dot_nt(q, kn), NEG_INF)
    m = jnp.maximum(jnp.maximum(jnp.max(sg, axis=-1, keepdims=True),
                                jnp.max(st, axis=-1, keepdims=True)),
                    jnp.max(sn, axis=-1, keepdims=True))
    pg = cg * jnp.exp(sg - m)
    pt = ct * jnp.exp(st - m)
    pn = cn * jnp.exp(sn - m)
    l = (jnp.sum(pg, axis=-1, keepdims=True) + jnp.sum(pt, axis=-1, keepdims=True)
         + jnp.sum(pn, axis=-1, keepdims=True))
    acc = (_dot(pg.astype(BF16), flat(vg_ref)) + _dot(pt.astype(BF16), flat(vt_ref))
           + _dot(pn.astype(BF16), vn))
    out = acc / l
    n_new = q_ref.shape[0]
    for h in range(ATT_HEADS):
        o_ref[:, h * ATT_HD:(h + 1) * ATT_HD] = out[h * n_new:(h + 1) * n_new, :]


def _attn_sample(proj, win_k, win_v, *, row0, n_seq, n_new, past_len=PAST_LEN,
                 dilations=DILATIONS):
    n_cache = win_k.shape[1]
    cg, ct, cn, n_grid, tail, half, d_max = _sample_key_multiplicity(
        n_new, n_cache, past_len, dilations)
    assert row0 % n_new == 0 and n_new % SUBLANES == 0 and n_cache % tail == 0
    eye = np.eye(ATT_HEADS, dtype=np.float32)
    key_major = lambda c: np.einsum("tk,hg->htkg", c, eye).reshape(ATT_HEADS * n_new, -1)
    head_major = lambda c: np.einsum("tk,hg->htgk", c, eye).reshape(ATT_HEADS * n_new, -1)
    cg, ct, cn = key_major(cg), key_major(ct), head_major(cn)
    rb = row0 // n_new
    n_groups = n_cache // d_max
    kgv = win_k.reshape(n_seq, n_groups, d_max, ATT_HEADS, ATT_HD)
    vgv = win_v.reshape(n_seq, n_groups, d_max, ATT_HEADS, ATT_HD)
    ktv = win_k.reshape(n_seq, n_cache // tail, tail, ATT_HEADS, ATT_HD)
    vtv = win_v.reshape(n_seq, n_cache // tail, tail, ATT_HEADS, ATT_HD)
    new = lambda off: pl.BlockSpec((n_new, D_ATT), lambda b, off=off: (rb + b, off))
    grid_spec = pl.BlockSpec((None, n_grid, half, ATT_HEADS, ATT_HD), lambda b: (b, 0, 0, 0, 0))
    tail_spec = pl.BlockSpec((None, None, tail, ATT_HEADS, ATT_HD),
                             lambda b: (b, n_cache // tail - 1, 0, 0, 0))
    const = lambda a: pl.BlockSpec(a.shape, lambda b: (0, 0))
    vmem = (2 * 2 * (n_grid * half + tail) * D_ATT * 4 + 4 * cg.size * 4 * 3) / 2**20 + 12
    return pl.pallas_call(
        _attn_sample_body,
        out_shape=jax.ShapeDtypeStruct((n_seq * n_new, D_ATT), F32),
        grid=(n_seq,),
        in_specs=[new(0), new(1), new(2), grid_spec, tail_spec, grid_spec, tail_spec,
                  const(cg), const(ct), const(cn)],
        out_specs=pl.BlockSpec((n_new, D_ATT), lambda b: (b, 0)),
        compiler_params=_cparams(("parallel",), vmem),
        name="attn_sample",
    )(proj, proj, proj, kgv, ktv, vgv, vtv, jnp.asarray(cg), jnp.asarray(ct), jnp.asarray(cn))


def _gelu_tanh(x):
    return 0.5 * x * (1.0 + jnp.tanh(math.sqrt(2.0 / math.pi) * (x + 0.044715 * (x * x * x))))


def _ssm_body(u_ref, bb_ref, cst_ref, a_ref, ap_ref, d_ref, hre_ref, him_ref, *rest,
              tl, npar, nseg, emit_y, exact_in):
    if emit_y:
        y_ref, fre_ref, fim_ref, x_s, h_s = rest
    else:
        fre_ref, fim_ref, x_s, h_s = rest
    ns = SSM_STATES_PER_TILE
    c = pl.program_id(1)

    @pl.when(c == 0)
    def _init():
        if nseg == 1:
            h_s[0] = hre_ref[...]
            h_s[1] = him_ref[...]
        else:
            pr, pi = ap_ref[0:1, :], ap_ref[1:2, :]
            for b in range(npar // nseg):
                sr = jnp.zeros((1, ns), F32)
                si = jnp.zeros((1, ns), F32)
                for j in range(nseg):
                    row = b * nseg + j
                    h_s[0, row:row + 1, :] = sr
                    h_s[1, row:row + 1, :] = si
                    er, ei = hre_ref[row:row + 1, :], him_ref[row:row + 1, :]
                    sr, si = pr * sr - pi * si + er, pr * si + pi * sr + ei

    u = u_ref[...].reshape(tl * npar, SSM_LANE_TILE)
    if exact_in:
        x_s[...] = jnp.dot(u, bb_ref[...], precision=lax.Precision.HIGHEST,
                           preferred_element_type=F32)
    else:
        x_s[...] = _dot(u.astype(BF16), bb_ref[...])

    ar = jnp.broadcast_to(a_ref[0:1, :], (SUBLANES, ns))
    ai = jnp.broadcast_to(a_ref[1:2, :], (SUBLANES, ns))
    ngrp = npar // SUBLANES

    def step(i, carry):
        out = []
        for g in range(ngrp):
            hr, hi = carry[2 * g], carry[2 * g + 1]
            rows = pl.ds(pl.multiple_of(i * npar + g * SUBLANES, SUBLANES), SUBLANES)
            nr = ar * hr - ai * hi + x_s[rows, 0:ns]
            ni = ar * hi + ai * hr + x_s[rows, ns:2 * ns]
            if emit_y:
                x_s[rows, 0:ns] = nr
                x_s[rows, ns:2 * ns] = ni
            out += [nr, ni]
        return tuple(out)

    init = []
    for g in range(ngrp):
        gs = slice(g * SUBLANES, (g + 1) * SUBLANES)
        init += [h_s[0, gs, :], h_s[1, gs, :]]
    fin = lax.fori_loop(0, tl, step, tuple(init), unroll=4)
    for g in range(ngrp):
        gs = slice(g * SUBLANES, (g + 1) * SUBLANES)
        h_s[0, gs, :] = fin[2 * g]
        h_s[1, gs, :] = fin[2 * g + 1]

    if emit_y:
        y = _dot(x_s[...].astype(BF16), cst_ref[...]) + d_ref[...] * u
        y_ref[...] = _gelu_tanh(y).reshape(tl, npar, SSM_LANE_TILE)

    @pl.when(c == pl.num_programs(1) - 1)
    def _fin():
        fre_ref[...] = h_s[0]
        fim_ref[...] = h_s[1]


def _ssm_scan(u3, prm, hin_re, hin_im, *, tl, nseg, emit_y, exact_in):
    n_steps, npar, d_ssm = u3.shape
    ns = SSM_STATES_PER_TILE
    nk = d_ssm // SSM_LANE_TILE
    bb = prm["bb_f32"] if exact_in else prm["bb_bf16"]
    in_specs = [
        pl.BlockSpec((tl, npar, SSM_LANE_TILE), lambda k, c: (c, 0, k)),
        pl.BlockSpec((None, SSM_LANE_TILE, 2 * ns), lambda k, c: (k, 0, 0)),
        pl.BlockSpec((None, 2 * ns, SSM_LANE_TILE), lambda k, c: (k, 0, 0)),
        pl.BlockSpec((None, 2, ns), lambda k, c: (k, 0, 0)),
        pl.BlockSpec((None, 2, ns), lambda k, c: (k, 0, 0)),
        pl.BlockSpec((1, SSM_LANE_TILE), lambda k, c: (0, k)),
        pl.BlockSpec((npar, ns), lambda k, c: (0, k)),
        pl.BlockSpec((npar, ns), lambda k, c: (0, k)),
    ]
    state_shape = jax.ShapeDtypeStruct((npar, nk * ns), F32)
    state_spec = pl.BlockSpec((npar, ns), lambda k, c: (0, k))
    out_shape = [state_shape, state_shape]
    out_specs = [state_spec, state_spec]
    if emit_y:
        out_shape = [jax.ShapeDtypeStruct(u3.shape, F32)] + out_shape
        out_specs = [pl.BlockSpec((tl, npar, SSM_LANE_TILE), lambda k, c: (c, 0, k))] + out_specs
    return pl.pallas_call(
        functools.partial(_ssm_body, tl=tl, npar=npar, nseg=nseg, emit_y=emit_y,
                          exact_in=exact_in),
        out_shape=out_shape,
        grid=(nk, n_steps // tl),
        in_specs=in_specs,
        out_specs=out_specs,
        scratch_shapes=[pltpu.VMEM((tl * npar, 2 * ns), F32), pltpu.VMEM((2, npar, ns), F32)],
        compiler_params=_cparams(("parallel", "arbitrary"), 32),
        name="ssm_scan_y" if emit_y else "ssm_scan_state",
    )(u3, bb, prm["cst"], prm["a"], prm["apow"], prm["d"], hin_re, hin_im)


def _ssm_params(lam_re, lam_im, log_dt, b_re, b_im, c_re, c_im, d_skip, seg_len):
    g, p, c = N_SSM_GROUPS, SSM_STATE, SSM_GROUP_CH
    nk, gt = g // SSM_GROUPS_PER_TILE, SSM_GROUPS_PER_TILE
    dt = jnp.exp(log_dt.astype(F32))[:, None]
    lr, li = lam_re.astype(F32), lam_im.astype(F32)
    mag = jnp.exp(lr * dt)
    a_re, a_im = mag * jnp.cos(li * dt), mag * jnp.sin(li * dt)
    magp = jnp.exp(lr * dt * seg_len)
    p_re, p_im = magp * jnp.cos(li * dt * seg_len), magp * jnp.sin(li * dt * seg_len)
    den = lr * lr + li * li
    nr, ni = a_re - 1.0, a_im
    f_re, f_im = (nr * lr + ni * li) / den, (ni * lr - nr * li) / den
    br, bi = b_re.astype(F32), b_im.astype(F32)
    bb_re = f_re[..., None] * br - f_im[..., None] * bi
    bb_im = f_re[..., None] * bi + f_im[..., None] * br
    eye = jnp.eye(gt, dtype=F32)

    def pack_b(m):
        return jnp.einsum("kgpc,gh->kgchp", m.reshape(nk, gt, p, c), eye).reshape(nk, gt * c, gt * p)

    def pack_c(m):
        return jnp.einsum("kgcp,gh->kgphc", m.reshape(nk, gt, c, p), eye).reshape(nk, gt * p, gt * c)

    bb = jnp.concatenate([pack_b(bb_re), pack_b(bb_im)], axis=2)
    cst = jnp.concatenate([pack_c(c_re.astype(F32)), -pack_c(c_im.astype(F32))], axis=1)
    tile = lambda v: v.reshape(nk, 1, gt * p)
    return {
        "bb_f32": bb, "bb_bf16": bb.astype(BF16), "cst": cst.astype(BF16),
        "a": jnp.concatenate([tile(a_re), tile(a_im)], axis=1),
        "apow": jnp.concatenate([tile(p_re), tile(p_im)], axis=1),
        "d": d_skip.astype(F32).reshape(1, g * c),
    }


def _glu_body(y_ref, w_ref, o_ref):
    yg = y_ref[...]
    z = _dot(yg.astype(BF16), w_ref[...])
    o_ref[...] = yg * (1.0 / (1.0 + jnp.exp(-z)))


def _glu(yg, w, *, tm, name):
    m, n = yg.shape
    return pl.pallas_call(
        _glu_body,
        out_shape=jax.ShapeDtypeStruct((m, n), F32),
        grid=(m // tm,),
        in_specs=[pl.BlockSpec((tm, n), lambda i: (i, 0)), pl.BlockSpec((n, n), lambda i: (0, 0))],
        out_specs=pl.BlockSpec((tm, n), lambda i: (i, 0)),
        compiler_params=_cparams(("parallel",), 4 * tm * n * 4 / 2**20 + 12),
        name=name,
    )(yg, w)


def _mix_body(attn_ref, ssm_ref, ga_ref, gs_ref, w_ref, x_ref, g_ref, b_ref, o_ref):
    a = _rmsnorm(attn_ref[...], ga_ref[...]).astype(BF16)
    s = _rmsnorm(ssm_ref[...], gs_ref[...]).astype(BF16)
    mix = _dot(a, w_ref[0:D_ATT, :]) + _dot(s, w_ref[D_ATT:D_ATT + D_SSM, :])
    o_ref[...] = _layernorm(DEEPNORM_ALPHA * x_ref[...] + mix, g_ref[...], b_ref[...])


def _mix(attn, ssm, ga, gs, w, x, g, b, *, tm, name):
    m = x.shape[0]
    row = lambda n: pl.BlockSpec((tm, n), lambda i: (i, 0))
    const = lambda a: pl.BlockSpec(a.shape, lambda i: (0, 0))
    return pl.pallas_call(
        _mix_body,
        out_shape=jax.ShapeDtypeStruct((m, D_MODEL), F32),
        grid=(m // tm,),
        in_specs=[row(D_ATT), row(D_SSM), const(ga), const(gs), const(w), row(D_MODEL),
                  const(g), const(b)],
        out_specs=row(D_MODEL),
        compiler_params=_cparams(("parallel",), 6 * tm * D_MODEL * 4 / 2**20 + 24),
        name=name,
    )(attn, ssm, ga, gs, w, x, g, b)


def _store_gatherable(o_ref, y):
    rows = y.shape[0]
    for c in range(ROW_CHUNKS):
        o_ref[pl.ds(c, rows, stride=ROW_PITCH), :] = y[:, c * LANES:(c + 1) * LANES]
    for c in range(ROW_CHUNKS, ROW_PITCH):
        o_ref[pl.ds(c, rows, stride=ROW_PITCH), :] = jnp.zeros((rows, LANES), F32)


def _load_gathered(buf, rows):
    return jnp.concatenate([buf[pl.ds(c, rows, stride=ROW_PITCH), :] for c in range(ROW_CHUNKS)],
                           axis=1)


def _start_row_gather(src_hbm, idx, buf, r, sem):
    pltpu.make_async_copy(src_hbm.at[pl.ds(idx * ROW_PITCH, ROW_CHUNKS), :],
                          buf.at[pl.ds(r * ROW_PITCH, ROW_CHUNKS), :], sem).start()


def _wait_row_gathers(buf, other, rows, sem):
    span = pl.ds(0, rows * ROW_CHUNKS)
    pltpu.make_async_copy(other.at[span, :], buf.at[span, :], sem).wait()


def _mm_ln_body(a1_ref, a2_ref, w_ref, x1_ref, x2_ref, g_ref, b_ref, o_ref, rows_ref, *, tiles1):
    first = pl.program_id(0) < tiles1
    a = jnp.where(first, a1_ref[...], a2_ref[...])
    x = jnp.where(first, x1_ref[...], x2_ref[...])
    y = _dot(a.astype(BF16), w_ref[...])
    out = _layernorm(DEEPNORM_ALPHA * x + y, g_ref[...], b_ref[...])
    o_ref[...] = out
    _store_gatherable(rows_ref, out)


def _mm_ln(a1, a2, w, x1, x2, g, b, *, name):
    tm = a2.shape[0]
    assert a1.shape[0] % tm == 0
    tiles1 = a1.shape[0] // tm
    m = a1.shape[0] + tm
    row1 = lambda n: pl.BlockSpec((tm, n), lambda i: (jnp.minimum(i, tiles1 - 1), 0))
    row2 = lambda n: pl.BlockSpec((tm, n), lambda i: (0, 0))
    const = lambda v: pl.BlockSpec(v.shape, lambda i: (0, 0))
    return pl.pallas_call(
        functools.partial(_mm_ln_body, tiles1=tiles1),
        out_shape=[jax.ShapeDtypeStruct((m, D_MODEL), F32),
                   jax.ShapeDtypeStruct((m * ROW_PITCH, LANES), F32)],
        grid=(tiles1 + 1,),
        in_specs=[row1(a1.shape[1]), row2(a2.shape[1]), const(w), row1(D_MODEL), row2(D_MODEL),
                  const(g), const(b)],
        out_specs=[pl.BlockSpec((tm, D_MODEL), lambda i: (i, 0)),
                   pl.BlockSpec((tm * ROW_PITCH, LANES), lambda i: (i, 0))],
        compiler_params=_cparams(("parallel",), 12 * tm * D_MODEL * 4 / 2**20 + 24),
        name=name,
    )(a1, a2, w, x1, x2, g, b)


def _memattn_body(q_ref, k_ref, v_ref, o_ref):
    scale = MEM_HD ** -0.5
    for h in range(MEM_HEADS):
        sl = slice(h * MEM_HD, (h + 1) * MEM_HD)
        s = _dot_nt(q_ref[:, sl].astype(BF16), k_ref[:, sl].astype(BF16)) * scale
        m = jnp.max(s, axis=-1, keepdims=True)
        p = jnp.exp(s - m)
        l = jnp.sum(p, axis=-1, keepdims=True)
        o_ref[:, sl] = _dot(p.astype(BF16), v_ref[:, sl].astype(BF16)) / l


def _memattn(q, mem_k, mem_v, *, row0, n_seq, seq, tq, name):
    assert seq % tq == 0 and row0 % tq == 0
    nq = seq // tq
    rb = row0 // tq
    mem_spec = pl.BlockSpec((None, N_MEM, D_MODEL), lambda b, i: (b, 0, 0))
    return pl.pallas_call(
        _memattn_body,
        out_shape=jax.ShapeDtypeStruct((n_seq * seq, D_MODEL), F32),
        grid=(n_seq, nq),
        in_specs=[pl.BlockSpec((tq, D_MODEL), lambda b, i: (rb + b * nq + i, 0)),
                  mem_spec, mem_spec],
        out_specs=pl.BlockSpec((tq, D_MODEL), lambda b, i: (b * nq + i, 0)),
        compiler_params=_cparams(("parallel", "parallel"),
                                 4 * (tq + N_MEM) * D_MODEL * 4 / 2**20 + 8),
        name=name,
    )(q, mem_k, mem_v)


def _memattn_heads_body(q_ref, k_ref, v_ref, c_ref, o_ref):
    scale = MEM_HD ** -0.5
    tq = q_ref.shape[0]
    q = jnp.concatenate([q_ref[:, h * MEM_HD:(h + 1) * MEM_HD] for h in range(MEM_HEADS)], axis=0)
    k = k_ref[...].reshape(N_MEM * MEM_HEADS, MEM_HD).astype(BF16)
    v = v_ref[...].reshape(N_MEM * MEM_HEADS, MEM_HD).astype(BF16)
    s = jnp.where(c_ref[...] > 0, _dot_nt(q.astype(BF16), k) * scale, NEG_INF)
    m = jnp.max(s, axis=-1, keepdims=True)
    p = jnp.exp(s - m)
    l = jnp.sum(p, axis=-1, keepdims=True)
    o = _dot(p.astype(BF16), v) / l
    for h in range(MEM_HEADS):
        o_ref[:, h * MEM_HD:(h + 1) * MEM_HD] = o[h * tq:(h + 1) * tq, :]


def _memattn_heads(q, mem_k, mem_v, *, row0, n_seq, seq, name):
    assert row0 % seq == 0 and seq % SUBLANES == 0
    rb = row0 // seq
    same_head = np.kron(np.eye(MEM_HEADS, dtype=np.float32), np.ones((seq, 1), np.float32))
    same_head = np.tile(same_head, (1, N_MEM))
    mem_spec = pl.BlockSpec((None, N_MEM, MEM_HEADS, MEM_HD), lambda b: (b, 0, 0, 0))
    return pl.pallas_call(
        _memattn_heads_body,
        out_shape=jax.ShapeDtypeStruct((n_seq * seq, D_MODEL), F32),
        grid=(n_seq,),
        in_specs=[pl.BlockSpec((seq, D_MODEL), lambda b: (rb + b, 0)), mem_spec, mem_spec,
                  pl.BlockSpec(same_head.shape, lambda b: (0, 0))],
        out_specs=pl.BlockSpec((seq, D_MODEL), lambda b: (b, 0)),
        compiler_params=_cparams(("parallel",), 8 * N_MEM * D_MODEL * 4 / 2**20 + 8),
        name=name,
    )(q, mem_k, mem_v, jnp.asarray(same_head))


def _router_body(x_ref, w_ref, b_ref, sel_ref, wts_ref, cnt_ref, run_s, *, tm):
    i = pl.program_id(0)

    @pl.when(i == 0)
    def _():
        run_s[...] = jnp.zeros_like(run_s)

    ng, epg = N_EXPERT_GROUPS, EXPERTS_PER_GROUP
    logits = jnp.dot(x_ref[...], w_ref[...], precision=lax.Precision.HIGHEST,
                     preferred_element_type=F32) + b_ref[...]
    lane = lax.broadcasted_iota(I32, (tm, ROUTER_LANES), 1)
    big = ROUTER_LANES

    def first_argmax(vals):
        mx = jnp.max(vals, axis=-1, keepdims=True)
        idx = jnp.min(jnp.where(vals == mx, lane, big), axis=-1, keepdims=True)
        return mx, idx

    gl = jnp.where(lane < ng, logits, NEG_INF)
    gmax, gsel = first_argmax(gl)
    g_w = 1.0 / jnp.sum(jnp.exp(gl - gmax), axis=-1, keepdims=True)
    lo = ng + gsel * epg
    el = jnp.where(jnp.logical_and(lane >= lo, lane < lo + epg), logits, NEG_INF)
    v1, i1 = first_argmax(el)
    v2, i2 = first_argmax(jnp.where(lane == i1, NEG_INF, el))
    e21 = jnp.exp(v2 - v1)
    w1 = g_w / (1.0 + e21)
    w2 = g_w * e21 / (1.0 + e21)

    onehot = jnp.logical_or(lane == i1, lane == i2)
    r = lax.broadcasted_iota(I32, (tm, tm), 0)
    cc = lax.broadcasted_iota(I32, (tm, tm), 1)
    tri = (cc < r).astype(BF16)
    before = _dot(tri, onehot.astype(BF16)) + run_s[...]
    rank1 = jnp.sum(jnp.where(lane == i1, before, 0.0), axis=-1, keepdims=True).astype(I32)
    rank2 = jnp.sum(jnp.where(lane == i2, before, 0.0), axis=-1, keepdims=True).astype(I32)
    run_s[...] = run_s[...] + jnp.sum(onehot.astype(F32), axis=0, keepdims=True)

    sel = jnp.where(lane == 0, i1 - ng, jnp.where(lane == 1, i2 - ng,
                    jnp.where(lane == 2, rank1, jnp.where(lane == 3, rank2, 0))))
    sel_ref[...] = sel
    wts_ref[...] = jnp.where(lane == 0, w1, jnp.where(lane == 1, w2, 0.0))
    cnt_ref[...] = run_s[...].astype(I32)


def _router(x, w, b, *, tm):
    m = x.shape[0]
    row = pl.BlockSpec((tm, ROUTER_LANES), lambda i: (i, 0))
    return pl.pallas_call(
        functools.partial(_router_body, tm=tm),
        out_shape=[jax.ShapeDtypeStruct((m, ROUTER_LANES), I32),
                   jax.ShapeDtypeStruct((m, ROUTER_LANES), F32),
                   jax.ShapeDtypeStruct((1, ROUTER_LANES), I32)],
        grid=(m // tm,),
        in_specs=[pl.BlockSpec((tm, D_MODEL), lambda i: (i, 0)),
                  pl.BlockSpec((D_MODEL, ROUTER_LANES), lambda i: (0, 0)),
                  pl.BlockSpec((1, ROUTER_LANES), lambda i: (0, 0))],
        out_specs=[row, row, pl.BlockSpec((1, ROUTER_LANES), lambda i: (0, 0))],
        scratch_shapes=[pltpu.VMEM((1, ROUTER_LANES), F32)],
        compiler_params=_cparams(("arbitrary",), 16),
        name="moe_router",
    )(x, w, b)


def _moe_body(te_ref, ord_ref, nxt_ref, src_ref, nact_ref, x_hbm, wg_hbm, wu_hbm, wd_hbm, o_ref,
              xbuf, xsem, wg_f, wu_f, wd_f, wsem, wg_s, wu_s, wd_s):
    i = pl.program_id(0)
    nact = nact_ref[0]
    tm = MOE_TILE
    slot = i % GATHER_SLOTS
    ahead = GATHER_SLOTS - 1

    def weight_copies(expert, ws):
        return [pltpu.make_async_copy(hbm.at[expert], stage.at[ws], wsem.at[ws])
                for hbm, stage in ((wg_hbm, wg_f), (wu_hbm, wu_f), (wd_hbm, wd_f))]

    def issue_gather(tile, slot_):
        base = tile * tm
        for r in range(tm):
            _start_row_gather(x_hbm, src_ref[base + r], xbuf.at[slot_], r, xsem.at[slot_])

    def tile_step(prefetch):
        prev = te_ref[jnp.maximum(i - 1, 0)]

        @pl.when(jnp.logical_or(i == 0, te_ref[i] != prev))
        def _():
            ws = ord_ref[i] % 2
            for cp in weight_copies(te_ref[i], ws):
                cp.wait()
            wg_s[...] = wg_f[ws].astype(BF16)
            wu_s[...] = wu_f[ws].astype(BF16)
            wd_s[...] = wd_f[ws].astype(BF16)

            @pl.when(nxt_ref[i] >= 0)
            def _():
                for cp in weight_copies(nxt_ref[i], 1 - ws):
                    cp.start(priority=1)

        _wait_row_gathers(xbuf.at[slot], xbuf.at[(i + 1) % GATHER_SLOTS], tm, xsem.at[slot])
        if prefetch:
            issue_gather(i + ahead, (i + ahead) % GATHER_SLOTS)
        x = _load_gathered(xbuf.at[slot], tm).astype(BF16)
        hg = _dot(x, wg_s[...])
        hu = _dot(x, wu_s[...])
        h = hg * (1.0 / (1.0 + jnp.exp(-hg))) * hu
        _store_gatherable(o_ref, _dot(h.astype(BF16), wd_s[...]))

    @pl.when(i == 0)
    def _():
        for cp in weight_copies(te_ref[0], 0):
            cp.start(priority=1)
        for t in range(ahead):
            @pl.when(t < nact)
            def _(t=t):
                issue_gather(t, t)

    @pl.when(i + ahead < nact)
    def _():
        tile_step(True)

    @pl.when(jnp.logical_and(i < nact, i + ahead >= nact))
    def _():
        tile_step(False)

    @pl.when(i >= nact)
    def _():
        o_ref[...] = jnp.zeros_like(o_ref)


def _moe_experts(x_rows, w_gate, w_up, w_down, tile_expert, tile_ord, tile_next, src, nact, *,
                 n_tiles):
    tm = MOE_TILE
    out_map = lambda i, te, od, nx, s, n: (i, 0)
    any_spec = pl.BlockSpec(memory_space=pl.ANY)
    grid_spec = pltpu.PrefetchScalarGridSpec(
        num_scalar_prefetch=5,
        grid=(n_tiles,),
        in_specs=[any_spec, any_spec, any_spec, any_spec],
        out_specs=pl.BlockSpec((tm * ROW_PITCH, LANES), out_map),
        scratch_shapes=[pltpu.VMEM((GATHER_SLOTS, tm * ROW_PITCH, LANES), F32),
                        pltpu.SemaphoreType.DMA((GATHER_SLOTS,)),
                        pltpu.VMEM((2, D_MODEL, D_EXPERT), F32),
                        pltpu.VMEM((2, D_MODEL, D_EXPERT), F32),
                        pltpu.VMEM((2, D_EXPERT, D_MODEL), F32),
                        pltpu.SemaphoreType.DMA((2,)),
                        pltpu.VMEM((D_MODEL, D_EXPERT), BF16),
                        pltpu.VMEM((D_MODEL, D_EXPERT), BF16),
                        pltpu.VMEM((D_EXPERT, D_MODEL), BF16)],
    )
    return pl.pallas_call(
        _moe_body,
        out_shape=jax.ShapeDtypeStruct((n_tiles * tm * ROW_PITCH, LANES), F32),
        grid_spec=grid_spec,
        compiler_params=_cparams(("arbitrary",), 48),
        name="moe_experts",
    )(tile_expert, tile_ord, tile_next, src, nact, x_rows, w_gate, w_up, w_down)


def _combine_body(pos_ref, ys_hbm, wts_ref, x_ref, g_ref, b_ref, o1_ref, o2_ref, buf, sem, *,
                  tc, tiles1):
    i = pl.program_id(0)
    n = pl.num_programs(0)
    slot = i % GATHER_SLOTS
    ahead = GATHER_SLOTS - 1

    def issue_gather(tile, slot_):
        base = tile * tc * 2
        for r in range(tc):
            for k in range(2):
                _start_row_gather(ys_hbm, pos_ref[base + 2 * r + k], buf.at[slot_, k], r,
                                  sem.at[slot_])

    @pl.when(i == 0)
    def _():
        for t in range(ahead):
            @pl.when(t < n)
            def _(t=t):
                issue_gather(t, t)

    for k in range(2):
        _wait_row_gathers(buf.at[slot, k], buf.at[(i + 1) % GATHER_SLOTS, k], tc, sem.at[slot])

    @pl.when(i + ahead < n)
    def _():
        issue_gather(i + ahead, (i + ahead) % GATHER_SLOTS)

    w = wts_ref[...]
    moe = (w[:, 0:1] * _load_gathered(buf.at[slot, 0], tc)
           + w[:, 1:2] * _load_gathered(buf.at[slot, 1], tc))
    out = _layernorm(DEEPNORM_ALPHA * x_ref[...] + moe, g_ref[...], b_ref[...])

    @pl.when(i < tiles1)
    def _():
        o1_ref[...] = out

    @pl.when(i >= tiles1)
    def _():
        o2_ref[...] = out


def _moe_combine(ys, pos, wts, x, g, b, *, tc, n_first):
    m = x.shape[0]
    assert n_first % tc == 0 and (m - n_first) % tc == 0
    tiles1 = n_first // tc
    grid_spec = pltpu.PrefetchScalarGridSpec(
        num_scalar_prefetch=1,
        grid=(m // tc,),
        in_specs=[pl.BlockSpec(memory_space=pl.ANY),
                  pl.BlockSpec((tc, ROUTER_LANES), lambda i, p: (i, 0)),
                  pl.BlockSpec((tc, D_MODEL), lambda i, p: (i, 0)),
                  pl.BlockSpec((1, D_MODEL), lambda i, p: (0, 0)),
                  pl.BlockSpec((1, D_MODEL), lambda i, p: (0, 0))],
        out_specs=[pl.BlockSpec((tc, D_MODEL), lambda i, p: (jnp.minimum(i, tiles1 - 1), 0)),
                   pl.BlockSpec((tc, D_MODEL), lambda i, p: (jnp.maximum(i - tiles1, 0), 0))],
        scratch_shapes=[pltpu.VMEM((GATHER_SLOTS, 2, tc * ROW_PITCH, LANES), F32),
                        pltpu.SemaphoreType.DMA((GATHER_SLOTS,))],
    )
    return pl.pallas_call(
        functools.partial(_combine_body, tc=tc, tiles1=tiles1),
        out_shape=[jax.ShapeDtypeStruct((n_first, D_MODEL), F32),
                   jax.ShapeDtypeStruct((m - n_first, D_MODEL), F32)],
        grid_spec=grid_spec,
        compiler_params=_cparams(("arbitrary",), 16 * tc * D_MODEL * 4 / 2**20 + 8),
        name="moe_combine_ln3",
    )(pos, ys, wts, x, g, b)


def _moe(x, x_rows, w_r1, b_r1, w_r2, b_r2, w_gate, w_up, w_down, g, b, *, n_first, tm_router, tc):
    n = x.shape[0]
    ng, ne = N_EXPERT_GROUPS, N_EXPERTS
    pad = ROUTER_LANES - ng - ne
    w_r = jnp.concatenate([w_r1, w_r2.reshape(D_MODEL, ne), jnp.zeros((D_MODEL, pad), F32)], axis=1)
    b_r = jnp.concatenate([b_r1, b_r2.reshape(ne), jnp.zeros((pad,), F32)]).reshape(1, ROUTER_LANES)
    sel, wts, cnt = _router(x, w_r, b_r, tm=tm_router)

    tm = MOE_TILE
    n_tiles = (2 * n) // tm + ne
    counts = cnt[0, ng:ng + ne]
    tiles_per = (counts + tm - 1) // tm
    tile_end = jnp.cumsum(tiles_per)
    row_off = (tile_end - tiles_per) * tm
    nact = tile_end[-1]
    ids, ranks = sel[:, 0:2], sel[:, 2:4]
    pos = row_off[ids] + ranks
    tile_ids = jnp.minimum(jnp.arange(n_tiles, dtype=I32), nact - 1)
    tile_expert = jnp.sum((tile_end[None, :] <= tile_ids[:, None]).astype(I32), axis=1)
    token = jnp.broadcast_to(jnp.arange(n, dtype=I32)[:, None], (n, 2))
    src = jnp.zeros((n_tiles * tm,), I32).at[pos.reshape(-1)].set(token.reshape(-1))
    used = tiles_per > 0
    eid = jnp.arange(ne, dtype=I32)
    ordinal = jnp.cumsum(used.astype(I32)) - 1
    later = jnp.where(jnp.logical_and(used[None, :], eid[None, :] > eid[:, None]), eid[None, :], ne)
    nxt = jnp.min(later, axis=1)
    nxt = jnp.where(nxt == ne, -1, nxt)

    ys = _moe_experts(x_rows, w_gate, w_up, w_down, tile_expert, ordinal[tile_expert],
                      nxt[tile_expert], src, nact.reshape(1).astype(I32), n_tiles=n_tiles)
    return _moe_combine(ys, pos.reshape(-1).astype(I32), wts, x, g, b, tc=tc, n_first=n_first)


def _row_tile(m, cap):
    best = SUBLANES
    for t in range(SUBLANES, cap + 1, SUBLANES):
        if m % t == 0:
            best = t
    return best


def kernel(x_prompt, x_sample, cache_win_k, cache_win_v, state_ssm_re, state_ssm_im, cache_mem_k, cache_mem_v, mem_prompt, w_in, ssm_lam_re, ssm_lam_im, ssm_log_dt, ssm_b_re, ssm_b_im, ssm_c_re, ssm_c_im, ssm_d, w_glu, g_attn, g_ssm, w_out, ln1_g, ln1_b, w_mq, w_mk, w_mv, w_mo, ln2_g, ln2_b, w_r1, b_r1, w_r2, b_r2, w_gate, w_up, w_down, ln3_g, ln3_b):
    nb, seq, d = x_prompt.shape
    ns, dseq, _ = x_sample.shape
    n_p, n_s = nb * seq, ns * dseq
    n = n_p + n_s
    l = 0
    row2 = lambda v: v[l].reshape(1, -1)

    x_p, x_s = x_prompt.reshape(n_p, d), x_sample.reshape(n_s, d)
    tm_p = _row_tile(n_p, 1024)
    tm_ln = _row_tile(n_p, 512)
    assert n_p % n_s == 0 and n_s % SUBLANES == 0

    w_in_b = w_in[l].astype(BF16)
    proj_p = _matmul(x_p, w_in_b, tm=tm_p, tn=1024, name="proj_in_prompt")
    proj_s = _matmul(x_s, w_in_b, tm=n_s, tn=1024, name="proj_in_sample")

    attn_p = _attn_prompt(proj_p, n_batch=nb, seq=seq)
    attn_s = _attn_sample(proj_s, cache_win_k[l], cache_win_v[l], row0=0, n_seq=ns, n_new=dseq)

    seg_len = seq // SSM_SEGMENTS
    prm = _ssm_params(ssm_lam_re[l], ssm_lam_im[l], ssm_log_dt[l], ssm_b_re[l], ssm_b_im[l],
                      ssm_c_re[l], ssm_c_im[l], ssm_d[l], seg_len)
    u_p = proj_p[:, 3 * D_ATT:].reshape(nb, SSM_SEGMENTS, seg_len, D_SSM)
    u_p = jnp.transpose(u_p, (2, 0, 1, 3)).reshape(seg_len, nb * SSM_SEGMENTS, D_SSM)
    zeros = jnp.zeros((nb * SSM_SEGMENTS, N_SSM_GROUPS * SSM_STATE), F32)
    tl = _row_tile(seg_len, 32)
    end_re, end_im = _ssm_scan(u_p, prm, zeros, zeros, tl=tl, nseg=1, emit_y=False, exact_in=False)
    yg_p, fin_re, fin_im = _ssm_scan(u_p, prm, end_re, end_im, tl=tl, nseg=SSM_SEGMENTS,
                                     emit_y=True, exact_in=False)
    yg_p = jnp.transpose(yg_p.reshape(seg_len, nb, SSM_SEGMENTS, D_SSM), (1, 2, 0, 3))
    last = SSM_SEGMENTS - 1
    ssm_re_p = fin_re.reshape(nb, SSM_SEGMENTS, N_SSM_GROUPS, SSM_STATE)[:, last]
    ssm_im_p = fin_im.reshape(nb, SSM_SEGMENTS, N_SSM_GROUPS, SSM_STATE)[:, last]

    u_s = jnp.transpose(proj_s[:, 3 * D_ATT:].reshape(ns, dseq, D_SSM), (1, 0, 2))
    h0_re = state_ssm_re[l].reshape(ns, -1)
    h0_im = state_ssm_im[l].reshape(ns, -1)
    yg_s, ssm_re_s, ssm_im_s = _ssm_scan(u_s, prm, h0_re, h0_im, tl=dseq, nseg=1,
                                         emit_y=True, exact_in=True)
    yg_s = jnp.transpose(yg_s, (1, 0, 2))
    w_glu_b = w_glu[l].astype(BF16)
    ssm_out_p = _glu(yg_p.reshape(n_p, D_SSM), w_glu_b, tm=tm_p, name="ssm_glu_prompt")
    ssm_out_s = _glu(yg_s.reshape(n_s, D_SSM), w_glu_b, tm=n_s, name="ssm_glu_sample")

    mix_args = (row2(g_attn), row2(g_ssm), w_out[l].astype(BF16))
    ln1 = (row2(ln1_g), row2(ln1_b))
    x1_p = _mix(attn_p, ssm_out_p, *mix_args, x_p, *ln1, tm=tm_ln, name="mix_out_ln1_prompt")
    x1_s = _mix(attn_s, ssm_out_s, *mix_args, x_s, *ln1, tm=n_s, name="mix_out_ln1_sample")

    mem_rows = mem_prompt.reshape(nb * N_MEM, d)
    mem_k = _matmul(mem_rows, w_mk[l].astype(BF16), tm=nb * N_MEM, tn=1024, name="mem_k")
    mem_v = _matmul(mem_rows, w_mv[l].astype(BF16), tm=nb * N_MEM, tn=1024, name="mem_v")
    w_mq_b = w_mq[l].astype(BF16)
    q_p = _matmul(x1_p, w_mq_b, tm=tm_p, tn=1024, name="mem_q_prompt")
    q_s = _matmul(x1_s, w_mq_b, tm=n_s, tn=1024, name="mem_q_sample")
    o_p = _memattn(q_p, mem_k.reshape(nb, N_MEM, d), mem_v.reshape(nb, N_MEM, d),
                   row0=0, n_seq=nb, seq=seq, tq=_row_tile(seq, 512), name="memattn_prompt")
    o_s = _memattn_heads(q_s, cache_mem_k[l], cache_mem_v[l], row0=0, n_seq=ns, seq=dseq,
                         name="memattn_sample")
    x2, x2_rows = _mm_ln(o_p, o_s, w_mo[l].astype(BF16), x1_p, x1_s, row2(ln2_g), row2(ln2_b),
                         name="mem_out_ln2")

    y_p, y_s = _moe(x2, x2_rows, w_r1[l], b_r1[l], w_r2[l], b_r2[l], w_gate[l], w_up[l],
                    w_down[l], row2(ln3_g), row2(ln3_b), n_first=n_p,
                    tm_router=_row_tile(n_s, 256), tc=_row_tile(n_s, 128))

    y_p = y_p.reshape(nb, seq, d)
    y_s = y_s.reshape(ns, dseq, d)
    k_p = proj_p[:, D_ATT:2 * D_ATT].reshape(nb, seq, ATT_HEADS, ATT_HD)
    v_p = proj_p[:, 2 * D_ATT:3 * D_ATT].reshape(nb, seq, ATT_HEADS, ATT_HD)
    wp = min(max(w for w, _ in DILATIONS), seq)
    k_s = proj_s[:, D_ATT:2 * D_ATT].reshape(ns, dseq, ATT_HEADS, ATT_HD)
    v_s = proj_s[:, 2 * D_ATT:3 * D_ATT].reshape(ns, dseq, ATT_HEADS, ATT_HD)
    state = lambda v, b_: v.reshape(1, b_, N_SSM_GROUPS, SSM_STATE)
    return (y_p, y_s, k_p[None, :, seq - wp:], v_p[None, :, seq - wp:], k_s[None], v_s[None],
            state(ssm_re_p, nb), state(ssm_im_p, nb), state(ssm_re_s, ns), state(ssm_im_s, ns),
            mem_k.reshape(1, nb, N_MEM, MEM_HEADS, MEM_HD),
            mem_v.reshape(1, nb, N_MEM, MEM_HEADS, MEM_HD))
```

```python
import functools
import math

import numpy as np
import jax
import jax.numpy as jnp
from jax import lax
from jax.experimental import pallas as pl
from jax.experimental.pallas import tpu as pltpu

F32 = jnp.float32
BF16 = jnp.bfloat16
I32 = jnp.int32

D_MODEL = 2048
PAST_LEN = 8192
D_ATT = D_MODEL // 2
ATT_HEADS = 8
ATT_HD = D_ATT // ATT_HEADS
DILATIONS = ((128, 1), (512, 4), (2048, 16))
D_SSM = D_MODEL - D_ATT
SSM_GROUP_CH = 16
N_SSM_GROUPS = D_SSM // SSM_GROUP_CH
SSM_STATE = 64
N_MEM = 256
MEM_HEADS = 4
MEM_HD = D_MODEL // MEM_HEADS
N_EXPERT_GROUPS = 4
EXPERTS_PER_GROUP = 8
N_EXPERTS = N_EXPERT_GROUPS * EXPERTS_PER_GROUP
D_EXPERT = D_MODEL // 4
DEPTH = 1
DEEPNORM_ALPHA = (2.0 * DEPTH) ** 0.25
LN_EPS = 1e-5
RMS_EPS = 1e-6

LANES = 128
SUBLANES = 8
ROW_CHUNKS = D_MODEL // LANES
ROW_PITCH = ROW_CHUNKS + 1
Q_BLOCK = 128
ATTN_GROUP = 8
SSM_LANE_TILE = 128
SSM_GROUPS_PER_TILE = SSM_LANE_TILE // SSM_GROUP_CH
SSM_STATES_PER_TILE = SSM_GROUPS_PER_TILE * SSM_STATE
SSM_SEGMENTS = 8
MOE_TILE = 256
GATHER_SLOTS = 3
ROUTER_LANES = 128
NEG_INF = float("-inf")


def _cparams(semantics, vmem_mib):
    return pltpu.CompilerParams(dimension_semantics=semantics,
                                vmem_limit_bytes=int(vmem_mib) << 20)


def _layernorm(y, g, b):
    mu = jnp.mean(y, axis=-1, keepdims=True)
    yc = y - mu
    var = jnp.mean(yc * yc, axis=-1, keepdims=True)
    return yc * lax.rsqrt(var + LN_EPS) * g + b


def _rmsnorm(v, g):
    return v * lax.rsqrt(jnp.mean(v * v, axis=-1, keepdims=True) + RMS_EPS) * g


def _dot(a, b):
    return jnp.dot(a, b, preferred_element_type=F32)


def _dot_nt(a, b):
    return lax.dot_general(a, b, (((1,), (1,)), ((), ())), preferred_element_type=F32)


def _mm_body(x_ref, w_ref, o_ref):
    o_ref[...] = _dot(x_ref[...].astype(BF16), w_ref[...]).astype(o_ref.dtype)


def _matmul(x, w, *, tm, tn, name):
    m, k = x.shape
    n = w.shape[1]
    vmem = 2 * (tm * k * x.dtype.itemsize + k * tn * 2 + tm * tn * 4) / 2**20 + 8
    return pl.pallas_call(
        _mm_body,
        out_shape=jax.ShapeDtypeStruct((m, n), F32),
        grid=(n // tn, m // tm),
        in_specs=[pl.BlockSpec((tm, k), lambda j, i: (i, 0)),
                  pl.BlockSpec((k, tn), lambda j, i: (0, j))],
        out_specs=pl.BlockSpec((tm, tn), lambda j, i: (i, j)),
        compiler_params=_cparams(("parallel", "parallel"), vmem),
        name=name,
    )(x, w)


def _attn_prompt_body(q_ref, k_ref, v_ref, o_ref, kt_s, va_s, on_s, lse_s, *, seq, dilations):
    scale = ATT_HD ** -0.5
    nblk = seq // Q_BLOCK
    qi = lax.broadcasted_iota(I32, (Q_BLOCK, Q_BLOCK), 0)
    kj = lax.broadcasted_iota(I32, (Q_BLOCK, Q_BLOCK), 1)
    cur_ok = kj <= qi
    prev_ok = kj >= qi
    va_s[:, :, ATT_HD:] = jnp.ones((nblk, Q_BLOCK, ATT_HD), BF16)

    for br, (_, d) in enumerate(dilations):
        span = d * Q_BLOCK
        nb = seq // span

        def stream_rows(t, d=d, span=span, nb=nb):
            r = t // nb
            ib = t % nb
            return r, ib, pl.ds(r + ib * span, Q_BLOCK, stride=d)

        def prep(g, carry, stream_rows=stream_rows):
            loaded = []
            for j in range(ATTN_GROUP):
                t = g * ATTN_GROUP + j
                _, _, rows = stream_rows(t)
                loaded.append((t, k_ref[rows, :], v_ref[rows, :]))
            for t, kk, vv in loaded:
                kt_s[t] = jnp.transpose(kk).astype(BF16)
                va_s[t, :, 0:ATT_HD] = vv.astype(BF16)
            return carry

        lax.fori_loop(0, nblk // ATTN_GROUP, prep, 0)

        def group(g, carry, br=br, nb=nb, stream_rows=stream_rows):
            scores = []
            for j in range(ATTN_GROUP):
                t = g * ATTN_GROUP + j
                r, ib, rows = stream_rows(t)
                tp = jnp.maximum(t - 1, r * nb)
                q = (q_ref[rows, :] * scale).astype(BF16)
                s = _dot(q, jnp.concatenate([kt_s[tp], kt_s[t]], axis=1))
                scores.append((t, tp, ib, rows, s))
            probs = []
            for t, tp, ib, rows, s in scores:
                ok = jnp.concatenate([jnp.logical_and(prev_ok, ib > 0), cur_ok], axis=1)
                s = jnp.where(ok, s, NEG_INF)
                m = jnp.max(s, axis=-1, keepdims=True)
                probs.append((t, tp, rows, m, jnp.exp(s - m).astype(BF16)))
            outs = [(rows, m, _dot(p, jnp.concatenate([va_s[tp], va_s[t]], axis=0)))
                    for t, tp, rows, m, p in probs]
            for rows, m, al in outs:
                l = al[:, ATT_HD:]
                on_s[br, rows, :] = al[:, :ATT_HD] / l
                lse_s[br, rows, :] = m + jnp.log(l)
            return carry

        lax.fori_loop(0, nblk // ATTN_GROUP, group, 0)

    chunk = 256
    nbr = len(dilations)

    def merge(c, carry):
        rows = pl.ds(pl.multiple_of(c * chunk, chunk), chunk)
        ls = [lse_s[b, rows, :] for b in range(nbr)]
        mx = functools.reduce(jnp.maximum, ls)
        es = [jnp.exp(li - mx) for li in ls]
        num = sum(es[b] * on_s[b, rows, :] for b in range(nbr))
        o_ref[rows, :] = num / sum(es)
        return carry

    lax.fori_loop(0, seq // chunk, merge, 0)


def _attn_prompt(proj, *, n_batch, seq, dilations=DILATIONS):
    for w, d in dilations:
        assert w // d == Q_BLOCK and seq % (d * Q_BLOCK) == 0
    nbr = len(dilations)
    nblk = seq // Q_BLOCK
    assert nblk % ATTN_GROUP == 0
    blk = lambda off: pl.BlockSpec((seq, ATT_HD), lambda b, h, off=off: (b, off + h))
    vmem = ((4 * 2 + 2 * nbr) * seq * ATT_HD * 4 + 3 * seq * ATT_HD * 2) / 2**20 + 8
    return pl.pallas_call(
        functools.partial(_attn_prompt_body, seq=seq, dilations=dilations),
        out_shape=jax.ShapeDtypeStruct((n_batch * seq, D_ATT), F32),
        grid=(n_batch, ATT_HEADS),
        in_specs=[blk(0), blk(ATT_HEADS), blk(2 * ATT_HEADS)],
        out_specs=pl.BlockSpec((seq, ATT_HD), lambda b, h: (b, h)),
        scratch_shapes=[pltpu.VMEM((nblk, ATT_HD, Q_BLOCK), BF16),
                        pltpu.VMEM((nblk, Q_BLOCK, 2 * ATT_HD), BF16),
                        pltpu.VMEM((nbr, seq, ATT_HD), F32),
                        pltpu.VMEM((nbr, seq, ATT_HD), F32)],
        compiler_params=_cparams(("parallel", "parallel"), vmem),
        name="attn_prompt",
    )(proj, proj, proj)


def _sample_key_multiplicity(n_new, n_cache, past_len, dilations):
    d_max = max(d for _, d in dilations)
    tail = max(w for w, d in dilations if d != d_max)
    assert past_len % d_max == 0 and n_cache % d_max == 0 and n_new <= d_max // 2
    assert tail % d_max == 0 and tail <= n_cache
    half = d_max // 2
    n_grid = (n_cache - tail) // d_max
    kv_start = past_len - n_cache
    grid_rows = (np.arange(n_grid)[:, None] * d_max + np.arange(half)[None, :]).reshape(-1)
    tail_rows = n_cache - tail + np.arange(tail)
    new_rows = n_cache + np.arange(n_new)
    qpos = past_len + np.arange(n_new)

    def mult(rows):
        kpos = kv_start + rows
        delta = qpos[:, None] - kpos[None, :]
        c = np.zeros(delta.shape, np.float32)
        for w, d in dilations:
            c += ((delta >= 0) & (delta <= w) & (delta % d == 0) & (kpos[None, :] >= kv_start))
        return c

    fetched = np.zeros(n_cache + n_new, bool)
    fetched[grid_rows] = True
    fetched[tail_rows] = True
    fetched[new_rows] = True
    assert not mult(np.nonzero(~fetched)[0]).any()
    return mult(grid_rows), mult(tail_rows), mult(new_rows), n_grid, tail, half, d_max


def _attn_sample_body(q_ref, kn_ref, vn_ref, kg_ref, kt_ref, vg_ref, vt_ref,
                      cg_ref, ct_ref, cn_ref, o_ref):
    scale = ATT_HD ** -0.5
    heads = lambda ref: jnp.concatenate(
        [ref[:, h * ATT_HD:(h + 1) * ATT_HD] for h in range(ATT_HEADS)], axis=0)
    q = (heads(q_ref) * scale).astype(BF16)
    kn = heads(kn_ref).astype(BF16)
    vn = heads(vn_ref).astype(BF16)
    flat = lambda ref: ref[...].reshape(-1, ATT_HD).astype(BF16)
    cg, ct, cn = cg_ref[...], ct_ref[...], cn_ref[...]
    sg = jnp.where(cg > 0, _dot_nt(q, flat(kg_ref)), NEG_INF)
    st = jnp.where(ct > 0, _dot_nt(q, flat(kt_ref)), NEG_INF)
    sn = jnp.where(cn > 0, _dot_nt(q, kn), NEG_INF)
    m = jnp.maximum(jnp.maximum(jnp.max(sg, axis=-1, keepdims=True),
                                jnp.max(st, axis=-1, keepdims=True)),
                    jnp.max(sn, axis=-1, keepdims=True))
    pg = cg * jnp.exp(sg - m)
    pt = ct * jnp.exp(st - m)
    pn = cn * jnp.exp(sn - m)
    l = (jnp.sum(pg, axis=-1, keepdims=True) + jnp.sum(pt, axis=-1, keepdims=True)
         + jnp.sum(pn, axis=-1, keepdims=True))
    acc = (_dot(pg.astype(BF16), flat(vg_ref)) + _dot(pt.astype(BF16), flat(vt_ref))
           + _dot(pn.astype(BF16), vn))
    out = acc / l
    n_new = q_ref.shape[0]
    for h in range(ATT_HEADS):
        o_ref[:, h * ATT_HD:(h + 1) * ATT_HD] = out[h * n_new:(h + 1) * n_new, :]


def _attn_sample(proj, win_k, win_v, *, row0, n_seq, n_new, past_len=PAST_LEN,
                 dilations=DILATIONS):
    n_cache = win_k.shape[1]
    cg, ct, cn, n_grid, tail, half, d_max = _sample_key_multiplicity(
        n_new, n_cache, past_len, dilations)
    assert row0 % n_new == 0 and n_new % SUBLANES == 0 and n_cache % tail == 0
    eye = np.eye(ATT_HEADS, dtype=np.float32)
    key_major = lambda c: np.einsum("tk,hg->htkg", c, eye).reshape(ATT_HEADS * n_new, -1)
    head_major = lambda c: np.einsum("tk,hg->htgk", c, eye).reshape(ATT_HEADS * n_new, -1)
    cg, ct, cn = key_major(cg), key_major(ct), head_major(cn)
    rb = row0 // n_new
    n_groups = n_cache // d_max
    kgv = win_k.reshape(n_seq, n_groups, d_max, ATT_HEADS, ATT_HD)
    vgv = win_v.reshape(n_seq, n_groups, d_max, ATT_HEADS, ATT_HD)
    ktv = win_k.reshape(n_seq, n_cache // tail, tail, ATT_HEADS, ATT_HD)
    vtv = win_v.reshape(n_seq, n_cache // tail, tail, ATT_HEADS, ATT_HD)
    new = lambda off: pl.BlockSpec((n_new, D_ATT), lambda b, off=off: (rb + b, off))
    grid_spec = pl.BlockSpec((None, n_grid, half, ATT_HEADS, ATT_HD), lambda b: (b, 0, 0, 0, 0))
    tail_spec = pl.BlockSpec((None, None, tail, ATT_HEADS, ATT_HD),
                             lambda b: (b, n_cache // tail - 1, 0, 0, 0))
    const = lambda a: pl.BlockSpec(a.shape, lambda b: (0, 0))
    vmem = (2 * 2 * (n_grid * half + tail) * D_ATT * 4 + 4 * cg.size * 4 * 3) / 2**20 + 12
    return pl.pallas_call(
        _attn_sample_body,
        out_shape=jax.ShapeDtypeStruct((n_seq * n_new, D_ATT), F32),
        grid=(n_seq,),
        in_specs=[new(0), new(1), new(2), grid_spec, tail_spec, grid_spec, tail_spec,
                  const(cg), const(ct), const(cn)],
        out_specs=pl.BlockSpec((n_new, D_ATT), lambda b: (b, 0)),
        compiler_params=_cparams(("parallel",), vmem),
        name="attn_sample",
    )(proj, proj, proj, kgv, ktv, vgv, vtv, jnp.asarray(cg), jnp.asarray(ct), jnp.asarray(cn))


def _gelu_tanh(x):
    return 0.5 * x * (1.0 + jnp.tanh(math.sqrt(2.0 / math.pi) * (x + 0.044715 * (x * x * x))))


def _ssm_body(u_ref, bb_ref, cst_ref, a_ref, ap_ref, d_ref, hre_ref, him_ref, *rest,
              tl, npar, seq_len, nseg, emit_y, exact_in):
    if emit_y:
        y_ref, fre_ref, fim_ref, x_s, h_s = rest
    else:
        fre_ref, fim_ref, x_s, h_s = rest
    ns = SSM_STATES_PER_TILE
    c = pl.program_id(1)
    ngrp = npar // SUBLANES

    def step_rows(i, g):
        return pl.ds(c * tl + i + g * SUBLANES * seq_len, SUBLANES, stride=seq_len)

    @pl.when(c == 0)
    def _init():
        if nseg == 1:
            h_s[0] = hre_ref[...]
            h_s[1] = him_ref[...]
        else:
            pr, pi = ap_ref[0:1, :], ap_ref[1:2, :]
            for b in range(npar // nseg):
                sr = jnp.zeros((1, ns), F32)
                si = jnp.zeros((1, ns), F32)
                for j in range(nseg):
                    row = b * nseg + j
                    h_s[0, row:row + 1, :] = sr
                    h_s[1, row:row + 1, :] = si
                    er, ei = hre_ref[row:row + 1, :], him_ref[row:row + 1, :]
                    sr, si = pr * sr - pi * si + er, pr * si + pi * sr + ei

    u = jnp.concatenate([u_ref[step_rows(i, g), :] for i in range(tl) for g in range(ngrp)], axis=0)
    if exact_in:
        x_s[...] = jnp.dot(u, bb_ref[...], precision=lax.Precision.HIGHEST,
                           preferred_element_type=F32)
    else:
        x_s[...] = _dot(u.astype(BF16), bb_ref[...])

    ar = jnp.broadcast_to(a_ref[0:1, :], (SUBLANES, ns))
    ai = jnp.broadcast_to(a_ref[1:2, :], (SUBLANES, ns))

    def step(i, carry):
        out = []
        for g in range(ngrp):
            hr, hi = carry[2 * g], carry[2 * g + 1]
            rows = pl.ds(pl.multiple_of(i * npar + g * SUBLANES, SUBLANES), SUBLANES)
            nr = ar * hr - ai * hi + x_s[rows, 0:ns]
            ni = ar * hi + ai * hr + x_s[rows, ns:2 * ns]
            if emit_y:
                x_s[rows, 0:ns] = nr
                x_s[rows, ns:2 * ns] = ni
            out += [nr, ni]
        return tuple(out)

    init = []
    for g in range(ngrp):
        gs = slice(g * SUBLANES, (g + 1) * SUBLANES)
        init += [h_s[0, gs, :], h_s[1, gs, :]]
    fin = lax.fori_loop(0, tl, step, tuple(init), unroll=4)
    for g in range(ngrp):
        gs = slice(g * SUBLANES, (g + 1) * SUBLANES)
        h_s[0, gs, :] = fin[2 * g]
        h_s[1, gs, :] = fin[2 * g + 1]

    if emit_y:
        y = _gelu_tanh(_dot(x_s[...].astype(BF16), cst_ref[...]) + d_ref[...] * u)
        for i in range(tl):
            for g in range(ngrp):
                r0 = i * npar + g * SUBLANES
                y_ref[step_rows(i, g), :] = y[r0:r0 + SUBLANES, :]

    @pl.when(c == pl.num_programs(1) - 1)
    def _fin():
        fre_ref[...] = h_s[0]
        fim_ref[...] = h_s[1]


def _ssm_scan(proj, prm, hin_re, hin_im, *, seq_len, tl, nseg, emit_y, exact_in, name):
    rows = proj.shape[0]
    npar = rows // seq_len
    assert npar % SUBLANES == 0 and seq_len % tl == 0
    ns = SSM_STATES_PER_TILE
    nk = D_SSM // SSM_LANE_TILE
    col0 = (proj.shape[1] - D_SSM) // SSM_LANE_TILE
    bb = prm["bb_f32"] if exact_in else prm["bb_bf16"]
    in_specs = [
        pl.BlockSpec((rows, SSM_LANE_TILE), lambda k, c: (0, col0 + k)),
        pl.BlockSpec((None, SSM_LANE_TILE, 2 * ns), lambda k, c: (k, 0, 0)),
        pl.BlockSpec((None, 2 * ns, SSM_LANE_TILE), lambda k, c: (k, 0, 0)),
        pl.BlockSpec((None, 2, ns), lambda k, c: (k, 0, 0)),
        pl.BlockSpec((None, 2, ns), lambda k, c: (k, 0, 0)),
        pl.BlockSpec((1, SSM_LANE_TILE), lambda k, c: (0, k)),
        pl.BlockSpec((npar, ns), lambda k, c: (0, k)),
        pl.BlockSpec((npar, ns), lambda k, c: (0, k)),
    ]
    state_shape = jax.ShapeDtypeStruct((npar, nk * ns), F32)
    state_spec = pl.BlockSpec((npar, ns), lambda k, c: (0, k))
    out_shape = [state_shape, state_shape]
    out_specs = [state_spec, state_spec]
    if emit_y:
        out_shape = [jax.ShapeDtypeStruct((rows, D_SSM), F32)] + out_shape
        out_specs = [pl.BlockSpec((rows, SSM_LANE_TILE), lambda k, c: (0, k))] + out_specs
    vmem = (4 * rows * SSM_LANE_TILE * 4 + tl * npar * 2 * ns * 4) / 2**20 + 16
    return pl.pallas_call(
        functools.partial(_ssm_body, tl=tl, npar=npar, seq_len=seq_len, nseg=nseg, emit_y=emit_y,
                          exact_in=exact_in),
        out_shape=out_shape,
        grid=(nk, seq_len // tl),
        in_specs=in_specs,
        out_specs=out_specs,
        scratch_shapes=[pltpu.VMEM((tl * npar, 2 * ns), F32), pltpu.VMEM((2, npar, ns), F32)],
        compiler_params=_cparams(("parallel", "arbitrary"), vmem),
        name=name,
    )(proj, bb, prm["cst"], prm["a"], prm["apow"], prm["d"], hin_re, hin_im)


def _ssm_params(lam_re, lam_im, log_dt, b_re, b_im, c_re, c_im, d_skip, seg_len):
    g, p, c = N_SSM_GROUPS, SSM_STATE, SSM_GROUP_CH
    nk, gt = g // SSM_GROUPS_PER_TILE, SSM_GROUPS_PER_TILE
    dt = jnp.exp(log_dt.astype(F32))[:, None]
    lr, li = lam_re.astype(F32), lam_im.astype(F32)
    mag = jnp.exp(lr * dt)
    a_re, a_im = mag * jnp.cos(li * dt), mag * jnp.sin(li * dt)
    magp = jnp.exp(lr * dt * seg_len)
    p_re, p_im = magp * jnp.cos(li * dt * seg_len), magp * jnp.sin(li * dt * seg_len)
    den = lr * lr + li * li
    nr, ni = a_re - 1.0, a_im
    f_re, f_im = (nr * lr + ni * li) / den, (ni * lr - nr * li) / den
    br, bi = b_re.astype(F32), b_im.astype(F32)
    bb_re = f_re[..., None] * br - f_im[..., None] * bi
    bb_im = f_re[..., None] * bi + f_im[..., None] * br
    eye = jnp.eye(gt, dtype=F32)

    def pack_b(m):
        return jnp.einsum("kgpc,gh->kgchp", m.reshape(nk, gt, p, c), eye).reshape(nk, gt * c, gt * p)

    def pack_c(m):
        return jnp.einsum("kgcp,gh->kgphc", m.reshape(nk, gt, c, p), eye).reshape(nk, gt * p, gt * c)

    bb = jnp.concatenate([pack_b(bb_re), pack_b(bb_im)], axis=2)
    cst = jnp.concatenate([pack_c(c_re.astype(F32)), -pack_c(c_im.astype(F32))], axis=1)
    tile = lambda v: v.reshape(nk, 1, gt * p)
    return {
        "bb_f32": bb, "bb_bf16": bb.astype(BF16), "cst": cst.astype(BF16),
        "a": jnp.concatenate([tile(a_re), tile(a_im)], axis=1),
        "apow": jnp.concatenate([tile(p_re), tile(p_im)], axis=1),
        "d": d_skip.astype(F32).reshape(1, g * c),
    }


def _glu_body(y_ref, w_ref, o_ref):
    yg = y_ref[...]
    z = _dot(yg.astype(BF16), w_ref[...])
    o_ref[...] = yg * (1.0 / (1.0 + jnp.exp(-z)))


def _glu(yg, w, *, tm, name):
    m, n = yg.shape
    return pl.pallas_call(
        _glu_body,
        out_shape=jax.ShapeDtypeStruct((m, n), F32),
        grid=(m // tm,),
        in_specs=[pl.BlockSpec((tm, n), lambda i: (i, 0)), pl.BlockSpec((n, n), lambda i: (0, 0))],
        out_specs=pl.BlockSpec((tm, n), lambda i: (i, 0)),
        compiler_params=_cparams(("parallel",), 4 * tm * n * 4 / 2**20 + 12),
        name=name,
    )(yg, w)


def _mix_body(attn_ref, ssm_ref, ga_ref, gs_ref, w_ref, x_ref, g_ref, b_ref, o_ref):
    a = _rmsnorm(attn_ref[...], ga_ref[...]).astype(BF16)
    s = _rmsnorm(ssm_ref[...], gs_ref[...]).astype(BF16)
    mix = _dot(a, w_ref[0:D_ATT, :]) + _dot(s, w_ref[D_ATT:D_ATT + D_SSM, :])
    o_ref[...] = _layernorm(DEEPNORM_ALPHA * x_ref[...] + mix, g_ref[...], b_ref[...])


def _mix(attn, ssm, ga, gs, w, x, g, b, *, tm, name):
    m = x.shape[0]
    row = lambda n: pl.BlockSpec((tm, n), lambda i: (i, 0))
    const = lambda a: pl.BlockSpec(a.shape, lambda i: (0, 0))
    return pl.pallas_call(
        _mix_body,
        out_shape=jax.ShapeDtypeStruct((m, D_MODEL), F32),
        grid=(m // tm,),
        in_specs=[row(D_ATT), row(D_SSM), const(ga), const(gs), const(w), row(D_MODEL),
                  const(g), const(b)],
        out_specs=row(D_MODEL),
        compiler_params=_cparams(("parallel",), 6 * tm * D_MODEL * 4 / 2**20 + 24),
        name=name,
    )(attn, ssm, ga, gs, w, x, g, b)


def _store_gatherable(o_ref, y):
    rows = y.shape[0]
    for c in range(ROW_CHUNKS):
        o_ref[pl.ds(c, rows, stride=ROW_PITCH), :] = y[:, c * LANES:(c + 1) * LANES]
    for c in range(ROW_CHUNKS, ROW_PITCH):
        o_ref[pl.ds(c, rows, stride=ROW_PITCH), :] = jnp.zeros((rows, LANES), F32)


def _load_gathered(buf, rows):
    return jnp.concatenate([buf[pl.ds(c, rows, stride=ROW_PITCH), :] for c in range(ROW_CHUNKS)],
                           axis=1)


def _start_row_gather(src_hbm, idx, buf, r, sem):
    pltpu.make_async_copy(src_hbm.at[pl.ds(idx * ROW_PITCH, ROW_CHUNKS), :],
                          buf.at[pl.ds(r * ROW_PITCH, ROW_CHUNKS), :], sem).start()


def _wait_row_gathers(buf, other, rows, sem):
    span = pl.ds(0, rows * ROW_CHUNKS)
    pltpu.make_async_copy(other.at[span, :], buf.at[span, :], sem).wait()


def _mm_ln_body(a1_ref, a2_ref, w_ref, x1_ref, x2_ref, g_ref, b_ref, o_ref, rows_ref, *, tiles1):
    first = pl.program_id(0) < tiles1
    a = jnp.where(first, a1_ref[...], a2_ref[...])
    x = jnp.where(first, x1_ref[...], x2_ref[...])
    y = _dot(a.astype(BF16), w_ref[...])
    out = _layernorm(DEEPNORM_ALPHA * x + y, g_ref[...], b_ref[...])
    o_ref[...] = out
    _store_gatherable(rows_ref, out)


def _mm_ln(a1, a2, w, x1, x2, g, b, *, name):
    tm = a2.shape[0]
    assert a1.shape[0] % tm == 0
    tiles1 = a1.shape[0] // tm
    m = a1.shape[0] + tm
    row1 = lambda n: pl.BlockSpec((tm, n), lambda i: (jnp.minimum(i, tiles1 - 1), 0))
    row2 = lambda n: pl.BlockSpec((tm, n), lambda i: (0, 0))
    const = lambda v: pl.BlockSpec(v.shape, lambda i: (0, 0))
    return pl.pallas_call(
        functools.partial(_mm_ln_body, tiles1=tiles1),
        out_shape=[jax.ShapeDtypeStruct((m, D_MODEL), F32),
                   jax.ShapeDtypeStruct((m * ROW_PITCH, LANES), F32)],
        grid=(tiles1 + 1,),
        in_specs=[row1(a1.shape[1]), row2(a2.shape[1]), const(w), row1(D_MODEL), row2(D_MODEL),
                  const(g), const(b)],
        out_specs=[pl.BlockSpec((tm, D_MODEL), lambda i: (i, 0)),
                   pl.BlockSpec((tm * ROW_PITCH, LANES), lambda i: (i, 0))],
        compiler_params=_cparams(("parallel",), 12 * tm * D_MODEL * 4 / 2**20 + 24),
        name=name,
    )(a1, a2, w, x1, x2, g, b)


def _memattn_body(q_ref, k_ref, v_ref, o_ref):
    scale = MEM_HD ** -0.5
    for h in range(MEM_HEADS):
        sl = slice(h * MEM_HD, (h + 1) * MEM_HD)
        s = _dot_nt(q_ref[:, sl].astype(BF16), k_ref[:, sl].astype(BF16)) * scale
        m = jnp.max(s, axis=-1, keepdims=True)
        p = jnp.exp(s - m)
        l = jnp.sum(p, axis=-1, keepdims=True)
        o_ref[:, sl] = _dot(p.astype(BF16), v_ref[:, sl].astype(BF16)) / l


def _memattn(q, mem_k, mem_v, *, row0, n_seq, seq, tq, name):
    assert seq % tq == 0 and row0 % tq == 0
    nq = seq // tq
    rb = row0 // tq
    mem_spec = pl.BlockSpec((None, N_MEM, D_MODEL), lambda b, i: (b, 0, 0))
    return pl.pallas_call(
        _memattn_body,
        out_shape=jax.ShapeDtypeStruct((n_seq * seq, D_MODEL), F32),
        grid=(n_seq, nq),
        in_specs=[pl.BlockSpec((tq, D_MODEL), lambda b, i: (rb + b * nq + i, 0)),
                  mem_spec, mem_spec],
        out_specs=pl.BlockSpec((tq, D_MODEL), lambda b, i: (b * nq + i, 0)),
        compiler_params=_cparams(("parallel", "parallel"),
                                 4 * (tq + N_MEM) * D_MODEL * 4 / 2**20 + 8),
        name=name,
    )(q, mem_k, mem_v)


def _memattn_heads_body(q_ref, k_ref, v_ref, c_ref, o_ref):
    scale = MEM_HD ** -0.5
    tq = q_ref.shape[0]
    q = jnp.concatenate([q_ref[:, h * MEM_HD:(h + 1) * MEM_HD] for h in range(MEM_HEADS)], axis=0)
    k = k_ref[...].reshape(N_MEM * MEM_HEADS, MEM_HD).astype(BF16)
    v = v_ref[...].reshape(N_MEM * MEM_HEADS, MEM_HD).astype(BF16)
    s = jnp.where(c_ref[...] > 0, _dot_nt(q.astype(BF16), k) * scale, NEG_INF)
    m = jnp.max(s, axis=-1, keepdims=True)
    p = jnp.exp(s - m)
    l = jnp.sum(p, axis=-1, keepdims=True)
    o = _dot(p.astype(BF16), v) / l
    for h in range(MEM_HEADS):
        o_ref[:, h * MEM_HD:(h + 1) * MEM_HD] = o[h * tq:(h + 1) * tq, :]


def _memattn_heads(q, mem_k, mem_v, *, row0, n_seq, seq, name):
    assert row0 % seq == 0 and seq % SUBLANES == 0
    rb = row0 // seq
    same_head = np.kron(np.eye(MEM_HEADS, dtype=np.float32), np.ones((seq, 1), np.float32))
    same_head = np.tile(same_head, (1, N_MEM))
    mem_spec = pl.BlockSpec((None, N_MEM, MEM_HEADS, MEM_HD), lambda b: (b, 0, 0, 0))
    return pl.pallas_call(
        _memattn_heads_body,
        out_shape=jax.ShapeDtypeStruct((n_seq * seq, D_MODEL), F32),
        grid=(n_seq,),
        in_specs=[pl.BlockSpec((seq, D_MODEL), lambda b: (rb + b, 0)), mem_spec, mem_spec,
                  pl.BlockSpec(same_head.shape, lambda b: (0, 0))],
        out_specs=pl.BlockSpec((seq, D_MODEL), lambda b: (b, 0)),
        compiler_params=_cparams(("parallel",), 8 * N_MEM * D_MODEL * 4 / 2**20 + 8),
        name=name,
    )(q, mem_k, mem_v, jnp.asarray(same_head))


def _router_body(x_ref, w_ref, b_ref, sel_ref, wts_ref, cnt_ref, run_s, *, tm):
    i = pl.program_id(0)

    @pl.when(i == 0)
    def _():
        run_s[...] = jnp.zeros_like(run_s)

    ng, epg = N_EXPERT_GROUPS, EXPERTS_PER_GROUP
    x = x_ref[...]
    x_hi = x.astype(BF16)
    x_lo = (x - x_hi.astype(F32)).astype(BF16)
    parts = _dot(x_hi, w_ref[...]) + _dot(x_lo, w_ref[...])
    logits = parts + pltpu.roll(parts, shift=ROUTER_LANES // 2, axis=1) + b_ref[...]
    lane = lax.broadcasted_iota(I32, (tm, ROUTER_LANES), 1)
    big = ROUTER_LANES

    def first_argmax(vals):
        mx = jnp.max(vals, axis=-1, keepdims=True)
        idx = jnp.min(jnp.where(vals == mx, lane, big), axis=-1, keepdims=True)
        return mx, idx

    gl = jnp.where(lane < ng, logits, NEG_INF)
    gmax, gsel = first_argmax(gl)
    g_w = 1.0 / jnp.sum(jnp.exp(gl - gmax), axis=-1, keepdims=True)
    lo = ng + gsel * epg
    el = jnp.where(jnp.logical_and(lane >= lo, lane < lo + epg), logits, NEG_INF)
    v1, i1 = first_argmax(el)
    v2, i2 = first_argmax(jnp.where(lane == i1, NEG_INF, el))
    e21 = jnp.exp(v2 - v1)
    w1 = g_w / (1.0 + e21)
    w2 = g_w * e21 / (1.0 + e21)

    onehot = jnp.logical_or(lane == i1, lane == i2)
    r = lax.broadcasted_iota(I32, (tm, tm), 0)
    cc = lax.broadcasted_iota(I32, (tm, tm), 1)
    tri = (cc < r).astype(BF16)
    before = _dot(tri, onehot.astype(BF16)) + run_s[...]
    rank1 = jnp.sum(jnp.where(lane == i1, before, 0.0), axis=-1, keepdims=True).astype(I32)
    rank2 = jnp.sum(jnp.where(lane == i2, before, 0.0), axis=-1, keepdims=True).astype(I32)
    run_s[...] = run_s[...] + jnp.sum(onehot.astype(F32), axis=0, keepdims=True)

    sel = jnp.where(lane == 0, i1 - ng, jnp.where(lane == 1, i2 - ng,
                    jnp.where(lane == 2, rank1, jnp.where(lane == 3, rank2, 0))))
    sel_ref[...] = sel
    wts_ref[...] = jnp.where(lane == 0, w1, jnp.where(lane == 1, w2, 0.0))
    cnt_ref[...] = run_s[...].astype(I32)


def _router(x, w, b, *, tm):
    m = x.shape[0]
    row = pl.BlockSpec((tm, ROUTER_LANES), lambda i: (i, 0))
    return pl.pallas_call(
        functools.partial(_router_body, tm=tm),
        out_shape=[jax.ShapeDtypeStruct((m, ROUTER_LANES), I32),
                   jax.ShapeDtypeStruct((m, ROUTER_LANES), F32),
                   jax.ShapeDtypeStruct((1, ROUTER_LANES), I32)],
        grid=(m // tm,),
        in_specs=[pl.BlockSpec((tm, D_MODEL), lambda i: (i, 0)),
                  pl.BlockSpec((D_MODEL, ROUTER_LANES), lambda i: (0, 0)),
                  pl.BlockSpec((1, ROUTER_LANES), lambda i: (0, 0))],
        out_specs=[row, row, pl.BlockSpec((1, ROUTER_LANES), lambda i: (0, 0))],
        scratch_shapes=[pltpu.VMEM((1, ROUTER_LANES), F32)],
        compiler_params=_cparams(("arbitrary",), 16),
        name="moe_router",
    )(x, w, b)


def _moe_body(te_ref, ord_ref, nxt_ref, src_ref, nact_ref, x_hbm, wg_hbm, wu_hbm, wd_hbm, o_ref,
              xbuf, xsem, wg_f, wu_f, wd_f, wsem, wg_s, wu_s, wd_s):
    i = pl.program_id(0)
    nact = nact_ref[0]
    tm = MOE_TILE
    slot = i % GATHER_SLOTS
    ahead = GATHER_SLOTS - 1

    def weight_copies(expert, ws):
        return [pltpu.make_async_copy(hbm.at[expert], stage.at[ws], wsem.at[ws])
                for hbm, stage in ((wg_hbm, wg_f), (wu_hbm, wu_f), (wd_hbm, wd_f))]

    def issue_gather(tile, slot_):
        base = tile * tm
        for r in range(tm):
            _start_row_gather(x_hbm, src_ref[base + r], xbuf.at[slot_], r, xsem.at[slot_])

    def tile_step(prefetch):
        prev = te_ref[jnp.maximum(i - 1, 0)]

        @pl.when(jnp.logical_or(i == 0, te_ref[i] != prev))
        def _():
            ws = ord_ref[i] % 2
            for cp in weight_copies(te_ref[i], ws):
                cp.wait()
            wg_s[...] = wg_f[ws].astype(BF16)
            wu_s[...] = wu_f[ws].astype(BF16)
            wd_s[...] = wd_f[ws].astype(BF16)

            @pl.when(nxt_ref[i] >= 0)
            def _():
                for cp in weight_copies(nxt_ref[i], 1 - ws):
                    cp.start(priority=1)

        _wait_row_gathers(xbuf.at[slot], xbuf.at[(i + 1) % GATHER_SLOTS], tm, xsem.at[slot])
        if prefetch:
            issue_gather(i + ahead, (i + ahead) % GATHER_SLOTS)
        x = _load_gathered(xbuf.at[slot], tm).astype(BF16)
        hg = _dot(x, wg_s[...])
        hu = _dot(x, wu_s[...])
        h = hg * (1.0 / (1.0 + jnp.exp(-hg))) * hu
        _store_gatherable(o_ref, _dot(h.astype(BF16), wd_s[...]))

    @pl.when(i == 0)
    def _():
        for cp in weight_copies(te_ref[0], 0):
            cp.start(priority=1)
        for t in range(ahead):
            @pl.when(t < nact)
            def _(t=t):
                issue_gather(t, t)

    @pl.when(i + ahead < nact)
    def _():
        tile_step(True)

    @pl.when(jnp.logical_and(i < nact, i + ahead >= nact))
    def _():
        tile_step(False)

    @pl.when(i >= nact)
    def _():
        o_ref[...] = jnp.zeros_like(o_ref)


def _moe_experts(x_rows, w_gate, w_up, w_down, tile_expert, tile_ord, tile_next, src, nact, *,
                 n_tiles):
    tm = MOE_TILE
    out_map = lambda i, te, od, nx, s, n: (i, 0)
    any_spec = pl.BlockSpec(memory_space=pl.ANY)
    grid_spec = pltpu.PrefetchScalarGridSpec(
        num_scalar_prefetch=5,
        grid=(n_tiles,),
        in_specs=[any_spec, any_spec, any_spec, any_spec],
        out_specs=pl.BlockSpec((tm * ROW_PITCH, LANES), out_map),
        scratch_shapes=[pltpu.VMEM((GATHER_SLOTS, tm * ROW_PITCH, LANES), F32),
                        pltpu.SemaphoreType.DMA((GATHER_SLOTS,)),
                        pltpu.VMEM((2, D_MODEL, D_EXPERT), F32),
                        pltpu.VMEM((2, D_MODEL, D_EXPERT), F32),
                        pltpu.VMEM((2, D_EXPERT, D_MODEL), F32),
                        pltpu.SemaphoreType.DMA((2,)),
                        pltpu.VMEM((D_MODEL, D_EXPERT), BF16),
                        pltpu.VMEM((D_MODEL, D_EXPERT), BF16),
                        pltpu.VMEM((D_EXPERT, D_MODEL), BF16)],
    )
    return pl.pallas_call(
        _moe_body,
        out_shape=jax.ShapeDtypeStruct((n_tiles * tm * ROW_PITCH, LANES), F32),
        grid_spec=grid_spec,
        compiler_params=_cparams(("arbitrary",), 48),
        name="moe_experts",
    )(tile_expert, tile_ord, tile_next, src, nact, x_rows, w_gate, w_up, w_down)


def _combine_body(pos_ref, ys_hbm, wts_ref, x_ref, g_ref, b_ref, o1_ref, o2_ref, buf, sem, *,
                  tc, tiles1):
    i = pl.program_id(0)
    n = pl.num_programs(0)
    slot = i % GATHER_SLOTS
    ahead = GATHER_SLOTS - 1

    def issue_gather(tile, slot_):
        base = tile * tc * 2
        for r in range(tc):
            for k in range(2):
                _start_row_gather(ys_hbm, pos_ref[base + 2 * r + k], buf.at[slot_, k], r,
                                  sem.at[slot_])

    @pl.when(i == 0)
    def _():
        for t in range(ahead):
            @pl.when(t < n)
            def _(t=t):
                issue_gather(t, t)

    for k in range(2):
        _wait_row_gathers(buf.at[slot, k], buf.at[(i + 1) % GATHER_SLOTS, k], tc, sem.at[slot])

    @pl.when(i + ahead < n)
    def _():
        issue_gather(i + ahead, (i + ahead) % GATHER_SLOTS)

    w = wts_ref[...]
    moe = (w[:, 0:1] * _load_gathered(buf.at[slot, 0], tc)
           + w[:, 1:2] * _load_gathered(buf.at[slot, 1], tc))
    out = _layernorm(DEEPNORM_ALPHA * x_ref[...] + moe, g_ref[...], b_ref[...])

    @pl.when(i < tiles1)
    def _():
        o1_ref[...] = out

    @pl.when(i >= tiles1)
    def _():
        o2_ref[...] = out


def _moe_combine(ys, pos, wts, x, g, b, *, tc, n_first):
    m = x.shape[0]
    assert n_first % tc == 0 and (m - n_first) % tc == 0
    tiles1 = n_first // tc
    grid_spec = pltpu.PrefetchScalarGridSpec(
        num_scalar_prefetch=1,
        grid=(m // tc,),
        in_specs=[pl.BlockSpec(memory_space=pl.ANY),
                  pl.BlockSpec((tc, ROUTER_LANES), lambda i, p: (i, 0)),
                  pl.BlockSpec((tc, D_MODEL), lambda i, p: (i, 0)),
                  pl.BlockSpec((1, D_MODEL), lambda i, p: (0, 0)),
                  pl.BlockSpec((1, D_MODEL), lambda i, p: (0, 0))],
        out_specs=[pl.BlockSpec((tc, D_MODEL), lambda i, p: (jnp.minimum(i, tiles1 - 1), 0)),
                   pl.BlockSpec((tc, D_MODEL), lambda i, p: (jnp.maximum(i - tiles1, 0), 0))],
        scratch_shapes=[pltpu.VMEM((GATHER_SLOTS, 2, tc * ROW_PITCH, LANES), F32),
                        pltpu.SemaphoreType.DMA((GATHER_SLOTS,))],
    )
    return pl.pallas_call(
        functools.partial(_combine_body, tc=tc, tiles1=tiles1),
        out_shape=[jax.ShapeDtypeStruct((n_first, D_MODEL), F32),
                   jax.ShapeDtypeStruct((m - n_first, D_MODEL), F32)],
        grid_spec=grid_spec,
        compiler_params=_cparams(("arbitrary",), 16 * tc * D_MODEL * 4 / 2**20 + 8),
        name="moe_combine_ln3",
    )(pos, ys, wts, x, g, b)


def _moe(x, x_rows, w_r1, b_r1, w_r2, b_r2, w_gate, w_up, w_down, g, b, *, n_first, tm_router, tc):
    n = x.shape[0]
    ng, ne = N_EXPERT_GROUPS, N_EXPERTS
    half = ROUTER_LANES // 2
    assert ng + ne <= half
    w_r = jnp.concatenate([w_r1, w_r2.reshape(D_MODEL, ne),
                           jnp.zeros((D_MODEL, half - ng - ne), F32)], axis=1)
    w_hi = w_r.astype(BF16)
    w_lo = (w_r - w_hi.astype(F32)).astype(BF16)
    w_r = jnp.concatenate([w_hi, w_lo], axis=1)
    b_r = jnp.concatenate([b_r1, b_r2.reshape(ne), jnp.zeros((half - ng - ne,), F32)])
    b_r = jnp.concatenate([b_r, b_r]).reshape(1, ROUTER_LANES)
    sel, wts, cnt = _router(x, w_r, b_r, tm=tm_router)

    tm = MOE_TILE
    n_tiles = (2 * n) // tm + ne
    counts = cnt[0, ng:ng + ne]
    tiles_per = (counts + tm - 1) // tm
    tile_end = jnp.cumsum(tiles_per)
    row_off = (tile_end - tiles_per) * tm
    nact = tile_end[-1]
    ids, ranks = sel[:, 0:2], sel[:, 2:4]
    pos = row_off[ids] + ranks
    tile_ids = jnp.minimum(jnp.arange(n_tiles, dtype=I32), nact - 1)
    tile_expert = jnp.sum((tile_end[None, :] <= tile_ids[:, None]).astype(I32), axis=1)
    token = jnp.broadcast_to(jnp.arange(n, dtype=I32)[:, None], (n, 2))
    src = jnp.zeros((n_tiles * tm,), I32).at[pos.reshape(-1)].set(token.reshape(-1))
    used = tiles_per > 0
    eid = jnp.arange(ne, dtype=I32)
    ordinal = jnp.cumsum(used.astype(I32)) - 1
    later = jnp.where(jnp.logical_and(used[None, :], eid[None, :] > eid[:, None]), eid[None, :], ne)
    nxt = jnp.min(later, axis=1)
    nxt = jnp.where(nxt == ne, -1, nxt)

    ys = _moe_experts(x_rows, w_gate, w_up, w_down, tile_expert, ordinal[tile_expert],
                      nxt[tile_expert], src, nact.reshape(1).astype(I32), n_tiles=n_tiles)
    return _moe_combine(ys, pos.reshape(-1).astype(I32), wts, x, g, b, tc=tc, n_first=n_first)


def _row_tile(m, cap):
    best = SUBLANES
    for t in range(SUBLANES, cap + 1, SUBLANES):
        if m % t == 0:
            best = t
    return best


def kernel(x_prompt, x_sample, cache_win_k, cache_win_v, state_ssm_re, state_ssm_im, cache_mem_k, cache_mem_v, mem_prompt, w_in, ssm_lam_re, ssm_lam_im, ssm_log_dt, ssm_b_re, ssm_b_im, ssm_c_re, ssm_c_im, ssm_d, w_glu, g_attn, g_ssm, w_out, ln1_g, ln1_b, w_mq, w_mk, w_mv, w_mo, ln2_g, ln2_b, w_r1, b_r1, w_r2, b_r2, w_gate, w_up, w_down, ln3_g, ln3_b):
    nb, seq, d = x_prompt.shape
    ns, dseq, _ = x_sample.shape
    n_p, n_s = nb * seq, ns * dseq
    n = n_p + n_s
    l = 0
    row2 = lambda v: v[l].reshape(1, -1)

    x_p, x_s = x_prompt.reshape(n_p, d), x_sample.reshape(n_s, d)
    tm_p = _row_tile(n_p, 1024)
    tm_ln = _row_tile(n_p, 512)
    assert n_p % n_s == 0 and n_s % SUBLANES == 0

    w_in_b = w_in[l].astype(BF16)
    proj_p = _matmul(x_p, w_in_b, tm=tm_p, tn=1024, name="proj_in_prompt")
    proj_s = _matmul(x_s, w_in_b, tm=n_s, tn=1024, name="proj_in_sample")

    attn_p = _attn_prompt(proj_p, n_batch=nb, seq=seq)
    attn_s = _attn_sample(proj_s, cache_win_k[l], cache_win_v[l], row0=0, n_seq=ns, n_new=dseq)

    seg_len = seq // SSM_SEGMENTS
    prm = _ssm_params(ssm_lam_re[l], ssm_lam_im[l], ssm_log_dt[l], ssm_b_re[l], ssm_b_im[l],
                      ssm_c_re[l], ssm_c_im[l], ssm_d[l], seg_len)
    zeros = jnp.zeros((nb * SSM_SEGMENTS, N_SSM_GROUPS * SSM_STATE), F32)
    tl = _row_tile(seg_len, 32)
    end_re, end_im = _ssm_scan(proj_p, prm, zeros, zeros, seq_len=seg_len, tl=tl, nseg=1,
                               emit_y=False, exact_in=False, name="ssm_state_prompt")
    yg_p, fin_re, fin_im = _ssm_scan(proj_p, prm, end_re, end_im, seq_len=seg_len, tl=tl,
                                     nseg=SSM_SEGMENTS, emit_y=True, exact_in=False,
                                     name="ssm_scan_prompt")
    last = SSM_SEGMENTS - 1
    ssm_re_p = fin_re.reshape(nb, SSM_SEGMENTS, N_SSM_GROUPS, SSM_STATE)[:, last]
    ssm_im_p = fin_im.reshape(nb, SSM_SEGMENTS, N_SSM_GROUPS, SSM_STATE)[:, last]

    h0_re = state_ssm_re[l].reshape(ns, -1)
    h0_im = state_ssm_im[l].reshape(ns, -1)
    yg_s, ssm_re_s, ssm_im_s = _ssm_scan(proj_s, prm, h0_re, h0_im, seq_len=dseq, tl=dseq, nseg=1,
                                         emit_y=True, exact_in=True, name="ssm_scan_sample")
    w_glu_b = w_glu[l].astype(BF16)
    ssm_out_p = _glu(yg_p, w_glu_b, tm=tm_p, name="ssm_glu_prompt")
    ssm_out_s = _glu(yg_s, w_glu_b, tm=n_s, name="ssm_glu_sample")

    mix_args = (row2(g_attn), row2(g_ssm), w_out[l].astype(BF16))
    ln1 = (row2(ln1_g), row2(ln1_b))
    x1_p = _mix(attn_p, ssm_out_p, *mix_args, x_p, *ln1, tm=tm_ln, name="mix_out_ln1_prompt")
    x1_s = _mix(attn_s, ssm_out_s, *mix_args, x_s, *ln1, tm=n_s, name="mix_out_ln1_sample")

    mem_rows = mem_prompt.reshape(nb * N_MEM, d)
    mem_k = _matmul(mem_rows, w_mk[l].astype(BF16), tm=nb * N_MEM, tn=1024, name="mem_k")
    mem_v = _matmul(mem_rows, w_mv[l].astype(BF16), tm=nb * N_MEM, tn=1024, name="mem_v")
    w_mq_b = w_mq[l].astype(BF16)
    q_p = _matmul(x1_p, w_mq_b, tm=tm_p, tn=1024, name="mem_q_prompt")
    q_s = _matmul(x1_s, w_mq_b, tm=n_s, tn=1024, name="mem_q_sample")
    o_p = _memattn(q_p, mem_k.reshape(nb, N_MEM, d), mem_v.reshape(nb, N_MEM, d),
                   row0=0, n_seq=nb, seq=seq, tq=_row_tile(seq, 512), name="memattn_prompt")
    o_s = _memattn_heads(q_s, cache_mem_k[l], cache_mem_v[l], row0=0, n_seq=ns, seq=dseq,
                         name="memattn_sample")
    x2, x2_rows = _mm_ln(o_p, o_s, w_mo[l].astype(BF16), x1_p, x1_s, row2(ln2_g), row2(ln2_b),
                         name="mem_out_ln2")

    y_p, y_s = _moe(x2, x2_rows, w_r1[l], b_r1[l], w_r2[l], b_r2[l], w_gate[l], w_up[l],
                    w_down[l], row2(ln3_g), row2(ln3_b), n_first=n_p,
                    tm_router=_row_tile(n_s, 256), tc=_row_tile(n_s, 128))

    y_p = y_p.reshape(nb, seq, d)
    y_s = y_s.reshape(ns, dseq, d)
    k_p = proj_p[:, D_ATT:2 * D_ATT].reshape(nb, seq, ATT_HEADS, ATT_HD)
    v_p = proj_p[:, 2 * D_ATT:3 * D_ATT].reshape(nb, seq, ATT_HEADS, ATT_HD)
    wp = min(max(w for w, _ in DILATIONS), seq)
    k_s = proj_s[:, D_ATT:2 * D_ATT].reshape(ns, dseq, ATT_HEADS, ATT_HD)
    v_s = proj_s[:, 2 * D_ATT:3 * D_ATT].reshape(ns, dseq, ATT_HEADS, ATT_HD)
    state = lambda v, b_: v.reshape(1, b_, N_SSM_GROUPS, SSM_STATE)
    return (y_p, y_s, k_p[None, :, seq - wp:], v_p[None, :, seq - wp:], k_s[None], v_s[None],
            state(ssm_re_p, nb), state(ssm_im_p, nb), state(ssm_re_s, ns), state(ssm_im_s, ns),
            mem_k.reshape(1, nb, N_MEM, MEM_HEADS, MEM_HD),
            mem_v.reshape(1, nb, N_MEM, MEM_HEADS, MEM_HD))
```

```python
import functools
import math

import numpy as np
import jax
import jax.numpy as jnp
from jax import lax
from jax.experimental import pallas as pl
from jax.experimental.pallas import tpu as pltpu

F32 = jnp.float32
BF16 = jnp.bfloat16
I32 = jnp.int32

D_MODEL = 2048
PAST_LEN = 8192
D_ATT = D_MODEL // 2
ATT_HEADS = 8
ATT_HD = D_ATT // ATT_HEADS
DILATIONS = ((128, 1), (512, 4), (2048, 16))
D_SSM = D_MODEL - D_ATT
SSM_GROUP_CH = 16
N_SSM_GROUPS = D_SSM // SSM_GROUP_CH
SSM_STATE = 64
N_MEM = 256
MEM_HEADS = 4
MEM_HD = D_MODEL // MEM_HEADS
N_EXPERT_GROUPS = 4
EXPERTS_PER_GROUP = 8
N_EXPERTS = N_EXPERT_GROUPS * EXPERTS_PER_GROUP
D_EXPERT = D_MODEL // 4
DEPTH = 1
DEEPNORM_ALPHA = (2.0 * DEPTH) ** 0.25
LN_EPS = 1e-5
RMS_EPS = 1e-6

LANES = 128
SUBLANES = 8
ROW_CHUNKS = D_MODEL // LANES
ROW_PITCH = ROW_CHUNKS + 1
Q_BLOCK = 128
ATTN_GROUP = 8
SSM_LANE_TILE = 128
SSM_GROUPS_PER_TILE = SSM_LANE_TILE // SSM_GROUP_CH
SSM_STATES_PER_TILE = SSM_GROUPS_PER_TILE * SSM_STATE
SSM_SEGMENTS = 8
MOE_TILE = 256
GATHER_SLOTS = 3
ROUTER_LANES = 128
NEG_INF = float("-inf")


def _cparams(semantics, vmem_mib):
    return pltpu.CompilerParams(dimension_semantics=semantics,
                                vmem_limit_bytes=int(vmem_mib) << 20)


def _layernorm(y, g, b):
    mu = jnp.mean(y, axis=-1, keepdims=True)
    yc = y - mu
    var = jnp.mean(yc * yc, axis=-1, keepdims=True)
    return yc * lax.rsqrt(var + LN_EPS) * g + b


def _rmsnorm(v, g):
    return v * lax.rsqrt(jnp.mean(v * v, axis=-1, keepdims=True) + RMS_EPS) * g


def _dot(a, b):
    return jnp.dot(a, b, preferred_element_type=F32)


def _dot_nt(a, b):
    return lax.dot_general(a, b, (((1,), (1,)), ((), ())), preferred_element_type=F32)


def _mm_body(x_ref, w_ref, o_ref):
    o_ref[...] = _dot(x_ref[...].astype(BF16), w_ref[...]).astype(o_ref.dtype)


def _matmul(x, w, *, tm, tn, name):
    m, k = x.shape
    n = w.shape[1]
    vmem = 2 * (tm * k * x.dtype.itemsize + k * tn * 2 + tm * tn * 4) / 2**20 + 8
    return pl.pallas_call(
        _mm_body,
        out_shape=jax.ShapeDtypeStruct((m, n), F32),
        grid=(n // tn, m // tm),
        in_specs=[pl.BlockSpec((tm, k), lambda j, i: (i, 0)),
                  pl.BlockSpec((k, tn), lambda j, i: (0, j))],
        out_specs=pl.BlockSpec((tm, tn), lambda j, i: (i, j)),
        compiler_params=_cparams(("parallel", "parallel"), vmem),
        name=name,
    )(x, w)


def _attn_prompt_body(q_ref, k_ref, v_ref, o_ref, kt_s, va_s, on_s, lse_s, *, seq, dilations):
    scale = ATT_HD ** -0.5
    nblk = seq // Q_BLOCK
    qi = lax.broadcasted_iota(I32, (Q_BLOCK, Q_BLOCK), 0)
    kj = lax.broadcasted_iota(I32, (Q_BLOCK, Q_BLOCK), 1)
    cur_ok = kj <= qi
    prev_ok = kj >= qi
    va_s[:, :, ATT_HD:] = jnp.ones((nblk, Q_BLOCK, ATT_HD), BF16)

    for br, (_, d) in enumerate(dilations):
        span = d * Q_BLOCK
        nb = seq // span

        def stream_rows(t, d=d, span=span, nb=nb):
            r = t // nb
            ib = t % nb
            return r, ib, pl.ds(r + ib * span, Q_BLOCK, stride=d)

        def prep(g, carry, stream_rows=stream_rows):
            loaded = []
            for j in range(ATTN_GROUP):
                t = g * ATTN_GROUP + j
                _, _, rows = stream_rows(t)
                loaded.append((t, k_ref[rows, :], v_ref[rows, :]))
            for t, kk, vv in loaded:
                kt_s[t] = jnp.transpose(kk).astype(BF16)
                va_s[t, :, 0:ATT_HD] = vv.astype(BF16)
            return carry

        lax.fori_loop(0, nblk // ATTN_GROUP, prep, 0)

        def group(g, carry, br=br, nb=nb, stream_rows=stream_rows):
            scores = []
            for j in range(ATTN_GROUP):
                t = g * ATTN_GROUP + j
                r, ib, rows = stream_rows(t)
                tp = jnp.maximum(t - 1, r * nb)
                q = (q_ref[rows, :] * scale).astype(BF16)
                s = _dot(q, jnp.concatenate([kt_s[tp], kt_s[t]], axis=1))
                scores.append((t, tp, ib, rows, s))
            probs = []
            for t, tp, ib, rows, s in scores:
                ok = jnp.concatenate([jnp.logical_and(prev_ok, ib > 0), cur_ok], axis=1)
                s = jnp.where(ok, s, NEG_INF)
                m = jnp.max(s, axis=-1, keepdims=True)
                probs.append((t, tp, rows, m, jnp.exp(s - m).astype(BF16)))
            outs = [(rows, m, _dot(p, jnp.concatenate([va_s[tp], va_s[t]], axis=0)))
                    for t, tp, rows, m, p in probs]
            for rows, m, al in outs:
                l = al[:, ATT_HD:]
                on_s[br, rows, :] = al[:, :ATT_HD] / l
                lse_s[br, rows, :] = m + jnp.log(l)
            return carry

        lax.fori_loop(0, nblk // ATTN_GROUP, group, 0)

    chunk = 256
    nbr = len(dilations)

    def merge(c, carry):
        rows = pl.ds(pl.multiple_of(c * chunk, chunk), chunk)
        ls = [lse_s[b, rows, :] for b in range(nbr)]
        mx = functools.reduce(jnp.maximum, ls)
        es = [jnp.exp(li - mx) for li in ls]
        num = sum(es[b] * on_s[b, rows, :] for b in range(nbr))
        o_ref[rows, :] = num / sum(es)
        return carry

    lax.fori_loop(0, seq // chunk, merge, 0)


def _attn_prompt(proj, *, n_batch, seq, dilations=DILATIONS):
    for w, d in dilations:
        assert w // d == Q_BLOCK and seq % (d * Q_BLOCK) == 0
    nbr = len(dilations)
    nblk = seq // Q_BLOCK
    assert nblk % ATTN_GROUP == 0
    blk = lambda off: pl.BlockSpec((seq, ATT_HD), lambda b, h, off=off: (b, off + h))
    vmem = ((4 * 2 + 2 * nbr) * seq * ATT_HD * 4 + 3 * seq * ATT_HD * 2) / 2**20 + 8
    return pl.pallas_call(
        functools.partial(_attn_prompt_body, seq=seq, dilations=dilations),
        out_shape=jax.ShapeDtypeStruct((n_batch * seq, D_ATT), F32),
        grid=(n_batch, ATT_HEADS),
        in_specs=[blk(0), blk(ATT_HEADS), blk(2 * ATT_HEADS)],
        out_specs=pl.BlockSpec((seq, ATT_HD), lambda b, h: (b, h)),
        scratch_shapes=[pltpu.VMEM((nblk, ATT_HD, Q_BLOCK), BF16),
                        pltpu.VMEM((nblk, Q_BLOCK, 2 * ATT_HD), BF16),
                        pltpu.VMEM((nbr, seq, ATT_HD), F32),
                        pltpu.VMEM((nbr, seq, ATT_HD), F32)],
        compiler_params=_cparams(("parallel", "parallel"), vmem),
        name="attn_prompt",
    )(proj, proj, proj)


def _sample_key_multiplicity(n_new, n_cache, past_len, dilations):
    d_max = max(d for _, d in dilations)
    tail = max(w for w, d in dilations if d != d_max)
    assert past_len % d_max == 0 and n_cache % d_max == 0 and n_new <= d_max // 2
    assert tail % d_max == 0 and tail <= n_cache
    half = d_max // 2
    n_grid = (n_cache - tail) // d_max
    kv_start = past_len - n_cache
    grid_rows = (np.arange(n_grid)[:, None] * d_max + np.arange(half)[None, :]).reshape(-1)
    tail_rows = n_cache - tail + np.arange(tail)
    new_rows = n_cache + np.arange(n_new)
    qpos = past_len + np.arange(n_new)

    def mult(rows):
        kpos = kv_start + rows
        delta = qpos[:, None] - kpos[None, :]
        c = np.zeros(delta.shape, np.float32)
        for w, d in dilations:
            c += ((delta >= 0) & (delta <= w) & (delta % d == 0) & (kpos[None, :] >= kv_start))
        return c

    fetched = np.zeros(n_cache + n_new, bool)
    fetched[grid_rows] = True
    fetched[tail_rows] = True
    fetched[new_rows] = True
    assert not mult(np.nonzero(~fetched)[0]).any()
    return mult(grid_rows), mult(tail_rows), mult(new_rows), n_grid, tail, half, d_max


def _attn_sample_body(q_ref, kn_ref, vn_ref, kg_ref, kt_ref, vg_ref, vt_ref,
                      cg_ref, ct_ref, cn_ref, o_ref):
    scale = ATT_HD ** -0.5
    heads = lambda ref: jnp.concatenate(
        [ref[:, h * ATT_HD:(h + 1) * ATT_HD] for h in range(ATT_HEADS)], axis=0)
    q = (heads(q_ref) * scale).astype(BF16)
    kn = heads(kn_ref).astype(BF16)
    vn = heads(vn_ref).astype(BF16)
    flat = lambda ref: ref[...].reshape(-1, ATT_HD).astype(BF16)
    cg, ct, cn = cg_ref[...], ct_ref[...], cn_ref[...]
    sg = jnp.where(cg > 0, _dot_nt(q, flat(kg_ref)), NEG_INF)
    st = jnp.where(ct > 0, _dot_nt(q, flat(kt_ref)), NEG_INF)
    sn = jnp.where(cn > 0, _dot_nt(q, kn), NEG_INF)
    m = jnp.maximum(jnp.maximum(jnp.max(sg, axis=-1, keepdims=True),
                                jnp.max(st, axis=-1, keepdims=True)),
                    jnp.max(sn, axis=-1, keepdims=True))
    pg = cg * jnp.exp(sg - m)
    pt = ct * jnp.exp(st - m)
    pn = cn * jnp.exp(sn - m)
    l = (jnp.sum(pg, axis=-1, keepdims=True) + jnp.sum(pt, axis=-1, keepdims=True)
         + jnp.sum(pn, axis=-1, keepdims=True))
    acc = (_dot(pg.astype(BF16), flat(vg_ref)) + _dot(pt.astype(BF16), flat(vt_ref))
           + _dot(pn.astype(BF16), vn))
    out = acc / l
    n_new = q_ref.shape[0]
    for h in range(ATT_HEADS):
        o_ref[:, h * ATT_HD:(h + 1) * ATT_HD] = out[h * n_new:(h + 1) * n_new, :]


def _attn_sample(proj, win_k, win_v, *, row0, n_seq, n_new, past_len=PAST_LEN,
                 dilations=DILATIONS):
    n_cache = win_k.shape[1]
    cg, ct, cn, n_grid, tail, half, d_max = _sample_key_multiplicity(
        n_new, n_cache, past_len, dilations)
    assert row0 % n_new == 0 and n_new % SUBLANES == 0 and n_cache % tail == 0
    eye = np.eye(ATT_HEADS, dtype=np.float32)
    key_major = lambda c: np.einsum("tk,hg->htkg", c, eye).reshape(ATT_HEADS * n_new, -1)
    head_major = lambda c: np.einsum("tk,hg->htgk", c, eye).reshape(ATT_HEADS * n_new, -1)
    cg, ct, cn = key_major(cg), key_major(ct), head_major(cn)
    rb = row0 // n_new
    n_groups = n_cache // d_max
    kgv = win_k.reshape(n_seq, n_groups, d_max, ATT_HEADS, ATT_HD)
    vgv = win_v.reshape(n_seq, n_groups, d_max, ATT_HEADS, ATT_HD)
    ktv = win_k.reshape(n_seq, n_cache // tail, tail, ATT_HEADS, ATT_HD)
    vtv = win_v.reshape(n_seq, n_cache // tail, tail, ATT_HEADS, ATT_HD)
    new = lambda off: pl.BlockSpec((n_new, D_ATT), lambda b, off=off: (rb + b, off))
    grid_spec = pl.BlockSpec((None, n_grid, half, ATT_HEADS, ATT_HD), lambda b: (b, 0, 0, 0, 0))
    tail_spec = pl.BlockSpec((None, None, tail, ATT_HEADS, ATT_HD),
                             lambda b: (b, n_cache // tail - 1, 0, 0, 0))
    const = lambda a: pl.BlockSpec(a.shape, lambda b: (0, 0))
    vmem = (2 * 2 * (n_grid * half + tail) * D_ATT * 4 + 4 * cg.size * 4 * 3) / 2**20 + 12
    return pl.pallas_call(
        _attn_sample_body,
        out_shape=jax.ShapeDtypeStruct((n_seq * n_new, D_ATT), F32),
        grid=(n_seq,),
        in_specs=[new(0), new(1), new(2), grid_spec, tail_spec, grid_spec, tail_spec,
                  const(cg), const(ct), const(cn)],
        out_specs=pl.BlockSpec((n_new, D_ATT), lambda b: (b, 0)),
        compiler_params=_cparams(("parallel",), vmem),
        name="attn_sample",
    )(proj, proj, proj, kgv, ktv, vgv, vtv, jnp.asarray(cg), jnp.asarray(ct), jnp.asarray(cn))


def _gelu_tanh(x):
    return 0.5 * x * (1.0 + jnp.tanh(math.sqrt(2.0 / math.pi) * (x + 0.044715 * (x * x * x))))


def _ssm_body(u_ref, bb_ref, cst_ref, a_ref, ap_ref, d_ref, hre_ref, him_ref, *rest,
              tl, npar, seq_len, nseg, emit_y, exact_in):
    if emit_y:
        y_ref, fre_ref, fim_ref, x_s, h_s = rest
    else:
        fre_ref, fim_ref, x_s, h_s = rest
    ns = SSM_STATES_PER_TILE
    c = pl.program_id(1)
    ngrp = npar // SUBLANES

    def step_rows(i, g):
        return pl.ds(c * tl + i + g * SUBLANES * seq_len, SUBLANES, stride=seq_len)

    @pl.when(c == 0)
    def _init():
        if nseg == 1:
            h_s[0] = hre_ref[...]
            h_s[1] = him_ref[...]
        else:
            pr, pi = ap_ref[0:1, :], ap_ref[1:2, :]
            for b in range(npar // nseg):
                sr = jnp.zeros((1, ns), F32)
                si = jnp.zeros((1, ns), F32)
                for j in range(nseg):
                    row = b * nseg + j
                    h_s[0, row:row + 1, :] = sr
                    h_s[1, row:row + 1, :] = si
                    er, ei = hre_ref[row:row + 1, :], him_ref[row:row + 1, :]
                    sr, si = pr * sr - pi * si + er, pr * si + pi * sr + ei

    u = jnp.concatenate([u_ref[step_rows(i, g), :] for i in range(tl) for g in range(ngrp)], axis=0)
    if exact_in:
        x_s[...] = jnp.dot(u, bb_ref[...], precision=lax.Precision.HIGHEST,
                           preferred_element_type=F32)
    else:
        x_s[...] = _dot(u.astype(BF16), bb_ref[...])

    ar = jnp.broadcast_to(a_ref[0:1, :], (SUBLANES, ns))
    ai = jnp.broadcast_to(a_ref[1:2, :], (SUBLANES, ns))

    def step(i, carry):
        out = []
        for g in range(ngrp):
            hr, hi = carry[2 * g], carry[2 * g + 1]
            rows = pl.ds(pl.multiple_of(i * npar + g * SUBLANES, SUBLANES), SUBLANES)
            nr = ar * hr - ai * hi + x_s[rows, 0:ns]
            ni = ar * hi + ai * hr + x_s[rows, ns:2 * ns]
            if emit_y:
                x_s[rows, 0:ns] = nr
                x_s[rows, ns:2 * ns] = ni
            out += [nr, ni]
        return tuple(out)

    init = []
    for g in range(ngrp):
        gs = slice(g * SUBLANES, (g + 1) * SUBLANES)
        init += [h_s[0, gs, :], h_s[1, gs, :]]
    fin = lax.fori_loop(0, tl, step, tuple(init), unroll=4)
    for g in range(ngrp):
        gs = slice(g * SUBLANES, (g + 1) * SUBLANES)
        h_s[0, gs, :] = fin[2 * g]
        h_s[1, gs, :] = fin[2 * g + 1]

    if emit_y:
        y = _gelu_tanh(_dot(x_s[...].astype(BF16), cst_ref[...]) + d_ref[...] * u)
        for i in range(tl):
            for g in range(ngrp):
                r0 = i * npar + g * SUBLANES
                y_ref[step_rows(i, g), :] = y[r0:r0 + SUBLANES, :]

    @pl.when(c == pl.num_programs(1) - 1)
    def _fin():
        fre_ref[...] = h_s[0]
        fim_ref[...] = h_s[1]


def _ssm_scan(proj, prm, hin_re, hin_im, *, seq_len, tl, nseg, emit_y, exact_in, name):
    rows = proj.shape[0]
    npar = rows // seq_len
    assert npar % SUBLANES == 0 and seq_len % tl == 0
    ns = SSM_STATES_PER_TILE
    nk = D_SSM // SSM_LANE_TILE
    col0 = (proj.shape[1] - D_SSM) // SSM_LANE_TILE
    bb = prm["bb_f32"] if exact_in else prm["bb_bf16"]
    in_specs = [
        pl.BlockSpec((rows, SSM_LANE_TILE), lambda k, c: (0, col0 + k)),
        pl.BlockSpec((None, SSM_LANE_TILE, 2 * ns), lambda k, c: (k, 0, 0)),
        pl.BlockSpec((None, 2 * ns, SSM_LANE_TILE), lambda k, c: (k, 0, 0)),
        pl.BlockSpec((None, 2, ns), lambda k, c: (k, 0, 0)),
        pl.BlockSpec((None, 2, ns), lambda k, c: (k, 0, 0)),
        pl.BlockSpec((1, SSM_LANE_TILE), lambda k, c: (0, k)),
        pl.BlockSpec((npar, ns), lambda k, c: (0, k)),
        pl.BlockSpec((npar, ns), lambda k, c: (0, k)),
    ]
    state_shape = jax.ShapeDtypeStruct((npar, nk * ns), F32)
    state_spec = pl.BlockSpec((npar, ns), lambda k, c: (0, k))
    out_shape = [state_shape, state_shape]
    out_specs = [state_spec, state_spec]
    if emit_y:
        out_shape = [jax.ShapeDtypeStruct((rows, D_SSM), F32)] + out_shape
        out_specs = [pl.BlockSpec((rows, SSM_LANE_TILE), lambda k, c: (0, k))] + out_specs
    vmem = (4 * rows * SSM_LANE_TILE * 4 + tl * npar * 2 * ns * 4) / 2**20 + 16
    return pl.pallas_call(
        functools.partial(_ssm_body, tl=tl, npar=npar, seq_len=seq_len, nseg=nseg, emit_y=emit_y,
                          exact_in=exact_in),
        out_shape=out_shape,
        grid=(nk, seq_len // tl),
        in_specs=in_specs,
        out_specs=out_specs,
        scratch_shapes=[pltpu.VMEM((tl * npar, 2 * ns), F32), pltpu.VMEM((2, npar, ns), F32)],
        compiler_params=_cparams(("parallel", "arbitrary"), vmem),
        name=name,
    )(proj, bb, prm["cst"], prm["a"], prm["apow"], prm["d"], hin_re, hin_im)


def _ssm_params(lam_re, lam_im, log_dt, b_re, b_im, c_re, c_im, d_skip, seg_len):
    g, p, c = N_SSM_GROUPS, SSM_STATE, SSM_GROUP_CH
    nk, gt = g // SSM_GROUPS_PER_TILE, SSM_GROUPS_PER_TILE
    dt = jnp.exp(log_dt.astype(F32))[:, None]
    lr, li = lam_re.astype(F32), lam_im.astype(F32)
    mag = jnp.exp(lr * dt)
    a_re, a_im = mag * jnp.cos(li * dt), mag * jnp.sin(li * dt)
    magp = jnp.exp(lr * dt * seg_len)
    p_re, p_im = magp * jnp.cos(li * dt * seg_len), magp * jnp.sin(li * dt * seg_len)
    den = lr * lr + li * li
    nr, ni = a_re - 1.0, a_im
    f_re, f_im = (nr * lr + ni * li) / den, (ni * lr - nr * li) / den
    br, bi = b_re.astype(F32), b_im.astype(F32)
    bb_re = f_re[..., None] * br - f_im[..., None] * bi
    bb_im = f_re[..., None] * bi + f_im[..., None] * br
    eye = jnp.eye(gt, dtype=F32)

    def pack_b(m):
        return jnp.einsum("kgpc,gh->kgchp", m.reshape(nk, gt, p, c), eye).reshape(nk, gt * c, gt * p)

    def pack_c(m):
        return jnp.einsum("kgcp,gh->kgphc", m.reshape(nk, gt, c, p), eye).reshape(nk, gt * p, gt * c)

    bb = jnp.concatenate([pack_b(bb_re), pack_b(bb_im)], axis=2)
    cst = jnp.concatenate([pack_c(c_re.astype(F32)), -pack_c(c_im.astype(F32))], axis=1)
    tile = lambda v: v.reshape(nk, 1, gt * p)
    return {
        "bb_f32": bb, "bb_bf16": bb.astype(BF16), "cst": cst.astype(BF16),
        "a": jnp.concatenate([tile(a_re), tile(a_im)], axis=1),
        "apow": jnp.concatenate([tile(p_re), tile(p_im)], axis=1),
        "d": d_skip.astype(F32).reshape(1, g * c),
    }


def _glu_body(y_ref, w_ref, o_ref):
    yg = y_ref[...]
    z = _dot(yg.astype(BF16), w_ref[...])
    o_ref[...] = yg * (1.0 / (1.0 + jnp.exp(-z)))


def _glu(yg, w, *, tm, name):
    m, n = yg.shape
    return pl.pallas_call(
        _glu_body,
        out_shape=jax.ShapeDtypeStruct((m, n), F32),
        grid=(m // tm,),
        in_specs=[pl.BlockSpec((tm, n), lambda i: (i, 0)), pl.BlockSpec((n, n), lambda i: (0, 0))],
        out_specs=pl.BlockSpec((tm, n), lambda i: (i, 0)),
        compiler_params=_cparams(("parallel",), 4 * tm * n * 4 / 2**20 + 12),
        name=name,
    )(yg, w)


def _mix_body(attn_ref, ssm_ref, ga_ref, gs_ref, w_ref, x_ref, g_ref, b_ref, o_ref):
    a = _rmsnorm(attn_ref[...], ga_ref[...]).astype(BF16)
    s = _rmsnorm(ssm_ref[...], gs_ref[...]).astype(BF16)
    mix = _dot(a, w_ref[0:D_ATT, :]) + _dot(s, w_ref[D_ATT:D_ATT + D_SSM, :])
    o_ref[...] = _layernorm(DEEPNORM_ALPHA * x_ref[...] + mix, g_ref[...], b_ref[...])


def _mix(attn, ssm, ga, gs, w, x, g, b, *, tm, name):
    m = x.shape[0]
    row = lambda n: pl.BlockSpec((tm, n), lambda i: (i, 0))
    const = lambda a: pl.BlockSpec(a.shape, lambda i: (0, 0))
    return pl.pallas_call(
        _mix_body,
        out_shape=jax.ShapeDtypeStruct((m, D_MODEL), F32),
        grid=(m // tm,),
        in_specs=[row(D_ATT), row(D_SSM), const(ga), const(gs), const(w), row(D_MODEL),
                  const(g), const(b)],
        out_specs=row(D_MODEL),
        compiler_params=_cparams(("parallel",), 6 * tm * D_MODEL * 4 / 2**20 + 24),
        name=name,
    )(attn, ssm, ga, gs, w, x, g, b)


def _store_gatherable(o_ref, y):
    rows = y.shape[0]
    for c in range(ROW_CHUNKS):
        o_ref[pl.ds(c, rows, stride=ROW_PITCH), :] = y[:, c * LANES:(c + 1) * LANES]
    for c in range(ROW_CHUNKS, ROW_PITCH):
        o_ref[pl.ds(c, rows, stride=ROW_PITCH), :] = jnp.zeros((rows, LANES), F32)


def _load_gathered(buf, rows):
    return jnp.concatenate([buf[pl.ds(c, rows, stride=ROW_PITCH), :] for c in range(ROW_CHUNKS)],
                           axis=1)


def _start_row_gather(src_hbm, idx, buf, r, sem):
    pltpu.make_async_copy(src_hbm.at[pl.ds(idx * ROW_PITCH, ROW_CHUNKS), :],
                          buf.at[pl.ds(r * ROW_PITCH, ROW_CHUNKS), :], sem).start()


def _wait_row_gathers(buf, other, rows, sem):
    span = pl.ds(0, rows * ROW_CHUNKS)
    pltpu.make_async_copy(other.at[span, :], buf.at[span, :], sem).wait()


def _mm_ln_body(a1_ref, a2_ref, w_ref, x1_ref, x2_ref, g_ref, b_ref, o_ref, rows_ref, *, tiles1):
    first = pl.program_id(0) < tiles1
    a = jnp.where(first, a1_ref[...], a2_ref[...])
    x = jnp.where(first, x1_ref[...], x2_ref[...])
    y = _dot(a.astype(BF16), w_ref[...])
    out = _layernorm(DEEPNORM_ALPHA * x + y, g_ref[...], b_ref[...])
    o_ref[...] = out
    _store_gatherable(rows_ref, out)


def _mm_ln(a1, a2, w, x1, x2, g, b, *, name):
    tm = a2.shape[0]
    assert a1.shape[0] % tm == 0
    tiles1 = a1.shape[0] // tm
    m = a1.shape[0] + tm
    row1 = lambda n: pl.BlockSpec((tm, n), lambda i: (jnp.minimum(i, tiles1 - 1), 0))
    row2 = lambda n: pl.BlockSpec((tm, n), lambda i: (0, 0))
    const = lambda v: pl.BlockSpec(v.shape, lambda i: (0, 0))
    return pl.pallas_call(
        functools.partial(_mm_ln_body, tiles1=tiles1),
        out_shape=[jax.ShapeDtypeStruct((m, D_MODEL), F32),
                   jax.ShapeDtypeStruct((m * ROW_PITCH, LANES), F32)],
        grid=(tiles1 + 1,),
        in_specs=[row1(a1.shape[1]), row2(a2.shape[1]), const(w), row1(D_MODEL), row2(D_MODEL),
                  const(g), const(b)],
        out_specs=[pl.BlockSpec((tm, D_MODEL), lambda i: (i, 0)),
                   pl.BlockSpec((tm * ROW_PITCH, LANES), lambda i: (i, 0))],
        compiler_params=_cparams(("parallel",), 12 * tm * D_MODEL * 4 / 2**20 + 24),
        name=name,
    )(a1, a2, w, x1, x2, g, b)


def _memattn_body(q_ref, k_ref, v_ref, o_ref):
    scale = MEM_HD ** -0.5
    for h in range(MEM_HEADS):
        sl = slice(h * MEM_HD, (h + 1) * MEM_HD)
        s = _dot_nt(q_ref[:, sl].astype(BF16), k_ref[:, sl].astype(BF16)) * scale
        m = jnp.max(s, axis=-1, keepdims=True)
        p = jnp.exp(s - m)
        l = jnp.sum(p, axis=-1, keepdims=True)
        o_ref[:, sl] = _dot(p.astype(BF16), v_ref[:, sl].astype(BF16)) / l


def _memattn(q, mem_k, mem_v, *, row0, n_seq, seq, tq, name):
    assert seq % tq == 0 and row0 % tq == 0
    nq = seq // tq
    rb = row0 // tq
    mem_spec = pl.BlockSpec((None, N_MEM, D_MODEL), lambda b, i: (b, 0, 0))
    return pl.pallas_call(
        _memattn_body,
        out_shape=jax.ShapeDtypeStruct((n_seq * seq, D_MODEL), F32),
        grid=(n_seq, nq),
        in_specs=[pl.BlockSpec((tq, D_MODEL), lambda b, i: (rb + b * nq + i, 0)),
                  mem_spec, mem_spec],
        out_specs=pl.BlockSpec((tq, D_MODEL), lambda b, i: (b * nq + i, 0)),
        compiler_params=_cparams(("parallel", "parallel"),
                                 4 * (tq + N_MEM) * D_MODEL * 4 / 2**20 + 8),
        name=name,
    )(q, mem_k, mem_v)


def _memattn_heads_body(q_ref, k_ref, v_ref, c_ref, o_ref):
    scale = MEM_HD ** -0.5
    tq = q_ref.shape[0]
    q = jnp.concatenate([q_ref[:, h * MEM_HD:(h + 1) * MEM_HD] for h in range(MEM_HEADS)], axis=0)
    k = k_ref[...].reshape(N_MEM * MEM_HEADS, MEM_HD).astype(BF16)
    v = v_ref[...].reshape(N_MEM * MEM_HEADS, MEM_HD).astype(BF16)
    s = jnp.where(c_ref[...] > 0, _dot_nt(q.astype(BF16), k) * scale, NEG_INF)
    m = jnp.max(s, axis=-1, keepdims=True)
    p = jnp.exp(s - m)
    l = jnp.sum(p, axis=-1, keepdims=True)
    o = _dot(p.astype(BF16), v) / l
    for h in range(MEM_HEADS):
        o_ref[:, h * MEM_HD:(h + 1) * MEM_HD] = o[h * tq:(h + 1) * tq, :]


def _memattn_heads(q, mem_k, mem_v, *, row0, n_seq, seq, name):
    assert row0 % seq == 0 and seq % SUBLANES == 0
    rb = row0 // seq
    same_head = np.kron(np.eye(MEM_HEADS, dtype=np.float32), np.ones((seq, 1), np.float32))
    same_head = np.tile(same_head, (1, N_MEM))
    mem_spec = pl.BlockSpec((None, N_MEM, MEM_HEADS, MEM_HD), lambda b: (b, 0, 0, 0))
    return pl.pallas_call(
        _memattn_heads_body,
        out_shape=jax.ShapeDtypeStruct((n_seq * seq, D_MODEL), F32),
        grid=(n_seq,),
        in_specs=[pl.BlockSpec((seq, D_MODEL), lambda b: (rb + b, 0)), mem_spec, mem_spec,
                  pl.BlockSpec(same_head.shape, lambda b: (0, 0))],
        out_specs=pl.BlockSpec((seq, D_MODEL), lambda b: (b, 0)),
        compiler_params=_cparams(("parallel",), 8 * N_MEM * D_MODEL * 4 / 2**20 + 8),
        name=name,
    )(q, mem_k, mem_v, jnp.asarray(same_head))


def _router_body(x_ref, w_ref, b_ref, sel_ref, wts_ref, cnt_ref, run_s, *, tm):
    i = pl.program_id(0)

    @pl.when(i == 0)
    def _():
        run_s[...] = jnp.zeros_like(run_s)

    ng, epg = N_EXPERT_GROUPS, EXPERTS_PER_GROUP
    x = x_ref[...]
    x_hi = x.astype(BF16)
    x_lo = (x - x_hi.astype(F32)).astype(BF16)
    parts = _dot(x_hi, w_ref[...]) + _dot(x_lo, w_ref[...])
    logits = parts + pltpu.roll(parts, shift=ROUTER_LANES // 2, axis=1) + b_ref[...]
    lane = lax.broadcasted_iota(I32, (tm, ROUTER_LANES), 1)
    big = ROUTER_LANES

    def first_argmax(vals):
        mx = jnp.max(vals, axis=-1, keepdims=True)
        idx = jnp.min(jnp.where(vals == mx, lane, big), axis=-1, keepdims=True)
        return mx, idx

    gl = jnp.where(lane < ng, logits, NEG_INF)
    gmax, gsel = first_argmax(gl)
    g_w = 1.0 / jnp.sum(jnp.exp(gl - gmax), axis=-1, keepdims=True)
    lo = ng + gsel * epg
    el = jnp.where(jnp.logical_and(lane >= lo, lane < lo + epg), logits, NEG_INF)
    v1, i1 = first_argmax(el)
    v2, i2 = first_argmax(jnp.where(lane == i1, NEG_INF, el))
    e21 = jnp.exp(v2 - v1)
    w1 = g_w / (1.0 + e21)
    w2 = g_w * e21 / (1.0 + e21)

    onehot = jnp.logical_or(lane == i1, lane == i2)
    r = lax.broadcasted_iota(I32, (tm, tm), 0)
    cc = lax.broadcasted_iota(I32, (tm, tm), 1)
    tri = (cc < r).astype(BF16)
    before = _dot(tri, onehot.astype(BF16)) + run_s[...]
    rank1 = jnp.sum(jnp.where(lane == i1, before, 0.0), axis=-1, keepdims=True).astype(I32)
    rank2 = jnp.sum(jnp.where(lane == i2, before, 0.0), axis=-1, keepdims=True).astype(I32)
    run_s[...] = run_s[...] + jnp.sum(onehot.astype(F32), axis=0, keepdims=True)

    sel = jnp.where(lane == 0, i1 - ng, jnp.where(lane == 1, i2 - ng,
                    jnp.where(lane == 2, rank1, jnp.where(lane == 3, rank2, 0))))
    sel_ref[...] = sel
    wts_ref[...] = jnp.where(lane == 0, w1, jnp.where(lane == 1, w2, 0.0))
    cnt_ref[...] = run_s[...].astype(I32)


def _router(x, w, b, *, tm):
    m = x.shape[0]
    row = pl.BlockSpec((tm, ROUTER_LANES), lambda i: (i, 0))
    return pl.pallas_call(
        functools.partial(_router_body, tm=tm),
        out_shape=[jax.ShapeDtypeStruct((m, ROUTER_LANES), I32),
                   jax.ShapeDtypeStruct((m, ROUTER_LANES), F32),
                   jax.ShapeDtypeStruct((1, ROUTER_LANES), I32)],
        grid=(m // tm,),
        in_specs=[pl.BlockSpec((tm, D_MODEL), lambda i: (i, 0)),
                  pl.BlockSpec((D_MODEL, ROUTER_LANES), lambda i: (0, 0)),
                  pl.BlockSpec((1, ROUTER_LANES), lambda i: (0, 0))],
        out_specs=[row, row, pl.BlockSpec((1, ROUTER_LANES), lambda i: (0, 0))],
        scratch_shapes=[pltpu.VMEM((1, ROUTER_LANES), F32)],
        compiler_params=_cparams(("arbitrary",), 16),
        name="moe_router",
    )(x, w, b)


def _dispatch_body(pos_ref, pad0_ref, npad_ref, nact_ref, x_ref, xs_hbm, zero_s, sem, *,
                   td, n_tiles):
    i = pl.program_id(0)
    tile_rows = MOE_TILE * ROW_PITCH
    zero_row = zero_s.at[pl.ds(0, ROW_PITCH), :]

    def row_copy(src, dst_row, s):
        return pltpu.make_async_copy(src, xs_hbm.at[pl.ds(dst_row * ROW_PITCH, ROW_PITCH), :], s)

    @pl.when(i == 0)
    def _():
        zero_s[...] = jnp.zeros_like(zero_s)
        for e in range(N_EXPERTS):
            def start(r, carry, e=e):
                row_copy(zero_row, pad0_ref[e] + r, sem.at[1]).start()
                return carry

            def wait(r, carry):
                row_copy(zero_row, 0, sem.at[1]).wait()
                return carry

            lax.fori_loop(0, npad_ref[e], start, 0)
            lax.fori_loop(0, npad_ref[e], wait, 0)

        def zero_tile(t, carry):
            parts = [pltpu.make_async_copy(
                zero_s, xs_hbm.at[pl.ds(t * tile_rows + j * MOE_TILE, MOE_TILE), :], sem.at[1])
                for j in range(ROW_PITCH)]
            for cp in parts:
                cp.start()
            for cp in parts:
                cp.wait()
            return carry

        lax.fori_loop(nact_ref[0], n_tiles, zero_tile, 0)

    base = i * td * 2
    for r in range(td):
        for k in range(2):
            row_copy(x_ref.at[pl.ds(r * ROW_PITCH, ROW_PITCH), :], pos_ref[base + 2 * r + k],
                     sem.at[0]).start()
    for k in range(2):
        pltpu.make_async_copy(x_ref, xs_hbm.at[pl.ds(0, td * ROW_PITCH), :], sem.at[0]).wait()


def _moe_dispatch(x_rows, pos, pad_start, pad_count, nact, *, td, n_tiles):
    n = x_rows.shape[0] // ROW_PITCH
    grid_spec = pltpu.PrefetchScalarGridSpec(
        num_scalar_prefetch=4,
        grid=(n // td,),
        in_specs=[pl.BlockSpec((td * ROW_PITCH, LANES), lambda i, p, a, c, na: (i, 0))],
        out_specs=pl.BlockSpec(memory_space=pl.ANY),
        scratch_shapes=[pltpu.VMEM((MOE_TILE, LANES), F32), pltpu.SemaphoreType.DMA((2,))],
    )
    return pl.pallas_call(
        functools.partial(_dispatch_body, td=td, n_tiles=n_tiles),
        out_shape=jax.ShapeDtypeStruct((n_tiles * MOE_TILE * ROW_PITCH, LANES), F32),
        grid_spec=grid_spec,
        compiler_params=_cparams(("arbitrary",), 16),
        name="moe_dispatch",
    )(pos, pad_start, pad_count, nact, x_rows)


def _moe_body(te_ref, ord_ref, nxt_ref, nact_ref, x_ref, wg_hbm, wu_hbm, wd_hbm, o_ref,
              wg_f, wu_f, wd_f, wsem, wg_s, wu_s, wd_s):
    i = pl.program_id(0)
    nact = nact_ref[0]
    tm = MOE_TILE

    def weight_copies(expert, ws):
        return [pltpu.make_async_copy(hbm.at[expert], stage.at[ws], wsem.at[ws])
                for hbm, stage in ((wg_hbm, wg_f), (wu_hbm, wu_f), (wd_hbm, wd_f))]

    def tile_step():
        prev = te_ref[jnp.maximum(i - 1, 0)]

        @pl.when(jnp.logical_or(i == 0, te_ref[i] != prev))
        def _():
            ws = ord_ref[i] % 2
            for cp in weight_copies(te_ref[i], ws):
                cp.wait()
            wg_s[...] = wg_f[ws].astype(BF16)
            wu_s[...] = wu_f[ws].astype(BF16)
            wd_s[...] = wd_f[ws].astype(BF16)

            @pl.when(nxt_ref[i] >= 0)
            def _():
                for cp in weight_copies(nxt_ref[i], 1 - ws):
                    cp.start(priority=1)

        x = _load_gathered(x_ref, tm).astype(BF16)
        hg = _dot(x, wg_s[...])
        hu = _dot(x, wu_s[...])
        h = hg * (1.0 / (1.0 + jnp.exp(-hg))) * hu
        _store_gatherable(o_ref, _dot(h.astype(BF16), wd_s[...]))

    @pl.when(i == 0)
    def _():
        for cp in weight_copies(te_ref[0], 0):
            cp.start(priority=1)

    @pl.when(i < nact)
    def _():
        tile_step()

    @pl.when(i >= nact)
    def _():
        o_ref[...] = jnp.zeros_like(o_ref)


def _moe_experts(x_sorted, w_gate, w_up, w_down, tile_expert, tile_ord, tile_next, nact, *,
                 n_tiles):
    tm = MOE_TILE
    in_map = lambda i, te, od, nx, n: (jnp.minimum(i, n[0] - 1), 0)
    any_spec = pl.BlockSpec(memory_space=pl.ANY)
    grid_spec = pltpu.PrefetchScalarGridSpec(
        num_scalar_prefetch=4,
        grid=(n_tiles,),
        in_specs=[pl.BlockSpec((tm * ROW_PITCH, LANES), in_map), any_spec, any_spec, any_spec],
        out_specs=pl.BlockSpec((tm * ROW_PITCH, LANES), lambda i, te, od, nx, n: (i, 0)),
        scratch_shapes=[pltpu.VMEM((2, D_MODEL, D_EXPERT), F32),
                        pltpu.VMEM((2, D_MODEL, D_EXPERT), F32),
                        pltpu.VMEM((2, D_EXPERT, D_MODEL), F32),
                        pltpu.SemaphoreType.DMA((2,)),
                        pltpu.VMEM((D_MODEL, D_EXPERT), BF16),
                        pltpu.VMEM((D_MODEL, D_EXPERT), BF16),
                        pltpu.VMEM((D_EXPERT, D_MODEL), BF16)],
    )
    return pl.pallas_call(
        _moe_body,
        out_shape=jax.ShapeDtypeStruct((n_tiles * tm * ROW_PITCH, LANES), F32),
        grid_spec=grid_spec,
        compiler_params=_cparams(("arbitrary",), 48),
        name="moe_experts",
    )(tile_expert, tile_ord, tile_next, nact, x_sorted, w_gate, w_up, w_down)


def _combine_body(pos_ref, ys_hbm, wts_ref, x_ref, g_ref, b_ref, o1_ref, o2_ref, buf, sem, *,
                  tc, tiles1):
    i = pl.program_id(0)
    n = pl.num_programs(0)
    slot = i % GATHER_SLOTS
    ahead = GATHER_SLOTS - 1

    def issue_gather(tile, slot_):
        base = tile * tc * 2
        for r in range(tc):
            for k in range(2):
                _start_row_gather(ys_hbm, pos_ref[base + 2 * r + k], buf.at[slot_, k], r,
                                  sem.at[slot_])

    @pl.when(i == 0)
    def _():
        for t in range(ahead):
            @pl.when(t < n)
            def _(t=t):
                issue_gather(t, t)

    for k in range(2):
        _wait_row_gathers(buf.at[slot, k], buf.at[(i + 1) % GATHER_SLOTS, k], tc, sem.at[slot])

    @pl.when(i + ahead < n)
    def _():
        issue_gather(i + ahead, (i + ahead) % GATHER_SLOTS)

    w = wts_ref[...]
    moe = (w[:, 0:1] * _load_gathered(buf.at[slot, 0], tc)
           + w[:, 1:2] * _load_gathered(buf.at[slot, 1], tc))
    out = _layernorm(DEEPNORM_ALPHA * x_ref[...] + moe, g_ref[...], b_ref[...])

    @pl.when(i < tiles1)
    def _():
        o1_ref[...] = out

    @pl.when(i >= tiles1)
    def _():
        o2_ref[...] = out


def _moe_combine(ys, pos, wts, x, g, b, *, tc, n_first):
    m = x.shape[0]
    assert n_first % tc == 0 and (m - n_first) % tc == 0
    tiles1 = n_first // tc
    grid_spec = pltpu.PrefetchScalarGridSpec(
        num_scalar_prefetch=1,
        grid=(m // tc,),
        in_specs=[pl.BlockSpec(memory_space=pl.ANY),
                  pl.BlockSpec((tc, ROUTER_LANES), lambda i, p: (i, 0)),
                  pl.BlockSpec((tc, D_MODEL), lambda i, p: (i, 0)),
                  pl.BlockSpec((1, D_MODEL), lambda i, p: (0, 0)),
                  pl.BlockSpec((1, D_MODEL), lambda i, p: (0, 0))],
        out_specs=[pl.BlockSpec((tc, D_MODEL), lambda i, p: (jnp.minimum(i, tiles1 - 1), 0)),
                   pl.BlockSpec((tc, D_MODEL), lambda i, p: (jnp.maximum(i - tiles1, 0), 0))],
        scratch_shapes=[pltpu.VMEM((GATHER_SLOTS, 2, tc * ROW_PITCH, LANES), F32),
                        pltpu.SemaphoreType.DMA((GATHER_SLOTS,))],
    )
    return pl.pallas_call(
        functools.partial(_combine_body, tc=tc, tiles1=tiles1),
        out_shape=[jax.ShapeDtypeStruct((n_first, D_MODEL), F32),
                   jax.ShapeDtypeStruct((m - n_first, D_MODEL), F32)],
        grid_spec=grid_spec,
        compiler_params=_cparams(("arbitrary",), 16 * tc * D_MODEL * 4 / 2**20 + 8),
        name="moe_combine_ln3",
    )(pos, ys, wts, x, g, b)


def _moe(x, x_rows, w_r1, b_r1, w_r2, b_r2, w_gate, w_up, w_down, g, b, *, n_first, tm_router, tc):
    n = x.shape[0]
    ng, ne = N_EXPERT_GROUPS, N_EXPERTS
    half = ROUTER_LANES // 2
    assert ng + ne <= half
    w_r = jnp.concatenate([w_r1, w_r2.reshape(D_MODEL, ne),
                           jnp.zeros((D_MODEL, half - ng - ne), F32)], axis=1)
    w_hi = w_r.astype(BF16)
    w_lo = (w_r - w_hi.astype(F32)).astype(BF16)
    w_r = jnp.concatenate([w_hi, w_lo], axis=1)
    b_r = jnp.concatenate([b_r1, b_r2.reshape(ne), jnp.zeros((half - ng - ne,), F32)])
    b_r = jnp.concatenate([b_r, b_r]).reshape(1, ROUTER_LANES)
    sel, wts, cnt = _router(x, w_r, b_r, tm=tm_router)

    tm = MOE_TILE
    n_tiles = (2 * n) // tm + ne
    counts = cnt[0, ng:ng + ne]
    tiles_per = (counts + tm - 1) // tm
    tile_end = jnp.cumsum(tiles_per)
    row_off = (tile_end - tiles_per) * tm
    nact = tile_end[-1]
    ids, ranks = sel[:, 0:2], sel[:, 2:4]
    pos = row_off[ids] + ranks
    tile_ids = jnp.minimum(jnp.arange(n_tiles, dtype=I32), nact - 1)
    tile_expert = jnp.sum((tile_end[None, :] <= tile_ids[:, None]).astype(I32), axis=1)
    pos = pos.reshape(-1).astype(I32)
    used = tiles_per > 0
    eid = jnp.arange(ne, dtype=I32)
    ordinal = jnp.cumsum(used.astype(I32)) - 1
    later = jnp.where(jnp.logical_and(used[None, :], eid[None, :] > eid[:, None]), eid[None, :], ne)
    nxt = jnp.min(later, axis=1)
    nxt = jnp.where(nxt == ne, -1, nxt)

    nact = nact.reshape(1).astype(I32)
    x_sorted = _moe_dispatch(x_rows, pos, (row_off + counts).astype(I32),
                             (tiles_per * tm - counts).astype(I32), nact, td=tc, n_tiles=n_tiles)
    ys = _moe_experts(x_sorted, w_gate, w_up, w_down, tile_expert, ordinal[tile_expert],
                      nxt[tile_expert], nact, n_tiles=n_tiles)
    return _moe_combine(ys, pos, wts, x, g, b, tc=tc, n_first=n_first)


def _row_tile(m, cap):
    best = SUBLANES
    for t in range(SUBLANES, cap + 1, SUBLANES):
        if m % t == 0:
            best = t
    return best


def kernel(x_prompt, x_sample, cache_win_k, cache_win_v, state_ssm_re, state_ssm_im, cache_mem_k, cache_mem_v, mem_prompt, w_in, ssm_lam_re, ssm_lam_im, ssm_log_dt, ssm_b_re, ssm_b_im, ssm_c_re, ssm_c_im, ssm_d, w_glu, g_attn, g_ssm, w_out, ln1_g, ln1_b, w_mq, w_mk, w_mv, w_mo, ln2_g, ln2_b, w_r1, b_r1, w_r2, b_r2, w_gate, w_up, w_down, ln3_g, ln3_b):
    nb, seq, d = x_prompt.shape
    ns, dseq, _ = x_sample.shape
    n_p, n_s = nb * seq, ns * dseq
    n = n_p + n_s
    l = 0
    row2 = lambda v: v[l].reshape(1, -1)

    x_p, x_s = x_prompt.reshape(n_p, d), x_sample.reshape(n_s, d)
    tm_p = _row_tile(n_p, 1024)
    tm_ln = _row_tile(n_p, 512)
    assert n_p % n_s == 0 and n_s % SUBLANES == 0

    w_in_b = w_in[l].astype(BF16)
    proj_p = _matmul(x_p, w_in_b, tm=tm_p, tn=1024, name="proj_in_prompt")
    proj_s = _matmul(x_s, w_in_b, tm=n_s, tn=1024, name="proj_in_sample")

    attn_p = _attn_prompt(proj_p, n_batch=nb, seq=seq)
    attn_s = _attn_sample(proj_s, cache_win_k[l], cache_win_v[l], row0=0, n_seq=ns, n_new=dseq)

    seg_len = seq // SSM_SEGMENTS
    prm = _ssm_params(ssm_lam_re[l], ssm_lam_im[l], ssm_log_dt[l], ssm_b_re[l], ssm_b_im[l],
                      ssm_c_re[l], ssm_c_im[l], ssm_d[l], seg_len)
    zeros = jnp.zeros((nb * SSM_SEGMENTS, N_SSM_GROUPS * SSM_STATE), F32)
    tl = _row_tile(seg_len, 32)
    end_re, end_im = _ssm_scan(proj_p, prm, zeros, zeros, seq_len=seg_len, tl=tl, nseg=1,
                               emit_y=False, exact_in=False, name="ssm_state_prompt")
    yg_p, fin_re, fin_im = _ssm_scan(proj_p, prm, end_re, end_im, seq_len=seg_len, tl=tl,
                                     nseg=SSM_SEGMENTS, emit_y=True, exact_in=False,
                                     name="ssm_scan_prompt")
    last = SSM_SEGMENTS - 1
    ssm_re_p = fin_re.reshape(nb, SSM_SEGMENTS, N_SSM_GROUPS, SSM_STATE)[:, last]
    ssm_im_p = fin_im.reshape(nb, SSM_SEGMENTS, N_SSM_GROUPS, SSM_STATE)[:, last]

    h0_re = state_ssm_re[l].reshape(ns, -1)
    h0_im = state_ssm_im[l].reshape(ns, -1)
    yg_s, ssm_re_s, ssm_im_s = _ssm_scan(proj_s, prm, h0_re, h0_im, seq_len=dseq, tl=dseq, nseg=1,
                                         emit_y=True, exact_in=True, name="ssm_scan_sample")
    w_glu_b = w_glu[l].astype(BF16)
    ssm_out_p = _glu(yg_p, w_glu_b, tm=tm_p, name="ssm_glu_prompt")
    ssm_out_s = _glu(yg_s, w_glu_b, tm=n_s, name="ssm_glu_sample")

    mix_args = (row2(g_attn), row2(g_ssm), w_out[l].astype(BF16))
    ln1 = (row2(ln1_g), row2(ln1_b))
    x1_p = _mix(attn_p, ssm_out_p, *mix_args, x_p, *ln1, tm=tm_ln, name="mix_out_ln1_prompt")
    x1_s = _mix(attn_s, ssm_out_s, *mix_args, x_s, *ln1, tm=n_s, name="mix_out_ln1_sample")

    mem_rows = mem_prompt.reshape(nb * N_MEM, d)
    mem_k = _matmul(mem_rows, w_mk[l].astype(BF16), tm=nb * N_MEM, tn=1024, name="mem_k")
    mem_v = _matmul(mem_rows, w_mv[l].astype(BF16), tm=nb * N_MEM, tn=1024, name="mem_v")
    w_mq_b = w_mq[l].astype(BF16)
    q_p = _matmul(x1_p, w_mq_b, tm=tm_p, tn=1024, name="mem_q_prompt")
    q_s = _matmul(x1_s, w_mq_b, tm=n_s, tn=1024, name="mem_q_sample")
    o_p = _memattn(q_p, mem_k.reshape(nb, N_MEM, d), mem_v.reshape(nb, N_MEM, d),
                   row0=0, n_seq=nb, seq=seq, tq=_row_tile(seq, 512), name="memattn_prompt")
    o_s = _memattn_heads(q_s, cache_mem_k[l], cache_mem_v[l], row0=0, n_seq=ns, seq=dseq,
                         name="memattn_sample")
    x2, x2_rows = _mm_ln(o_p, o_s, w_mo[l].astype(BF16), x1_p, x1_s, row2(ln2_g), row2(ln2_b),
                         name="mem_out_ln2")

    y_p, y_s = _moe(x2, x2_rows, w_r1[l], b_r1[l], w_r2[l], b_r2[l], w_gate[l], w_up[l],
                    w_down[l], row2(ln3_g), row2(ln3_b), n_first=n_p,
                    tm_router=_row_tile(n_s, 256), tc=_row_tile(n_s, 128))

    y_p = y_p.reshape(nb, seq, d)
    y_s = y_s.reshape(ns, dseq, d)
    k_p = proj_p[:, D_ATT:2 * D_ATT].reshape(nb, seq, ATT_HEADS, ATT_HD)
    v_p = proj_p[:, 2 * D_ATT:3 * D_ATT].reshape(nb, seq, ATT_HEADS, ATT_HD)
    wp = min(max(w for w, _ in DILATIONS), seq)
    k_s = proj_s[:, D_ATT:2 * D_ATT].reshape(ns, dseq, ATT_HEADS, ATT_HD)
    v_s = proj_s[:, 2 * D_ATT:3 * D_ATT].reshape(ns, dseq, ATT_HEADS, ATT_HD)
    state = lambda v, b_: v.reshape(1, b_, N_SSM_GROUPS, SSM_STATE)
    return (y_p, y_s, k_p[None, :, seq - wp:], v_p[None, :, seq - wp:], k_s[None], v_s[None],
            state(ssm_re_p, nb), state(ssm_im_p, nb), state(ssm_re_s, ns), state(ssm_im_s, ns),
            mem_k.reshape(1, nb, N_MEM, MEM_HEADS, MEM_HD),
            mem_v.reshape(1, nb, N_MEM, MEM_HEADS, MEM_HD))
```

```python
import functools
import math

import numpy as np
import jax
import jax.numpy as jnp
from jax import lax
from jax.experimental import pallas as pl
from jax.experimental.pallas import tpu as pltpu

F32 = jnp.float32
BF16 = jnp.bfloat16
I32 = jnp.int32

D_MODEL = 2048
PAST_LEN = 8192
D_ATT = D_MODEL // 2
ATT_HEADS = 8
ATT_HD = D_ATT // ATT_HEADS
DILATIONS = ((128, 1), (512, 4), (2048, 16))
D_SSM = D_MODEL - D_ATT
SSM_GROUP_CH = 16
N_SSM_GROUPS = D_SSM // SSM_GROUP_CH
SSM_STATE = 64
N_MEM = 256
MEM_HEADS = 4
MEM_HD = D_MODEL // MEM_HEADS
N_EXPERT_GROUPS = 4
EXPERTS_PER_GROUP = 8
N_EXPERTS = N_EXPERT_GROUPS * EXPERTS_PER_GROUP
D_EXPERT = D_MODEL // 4
DEPTH = 1
DEEPNORM_ALPHA = (2.0 * DEPTH) ** 0.25
LN_EPS = 1e-5
RMS_EPS = 1e-6

LANES = 128
SUBLANES = 8
ROW_CHUNKS = D_MODEL // LANES
ROW_PITCH = ROW_CHUNKS + 1
Q_BLOCK = 128
ATTN_GROUP = 8
SSM_LANE_TILE = 128
SSM_GROUPS_PER_TILE = SSM_LANE_TILE // SSM_GROUP_CH
SSM_STATES_PER_TILE = SSM_GROUPS_PER_TILE * SSM_STATE
SSM_SEGMENTS = 8
MOE_TILE = 256
GATHER_SLOTS = 3
DISPATCH_SLOTS = 3
ROUTER_LANES = 128
NEG_INF = float("-inf")


def _cparams(semantics, vmem_mib):
    return pltpu.CompilerParams(dimension_semantics=semantics,
                                vmem_limit_bytes=int(vmem_mib) << 20)


def _layernorm(y, g, b):
    mu = jnp.mean(y, axis=-1, keepdims=True)
    yc = y - mu
    var = jnp.mean(yc * yc, axis=-1, keepdims=True)
    return yc * lax.rsqrt(var + LN_EPS) * g + b


def _rmsnorm(v, g):
    return v * lax.rsqrt(jnp.mean(v * v, axis=-1, keepdims=True) + RMS_EPS) * g


def _dot(a, b):
    return jnp.dot(a, b, preferred_element_type=F32)


def _dot_nt(a, b):
    return lax.dot_general(a, b, (((1,), (1,)), ((), ())), preferred_element_type=F32)


def _mm_body(x_ref, w_ref, o_ref, wb_s):
    @pl.when(pl.program_id(1) == 0)
    def _():
        wb_s[...] = w_ref[...].astype(BF16)

    o_ref[...] = _dot(x_ref[...].astype(BF16), wb_s[...]).astype(o_ref.dtype)


def _matmul(x, w, *, tm, tn, name):
    m, k = x.shape
    n = w.shape[1]
    vmem = (2 * (tm * k * 4 + k * tn * 4 + tm * tn * 4) + k * tn * 2 + tm * k * 2) / 2**20 + 8
    return pl.pallas_call(
        _mm_body,
        out_shape=jax.ShapeDtypeStruct((m, n), F32),
        grid=(n // tn, m // tm),
        in_specs=[pl.BlockSpec((tm, k), lambda j, i: (i, 0)),
                  pl.BlockSpec((k, tn), lambda j, i: (0, j))],
        out_specs=pl.BlockSpec((tm, tn), lambda j, i: (i, j)),
        scratch_shapes=[pltpu.VMEM((k, tn), BF16)],
        compiler_params=_cparams(("parallel", "arbitrary"), vmem),
        name=name,
    )(x, w)


def _attn_prompt_body(q_ref, k_ref, v_ref, o_ref, kt_s, va_s, on_s, lse_s, *, seq, dilations):
    scale = ATT_HD ** -0.5
    nblk = seq // Q_BLOCK
    qi = lax.broadcasted_iota(I32, (Q_BLOCK, Q_BLOCK), 0)
    kj = lax.broadcasted_iota(I32, (Q_BLOCK, Q_BLOCK), 1)
    cur_ok = kj <= qi
    prev_ok = kj >= qi
    va_s[:, :, ATT_HD:] = jnp.ones((nblk, Q_BLOCK, ATT_HD), BF16)

    for br, (_, d) in enumerate(dilations):
        span = d * Q_BLOCK
        nb = seq // span

        def stream_rows(t, d=d, span=span, nb=nb):
            r = t // nb
            ib = t % nb
            return r, ib, pl.ds(r + ib * span, Q_BLOCK, stride=d)

        def prep(g, carry, stream_rows=stream_rows):
            loaded = []
            for j in range(ATTN_GROUP):
                t = g * ATTN_GROUP + j
                _, _, rows = stream_rows(t)
                loaded.append((t, k_ref[rows, :], v_ref[rows, :]))
            for t, kk, vv in loaded:
                kt_s[t] = jnp.transpose(kk).astype(BF16)
                va_s[t, :, 0:ATT_HD] = vv.astype(BF16)
            return carry

        lax.fori_loop(0, nblk // ATTN_GROUP, prep, 0)

        def group(g, carry, br=br, nb=nb, stream_rows=stream_rows):
            scores = []
            for j in range(ATTN_GROUP):
                t = g * ATTN_GROUP + j
                r, ib, rows = stream_rows(t)
                tp = jnp.maximum(t - 1, r * nb)
                q = (q_ref[rows, :] * scale).astype(BF16)
                s = _dot(q, jnp.concatenate([kt_s[tp], kt_s[t]], axis=1))
                scores.append((t, tp, ib, rows, s))
            probs = []
            for t, tp, ib, rows, s in scores:
                ok = jnp.concatenate([jnp.logical_and(prev_ok, ib > 0), cur_ok], axis=1)
                s = jnp.where(ok, s, NEG_INF)
                m = jnp.max(s, axis=-1, keepdims=True)
                probs.append((t, tp, rows, m, jnp.exp(s - m).astype(BF16)))
            outs = [(rows, m, _dot(p, jnp.concatenate([va_s[tp], va_s[t]], axis=0)))
                    for t, tp, rows, m, p in probs]
            for rows, m, al in outs:
                l = al[:, ATT_HD:]
                on_s[br, rows, :] = al[:, :ATT_HD] / l
                lse_s[br, rows, :] = m + jnp.log(l)
            return carry

        lax.fori_loop(0, nblk // ATTN_GROUP, group, 0)

    chunk = 256
    nbr = len(dilations)

    def merge(c, carry):
        rows = pl.ds(pl.multiple_of(c * chunk, chunk), chunk)
        ls = [lse_s[b, rows, :] for b in range(nbr)]
        mx = functools.reduce(jnp.maximum, ls)
        es = [jnp.exp(li - mx) for li in ls]
        num = sum(es[b] * on_s[b, rows, :] for b in range(nbr))
        o_ref[rows, :] = num / sum(es)
        return carry

    lax.fori_loop(0, seq // chunk, merge, 0)


def _attn_prompt(proj, *, n_batch, seq, dilations=DILATIONS):
    for w, d in dilations:
        assert w // d == Q_BLOCK and seq % (d * Q_BLOCK) == 0
    nbr = len(dilations)
    nblk = seq // Q_BLOCK
    assert nblk % ATTN_GROUP == 0
    blk = lambda off: pl.BlockSpec((seq, ATT_HD), lambda b, h, off=off: (b, off + h))
    vmem = ((4 * 2 + 2 * nbr) * seq * ATT_HD * 4 + 3 * seq * ATT_HD * 2) / 2**20 + 8
    return pl.pallas_call(
        functools.partial(_attn_prompt_body, seq=seq, dilations=dilations),
        out_shape=jax.ShapeDtypeStruct((n_batch * seq, D_ATT), F32),
        grid=(n_batch, ATT_HEADS),
        in_specs=[blk(0), blk(ATT_HEADS), blk(2 * ATT_HEADS)],
        out_specs=pl.BlockSpec((seq, ATT_HD), lambda b, h: (b, h)),
        scratch_shapes=[pltpu.VMEM((nblk, ATT_HD, Q_BLOCK), BF16),
                        pltpu.VMEM((nblk, Q_BLOCK, 2 * ATT_HD), BF16),
                        pltpu.VMEM((nbr, seq, ATT_HD), F32),
                        pltpu.VMEM((nbr, seq, ATT_HD), F32)],
        compiler_params=_cparams(("parallel", "parallel"), vmem),
        name="attn_prompt",
    )(proj, proj, proj)


def _sample_key_multiplicity(n_new, n_cache, past_len, dilations):
    d_max = max(d for _, d in dilations)
    tail = max(w for w, d in dilations if d != d_max)
    assert past_len % d_max == 0 and n_cache % d_max == 0 and n_new <= d_max // 2
    assert tail % d_max == 0 and tail <= n_cache
    half = d_max // 2
    n_grid = (n_cache - tail) // d_max
    kv_start = past_len - n_cache
    grid_rows = (np.arange(n_grid)[:, None] * d_max + np.arange(half)[None, :]).reshape(-1)
    tail_rows = n_cache - tail + np.arange(tail)
    new_rows = n_cache + np.arange(n_new)
    qpos = past_len + np.arange(n_new)

    def mult(rows):
        kpos = kv_start + rows
        delta = qpos[:, None] - kpos[None, :]
        c = np.zeros(delta.shape, np.float32)
        for w, d in dilations:
            c += ((delta >= 0) & (delta <= w) & (delta % d == 0) & (kpos[None, :] >= kv_start))
        return c

    fetched = np.zeros(n_cache + n_new, bool)
    fetched[grid_rows] = True
    fetched[tail_rows] = True
    fetched[new_rows] = True
    assert not mult(np.nonzero(~fetched)[0]).any()
    return mult(grid_rows), mult(tail_rows), mult(new_rows), n_grid, tail, half, d_max


def _attn_sample_body(q_ref, kn_ref, vn_ref, kg_ref, kt_ref, vg_ref, vt_ref,
                      cg_ref, ct_ref, cn_ref, o_ref):
    scale = ATT_HD ** -0.5
    heads = lambda ref: jnp.concatenate(
        [ref[:, h * ATT_HD:(h + 1) * ATT_HD] for h in range(ATT_HEADS)], axis=0)
    q = (heads(q_ref) * scale).astype(BF16)
    kn = heads(kn_ref).astype(BF16)
    vn = heads(vn_ref).astype(BF16)
    flat = lambda ref: ref[...].reshape(-1, ATT_HD).astype(BF16)
    cg, ct, cn = cg_ref[...], ct_ref[...], cn_ref[...]
    sg = jnp.where(cg > 0, _dot_nt(q, flat(kg_ref)), NEG_INF)
    st = jnp.where(ct > 0, _dot_nt(q, flat(kt_ref)), NEG_INF)
    sn = jnp.where(cn > 0, _dot_nt(q, kn), NEG_INF)
    m = jnp.maximum(jnp.maximum(jnp.max(sg, axis=-1, keepdims=True),
                                jnp.max(st, axis=-1, keepdims=True)),
                    jnp.max(sn, axis=-1, keepdims=True))
    pg = cg * jnp.exp(sg - m)
    pt = ct * jnp.exp(st - m)
    pn = cn * jnp.exp(sn - m)
    l = (jnp.sum(pg, axis=-1, keepdims=True) + jnp.sum(pt, axis=-1, keepdims=True)
         + jnp.sum(pn, axis=-1, keepdims=True))
    acc = (_dot(pg.astype(BF16), flat(vg_ref)) + _dot(pt.astype(BF16), flat(vt_ref))
           + _dot(pn.astype(BF16), vn))
    out = acc / l
    n_new = q_ref.shape[0]
    for h in range(ATT_HEADS):
        o_ref[:, h * ATT_HD:(h + 1) * ATT_HD] = out[h * n_new:(h + 1) * n_new, :]


def _attn_sample(proj, win_k, win_v, *, row0, n_seq, n_new, past_len=PAST_LEN,
                 dilations=DILATIONS):
    n_cache = win_k.shape[1]
    cg, ct, cn, n_grid, tail, half, d_max = _sample_key_multiplicity(
        n_new, n_cache, past_len, dilations)
    assert row0 % n_new == 0 and n_new % SUBLANES == 0 and n_cache % tail == 0
    eye = np.eye(ATT_HEADS, dtype=np.float32)
    key_major = lambda c: np.einsum("tk,hg->htkg", c, eye).reshape(ATT_HEADS * n_new, -1)
    head_major = lambda c: np.einsum("tk,hg->htgk", c, eye).reshape(ATT_HEADS * n_new, -1)
    cg, ct, cn = key_major(cg), key_major(ct), head_major(cn)
    rb = row0 // n_new
    n_groups = n_cache // d_max
    kgv = win_k.reshape(n_seq, n_groups, d_max, ATT_HEADS, ATT_HD)
    vgv = win_v.reshape(n_seq, n_groups, d_max, ATT_HEADS, ATT_HD)
    ktv = win_k.reshape(n_seq, n_cache // tail, tail, ATT_HEADS, ATT_HD)
    vtv = win_v.reshape(n_seq, n_cache // tail, tail, ATT_HEADS, ATT_HD)
    new = lambda off: pl.BlockSpec((n_new, D_ATT), lambda b, off=off: (rb + b, off))
    grid_spec = pl.BlockSpec((None, n_grid, half, ATT_HEADS, ATT_HD), lambda b: (b, 0, 0, 0, 0))
    tail_spec = pl.BlockSpec((None, None, tail, ATT_HEADS, ATT_HD),
                             lambda b: (b, n_cache // tail - 1, 0, 0, 0))
    const = lambda a: pl.BlockSpec(a.shape, lambda b: (0, 0))
    vmem = (2 * 2 * (n_grid * half + tail) * D_ATT * 4 + 4 * cg.size * 4 * 3) / 2**20 + 12
    return pl.pallas_call(
        _attn_sample_body,
        out_shape=jax.ShapeDtypeStruct((n_seq * n_new, D_ATT), F32),
        grid=(n_seq,),
        in_specs=[new(0), new(1), new(2), grid_spec, tail_spec, grid_spec, tail_spec,
                  const(cg), const(ct), const(cn)],
        out_specs=pl.BlockSpec((n_new, D_ATT), lambda b: (b, 0)),
        compiler_params=_cparams(("parallel",), vmem),
        name="attn_sample",
    )(proj, proj, proj, kgv, ktv, vgv, vtv, jnp.asarray(cg), jnp.asarray(ct), jnp.asarray(cn))


def _gelu_tanh(x):
    return 0.5 * x * (1.0 + jnp.tanh(math.sqrt(2.0 / math.pi) * (x + 0.044715 * (x * x * x))))


def _ssm_body(u_ref, bb_ref, cst_ref, a_ref, ap_ref, d_ref, hre_ref, him_ref, *rest,
              tl, npar, seq_len, nseg, emit_y, exact_in):
    if emit_y:
        y_ref, fre_ref, fim_ref, x_s, h_s = rest
    else:
        fre_ref, fim_ref, x_s, h_s = rest
    ns = SSM_STATES_PER_TILE
    c = pl.program_id(1)
    ngrp = npar // SUBLANES

    def step_rows(i, g):
        return pl.ds(c * tl + i + g * SUBLANES * seq_len, SUBLANES, stride=seq_len)

    @pl.when(c == 0)
    def _init():
        if nseg == 1:
            h_s[0] = hre_ref[...]
            h_s[1] = him_ref[...]
        else:
            pr, pi = ap_ref[0:1, :], ap_ref[1:2, :]
            for b in range(npar // nseg):
                sr = jnp.zeros((1, ns), F32)
                si = jnp.zeros((1, ns), F32)
                for j in range(nseg):
                    row = b * nseg + j
                    h_s[0, row:row + 1, :] = sr
                    h_s[1, row:row + 1, :] = si
                    er, ei = hre_ref[row:row + 1, :], him_ref[row:row + 1, :]
                    sr, si = pr * sr - pi * si + er, pr * si + pi * sr + ei

    u = jnp.concatenate([u_ref[step_rows(i, g), :] for i in range(tl) for g in range(ngrp)], axis=0)
    if exact_in:
        x_s[...] = jnp.dot(u, bb_ref[...], precision=lax.Precision.HIGHEST,
                           preferred_element_type=F32)
    else:
        x_s[...] = _dot(u.astype(BF16), bb_ref[...])

    ar = jnp.broadcast_to(a_ref[0:1, :], (SUBLANES, ns))
    ai = jnp.broadcast_to(a_ref[1:2, :], (SUBLANES, ns))

    def step(i, carry):
        out = []
        for g in range(ngrp):
            hr, hi = carry[2 * g], carry[2 * g + 1]
            rows = pl.ds(pl.multiple_of(i * npar + g * SUBLANES, SUBLANES), SUBLANES)
            nr = ar * hr - ai * hi + x_s[rows, 0:ns]
            ni = ar * hi + ai * hr + x_s[rows, ns:2 * ns]
            if emit_y:
                x_s[rows, 0:ns] = nr
                x_s[rows, ns:2 * ns] = ni
            out += [nr, ni]
        return tuple(out)

    init = []
    for g in range(ngrp):
        gs = slice(g * SUBLANES, (g + 1) * SUBLANES)
        init += [h_s[0, gs, :], h_s[1, gs, :]]
    fin = lax.fori_loop(0, tl, step, tuple(init), unroll=4)
    for g in range(ngrp):
        gs = slice(g * SUBLANES, (g + 1) * SUBLANES)
        h_s[0, gs, :] = fin[2 * g]
        h_s[1, gs, :] = fin[2 * g + 1]

    if emit_y:
        y = _gelu_tanh(_dot(x_s[...].astype(BF16), cst_ref[...]) + d_ref[...] * u)
        for i in range(tl):
            for g in range(ngrp):
                r0 = i * npar + g * SUBLANES
                y_ref[step_rows(i, g), :] = y[r0:r0 + SUBLANES, :]

    @pl.when(c == pl.num_programs(1) - 1)
    def _fin():
        fre_ref[...] = h_s[0]
        fim_ref[...] = h_s[1]


def _ssm_scan(proj, prm, hin_re, hin_im, *, seq_len, tl, nseg, emit_y, exact_in, name):
    rows = proj.shape[0]
    npar = rows // seq_len
    assert npar % SUBLANES == 0 and seq_len % tl == 0
    ns = SSM_STATES_PER_TILE
    nk = D_SSM // SSM_LANE_TILE
    col0 = (proj.shape[1] - D_SSM) // SSM_LANE_TILE
    bb = prm["bb_f32"] if exact_in else prm["bb_bf16"]
    in_specs = [
        pl.BlockSpec((rows, SSM_LANE_TILE), lambda k, c: (0, col0 + k)),
        pl.BlockSpec((None, SSM_LANE_TILE, 2 * ns), lambda k, c: (k, 0, 0)),
        pl.BlockSpec((None, 2 * ns, SSM_LANE_TILE), lambda k, c: (k, 0, 0)),
        pl.BlockSpec((None, 2, ns), lambda k, c: (k, 0, 0)),
        pl.BlockSpec((None, 2, ns), lambda k, c: (k, 0, 0)),
        pl.BlockSpec((1, SSM_LANE_TILE), lambda k, c: (0, k)),
        pl.BlockSpec((npar, ns), lambda k, c: (0, k)),
        pl.BlockSpec((npar, ns), lambda k, c: (0, k)),
    ]
    state_shape = jax.ShapeDtypeStruct((npar, nk * ns), F32)
    state_spec = pl.BlockSpec((npar, ns), lambda k, c: (0, k))
    out_shape = [state_shape, state_shape]
    out_specs = [state_spec, state_spec]
    if emit_y:
        out_shape = [jax.ShapeDtypeStruct((rows, D_SSM), F32)] + out_shape
        out_specs = [pl.BlockSpec((rows, SSM_LANE_TILE), lambda k, c: (0, k))] + out_specs
    vmem = (4 * rows * SSM_LANE_TILE * 4 + tl * npar * 2 * ns * 4) / 2**20 + 16
    return pl.pallas_call(
        functools.partial(_ssm_body, tl=tl, npar=npar, seq_len=seq_len, nseg=nseg, emit_y=emit_y,
                          exact_in=exact_in),
        out_shape=out_shape,
        grid=(nk, seq_len // tl),
        in_specs=in_specs,
        out_specs=out_specs,
        scratch_shapes=[pltpu.VMEM((tl * npar, 2 * ns), F32), pltpu.VMEM((2, npar, ns), F32)],
        compiler_params=_cparams(("parallel", "arbitrary"), vmem),
        name=name,
    )(proj, bb, prm["cst"], prm["a"], prm["apow"], prm["d"], hin_re, hin_im)


def _ssm_params(lam_re, lam_im, log_dt, b_re, b_im, c_re, c_im, d_skip, seg_len):
    g, p, c = N_SSM_GROUPS, SSM_STATE, SSM_GROUP_CH
    nk, gt = g // SSM_GROUPS_PER_TILE, SSM_GROUPS_PER_TILE
    dt = jnp.exp(log_dt.astype(F32))[:, None]
    lr, li = lam_re.astype(F32), lam_im.astype(F32)
    mag = jnp.exp(lr * dt)
    a_re, a_im = mag * jnp.cos(li * dt), mag * jnp.sin(li * dt)
    magp = jnp.exp(lr * dt * seg_len)
    p_re, p_im = magp * jnp.cos(li * dt * seg_len), magp * jnp.sin(li * dt * seg_len)
    den = lr * lr + li * li
    nr, ni = a_re - 1.0, a_im
    f_re, f_im = (nr * lr + ni * li) / den, (ni * lr - nr * li) / den
    br, bi = b_re.astype(F32), b_im.astype(F32)
    bb_re = f_re[..., None] * br - f_im[..., None] * bi
    bb_im = f_re[..., None] * bi + f_im[..., None] * br
    eye = jnp.eye(gt, dtype=F32)

    def pack_b(m):
        return jnp.einsum("kgpc,gh->kgchp", m.reshape(nk, gt, p, c), eye).reshape(nk, gt * c, gt * p)

    def pack_c(m):
        return jnp.einsum("kgcp,gh->kgphc", m.reshape(nk, gt, c, p), eye).reshape(nk, gt * p, gt * c)

    bb = jnp.concatenate([pack_b(bb_re), pack_b(bb_im)], axis=2)
    cst = jnp.concatenate([pack_c(c_re.astype(F32)), -pack_c(c_im.astype(F32))], axis=1)
    tile = lambda v: v.reshape(nk, 1, gt * p)
    return {
        "bb_f32": bb, "bb_bf16": bb.astype(BF16), "cst": cst.astype(BF16),
        "a": jnp.concatenate([tile(a_re), tile(a_im)], axis=1),
        "apow": jnp.concatenate([tile(p_re), tile(p_im)], axis=1),
        "d": d_skip.astype(F32).reshape(1, g * c),
    }


def _glu_body(y_ref, w_ref, o_ref):
    yg = y_ref[...]
    z = _dot(yg.astype(BF16), w_ref[...])
    o_ref[...] = yg * (1.0 / (1.0 + jnp.exp(-z)))


def _glu(yg, w, *, tm, name):
    m, n = yg.shape
    return pl.pallas_call(
        _glu_body,
        out_shape=jax.ShapeDtypeStruct((m, n), F32),
        grid=(m // tm,),
        in_specs=[pl.BlockSpec((tm, n), lambda i: (i, 0)), pl.BlockSpec((n, n), lambda i: (0, 0))],
        out_specs=pl.BlockSpec((tm, n), lambda i: (i, 0)),
        compiler_params=_cparams(("parallel",), 4 * tm * n * 4 / 2**20 + 12),
        name=name,
    )(yg, w)


def _mix_body(attn_ref, ssm_ref, ga_ref, gs_ref, w_ref, x_ref, g_ref, b_ref, o_ref):
    a = _rmsnorm(attn_ref[...], ga_ref[...]).astype(BF16)
    s = _rmsnorm(ssm_ref[...], gs_ref[...]).astype(BF16)
    mix = _dot(a, w_ref[0:D_ATT, :]) + _dot(s, w_ref[D_ATT:D_ATT + D_SSM, :])
    o_ref[...] = _layernorm(DEEPNORM_ALPHA * x_ref[...] + mix, g_ref[...], b_ref[...])


def _mix(attn, ssm, ga, gs, w, x, g, b, *, tm, name):
    m = x.shape[0]
    row = lambda n: pl.BlockSpec((tm, n), lambda i: (i, 0))
    const = lambda a: pl.BlockSpec(a.shape, lambda i: (0, 0))
    return pl.pallas_call(
        _mix_body,
        out_shape=jax.ShapeDtypeStruct((m, D_MODEL), F32),
        grid=(m // tm,),
        in_specs=[row(D_ATT), row(D_SSM), const(ga), const(gs), const(w), row(D_MODEL),
                  const(g), const(b)],
        out_specs=row(D_MODEL),
        compiler_params=_cparams(("parallel",), 6 * tm * D_MODEL * 4 / 2**20 + 24),
        name=name,
    )(attn, ssm, ga, gs, w, x, g, b)


def _store_gatherable(o_ref, y):
    rows = y.shape[0]
    for c in range(ROW_CHUNKS):
        o_ref[pl.ds(c, rows, stride=ROW_PITCH), :] = y[:, c * LANES:(c + 1) * LANES]
    for c in range(ROW_CHUNKS, ROW_PITCH):
        o_ref[pl.ds(c, rows, stride=ROW_PITCH), :] = jnp.zeros((rows, LANES), F32)


def _load_gathered(buf, rows):
    return jnp.concatenate([buf[pl.ds(c, rows, stride=ROW_PITCH), :] for c in range(ROW_CHUNKS)],
                           axis=1)


def _start_row_gather(src_hbm, idx, buf, r, sem):
    pltpu.make_async_copy(src_hbm.at[pl.ds(idx * ROW_PITCH, ROW_CHUNKS), :],
                          buf.at[pl.ds(r * ROW_PITCH, ROW_CHUNKS), :], sem).start()


def _wait_row_gathers(buf, other, rows, sem):
    span = pl.ds(0, rows * ROW_CHUNKS)
    pltpu.make_async_copy(other.at[span, :], buf.at[span, :], sem).wait()


def _mm_ln_body(a1_ref, a2_ref, w_ref, x1_ref, x2_ref, g_ref, b_ref, o_ref, rows_ref, *, tiles1):
    first = pl.program_id(0) < tiles1
    a = jnp.where(first, a1_ref[...], a2_ref[...])
    x = jnp.where(first, x1_ref[...], x2_ref[...])
    y = _dot(a.astype(BF16), w_ref[...])
    out = _layernorm(DEEPNORM_ALPHA * x + y, g_ref[...], b_ref[...])
    o_ref[...] = out
    _store_gatherable(rows_ref, out)


def _mm_ln(a1, a2, w, x1, x2, g, b, *, name):
    tm = a2.shape[0]
    assert a1.shape[0] % tm == 0
    tiles1 = a1.shape[0] // tm
    m = a1.shape[0] + tm
    row1 = lambda n: pl.BlockSpec((tm, n), lambda i: (jnp.minimum(i, tiles1 - 1), 0))
    row2 = lambda n: pl.BlockSpec((tm, n), lambda i: (0, 0))
    const = lambda v: pl.BlockSpec(v.shape, lambda i: (0, 0))
    return pl.pallas_call(
        functools.partial(_mm_ln_body, tiles1=tiles1),
        out_shape=[jax.ShapeDtypeStruct((m, D_MODEL), F32),
                   jax.ShapeDtypeStruct((m * ROW_PITCH, LANES), F32)],
        grid=(tiles1 + 1,),
        in_specs=[row1(a1.shape[1]), row2(a2.shape[1]), const(w), row1(D_MODEL), row2(D_MODEL),
                  const(g), const(b)],
        out_specs=[pl.BlockSpec((tm, D_MODEL), lambda i: (i, 0)),
                   pl.BlockSpec((tm * ROW_PITCH, LANES), lambda i: (i, 0))],
        compiler_params=_cparams(("parallel",), 12 * tm * D_MODEL * 4 / 2**20 + 24),
        name=name,
    )(a1, a2, w, x1, x2, g, b)


def _memattn_body(q_ref, k_ref, v_ref, o_ref):
    scale = MEM_HD ** -0.5
    for h in range(MEM_HEADS):
        sl = slice(h * MEM_HD, (h + 1) * MEM_HD)
        s = _dot_nt(q_ref[:, sl].astype(BF16), k_ref[:, sl].astype(BF16)) * scale
        m = jnp.max(s, axis=-1, keepdims=True)
        p = jnp.exp(s - m)
        l = jnp.sum(p, axis=-1, keepdims=True)
        o_ref[:, sl] = _dot(p.astype(BF16), v_ref[:, sl].astype(BF16)) / l


def _memattn(q, mem_k, mem_v, *, row0, n_seq, seq, tq, name):
    assert seq % tq == 0 and row0 % tq == 0
    nq = seq // tq
    rb = row0 // tq
    mem_spec = pl.BlockSpec((None, N_MEM, D_MODEL), lambda b, i: (b, 0, 0))
    return pl.pallas_call(
        _memattn_body,
        out_shape=jax.ShapeDtypeStruct((n_seq * seq, D_MODEL), F32),
        grid=(n_seq, nq),
        in_specs=[pl.BlockSpec((tq, D_MODEL), lambda b, i: (rb + b * nq + i, 0)),
                  mem_spec, mem_spec],
        out_specs=pl.BlockSpec((tq, D_MODEL), lambda b, i: (b * nq + i, 0)),
        compiler_params=_cparams(("parallel", "parallel"),
                                 4 * (tq + N_MEM) * D_MODEL * 4 / 2**20 + 8),
        name=name,
    )(q, mem_k, mem_v)


def _memattn_heads_body(q_ref, k_ref, v_ref, c_ref, o_ref):
    scale = MEM_HD ** -0.5
    tq = q_ref.shape[0]
    q = jnp.concatenate([q_ref[:, h * MEM_HD:(h + 1) * MEM_HD] for h in range(MEM_HEADS)], axis=0)
    k = k_ref[...].reshape(N_MEM * MEM_HEADS, MEM_HD).astype(BF16)
    v = v_ref[...].reshape(N_MEM * MEM_HEADS, MEM_HD).astype(BF16)
    s = jnp.where(c_ref[...] > 0, _dot_nt(q.astype(BF16), k) * scale, NEG_INF)
    m = jnp.max(s, axis=-1, keepdims=True)
    p = jnp.exp(s - m)
    l = jnp.sum(p, axis=-1, keepdims=True)
    o = _dot(p.astype(BF16), v) / l
    for h in range(MEM_HEADS):
        o_ref[:, h * MEM_HD:(h + 1) * MEM_HD] = o[h * tq:(h + 1) * tq, :]


def _memattn_heads(q, mem_k, mem_v, *, row0, n_seq, seq, name):
    assert row0 % seq == 0 and seq % SUBLANES == 0
    rb = row0 // seq
    same_head = np.kron(np.eye(MEM_HEADS, dtype=np.float32), np.ones((seq, 1), np.float32))
    same_head = np.tile(same_head, (1, N_MEM))
    mem_spec = pl.BlockSpec((None, N_MEM, MEM_HEADS, MEM_HD), lambda b: (b, 0, 0, 0))
    return pl.pallas_call(
        _memattn_heads_body,
        out_shape=jax.ShapeDtypeStruct((n_seq * seq, D_MODEL), F32),
        grid=(n_seq,),
        in_specs=[pl.BlockSpec((seq, D_MODEL), lambda b: (rb + b, 0)), mem_spec, mem_spec,
                  pl.BlockSpec(same_head.shape, lambda b: (0, 0))],
        out_specs=pl.BlockSpec((seq, D_MODEL), lambda b: (b, 0)),
        compiler_params=_cparams(("parallel",), 8 * N_MEM * D_MODEL * 4 / 2**20 + 8),
        name=name,
    )(q, mem_k, mem_v, jnp.asarray(same_head))


def _router_body(x_ref, w_ref, b_ref, sel_ref, wts_ref, cnt_ref, run_s, *, tm):
    i = pl.program_id(0)

    @pl.when(i == 0)
    def _():
        run_s[...] = jnp.zeros_like(run_s)

    ng, epg = N_EXPERT_GROUPS, EXPERTS_PER_GROUP
    x = x_ref[...]
    x_hi = x.astype(BF16)
    x_lo = (x - x_hi.astype(F32)).astype(BF16)
    parts = _dot(x_hi, w_ref[...]) + _dot(x_lo, w_ref[...])
    logits = parts + pltpu.roll(parts, shift=ROUTER_LANES // 2, axis=1) + b_ref[...]
    lane = lax.broadcasted_iota(I32, (tm, ROUTER_LANES), 1)
    big = ROUTER_LANES

    def first_argmax(vals):
        mx = jnp.max(vals, axis=-1, keepdims=True)
        idx = jnp.min(jnp.where(vals == mx, lane, big), axis=-1, keepdims=True)
        return mx, idx

    gl = jnp.where(lane < ng, logits, NEG_INF)
    gmax, gsel = first_argmax(gl)
    g_w = 1.0 / jnp.sum(jnp.exp(gl - gmax), axis=-1, keepdims=True)
    lo = ng + gsel * epg
    el = jnp.where(jnp.logical_and(lane >= lo, lane < lo + epg), logits, NEG_INF)
    v1, i1 = first_argmax(el)
    v2, i2 = first_argmax(jnp.where(lane == i1, NEG_INF, el))
    e21 = jnp.exp(v2 - v1)
    w1 = g_w / (1.0 + e21)
    w2 = g_w * e21 / (1.0 + e21)

    onehot = jnp.logical_or(lane == i1, lane == i2)
    r = lax.broadcasted_iota(I32, (tm, tm), 0)
    cc = lax.broadcasted_iota(I32, (tm, tm), 1)
    tri = (cc < r).astype(BF16)
    before = _dot(tri, onehot.astype(BF16)) + run_s[...]
    rank1 = jnp.sum(jnp.where(lane == i1, before, 0.0), axis=-1, keepdims=True).astype(I32)
    rank2 = jnp.sum(jnp.where(lane == i2, before, 0.0), axis=-1, keepdims=True).astype(I32)
    run_s[...] = run_s[...] + jnp.sum(onehot.astype(F32), axis=0, keepdims=True)

    sel = jnp.where(lane == 0, i1 - ng, jnp.where(lane == 1, i2 - ng,
                    jnp.where(lane == 2, rank1, jnp.where(lane == 3, rank2, 0))))
    sel_ref[...] = sel
    wts_ref[...] = jnp.where(lane == 0, w1, jnp.where(lane == 1, w2, 0.0))
    cnt_ref[...] = run_s[...].astype(I32)


def _router(x, w, b, *, tm):
    m = x.shape[0]
    row = pl.BlockSpec((tm, ROUTER_LANES), lambda i: (i, 0))
    return pl.pallas_call(
        functools.partial(_router_body, tm=tm),
        out_shape=[jax.ShapeDtypeStruct((m, ROUTER_LANES), I32),
                   jax.ShapeDtypeStruct((m, ROUTER_LANES), F32),
                   jax.ShapeDtypeStruct((1, ROUTER_LANES), I32)],
        grid=(m // tm,),
        in_specs=[pl.BlockSpec((tm, D_MODEL), lambda i: (i, 0)),
                  pl.BlockSpec((D_MODEL, ROUTER_LANES), lambda i: (0, 0)),
                  pl.BlockSpec((1, ROUTER_LANES), lambda i: (0, 0))],
        out_specs=[row, row, pl.BlockSpec((1, ROUTER_LANES), lambda i: (0, 0))],
        scratch_shapes=[pltpu.VMEM((1, ROUTER_LANES), F32)],
        compiler_params=_cparams(("arbitrary",), 16),
        name="moe_router",
    )(x, w, b)


def _dispatch_body(pos_ref, pad0_ref, npad_ref, nact_ref, x_hbm, xs_hbm, xbuf, zero_s,
                   in_sem, out_sem, pad_sem, *, td, n_tiles):
    i = pl.program_id(0)
    n = pl.num_programs(0)
    tile_rows = MOE_TILE * ROW_PITCH
    in_rows = td * ROW_PITCH
    zero_row = zero_s.at[pl.ds(0, ROW_PITCH), :]

    def row_copy(src, dst_row, s):
        return pltpu.make_async_copy(src, xs_hbm.at[pl.ds(dst_row * ROW_PITCH, ROW_PITCH), :], s)

    def tile_load(t):
        s = t % DISPATCH_SLOTS
        return pltpu.make_async_copy(x_hbm.at[pl.ds(t * in_rows, in_rows), :], xbuf.at[s],
                                     in_sem.at[s])

    def wait_scatter(par):
        for _ in range(2):
            pltpu.make_async_copy(xbuf.at[0], xs_hbm.at[pl.ds(0, in_rows), :],
                                  out_sem.at[par]).wait()

    @pl.when(i == 0)
    def _():
        for t in range(DISPATCH_SLOTS - 1):
            @pl.when(t < n)
            def _(t=t):
                tile_load(t).start()
        zero_s[...] = jnp.zeros_like(zero_s)

        def pad_start(r, carry, e):
            row_copy(zero_row, pad0_ref[e] + r, pad_sem.at[0]).start()
            return carry

        def pad_wait(r, carry):
            row_copy(zero_row, 0, pad_sem.at[0]).wait()
            return carry

        for e in range(N_EXPERTS):
            lax.fori_loop(0, npad_ref[e], functools.partial(pad_start, e=e), 0)
        for e in range(N_EXPERTS):
            lax.fori_loop(0, npad_ref[e], pad_wait, 0)

        def zero_tile(t, carry):
            parts = [pltpu.make_async_copy(
                zero_s, xs_hbm.at[pl.ds(t * tile_rows + j * MOE_TILE, MOE_TILE), :],
                pad_sem.at[0]) for j in range(ROW_PITCH)]
            for cp in parts:
                cp.start()
            for cp in parts:
                cp.wait()
            return carry

        lax.fori_loop(nact_ref[0], n_tiles, zero_tile, 0)

    slot = i % DISPATCH_SLOTS
    par = i % 2
    tile_load(i).wait()
    base = i * td * 2
    for r in range(td):
        for k in range(2):
            row_copy(xbuf.at[slot, pl.ds(r * ROW_PITCH, ROW_PITCH), :], pos_ref[base + 2 * r + k],
                     out_sem.at[par]).start()

    @pl.when(i > 0)
    def _():
        wait_scatter(1 - par)

    @pl.when(i + DISPATCH_SLOTS - 1 < n)
    def _():
        tile_load(i + DISPATCH_SLOTS - 1).start()

    @pl.when(i == n - 1)
    def _():
        wait_scatter(par)


def _moe_dispatch(x_rows, pos, pad_start, pad_count, nact, *, td, n_tiles):
    n = x_rows.shape[0] // ROW_PITCH
    grid_spec = pltpu.PrefetchScalarGridSpec(
        num_scalar_prefetch=4,
        grid=(n // td,),
        in_specs=[pl.BlockSpec(memory_space=pl.ANY)],
        out_specs=pl.BlockSpec(memory_space=pl.ANY),
        scratch_shapes=[pltpu.VMEM((DISPATCH_SLOTS, td * ROW_PITCH, LANES), F32),
                        pltpu.VMEM((MOE_TILE, LANES), F32),
                        pltpu.SemaphoreType.DMA((DISPATCH_SLOTS,)),
                        pltpu.SemaphoreType.DMA((2,)),
                        pltpu.SemaphoreType.DMA((1,))],
    )
    return pl.pallas_call(
        functools.partial(_dispatch_body, td=td, n_tiles=n_tiles),
        out_shape=jax.ShapeDtypeStruct((n_tiles * MOE_TILE * ROW_PITCH, LANES), F32),
        grid_spec=grid_spec,
        compiler_params=_cparams(("arbitrary",), 16),
        name="moe_dispatch",
    )(pos, pad_start, pad_count, nact, x_rows)


def _moe_body(te_ref, ord_ref, nxt_ref, nact_ref, x_ref, wg_hbm, wu_hbm, wd_hbm, o_ref,
              wg_f, wu_f, wd_f, wsem, wg_s, wu_s, wd_s):
    i = pl.program_id(0)
    nact = nact_ref[0]
    tm = MOE_TILE

    def weight_copies(expert, ws):
        return [pltpu.make_async_copy(hbm.at[expert], stage.at[ws], wsem.at[ws])
                for hbm, stage in ((wg_hbm, wg_f), (wu_hbm, wu_f), (wd_hbm, wd_f))]

    def tile_step():
        prev = te_ref[jnp.maximum(i - 1, 0)]

        @pl.when(jnp.logical_or(i == 0, te_ref[i] != prev))
        def _():
            ws = ord_ref[i] % 2
            for cp in weight_copies(te_ref[i], ws):
                cp.wait()
            wg_s[...] = wg_f[ws].astype(BF16)
            wu_s[...] = wu_f[ws].astype(BF16)
            wd_s[...] = wd_f[ws].astype(BF16)

            @pl.when(nxt_ref[i] >= 0)
            def _():
                for cp in weight_copies(nxt_ref[i], 1 - ws):
                    cp.start(priority=1)

        x = _load_gathered(x_ref, tm).astype(BF16)
        hg = _dot(x, wg_s[...])
        hu = _dot(x, wu_s[...])
        h = hg * (1.0 / (1.0 + jnp.exp(-hg))) * hu
        _store_gatherable(o_ref, _dot(h.astype(BF16), wd_s[...]))

    @pl.when(i == 0)
    def _():
        for cp in weight_copies(te_ref[0], 0):
            cp.start(priority=1)

    @pl.when(i < nact)
    def _():
        tile_step()

    @pl.when(i >= nact)
    def _():
        o_ref[...] = jnp.zeros_like(o_ref)


def _moe_experts(x_sorted, w_gate, w_up, w_down, tile_expert, tile_ord, tile_next, nact, *,
                 n_tiles):
    tm = MOE_TILE
    in_map = lambda i, te, od, nx, n: (jnp.minimum(i, n[0] - 1), 0)
    any_spec = pl.BlockSpec(memory_space=pl.ANY)
    grid_spec = pltpu.PrefetchScalarGridSpec(
        num_scalar_prefetch=4,
        grid=(n_tiles,),
        in_specs=[pl.BlockSpec((tm * ROW_PITCH, LANES), in_map), any_spec, any_spec, any_spec],
        out_specs=pl.BlockSpec((tm * ROW_PITCH, LANES), lambda i, te, od, nx, n: (i, 0)),
        scratch_shapes=[pltpu.VMEM((2, D_MODEL, D_EXPERT), F32),
                        pltpu.VMEM((2, D_MODEL, D_EXPERT), F32),
                        pltpu.VMEM((2, D_EXPERT, D_MODEL), F32),
                        pltpu.SemaphoreType.DMA((2,)),
                        pltpu.VMEM((D_MODEL, D_EXPERT), BF16),
                        pltpu.VMEM((D_MODEL, D_EXPERT), BF16),
                        pltpu.VMEM((D_EXPERT, D_MODEL), BF16)],
    )
    return pl.pallas_call(
        _moe_body,
        out_shape=jax.ShapeDtypeStruct((n_tiles * tm * ROW_PITCH, LANES), F32),
        grid_spec=grid_spec,
        compiler_params=_cparams(("arbitrary",), 48),
        name="moe_experts",
    )(tile_expert, tile_ord, tile_next, nact, x_sorted, w_gate, w_up, w_down)


def _combine_body(pos_ref, ys_hbm, wts_ref, x_ref, g_ref, b_ref, o1_ref, o2_ref, buf, sem, *,
                  tc, tiles1):
    i = pl.program_id(0)
    n = pl.num_programs(0)
    slot = i % GATHER_SLOTS
    ahead = GATHER_SLOTS - 1

    def issue_gather(tile, slot_):
        base = tile * tc * 2
        for r in range(tc):
            for k in range(2):
                _start_row_gather(ys_hbm, pos_ref[base + 2 * r + k], buf.at[slot_, k], r,
                                  sem.at[slot_])

    @pl.when(i == 0)
    def _():
        for t in range(ahead):
            @pl.when(t < n)
            def _(t=t):
                issue_gather(t, t)

    for k in range(2):
        _wait_row_gathers(buf.at[slot, k], buf.at[(i + 1) % GATHER_SLOTS, k], tc, sem.at[slot])

    @pl.when(i + ahead < n)
    def _():
        issue_gather(i + ahead, (i + ahead) % GATHER_SLOTS)

    w = wts_ref[...]
    moe = (w[:, 0:1] * _load_gathered(buf.at[slot, 0], tc)
           + w[:, 1:2] * _load_gathered(buf.at[slot, 1], tc))
    out = _layernorm(DEEPNORM_ALPHA * x_ref[...] + moe, g_ref[...], b_ref[...])

    @pl.when(i < tiles1)
    def _():
        o1_ref[...] = out

    @pl.when(i >= tiles1)
    def _():
        o2_ref[...] = out


def _moe_combine(ys, pos, wts, x, g, b, *, tc, n_first):
    m = x.shape[0]
    assert n_first % tc == 0 and (m - n_first) % tc == 0
    tiles1 = n_first // tc
    grid_spec = pltpu.PrefetchScalarGridSpec(
        num_scalar_prefetch=1,
        grid=(m // tc,),
        in_specs=[pl.BlockSpec(memory_space=pl.ANY),
                  pl.BlockSpec((tc, ROUTER_LANES), lambda i, p: (i, 0)),
                  pl.BlockSpec((tc, D_MODEL), lambda i, p: (i, 0)),
                  pl.BlockSpec((1, D_MODEL), lambda i, p: (0, 0)),
                  pl.BlockSpec((1, D_MODEL), lambda i, p: (0, 0))],
        out_specs=[pl.BlockSpec((tc, D_MODEL), lambda i, p: (jnp.minimum(i, tiles1 - 1), 0)),
                   pl.BlockSpec((tc, D_MODEL), lambda i, p: (jnp.maximum(i - tiles1, 0), 0))],
        scratch_shapes=[pltpu.VMEM((GATHER_SLOTS, 2, tc * ROW_PITCH, LANES), F32),
                        pltpu.SemaphoreType.DMA((GATHER_SLOTS,))],
    )
    return pl.pallas_call(
        functools.partial(_combine_body, tc=tc, tiles1=tiles1),
        out_shape=[jax.ShapeDtypeStruct((n_first, D_MODEL), F32),
                   jax.ShapeDtypeStruct((m - n_first, D_MODEL), F32)],
        grid_spec=grid_spec,
        compiler_params=_cparams(("arbitrary",), 16 * tc * D_MODEL * 4 / 2**20 + 8),
        name="moe_combine_ln3",
    )(pos, ys, wts, x, g, b)


def _moe(x, x_rows, w_r1, b_r1, w_r2, b_r2, w_gate, w_up, w_down, g, b, *, n_first, tm_router, tc):
    n = x.shape[0]
    ng, ne = N_EXPERT_GROUPS, N_EXPERTS
    half = ROUTER_LANES // 2
    assert ng + ne <= half
    w_r = jnp.concatenate([w_r1, w_r2.reshape(D_MODEL, ne),
                           jnp.zeros((D_MODEL, half - ng - ne), F32)], axis=1)
    w_hi = w_r.astype(BF16)
    w_lo = (w_r - w_hi.astype(F32)).astype(BF16)
    w_r = jnp.concatenate([w_hi, w_lo], axis=1)
    b_r = jnp.concatenate([b_r1, b_r2.reshape(ne), jnp.zeros((half - ng - ne,), F32)])
    b_r = jnp.concatenate([b_r, b_r]).reshape(1, ROUTER_LANES)
    sel, wts, cnt = _router(x, w_r, b_r, tm=tm_router)

    tm = MOE_TILE
    n_tiles = (2 * n) // tm + ne
    counts = cnt[0, ng:ng + ne]
    tiles_per = (counts + tm - 1) // tm
    tile_end = jnp.cumsum(tiles_per)
    row_off = (tile_end - tiles_per) * tm
    nact = tile_end[-1]
    ids, ranks = sel[:, 0:2], sel[:, 2:4]
    pos = row_off[ids] + ranks
    tile_ids = jnp.minimum(jnp.arange(n_tiles, dtype=I32), nact - 1)
    tile_expert = jnp.sum((tile_end[None, :] <= tile_ids[:, None]).astype(I32), axis=1)
    pos = pos.reshape(-1).astype(I32)
    used = tiles_per > 0
    eid = jnp.arange(ne, dtype=I32)
    ordinal = jnp.cumsum(used.astype(I32)) - 1
    later = jnp.where(jnp.logical_and(used[None, :], eid[None, :] > eid[:, None]), eid[None, :], ne)
    nxt = jnp.min(later, axis=1)
    nxt = jnp.where(nxt == ne, -1, nxt)

    nact = nact.reshape(1).astype(I32)
    x_sorted = _moe_dispatch(x_rows, pos, (row_off + counts).astype(I32),
                             (tiles_per * tm - counts).astype(I32), nact, td=tc, n_tiles=n_tiles)
    ys = _moe_experts(x_sorted, w_gate, w_up, w_down, tile_expert, ordinal[tile_expert],
                      nxt[tile_expert], nact, n_tiles=n_tiles)
    return _moe_combine(ys, pos, wts, x, g, b, tc=tc, n_first=n_first)


def _row_tile(m, cap):
    best = SUBLANES
    for t in range(SUBLANES, cap + 1, SUBLANES):
        if m % t == 0:
            best = t
    return best


def kernel(x_prompt, x_sample, cache_win_k, cache_win_v, state_ssm_re, state_ssm_im, cache_mem_k, cache_mem_v, mem_prompt, w_in, ssm_lam_re, ssm_lam_im, ssm_log_dt, ssm_b_re, ssm_b_im, ssm_c_re, ssm_c_im, ssm_d, w_glu, g_attn, g_ssm, w_out, ln1_g, ln1_b, w_mq, w_mk, w_mv, w_mo, ln2_g, ln2_b, w_r1, b_r1, w_r2, b_r2, w_gate, w_up, w_down, ln3_g, ln3_b):
    nb, seq, d = x_prompt.shape
    ns, dseq, _ = x_sample.shape
    n_p, n_s = nb * seq, ns * dseq
    n = n_p + n_s
    l = 0
    row2 = lambda v: v[l].reshape(1, -1)

    x_p, x_s = x_prompt.reshape(n_p, d), x_sample.reshape(n_s, d)
    tm_p = _row_tile(n_p, 1024)
    tm_ln = _row_tile(n_p, 512)
    assert n_p % n_s == 0 and n_s % SUBLANES == 0

    proj_p = _matmul(x_p, w_in[l], tm=tm_p, tn=1024, name="proj_in_prompt")
    proj_s = _matmul(x_s, w_in[l], tm=n_s, tn=1024, name="proj_in_sample")

    attn_p = _attn_prompt(proj_p, n_batch=nb, seq=seq)
    attn_s = _attn_sample(proj_s, cache_win_k[l], cache_win_v[l], row0=0, n_seq=ns, n_new=dseq)

    seg_len = seq // SSM_SEGMENTS
    prm = _ssm_params(ssm_lam_re[l], ssm_lam_im[l], ssm_log_dt[l], ssm_b_re[l], ssm_b_im[l],
                      ssm_c_re[l], ssm_c_im[l], ssm_d[l], seg_len)
    zeros = jnp.zeros((nb * SSM_SEGMENTS, N_SSM_GROUPS * SSM_STATE), F32)
    tl = _row_tile(seg_len, 32)
    end_re, end_im = _ssm_scan(proj_p, prm, zeros, zeros, seq_len=seg_len, tl=tl, nseg=1,
                               emit_y=False, exact_in=False, name="ssm_state_prompt")
    yg_p, fin_re, fin_im = _ssm_scan(proj_p, prm, end_re, end_im, seq_len=seg_len, tl=tl,
                                     nseg=SSM_SEGMENTS, emit_y=True, exact_in=False,
                                     name="ssm_scan_prompt")
    last = SSM_SEGMENTS - 1
    ssm_re_p = fin_re.reshape(nb, SSM_SEGMENTS, N_SSM_GROUPS, SSM_STATE)[:, last]
    ssm_im_p = fin_im.reshape(nb, SSM_SEGMENTS, N_SSM_GROUPS, SSM_STATE)[:, last]

    h0_re = state_ssm_re[l].reshape(ns, -1)
    h0_im = state_ssm_im[l].reshape(ns, -1)
    yg_s, ssm_re_s, ssm_im_s = _ssm_scan(proj_s, prm, h0_re, h0_im, seq_len=dseq, tl=dseq, nseg=1,
                                         emit_y=True, exact_in=True, name="ssm_scan_sample")
    w_glu_b = w_glu[l].astype(BF16)
    ssm_out_p = _glu(yg_p, w_glu_b, tm=tm_p, name="ssm_glu_prompt")
    ssm_out_s = _glu(yg_s, w_glu_b, tm=n_s, name="ssm_glu_sample")

    mix_args = (row2(g_attn), row2(g_ssm), w_out[l].astype(BF16))
    ln1 = (row2(ln1_g), row2(ln1_b))
    x1_p = _mix(attn_p, ssm_out_p, *mix_args, x_p, *ln1, tm=tm_ln, name="mix_out_ln1_prompt")
    x1_s = _mix(attn_s, ssm_out_s, *mix_args, x_s, *ln1, tm=n_s, name="mix_out_ln1_sample")

    mem_rows = mem_prompt.reshape(nb * N_MEM, d)
    mem_k = _matmul(mem_rows, w_mk[l], tm=nb * N_MEM, tn=1024, name="mem_k")
    mem_v = _matmul(mem_rows, w_mv[l], tm=nb * N_MEM, tn=1024, name="mem_v")
    q_p = _matmul(x1_p, w_mq[l], tm=tm_p, tn=1024, name="mem_q_prompt")
    q_s = _matmul(x1_s, w_mq[l], tm=n_s, tn=1024, name="mem_q_sample")
    o_p = _memattn(q_p, mem_k.reshape(nb, N_MEM, d), mem_v.reshape(nb, N_MEM, d),
                   row0=0, n_seq=nb, seq=seq, tq=_row_tile(seq, 512), name="memattn_prompt")
    o_s = _memattn_heads(q_s, cache_mem_k[l], cache_mem_v[l], row0=0, n_seq=ns, seq=dseq,
                         name="memattn_sample")
    x2, x2_rows = _mm_ln(o_p, o_s, w_mo[l].astype(BF16), x1_p, x1_s, row2(ln2_g), row2(ln2_b),
                         name="mem_out_ln2")

    y_p, y_s = _moe(x2, x2_rows, w_r1[l], b_r1[l], w_r2[l], b_r2[l], w_gate[l], w_up[l],
                    w_down[l], row2(ln3_g), row2(ln3_b), n_first=n_p,
                    tm_router=_row_tile(n_s, 256), tc=_row_tile(n_s, 128))

    y_p = y_p.reshape(nb, seq, d)
    y_s = y_s.reshape(ns, dseq, d)
    k_p = proj_p[:, D_ATT:2 * D_ATT].reshape(nb, seq, ATT_HEADS, ATT_HD)
    v_p = proj_p[:, 2 * D_ATT:3 * D_ATT].reshape(nb, seq, ATT_HEADS, ATT_HD)
    wp = min(max(w for w, _ in DILATIONS), seq)
    k_s = proj_s[:, D_ATT:2 * D_ATT].reshape(ns, dseq, ATT_HEADS, ATT_HD)
    v_s = proj_s[:, 2 * D_ATT:3 * D_ATT].reshape(ns, dseq, ATT_HEADS, ATT_HD)
    state = lambda v, b_: v.reshape(1, b_, N_SSM_GROUPS, SSM_STATE)
    return (y_p, y_s, k_p[None, :, seq - wp:], v_p[None, :, seq - wp:], k_s[None], v_s[None],
            state(ssm_re_p, nb), state(ssm_im_p, nb), state(ssm_re_s, ns), state(ssm_im_s, ns),
            mem_k.reshape(1, nb, N_MEM, MEM_HEADS, MEM_HD),
            mem_v.reshape(1, nb, N_MEM, MEM_HEADS, MEM_HD))
```

```python
import functools
import math

import numpy as np
import jax
import jax.numpy as jnp
from jax import lax
from jax.experimental import pallas as pl
from jax.experimental.pallas import tpu as pltpu

F32 = jnp.float32
BF16 = jnp.bfloat16
I32 = jnp.int32

D_MODEL = 2048
PAST_LEN = 8192
D_ATT = D_MODEL // 2
ATT_HEADS = 8
ATT_HD = D_ATT // ATT_HEADS
DILATIONS = ((128, 1), (512, 4), (2048, 16))
D_SSM = D_MODEL - D_ATT
SSM_GROUP_CH = 16
N_SSM_GROUPS = D_SSM // SSM_GROUP_CH
SSM_STATE = 64
N_MEM = 256
MEM_HEADS = 4
MEM_HD = D_MODEL // MEM_HEADS
N_EXPERT_GROUPS = 4
EXPERTS_PER_GROUP = 8
N_EXPERTS = N_EXPERT_GROUPS * EXPERTS_PER_GROUP
D_EXPERT = D_MODEL // 4
DEPTH = 1
DEEPNORM_ALPHA = (2.0 * DEPTH) ** 0.25
LN_EPS = 1e-5
RMS_EPS = 1e-6

LANES = 128
SUBLANES = 8
ROW_CHUNKS = D_MODEL // LANES
ROW_PITCH = ROW_CHUNKS + 1
Q_BLOCK = 128
ATTN_GROUP = 8
SSM_LANE_TILE = 128
SSM_GROUPS_PER_TILE = SSM_LANE_TILE // SSM_GROUP_CH
SSM_STATES_PER_TILE = SSM_GROUPS_PER_TILE * SSM_STATE
SSM_SEGMENTS = 8
MOE_TILE = 256
GATHER_SLOTS = 3
DISPATCH_SLOTS = 3
ROUTER_LANES = 128
NEG_INF = float("-inf")


def _cparams(semantics, vmem_mib):
    return pltpu.CompilerParams(dimension_semantics=semantics,
                                vmem_limit_bytes=int(vmem_mib) << 20)


def _layernorm(y, g, b):
    mu = jnp.mean(y, axis=-1, keepdims=True)
    yc = y - mu
    var = jnp.mean(yc * yc, axis=-1, keepdims=True)
    return yc * lax.rsqrt(var + LN_EPS) * g + b


def _rmsnorm(v, g):
    return v * lax.rsqrt(jnp.mean(v * v, axis=-1, keepdims=True) + RMS_EPS) * g


def _dot(a, b):
    return jnp.dot(a, b, preferred_element_type=F32)


def _dot_nt(a, b):
    return lax.dot_general(a, b, (((1,), (1,)), ((), ())), preferred_element_type=F32)


def _mm_body(x_ref, w_ref, o_ref, wb_s):
    @pl.when(pl.program_id(1) == 0)
    def _():
        wb_s[...] = w_ref[...].astype(BF16)

    o_ref[...] = _dot(x_ref[...].astype(BF16), wb_s[...]).astype(o_ref.dtype)


def _matmul(x, w, *, tm, tn, name):
    m, k = x.shape
    n = w.shape[1]
    vmem = (2 * (tm * k * 4 + k * tn * 4 + tm * tn * 4) + k * tn * 2 + tm * k * 2) / 2**20 + 8
    return pl.pallas_call(
        _mm_body,
        out_shape=jax.ShapeDtypeStruct((m, n), F32),
        grid=(n // tn, m // tm),
        in_specs=[pl.BlockSpec((tm, k), lambda j, i: (i, 0)),
                  pl.BlockSpec((k, tn), lambda j, i: (0, j))],
        out_specs=pl.BlockSpec((tm, tn), lambda j, i: (i, j)),
        scratch_shapes=[pltpu.VMEM((k, tn), BF16)],
        compiler_params=_cparams(("parallel", "arbitrary"), vmem),
        name=name,
    )(x, w)


def _kv_window_body(k_ref, v_ref, ko_ref, vo_ref):
    for h in range(ATT_HEADS):
        sl = slice(h * ATT_HD, (h + 1) * ATT_HD)
        ko_ref[:, h, :] = k_ref[:, sl]
        vo_ref[:, h, :] = v_ref[:, sl]


def _kv_window(proj, *, n_batch, seq, window, tr):
    assert window % tr == 0 and seq % tr == 0
    per, first = seq // tr, (seq - window) // tr
    col = lambda c: pl.BlockSpec((tr, D_ATT), lambda b, t: (b * per + first + t, c))
    out_spec = pl.BlockSpec((None, tr, ATT_HEADS, ATT_HD), lambda b, t: (b, t, 0, 0))
    out_shape = jax.ShapeDtypeStruct((n_batch, window, ATT_HEADS, ATT_HD), F32)
    return pl.pallas_call(
        _kv_window_body,
        out_shape=[out_shape, out_shape],
        grid=(n_batch, window // tr),
        in_specs=[col(1), col(2)],
        out_specs=[out_spec, out_spec],
        compiler_params=_cparams(("parallel", "parallel"), 8 * tr * D_ATT * 4 / 2**20 + 8),
        name="kv_window",
    )(proj, proj)


def _attn_prompt_body(q_ref, k_ref, v_ref, o_ref, kt_s, va_s, on_s, lse_s, *, seq, dilations):
    scale = ATT_HD ** -0.5
    nblk = seq // Q_BLOCK
    qi = lax.broadcasted_iota(I32, (Q_BLOCK, Q_BLOCK), 0)
    kj = lax.broadcasted_iota(I32, (Q_BLOCK, Q_BLOCK), 1)
    cur_ok = kj <= qi
    prev_ok = kj >= qi
    va_s[:, :, ATT_HD:] = jnp.ones((nblk, Q_BLOCK, ATT_HD), BF16)

    for br, (_, d) in enumerate(dilations):
        span = d * Q_BLOCK
        nb = seq // span

        def stream_rows(t, d=d, span=span, nb=nb):
            r = t // nb
            ib = t % nb
            return r, ib, pl.ds(r + ib * span, Q_BLOCK, stride=d)

        def prep(g, carry, stream_rows=stream_rows):
            loaded = []
            for j in range(ATTN_GROUP):
                t = g * ATTN_GROUP + j
                _, _, rows = stream_rows(t)
                loaded.append((t, k_ref[rows, :], v_ref[rows, :]))
            for t, kk, vv in loaded:
                kt_s[t] = jnp.transpose(kk).astype(BF16)
                va_s[t, :, 0:ATT_HD] = vv.astype(BF16)
            return carry

        lax.fori_loop(0, nblk // ATTN_GROUP, prep, 0)

        def group(g, carry, br=br, nb=nb, stream_rows=stream_rows):
            scores = []
            for j in range(ATTN_GROUP):
                t = g * ATTN_GROUP + j
                r, ib, rows = stream_rows(t)
                tp = jnp.maximum(t - 1, r * nb)
                q = (q_ref[rows, :] * scale).astype(BF16)
                s = _dot(q, jnp.concatenate([kt_s[tp], kt_s[t]], axis=1))
                scores.append((t, tp, ib, rows, s))
            probs = []
            for t, tp, ib, rows, s in scores:
                ok = jnp.concatenate([jnp.logical_and(prev_ok, ib > 0), cur_ok], axis=1)
                s = jnp.where(ok, s, NEG_INF)
                m = jnp.max(s, axis=-1, keepdims=True)
                probs.append((t, tp, rows, m, jnp.exp(s - m).astype(BF16)))
            outs = [(rows, m, _dot(p, jnp.concatenate([va_s[tp], va_s[t]], axis=0)))
                    for t, tp, rows, m, p in probs]
            for rows, m, al in outs:
                l = al[:, ATT_HD:]
                on_s[br, rows, :] = al[:, :ATT_HD] / l
                lse_s[br, rows, :] = m + jnp.log(l)
            return carry

        lax.fori_loop(0, nblk // ATTN_GROUP, group, 0)

    chunk = 256
    nbr = len(dilations)

    def merge(c, carry):
        rows = pl.ds(pl.multiple_of(c * chunk, chunk), chunk)
        ls = [lse_s[b, rows, :] for b in range(nbr)]
        mx = functools.reduce(jnp.maximum, ls)
        es = [jnp.exp(li - mx) for li in ls]
        num = sum(es[b] * on_s[b, rows, :] for b in range(nbr))
        o_ref[rows, :] = num / sum(es)
        return carry

    lax.fori_loop(0, seq // chunk, merge, 0)


def _attn_prompt(proj, *, n_batch, seq, dilations=DILATIONS):
    for w, d in dilations:
        assert w // d == Q_BLOCK and seq % (d * Q_BLOCK) == 0
    nbr = len(dilations)
    nblk = seq // Q_BLOCK
    assert nblk % ATTN_GROUP == 0
    blk = lambda off: pl.BlockSpec((seq, ATT_HD), lambda b, h, off=off: (b, off + h))
    vmem = ((4 * 2 + 2 * nbr) * seq * ATT_HD * 4 + 3 * seq * ATT_HD * 2) / 2**20 + 8
    return pl.pallas_call(
        functools.partial(_attn_prompt_body, seq=seq, dilations=dilations),
        out_shape=jax.ShapeDtypeStruct((n_batch * seq, D_ATT), F32),
        grid=(n_batch, ATT_HEADS),
        in_specs=[blk(0), blk(ATT_HEADS), blk(2 * ATT_HEADS)],
        out_specs=pl.BlockSpec((seq, ATT_HD), lambda b, h: (b, h)),
        scratch_shapes=[pltpu.VMEM((nblk, ATT_HD, Q_BLOCK), BF16),
                        pltpu.VMEM((nblk, Q_BLOCK, 2 * ATT_HD), BF16),
                        pltpu.VMEM((nbr, seq, ATT_HD), F32),
                        pltpu.VMEM((nbr, seq, ATT_HD), F32)],
        compiler_params=_cparams(("parallel", "parallel"), vmem),
        name="attn_prompt",
    )(proj, proj, proj)


def _sample_key_multiplicity(n_new, n_cache, past_len, dilations):
    d_max = max(d for _, d in dilations)
    tail = max(w for w, d in dilations if d != d_max)
    assert past_len % d_max == 0 and n_cache % d_max == 0 and n_new <= d_max // 2
    assert tail % d_max == 0 and tail <= n_cache
    half = d_max // 2
    n_grid = (n_cache - tail) // d_max
    kv_start = past_len - n_cache
    grid_rows = (np.arange(n_grid)[:, None] * d_max + np.arange(half)[None, :]).reshape(-1)
    tail_rows = n_cache - tail + np.arange(tail)
    new_rows = n_cache + np.arange(n_new)
    qpos = past_len + np.arange(n_new)

    def mult(rows):
        kpos = kv_start + rows
        delta = qpos[:, None] - kpos[None, :]
        c = np.zeros(delta.shape, np.float32)
        for w, d in dilations:
            c += ((delta >= 0) & (delta <= w) & (delta % d == 0) & (kpos[None, :] >= kv_start))
        return c

    fetched = np.zeros(n_cache + n_new, bool)
    fetched[grid_rows] = True
    fetched[tail_rows] = True
    fetched[new_rows] = True
    assert not mult(np.nonzero(~fetched)[0]).any()
    return mult(grid_rows), mult(tail_rows), mult(new_rows), n_grid, tail, half, d_max


def _attn_sample_body(q_ref, kn_ref, vn_ref, kg_ref, kt_ref, vg_ref, vt_ref,
                      cg_ref, ct_ref, cn_ref, o_ref):
    scale = ATT_HD ** -0.5
    heads = lambda ref: jnp.concatenate(
        [ref[:, h * ATT_HD:(h + 1) * ATT_HD] for h in range(ATT_HEADS)], axis=0)
    q = (heads(q_ref) * scale).astype(BF16)
    kn = heads(kn_ref).astype(BF16)
    vn = heads(vn_ref).astype(BF16)
    flat = lambda ref: ref[...].reshape(-1, ATT_HD).astype(BF16)
    cg, ct, cn = cg_ref[...], ct_ref[...], cn_ref[...]
    sg = jnp.where(cg > 0, _dot_nt(q, flat(kg_ref)), NEG_INF)
    st = jnp.where(ct > 0, _dot_nt(q, flat(kt_ref)), NEG_INF)
    sn = jnp.where(cn > 0, _dot_nt(q, kn), NEG_INF)
    m = jnp.maximum(jnp.maximum(jnp.max(sg, axis=-1, keepdims=True),
                                jnp.max(st, axis=-1, keepdims=True)),
                    jnp.max(sn, axis=-1, keepdims=True))
    pg = cg * jnp.exp(sg - m)
    pt = ct * jnp.exp(st - m)
    pn = cn * jnp.exp(sn - m)
    l = (jnp.sum(pg, axis=-1, keepdims=True) + jnp.sum(pt, axis=-1, keepdims=True)
         + jnp.sum(pn, axis=-1, keepdims=True))
    acc = (_dot(pg.astype(BF16), flat(vg_ref)) + _dot(pt.astype(BF16), flat(vt_ref))
           + _dot(pn.astype(BF16), vn))
    out = acc / l
    n_new = q_ref.shape[0]
    for h in range(ATT_HEADS):
        o_ref[:, h * ATT_HD:(h + 1) * ATT_HD] = out[h * n_new:(h + 1) * n_new, :]


def _attn_sample(proj, win_k, win_v, *, row0, n_seq, n_new, past_len=PAST_LEN,
                 dilations=DILATIONS):
    n_cache = win_k.shape[1]
    cg, ct, cn, n_grid, tail, half, d_max = _sample_key_multiplicity(
        n_new, n_cache, past_len, dilations)
    assert row0 % n_new == 0 and n_new % SUBLANES == 0 and n_cache % tail == 0
    eye = np.eye(ATT_HEADS, dtype=np.float32)
    key_major = lambda c: np.einsum("tk,hg->htkg", c, eye).reshape(ATT_HEADS * n_new, -1)
    head_major = lambda c: np.einsum("tk,hg->htgk", c, eye).reshape(ATT_HEADS * n_new, -1)
    cg, ct, cn = key_major(cg), key_major(ct), head_major(cn)
    rb = row0 // n_new
    n_groups = n_cache // d_max
    kgv = win_k.reshape(n_seq, n_groups, d_max, ATT_HEADS, ATT_HD)
    vgv = win_v.reshape(n_seq, n_groups, d_max, ATT_HEADS, ATT_HD)
    ktv = win_k.reshape(n_seq, n_cache // tail, tail, ATT_HEADS, ATT_HD)
    vtv = win_v.reshape(n_seq, n_cache // tail, tail, ATT_HEADS, ATT_HD)
    new = lambda off: pl.BlockSpec((n_new, D_ATT), lambda b, off=off: (rb + b, off))
    grid_spec = pl.BlockSpec((None, n_grid, half, ATT_HEADS, ATT_HD), lambda b: (b, 0, 0, 0, 0))
    tail_spec = pl.BlockSpec((None, None, tail, ATT_HEADS, ATT_HD),
                             lambda b: (b, n_cache // tail - 1, 0, 0, 0))
    const = lambda a: pl.BlockSpec(a.shape, lambda b: (0, 0))
    vmem = (2 * 2 * (n_grid * half + tail) * D_ATT * 4 + 4 * cg.size * 4 * 3) / 2**20 + 12
    return pl.pallas_call(
        _attn_sample_body,
        out_shape=jax.ShapeDtypeStruct((n_seq * n_new, D_ATT), F32),
        grid=(n_seq,),
        in_specs=[new(0), new(1), new(2), grid_spec, tail_spec, grid_spec, tail_spec,
                  const(cg), const(ct), const(cn)],
        out_specs=pl.BlockSpec((n_new, D_ATT), lambda b: (b, 0)),
        compiler_params=_cparams(("parallel",), vmem),
        name="attn_sample",
    )(proj, proj, proj, kgv, ktv, vgv, vtv, jnp.asarray(cg), jnp.asarray(ct), jnp.asarray(cn))


def _gelu_tanh(x):
    return 0.5 * x * (1.0 + jnp.tanh(math.sqrt(2.0 / math.pi) * (x + 0.044715 * (x * x * x))))


def _ssm_body(u_ref, bb_ref, cst_ref, a_ref, ap_ref, d_ref, hre_ref, him_ref, *rest,
              tl, npar, seq_len, nseg, emit_y, exact_in):
    if emit_y:
        y_ref, fre_ref, fim_ref, x_s, h_s = rest
    else:
        fre_ref, fim_ref, x_s, h_s = rest
    ns = SSM_STATES_PER_TILE
    c = pl.program_id(1)
    ngrp = npar // SUBLANES

    def step_rows(i, g):
        return pl.ds(c * tl + i + g * SUBLANES * seq_len, SUBLANES, stride=seq_len)

    @pl.when(c == 0)
    def _init():
        if nseg == 1:
            h_s[0] = hre_ref[...]
            h_s[1] = him_ref[...]
        else:
            pr, pi = ap_ref[0:1, :], ap_ref[1:2, :]
            for b in range(npar // nseg):
                sr = jnp.zeros((1, ns), F32)
                si = jnp.zeros((1, ns), F32)
                for j in range(nseg):
                    row = b * nseg + j
                    h_s[0, row:row + 1, :] = sr
                    h_s[1, row:row + 1, :] = si
                    er, ei = hre_ref[row:row + 1, :], him_ref[row:row + 1, :]
                    sr, si = pr * sr - pi * si + er, pr * si + pi * sr + ei

    u = jnp.concatenate([u_ref[step_rows(i, g), :] for i in range(tl) for g in range(ngrp)], axis=0)
    if exact_in:
        x_s[...] = jnp.dot(u, bb_ref[...], precision=lax.Precision.HIGHEST,
                           preferred_element_type=F32)
    else:
        x_s[...] = _dot(u.astype(BF16), bb_ref[...])

    ar = jnp.broadcast_to(a_ref[0:1, :], (SUBLANES, ns))
    ai = jnp.broadcast_to(a_ref[1:2, :], (SUBLANES, ns))

    def step(i, carry):
        out = []
        for g in range(ngrp):
            hr, hi = carry[2 * g], carry[2 * g + 1]
            rows = pl.ds(pl.multiple_of(i * npar + g * SUBLANES, SUBLANES), SUBLANES)
            nr = ar * hr - ai * hi + x_s[rows, 0:ns]
            ni = ar * hi + ai * hr + x_s[rows, ns:2 * ns]
            if emit_y:
                x_s[rows, 0:ns] = nr
                x_s[rows, ns:2 * ns] = ni
            out += [nr, ni]
        return tuple(out)

    init = []
    for g in range(ngrp):
        gs = slice(g * SUBLANES, (g + 1) * SUBLANES)
        init += [h_s[0, gs, :], h_s[1, gs, :]]
    fin = lax.fori_loop(0, tl, step, tuple(init), unroll=4)
    for g in range(ngrp):
        gs = slice(g * SUBLANES, (g + 1) * SUBLANES)
        h_s[0, gs, :] = fin[2 * g]
        h_s[1, gs, :] = fin[2 * g + 1]

    if emit_y:
        y = _gelu_tanh(_dot(x_s[...].astype(BF16), cst_ref[...]) + d_ref[...] * u)
        for i in range(tl):
            for g in range(ngrp):
                r0 = i * npar + g * SUBLANES
                y_ref[step_rows(i, g), :] = y[r0:r0 + SUBLANES, :]

    @pl.when(c == pl.num_programs(1) - 1)
    def _fin():
        fre_ref[...] = h_s[0]
        fim_ref[...] = h_s[1]


def _ssm_scan(proj, prm, hin_re, hin_im, *, seq_len, tl, nseg, emit_y, exact_in, name):
    rows = proj.shape[0]
    npar = rows // seq_len
    assert npar % SUBLANES == 0 and seq_len % tl == 0
    ns = SSM_STATES_PER_TILE
    nk = D_SSM // SSM_LANE_TILE
    col0 = (proj.shape[1] - D_SSM) // SSM_LANE_TILE
    bb = prm["bb_f32"] if exact_in else prm["bb_bf16"]
    in_specs = [
        pl.BlockSpec((rows, SSM_LANE_TILE), lambda k, c: (0, col0 + k)),
        pl.BlockSpec((None, SSM_LANE_TILE, 2 * ns), lambda k, c: (k, 0, 0)),
        pl.BlockSpec((None, 2 * ns, SSM_LANE_TILE), lambda k, c: (k, 0, 0)),
        pl.BlockSpec((None, 2, ns), lambda k, c: (k, 0, 0)),
        pl.BlockSpec((None, 2, ns), lambda k, c: (k, 0, 0)),
        pl.BlockSpec((1, SSM_LANE_TILE), lambda k, c: (0, k)),
        pl.BlockSpec((npar, ns), lambda k, c: (0, k)),
        pl.BlockSpec((npar, ns), lambda k, c: (0, k)),
    ]
    state_shape = jax.ShapeDtypeStruct((npar, nk * ns), F32)
    state_spec = pl.BlockSpec((npar, ns), lambda k, c: (0, k))
    out_shape = [state_shape, state_shape]
    out_specs = [state_spec, state_spec]
    if emit_y:
        out_shape = [jax.ShapeDtypeStruct((rows, D_SSM), F32)] + out_shape
        out_specs = [pl.BlockSpec((rows, SSM_LANE_TILE), lambda k, c: (0, k))] + out_specs
    vmem = (4 * rows * SSM_LANE_TILE * 4 + tl * npar * 2 * ns * 4) / 2**20 + 16
    return pl.pallas_call(
        functools.partial(_ssm_body, tl=tl, npar=npar, seq_len=seq_len, nseg=nseg, emit_y=emit_y,
                          exact_in=exact_in),
        out_shape=out_shape,
        grid=(nk, seq_len // tl),
        in_specs=in_specs,
        out_specs=out_specs,
        scratch_shapes=[pltpu.VMEM((tl * npar, 2 * ns), F32), pltpu.VMEM((2, npar, ns), F32)],
        compiler_params=_cparams(("parallel", "arbitrary"), vmem),
        name=name,
    )(proj, bb, prm["cst"], prm["a"], prm["apow"], prm["d"], hin_re, hin_im)


def _ssm_params(lam_re, lam_im, log_dt, b_re, b_im, c_re, c_im, d_skip, seg_len):
    g, p, c = N_SSM_GROUPS, SSM_STATE, SSM_GROUP_CH
    nk, gt = g // SSM_GROUPS_PER_TILE, SSM_GROUPS_PER_TILE
    dt = jnp.exp(log_dt.astype(F32))[:, None]
    lr, li = lam_re.astype(F32), lam_im.astype(F32)
    mag = jnp.exp(lr * dt)
    a_re, a_im = mag * jnp.cos(li * dt), mag * jnp.sin(li * dt)
    magp = jnp.exp(lr * dt * seg_len)
    p_re, p_im = magp * jnp.cos(li * dt * seg_len), magp * jnp.sin(li * dt * seg_len)
    den = lr * lr + li * li
    nr, ni = a_re - 1.0, a_im
    f_re, f_im = (nr * lr + ni * li) / den, (ni * lr - nr * li) / den
    br, bi = b_re.astype(F32), b_im.astype(F32)
    bb_re = f_re[..., None] * br - f_im[..., None] * bi
    bb_im = f_re[..., None] * bi + f_im[..., None] * br
    eye = jnp.eye(gt, dtype=F32)

    def pack_b(m):
        return jnp.einsum("kgpc,gh->kgchp", m.reshape(nk, gt, p, c), eye).reshape(nk, gt * c, gt * p)

    def pack_c(m):
        return jnp.einsum("kgcp,gh->kgphc", m.reshape(nk, gt, c, p), eye).reshape(nk, gt * p, gt * c)

    bb = jnp.concatenate([pack_b(bb_re), pack_b(bb_im)], axis=2)
    cst = jnp.concatenate([pack_c(c_re.astype(F32)), -pack_c(c_im.astype(F32))], axis=1)
    tile = lambda v: v.reshape(nk, 1, gt * p)
    return {
        "bb_f32": bb, "bb_bf16": bb.astype(BF16), "cst": cst.astype(BF16),
        "a": jnp.concatenate([tile(a_re), tile(a_im)], axis=1),
        "apow": jnp.concatenate([tile(p_re), tile(p_im)], axis=1),
        "d": d_skip.astype(F32).reshape(1, g * c),
    }


def _glu_body(y_ref, w_ref, o_ref):
    yg = y_ref[...]
    z = _dot(yg.astype(BF16), w_ref[...])
    o_ref[...] = yg * (1.0 / (1.0 + jnp.exp(-z)))


def _glu(yg, w, *, tm, name):
    m, n = yg.shape
    return pl.pallas_call(
        _glu_body,
        out_shape=jax.ShapeDtypeStruct((m, n), F32),
        grid=(m // tm,),
        in_specs=[pl.BlockSpec((tm, n), lambda i: (i, 0)), pl.BlockSpec((n, n), lambda i: (0, 0))],
        out_specs=pl.BlockSpec((tm, n), lambda i: (i, 0)),
        compiler_params=_cparams(("parallel",), 4 * tm * n * 4 / 2**20 + 12),
        name=name,
    )(yg, w)


def _mix_body(attn_ref, ssm_ref, ga_ref, gs_ref, w_ref, x_ref, g_ref, b_ref, o_ref):
    a = _rmsnorm(attn_ref[...], ga_ref[...]).astype(BF16)
    s = _rmsnorm(ssm_ref[...], gs_ref[...]).astype(BF16)
    mix = _dot(a, w_ref[0:D_ATT, :]) + _dot(s, w_ref[D_ATT:D_ATT + D_SSM, :])
    o_ref[...] = _layernorm(DEEPNORM_ALPHA * x_ref[...] + mix, g_ref[...], b_ref[...])


def _mix(attn, ssm, ga, gs, w, x, g, b, *, tm, name):
    m = x.shape[0]
    row = lambda n: pl.BlockSpec((tm, n), lambda i: (i, 0))
    const = lambda a: pl.BlockSpec(a.shape, lambda i: (0, 0))
    return pl.pallas_call(
        _mix_body,
        out_shape=jax.ShapeDtypeStruct((m, D_MODEL), F32),
        grid=(m // tm,),
        in_specs=[row(D_ATT), row(D_SSM), const(ga), const(gs), const(w), row(D_MODEL),
                  const(g), const(b)],
        out_specs=row(D_MODEL),
        compiler_params=_cparams(("parallel",), 6 * tm * D_MODEL * 4 / 2**20 + 24),
        name=name,
    )(attn, ssm, ga, gs, w, x, g, b)


def _store_gatherable(o_ref, y):
    rows = y.shape[0]
    for c in range(ROW_CHUNKS):
        o_ref[pl.ds(c, rows, stride=ROW_PITCH), :] = y[:, c * LANES:(c + 1) * LANES]
    for c in range(ROW_CHUNKS, ROW_PITCH):
        o_ref[pl.ds(c, rows, stride=ROW_PITCH), :] = jnp.zeros((rows, LANES), F32)


def _load_gathered(buf, rows):
    return jnp.concatenate([buf[pl.ds(c, rows, stride=ROW_PITCH), :] for c in range(ROW_CHUNKS)],
                           axis=1)


def _start_row_gather(src_hbm, idx, buf, r, sem):
    pltpu.make_async_copy(src_hbm.at[pl.ds(idx * ROW_PITCH, ROW_CHUNKS), :],
                          buf.at[pl.ds(r * ROW_PITCH, ROW_CHUNKS), :], sem).start()


def _wait_row_gathers(buf, other, rows, sem):
    span = pl.ds(0, rows * ROW_CHUNKS)
    pltpu.make_async_copy(other.at[span, :], buf.at[span, :], sem).wait()


def _mm_ln_body(a1_ref, a2_ref, w_ref, x1_ref, x2_ref, g_ref, b_ref, o_ref, rows_ref, *, tiles1):
    first = pl.program_id(0) < tiles1
    a = jnp.where(first, a1_ref[...], a2_ref[...])
    x = jnp.where(first, x1_ref[...], x2_ref[...])
    y = _dot(a.astype(BF16), w_ref[...])
    out = _layernorm(DEEPNORM_ALPHA * x + y, g_ref[...], b_ref[...])
    o_ref[...] = out
    _store_gatherable(rows_ref, out)


def _mm_ln(a1, a2, w, x1, x2, g, b, *, name):
    tm = a2.shape[0]
    assert a1.shape[0] % tm == 0
    tiles1 = a1.shape[0] // tm
    m = a1.shape[0] + tm
    row1 = lambda n: pl.BlockSpec((tm, n), lambda i: (jnp.minimum(i, tiles1 - 1), 0))
    row2 = lambda n: pl.BlockSpec((tm, n), lambda i: (0, 0))
    const = lambda v: pl.BlockSpec(v.shape, lambda i: (0, 0))
    return pl.pallas_call(
        functools.partial(_mm_ln_body, tiles1=tiles1),
        out_shape=[jax.ShapeDtypeStruct((m, D_MODEL), F32),
                   jax.ShapeDtypeStruct((m * ROW_PITCH, LANES), F32)],
        grid=(tiles1 + 1,),
        in_specs=[row1(a1.shape[1]), row2(a2.shape[1]), const(w), row1(D_MODEL), row2(D_MODEL),
                  const(g), const(b)],
        out_specs=[pl.BlockSpec((tm, D_MODEL), lambda i: (i, 0)),
                   pl.BlockSpec((tm * ROW_PITCH, LANES), lambda i: (i, 0))],
        compiler_params=_cparams(("parallel",), 12 * tm * D_MODEL * 4 / 2**20 + 24),
        name=name,
    )(a1, a2, w, x1, x2, g, b)


def _memattn_body(q_ref, k_ref, v_ref, o_ref):
    scale = MEM_HD ** -0.5
    for h in range(MEM_HEADS):
        sl = slice(h * MEM_HD, (h + 1) * MEM_HD)
        s = _dot_nt(q_ref[:, sl].astype(BF16), k_ref[:, sl].astype(BF16)) * scale
        m = jnp.max(s, axis=-1, keepdims=True)
        p = jnp.exp(s - m)
        l = jnp.sum(p, axis=-1, keepdims=True)
        o_ref[:, sl] = _dot(p.astype(BF16), v_ref[:, sl].astype(BF16)) / l


def _memattn(q, mem_k, mem_v, *, row0, n_seq, seq, tq, name):
    assert seq % tq == 0 and row0 % tq == 0
    nq = seq // tq
    rb = row0 // tq
    mem_spec = pl.BlockSpec((None, N_MEM, D_MODEL), lambda b, i: (b, 0, 0))
    return pl.pallas_call(
        _memattn_body,
        out_shape=jax.ShapeDtypeStruct((n_seq * seq, D_MODEL), F32),
        grid=(n_seq, nq),
        in_specs=[pl.BlockSpec((tq, D_MODEL), lambda b, i: (rb + b * nq + i, 0)),
                  mem_spec, mem_spec],
        out_specs=pl.BlockSpec((tq, D_MODEL), lambda b, i: (b * nq + i, 0)),
        compiler_params=_cparams(("parallel", "parallel"),
                                 4 * (tq + N_MEM) * D_MODEL * 4 / 2**20 + 8),
        name=name,
    )(q, mem_k, mem_v)


def _memattn_heads_body(q_ref, k_ref, v_ref, c_ref, o_ref):
    scale = MEM_HD ** -0.5
    tq = q_ref.shape[0]
    q = jnp.concatenate([q_ref[:, h * MEM_HD:(h + 1) * MEM_HD] for h in range(MEM_HEADS)], axis=0)
    k = k_ref[...].reshape(N_MEM * MEM_HEADS, MEM_HD).astype(BF16)
    v = v_ref[...].reshape(N_MEM * MEM_HEADS, MEM_HD).astype(BF16)
    s = jnp.where(c_ref[...] > 0, _dot_nt(q.astype(BF16), k) * scale, NEG_INF)
    m = jnp.max(s, axis=-1, keepdims=True)
    p = jnp.exp(s - m)
    l = jnp.sum(p, axis=-1, keepdims=True)
    o = _dot(p.astype(BF16), v) / l
    for h in range(MEM_HEADS):
        o_ref[:, h * MEM_HD:(h + 1) * MEM_HD] = o[h * tq:(h + 1) * tq, :]


def _memattn_heads(q, mem_k, mem_v, *, row0, n_seq, seq, name):
    assert row0 % seq == 0 and seq % SUBLANES == 0
    rb = row0 // seq
    same_head = np.kron(np.eye(MEM_HEADS, dtype=np.float32), np.ones((seq, 1), np.float32))
    same_head = np.tile(same_head, (1, N_MEM))
    mem_spec = pl.BlockSpec((None, N_MEM, MEM_HEADS, MEM_HD), lambda b: (b, 0, 0, 0))
    return pl.pallas_call(
        _memattn_heads_body,
        out_shape=jax.ShapeDtypeStruct((n_seq * seq, D_MODEL), F32),
        grid=(n_seq,),
        in_specs=[pl.BlockSpec((seq, D_MODEL), lambda b: (rb + b, 0)), mem_spec, mem_spec,
                  pl.BlockSpec(same_head.shape, lambda b: (0, 0))],
        out_specs=pl.BlockSpec((seq, D_MODEL), lambda b: (b, 0)),
        compiler_params=_cparams(("parallel",), 8 * N_MEM * D_MODEL * 4 / 2**20 + 8),
        name=name,
    )(q, mem_k, mem_v, jnp.asarray(same_head))


def _router_body(x_ref, w_ref, b_ref, sel_ref, wts_ref, cnt_ref, run_s, *, tm):
    i = pl.program_id(0)

    @pl.when(i == 0)
    def _():
        run_s[...] = jnp.zeros_like(run_s)

    ng, epg = N_EXPERT_GROUPS, EXPERTS_PER_GROUP
    x = x_ref[...]
    x_hi = x.astype(BF16)
    x_lo = (x - x_hi.astype(F32)).astype(BF16)
    parts = _dot(x_hi, w_ref[...]) + _dot(x_lo, w_ref[...])
    logits = parts + pltpu.roll(parts, shift=ROUTER_LANES // 2, axis=1) + b_ref[...]
    lane = lax.broadcasted_iota(I32, (tm, ROUTER_LANES), 1)
    big = ROUTER_LANES

    def first_argmax(vals):
        mx = jnp.max(vals, axis=-1, keepdims=True)
        idx = jnp.min(jnp.where(vals == mx, lane, big), axis=-1, keepdims=True)
        return mx, idx

    gl = jnp.where(lane < ng, logits, NEG_INF)
    gmax, gsel = first_argmax(gl)
    g_w = 1.0 / jnp.sum(jnp.exp(gl - gmax), axis=-1, keepdims=True)
    lo = ng + gsel * epg
    el = jnp.where(jnp.logical_and(lane >= lo, lane < lo + epg), logits, NEG_INF)
    v1, i1 = first_argmax(el)
    v2, i2 = first_argmax(jnp.where(lane == i1, NEG_INF, el))
    e21 = jnp.exp(v2 - v1)
    w1 = g_w / (1.0 + e21)
    w2 = g_w * e21 / (1.0 + e21)

    onehot = jnp.logical_or(lane == i1, lane == i2)
    r = lax.broadcasted_iota(I32, (tm, tm), 0)
    cc = lax.broadcasted_iota(I32, (tm, tm), 1)
    tri = (cc < r).astype(BF16)
    before = _dot(tri, onehot.astype(BF16)) + run_s[...]
    rank1 = jnp.sum(jnp.where(lane == i1, before, 0.0), axis=-1, keepdims=True).astype(I32)
    rank2 = jnp.sum(jnp.where(lane == i2, before, 0.0), axis=-1, keepdims=True).astype(I32)
    run_s[...] = run_s[...] + jnp.sum(onehot.astype(F32), axis=0, keepdims=True)

    sel = jnp.where(lane == 0, i1 - ng, jnp.where(lane == 1, i2 - ng,
                    jnp.where(lane == 2, rank1, jnp.where(lane == 3, rank2, 0))))
    sel_ref[...] = sel
    wts_ref[...] = jnp.where(lane == 0, w1, jnp.where(lane == 1, w2, 0.0))
    cnt_ref[...] = run_s[...].astype(I32)


def _router(x, w, b, *, tm):
    m = x.shape[0]
    row = pl.BlockSpec((tm, ROUTER_LANES), lambda i: (i, 0))
    return pl.pallas_call(
        functools.partial(_router_body, tm=tm),
        out_shape=[jax.ShapeDtypeStruct((m, ROUTER_LANES), I32),
                   jax.ShapeDtypeStruct((m, ROUTER_LANES), F32),
                   jax.ShapeDtypeStruct((1, ROUTER_LANES), I32)],
        grid=(m // tm,),
        in_specs=[pl.BlockSpec((tm, D_MODEL), lambda i: (i, 0)),
                  pl.BlockSpec((D_MODEL, ROUTER_LANES), lambda i: (0, 0)),
                  pl.BlockSpec((1, ROUTER_LANES), lambda i: (0, 0))],
        out_specs=[row, row, pl.BlockSpec((1, ROUTER_LANES), lambda i: (0, 0))],
        scratch_shapes=[pltpu.VMEM((1, ROUTER_LANES), F32)],
        compiler_params=_cparams(("arbitrary",), 16),
        name="moe_router",
    )(x, w, b)


def _dispatch_body(eid_ref, rank_ref, off_ref, pad0_ref, npad_ref, nact_ref, x_hbm, xs_hbm, xbuf,
                   zero_s, in_sem, out_sem, pad_sem, *, td, n_tiles):
    i = pl.program_id(0)
    n = pl.num_programs(0)
    tile_rows = MOE_TILE * ROW_PITCH
    in_rows = td * ROW_PITCH

    def row_copy(src, dst_row, s):
        return pltpu.make_async_copy(src, xs_hbm.at[pl.ds(dst_row * ROW_PITCH, ROW_PITCH), :], s)

    def tile_load(t):
        s = t % DISPATCH_SLOTS
        return pltpu.make_async_copy(x_hbm.at[pl.ds(t * in_rows, in_rows), :], xbuf.at[s],
                                     in_sem.at[s])

    def wait_scatter(par):
        for _ in range(2):
            pltpu.make_async_copy(xbuf.at[0], xs_hbm.at[pl.ds(0, in_rows), :],
                                  out_sem.at[par]).wait()

    @pl.when(i == 0)
    def _():
        for t in range(DISPATCH_SLOTS - 1):
            @pl.when(t < n)
            def _(t=t):
                tile_load(t).start()
        zero_s[...] = jnp.zeros_like(zero_s)

        def pad_copies(e):
            out = []
            for bit in reversed(range(MOE_TILE.bit_length() - 1)):
                rows = (1 << bit) * ROW_PITCH
                first = (pad0_ref[e] + (npad_ref[e] >> (bit + 1) << (bit + 1))) * ROW_PITCH
                out.append((jnp.bitwise_and(npad_ref[e] >> bit, 1) == 1, pltpu.make_async_copy(
                    zero_s.at[pl.ds(0, rows), :], xs_hbm.at[pl.ds(first, rows), :], pad_sem.at[0])))
            return out

        for e in range(N_EXPERTS):
            for on, cp in pad_copies(e):
                pl.when(on)(cp.start)
        for e in range(N_EXPERTS):
            for on, cp in pad_copies(e):
                pl.when(on)(cp.wait)

        def zero_tile(t, carry):
            parts = [pltpu.make_async_copy(
                zero_s.at[pl.ds(0, MOE_TILE), :],
                xs_hbm.at[pl.ds(t * tile_rows + j * MOE_TILE, MOE_TILE), :],
                pad_sem.at[0]) for j in range(ROW_PITCH)]
            for cp in parts:
                cp.start()
            for cp in parts:
                cp.wait()
            return carry

        lax.fori_loop(nact_ref[0], n_tiles, zero_tile, 0)

    slot = i % DISPATCH_SLOTS
    par = i % 2
    tile_load(i).wait()
    base = i * td * 2
    for r in range(td):
        for k in range(2):
            j = base + 2 * r + k
            row_copy(xbuf.at[slot, pl.ds(r * ROW_PITCH, ROW_PITCH), :],
                     off_ref[eid_ref[j]] + rank_ref[j], out_sem.at[par]).start()

    @pl.when(i > 0)
    def _():
        wait_scatter(1 - par)

    @pl.when(i + DISPATCH_SLOTS - 1 < n)
    def _():
        tile_load(i + DISPATCH_SLOTS - 1).start()

    @pl.when(i == n - 1)
    def _():
        wait_scatter(par)


def _moe_dispatch(x_rows, eid, rank, row_off, pad_start, pad_count, nact, *, td, n_tiles):
    n = x_rows.shape[0] // ROW_PITCH
    grid_spec = pltpu.PrefetchScalarGridSpec(
        num_scalar_prefetch=6,
        grid=(n // td,),
        in_specs=[pl.BlockSpec(memory_space=pl.ANY)],
        out_specs=pl.BlockSpec(memory_space=pl.ANY),
        scratch_shapes=[pltpu.VMEM((DISPATCH_SLOTS, td * ROW_PITCH, LANES), F32),
                        pltpu.VMEM((MOE_TILE // 2 * ROW_PITCH, LANES), F32),
                        pltpu.SemaphoreType.DMA((DISPATCH_SLOTS,)),
                        pltpu.SemaphoreType.DMA((2,)),
                        pltpu.SemaphoreType.DMA((1,))],
    )
    return pl.pallas_call(
        functools.partial(_dispatch_body, td=td, n_tiles=n_tiles),
        out_shape=jax.ShapeDtypeStruct((n_tiles * MOE_TILE * ROW_PITCH, LANES), F32),
        grid_spec=grid_spec,
        compiler_params=_cparams(("arbitrary",), 16),
        name="moe_dispatch",
    )(eid, rank, row_off, pad_start, pad_count, nact, x_rows)


def _moe_body(te_ref, ord_ref, nxt_ref, nact_ref, x_ref, wg_hbm, wu_hbm, wd_hbm, o_ref,
              wg_f, wu_f, wd_f, wsem, wg_s, wu_s, wd_s):
    i = pl.program_id(0)
    nact = nact_ref[0]
    tm = MOE_TILE

    def weight_copies(expert, ws):
        return [pltpu.make_async_copy(hbm.at[expert], stage.at[ws], wsem.at[ws])
                for hbm, stage in ((wg_hbm, wg_f), (wu_hbm, wu_f), (wd_hbm, wd_f))]

    def tile_step():
        prev = te_ref[jnp.maximum(i - 1, 0)]

        @pl.when(jnp.logical_or(i == 0, te_ref[i] != prev))
        def _():
            ws = ord_ref[i] % 2
            for cp in weight_copies(te_ref[i], ws):
                cp.wait()
            wg_s[...] = wg_f[ws].astype(BF16)
            wu_s[...] = wu_f[ws].astype(BF16)
            wd_s[...] = wd_f[ws].astype(BF16)

            @pl.when(nxt_ref[i] >= 0)
            def _():
                for cp in weight_copies(nxt_ref[i], 1 - ws):
                    cp.start(priority=1)

        x = _load_gathered(x_ref, tm).astype(BF16)
        hg = _dot(x, wg_s[...])
        hu = _dot(x, wu_s[...])
        h = hg * (1.0 / (1.0 + jnp.exp(-hg))) * hu
        _store_gatherable(o_ref, _dot(h.astype(BF16), wd_s[...]))

    @pl.when(i == 0)
    def _():
        for cp in weight_copies(te_ref[0], 0):
            cp.start(priority=1)

    @pl.when(i < nact)
    def _():
        tile_step()

    @pl.when(i >= nact)
    def _():
        o_ref[...] = jnp.zeros_like(o_ref)


def _moe_experts(x_sorted, w_gate, w_up, w_down, tile_expert, tile_ord, tile_next, nact, *,
                 n_tiles):
    tm = MOE_TILE
    in_map = lambda i, te, od, nx, n: (jnp.minimum(i, n[0] - 1), 0)
    any_spec = pl.BlockSpec(memory_space=pl.ANY)
    grid_spec = pltpu.PrefetchScalarGridSpec(
        num_scalar_prefetch=4,
        grid=(n_tiles,),
        in_specs=[pl.BlockSpec((tm * ROW_PITCH, LANES), in_map), any_spec, any_spec, any_spec],
        out_specs=pl.BlockSpec((tm * ROW_PITCH, LANES), lambda i, te, od, nx, n: (i, 0)),
        scratch_shapes=[pltpu.VMEM((2, D_MODEL, D_EXPERT), F32),
                        pltpu.VMEM((2, D_MODEL, D_EXPERT), F32),
                        pltpu.VMEM((2, D_EXPERT, D_MODEL), F32),
                        pltpu.SemaphoreType.DMA((2,)),
                        pltpu.VMEM((D_MODEL, D_EXPERT), BF16),
                        pltpu.VMEM((D_MODEL, D_EXPERT), BF16),
                        pltpu.VMEM((D_EXPERT, D_MODEL), BF16)],
    )
    return pl.pallas_call(
        _moe_body,
        out_shape=jax.ShapeDtypeStruct((n_tiles * tm * ROW_PITCH, LANES), F32),
        grid_spec=grid_spec,
        compiler_params=_cparams(("arbitrary",), 48),
        name="moe_experts",
    )(tile_expert, tile_ord, tile_next, nact, x_sorted, w_gate, w_up, w_down)


def _combine_body(eid_ref, rank_ref, off_ref, ys_hbm, wts_ref, x_ref, g_ref, b_ref, o1_ref, o2_ref,
                  buf, sem, *, tc, tiles1):
    i = pl.program_id(0)
    n = pl.num_programs(0)
    slot = i % GATHER_SLOTS
    ahead = GATHER_SLOTS - 1

    def issue_gather(tile, slot_):
        base = tile * tc * 2
        for r in range(tc):
            for k in range(2):
                j = base + 2 * r + k
                _start_row_gather(ys_hbm, off_ref[eid_ref[j]] + rank_ref[j], buf.at[slot_, k], r,
                                  sem.at[slot_])

    @pl.when(i == 0)
    def _():
        for t in range(ahead):
            @pl.when(t < n)
            def _(t=t):
                issue_gather(t, t)

    for k in range(2):
        _wait_row_gathers(buf.at[slot, k], buf.at[(i + 1) % GATHER_SLOTS, k], tc, sem.at[slot])

    @pl.when(i + ahead < n)
    def _():
        issue_gather(i + ahead, (i + ahead) % GATHER_SLOTS)

    w = wts_ref[...]
    moe = (w[:, 0:1] * _load_gathered(buf.at[slot, 0], tc)
           + w[:, 1:2] * _load_gathered(buf.at[slot, 1], tc))
    out = _layernorm(DEEPNORM_ALPHA * x_ref[...] + moe, g_ref[...], b_ref[...])

    @pl.when(i < tiles1)
    def _():
        o1_ref[...] = out

    @pl.when(i >= tiles1)
    def _():
        o2_ref[...] = out


def _moe_combine(ys, eid, rank, row_off, wts, x, g, b, *, tc, n_first):
    m = x.shape[0]
    assert n_first % tc == 0 and (m - n_first) % tc == 0
    tiles1 = n_first // tc
    grid_spec = pltpu.PrefetchScalarGridSpec(
        num_scalar_prefetch=3,
        grid=(m // tc,),
        in_specs=[pl.BlockSpec(memory_space=pl.ANY),
                  pl.BlockSpec((tc, ROUTER_LANES), lambda i, *_: (i, 0)),
                  pl.BlockSpec((tc, D_MODEL), lambda i, *_: (i, 0)),
                  pl.BlockSpec((1, D_MODEL), lambda i, *_: (0, 0)),
                  pl.BlockSpec((1, D_MODEL), lambda i, *_: (0, 0))],
        out_specs=[pl.BlockSpec((tc, D_MODEL), lambda i, *_: (jnp.minimum(i, tiles1 - 1), 0)),
                   pl.BlockSpec((tc, D_MODEL), lambda i, *_: (jnp.maximum(i - tiles1, 0), 0))],
        scratch_shapes=[pltpu.VMEM((GATHER_SLOTS, 2, tc * ROW_PITCH, LANES), F32),
                        pltpu.SemaphoreType.DMA((GATHER_SLOTS,))],
    )
    return pl.pallas_call(
        functools.partial(_combine_body, tc=tc, tiles1=tiles1),
        out_shape=[jax.ShapeDtypeStruct((n_first, D_MODEL), F32),
                   jax.ShapeDtypeStruct((m - n_first, D_MODEL), F32)],
        grid_spec=grid_spec,
        compiler_params=_cparams(("arbitrary",), 16 * tc * D_MODEL * 4 / 2**20 + 8),
        name="moe_combine_ln3",
    )(eid, rank, row_off, ys, wts, x, g, b)


def _moe(x, x_rows, w_r1, b_r1, w_r2, b_r2, w_gate, w_up, w_down, g, b, *, n_first, tm_router, tc):
    n = x.shape[0]
    ng, ne = N_EXPERT_GROUPS, N_EXPERTS
    half = ROUTER_LANES // 2
    assert ng + ne <= half
    w_r = jnp.concatenate([w_r1, w_r2.reshape(D_MODEL, ne),
                           jnp.zeros((D_MODEL, half - ng - ne), F32)], axis=1)
    w_hi = w_r.astype(BF16)
    w_lo = (w_r - w_hi.astype(F32)).astype(BF16)
    w_r = jnp.concatenate([w_hi, w_lo], axis=1)
    b_r = jnp.concatenate([b_r1, b_r2.reshape(ne), jnp.zeros((half - ng - ne,), F32)])
    b_r = jnp.concatenate([b_r, b_r]).reshape(1, ROUTER_LANES)
    sel, wts, cnt = _router(x, w_r, b_r, tm=tm_router)

    tm = MOE_TILE
    n_tiles = (2 * n) // tm + ne
    counts = cnt[0, ng:ng + ne]
    tiles_per = (counts + tm - 1) // tm
    tile_end = jnp.cumsum(tiles_per)
    row_off = (tile_end - tiles_per) * tm
    nact = tile_end[-1]
    a_eid, a_rank = sel[:, 0:2].reshape(-1), sel[:, 2:4].reshape(-1)
    row_off = row_off.astype(I32)
    tile_ids = jnp.minimum(jnp.arange(n_tiles, dtype=I32), nact - 1)
    tile_expert = jnp.sum((tile_end[None, :] <= tile_ids[:, None]).astype(I32), axis=1)
    used = tiles_per > 0
    eid = jnp.arange(ne, dtype=I32)
    ordinal = jnp.cumsum(used.astype(I32)) - 1
    later = jnp.where(jnp.logical_and(used[None, :], eid[None, :] > eid[:, None]), eid[None, :], ne)
    nxt = jnp.min(later, axis=1)
    nxt = jnp.where(nxt == ne, -1, nxt)

    nact = nact.reshape(1).astype(I32)
    x_sorted = _moe_dispatch(x_rows, a_eid, a_rank, row_off, (row_off + counts).astype(I32),
                             (tiles_per * tm - counts).astype(I32), nact, td=tc, n_tiles=n_tiles)
    ys = _moe_experts(x_sorted, w_gate, w_up, w_down, tile_expert, ordinal[tile_expert],
                      nxt[tile_expert], nact, n_tiles=n_tiles)
    return _moe_combine(ys, a_eid, a_rank, row_off, wts, x, g, b, tc=tc, n_first=n_first)


def _row_tile(m, cap):
    best = SUBLANES
    for t in range(SUBLANES, cap + 1, SUBLANES):
        if m % t == 0:
            best = t
    return best


def kernel(x_prompt, x_sample, cache_win_k, cache_win_v, state_ssm_re, state_ssm_im, cache_mem_k, cache_mem_v, mem_prompt, w_in, ssm_lam_re, ssm_lam_im, ssm_log_dt, ssm_b_re, ssm_b_im, ssm_c_re, ssm_c_im, ssm_d, w_glu, g_attn, g_ssm, w_out, ln1_g, ln1_b, w_mq, w_mk, w_mv, w_mo, ln2_g, ln2_b, w_r1, b_r1, w_r2, b_r2, w_gate, w_up, w_down, ln3_g, ln3_b):
    nb, seq, d = x_prompt.shape
    ns, dseq, _ = x_sample.shape
    n_p, n_s = nb * seq, ns * dseq
    n = n_p + n_s
    l = 0
    row2 = lambda v: v[l].reshape(1, -1)

    x_p, x_s = x_prompt.reshape(n_p, d), x_sample.reshape(n_s, d)
    tm_p = _row_tile(n_p, 1024)
    tm_ln = _row_tile(n_p, 512)
    assert n_p % n_s == 0 and n_s % SUBLANES == 0

    proj_p = _matmul(x_p, w_in[l], tm=tm_p, tn=1024, name="proj_in_prompt")
    proj_s = _matmul(x_s, w_in[l], tm=n_s, tn=1024, name="proj_in_sample")

    attn_p = _attn_prompt(proj_p, n_batch=nb, seq=seq)
    attn_s = _attn_sample(proj_s, cache_win_k[l], cache_win_v[l], row0=0, n_seq=ns, n_new=dseq)

    seg_len = seq // SSM_SEGMENTS
    prm = _ssm_params(ssm_lam_re[l], ssm_lam_im[l], ssm_log_dt[l], ssm_b_re[l], ssm_b_im[l],
                      ssm_c_re[l], ssm_c_im[l], ssm_d[l], seg_len)
    zeros = jnp.zeros((nb * SSM_SEGMENTS, N_SSM_GROUPS * SSM_STATE), F32)
    tl = _row_tile(seg_len, 32)
    end_re, end_im = _ssm_scan(proj_p, prm, zeros, zeros, seq_len=seg_len, tl=tl, nseg=1,
                               emit_y=False, exact_in=False, name="ssm_state_prompt")
    yg_p, fin_re, fin_im = _ssm_scan(proj_p, prm, end_re, end_im, seq_len=seg_len, tl=tl,
                                     nseg=SSM_SEGMENTS, emit_y=True, exact_in=False,
                                     name="ssm_scan_prompt")
    last = SSM_SEGMENTS - 1
    ssm_re_p = fin_re.reshape(nb, SSM_SEGMENTS, N_SSM_GROUPS, SSM_STATE)[:, last]
    ssm_im_p = fin_im.reshape(nb, SSM_SEGMENTS, N_SSM_GROUPS, SSM_STATE)[:, last]

    h0_re = state_ssm_re[l].reshape(ns, -1)
    h0_im = state_ssm_im[l].reshape(ns, -1)
    yg_s, ssm_re_s, ssm_im_s = _ssm_scan(proj_s, prm, h0_re, h0_im, seq_len=dseq, tl=dseq, nseg=1,
                                         emit_y=True, exact_in=True, name="ssm_scan_sample")
    w_glu_b = w_glu[l].astype(BF16)
    ssm_out_p = _glu(yg_p, w_glu_b, tm=tm_p, name="ssm_glu_prompt")
    ssm_out_s = _glu(yg_s, w_glu_b, tm=n_s, name="ssm_glu_sample")

    mix_args = (row2(g_attn), row2(g_ssm), w_out[l].astype(BF16))
    ln1 = (row2(ln1_g), row2(ln1_b))
    x1_p = _mix(attn_p, ssm_out_p, *mix_args, x_p, *ln1, tm=tm_ln, name="mix_out_ln1_prompt")
    x1_s = _mix(attn_s, ssm_out_s, *mix_args, x_s, *ln1, tm=n_s, name="mix_out_ln1_sample")

    mem_rows = mem_prompt.reshape(nb * N_MEM, d)
    mem_k = _matmul(mem_rows, w_mk[l], tm=nb * N_MEM, tn=1024, name="mem_k")
    mem_v = _matmul(mem_rows, w_mv[l], tm=nb * N_MEM, tn=1024, name="mem_v")
    q_p = _matmul(x1_p, w_mq[l], tm=tm_p, tn=1024, name="mem_q_prompt")
    q_s = _matmul(x1_s, w_mq[l], tm=n_s, tn=1024, name="mem_q_sample")
    o_p = _memattn(q_p, mem_k.reshape(nb, N_MEM, d), mem_v.reshape(nb, N_MEM, d),
                   row0=0, n_seq=nb, seq=seq, tq=_row_tile(seq, 512), name="memattn_prompt")
    o_s = _memattn_heads(q_s, cache_mem_k[l], cache_mem_v[l], row0=0, n_seq=ns, seq=dseq,
                         name="memattn_sample")
    x2, x2_rows = _mm_ln(o_p, o_s, w_mo[l].astype(BF16), x1_p, x1_s, row2(ln2_g), row2(ln2_b),
                         name="mem_out_ln2")

    y_p, y_s = _moe(x2, x2_rows, w_r1[l], b_r1[l], w_r2[l], b_r2[l], w_gate[l], w_up[l],
                    w_down[l], row2(ln3_g), row2(ln3_b), n_first=n_p,
                    tm_router=_row_tile(n_s, 256), tc=_row_tile(n_s, 128))

    y_p = y_p.reshape(nb, seq, d)
    y_s = y_s.reshape(ns, dseq, d)
    wp = min(max(w for w, _ in DILATIONS), seq)
    k_p, v_p = _kv_window(proj_p, n_batch=nb, seq=seq, window=wp, tr=_row_tile(wp, 512))
    k_s = proj_s[:, D_ATT:2 * D_ATT].reshape(ns, dseq, ATT_HEADS, ATT_HD)
    v_s = proj_s[:, 2 * D_ATT:3 * D_ATT].reshape(ns, dseq, ATT_HEADS, ATT_HD)
    state = lambda v, b_: v.reshape(1, b_, N_SSM_GROUPS, SSM_STATE)
    return (y_p, y_s, k_p[None], v_p[None], k_s[None], v_s[None],
            state(ssm_re_p, nb), state(ssm_im_p, nb), state(ssm_re_s, ns), state(ssm_im_s, ns),
            mem_k.reshape(1, nb, N_MEM, MEM_HEADS, MEM_HD),
            mem_v.reshape(1, nb, N_MEM, MEM_HEADS, MEM_HD))
```

```python
import functools
import math

import numpy as np
import jax
import jax.numpy as jnp
from jax import lax
from jax.experimental import pallas as pl
from jax.experimental.pallas import tpu as pltpu

F32 = jnp.float32
BF16 = jnp.bfloat16
I32 = jnp.int32

D_MODEL = 2048
PAST_LEN = 8192
D_ATT = D_MODEL // 2
ATT_HEADS = 8
ATT_HD = D_ATT // ATT_HEADS
DILATIONS = ((128, 1), (512, 4), (2048, 16))
D_SSM = D_MODEL - D_ATT
SSM_GROUP_CH = 16
N_SSM_GROUPS = D_SSM // SSM_GROUP_CH
SSM_STATE = 64
N_MEM = 256
MEM_HEADS = 4
MEM_HD = D_MODEL // MEM_HEADS
N_EXPERT_GROUPS = 4
EXPERTS_PER_GROUP = 8
N_EXPERTS = N_EXPERT_GROUPS * EXPERTS_PER_GROUP
D_EXPERT = D_MODEL // 4
DEPTH = 1
DEEPNORM_ALPHA = (2.0 * DEPTH) ** 0.25
LN_EPS = 1e-5
RMS_EPS = 1e-6

LANES = 128
SUBLANES = 8
ROW_CHUNKS = D_MODEL // LANES
ROW_PITCH = ROW_CHUNKS + 1
Q_BLOCK = 128
ATTN_GROUP = 8
SSM_LANE_TILE = 128
SSM_GROUPS_PER_TILE = SSM_LANE_TILE // SSM_GROUP_CH
SSM_STATES_PER_TILE = SSM_GROUPS_PER_TILE * SSM_STATE
SSM_SEGMENTS = 8
MOE_TILE = 256
GATHER_SLOTS = 3
DISPATCH_SLOTS = 3
ROUTER_LANES = 128
NEG_INF = float("-inf")


def _cparams(semantics, vmem_mib):
    return pltpu.CompilerParams(dimension_semantics=semantics,
                                vmem_limit_bytes=int(vmem_mib) << 20)


def _layernorm(y, g, b):
    mu = jnp.mean(y, axis=-1, keepdims=True)
    yc = y - mu
    var = jnp.mean(yc * yc, axis=-1, keepdims=True)
    return yc * lax.rsqrt(var + LN_EPS) * g + b


def _rmsnorm(v, g):
    return v * lax.rsqrt(jnp.mean(v * v, axis=-1, keepdims=True) + RMS_EPS) * g


def _dot(a, b):
    return jnp.dot(a, b, preferred_element_type=F32)


def _dot_nt(a, b):
    return lax.dot_general(a, b, (((1,), (1,)), ((), ())), preferred_element_type=F32)


def _mm_body(x_ref, w_ref, o_ref, wb_s):
    @pl.when(pl.program_id(1) == 0)
    def _():
        wb_s[...] = w_ref[...].astype(BF16)

    o_ref[...] = _dot(x_ref[...].astype(BF16), wb_s[...]).astype(o_ref.dtype)


def _matmul(x, w, *, tm, tn, name):
    m, k = x.shape
    n = w.shape[1]
    vmem = (2 * (tm * k * 4 + k * tn * 4 + tm * tn * 4) + k * tn * 2 + tm * k * 2) / 2**20 + 8
    return pl.pallas_call(
        _mm_body,
        out_shape=jax.ShapeDtypeStruct((m, n), F32),
        grid=(n // tn, m // tm),
        in_specs=[pl.BlockSpec((tm, k), lambda j, i: (i, 0)),
                  pl.BlockSpec((k, tn), lambda j, i: (0, j))],
        out_specs=pl.BlockSpec((tm, tn), lambda j, i: (i, j)),
        scratch_shapes=[pltpu.VMEM((k, tn), BF16)],
        compiler_params=_cparams(("parallel", "arbitrary"), vmem),
        name=name,
    )(x, w)


def _kv_window_body(k_ref, v_ref, ko_ref, vo_ref):
    for h in range(ATT_HEADS):
        sl = slice(h * ATT_HD, (h + 1) * ATT_HD)
        ko_ref[:, h, :] = k_ref[:, sl]
        vo_ref[:, h, :] = v_ref[:, sl]


def _kv_window(proj, *, n_batch, seq, window, tr):
    assert window % tr == 0 and seq % tr == 0
    per, first = seq // tr, (seq - window) // tr
    col = lambda c: pl.BlockSpec((tr, D_ATT), lambda b, t: (b * per + first + t, c))
    out_spec = pl.BlockSpec((None, tr, ATT_HEADS, ATT_HD), lambda b, t: (b, t, 0, 0))
    out_shape = jax.ShapeDtypeStruct((n_batch, window, ATT_HEADS, ATT_HD), F32)
    return pl.pallas_call(
        _kv_window_body,
        out_shape=[out_shape, out_shape],
        grid=(n_batch, window // tr),
        in_specs=[col(1), col(2)],
        out_specs=[out_spec, out_spec],
        compiler_params=_cparams(("parallel", "parallel"), 8 * tr * D_ATT * 4 / 2**20 + 8),
        name="kv_window",
    )(proj, proj)


def _attn_prompt_body(q_ref, k_ref, v_ref, o_ref, kt_s, va_s, on_s, lse_s, *, seq, dilations):
    scale = ATT_HD ** -0.5
    nblk = seq // Q_BLOCK
    qi = lax.broadcasted_iota(I32, (Q_BLOCK, Q_BLOCK), 0)
    kj = lax.broadcasted_iota(I32, (Q_BLOCK, Q_BLOCK), 1)
    cur_ok = kj <= qi
    prev_ok = kj >= qi
    va_s[:, :, ATT_HD:] = jnp.ones((nblk, Q_BLOCK, ATT_HD), BF16)

    for br, (_, d) in enumerate(dilations):
        span = d * Q_BLOCK
        nb = seq // span

        def stream_rows(t, d=d, span=span, nb=nb):
            r = t // nb
            ib = t % nb
            return r, ib, pl.ds(r + ib * span, Q_BLOCK, stride=d)

        def prep(g, carry, stream_rows=stream_rows):
            loaded = []
            for j in range(ATTN_GROUP):
                t = g * ATTN_GROUP + j
                _, _, rows = stream_rows(t)
                loaded.append((t, k_ref[rows, :], v_ref[rows, :]))
            for t, kk, vv in loaded:
                kt_s[t] = jnp.transpose(kk).astype(BF16)
                va_s[t, :, 0:ATT_HD] = vv.astype(BF16)
            return carry

        lax.fori_loop(0, nblk // ATTN_GROUP, prep, 0)

        def group(g, carry, br=br, nb=nb, stream_rows=stream_rows):
            scores = []
            for j in range(ATTN_GROUP):
                t = g * ATTN_GROUP + j
                r, ib, rows = stream_rows(t)
                tp = jnp.maximum(t - 1, r * nb)
                q = (q_ref[rows, :] * scale).astype(BF16)
                s = _dot(q, jnp.concatenate([kt_s[tp], kt_s[t]], axis=1))
                scores.append((t, tp, ib, rows, s))
            probs = []
            for t, tp, ib, rows, s in scores:
                ok = jnp.concatenate([jnp.logical_and(prev_ok, ib > 0), cur_ok], axis=1)
                s = jnp.where(ok, s, NEG_INF)
                m = jnp.max(s, axis=-1, keepdims=True)
                probs.append((t, tp, rows, m, jnp.exp(s - m).astype(BF16)))
            outs = [(rows, m, _dot(p, jnp.concatenate([va_s[tp], va_s[t]], axis=0)))
                    for t, tp, rows, m, p in probs]
            for rows, m, al in outs:
                l = al[:, ATT_HD:]
                on_s[br, rows, :] = al[:, :ATT_HD] / l
                lse_s[br, rows, :] = m + jnp.log(l)
            return carry

        lax.fori_loop(0, nblk // ATTN_GROUP, group, 0)

    chunk = 256
    nbr = len(dilations)

    def merge(c, carry):
        rows = pl.ds(pl.multiple_of(c * chunk, chunk), chunk)
        ls = [lse_s[b, rows, :] for b in range(nbr)]
        mx = functools.reduce(jnp.maximum, ls)
        es = [jnp.exp(li - mx) for li in ls]
        num = sum(es[b] * on_s[b, rows, :] for b in range(nbr))
        o_ref[rows, :] = num / sum(es)
        return carry

    lax.fori_loop(0, seq // chunk, merge, 0)


def _attn_prompt(proj, *, n_batch, seq, dilations=DILATIONS):
    for w, d in dilations:
        assert w // d == Q_BLOCK and seq % (d * Q_BLOCK) == 0
    nbr = len(dilations)
    nblk = seq // Q_BLOCK
    assert nblk % ATTN_GROUP == 0
    blk = lambda off: pl.BlockSpec((seq, ATT_HD), lambda b, h, off=off: (b, off + h))
    vmem = ((4 * 2 + 2 * nbr) * seq * ATT_HD * 4 + 3 * seq * ATT_HD * 2) / 2**20 + 8
    return pl.pallas_call(
        functools.partial(_attn_prompt_body, seq=seq, dilations=dilations),
        out_shape=jax.ShapeDtypeStruct((n_batch * seq, D_ATT), F32),
        grid=(n_batch, ATT_HEADS),
        in_specs=[blk(0), blk(ATT_HEADS), blk(2 * ATT_HEADS)],
        out_specs=pl.BlockSpec((seq, ATT_HD), lambda b, h: (b, h)),
        scratch_shapes=[pltpu.VMEM((nblk, ATT_HD, Q_BLOCK), BF16),
                        pltpu.VMEM((nblk, Q_BLOCK, 2 * ATT_HD), BF16),
                        pltpu.VMEM((nbr, seq, ATT_HD), F32),
                        pltpu.VMEM((nbr, seq, ATT_HD), F32)],
        compiler_params=_cparams(("parallel", "parallel"), vmem),
        name="attn_prompt",
    )(proj, proj, proj)


def _sample_key_multiplicity(n_new, n_cache, past_len, dilations):
    d_max = max(d for _, d in dilations)
    tail = max(w for w, d in dilations if d != d_max)
    assert past_len % d_max == 0 and n_cache % d_max == 0 and n_new <= d_max // 2
    assert tail % d_max == 0 and tail <= n_cache
    half = d_max // 2
    n_grid = (n_cache - tail) // d_max
    kv_start = past_len - n_cache
    grid_rows = (np.arange(n_grid)[:, None] * d_max + np.arange(half)[None, :]).reshape(-1)
    tail_rows = n_cache - tail + np.arange(tail)
    new_rows = n_cache + np.arange(n_new)
    qpos = past_len + np.arange(n_new)

    def mult(rows):
        kpos = kv_start + rows
        delta = qpos[:, None] - kpos[None, :]
        c = np.zeros(delta.shape, np.float32)
        for w, d in dilations:
            c += ((delta >= 0) & (delta <= w) & (delta % d == 0) & (kpos[None, :] >= kv_start))
        return c

    fetched = np.zeros(n_cache + n_new, bool)
    fetched[grid_rows] = True
    fetched[tail_rows] = True
    fetched[new_rows] = True
    assert not mult(np.nonzero(~fetched)[0]).any()
    return mult(grid_rows), mult(tail_rows), mult(new_rows), n_grid, tail, half, d_max


def _attn_sample_body(q_ref, kn_ref, vn_ref, kg_ref, kt_ref, vg_ref, vt_ref,
                      cg_ref, ct_ref, cn_ref, o_ref):
    scale = ATT_HD ** -0.5
    heads = lambda ref: jnp.concatenate(
        [ref[:, h * ATT_HD:(h + 1) * ATT_HD] for h in range(ATT_HEADS)], axis=0)
    q = (heads(q_ref) * scale).astype(BF16)
    kn = heads(kn_ref).astype(BF16)
    vn = heads(vn_ref).astype(BF16)
    flat = lambda ref: ref[...].reshape(-1, ATT_HD).astype(BF16)
    cg, ct, cn = cg_ref[...], ct_ref[...], cn_ref[...]
    sg = jnp.where(cg > 0, _dot_nt(q, flat(kg_ref)), NEG_INF)
    st = jnp.where(ct > 0, _dot_nt(q, flat(kt_ref)), NEG_INF)
    sn = jnp.where(cn > 0, _dot_nt(q, kn), NEG_INF)
    m = jnp.maximum(jnp.maximum(jnp.max(sg, axis=-1, keepdims=True),
                                jnp.max(st, axis=-1, keepdims=True)),
                    jnp.max(sn, axis=-1, keepdims=True))
    pg = cg * jnp.exp(sg - m)
    pt = ct * jnp.exp(st - m)
    pn = cn * jnp.exp(sn - m)
    l = (jnp.sum(pg, axis=-1, keepdims=True) + jnp.sum(pt, axis=-1, keepdims=True)
         + jnp.sum(pn, axis=-1, keepdims=True))
    acc = (_dot(pg.astype(BF16), flat(vg_ref)) + _dot(pt.astype(BF16), flat(vt_ref))
           + _dot(pn.astype(BF16), vn))
    out = acc / l
    n_new = q_ref.shape[0]
    for h in range(ATT_HEADS):
        o_ref[:, h * ATT_HD:(h + 1) * ATT_HD] = out[h * n_new:(h + 1) * n_new, :]


def _attn_sample(proj, win_k, win_v, *, row0, n_seq, n_new, past_len=PAST_LEN,
                 dilations=DILATIONS):
    n_cache = win_k.shape[1]
    cg, ct, cn, n_grid, tail, half, d_max = _sample_key_multiplicity(
        n_new, n_cache, past_len, dilations)
    assert row0 % n_new == 0 and n_new % SUBLANES == 0 and n_cache % tail == 0
    eye = np.eye(ATT_HEADS, dtype=np.float32)
    key_major = lambda c: np.einsum("tk,hg->htkg", c, eye).reshape(ATT_HEADS * n_new, -1)
    head_major = lambda c: np.einsum("tk,hg->htgk", c, eye).reshape(ATT_HEADS * n_new, -1)
    cg, ct, cn = key_major(cg), key_major(ct), head_major(cn)
    rb = row0 // n_new
    n_groups = n_cache // d_max
    kgv = win_k.reshape(n_seq, n_groups, d_max, ATT_HEADS, ATT_HD)
    vgv = win_v.reshape(n_seq, n_groups, d_max, ATT_HEADS, ATT_HD)
    ktv = win_k.reshape(n_seq, n_cache // tail, tail, ATT_HEADS, ATT_HD)
    vtv = win_v.reshape(n_seq, n_cache // tail, tail, ATT_HEADS, ATT_HD)
    new = lambda off: pl.BlockSpec((n_new, D_ATT), lambda b, off=off: (rb + b, off))
    grid_spec = pl.BlockSpec((None, n_grid, half, ATT_HEADS, ATT_HD), lambda b: (b, 0, 0, 0, 0))
    tail_spec = pl.BlockSpec((None, None, tail, ATT_HEADS, ATT_HD),
                             lambda b: (b, n_cache // tail - 1, 0, 0, 0))
    const = lambda a: pl.BlockSpec(a.shape, lambda b: (0, 0))
    vmem = (2 * 2 * (n_grid * half + tail) * D_ATT * 4 + 4 * cg.size * 4 * 3) / 2**20 + 12
    return pl.pallas_call(
        _attn_sample_body,
        out_shape=jax.ShapeDtypeStruct((n_seq * n_new, D_ATT), F32),
        grid=(n_seq,),
        in_specs=[new(0), new(1), new(2), grid_spec, tail_spec, grid_spec, tail_spec,
                  const(cg), const(ct), const(cn)],
        out_specs=pl.BlockSpec((n_new, D_ATT), lambda b: (b, 0)),
        compiler_params=_cparams(("parallel",), vmem),
        name="attn_sample",
    )(proj, proj, proj, kgv, ktv, vgv, vtv, jnp.asarray(cg), jnp.asarray(ct), jnp.asarray(cn))


def _gelu_tanh(x):
    return 0.5 * x * (1.0 + jnp.tanh(math.sqrt(2.0 / math.pi) * (x + 0.044715 * (x * x * x))))


def _ssm_body(u_ref, bb_ref, cst_ref, a_ref, ap_ref, d_ref, hre_ref, him_ref, *rest,
              tl, npar, seq_len, nseg, emit_y, exact_in):
    if emit_y:
        y_ref, fre_ref, fim_ref, h_s = rest
    else:
        fre_ref, fim_ref, h_s = rest
    ns = SSM_STATES_PER_TILE
    c = pl.program_id(1)
    ngrp = npar // SUBLANES

    def step_rows(i, g):
        return pl.ds(c * tl + i + g * SUBLANES * seq_len, SUBLANES, stride=seq_len)

    @pl.when(c == 0)
    def _init():
        if nseg == 1:
            h_s[0] = hre_ref[...]
            h_s[1] = him_ref[...]
        else:
            pr, pi = ap_ref[0:1, :], ap_ref[1:2, :]
            for b in range(npar // nseg):
                sr = jnp.zeros((1, ns), F32)
                si = jnp.zeros((1, ns), F32)
                for j in range(nseg):
                    row = b * nseg + j
                    h_s[0, row:row + 1, :] = sr
                    h_s[1, row:row + 1, :] = si
                    er, ei = hre_ref[row:row + 1, :], him_ref[row:row + 1, :]
                    sr, si = pr * sr - pi * si + er, pr * si + pi * sr + ei

    ar = jnp.broadcast_to(a_ref[0:1, :], (SUBLANES, ns))
    ai = jnp.broadcast_to(a_ref[1:2, :], (SUBLANES, ns))
    us = [jnp.concatenate([u_ref[step_rows(i, g), :] for i in range(tl)], axis=0)
          for g in range(ngrp)]
    if exact_in:
        xs = [jnp.dot(u, bb_ref[...], precision=lax.Precision.HIGHEST, preferred_element_type=F32)
              for u in us]
    else:
        xs = [_dot(u.astype(BF16), bb_ref[...]) for u in us]

    hs = []
    for g in range(ngrp):
        gs = slice(g * SUBLANES, (g + 1) * SUBLANES)
        hr, hi = h_s[0, gs, :], h_s[1, gs, :]
        states = []
        for i in range(tl):
            xr = xs[g][i * SUBLANES:(i + 1) * SUBLANES, 0:ns]
            xi = xs[g][i * SUBLANES:(i + 1) * SUBLANES, ns:2 * ns]
            hr, hi = ar * hr - ai * hi + xr, ar * hi + ai * hr + xi
            if emit_y:
                states.append(jnp.concatenate([hr, hi], axis=1))
        h_s[0, gs, :] = hr
        h_s[1, gs, :] = hi
        if emit_y:
            hs.append(jnp.concatenate(states, axis=0))

    if emit_y:
        for g in range(ngrp):
            y = _gelu_tanh(_dot(hs[g].astype(BF16), cst_ref[...]) + d_ref[...] * us[g])
            for i in range(tl):
                y_ref[step_rows(i, g), :] = y[i * SUBLANES:(i + 1) * SUBLANES, :]

    @pl.when(c == pl.num_programs(1) - 1)
    def _fin():
        fre_ref[...] = h_s[0]
        fim_ref[...] = h_s[1]


def _ssm_scan(proj, prm, hin_re, hin_im, *, seq_len, tl, nseg, emit_y, exact_in, name):
    rows = proj.shape[0]
    npar = rows // seq_len
    assert npar % SUBLANES == 0 and seq_len % tl == 0
    ns = SSM_STATES_PER_TILE
    nk = D_SSM // SSM_LANE_TILE
    col0 = (proj.shape[1] - D_SSM) // SSM_LANE_TILE
    bb = prm["bb_f32"] if exact_in else prm["bb_bf16"]
    in_specs = [
        pl.BlockSpec((rows, SSM_LANE_TILE), lambda k, c: (0, col0 + k)),
        pl.BlockSpec((None, SSM_LANE_TILE, 2 * ns), lambda k, c: (k, 0, 0)),
        pl.BlockSpec((None, 2 * ns, SSM_LANE_TILE), lambda k, c: (k, 0, 0)),
        pl.BlockSpec((None, 2, ns), lambda k, c: (k, 0, 0)),
        pl.BlockSpec((None, 2, ns), lambda k, c: (k, 0, 0)),
        pl.BlockSpec((1, SSM_LANE_TILE), lambda k, c: (0, k)),
        pl.BlockSpec((npar, ns), lambda k, c: (0, k)),
        pl.BlockSpec((npar, ns), lambda k, c: (0, k)),
    ]
    state_shape = jax.ShapeDtypeStruct((npar, nk * ns), F32)
    state_spec = pl.BlockSpec((npar, ns), lambda k, c: (0, k))
    out_shape = [state_shape, state_shape]
    out_specs = [state_spec, state_spec]
    if emit_y:
        out_shape = [jax.ShapeDtypeStruct((rows, D_SSM), F32)] + out_shape
        out_specs = [pl.BlockSpec((rows, SSM_LANE_TILE), lambda k, c: (0, k))] + out_specs
    vmem = (4 * rows * SSM_LANE_TILE * 4 + tl * npar * 2 * ns * 4) / 2**20 + 16
    return pl.pallas_call(
        functools.partial(_ssm_body, tl=tl, npar=npar, seq_len=seq_len, nseg=nseg, emit_y=emit_y,
                          exact_in=exact_in),
        out_shape=out_shape,
        grid=(nk, seq_len // tl),
        in_specs=in_specs,
        out_specs=out_specs,
        scratch_shapes=[pltpu.VMEM((2, npar, ns), F32)],
        compiler_params=_cparams(("parallel", "arbitrary"), vmem),
        name=name,
    )(proj, bb, prm["cst"], prm["a"], prm["apow"], prm["d"], hin_re, hin_im)


def _ssm_params(lam_re, lam_im, log_dt, b_re, b_im, c_re, c_im, d_skip, seg_len):
    g, p, c = N_SSM_GROUPS, SSM_STATE, SSM_GROUP_CH
    nk, gt = g // SSM_GROUPS_PER_TILE, SSM_GROUPS_PER_TILE
    dt = jnp.exp(log_dt.astype(F32))[:, None]
    lr, li = lam_re.astype(F32), lam_im.astype(F32)
    mag = jnp.exp(lr * dt)
    a_re, a_im = mag * jnp.cos(li * dt), mag * jnp.sin(li * dt)
    magp = jnp.exp(lr * dt * seg_len)
    p_re, p_im = magp * jnp.cos(li * dt * seg_len), magp * jnp.sin(li * dt * seg_len)
    den = lr * lr + li * li
    nr, ni = a_re - 1.0, a_im
    f_re, f_im = (nr * lr + ni * li) / den, (ni * lr - nr * li) / den
    br, bi = b_re.astype(F32), b_im.astype(F32)
    bb_re = f_re[..., None] * br - f_im[..., None] * bi
    bb_im = f_re[..., None] * bi + f_im[..., None] * br
    eye = jnp.eye(gt, dtype=F32)

    def pack_b(m):
        return jnp.einsum("kgpc,gh->kgchp", m.reshape(nk, gt, p, c), eye).reshape(nk, gt * c, gt * p)

    def pack_c(m):
        return jnp.einsum("kgcp,gh->kgphc", m.reshape(nk, gt, c, p), eye).reshape(nk, gt * p, gt * c)

    bb = jnp.concatenate([pack_b(bb_re), pack_b(bb_im)], axis=2)
    cst = jnp.concatenate([pack_c(c_re.astype(F32)), -pack_c(c_im.astype(F32))], axis=1)
    tile = lambda v: v.reshape(nk, 1, gt * p)
    return {
        "bb_f32": bb, "bb_bf16": bb.astype(BF16), "cst": cst.astype(BF16),
        "a": jnp.concatenate([tile(a_re), tile(a_im)], axis=1),
        "apow": jnp.concatenate([tile(p_re), tile(p_im)], axis=1),
        "d": d_skip.astype(F32).reshape(1, g * c),
    }


def _glu_body(y_ref, w_ref, o_ref):
    yg = y_ref[...]
    z = _dot(yg.astype(BF16), w_ref[...])
    o_ref[...] = yg * (1.0 / (1.0 + jnp.exp(-z)))


def _glu(yg, w, *, tm, name):
    m, n = yg.shape
    return pl.pallas_call(
        _glu_body,
        out_shape=jax.ShapeDtypeStruct((m, n), F32),
        grid=(m // tm,),
        in_specs=[pl.BlockSpec((tm, n), lambda i: (i, 0)), pl.BlockSpec((n, n), lambda i: (0, 0))],
        out_specs=pl.BlockSpec((tm, n), lambda i: (i, 0)),
        compiler_params=_cparams(("parallel",), 4 * tm * n * 4 / 2**20 + 12),
        name=name,
    )(yg, w)


def _mix_body(attn_ref, ssm_ref, ga_ref, gs_ref, w_ref, x_ref, g_ref, b_ref, o_ref):
    a = _rmsnorm(attn_ref[...], ga_ref[...]).astype(BF16)
    s = _rmsnorm(ssm_ref[...], gs_ref[...]).astype(BF16)
    mix = _dot(a, w_ref[0:D_ATT, :]) + _dot(s, w_ref[D_ATT:D_ATT + D_SSM, :])
    o_ref[...] = _layernorm(DEEPNORM_ALPHA * x_ref[...] + mix, g_ref[...], b_ref[...])


def _mix(attn, ssm, ga, gs, w, x, g, b, *, tm, name):
    m = x.shape[0]
    row = lambda n: pl.BlockSpec((tm, n), lambda i: (i, 0))
    const = lambda a: pl.BlockSpec(a.shape, lambda i: (0, 0))
    return pl.pallas_call(
        _mix_body,
        out_shape=jax.ShapeDtypeStruct((m, D_MODEL), F32),
        grid=(m // tm,),
        in_specs=[row(D_ATT), row(D_SSM), const(ga), const(gs), const(w), row(D_MODEL),
                  const(g), const(b)],
        out_specs=row(D_MODEL),
        compiler_params=_cparams(("parallel",), 6 * tm * D_MODEL * 4 / 2**20 + 24),
        name=name,
    )(attn, ssm, ga, gs, w, x, g, b)


def _store_gatherable(o_ref, y):
    rows = y.shape[0]
    for c in range(ROW_CHUNKS):
        o_ref[pl.ds(c, rows, stride=ROW_PITCH), :] = y[:, c * LANES:(c + 1) * LANES]
    for c in range(ROW_CHUNKS, ROW_PITCH):
        o_ref[pl.ds(c, rows, stride=ROW_PITCH), :] = jnp.zeros((rows, LANES), F32)


def _load_gathered(buf, rows):
    return jnp.concatenate([buf[pl.ds(c, rows, stride=ROW_PITCH), :] for c in range(ROW_CHUNKS)],
                           axis=1)


def _start_row_gather(src_hbm, idx, buf, r, sem):
    pltpu.make_async_copy(src_hbm.at[pl.ds(idx * ROW_PITCH, ROW_CHUNKS), :],
                          buf.at[pl.ds(r * ROW_PITCH, ROW_CHUNKS), :], sem).start()


def _wait_row_gathers(buf, other, rows, sem):
    span = pl.ds(0, rows * ROW_CHUNKS)
    pltpu.make_async_copy(other.at[span, :], buf.at[span, :], sem).wait()


def _mm_ln_body(a1_ref, a2_ref, w_ref, x1_ref, x2_ref, g_ref, b_ref, o_ref, rows_ref, *, tiles1):
    first = pl.program_id(0) < tiles1
    a = jnp.where(first, a1_ref[...], a2_ref[...])
    x = jnp.where(first, x1_ref[...], x2_ref[...])
    y = _dot(a.astype(BF16), w_ref[...])
    out = _layernorm(DEEPNORM_ALPHA * x + y, g_ref[...], b_ref[...])
    o_ref[...] = out
    _store_gatherable(rows_ref, out)


def _mm_ln(a1, a2, w, x1, x2, g, b, *, name):
    tm = a2.shape[0]
    assert a1.shape[0] % tm == 0
    tiles1 = a1.shape[0] // tm
    m = a1.shape[0] + tm
    row1 = lambda n: pl.BlockSpec((tm, n), lambda i: (jnp.minimum(i, tiles1 - 1), 0))
    row2 = lambda n: pl.BlockSpec((tm, n), lambda i: (0, 0))
    const = lambda v: pl.BlockSpec(v.shape, lambda i: (0, 0))
    return pl.pallas_call(
        functools.partial(_mm_ln_body, tiles1=tiles1),
        out_shape=[jax.ShapeDtypeStruct((m, D_MODEL), F32),
                   jax.ShapeDtypeStruct((m * ROW_PITCH, LANES), F32)],
        grid=(tiles1 + 1,),
        in_specs=[row1(a1.shape[1]), row2(a2.shape[1]), const(w), row1(D_MODEL), row2(D_MODEL),
                  const(g), const(b)],
        out_specs=[pl.BlockSpec((tm, D_MODEL), lambda i: (i, 0)),
                   pl.BlockSpec((tm * ROW_PITCH, LANES), lambda i: (i, 0))],
        compiler_params=_cparams(("parallel",), 12 * tm * D_MODEL * 4 / 2**20 + 24),
        name=name,
    )(a1, a2, w, x1, x2, g, b)


def _memattn_body(q_ref, k_ref, v_ref, o_ref):
    scale = MEM_HD ** -0.5
    for h in range(MEM_HEADS):
        sl = slice(h * MEM_HD, (h + 1) * MEM_HD)
        s = _dot_nt(q_ref[:, sl].astype(BF16), k_ref[:, sl].astype(BF16)) * scale
        m = jnp.max(s, axis=-1, keepdims=True)
        p = jnp.exp(s - m)
        l = jnp.sum(p, axis=-1, keepdims=True)
        o_ref[:, sl] = _dot(p.astype(BF16), v_ref[:, sl].astype(BF16)) / l


def _memattn(q, mem_k, mem_v, *, row0, n_seq, seq, tq, name):
    assert seq % tq == 0 and row0 % tq == 0
    nq = seq // tq
    rb = row0 // tq
    mem_spec = pl.BlockSpec((None, N_MEM, D_MODEL), lambda b, i: (b, 0, 0))
    return pl.pallas_call(
        _memattn_body,
        out_shape=jax.ShapeDtypeStruct((n_seq * seq, D_MODEL), F32),
        grid=(n_seq, nq),
        in_specs=[pl.BlockSpec((tq, D_MODEL), lambda b, i: (rb + b * nq + i, 0)),
                  mem_spec, mem_spec],
        out_specs=pl.BlockSpec((tq, D_MODEL), lambda b, i: (b * nq + i, 0)),
        compiler_params=_cparams(("parallel", "parallel"),
                                 4 * (tq + N_MEM) * D_MODEL * 4 / 2**20 + 8),
        name=name,
    )(q, mem_k, mem_v)


def _memattn_heads_body(q_ref, k_ref, v_ref, c_ref, o_ref):
    scale = MEM_HD ** -0.5
    tq = q_ref.shape[0]
    q = jnp.concatenate([q_ref[:, h * MEM_HD:(h + 1) * MEM_HD] for h in range(MEM_HEADS)], axis=0)
    k = k_ref[...].reshape(N_MEM * MEM_HEADS, MEM_HD).astype(BF16)
    v = v_ref[...].reshape(N_MEM * MEM_HEADS, MEM_HD).astype(BF16)
    s = jnp.where(c_ref[...] > 0, _dot_nt(q.astype(BF16), k) * scale, NEG_INF)
    m = jnp.max(s, axis=-1, keepdims=True)
    p = jnp.exp(s - m)
    l = jnp.sum(p, axis=-1, keepdims=True)
    o = _dot(p.astype(BF16), v) / l
    for h in range(MEM_HEADS):
        o_ref[:, h * MEM_HD:(h + 1) * MEM_HD] = o[h * tq:(h + 1) * tq, :]


def _memattn_heads(q, mem_k, mem_v, *, row0, n_seq, seq, name):
    assert row0 % seq == 0 and seq % SUBLANES == 0
    rb = row0 // seq
    same_head = np.kron(np.eye(MEM_HEADS, dtype=np.float32), np.ones((seq, 1), np.float32))
    same_head = np.tile(same_head, (1, N_MEM))
    mem_spec = pl.BlockSpec((None, N_MEM, MEM_HEADS, MEM_HD), lambda b: (b, 0, 0, 0))
    return pl.pallas_call(
        _memattn_heads_body,
        out_shape=jax.ShapeDtypeStruct((n_seq * seq, D_MODEL), F32),
        grid=(n_seq,),
        in_specs=[pl.BlockSpec((seq, D_MODEL), lambda b: (rb + b, 0)), mem_spec, mem_spec,
                  pl.BlockSpec(same_head.shape, lambda b: (0, 0))],
        out_specs=pl.BlockSpec((seq, D_MODEL), lambda b: (b, 0)),
        compiler_params=_cparams(("parallel",), 8 * N_MEM * D_MODEL * 4 / 2**20 + 8),
        name=name,
    )(q, mem_k, mem_v, jnp.asarray(same_head))


def _router_body(x_ref, w_ref, b_ref, sel_ref, wts_ref, cnt_ref, run_s, *, tm):
    i = pl.program_id(0)

    @pl.when(i == 0)
    def _():
        run_s[...] = jnp.zeros_like(run_s)

    ng, epg = N_EXPERT_GROUPS, EXPERTS_PER_GROUP
    x = x_ref[...]
    x_hi = x.astype(BF16)
    x_lo = (x - x_hi.astype(F32)).astype(BF16)
    parts = _dot(x_hi, w_ref[...]) + _dot(x_lo, w_ref[...])
    logits = parts + pltpu.roll(parts, shift=ROUTER_LANES // 2, axis=1) + b_ref[...]
    lane = lax.broadcasted_iota(I32, (tm, ROUTER_LANES), 1)
    big = ROUTER_LANES

    def first_argmax(vals):
        mx = jnp.max(vals, axis=-1, keepdims=True)
        idx = jnp.min(jnp.where(vals == mx, lane, big), axis=-1, keepdims=True)
        return mx, idx

    gl = jnp.where(lane < ng, logits, NEG_INF)
    gmax, gsel = first_argmax(gl)
    g_w = 1.0 / jnp.sum(jnp.exp(gl - gmax), axis=-1, keepdims=True)
    lo = ng + gsel * epg
    el = jnp.where(jnp.logical_and(lane >= lo, lane < lo + epg), logits, NEG_INF)
    v1, i1 = first_argmax(el)
    v2, i2 = first_argmax(jnp.where(lane == i1, NEG_INF, el))
    e21 = jnp.exp(v2 - v1)
    w1 = g_w / (1.0 + e21)
    w2 = g_w * e21 / (1.0 + e21)

    onehot = jnp.logical_or(lane == i1, lane == i2)
    r = lax.broadcasted_iota(I32, (tm, tm), 0)
    cc = lax.broadcasted_iota(I32, (tm, tm), 1)
    tri = (cc < r).astype(BF16)
    before = _dot(tri, onehot.astype(BF16)) + run_s[...]
    rank1 = jnp.sum(jnp.where(lane == i1, before, 0.0), axis=-1, keepdims=True).astype(I32)
    rank2 = jnp.sum(jnp.where(lane == i2, before, 0.0), axis=-1, keepdims=True).astype(I32)
    run_s[...] = run_s[...] + jnp.sum(onehot.astype(F32), axis=0, keepdims=True)

    sel = jnp.where(lane == 0, i1 - ng, jnp.where(lane == 1, i2 - ng,
                    jnp.where(lane == 2, rank1, jnp.where(lane == 3, rank2, 0))))
    sel_ref[...] = sel
    wts_ref[...] = jnp.where(lane == 0, w1, jnp.where(lane == 1, w2, 0.0))
    cnt_ref[...] = run_s[...].astype(I32)


def _router(x, w, b, *, tm):
    m = x.shape[0]
    row = pl.BlockSpec((tm, ROUTER_LANES), lambda i: (i, 0))
    return pl.pallas_call(
        functools.partial(_router_body, tm=tm),
        out_shape=[jax.ShapeDtypeStruct((m, ROUTER_LANES), I32),
                   jax.ShapeDtypeStruct((m, ROUTER_LANES), F32),
                   jax.ShapeDtypeStruct((1, ROUTER_LANES), I32)],
        grid=(m // tm,),
        in_specs=[pl.BlockSpec((tm, D_MODEL), lambda i: (i, 0)),
                  pl.BlockSpec((D_MODEL, ROUTER_LANES), lambda i: (0, 0)),
                  pl.BlockSpec((1, ROUTER_LANES), lambda i: (0, 0))],
        out_specs=[row, row, pl.BlockSpec((1, ROUTER_LANES), lambda i: (0, 0))],
        scratch_shapes=[pltpu.VMEM((1, ROUTER_LANES), F32)],
        compiler_params=_cparams(("arbitrary",), 16),
        name="moe_router",
    )(x, w, b)


def _dispatch_body(eid_ref, rank_ref, off_ref, pad0_ref, npad_ref, nact_ref, x_hbm, xs_hbm, xbuf,
                   zero_s, in_sem, out_sem, pad_sem, *, td, n_tiles):
    i = pl.program_id(0)
    n = pl.num_programs(0)
    tile_rows = MOE_TILE * ROW_PITCH
    in_rows = td * ROW_PITCH

    def row_copy(src, dst_row, s):
        return pltpu.make_async_copy(src, xs_hbm.at[pl.ds(dst_row * ROW_PITCH, ROW_PITCH), :], s)

    def tile_load(t):
        s = t % DISPATCH_SLOTS
        return pltpu.make_async_copy(x_hbm.at[pl.ds(t * in_rows, in_rows), :], xbuf.at[s],
                                     in_sem.at[s])

    def wait_scatter(par):
        for _ in range(2):
            pltpu.make_async_copy(xbuf.at[0], xs_hbm.at[pl.ds(0, in_rows), :],
                                  out_sem.at[par]).wait()

    @pl.when(i == 0)
    def _():
        for t in range(DISPATCH_SLOTS - 1):
            @pl.when(t < n)
            def _(t=t):
                tile_load(t).start()
        zero_s[...] = jnp.zeros_like(zero_s)

        def pad_copies(e):
            out = []
            for bit in reversed(range(MOE_TILE.bit_length() - 1)):
                rows = (1 << bit) * ROW_PITCH
                first = (pad0_ref[e] + (npad_ref[e] >> (bit + 1) << (bit + 1))) * ROW_PITCH
                out.append((jnp.bitwise_and(npad_ref[e] >> bit, 1) == 1, pltpu.make_async_copy(
                    zero_s.at[pl.ds(0, rows), :], xs_hbm.at[pl.ds(first, rows), :], pad_sem.at[0])))
            return out

        for e in range(N_EXPERTS):
            for on, cp in pad_copies(e):
                pl.when(on)(cp.start)
        for e in range(N_EXPERTS):
            for on, cp in pad_copies(e):
                pl.when(on)(cp.wait)

        def zero_tile(t, carry):
            parts = [pltpu.make_async_copy(
                zero_s.at[pl.ds(0, MOE_TILE), :],
                xs_hbm.at[pl.ds(t * tile_rows + j * MOE_TILE, MOE_TILE), :],
                pad_sem.at[0]) for j in range(ROW_PITCH)]
            for cp in parts:
                cp.start()
            for cp in parts:
                cp.wait()
            return carry

        lax.fori_loop(nact_ref[0], n_tiles, zero_tile, 0)

    slot = i % DISPATCH_SLOTS
    par = i % 2
    tile_load(i).wait()
    base = i * td * 2
    for r in range(td):
        for k in range(2):
            j = base + 2 * r + k
            row_copy(xbuf.at[slot, pl.ds(r * ROW_PITCH, ROW_PITCH), :],
                     off_ref[eid_ref[j]] + rank_ref[j], out_sem.at[par]).start()

    @pl.when(i > 0)
    def _():
        wait_scatter(1 - par)

    @pl.when(i + DISPATCH_SLOTS - 1 < n)
    def _():
        tile_load(i + DISPATCH_SLOTS - 1).start()

    @pl.when(i == n - 1)
    def _():
        wait_scatter(par)


def _moe_dispatch(x_rows, eid, rank, row_off, pad_start, pad_count, nact, *, td, n_tiles):
    n = x_rows.shape[0] // ROW_PITCH
    grid_spec = pltpu.PrefetchScalarGridSpec(
        num_scalar_prefetch=6,
        grid=(n // td,),
        in_specs=[pl.BlockSpec(memory_space=pl.ANY)],
        out_specs=pl.BlockSpec(memory_space=pl.ANY),
        scratch_shapes=[pltpu.VMEM((DISPATCH_SLOTS, td * ROW_PITCH, LANES), F32),
                        pltpu.VMEM((MOE_TILE // 2 * ROW_PITCH, LANES), F32),
                        pltpu.SemaphoreType.DMA((DISPATCH_SLOTS,)),
                        pltpu.SemaphoreType.DMA((2,)),
                        pltpu.SemaphoreType.DMA((1,))],
    )
    return pl.pallas_call(
        functools.partial(_dispatch_body, td=td, n_tiles=n_tiles),
        out_shape=jax.ShapeDtypeStruct((n_tiles * MOE_TILE * ROW_PITCH, LANES), F32),
        grid_spec=grid_spec,
        compiler_params=_cparams(("arbitrary",), 16),
        name="moe_dispatch",
    )(eid, rank, row_off, pad_start, pad_count, nact, x_rows)


def _moe_body(te_ref, ord_ref, nxt_ref, nact_ref, x_ref, wg_hbm, wu_hbm, wd_hbm, o_ref,
              wg_f, wu_f, wd_f, wsem, wg_s, wu_s, wd_s):
    i = pl.program_id(0)
    nact = nact_ref[0]
    tm = MOE_TILE

    def weight_copies(expert, ws):
        return [pltpu.make_async_copy(hbm.at[expert], stage.at[ws], wsem.at[ws])
                for hbm, stage in ((wg_hbm, wg_f), (wu_hbm, wu_f), (wd_hbm, wd_f))]

    def tile_step():
        prev = te_ref[jnp.maximum(i - 1, 0)]

        @pl.when(jnp.logical_or(i == 0, te_ref[i] != prev))
        def _():
            ws = ord_ref[i] % 2
            for cp in weight_copies(te_ref[i], ws):
                cp.wait()
            wg_s[...] = wg_f[ws].astype(BF16)
            wu_s[...] = wu_f[ws].astype(BF16)
            wd_s[...] = wd_f[ws].astype(BF16)

            @pl.when(nxt_ref[i] >= 0)
            def _():
                for cp in weight_copies(nxt_ref[i], 1 - ws):
                    cp.start(priority=1)

        x = _load_gathered(x_ref, tm).astype(BF16)
        hg = _dot(x, wg_s[...])
        hu = _dot(x, wu_s[...])
        h = hg * (1.0 / (1.0 + jnp.exp(-hg))) * hu
        _store_gatherable(o_ref, _dot(h.astype(BF16), wd_s[...]))

    @pl.when(i == 0)
    def _():
        for cp in weight_copies(te_ref[0], 0):
            cp.start(priority=1)

    @pl.when(i < nact)
    def _():
        tile_step()

    @pl.when(i >= nact)
    def _():
        o_ref[...] = jnp.zeros_like(o_ref)


def _moe_experts(x_sorted, w_gate, w_up, w_down, tile_expert, tile_ord, tile_next, nact, *,
                 n_tiles):
    tm = MOE_TILE
    in_map = lambda i, te, od, nx, n: (jnp.minimum(i, n[0] - 1), 0)
    any_spec = pl.BlockSpec(memory_space=pl.ANY)
    grid_spec = pltpu.PrefetchScalarGridSpec(
        num_scalar_prefetch=4,
        grid=(n_tiles,),
        in_specs=[pl.BlockSpec((tm * ROW_PITCH, LANES), in_map), any_spec, any_spec, any_spec],
        out_specs=pl.BlockSpec((tm * ROW_PITCH, LANES), lambda i, te, od, nx, n: (i, 0)),
        scratch_shapes=[pltpu.VMEM((2, D_MODEL, D_EXPERT), F32),
                        pltpu.VMEM((2, D_MODEL, D_EXPERT), F32),
                        pltpu.VMEM((2, D_EXPERT, D_MODEL), F32),
                        pltpu.SemaphoreType.DMA((2,)),
                        pltpu.VMEM((D_MODEL, D_EXPERT), BF16),
                        pltpu.VMEM((D_MODEL, D_EXPERT), BF16),
                        pltpu.VMEM((D_EXPERT, D_MODEL), BF16)],
    )
    return pl.pallas_call(
        _moe_body,
        out_shape=jax.ShapeDtypeStruct((n_tiles * tm * ROW_PITCH, LANES), F32),
        grid_spec=grid_spec,
        compiler_params=_cparams(("arbitrary",), 48),
        name="moe_experts",
    )(tile_expert, tile_ord, tile_next, nact, x_sorted, w_gate, w_up, w_down)


def _combine_body(eid_ref, rank_ref, off_ref, ys_hbm, wts_ref, x_ref, g_ref, b_ref, o1_ref, o2_ref,
                  buf, sem, *, tc, tiles1):
    i = pl.program_id(0)
    n = pl.num_programs(0)
    slot = i % GATHER_SLOTS
    ahead = GATHER_SLOTS - 1

    def issue_gather(tile, slot_):
        base = tile * tc * 2
        for r in range(tc):
            for k in range(2):
                j = base + 2 * r + k
                _start_row_gather(ys_hbm, off_ref[eid_ref[j]] + rank_ref[j], buf.at[slot_, k], r,
                                  sem.at[slot_])

    @pl.when(i == 0)
    def _():
        for t in range(ahead):
            @pl.when(t < n)
            def _(t=t):
                issue_gather(t, t)

    for k in range(2):
        _wait_row_gathers(buf.at[slot, k], buf.at[(i + 1) % GATHER_SLOTS, k], tc, sem.at[slot])

    @pl.when(i + ahead < n)
    def _():
        issue_gather(i + ahead, (i + ahead) % GATHER_SLOTS)

    w = wts_ref[...]
    moe = (w[:, 0:1] * _load_gathered(buf.at[slot, 0], tc)
           + w[:, 1:2] * _load_gathered(buf.at[slot, 1], tc))
    out = _layernorm(DEEPNORM_ALPHA * x_ref[...] + moe, g_ref[...], b_ref[...])

    @pl.when(i < tiles1)
    def _():
        o1_ref[...] = out

    @pl.when(i >= tiles1)
    def _():
        o2_ref[...] = out


def _moe_combine(ys, eid, rank, row_off, wts, x, g, b, *, tc, n_first):
    m = x.shape[0]
    assert n_first % tc == 0 and (m - n_first) % tc == 0
    tiles1 = n_first // tc
    grid_spec = pltpu.PrefetchScalarGridSpec(
        num_scalar_prefetch=3,
        grid=(m // tc,),
        in_specs=[pl.BlockSpec(memory_space=pl.ANY),
                  pl.BlockSpec((tc, ROUTER_LANES), lambda i, *_: (i, 0)),
                  pl.BlockSpec((tc, D_MODEL), lambda i, *_: (i, 0)),
                  pl.BlockSpec((1, D_MODEL), lambda i, *_: (0, 0)),
                  pl.BlockSpec((1, D_MODEL), lambda i, *_: (0, 0))],
        out_specs=[pl.BlockSpec((tc, D_MODEL), lambda i, *_: (jnp.minimum(i, tiles1 - 1), 0)),
                   pl.BlockSpec((tc, D_MODEL), lambda i, *_: (jnp.maximum(i - tiles1, 0), 0))],
        scratch_shapes=[pltpu.VMEM((GATHER_SLOTS, 2, tc * ROW_PITCH, LANES), F32),
                        pltpu.SemaphoreType.DMA((GATHER_SLOTS,))],
    )
    return pl.pallas_call(
        functools.partial(_combine_body, tc=tc, tiles1=tiles1),
        out_shape=[jax.ShapeDtypeStruct((n_first, D_MODEL), F32),
                   jax.ShapeDtypeStruct((m - n_first, D_MODEL), F32)],
        grid_spec=grid_spec,
        compiler_params=_cparams(("arbitrary",), 16 * tc * D_MODEL * 4 / 2**20 + 8),
        name="moe_combine_ln3",
    )(eid, rank, row_off, ys, wts, x, g, b)


def _moe(x, x_rows, w_r1, b_r1, w_r2, b_r2, w_gate, w_up, w_down, g, b, *, n_first, tm_router, tc):
    n = x.shape[0]
    ng, ne = N_EXPERT_GROUPS, N_EXPERTS
    half = ROUTER_LANES // 2
    assert ng + ne <= half
    w_r = jnp.concatenate([w_r1, w_r2.reshape(D_MODEL, ne),
                           jnp.zeros((D_MODEL, half - ng - ne), F32)], axis=1)
    w_hi = w_r.astype(BF16)
    w_lo = (w_r - w_hi.astype(F32)).astype(BF16)
    w_r = jnp.concatenate([w_hi, w_lo], axis=1)
    b_r = jnp.concatenate([b_r1, b_r2.reshape(ne), jnp.zeros((half - ng - ne,), F32)])
    b_r = jnp.concatenate([b_r, b_r]).reshape(1, ROUTER_LANES)
    sel, wts, cnt = _router(x, w_r, b_r, tm=tm_router)

    tm = MOE_TILE
    n_tiles = (2 * n) // tm + ne
    counts = cnt[0, ng:ng + ne]
    tiles_per = (counts + tm - 1) // tm
    tile_end = jnp.cumsum(tiles_per)
    row_off = (tile_end - tiles_per) * tm
    nact = tile_end[-1]
    a_eid, a_rank = sel[:, 0:2].reshape(-1), sel[:, 2:4].reshape(-1)
    row_off = row_off.astype(I32)
    tile_ids = jnp.minimum(jnp.arange(n_tiles, dtype=I32), nact - 1)
    tile_expert = jnp.sum((tile_end[None, :] <= tile_ids[:, None]).astype(I32), axis=1)
    used = tiles_per > 0
    eid = jnp.arange(ne, dtype=I32)
    ordinal = jnp.cumsum(used.astype(I32)) - 1
    later = jnp.where(jnp.logical_and(used[None, :], eid[None, :] > eid[:, None]), eid[None, :], ne)
    nxt = jnp.min(later, axis=1)
    nxt = jnp.where(nxt == ne, -1, nxt)

    nact = nact.reshape(1).astype(I32)
    x_sorted = _moe_dispatch(x_rows, a_eid, a_rank, row_off, (row_off + counts).astype(I32),
                             (tiles_per * tm - counts).astype(I32), nact, td=tc, n_tiles=n_tiles)
    ys = _moe_experts(x_sorted, w_gate, w_up, w_down, tile_expert, ordinal[tile_expert],
                      nxt[tile_expert], nact, n_tiles=n_tiles)
    return _moe_combine(ys, a_eid, a_rank, row_off, wts, x, g, b, tc=tc, n_first=n_first)


def _row_tile(m, cap):
    best = SUBLANES
    for t in range(SUBLANES, cap + 1, SUBLANES):
        if m % t == 0:
            best = t
    return best


def kernel(x_prompt, x_sample, cache_win_k, cache_win_v, state_ssm_re, state_ssm_im, cache_mem_k, cache_mem_v, mem_prompt, w_in, ssm_lam_re, ssm_lam_im, ssm_log_dt, ssm_b_re, ssm_b_im, ssm_c_re, ssm_c_im, ssm_d, w_glu, g_attn, g_ssm, w_out, ln1_g, ln1_b, w_mq, w_mk, w_mv, w_mo, ln2_g, ln2_b, w_r1, b_r1, w_r2, b_r2, w_gate, w_up, w_down, ln3_g, ln3_b):
    nb, seq, d = x_prompt.shape
    ns, dseq, _ = x_sample.shape
    n_p, n_s = nb * seq, ns * dseq
    n = n_p + n_s
    l = 0
    row2 = lambda v: v[l].reshape(1, -1)

    x_p, x_s = x_prompt.reshape(n_p, d), x_sample.reshape(n_s, d)
    tm_p = _row_tile(n_p, 1024)
    tm_ln = _row_tile(n_p, 512)
    assert n_p % n_s == 0 and n_s % SUBLANES == 0

    proj_p = _matmul(x_p, w_in[l], tm=tm_p, tn=1024, name="proj_in_prompt")
    proj_s = _matmul(x_s, w_in[l], tm=n_s, tn=1024, name="proj_in_sample")

    attn_p = _attn_prompt(proj_p, n_batch=nb, seq=seq)
    attn_s = _attn_sample(proj_s, cache_win_k[l], cache_win_v[l], row0=0, n_seq=ns, n_new=dseq)

    seg_len = seq // SSM_SEGMENTS
    prm = _ssm_params(ssm_lam_re[l], ssm_lam_im[l], ssm_log_dt[l], ssm_b_re[l], ssm_b_im[l],
                      ssm_c_re[l], ssm_c_im[l], ssm_d[l], seg_len)
    zeros = jnp.zeros((nb * SSM_SEGMENTS, N_SSM_GROUPS * SSM_STATE), F32)
    tl = _row_tile(seg_len, 32)
    end_re, end_im = _ssm_scan(proj_p, prm, zeros, zeros, seq_len=seg_len, tl=tl, nseg=1,
                               emit_y=False, exact_in=False, name="ssm_state_prompt")
    yg_p, fin_re, fin_im = _ssm_scan(proj_p, prm, end_re, end_im, seq_len=seg_len, tl=tl,
                                     nseg=SSM_SEGMENTS, emit_y=True, exact_in=False,
                                     name="ssm_scan_prompt")
    last = SSM_SEGMENTS - 1
    ssm_re_p = fin_re.reshape(nb, SSM_SEGMENTS, N_SSM_GROUPS, SSM_STATE)[:, last]
    ssm_im_p = fin_im.reshape(nb, SSM_SEGMENTS, N_SSM_GROUPS, SSM_STATE)[:, last]

    h0_re = state_ssm_re[l].reshape(ns, -1)
    h0_im = state_ssm_im[l].reshape(ns, -1)
    yg_s, ssm_re_s, ssm_im_s = _ssm_scan(proj_s, prm, h0_re, h0_im, seq_len=dseq, tl=dseq, nseg=1,
                                         emit_y=True, exact_in=True, name="ssm_scan_sample")
    w_glu_b = w_glu[l].astype(BF16)
    ssm_out_p = _glu(yg_p, w_glu_b, tm=tm_p, name="ssm_glu_prompt")
    ssm_out_s = _glu(yg_s, w_glu_b, tm=n_s, name="ssm_glu_sample")

    mix_args = (row2(g_attn), row2(g_ssm), w_out[l].astype(BF16))
    ln1 = (row2(ln1_g), row2(ln1_b))
    x1_p = _mix(attn_p, ssm_out_p, *mix_args, x_p, *ln1, tm=tm_ln, name="mix_out_ln1_prompt")
    x1_s = _mix(attn_s, ssm_out_s, *mix_args, x_s, *ln1, tm=n_s, name="mix_out_ln1_sample")

    mem_rows = mem_prompt.reshape(nb * N_MEM, d)
    mem_k = _matmul(mem_rows, w_mk[l], tm=nb * N_MEM, tn=1024, name="mem_k")
    mem_v = _matmul(mem_rows, w_mv[l], tm=nb * N_MEM, tn=1024, name="mem_v")
    q_p = _matmul(x1_p, w_mq[l], tm=tm_p, tn=1024, name="mem_q_prompt")
    q_s = _matmul(x1_s, w_mq[l], tm=n_s, tn=1024, name="mem_q_sample")
    o_p = _memattn(q_p, mem_k.reshape(nb, N_MEM, d), mem_v.reshape(nb, N_MEM, d),
                   row0=0, n_seq=nb, seq=seq, tq=_row_tile(seq, 512), name="memattn_prompt")
    o_s = _memattn_heads(q_s, cache_mem_k[l], cache_mem_v[l], row0=0, n_seq=ns, seq=dseq,
                         name="memattn_sample")
    x2, x2_rows = _mm_ln(o_p, o_s, w_mo[l].astype(BF16), x1_p, x1_s, row2(ln2_g), row2(ln2_b),
                         name="mem_out_ln2")

    y_p, y_s = _moe(x2, x2_rows, w_r1[l], b_r1[l], w_r2[l], b_r2[l], w_gate[l], w_up[l],
                    w_down[l], row2(ln3_g), row2(ln3_b), n_first=n_p,
                    tm_router=_row_tile(n_s, 256), tc=_row_tile(n_s, 128))

    y_p = y_p.reshape(nb, seq, d)
    y_s = y_s.reshape(ns, dseq, d)
    wp = min(max(w for w, _ in DILATIONS), seq)
    k_p, v_p = _kv_window(proj_p, n_batch=nb, seq=seq, window=wp, tr=_row_tile(wp, 512))
    k_s = proj_s[:, D_ATT:2 * D_ATT].reshape(ns, dseq, ATT_HEADS, ATT_HD)
    v_s = proj_s[:, 2 * D_ATT:3 * D_ATT].reshape(ns, dseq, ATT_HEADS, ATT_HD)
    state = lambda v, b_: v.reshape(1, b_, N_SSM_GROUPS, SSM_STATE)
    return (y_p, y_s, k_p[None], v_p[None], k_s[None], v_s[None],
            state(ssm_re_p, nb), state(ssm_im_p, nb), state(ssm_re_s, ns), state(ssm_im_s, ns),
            mem_k.reshape(1, nb, N_MEM, MEM_HEADS, MEM_HD),
            mem_v.reshape(1, nb, N_MEM, MEM_HEADS, MEM_HD))
```

```python
import functools
import math

import numpy as np
import jax
import jax.numpy as jnp
from jax import lax
from jax.experimental import pallas as pl
from jax.experimental.pallas import tpu as pltpu

F32 = jnp.float32
BF16 = jnp.bfloat16
I32 = jnp.int32

D_MODEL = 2048
PAST_LEN = 8192
D_ATT = D_MODEL // 2
ATT_HEADS = 8
ATT_HD = D_ATT // ATT_HEADS
DILATIONS = ((128, 1), (512, 4), (2048, 16))
D_SSM = D_MODEL - D_ATT
SSM_GROUP_CH = 16
N_SSM_GROUPS = D_SSM // SSM_GROUP_CH
SSM_STATE = 64
N_MEM = 256
MEM_HEADS = 4
MEM_HD = D_MODEL // MEM_HEADS
N_EXPERT_GROUPS = 4
EXPERTS_PER_GROUP = 8
N_EXPERTS = N_EXPERT_GROUPS * EXPERTS_PER_GROUP
D_EXPERT = D_MODEL // 4
DEPTH = 1
DEEPNORM_ALPHA = (2.0 * DEPTH) ** 0.25
LN_EPS = 1e-5
RMS_EPS = 1e-6

LANES = 128
SUBLANES = 8
ROW_CHUNKS = D_MODEL // LANES
ROW_PITCH = ROW_CHUNKS + 1
Q_BLOCK = 128
ATTN_GROUP = 8
SSM_LANE_TILE = 128
SSM_GROUPS_PER_TILE = SSM_LANE_TILE // SSM_GROUP_CH
SSM_STATES_PER_TILE = SSM_GROUPS_PER_TILE * SSM_STATE
SSM_SEGMENTS = 8
MOE_TILE = 256
GATHER_SLOTS = 3
DISPATCH_SLOTS = 3
ROW_SPLIT = 2
ROUTER_LANES = 128
NEG_INF = float("-inf")


def _cparams(semantics, vmem_mib):
    return pltpu.CompilerParams(dimension_semantics=semantics,
                                vmem_limit_bytes=int(vmem_mib) << 20)


def _layernorm(y, g, b):
    mu = jnp.mean(y, axis=-1, keepdims=True)
    yc = y - mu
    var = jnp.mean(yc * yc, axis=-1, keepdims=True)
    return yc * lax.rsqrt(var + LN_EPS) * g + b


def _rmsnorm(v, g):
    return v * lax.rsqrt(jnp.mean(v * v, axis=-1, keepdims=True) + RMS_EPS) * g


def _dot(a, b):
    return jnp.dot(a, b, preferred_element_type=F32)


def _dot_nt(a, b):
    return lax.dot_general(a, b, (((1,), (1,)), ((), ())), preferred_element_type=F32)


def _mm_body(x_ref, w_ref, o_ref, wb_s):
    @pl.when(pl.program_id(1) == 0)
    def _():
        wb_s[...] = w_ref[...].astype(BF16)

    o_ref[...] = _dot(x_ref[...].astype(BF16), wb_s[...]).astype(o_ref.dtype)


def _matmul(x, w, *, tm, tn, name):
    m, k = x.shape
    n = w.shape[1]
    vmem = (2 * (tm * k * 4 + k * tn * 4 + tm * tn * 4) + k * tn * 2 + tm * k * 2) / 2**20 + 8
    return pl.pallas_call(
        _mm_body,
        out_shape=jax.ShapeDtypeStruct((m, n), F32),
        grid=(n // tn, m // tm),
        in_specs=[pl.BlockSpec((tm, k), lambda j, i: (i, 0)),
                  pl.BlockSpec((k, tn), lambda j, i: (0, j))],
        out_specs=pl.BlockSpec((tm, tn), lambda j, i: (i, j)),
        scratch_shapes=[pltpu.VMEM((k, tn), BF16)],
        compiler_params=_cparams(("parallel", "arbitrary"), vmem),
        name=name,
    )(x, w)


def _kv_window_body(k_ref, v_ref, ko_ref, vo_ref):
    for h in range(ATT_HEADS):
        sl = slice(h * ATT_HD, (h + 1) * ATT_HD)
        ko_ref[:, h, :] = k_ref[:, sl]
        vo_ref[:, h, :] = v_ref[:, sl]


def _kv_window(proj, *, n_batch, seq, window, tr):
    assert window % tr == 0 and seq % tr == 0
    per, first = seq // tr, (seq - window) // tr
    col = lambda c: pl.BlockSpec((tr, D_ATT), lambda b, t: (b * per + first + t, c))
    out_spec = pl.BlockSpec((None, tr, ATT_HEADS, ATT_HD), lambda b, t: (b, t, 0, 0))
    out_shape = jax.ShapeDtypeStruct((n_batch, window, ATT_HEADS, ATT_HD), F32)
    return pl.pallas_call(
        _kv_window_body,
        out_shape=[out_shape, out_shape],
        grid=(n_batch, window // tr),
        in_specs=[col(1), col(2)],
        out_specs=[out_spec, out_spec],
        compiler_params=_cparams(("parallel", "parallel"), 8 * tr * D_ATT * 4 / 2**20 + 8),
        name="kv_window",
    )(proj, proj)


def _attn_prompt_body(q_ref, k_ref, v_ref, o_ref, kt_s, va_s, on_s, lse_s, *, seq, dilations):
    scale = ATT_HD ** -0.5
    nblk = seq // Q_BLOCK
    qi = lax.broadcasted_iota(I32, (Q_BLOCK, Q_BLOCK), 0)
    kj = lax.broadcasted_iota(I32, (Q_BLOCK, Q_BLOCK), 1)
    cur_ok = kj <= qi
    prev_ok = kj >= qi
    va_s[:, :, ATT_HD:] = jnp.ones((nblk, Q_BLOCK, ATT_HD), BF16)

    for br, (_, d) in enumerate(dilations):
        span = d * Q_BLOCK
        nb = seq // span

        def stream_rows(t, d=d, span=span, nb=nb):
            r = t // nb
            ib = t % nb
            return r, ib, pl.ds(r + ib * span, Q_BLOCK, stride=d)

        def prep(g, carry, stream_rows=stream_rows):
            loaded = []
            for j in range(ATTN_GROUP):
                t = g * ATTN_GROUP + j
                _, _, rows = stream_rows(t)
                loaded.append((t, k_ref[rows, :], v_ref[rows, :]))
            for t, kk, vv in loaded:
                kt_s[t] = jnp.transpose(kk).astype(BF16)
                va_s[t, :, 0:ATT_HD] = vv.astype(BF16)
            return carry

        lax.fori_loop(0, nblk // ATTN_GROUP, prep, 0)

        def group(g, carry, br=br, nb=nb, stream_rows=stream_rows):
            scores = []
            for j in range(ATTN_GROUP):
                t = g * ATTN_GROUP + j
                r, ib, rows = stream_rows(t)
                tp = jnp.maximum(t - 1, r * nb)
                q = (q_ref[rows, :] * scale).astype(BF16)
                s = _dot(q, jnp.concatenate([kt_s[tp], kt_s[t]], axis=1))
                scores.append((t, tp, ib, rows, s))
            probs = []
            for t, tp, ib, rows, s in scores:
                ok = jnp.concatenate([jnp.logical_and(prev_ok, ib > 0), cur_ok], axis=1)
                s = jnp.where(ok, s, NEG_INF)
                m = jnp.max(s, axis=-1, keepdims=True)
                probs.append((t, tp, rows, m, jnp.exp(s - m).astype(BF16)))
            outs = [(rows, m, _dot(p, jnp.concatenate([va_s[tp], va_s[t]], axis=0)))
                    for t, tp, rows, m, p in probs]
            for rows, m, al in outs:
                l = al[:, ATT_HD:]
                on_s[br, rows, :] = al[:, :ATT_HD] / l
                lse_s[br, rows, :] = m + jnp.log(l)
            return carry

        lax.fori_loop(0, nblk // ATTN_GROUP, group, 0)

    chunk = 256
    nbr = len(dilations)

    def merge(c, carry):
        rows = pl.ds(pl.multiple_of(c * chunk, chunk), chunk)
        ls = [lse_s[b, rows, :] for b in range(nbr)]
        mx = functools.reduce(jnp.maximum, ls)
        es = [jnp.exp(li - mx) for li in ls]
        num = sum(es[b] * on_s[b, rows, :] for b in range(nbr))
        o_ref[rows, :] = num / sum(es)
        return carry

    lax.fori_loop(0, seq // chunk, merge, 0)


def _attn_prompt(proj, *, n_batch, seq, dilations=DILATIONS):
    for w, d in dilations:
        assert w // d == Q_BLOCK and seq % (d * Q_BLOCK) == 0
    nbr = len(dilations)
    nblk = seq // Q_BLOCK
    assert nblk % ATTN_GROUP == 0
    blk = lambda off: pl.BlockSpec((seq, ATT_HD), lambda b, h, off=off: (b, off + h))
    vmem = ((4 * 2 + 2 * nbr) * seq * ATT_HD * 4 + 3 * seq * ATT_HD * 2) / 2**20 + 8
    return pl.pallas_call(
        functools.partial(_attn_prompt_body, seq=seq, dilations=dilations),
        out_shape=jax.ShapeDtypeStruct((n_batch * seq, D_ATT), F32),
        grid=(n_batch, ATT_HEADS),
        in_specs=[blk(0), blk(ATT_HEADS), blk(2 * ATT_HEADS)],
        out_specs=pl.BlockSpec((seq, ATT_HD), lambda b, h: (b, h)),
        scratch_shapes=[pltpu.VMEM((nblk, ATT_HD, Q_BLOCK), BF16),
                        pltpu.VMEM((nblk, Q_BLOCK, 2 * ATT_HD), BF16),
                        pltpu.VMEM((nbr, seq, ATT_HD), F32),
                        pltpu.VMEM((nbr, seq, ATT_HD), F32)],
        compiler_params=_cparams(("parallel", "parallel"), vmem),
        name="attn_prompt",
    )(proj, proj, proj)


def _sample_key_multiplicity(n_new, n_cache, past_len, dilations):
    d_max = max(d for _, d in dilations)
    tail = max(w for w, d in dilations if d != d_max)
    assert past_len % d_max == 0 and n_cache % d_max == 0 and n_new <= d_max // 2
    assert tail % d_max == 0 and tail <= n_cache
    half = d_max // 2
    n_grid = (n_cache - tail) // d_max
    kv_start = past_len - n_cache
    grid_rows = (np.arange(n_grid)[:, None] * d_max + np.arange(half)[None, :]).reshape(-1)
    tail_rows = n_cache - tail + np.arange(tail)
    new_rows = n_cache + np.arange(n_new)
    qpos = past_len + np.arange(n_new)

    def mult(rows):
        kpos = kv_start + rows
        delta = qpos[:, None] - kpos[None, :]
        c = np.zeros(delta.shape, np.float32)
        for w, d in dilations:
            c += ((delta >= 0) & (delta <= w) & (delta % d == 0) & (kpos[None, :] >= kv_start))
        return c

    fetched = np.zeros(n_cache + n_new, bool)
    fetched[grid_rows] = True
    fetched[tail_rows] = True
    fetched[new_rows] = True
    assert not mult(np.nonzero(~fetched)[0]).any()
    return mult(grid_rows), mult(tail_rows), mult(new_rows), n_grid, tail, half, d_max


def _attn_sample_body(q_ref, kn_ref, vn_ref, kg_ref, kt_ref, vg_ref, vt_ref,
                      cg_ref, ct_ref, cn_ref, o_ref):
    scale = ATT_HD ** -0.5
    heads = lambda ref: jnp.concatenate(
        [ref[:, h * ATT_HD:(h + 1) * ATT_HD] for h in range(ATT_HEADS)], axis=0)
    q = (heads(q_ref) * scale).astype(BF16)
    kn = heads(kn_ref).astype(BF16)
    vn = heads(vn_ref).astype(BF16)
    flat = lambda ref: ref[...].reshape(-1, ATT_HD).astype(BF16)
    cg, ct, cn = cg_ref[...], ct_ref[...], cn_ref[...]
    sg = jnp.where(cg > 0, _dot_nt(q, flat(kg_ref)), NEG_INF)
    st = jnp.where(ct > 0, _dot_nt(q, flat(kt_ref)), NEG_INF)
    sn = jnp.where(cn > 0, _dot_nt(q, kn), NEG_INF)
    m = jnp.maximum(jnp.maximum(jnp.max(sg, axis=-1, keepdims=True),
                                jnp.max(st, axis=-1, keepdims=True)),
                    jnp.max(sn, axis=-1, keepdims=True))
    pg = cg * jnp.exp(sg - m)
    pt = ct * jnp.exp(st - m)
    pn = cn * jnp.exp(sn - m)
    l = (jnp.sum(pg, axis=-1, keepdims=True) + jnp.sum(pt, axis=-1, keepdims=True)
         + jnp.sum(pn, axis=-1, keepdims=True))
    acc = (_dot(pg.astype(BF16), flat(vg_ref)) + _dot(pt.astype(BF16), flat(vt_ref))
           + _dot(pn.astype(BF16), vn))
    out = acc / l
    n_new = q_ref.shape[0]
    for h in range(ATT_HEADS):
        o_ref[:, h * ATT_HD:(h + 1) * ATT_HD] = out[h * n_new:(h + 1) * n_new, :]


def _attn_sample(proj, win_k, win_v, *, row0, n_seq, n_new, past_len=PAST_LEN,
                 dilations=DILATIONS):
    n_cache = win_k.shape[1]
    cg, ct, cn, n_grid, tail, half, d_max = _sample_key_multiplicity(
        n_new, n_cache, past_len, dilations)
    assert row0 % n_new == 0 and n_new % SUBLANES == 0 and n_cache % tail == 0
    eye = np.eye(ATT_HEADS, dtype=np.float32)
    key_major = lambda c: np.einsum("tk,hg->htkg", c, eye).reshape(ATT_HEADS * n_new, -1)
    head_major = lambda c: np.einsum("tk,hg->htgk", c, eye).reshape(ATT_HEADS * n_new, -1)
    cg, ct, cn = key_major(cg), key_major(ct), head_major(cn)
    rb = row0 // n_new
    n_groups = n_cache // d_max
    kgv = win_k.reshape(n_seq, n_groups, d_max, ATT_HEADS, ATT_HD)
    vgv = win_v.reshape(n_seq, n_groups, d_max, ATT_HEADS, ATT_HD)
    ktv = win_k.reshape(n_seq, n_cache // tail, tail, ATT_HEADS, ATT_HD)
    vtv = win_v.reshape(n_seq, n_cache // tail, tail, ATT_HEADS, ATT_HD)
    new = lambda off: pl.BlockSpec((n_new, D_ATT), lambda b, off=off: (rb + b, off))
    grid_spec = pl.BlockSpec((None, n_grid, half, ATT_HEADS, ATT_HD), lambda b: (b, 0, 0, 0, 0))
    tail_spec = pl.BlockSpec((None, None, tail, ATT_HEADS, ATT_HD),
                             lambda b: (b, n_cache // tail - 1, 0, 0, 0))
    const = lambda a: pl.BlockSpec(a.shape, lambda b: (0, 0))
    vmem = (2 * 2 * (n_grid * half + tail) * D_ATT * 4 + 4 * cg.size * 4 * 3) / 2**20 + 12
    return pl.pallas_call(
        _attn_sample_body,
        out_shape=jax.ShapeDtypeStruct((n_seq * n_new, D_ATT), F32),
        grid=(n_seq,),
        in_specs=[new(0), new(1), new(2), grid_spec, tail_spec, grid_spec, tail_spec,
                  const(cg), const(ct), const(cn)],
        out_specs=pl.BlockSpec((n_new, D_ATT), lambda b: (b, 0)),
        compiler_params=_cparams(("parallel",), vmem),
        name="attn_sample",
    )(proj, proj, proj, kgv, ktv, vgv, vtv, jnp.asarray(cg), jnp.asarray(ct), jnp.asarray(cn))


def _gelu_tanh(x):
    return 0.5 * x * (1.0 + jnp.tanh(math.sqrt(2.0 / math.pi) * (x + 0.044715 * (x * x * x))))


def _ssm_body(u_ref, bb_ref, cst_ref, a_ref, ap_ref, d_ref, hre_ref, him_ref, *rest,
              tl, npar, seq_len, nseg, emit_y, exact_in):
    if emit_y:
        y_ref, fre_ref, fim_ref, h_s = rest
    else:
        fre_ref, fim_ref, h_s = rest
    ns = SSM_STATES_PER_TILE
    c = pl.program_id(1)
    ngrp = npar // SUBLANES

    def step_rows(i, g):
        return pl.ds(c * tl + i + g * SUBLANES * seq_len, SUBLANES, stride=seq_len)

    @pl.when(c == 0)
    def _init():
        if nseg == 1:
            h_s[0] = hre_ref[...]
            h_s[1] = him_ref[...]
        else:
            pr, pi = ap_ref[0:1, :], ap_ref[1:2, :]
            for b in range(npar // nseg):
                sr = jnp.zeros((1, ns), F32)
                si = jnp.zeros((1, ns), F32)
                for j in range(nseg):
                    row = b * nseg + j
                    h_s[0, row:row + 1, :] = sr
                    h_s[1, row:row + 1, :] = si
                    er, ei = hre_ref[row:row + 1, :], him_ref[row:row + 1, :]
                    sr, si = pr * sr - pi * si + er, pr * si + pi * sr + ei

    ar = jnp.broadcast_to(a_ref[0:1, :], (SUBLANES, ns))
    ai = jnp.broadcast_to(a_ref[1:2, :], (SUBLANES, ns))
    us = [jnp.concatenate([u_ref[step_rows(i, g), :] for i in range(tl)], axis=0)
          for g in range(ngrp)]
    if exact_in:
        xs = [jnp.dot(u, bb_ref[...], precision=lax.Precision.HIGHEST, preferred_element_type=F32)
              for u in us]
    else:
        xs = [_dot(u.astype(BF16), bb_ref[...]) for u in us]

    hs = []
    for g in range(ngrp):
        gs = slice(g * SUBLANES, (g + 1) * SUBLANES)
        hr, hi = h_s[0, gs, :], h_s[1, gs, :]
        states = []
        for i in range(tl):
            xr = xs[g][i * SUBLANES:(i + 1) * SUBLANES, 0:ns]
            xi = xs[g][i * SUBLANES:(i + 1) * SUBLANES, ns:2 * ns]
            hr, hi = ar * hr - ai * hi + xr, ar * hi + ai * hr + xi
            if emit_y:
                states.append(jnp.concatenate([hr, hi], axis=1))
        h_s[0, gs, :] = hr
        h_s[1, gs, :] = hi
        if emit_y:
            hs.append(jnp.concatenate(states, axis=0))

    if emit_y:
        for g in range(ngrp):
            y = _gelu_tanh(_dot(hs[g].astype(BF16), cst_ref[...]) + d_ref[...] * us[g])
            for i in range(tl):
                y_ref[step_rows(i, g), :] = y[i * SUBLANES:(i + 1) * SUBLANES, :]

    @pl.when(c == pl.num_programs(1) - 1)
    def _fin():
        fre_ref[...] = h_s[0]
        fim_ref[...] = h_s[1]


def _ssm_scan(proj, prm, hin_re, hin_im, *, seq_len, tl, nseg, emit_y, exact_in, name):
    rows = proj.shape[0]
    npar = rows // seq_len
    assert npar % SUBLANES == 0 and seq_len % tl == 0
    ns = SSM_STATES_PER_TILE
    nk = D_SSM // SSM_LANE_TILE
    col0 = (proj.shape[1] - D_SSM) // SSM_LANE_TILE
    bb = prm["bb_f32"] if exact_in else prm["bb_bf16"]
    in_specs = [
        pl.BlockSpec((rows, SSM_LANE_TILE), lambda k, c: (0, col0 + k)),
        pl.BlockSpec((None, SSM_LANE_TILE, 2 * ns), lambda k, c: (k, 0, 0)),
        pl.BlockSpec((None, 2 * ns, SSM_LANE_TILE), lambda k, c: (k, 0, 0)),
        pl.BlockSpec((None, 2, ns), lambda k, c: (k, 0, 0)),
        pl.BlockSpec((None, 2, ns), lambda k, c: (k, 0, 0)),
        pl.BlockSpec((1, SSM_LANE_TILE), lambda k, c: (0, k)),
        pl.BlockSpec((npar, ns), lambda k, c: (0, k)),
        pl.BlockSpec((npar, ns), lambda k, c: (0, k)),
    ]
    state_shape = jax.ShapeDtypeStruct((npar, nk * ns), F32)
    state_spec = pl.BlockSpec((npar, ns), lambda k, c: (0, k))
    out_shape = [state_shape, state_shape]
    out_specs = [state_spec, state_spec]
    if emit_y:
        out_shape = [jax.ShapeDtypeStruct((rows, D_SSM), F32)] + out_shape
        out_specs = [pl.BlockSpec((rows, SSM_LANE_TILE), lambda k, c: (0, k))] + out_specs
    vmem = (4 * rows * SSM_LANE_TILE * 4 + tl * npar * 2 * ns * 4) / 2**20 + 16
    return pl.pallas_call(
        functools.partial(_ssm_body, tl=tl, npar=npar, seq_len=seq_len, nseg=nseg, emit_y=emit_y,
                          exact_in=exact_in),
        out_shape=out_shape,
        grid=(nk, seq_len // tl),
        in_specs=in_specs,
        out_specs=out_specs,
        scratch_shapes=[pltpu.VMEM((2, npar, ns), F32)],
        compiler_params=_cparams(("parallel", "arbitrary"), vmem),
        name=name,
    )(proj, bb, prm["cst"], prm["a"], prm["apow"], prm["d"], hin_re, hin_im)


def _ssm_params(lam_re, lam_im, log_dt, b_re, b_im, c_re, c_im, d_skip, seg_len):
    g, p, c = N_SSM_GROUPS, SSM_STATE, SSM_GROUP_CH
    nk, gt = g // SSM_GROUPS_PER_TILE, SSM_GROUPS_PER_TILE
    dt = jnp.exp(log_dt.astype(F32))[:, None]
    lr, li = lam_re.astype(F32), lam_im.astype(F32)
    mag = jnp.exp(lr * dt)
    a_re, a_im = mag * jnp.cos(li * dt), mag * jnp.sin(li * dt)
    magp = jnp.exp(lr * dt * seg_len)
    p_re, p_im = magp * jnp.cos(li * dt * seg_len), magp * jnp.sin(li * dt * seg_len)
    den = lr * lr + li * li
    nr, ni = a_re - 1.0, a_im
    f_re, f_im = (nr * lr + ni * li) / den, (ni * lr - nr * li) / den
    br, bi = b_re.astype(F32), b_im.astype(F32)
    bb_re = f_re[..., None] * br - f_im[..., None] * bi
    bb_im = f_re[..., None] * bi + f_im[..., None] * br
    eye = jnp.eye(gt, dtype=F32)

    def pack_b(m):
        return jnp.einsum("kgpc,gh->kgchp", m.reshape(nk, gt, p, c), eye).reshape(nk, gt * c, gt * p)

    def pack_c(m):
        return jnp.einsum("kgcp,gh->kgphc", m.reshape(nk, gt, c, p), eye).reshape(nk, gt * p, gt * c)

    bb = jnp.concatenate([pack_b(bb_re), pack_b(bb_im)], axis=2)
    cst = jnp.concatenate([pack_c(c_re.astype(F32)), -pack_c(c_im.astype(F32))], axis=1)
    tile = lambda v: v.reshape(nk, 1, gt * p)
    return {
        "bb_f32": bb, "bb_bf16": bb.astype(BF16), "cst": cst.astype(BF16),
        "a": jnp.concatenate([tile(a_re), tile(a_im)], axis=1),
        "apow": jnp.concatenate([tile(p_re), tile(p_im)], axis=1),
        "d": d_skip.astype(F32).reshape(1, g * c),
    }


def _glu_body(y_ref, w_ref, o_ref):
    yg = y_ref[...]
    z = _dot(yg.astype(BF16), w_ref[...])
    o_ref[...] = yg * (1.0 / (1.0 + jnp.exp(-z)))


def _glu(yg, w, *, tm, name):
    m, n = yg.shape
    return pl.pallas_call(
        _glu_body,
        out_shape=jax.ShapeDtypeStruct((m, n), F32),
        grid=(m // tm,),
        in_specs=[pl.BlockSpec((tm, n), lambda i: (i, 0)), pl.BlockSpec((n, n), lambda i: (0, 0))],
        out_specs=pl.BlockSpec((tm, n), lambda i: (i, 0)),
        compiler_params=_cparams(("parallel",), 4 * tm * n * 4 / 2**20 + 12),
        name=name,
    )(yg, w)


def _mix_body(attn_ref, ssm_ref, ga_ref, gs_ref, w_ref, x_ref, g_ref, b_ref, o_ref):
    half = attn_ref.shape[0] // ROW_SPLIT
    for r in range(ROW_SPLIT):
        rows = slice(r * half, (r + 1) * half)
        a = _rmsnorm(attn_ref[rows, :], ga_ref[...]).astype(BF16)
        s = _rmsnorm(ssm_ref[rows, :], gs_ref[...]).astype(BF16)
        mix = _dot(jnp.concatenate([a, s], axis=1), w_ref[...])
        o_ref[rows, :] = _layernorm(DEEPNORM_ALPHA * x_ref[rows, :] + mix, g_ref[...], b_ref[...])


def _mix(attn, ssm, ga, gs, w, x, g, b, *, tm, name):
    m = x.shape[0]
    row = lambda n: pl.BlockSpec((tm, n), lambda i: (i, 0))
    const = lambda a: pl.BlockSpec(a.shape, lambda i: (0, 0))
    return pl.pallas_call(
        _mix_body,
        out_shape=jax.ShapeDtypeStruct((m, D_MODEL), F32),
        grid=(m // tm,),
        in_specs=[row(D_ATT), row(D_SSM), const(ga), const(gs), const(w), row(D_MODEL),
                  const(g), const(b)],
        out_specs=row(D_MODEL),
        compiler_params=_cparams(("parallel",), 6 * tm * D_MODEL * 4 / 2**20 + 24),
        name=name,
    )(attn, ssm, ga, gs, w, x, g, b)


def _store_gatherable(o_ref, y):
    rows = y.shape[0]
    for c in range(ROW_CHUNKS):
        o_ref[pl.ds(c, rows, stride=ROW_PITCH), :] = y[:, c * LANES:(c + 1) * LANES]
    for c in range(ROW_CHUNKS, ROW_PITCH):
        o_ref[pl.ds(c, rows, stride=ROW_PITCH), :] = jnp.zeros((rows, LANES), F32)


def _load_gathered(buf, rows):
    return jnp.concatenate([buf[pl.ds(c, rows, stride=ROW_PITCH), :] for c in range(ROW_CHUNKS)],
                           axis=1)


def _start_row_gather(src_hbm, idx, buf, r, sem):
    pltpu.make_async_copy(src_hbm.at[pl.ds(idx * ROW_PITCH, ROW_CHUNKS), :],
                          buf.at[pl.ds(r * ROW_PITCH, ROW_CHUNKS), :], sem).start()


def _wait_row_gathers(buf, other, rows, sem):
    span = pl.ds(0, rows * ROW_CHUNKS)
    pltpu.make_async_copy(other.at[span, :], buf.at[span, :], sem).wait()


def _mm_ln_body(a1_ref, a2_ref, w_ref, x1_ref, x2_ref, g_ref, b_ref, wr_ref, br_ref,
                o_ref, rows_ref, sel_ref, wts_ref, cnt_ref, run_s, *, tiles1):
    first = pl.program_id(0) < tiles1
    half = a1_ref.shape[0] // ROW_SPLIT
    outs = []
    for r in range(ROW_SPLIT):
        rows = slice(r * half, (r + 1) * half)
        a = jnp.where(first, a1_ref[rows, :], a2_ref[rows, :])
        x = jnp.where(first, x1_ref[rows, :], x2_ref[rows, :])
        y = _dot(a.astype(BF16), w_ref[...])
        out = _layernorm(DEEPNORM_ALPHA * x + y, g_ref[...], b_ref[...])
        o_ref[rows, :] = out
        outs.append(out)
    out = jnp.concatenate(outs, axis=0)
    _store_gatherable(rows_ref, out)
    _route_tile(out, wr_ref, br_ref, sel_ref, wts_ref, cnt_ref, run_s)


def _mm_ln(a1, a2, w, x1, x2, g, b, w_r, b_r, *, name):
    tm = a2.shape[0]
    assert a1.shape[0] % tm == 0
    tiles1 = a1.shape[0] // tm
    m = a1.shape[0] + tm
    row1 = lambda n: pl.BlockSpec((tm, n), lambda i: (jnp.minimum(i, tiles1 - 1), 0))
    row2 = lambda n: pl.BlockSpec((tm, n), lambda i: (0, 0))
    const = lambda v: pl.BlockSpec(v.shape, lambda i: (0, 0))
    lanes = pl.BlockSpec((tm, ROUTER_LANES), lambda i: (i, 0))
    return pl.pallas_call(
        functools.partial(_mm_ln_body, tiles1=tiles1),
        out_shape=[jax.ShapeDtypeStruct((m, D_MODEL), F32),
                   jax.ShapeDtypeStruct((m * ROW_PITCH, LANES), F32),
                   jax.ShapeDtypeStruct((m, ROUTER_LANES), I32),
                   jax.ShapeDtypeStruct((m, ROUTER_LANES), F32),
                   jax.ShapeDtypeStruct((1, ROUTER_LANES), I32)],
        grid=(tiles1 + 1,),
        in_specs=[row1(a1.shape[1]), row2(a2.shape[1]), const(w), row1(D_MODEL), row2(D_MODEL),
                  const(g), const(b), const(w_r), const(b_r)],
        out_specs=[pl.BlockSpec((tm, D_MODEL), lambda i: (i, 0)),
                   pl.BlockSpec((tm * ROW_PITCH, LANES), lambda i: (i, 0)),
                   lanes, lanes, pl.BlockSpec((1, ROUTER_LANES), lambda i: (0, 0))],
        scratch_shapes=[pltpu.VMEM((1, ROUTER_LANES), F32)],
        compiler_params=_cparams(("arbitrary",), 12 * tm * D_MODEL * 4 / 2**20 + 24),
        name=name,
    )(a1, a2, w, x1, x2, g, b, w_r, b_r)


def _memattn_body(q_ref, k_ref, v_ref, o_ref):
    scale = MEM_HD ** -0.5
    for h in range(MEM_HEADS):
        sl = slice(h * MEM_HD, (h + 1) * MEM_HD)
        s = _dot_nt(q_ref[:, sl].astype(BF16), k_ref[:, sl].astype(BF16)) * scale
        m = jnp.max(s, axis=-1, keepdims=True)
        p = jnp.exp(s - m)
        l = jnp.sum(p, axis=-1, keepdims=True)
        o_ref[:, sl] = _dot(p.astype(BF16), v_ref[:, sl].astype(BF16)) / l


def _memattn(q, mem_k, mem_v, *, row0, n_seq, seq, tq, name):
    assert seq % tq == 0 and row0 % tq == 0
    nq = seq // tq
    rb = row0 // tq
    mem_spec = pl.BlockSpec((None, N_MEM, D_MODEL), lambda b, i: (b, 0, 0))
    return pl.pallas_call(
        _memattn_body,
        out_shape=jax.ShapeDtypeStruct((n_seq * seq, D_MODEL), F32),
        grid=(n_seq, nq),
        in_specs=[pl.BlockSpec((tq, D_MODEL), lambda b, i: (rb + b * nq + i, 0)),
                  mem_spec, mem_spec],
        out_specs=pl.BlockSpec((tq, D_MODEL), lambda b, i: (b * nq + i, 0)),
        compiler_params=_cparams(("parallel", "parallel"),
                                 4 * (tq + N_MEM) * D_MODEL * 4 / 2**20 + 8),
        name=name,
    )(q, mem_k, mem_v)


def _memattn_heads_body(q_ref, k_ref, v_ref, c_ref, o_ref):
    scale = MEM_HD ** -0.5
    tq = q_ref.shape[0]
    q = jnp.concatenate([q_ref[:, h * MEM_HD:(h + 1) * MEM_HD] for h in range(MEM_HEADS)], axis=0)
    k = k_ref[...].reshape(N_MEM * MEM_HEADS, MEM_HD).astype(BF16)
    v = v_ref[...].reshape(N_MEM * MEM_HEADS, MEM_HD).astype(BF16)
    s = jnp.where(c_ref[...] > 0, _dot_nt(q.astype(BF16), k) * scale, NEG_INF)
    m = jnp.max(s, axis=-1, keepdims=True)
    p = jnp.exp(s - m)
    l = jnp.sum(p, axis=-1, keepdims=True)
    o = _dot(p.astype(BF16), v) / l
    for h in range(MEM_HEADS):
        o_ref[:, h * MEM_HD:(h + 1) * MEM_HD] = o[h * tq:(h + 1) * tq, :]


def _memattn_heads(q, mem_k, mem_v, *, row0, n_seq, seq, name):
    assert row0 % seq == 0 and seq % SUBLANES == 0
    rb = row0 // seq
    same_head = np.kron(np.eye(MEM_HEADS, dtype=np.float32), np.ones((seq, 1), np.float32))
    same_head = np.tile(same_head, (1, N_MEM))
    mem_spec = pl.BlockSpec((None, N_MEM, MEM_HEADS, MEM_HD), lambda b: (b, 0, 0, 0))
    return pl.pallas_call(
        _memattn_heads_body,
        out_shape=jax.ShapeDtypeStruct((n_seq * seq, D_MODEL), F32),
        grid=(n_seq,),
        in_specs=[pl.BlockSpec((seq, D_MODEL), lambda b: (rb + b, 0)), mem_spec, mem_spec,
                  pl.BlockSpec(same_head.shape, lambda b: (0, 0))],
        out_specs=pl.BlockSpec((seq, D_MODEL), lambda b: (b, 0)),
        compiler_params=_cparams(("parallel",), 8 * N_MEM * D_MODEL * 4 / 2**20 + 8),
        name=name,
    )(q, mem_k, mem_v, jnp.asarray(same_head))


def _route_tile(x, w_ref, b_ref, sel_ref, wts_ref, cnt_ref, run_s):
    tm = x.shape[0]

    @pl.when(pl.program_id(0) == 0)
    def _():
        run_s[...] = jnp.zeros_like(run_s)

    ng, epg = N_EXPERT_GROUPS, EXPERTS_PER_GROUP
    x_hi = x.astype(BF16)
    x_lo = (x - x_hi.astype(F32)).astype(BF16)
    parts = _dot(x_hi, w_ref[...]) + _dot(x_lo, w_ref[...])
    logits = parts + pltpu.roll(parts, shift=ROUTER_LANES // 2, axis=1) + b_ref[...]
    lane = lax.broadcasted_iota(I32, (tm, ROUTER_LANES), 1)
    big = ROUTER_LANES

    def first_argmax(vals):
        mx = jnp.max(vals, axis=-1, keepdims=True)
        idx = jnp.min(jnp.where(vals == mx, lane, big), axis=-1, keepdims=True)
        return mx, idx

    gl = jnp.where(lane < ng, logits, NEG_INF)
    gmax, gsel = first_argmax(gl)
    g_w = 1.0 / jnp.sum(jnp.exp(gl - gmax), axis=-1, keepdims=True)
    lo = ng + gsel * epg
    el = jnp.where(jnp.logical_and(lane >= lo, lane < lo + epg), logits, NEG_INF)
    v1, i1 = first_argmax(el)
    v2, i2 = first_argmax(jnp.where(lane == i1, NEG_INF, el))
    e21 = jnp.exp(v2 - v1)
    w1 = g_w / (1.0 + e21)
    w2 = g_w * e21 / (1.0 + e21)

    onehot = jnp.logical_or(lane == i1, lane == i2)
    r = lax.broadcasted_iota(I32, (tm, tm), 0)
    cc = lax.broadcasted_iota(I32, (tm, tm), 1)
    tri = (cc < r).astype(BF16)
    before = _dot(tri, onehot.astype(BF16)) + run_s[...]
    rank1 = jnp.sum(jnp.where(lane == i1, before, 0.0), axis=-1, keepdims=True).astype(I32)
    rank2 = jnp.sum(jnp.where(lane == i2, before, 0.0), axis=-1, keepdims=True).astype(I32)
    run_s[...] = run_s[...] + jnp.sum(onehot.astype(F32), axis=0, keepdims=True)

    sel = jnp.where(lane == 0, i1 - ng, jnp.where(lane == 1, i2 - ng,
                    jnp.where(lane == 2, rank1, jnp.where(lane == 3, rank2, 0))))
    sel_ref[...] = sel
    wts_ref[...] = jnp.where(lane == 0, w1, jnp.where(lane == 1, w2, 0.0))
    cnt_ref[...] = run_s[...].astype(I32)


def _router_weights(w_r1, b_r1, w_r2, b_r2):
    ng, ne = N_EXPERT_GROUPS, N_EXPERTS
    half = ROUTER_LANES // 2
    assert ng + ne <= half
    w_r = jnp.concatenate([w_r1, w_r2.reshape(D_MODEL, ne),
                           jnp.zeros((D_MODEL, half - ng - ne), F32)], axis=1)
    w_hi = w_r.astype(BF16)
    w_lo = (w_r - w_hi.astype(F32)).astype(BF16)
    b_r = jnp.concatenate([b_r1, b_r2.reshape(ne), jnp.zeros((half - ng - ne,), F32)])
    return jnp.concatenate([w_hi, w_lo], axis=1), jnp.concatenate([b_r, b_r]).reshape(1, ROUTER_LANES)


def _dispatch_body(eid_ref, rank_ref, off_ref, pad0_ref, npad_ref, nact_ref, x_hbm, xs_hbm, xbuf,
                   zero_s, in_sem, out_sem, pad_sem, *, td, n_tiles):
    i = pl.program_id(0)
    n = pl.num_programs(0)
    tile_rows = MOE_TILE * ROW_PITCH
    in_rows = td * ROW_PITCH

    def row_copy(src, dst_row, s):
        return pltpu.make_async_copy(src, xs_hbm.at[pl.ds(dst_row * ROW_PITCH, ROW_PITCH), :], s)

    def tile_load(t):
        s = t % DISPATCH_SLOTS
        return pltpu.make_async_copy(x_hbm.at[pl.ds(t * in_rows, in_rows), :], xbuf.at[s],
                                     in_sem.at[s])

    def wait_scatter(par):
        for _ in range(2):
            pltpu.make_async_copy(xbuf.at[0], xs_hbm.at[pl.ds(0, in_rows), :],
                                  out_sem.at[par]).wait()

    @pl.when(i == 0)
    def _():
        for t in range(DISPATCH_SLOTS - 1):
            @pl.when(t < n)
            def _(t=t):
                tile_load(t).start()
        zero_s[...] = jnp.zeros_like(zero_s)

        def pad_copies(e):
            out = []
            for bit in reversed(range(MOE_TILE.bit_length() - 1)):
                rows = (1 << bit) * ROW_PITCH
                first = (pad0_ref[e] + (npad_ref[e] >> (bit + 1) << (bit + 1))) * ROW_PITCH
                out.append((jnp.bitwise_and(npad_ref[e] >> bit, 1) == 1, pltpu.make_async_copy(
                    zero_s.at[pl.ds(0, rows), :], xs_hbm.at[pl.ds(first, rows), :], pad_sem.at[0])))
            return out

        for e in range(N_EXPERTS):
            for on, cp in pad_copies(e):
                pl.when(on)(cp.start)
        for e in range(N_EXPERTS):
            for on, cp in pad_copies(e):
                pl.when(on)(cp.wait)

        def zero_tile(t, carry):
            parts = [pltpu.make_async_copy(
                zero_s.at[pl.ds(0, MOE_TILE), :],
                xs_hbm.at[pl.ds(t * tile_rows + j * MOE_TILE, MOE_TILE), :],
                pad_sem.at[0]) for j in range(ROW_PITCH)]
            for cp in parts:
                cp.start()
            for cp in parts:
                cp.wait()
            return carry

        lax.fori_loop(nact_ref[0], n_tiles, zero_tile, 0)

    slot = i % DISPATCH_SLOTS
    par = i % 2
    tile_load(i).wait()
    base = i * td * 2
    for r in range(td):
        for k in range(2):
            j = base + 2 * r + k
            row_copy(xbuf.at[slot, pl.ds(r * ROW_PITCH, ROW_PITCH), :],
                     off_ref[eid_ref[j]] + rank_ref[j], out_sem.at[par]).start()

    @pl.when(i > 0)
    def _():
        wait_scatter(1 - par)

    @pl.when(i + DISPATCH_SLOTS - 1 < n)
    def _():
        tile_load(i + DISPATCH_SLOTS - 1).start()

    @pl.when(i == n - 1)
    def _():
        wait_scatter(par)


def _moe_dispatch(x_rows, eid, rank, row_off, pad_start, pad_count, nact, *, td, n_tiles):
    n = x_rows.shape[0] // ROW_PITCH
    grid_spec = pltpu.PrefetchScalarGridSpec(
        num_scalar_prefetch=6,
        grid=(n // td,),
        in_specs=[pl.BlockSpec(memory_space=pl.ANY)],
        out_specs=pl.BlockSpec(memory_space=pl.ANY),
        scratch_shapes=[pltpu.VMEM((DISPATCH_SLOTS, td * ROW_PITCH, LANES), F32),
                        pltpu.VMEM((MOE_TILE // 2 * ROW_PITCH, LANES), F32),
                        pltpu.SemaphoreType.DMA((DISPATCH_SLOTS,)),
                        pltpu.SemaphoreType.DMA((2,)),
                        pltpu.SemaphoreType.DMA((1,))],
    )
    return pl.pallas_call(
        functools.partial(_dispatch_body, td=td, n_tiles=n_tiles),
        out_shape=jax.ShapeDtypeStruct((n_tiles * MOE_TILE * ROW_PITCH, LANES), F32),
        grid_spec=grid_spec,
        compiler_params=_cparams(("arbitrary",), 16),
        name="moe_dispatch",
    )(eid, rank, row_off, pad_start, pad_count, nact, x_rows)


def _moe_body(te_ref, ord_ref, nxt_ref, nact_ref, x_ref, wg_hbm, wu_hbm, wd_hbm, o_ref,
              wg_f, wu_f, wd_f, wsem, wg_s, wu_s, wd_s):
    i = pl.program_id(0)
    nact = nact_ref[0]
    tm = MOE_TILE

    def weight_copies(expert, ws):
        return [pltpu.make_async_copy(hbm.at[expert], stage.at[ws], wsem.at[ws])
                for hbm, stage in ((wg_hbm, wg_f), (wu_hbm, wu_f), (wd_hbm, wd_f))]

    def tile_step():
        prev = te_ref[jnp.maximum(i - 1, 0)]

        @pl.when(jnp.logical_or(i == 0, te_ref[i] != prev))
        def _():
            ws = ord_ref[i] % 2
            for cp in weight_copies(te_ref[i], ws):
                cp.wait()
            wg_s[...] = wg_f[ws].astype(BF16)
            wu_s[...] = wu_f[ws].astype(BF16)
            wd_s[...] = wd_f[ws].astype(BF16)

            @pl.when(nxt_ref[i] >= 0)
            def _():
                for cp in weight_copies(nxt_ref[i], 1 - ws):
                    cp.start(priority=1)

        x = _load_gathered(x_ref, tm).astype(BF16)
        hg = _dot(x, wg_s[...])
        hu = _dot(x, wu_s[...])
        h = hg * (1.0 / (1.0 + jnp.exp(-hg))) * hu
        _store_gatherable(o_ref, _dot(h.astype(BF16), wd_s[...]))

    @pl.when(i == 0)
    def _():
        for cp in weight_copies(te_ref[0], 0):
            cp.start(priority=1)

    @pl.when(i < nact)
    def _():
        tile_step()

    @pl.when(i >= nact)
    def _():
        o_ref[...] = jnp.zeros_like(o_ref)


def _moe_experts(x_sorted, w_gate, w_up, w_down, tile_expert, tile_ord, tile_next, nact, *,
                 n_tiles):
    tm = MOE_TILE
    in_map = lambda i, te, od, nx, n: (jnp.minimum(i, n[0] - 1), 0)
    any_spec = pl.BlockSpec(memory_space=pl.ANY)
    grid_spec = pltpu.PrefetchScalarGridSpec(
        num_scalar_prefetch=4,
        grid=(n_tiles,),
        in_specs=[pl.BlockSpec((tm * ROW_PITCH, LANES), in_map), any_spec, any_spec, any_spec],
        out_specs=pl.BlockSpec((tm * ROW_PITCH, LANES), lambda i, te, od, nx, n: (i, 0)),
        scratch_shapes=[pltpu.VMEM((2, D_MODEL, D_EXPERT), F32),
                        pltpu.VMEM((2, D_MODEL, D_EXPERT), F32),
                        pltpu.VMEM((2, D_EXPERT, D_MODEL), F32),
                        pltpu.SemaphoreType.DMA((2,)),
                        pltpu.VMEM((D_MODEL, D_EXPERT), BF16),
                        pltpu.VMEM((D_MODEL, D_EXPERT), BF16),
                        pltpu.VMEM((D_EXPERT, D_MODEL), BF16)],
    )
    return pl.pallas_call(
        _moe_body,
        out_shape=jax.ShapeDtypeStruct((n_tiles * tm * ROW_PITCH, LANES), F32),
        grid_spec=grid_spec,
        compiler_params=_cparams(("arbitrary",), 48),
        name="moe_experts",
    )(tile_expert, tile_ord, tile_next, nact, x_sorted, w_gate, w_up, w_down)


def _combine_body(eid_ref, rank_ref, off_ref, ys_hbm, wts_ref, x_ref, g_ref, b_ref, o1_ref, o2_ref,
                  buf, sem, *, tc, tiles1):
    i = pl.program_id(0)
    n = pl.num_programs(0)
    slot = i % GATHER_SLOTS
    ahead = GATHER_SLOTS - 1

    def issue_gather(tile, slot_):
        base = tile * tc * 2
        for r in range(tc):
            for k in range(2):
                j = base + 2 * r + k
                _start_row_gather(ys_hbm, off_ref[eid_ref[j]] + rank_ref[j], buf.at[slot_, k], r,
                                  sem.at[slot_])

    @pl.when(i == 0)
    def _():
        for t in range(ahead):
            @pl.when(t < n)
            def _(t=t):
                issue_gather(t, t)

    for k in range(2):
        _wait_row_gathers(buf.at[slot, k], buf.at[(i + 1) % GATHER_SLOTS, k], tc, sem.at[slot])

    @pl.when(i + ahead < n)
    def _():
        issue_gather(i + ahead, (i + ahead) % GATHER_SLOTS)

    w = wts_ref[...]
    moe = (w[:, 0:1] * _load_gathered(buf.at[slot, 0], tc)
           + w[:, 1:2] * _load_gathered(buf.at[slot, 1], tc))
    out = _layernorm(DEEPNORM_ALPHA * x_ref[...] + moe, g_ref[...], b_ref[...])

    @pl.when(i < tiles1)
    def _():
        o1_ref[...] = out

    @pl.when(i >= tiles1)
    def _():
        o2_ref[...] = out


def _moe_combine(ys, eid, rank, row_off, wts, x, g, b, *, tc, n_first):
    m = x.shape[0]
    assert n_first % tc == 0 and (m - n_first) % tc == 0
    tiles1 = n_first // tc
    grid_spec = pltpu.PrefetchScalarGridSpec(
        num_scalar_prefetch=3,
        grid=(m // tc,),
        in_specs=[pl.BlockSpec(memory_space=pl.ANY),
                  pl.BlockSpec((tc, ROUTER_LANES), lambda i, *_: (i, 0)),
                  pl.BlockSpec((tc, D_MODEL), lambda i, *_: (i, 0)),
                  pl.BlockSpec((1, D_MODEL), lambda i, *_: (0, 0)),
                  pl.BlockSpec((1, D_MODEL), lambda i, *_: (0, 0))],
        out_specs=[pl.BlockSpec((tc, D_MODEL), lambda i, *_: (jnp.minimum(i, tiles1 - 1), 0)),
                   pl.BlockSpec((tc, D_MODEL), lambda i, *_: (jnp.maximum(i - tiles1, 0), 0))],
        scratch_shapes=[pltpu.VMEM((GATHER_SLOTS, 2, tc * ROW_PITCH, LANES), F32),
                        pltpu.SemaphoreType.DMA((GATHER_SLOTS,))],
    )
    return pl.pallas_call(
        functools.partial(_combine_body, tc=tc, tiles1=tiles1),
        out_shape=[jax.ShapeDtypeStruct((n_first, D_MODEL), F32),
                   jax.ShapeDtypeStruct((m - n_first, D_MODEL), F32)],
        grid_spec=grid_spec,
        compiler_params=_cparams(("arbitrary",), 16 * tc * D_MODEL * 4 / 2**20 + 8),
        name="moe_combine_ln3",
    )(eid, rank, row_off, ys, wts, x, g, b)


def _moe(x, x_rows, sel, wts, cnt, w_gate, w_up, w_down, g, b, *, n_first, tc):
    n = x.shape[0]
    ng, ne = N_EXPERT_GROUPS, N_EXPERTS

    tm = MOE_TILE
    n_tiles = (2 * n) // tm + ne
    counts = cnt[0, ng:ng + ne]
    tiles_per = (counts + tm - 1) // tm
    tile_end = jnp.cumsum(tiles_per)
    row_off = (tile_end - tiles_per) * tm
    nact = tile_end[-1]
    a_eid, a_rank = sel[:, 0:2].reshape(-1), sel[:, 2:4].reshape(-1)
    row_off = row_off.astype(I32)
    tile_ids = jnp.minimum(jnp.arange(n_tiles, dtype=I32), nact - 1)
    tile_expert = jnp.sum((tile_end[None, :] <= tile_ids[:, None]).astype(I32), axis=1)
    used = tiles_per > 0
    eid = jnp.arange(ne, dtype=I32)
    ordinal = jnp.cumsum(used.astype(I32)) - 1
    later = jnp.where(jnp.logical_and(used[None, :], eid[None, :] > eid[:, None]), eid[None, :], ne)
    nxt = jnp.min(later, axis=1)
    nxt = jnp.where(nxt == ne, -1, nxt)

    nact = nact.reshape(1).astype(I32)
    x_sorted = _moe_dispatch(x_rows, a_eid, a_rank, row_off, (row_off + counts).astype(I32),
                             (tiles_per * tm - counts).astype(I32), nact, td=tc, n_tiles=n_tiles)
    ys = _moe_experts(x_sorted, w_gate, w_up, w_down, tile_expert, ordinal[tile_expert],
                      nxt[tile_expert], nact, n_tiles=n_tiles)
    return _moe_combine(ys, a_eid, a_rank, row_off, wts, x, g, b, tc=tc, n_first=n_first)


def _row_tile(m, cap):
    best = SUBLANES
    for t in range(SUBLANES, cap + 1, SUBLANES):
        if m % t == 0:
            best = t
    return best


def kernel(x_prompt, x_sample, cache_win_k, cache_win_v, state_ssm_re, state_ssm_im, cache_mem_k, cache_mem_v, mem_prompt, w_in, ssm_lam_re, ssm_lam_im, ssm_log_dt, ssm_b_re, ssm_b_im, ssm_c_re, ssm_c_im, ssm_d, w_glu, g_attn, g_ssm, w_out, ln1_g, ln1_b, w_mq, w_mk, w_mv, w_mo, ln2_g, ln2_b, w_r1, b_r1, w_r2, b_r2, w_gate, w_up, w_down, ln3_g, ln3_b):
    nb, seq, d = x_prompt.shape
    ns, dseq, _ = x_sample.shape
    n_p, n_s = nb * seq, ns * dseq
    n = n_p + n_s
    l = 0
    row2 = lambda v: v[l].reshape(1, -1)

    x_p, x_s = x_prompt.reshape(n_p, d), x_sample.reshape(n_s, d)
    tm_p = _row_tile(n_p, 1024)
    tm_ln = _row_tile(n_p, 512)
    assert n_p % n_s == 0 and n_s % SUBLANES == 0

    proj_p = _matmul(x_p, w_in[l], tm=tm_p, tn=1024, name="proj_in_prompt")
    proj_s = _matmul(x_s, w_in[l], tm=n_s, tn=1024, name="proj_in_sample")

    attn_p = _attn_prompt(proj_p, n_batch=nb, seq=seq)
    attn_s = _attn_sample(proj_s, cache_win_k[l], cache_win_v[l], row0=0, n_seq=ns, n_new=dseq)

    seg_len = seq // SSM_SEGMENTS
    prm = _ssm_params(ssm_lam_re[l], ssm_lam_im[l], ssm_log_dt[l], ssm_b_re[l], ssm_b_im[l],
                      ssm_c_re[l], ssm_c_im[l], ssm_d[l], seg_len)
    zeros = jnp.zeros((nb * SSM_SEGMENTS, N_SSM_GROUPS * SSM_STATE), F32)
    tl = _row_tile(seg_len, 32)
    end_re, end_im = _ssm_scan(proj_p, prm, zeros, zeros, seq_len=seg_len, tl=tl, nseg=1,
                               emit_y=False, exact_in=False, name="ssm_state_prompt")
    yg_p, fin_re, fin_im = _ssm_scan(proj_p, prm, end_re, end_im, seq_len=seg_len, tl=tl,
                                     nseg=SSM_SEGMENTS, emit_y=True, exact_in=False,
                                     name="ssm_scan_prompt")
    last = SSM_SEGMENTS - 1
    ssm_re_p = fin_re.reshape(nb, SSM_SEGMENTS, N_SSM_GROUPS, SSM_STATE)[:, last]
    ssm_im_p = fin_im.reshape(nb, SSM_SEGMENTS, N_SSM_GROUPS, SSM_STATE)[:, last]

    h0_re = state_ssm_re[l].reshape(ns, -1)
    h0_im = state_ssm_im[l].reshape(ns, -1)
    yg_s, ssm_re_s, ssm_im_s = _ssm_scan(proj_s, prm, h0_re, h0_im, seq_len=dseq, tl=dseq, nseg=1,
                                         emit_y=True, exact_in=True, name="ssm_scan_sample")
    w_glu_b = w_glu[l].astype(BF16)
    ssm_out_p = _glu(yg_p, w_glu_b, tm=tm_p, name="ssm_glu_prompt")
    ssm_out_s = _glu(yg_s, w_glu_b, tm=n_s, name="ssm_glu_sample")

    mix_args = (row2(g_attn), row2(g_ssm), w_out[l].astype(BF16))
    ln1 = (row2(ln1_g), row2(ln1_b))
    x1_p = _mix(attn_p, ssm_out_p, *mix_args, x_p, *ln1, tm=tm_ln, name="mix_out_ln1_prompt")
    x1_s = _mix(attn_s, ssm_out_s, *mix_args, x_s, *ln1, tm=n_s, name="mix_out_ln1_sample")

    mem_rows = mem_prompt.reshape(nb * N_MEM, d)
    mem_k = _matmul(mem_rows, w_mk[l], tm=nb * N_MEM, tn=1024, name="mem_k")
    mem_v = _matmul(mem_rows, w_mv[l], tm=nb * N_MEM, tn=1024, name="mem_v")
    q_p = _matmul(x1_p, w_mq[l], tm=tm_p, tn=1024, name="mem_q_prompt")
    q_s = _matmul(x1_s, w_mq[l], tm=n_s, tn=1024, name="mem_q_sample")
    o_p = _memattn(q_p, mem_k.reshape(nb, N_MEM, d), mem_v.reshape(nb, N_MEM, d),
                   row0=0, n_seq=nb, seq=seq, tq=_row_tile(seq, 512), name="memattn_prompt")
    o_s = _memattn_heads(q_s, cache_mem_k[l], cache_mem_v[l], row0=0, n_seq=ns, seq=dseq,
                         name="memattn_sample")
    w_r, b_r = _router_weights(w_r1[l], b_r1[l], w_r2[l], b_r2[l])
    x2, x2_rows, sel, wts, cnt = _mm_ln(o_p, o_s, w_mo[l].astype(BF16), x1_p, x1_s, row2(ln2_g),
                                        row2(ln2_b), w_r, b_r, name="mem_out_ln2_route")

    y_p, y_s = _moe(x2, x2_rows, sel, wts, cnt, w_gate[l], w_up[l], w_down[l], row2(ln3_g),
                    row2(ln3_b), n_first=n_p, tc=_row_tile(n_s, 128))

    y_p = y_p.reshape(nb, seq, d)
    y_s = y_s.reshape(ns, dseq, d)
    wp = min(max(w for w, _ in DILATIONS), seq)
    k_p, v_p = _kv_window(proj_p, n_batch=nb, seq=seq, window=wp, tr=_row_tile(wp, 512))
    k_s = proj_s[:, D_ATT:2 * D_ATT].reshape(ns, dseq, ATT_HEADS, ATT_HD)
    v_s = proj_s[:, 2 * D_ATT:3 * D_ATT].reshape(ns, dseq, ATT_HEADS, ATT_HD)
    state = lambda v, b_: v.reshape(1, b_, N_SSM_GROUPS, SSM_STATE)
    return (y_p, y_s, k_p[None], v_p[None], k_s[None], v_s[None],
            state(ssm_re_p, nb), state(ssm_im_p, nb), state(ssm_re_s, ns), state(ssm_im_s, ns),
            mem_k.reshape(1, nb, N_MEM, MEM_HEADS, MEM_HD),
            mem_v.reshape(1, nb, N_MEM, MEM_HEADS, MEM_HD))
```

```python
import functools
import math

import numpy as np
import jax
import jax.numpy as jnp
from jax import lax
from jax.experimental import pallas as pl
from jax.experimental.pallas import tpu as pltpu

F32 = jnp.float32
BF16 = jnp.bfloat16
I32 = jnp.int32

D_MODEL = 2048
PAST_LEN = 8192
D_ATT = D_MODEL // 2
ATT_HEADS = 8
ATT_HD = D_ATT // ATT_HEADS
DILATIONS = ((128, 1), (512, 4), (2048, 16))
D_SSM = D_MODEL - D_ATT
SSM_GROUP_CH = 16
N_SSM_GROUPS = D_SSM // SSM_GROUP_CH
SSM_STATE = 64
N_MEM = 256
MEM_HEADS = 4
MEM_HD = D_MODEL // MEM_HEADS
N_EXPERT_GROUPS = 4
EXPERTS_PER_GROUP = 8
N_EXPERTS = N_EXPERT_GROUPS * EXPERTS_PER_GROUP
D_EXPERT = D_MODEL // 4
DEPTH = 1
DEEPNORM_ALPHA = (2.0 * DEPTH) ** 0.25
LN_EPS = 1e-5
RMS_EPS = 1e-6

LANES = 128
SUBLANES = 8
ROW_CHUNKS = D_MODEL // LANES
ROW_PITCH = ROW_CHUNKS + 1
Q_BLOCK = 128
ATTN_GROUP = 8
SSM_LANE_TILE = 128
SSM_GROUPS_PER_TILE = SSM_LANE_TILE // SSM_GROUP_CH
SSM_STATES_PER_TILE = SSM_GROUPS_PER_TILE * SSM_STATE
SSM_SEGMENTS = 8
MOE_TILE = 256
GATHER_SLOTS = 3
DISPATCH_SLOTS = 3
ROW_SPLIT = 2
ROUTER_LANES = 128
NEG_INF = float("-inf")


def _cparams(semantics, vmem_mib):
    return pltpu.CompilerParams(dimension_semantics=semantics,
                                vmem_limit_bytes=int(vmem_mib) << 20)


def _layernorm(y, g, b):
    mu = jnp.mean(y, axis=-1, keepdims=True)
    yc = y - mu
    var = jnp.mean(yc * yc, axis=-1, keepdims=True)
    return yc * lax.rsqrt(var + LN_EPS) * g + b


def _rmsnorm(v, g):
    return v * lax.rsqrt(jnp.mean(v * v, axis=-1, keepdims=True) + RMS_EPS) * g


def _dot(a, b):
    return jnp.dot(a, b, preferred_element_type=F32)


def _dot_nt(a, b):
    return lax.dot_general(a, b, (((1,), (1,)), ((), ())), preferred_element_type=F32)


def _mm_body(x_ref, w_ref, o_ref, wb_s):
    @pl.when(pl.program_id(1) == 0)
    def _():
        wb_s[...] = w_ref[...].astype(BF16)

    o_ref[...] = _dot(x_ref[...].astype(BF16), wb_s[...]).astype(o_ref.dtype)


def _matmul(x, w, *, tm, tn, name, out_dtype=F32):
    m, k = x.shape
    n = w.shape[1]
    vmem = (2 * (tm * k * 4 + k * tn * 4 + tm * tn * 4) + k * tn * 2 + tm * k * 2) / 2**20 + 8
    return pl.pallas_call(
        _mm_body,
        out_shape=jax.ShapeDtypeStruct((m, n), out_dtype),
        grid=(n // tn, m // tm),
        in_specs=[pl.BlockSpec((tm, k), lambda j, i: (i, 0)),
                  pl.BlockSpec((k, tn), lambda j, i: (0, j))],
        out_specs=pl.BlockSpec((tm, tn), lambda j, i: (i, j)),
        scratch_shapes=[pltpu.VMEM((k, tn), BF16)],
        compiler_params=_cparams(("parallel", "arbitrary"), vmem),
        name=name,
    )(x, w)


def _kv_window_body(k_ref, v_ref, ko_ref, vo_ref):
    for h in range(ATT_HEADS):
        sl = slice(h * ATT_HD, (h + 1) * ATT_HD)
        ko_ref[:, h, :] = k_ref[:, sl]
        vo_ref[:, h, :] = v_ref[:, sl]


def _kv_window(proj, *, n_batch, seq, window, tr):
    assert window % tr == 0 and seq % tr == 0
    per, first = seq // tr, (seq - window) // tr
    col = lambda c: pl.BlockSpec((tr, D_ATT), lambda b, t: (b * per + first + t, c))
    out_spec = pl.BlockSpec((None, tr, ATT_HEADS, ATT_HD), lambda b, t: (b, t, 0, 0))
    out_shape = jax.ShapeDtypeStruct((n_batch, window, ATT_HEADS, ATT_HD), F32)
    return pl.pallas_call(
        _kv_window_body,
        out_shape=[out_shape, out_shape],
        grid=(n_batch, window // tr),
        in_specs=[col(1), col(2)],
        out_specs=[out_spec, out_spec],
        compiler_params=_cparams(("parallel", "parallel"), 8 * tr * D_ATT * 4 / 2**20 + 8),
        name="kv_window",
    )(proj, proj)


def _attn_prompt_body(q_ref, k_ref, v_ref, o_ref, kt_s, va_s, on_s, lse_s, *, seq, dilations):
    scale = ATT_HD ** -0.5
    nblk = seq // Q_BLOCK
    qi = lax.broadcasted_iota(I32, (Q_BLOCK, Q_BLOCK), 0)
    kj = lax.broadcasted_iota(I32, (Q_BLOCK, Q_BLOCK), 1)
    cur_ok = kj <= qi
    prev_ok = kj >= qi
    va_s[:, :, ATT_HD:] = jnp.ones((nblk, Q_BLOCK, ATT_HD), BF16)

    for br, (_, d) in enumerate(dilations):
        span = d * Q_BLOCK
        nb = seq // span

        def stream_rows(t, d=d, span=span, nb=nb):
            r = t // nb
            ib = t % nb
            return r, ib, pl.ds(r + ib * span, Q_BLOCK, stride=d)

        def prep(g, carry, stream_rows=stream_rows):
            loaded = []
            for j in range(ATTN_GROUP):
                t = g * ATTN_GROUP + j
                _, _, rows = stream_rows(t)
                loaded.append((t, k_ref[rows, :], v_ref[rows, :]))
            for t, kk, vv in loaded:
                kt_s[t] = jnp.transpose(kk).astype(BF16)
                va_s[t, :, 0:ATT_HD] = vv.astype(BF16)
            return carry

        lax.fori_loop(0, nblk // ATTN_GROUP, prep, 0)

        def group(g, carry, br=br, nb=nb, stream_rows=stream_rows):
            scores = []
            for j in range(ATTN_GROUP):
                t = g * ATTN_GROUP + j
                r, ib, rows = stream_rows(t)
                tp = jnp.maximum(t - 1, r * nb)
                q = (q_ref[rows, :] * scale).astype(BF16)
                s = _dot(q, jnp.concatenate([kt_s[tp], kt_s[t]], axis=1))
                scores.append((t, tp, ib, rows, s))
            probs = []
            for t, tp, ib, rows, s in scores:
                ok = jnp.concatenate([jnp.logical_and(prev_ok, ib > 0), cur_ok], axis=1)
                s = jnp.where(ok, s, NEG_INF)
                m = jnp.max(s, axis=-1, keepdims=True)
                probs.append((t, tp, rows, m, jnp.exp(s - m).astype(BF16)))
            outs = [(rows, m, _dot(p, jnp.concatenate([va_s[tp], va_s[t]], axis=0)))
                    for t, tp, rows, m, p in probs]
            for rows, m, al in outs:
                l = al[:, ATT_HD:]
                on_s[br, rows, :] = al[:, :ATT_HD] / l
                lse_s[br, rows, :] = m + jnp.log(l)
            return carry

        lax.fori_loop(0, nblk // ATTN_GROUP, group, 0)

    chunk = 256
    nbr = len(dilations)

    def merge(c, carry):
        rows = pl.ds(pl.multiple_of(c * chunk, chunk), chunk)
        ls = [lse_s[b, rows, :] for b in range(nbr)]
        mx = functools.reduce(jnp.maximum, ls)
        es = [jnp.exp(li - mx) for li in ls]
        num = sum(es[b] * on_s[b, rows, :] for b in range(nbr))
        o_ref[rows, :] = num / sum(es)
        return carry

    lax.fori_loop(0, seq // chunk, merge, 0)


def _attn_prompt(proj, *, n_batch, seq, dilations=DILATIONS):
    for w, d in dilations:
        assert w // d == Q_BLOCK and seq % (d * Q_BLOCK) == 0
    nbr = len(dilations)
    nblk = seq // Q_BLOCK
    assert nblk % ATTN_GROUP == 0
    blk = lambda off: pl.BlockSpec((seq, ATT_HD), lambda b, h, off=off: (b, off + h))
    vmem = ((4 * 2 + 2 * nbr) * seq * ATT_HD * 4 + 3 * seq * ATT_HD * 2) / 2**20 + 8
    return pl.pallas_call(
        functools.partial(_attn_prompt_body, seq=seq, dilations=dilations),
        out_shape=jax.ShapeDtypeStruct((n_batch * seq, D_ATT), F32),
        grid=(n_batch, ATT_HEADS),
        in_specs=[blk(0), blk(ATT_HEADS), blk(2 * ATT_HEADS)],
        out_specs=pl.BlockSpec((seq, ATT_HD), lambda b, h: (b, h)),
        scratch_shapes=[pltpu.VMEM((nblk, ATT_HD, Q_BLOCK), BF16),
                        pltpu.VMEM((nblk, Q_BLOCK, 2 * ATT_HD), BF16),
                        pltpu.VMEM((nbr, seq, ATT_HD), F32),
                        pltpu.VMEM((nbr, seq, ATT_HD), F32)],
        compiler_params=_cparams(("parallel", "parallel"), vmem),
        name="attn_prompt",
    )(proj, proj, proj)


def _sample_key_multiplicity(n_new, n_cache, past_len, dilations):
    d_max = max(d for _, d in dilations)
    tail = max(w for w, d in dilations if d != d_max)
    assert past_len % d_max == 0 and n_cache % d_max == 0 and n_new <= d_max // 2
    assert tail % d_max == 0 and tail <= n_cache
    half = d_max // 2
    n_grid = (n_cache - tail) // d_max
    kv_start = past_len - n_cache
    grid_rows = (np.arange(n_grid)[:, None] * d_max + np.arange(half)[None, :]).reshape(-1)
    tail_rows = n_cache - tail + np.arange(tail)
    new_rows = n_cache + np.arange(n_new)
    qpos = past_len + np.arange(n_new)

    def mult(rows):
        kpos = kv_start + rows
        delta = qpos[:, None] - kpos[None, :]
        c = np.zeros(delta.shape, np.float32)
        for w, d in dilations:
            c += ((delta >= 0) & (delta <= w) & (delta % d == 0) & (kpos[None, :] >= kv_start))
        return c

    fetched = np.zeros(n_cache + n_new, bool)
    fetched[grid_rows] = True
    fetched[tail_rows] = True
    fetched[new_rows] = True
    assert not mult(np.nonzero(~fetched)[0]).any()
    return mult(grid_rows), mult(tail_rows), mult(new_rows), n_grid, tail, half, d_max


def _attn_sample_body(q_ref, kn_ref, vn_ref, kg_ref, kt_ref, vg_ref, vt_ref,
                      cg_ref, ct_ref, cn_ref, o_ref):
    scale = ATT_HD ** -0.5
    heads = lambda ref: jnp.concatenate(
        [ref[:, h * ATT_HD:(h + 1) * ATT_HD] for h in range(ATT_HEADS)], axis=0)
    q = (heads(q_ref) * scale).astype(BF16)
    kn = heads(kn_ref).astype(BF16)
    vn = heads(vn_ref).astype(BF16)
    flat = lambda ref: ref[...].reshape(-1, ATT_HD).astype(BF16)
    cg, ct, cn = cg_ref[...], ct_ref[...], cn_ref[...]
    sg = jnp.where(cg > 0, _dot_nt(q, flat(kg_ref)), NEG_INF)
    st = jnp.where(ct > 0, _dot_nt(q, flat(kt_ref)), NEG_INF)
    sn = jnp.where(cn > 0, _dot_nt(q, kn), NEG_INF)
    m = jnp.maximum(jnp.maximum(jnp.max(sg, axis=-1, keepdims=True),
                                jnp.max(st, axis=-1, keepdims=True)),
                    jnp.max(sn, axis=-1, keepdims=True))
    pg = cg * jnp.exp(sg - m)
    pt = ct * jnp.exp(st - m)
    pn = cn * jnp.exp(sn - m)
    l = (jnp.sum(pg, axis=-1, keepdims=True) + jnp.sum(pt, axis=-1, keepdims=True)
         + jnp.sum(pn, axis=-1, keepdims=True))
    acc = (_dot(pg.astype(BF16), flat(vg_ref)) + _dot(pt.astype(BF16), flat(vt_ref))
           + _dot(pn.astype(BF16), vn))
    out = acc / l
    n_new = q_ref.shape[0]
    for h in range(ATT_HEADS):
        o_ref[:, h * ATT_HD:(h + 1) * ATT_HD] = out[h * n_new:(h + 1) * n_new, :]


def _attn_sample(proj, win_k, win_v, *, row0, n_seq, n_new, past_len=PAST_LEN,
                 dilations=DILATIONS):
    n_cache = win_k.shape[1]
    cg, ct, cn, n_grid, tail, half, d_max = _sample_key_multiplicity(
        n_new, n_cache, past_len, dilations)
    assert row0 % n_new == 0 and n_new % SUBLANES == 0 and n_cache % tail == 0
    eye = np.eye(ATT_HEADS, dtype=np.float32)
    key_major = lambda c: np.einsum("tk,hg->htkg", c, eye).reshape(ATT_HEADS * n_new, -1)
    head_major = lambda c: np.einsum("tk,hg->htgk", c, eye).reshape(ATT_HEADS * n_new, -1)
    cg, ct, cn = key_major(cg), key_major(ct), head_major(cn)
    rb = row0 // n_new
    n_groups = n_cache // d_max
    kgv = win_k.reshape(n_seq, n_groups, d_max, ATT_HEADS, ATT_HD)
    vgv = win_v.reshape(n_seq, n_groups, d_max, ATT_HEADS, ATT_HD)
    ktv = win_k.reshape(n_seq, n_cache // tail, tail, ATT_HEADS, ATT_HD)
    vtv = win_v.reshape(n_seq, n_cache // tail, tail, ATT_HEADS, ATT_HD)
    new = lambda off: pl.BlockSpec((n_new, D_ATT), lambda b, off=off: (rb + b, off))
    grid_spec = pl.BlockSpec((None, n_grid, half, ATT_HEADS, ATT_HD), lambda b: (b, 0, 0, 0, 0))
    tail_spec = pl.BlockSpec((None, None, tail, ATT_HEADS, ATT_HD),
                             lambda b: (b, n_cache // tail - 1, 0, 0, 0))
    const = lambda a: pl.BlockSpec(a.shape, lambda b: (0, 0))
    vmem = (2 * 2 * (n_grid * half + tail) * D_ATT * 4 + 4 * cg.size * 4 * 3) / 2**20 + 12
    return pl.pallas_call(
        _attn_sample_body,
        out_shape=jax.ShapeDtypeStruct((n_seq * n_new, D_ATT), F32),
        grid=(n_seq,),
        in_specs=[new(0), new(1), new(2), grid_spec, tail_spec, grid_spec, tail_spec,
                  const(cg), const(ct), const(cn)],
        out_specs=pl.BlockSpec((n_new, D_ATT), lambda b: (b, 0)),
        compiler_params=_cparams(("parallel",), vmem),
        name="attn_sample",
    )(proj, proj, proj, kgv, ktv, vgv, vtv, jnp.asarray(cg), jnp.asarray(ct), jnp.asarray(cn))


def _gelu_tanh(x):
    return 0.5 * x * (1.0 + jnp.tanh(math.sqrt(2.0 / math.pi) * (x + 0.044715 * (x * x * x))))


def _ssm_body(u_ref, bb_ref, cst_ref, a_ref, ap_ref, d_ref, hre_ref, him_ref, *rest,
              tl, npar, seq_len, nseg, emit_y, exact_in):
    if emit_y:
        y_ref, fre_ref, fim_ref, h_s = rest
    else:
        fre_ref, fim_ref, h_s = rest
    ns = SSM_STATES_PER_TILE
    c = pl.program_id(1)
    ngrp = npar // SUBLANES

    def step_rows(i, g):
        return pl.ds(c * tl + i + g * SUBLANES * seq_len, SUBLANES, stride=seq_len)

    @pl.when(c == 0)
    def _init():
        if nseg == 1:
            h_s[0] = hre_ref[...]
            h_s[1] = him_ref[...]
        else:
            pr, pi = ap_ref[0:1, :], ap_ref[1:2, :]
            for b in range(npar // nseg):
                sr = jnp.zeros((1, ns), F32)
                si = jnp.zeros((1, ns), F32)
                for j in range(nseg):
                    row = b * nseg + j
                    h_s[0, row:row + 1, :] = sr
                    h_s[1, row:row + 1, :] = si
                    er, ei = hre_ref[row:row + 1, :], him_ref[row:row + 1, :]
                    sr, si = pr * sr - pi * si + er, pr * si + pi * sr + ei

    ar = jnp.broadcast_to(a_ref[0:1, :], (SUBLANES, ns))
    ai = jnp.broadcast_to(a_ref[1:2, :], (SUBLANES, ns))
    us = [jnp.concatenate([u_ref[step_rows(i, g), :] for i in range(tl)], axis=0)
          for g in range(ngrp)]
    if exact_in:
        xs = [jnp.dot(u, bb_ref[...], precision=lax.Precision.HIGHEST, preferred_element_type=F32)
              for u in us]
    else:
        xs = [_dot(u.astype(BF16), bb_ref[...]) for u in us]

    hs = []
    for g in range(ngrp):
        gs = slice(g * SUBLANES, (g + 1) * SUBLANES)
        hr, hi = h_s[0, gs, :], h_s[1, gs, :]
        states = []
        for i in range(tl):
            xr = xs[g][i * SUBLANES:(i + 1) * SUBLANES, 0:ns]
            xi = xs[g][i * SUBLANES:(i + 1) * SUBLANES, ns:2 * ns]
            hr, hi = ar * hr - ai * hi + xr, ar * hi + ai * hr + xi
            if emit_y:
                states.append(jnp.concatenate([hr, hi], axis=1))
        h_s[0, gs, :] = hr
        h_s[1, gs, :] = hi
        if emit_y:
            hs.append(jnp.concatenate(states, axis=0))

    if emit_y:
        for g in range(ngrp):
            y = _gelu_tanh(_dot(hs[g].astype(BF16), cst_ref[...]) + d_ref[...] * us[g])
            for i in range(tl):
                y_ref[step_rows(i, g), :] = y[i * SUBLANES:(i + 1) * SUBLANES, :]

    @pl.when(c == pl.num_programs(1) - 1)
    def _fin():
        fre_ref[...] = h_s[0]
        fim_ref[...] = h_s[1]


def _ssm_scan(proj, prm, hin_re, hin_im, *, seq_len, tl, nseg, emit_y, exact_in, name):
    rows = proj.shape[0]
    npar = rows // seq_len
    assert npar % SUBLANES == 0 and seq_len % tl == 0
    ns = SSM_STATES_PER_TILE
    nk = D_SSM // SSM_LANE_TILE
    col0 = (proj.shape[1] - D_SSM) // SSM_LANE_TILE
    bb = prm["bb_f32"] if exact_in else prm["bb_bf16"]
    in_specs = [
        pl.BlockSpec((rows, SSM_LANE_TILE), lambda k, c: (0, col0 + k)),
        pl.BlockSpec((None, SSM_LANE_TILE, 2 * ns), lambda k, c: (k, 0, 0)),
        pl.BlockSpec((None, 2 * ns, SSM_LANE_TILE), lambda k, c: (k, 0, 0)),
        pl.BlockSpec((None, 2, ns), lambda k, c: (k, 0, 0)),
        pl.BlockSpec((None, 2, ns), lambda k, c: (k, 0, 0)),
        pl.BlockSpec((1, SSM_LANE_TILE), lambda k, c: (0, k)),
        pl.BlockSpec((npar, ns), lambda k, c: (0, k)),
        pl.BlockSpec((npar, ns), lambda k, c: (0, k)),
    ]
    state_shape = jax.ShapeDtypeStruct((npar, nk * ns), F32)
    state_spec = pl.BlockSpec((npar, ns), lambda k, c: (0, k))
    out_shape = [state_shape, state_shape]
    out_specs = [state_spec, state_spec]
    if emit_y:
        out_shape = [jax.ShapeDtypeStruct((rows, D_SSM), F32)] + out_shape
        out_specs = [pl.BlockSpec((rows, SSM_LANE_TILE), lambda k, c: (0, k))] + out_specs
    vmem = (4 * rows * SSM_LANE_TILE * 4 + tl * npar * 2 * ns * 4) / 2**20 + 16
    return pl.pallas_call(
        functools.partial(_ssm_body, tl=tl, npar=npar, seq_len=seq_len, nseg=nseg, emit_y=emit_y,
                          exact_in=exact_in),
        out_shape=out_shape,
        grid=(nk, seq_len // tl),
        in_specs=in_specs,
        out_specs=out_specs,
        scratch_shapes=[pltpu.VMEM((2, npar, ns), F32)],
        compiler_params=_cparams(("parallel", "arbitrary"), vmem),
        name=name,
    )(proj, bb, prm["cst"], prm["a"], prm["apow"], prm["d"], hin_re, hin_im)


def _ssm_params(lam_re, lam_im, log_dt, b_re, b_im, c_re, c_im, d_skip, seg_len):
    g, p, c = N_SSM_GROUPS, SSM_STATE, SSM_GROUP_CH
    nk, gt = g // SSM_GROUPS_PER_TILE, SSM_GROUPS_PER_TILE
    dt = jnp.exp(log_dt.astype(F32))[:, None]
    lr, li = lam_re.astype(F32), lam_im.astype(F32)
    mag = jnp.exp(lr * dt)
    a_re, a_im = mag * jnp.cos(li * dt), mag * jnp.sin(li * dt)
    magp = jnp.exp(lr * dt * seg_len)
    p_re, p_im = magp * jnp.cos(li * dt * seg_len), magp * jnp.sin(li * dt * seg_len)
    den = lr * lr + li * li
    nr, ni = a_re - 1.0, a_im
    f_re, f_im = (nr * lr + ni * li) / den, (ni * lr - nr * li) / den
    br, bi = b_re.astype(F32), b_im.astype(F32)
    bb_re = f_re[..., None] * br - f_im[..., None] * bi
    bb_im = f_re[..., None] * bi + f_im[..., None] * br
    eye = jnp.eye(gt, dtype=F32)

    def pack_b(m):
        return jnp.einsum("kgpc,gh->kgchp", m.reshape(nk, gt, p, c), eye).reshape(nk, gt * c, gt * p)

    def pack_c(m):
        return jnp.einsum("kgcp,gh->kgphc", m.reshape(nk, gt, c, p), eye).reshape(nk, gt * p, gt * c)

    bb = jnp.concatenate([pack_b(bb_re), pack_b(bb_im)], axis=2)
    cst = jnp.concatenate([pack_c(c_re.astype(F32)), -pack_c(c_im.astype(F32))], axis=1)
    tile = lambda v: v.reshape(nk, 1, gt * p)
    return {
        "bb_f32": bb, "bb_bf16": bb.astype(BF16), "cst": cst.astype(BF16),
        "a": jnp.concatenate([tile(a_re), tile(a_im)], axis=1),
        "apow": jnp.concatenate([tile(p_re), tile(p_im)], axis=1),
        "d": d_skip.astype(F32).reshape(1, g * c),
    }


def _glu_body(y_ref, w_ref, o_ref):
    yg = y_ref[...]
    z = _dot(yg.astype(BF16), w_ref[...])
    o_ref[...] = yg * (1.0 / (1.0 + jnp.exp(-z)))


def _glu(yg, w, *, tm, name):
    m, n = yg.shape
    return pl.pallas_call(
        _glu_body,
        out_shape=jax.ShapeDtypeStruct((m, n), F32),
        grid=(m // tm,),
        in_specs=[pl.BlockSpec((tm, n), lambda i: (i, 0)), pl.BlockSpec((n, n), lambda i: (0, 0))],
        out_specs=pl.BlockSpec((tm, n), lambda i: (i, 0)),
        compiler_params=_cparams(("parallel",), 4 * tm * n * 4 / 2**20 + 12),
        name=name,
    )(yg, w)


def _mix_body(attn_ref, ssm_ref, ga_ref, gs_ref, w_ref, x_ref, g_ref, b_ref, o_ref):
    half = attn_ref.shape[0] // ROW_SPLIT
    for r in range(ROW_SPLIT):
        rows = slice(r * half, (r + 1) * half)
        a = _rmsnorm(attn_ref[rows, :], ga_ref[...]).astype(BF16)
        s = _rmsnorm(ssm_ref[rows, :], gs_ref[...]).astype(BF16)
        mix = _dot(jnp.concatenate([a, s], axis=1), w_ref[...])
        o_ref[rows, :] = _layernorm(DEEPNORM_ALPHA * x_ref[rows, :] + mix, g_ref[...], b_ref[...])


def _mix(attn, ssm, ga, gs, w, x, g, b, *, tm, name):
    m = x.shape[0]
    row = lambda n: pl.BlockSpec((tm, n), lambda i: (i, 0))
    const = lambda a: pl.BlockSpec(a.shape, lambda i: (0, 0))
    return pl.pallas_call(
        _mix_body,
        out_shape=jax.ShapeDtypeStruct((m, D_MODEL), F32),
        grid=(m // tm,),
        in_specs=[row(D_ATT), row(D_SSM), const(ga), const(gs), const(w), row(D_MODEL),
                  const(g), const(b)],
        out_specs=row(D_MODEL),
        compiler_params=_cparams(("parallel",), 6 * tm * D_MODEL * 4 / 2**20 + 24),
        name=name,
    )(attn, ssm, ga, gs, w, x, g, b)


def _store_gatherable(o_ref, y):
    rows = y.shape[0]
    for c in range(ROW_CHUNKS):
        o_ref[pl.ds(c, rows, stride=ROW_PITCH), :] = y[:, c * LANES:(c + 1) * LANES]
    for c in range(ROW_CHUNKS, ROW_PITCH):
        o_ref[pl.ds(c, rows, stride=ROW_PITCH), :] = jnp.zeros((rows, LANES), F32)


def _load_gathered(buf, rows):
    return jnp.concatenate([buf[pl.ds(c, rows, stride=ROW_PITCH), :] for c in range(ROW_CHUNKS)],
                           axis=1)


def _start_row_gather(src_hbm, idx, buf, r, sem):
    pltpu.make_async_copy(src_hbm.at[pl.ds(idx * ROW_PITCH, ROW_CHUNKS), :],
                          buf.at[pl.ds(r * ROW_PITCH, ROW_CHUNKS), :], sem).start()


def _wait_row_gathers(buf, other, rows, sem):
    span = pl.ds(0, rows * ROW_CHUNKS)
    pltpu.make_async_copy(other.at[span, :], buf.at[span, :], sem).wait()


def _mm_ln_body(a1_ref, a2_ref, w_ref, x1_ref, x2_ref, g_ref, b_ref, wr_ref, br_ref,
                o_ref, rows_ref, sel_ref, wts_ref, cnt_ref, run_s, *, tiles1):
    first = pl.program_id(0) < tiles1
    half = a1_ref.shape[0] // ROW_SPLIT
    outs = []
    for r in range(ROW_SPLIT):
        rows = slice(r * half, (r + 1) * half)
        a = jnp.where(first, a1_ref[rows, :].astype(BF16), a2_ref[rows, :].astype(BF16))
        x = jnp.where(first, x1_ref[rows, :], x2_ref[rows, :])
        y = _dot(a, w_ref[...])
        out = _layernorm(DEEPNORM_ALPHA * x + y, g_ref[...], b_ref[...])
        o_ref[rows, :] = out
        outs.append(out)
    out = jnp.concatenate(outs, axis=0)
    _store_gatherable(rows_ref, out)
    _route_tile(out, wr_ref, br_ref, sel_ref, wts_ref, cnt_ref, run_s)


def _mm_ln(a1, a2, w, x1, x2, g, b, w_r, b_r, *, name):
    tm = a2.shape[0]
    assert a1.shape[0] % tm == 0
    tiles1 = a1.shape[0] // tm
    m = a1.shape[0] + tm
    row1 = lambda n: pl.BlockSpec((tm, n), lambda i: (jnp.minimum(i, tiles1 - 1), 0))
    row2 = lambda n: pl.BlockSpec((tm, n), lambda i: (0, 0))
    const = lambda v: pl.BlockSpec(v.shape, lambda i: (0, 0))
    lanes = pl.BlockSpec((tm, ROUTER_LANES), lambda i: (i, 0))
    return pl.pallas_call(
        functools.partial(_mm_ln_body, tiles1=tiles1),
        out_shape=[jax.ShapeDtypeStruct((m, D_MODEL), F32),
                   jax.ShapeDtypeStruct((m * ROW_PITCH, LANES), F32),
                   jax.ShapeDtypeStruct((m, ROUTER_LANES), I32),
                   jax.ShapeDtypeStruct((m, ROUTER_LANES), F32),
                   jax.ShapeDtypeStruct((1, ROUTER_LANES), I32)],
        grid=(tiles1 + 1,),
        in_specs=[row1(a1.shape[1]), row2(a2.shape[1]), const(w), row1(D_MODEL), row2(D_MODEL),
                  const(g), const(b), const(w_r), const(b_r)],
        out_specs=[pl.BlockSpec((tm, D_MODEL), lambda i: (i, 0)),
                   pl.BlockSpec((tm * ROW_PITCH, LANES), lambda i: (i, 0)),
                   lanes, lanes, pl.BlockSpec((1, ROUTER_LANES), lambda i: (0, 0))],
        scratch_shapes=[pltpu.VMEM((1, ROUTER_LANES), F32)],
        compiler_params=_cparams(("arbitrary",), 12 * tm * D_MODEL * 4 / 2**20 + 24),
        name=name,
    )(a1, a2, w, x1, x2, g, b, w_r, b_r)


def _memattn_body(q_ref, k_ref, v_ref, o_ref):
    scale = MEM_HD ** -0.5
    for h in range(MEM_HEADS):
        sl = slice(h * MEM_HD, (h + 1) * MEM_HD)
        s = _dot_nt(q_ref[:, sl].astype(BF16), k_ref[:, sl].astype(BF16)) * scale
        m = jnp.max(s, axis=-1, keepdims=True)
        p = jnp.exp(s - m)
        l = jnp.sum(p, axis=-1, keepdims=True)
        o_ref[:, sl] = (_dot(p.astype(BF16), v_ref[:, sl].astype(BF16)) / l).astype(o_ref.dtype)


def _memattn(q, mem_k, mem_v, *, row0, n_seq, seq, tq, name):
    assert seq % tq == 0 and row0 % tq == 0
    nq = seq // tq
    rb = row0 // tq
    mem_spec = pl.BlockSpec((None, N_MEM, D_MODEL), lambda b, i: (b, 0, 0))
    return pl.pallas_call(
        _memattn_body,
        out_shape=jax.ShapeDtypeStruct((n_seq * seq, D_MODEL), q.dtype),
        grid=(n_seq, nq),
        in_specs=[pl.BlockSpec((tq, D_MODEL), lambda b, i: (rb + b * nq + i, 0)),
                  mem_spec, mem_spec],
        out_specs=pl.BlockSpec((tq, D_MODEL), lambda b, i: (b * nq + i, 0)),
        compiler_params=_cparams(("parallel", "parallel"),
                                 4 * (tq + N_MEM) * D_MODEL * 4 / 2**20 + 8),
        name=name,
    )(q, mem_k, mem_v)


def _memattn_heads_body(q_ref, k_ref, v_ref, c_ref, o_ref):
    scale = MEM_HD ** -0.5
    tq = q_ref.shape[0]
    q = jnp.concatenate([q_ref[:, h * MEM_HD:(h + 1) * MEM_HD] for h in range(MEM_HEADS)], axis=0)
    k = k_ref[...].reshape(N_MEM * MEM_HEADS, MEM_HD).astype(BF16)
    v = v_ref[...].reshape(N_MEM * MEM_HEADS, MEM_HD).astype(BF16)
    s = jnp.where(c_ref[...] > 0, _dot_nt(q.astype(BF16), k) * scale, NEG_INF)
    m = jnp.max(s, axis=-1, keepdims=True)
    p = jnp.exp(s - m)
    l = jnp.sum(p, axis=-1, keepdims=True)
    o = _dot(p.astype(BF16), v) / l
    for h in range(MEM_HEADS):
        o_ref[:, h * MEM_HD:(h + 1) * MEM_HD] = o[h * tq:(h + 1) * tq, :]


def _memattn_heads(q, mem_k, mem_v, *, row0, n_seq, seq, name):
    assert row0 % seq == 0 and seq % SUBLANES == 0
    rb = row0 // seq
    same_head = np.kron(np.eye(MEM_HEADS, dtype=np.float32), np.ones((seq, 1), np.float32))
    same_head = np.tile(same_head, (1, N_MEM))
    mem_spec = pl.BlockSpec((None, N_MEM, MEM_HEADS, MEM_HD), lambda b: (b, 0, 0, 0))
    return pl.pallas_call(
        _memattn_heads_body,
        out_shape=jax.ShapeDtypeStruct((n_seq * seq, D_MODEL), F32),
        grid=(n_seq,),
        in_specs=[pl.BlockSpec((seq, D_MODEL), lambda b: (rb + b, 0)), mem_spec, mem_spec,
                  pl.BlockSpec(same_head.shape, lambda b: (0, 0))],
        out_specs=pl.BlockSpec((seq, D_MODEL), lambda b: (b, 0)),
        compiler_params=_cparams(("parallel",), 8 * N_MEM * D_MODEL * 4 / 2**20 + 8),
        name=name,
    )(q, mem_k, mem_v, jnp.asarray(same_head))


def _route_tile(x, w_ref, b_ref, sel_ref, wts_ref, cnt_ref, run_s):
    tm = x.shape[0]

    @pl.when(pl.program_id(0) == 0)
    def _():
        run_s[...] = jnp.zeros_like(run_s)

    ng, epg = N_EXPERT_GROUPS, EXPERTS_PER_GROUP
    x_hi = x.astype(BF16)
    x_lo = (x - x_hi.astype(F32)).astype(BF16)
    parts = _dot(x_hi, w_ref[...]) + _dot(x_lo, w_ref[...])
    logits = parts + pltpu.roll(parts, shift=ROUTER_LANES // 2, axis=1) + b_ref[...]
    lane = lax.broadcasted_iota(I32, (tm, ROUTER_LANES), 1)
    big = ROUTER_LANES

    def first_argmax(vals):
        mx = jnp.max(vals, axis=-1, keepdims=True)
        idx = jnp.min(jnp.where(vals == mx, lane, big), axis=-1, keepdims=True)
        return mx, idx

    gl = jnp.where(lane < ng, logits, NEG_INF)
    gmax, gsel = first_argmax(gl)
    g_w = 1.0 / jnp.sum(jnp.exp(gl - gmax), axis=-1, keepdims=True)
    lo = ng + gsel * epg
    el = jnp.where(jnp.logical_and(lane >= lo, lane < lo + epg), logits, NEG_INF)
    v1, i1 = first_argmax(el)
    v2, i2 = first_argmax(jnp.where(lane == i1, NEG_INF, el))
    e21 = jnp.exp(v2 - v1)
    w1 = g_w / (1.0 + e21)
    w2 = g_w * e21 / (1.0 + e21)

    onehot = jnp.logical_or(lane == i1, lane == i2)
    r = lax.broadcasted_iota(I32, (tm, tm), 0)
    cc = lax.broadcasted_iota(I32, (tm, tm), 1)
    tri = (cc < r).astype(BF16)
    before = _dot(tri, onehot.astype(BF16)) + run_s[...]
    rank1 = jnp.sum(jnp.where(lane == i1, before, 0.0), axis=-1, keepdims=True).astype(I32)
    rank2 = jnp.sum(jnp.where(lane == i2, before, 0.0), axis=-1, keepdims=True).astype(I32)
    run_s[...] = run_s[...] + jnp.sum(onehot.astype(F32), axis=0, keepdims=True)

    sel = jnp.where(lane == 0, i1 - ng, jnp.where(lane == 1, i2 - ng,
                    jnp.where(lane == 2, rank1, jnp.where(lane == 3, rank2, 0))))
    sel_ref[...] = sel
    wts_ref[...] = jnp.where(lane == 0, w1, jnp.where(lane == 1, w2, 0.0))
    cnt_ref[...] = run_s[...].astype(I32)


def _router_weights(w_r1, b_r1, w_r2, b_r2):
    ng, ne = N_EXPERT_GROUPS, N_EXPERTS
    half = ROUTER_LANES // 2
    assert ng + ne <= half
    w_r = jnp.concatenate([w_r1, w_r2.reshape(D_MODEL, ne),
                           jnp.zeros((D_MODEL, half - ng - ne), F32)], axis=1)
    w_hi = w_r.astype(BF16)
    w_lo = (w_r - w_hi.astype(F32)).astype(BF16)
    b_r = jnp.concatenate([b_r1, b_r2.reshape(ne), jnp.zeros((half - ng - ne,), F32)])
    return jnp.concatenate([w_hi, w_lo], axis=1), jnp.concatenate([b_r, b_r]).reshape(1, ROUTER_LANES)


def _dispatch_body(eid_ref, rank_ref, off_ref, pad0_ref, npad_ref, nact_ref, x_hbm, xs_hbm, xbuf,
                   zero_s, in_sem, out_sem, pad_sem, *, td, n_tiles):
    i = pl.program_id(0)
    n = pl.num_programs(0)
    tile_rows = MOE_TILE * ROW_PITCH
    in_rows = td * ROW_PITCH

    def row_copy(src, dst_row, s):
        return pltpu.make_async_copy(src, xs_hbm.at[pl.ds(dst_row * ROW_PITCH, ROW_PITCH), :], s)

    def tile_load(t):
        s = t % DISPATCH_SLOTS
        return pltpu.make_async_copy(x_hbm.at[pl.ds(t * in_rows, in_rows), :], xbuf.at[s],
                                     in_sem.at[s])

    def wait_scatter(par):
        for _ in range(2):
            pltpu.make_async_copy(xbuf.at[0], xs_hbm.at[pl.ds(0, in_rows), :],
                                  out_sem.at[par]).wait()

    @pl.when(i == 0)
    def _():
        for t in range(DISPATCH_SLOTS - 1):
            @pl.when(t < n)
            def _(t=t):
                tile_load(t).start()
        zero_s[...] = jnp.zeros_like(zero_s)

        def pad_copies(e):
            out = []
            for bit in reversed(range(MOE_TILE.bit_length() - 1)):
                rows = (1 << bit) * ROW_PITCH
                first = (pad0_ref[e] + (npad_ref[e] >> (bit + 1) << (bit + 1))) * ROW_PITCH
                out.append((jnp.bitwise_and(npad_ref[e] >> bit, 1) == 1, pltpu.make_async_copy(
                    zero_s.at[pl.ds(0, rows), :], xs_hbm.at[pl.ds(first, rows), :], pad_sem.at[0])))
            return out

        for e in range(N_EXPERTS):
            for on, cp in pad_copies(e):
                pl.when(on)(cp.start)
        for e in range(N_EXPERTS):
            for on, cp in pad_copies(e):
                pl.when(on)(cp.wait)

        def zero_tile(t, carry):
            parts = [pltpu.make_async_copy(
                zero_s.at[pl.ds(0, MOE_TILE), :],
                xs_hbm.at[pl.ds(t * tile_rows + j * MOE_TILE, MOE_TILE), :],
                pad_sem.at[0]) for j in range(ROW_PITCH)]
            for cp in parts:
                cp.start()
            for cp in parts:
                cp.wait()
            return carry

        lax.fori_loop(nact_ref[0], n_tiles, zero_tile, 0)

    slot = i % DISPATCH_SLOTS
    par = i % 2
    tile_load(i).wait()
    base = i * td * 2
    for r in range(td):
        for k in range(2):
            j = base + 2 * r + k
            row_copy(xbuf.at[slot, pl.ds(r * ROW_PITCH, ROW_PITCH), :],
                     off_ref[eid_ref[j]] + rank_ref[j], out_sem.at[par]).start()

    @pl.when(i > 0)
    def _():
        wait_scatter(1 - par)

    @pl.when(i + DISPATCH_SLOTS - 1 < n)
    def _():
        tile_load(i + DISPATCH_SLOTS - 1).start()

    @pl.when(i == n - 1)
    def _():
        wait_scatter(par)


def _moe_dispatch(x_rows, eid, rank, row_off, pad_start, pad_count, nact, *, td, n_tiles):
    n = x_rows.shape[0] // ROW_PITCH
    grid_spec = pltpu.PrefetchScalarGridSpec(
        num_scalar_prefetch=6,
        grid=(n // td,),
        in_specs=[pl.BlockSpec(memory_space=pl.ANY)],
        out_specs=pl.BlockSpec(memory_space=pl.ANY),
        scratch_shapes=[pltpu.VMEM((DISPATCH_SLOTS, td * ROW_PITCH, LANES), F32),
                        pltpu.VMEM((MOE_TILE // 2 * ROW_PITCH, LANES), F32),
                        pltpu.SemaphoreType.DMA((DISPATCH_SLOTS,)),
                        pltpu.SemaphoreType.DMA((2,)),
                        pltpu.SemaphoreType.DMA((1,))],
    )
    return pl.pallas_call(
        functools.partial(_dispatch_body, td=td, n_tiles=n_tiles),
        out_shape=jax.ShapeDtypeStruct((n_tiles * MOE_TILE * ROW_PITCH, LANES), F32),
        grid_spec=grid_spec,
        compiler_params=_cparams(("arbitrary",), 16),
        name="moe_dispatch",
    )(eid, rank, row_off, pad_start, pad_count, nact, x_rows)


def _moe_body(te_ref, ord_ref, nxt_ref, nact_ref, x_ref, wg_hbm, wu_hbm, wd_hbm, o_ref,
              wg_f, wu_f, wd_f, wsem, wg_s, wu_s, wd_s):
    i = pl.program_id(0)
    nact = nact_ref[0]
    tm = MOE_TILE

    def weight_copies(expert, ws):
        return [pltpu.make_async_copy(hbm.at[expert], stage.at[ws], wsem.at[ws])
                for hbm, stage in ((wg_hbm, wg_f), (wu_hbm, wu_f), (wd_hbm, wd_f))]

    def ffn(wg, wu, wd):
        x = _load_gathered(x_ref, tm).astype(BF16)
        hg = _dot(x, wg)
        hu = _dot(x, wu)
        h = hg * (1.0 / (1.0 + jnp.exp(-hg))) * hu
        _store_gatherable(o_ref, _dot(h.astype(BF16), wd))

    def tile_step():
        first = jnp.logical_or(i == 0, te_ref[i] != te_ref[jnp.maximum(i - 1, 0)])

        @pl.when(first)
        def _():
            ws = ord_ref[i] % 2
            for cp in weight_copies(te_ref[i], ws):
                cp.wait()

            @pl.when(nxt_ref[i] >= 0)
            def _():
                for cp in weight_copies(nxt_ref[i], 1 - ws):
                    cp.start(priority=1)

            wg = wg_f[ws].astype(BF16)
            wu = wu_f[ws].astype(BF16)
            wd = wd_f[ws].astype(BF16)
            wg_s[...] = wg
            wu_s[...] = wu
            wd_s[...] = wd
            ffn(wg, wu, wd)

        @pl.when(jnp.logical_not(first))
        def _():
            ffn(wg_s[...], wu_s[...], wd_s[...])

    @pl.when(i == 0)
    def _():
        for cp in weight_copies(te_ref[0], 0):
            cp.start(priority=1)

    @pl.when(i < nact)
    def _():
        tile_step()

    @pl.when(i >= nact)
    def _():
        o_ref[...] = jnp.zeros_like(o_ref)


def _moe_experts(x_sorted, w_gate, w_up, w_down, tile_expert, tile_ord, tile_next, nact, *,
                 n_tiles):
    tm = MOE_TILE
    in_map = lambda i, te, od, nx, n: (jnp.minimum(i, n[0] - 1), 0)
    any_spec = pl.BlockSpec(memory_space=pl.ANY)
    grid_spec = pltpu.PrefetchScalarGridSpec(
        num_scalar_prefetch=4,
        grid=(n_tiles,),
        in_specs=[pl.BlockSpec((tm * ROW_PITCH, LANES), in_map), any_spec, any_spec, any_spec],
        out_specs=pl.BlockSpec((tm * ROW_PITCH, LANES), lambda i, te, od, nx, n: (i, 0)),
        scratch_shapes=[pltpu.VMEM((2, D_MODEL, D_EXPERT), F32),
                        pltpu.VMEM((2, D_MODEL, D_EXPERT), F32),
                        pltpu.VMEM((2, D_EXPERT, D_MODEL), F32),
                        pltpu.SemaphoreType.DMA((2,)),
                        pltpu.VMEM((D_MODEL, D_EXPERT), BF16),
                        pltpu.VMEM((D_MODEL, D_EXPERT), BF16),
                        pltpu.VMEM((D_EXPERT, D_MODEL), BF16)],
    )
    return pl.pallas_call(
        _moe_body,
        out_shape=jax.ShapeDtypeStruct((n_tiles * tm * ROW_PITCH, LANES), F32),
        grid_spec=grid_spec,
        compiler_params=_cparams(("arbitrary",), 48),
        name="moe_experts",
    )(tile_expert, tile_ord, tile_next, nact, x_sorted, w_gate, w_up, w_down)


def _combine_body(eid_ref, rank_ref, off_ref, ys_hbm, wts_ref, x_ref, g_ref, b_ref, o1_ref, o2_ref,
                  buf, sem, *, tc, tiles1):
    i = pl.program_id(0)
    n = pl.num_programs(0)
    slot = i % GATHER_SLOTS
    ahead = GATHER_SLOTS - 1

    def issue_gather(tile, slot_):
        base = tile * tc * 2
        for r in range(tc):
            for k in range(2):
                j = base + 2 * r + k
                _start_row_gather(ys_hbm, off_ref[eid_ref[j]] + rank_ref[j], buf.at[slot_, k], r,
                                  sem.at[slot_])

    @pl.when(i == 0)
    def _():
        for t in range(ahead):
            @pl.when(t < n)
            def _(t=t):
                issue_gather(t, t)

    for k in range(2):
        _wait_row_gathers(buf.at[slot, k], buf.at[(i + 1) % GATHER_SLOTS, k], tc, sem.at[slot])

    @pl.when(i + ahead < n)
    def _():
        issue_gather(i + ahead, (i + ahead) % GATHER_SLOTS)

    w = wts_ref[...]
    moe = (w[:, 0:1] * _load_gathered(buf.at[slot, 0], tc)
           + w[:, 1:2] * _load_gathered(buf.at[slot, 1], tc))
    out = _layernorm(DEEPNORM_ALPHA * x_ref[...] + moe, g_ref[...], b_ref[...])

    @pl.when(i < tiles1)
    def _():
        o1_ref[...] = out

    @pl.when(i >= tiles1)
    def _():
        o2_ref[...] = out


def _moe_combine(ys, eid, rank, row_off, wts, x, g, b, *, tc, n_first):
    m = x.shape[0]
    assert n_first % tc == 0 and (m - n_first) % tc == 0
    tiles1 = n_first // tc
    grid_spec = pltpu.PrefetchScalarGridSpec(
        num_scalar_prefetch=3,
        grid=(m // tc,),
        in_specs=[pl.BlockSpec(memory_space=pl.ANY),
                  pl.BlockSpec((tc, ROUTER_LANES), lambda i, *_: (i, 0)),
                  pl.BlockSpec((tc, D_MODEL), lambda i, *_: (i, 0)),
                  pl.BlockSpec((1, D_MODEL), lambda i, *_: (0, 0)),
                  pl.BlockSpec((1, D_MODEL), lambda i, *_: (0, 0))],
        out_specs=[pl.BlockSpec((tc, D_MODEL), lambda i, *_: (jnp.minimum(i, tiles1 - 1), 0)),
                   pl.BlockSpec((tc, D_MODEL), lambda i, *_: (jnp.maximum(i - tiles1, 0), 0))],
        scratch_shapes=[pltpu.VMEM((GATHER_SLOTS, 2, tc * ROW_PITCH, LANES), F32),
                        pltpu.SemaphoreType.DMA((GATHER_SLOTS,))],
    )
    return pl.pallas_call(
        functools.partial(_combine_body, tc=tc, tiles1=tiles1),
        out_shape=[jax.ShapeDtypeStruct((n_first, D_MODEL), F32),
                   jax.ShapeDtypeStruct((m - n_first, D_MODEL), F32)],
        grid_spec=grid_spec,
        compiler_params=_cparams(("arbitrary",), 16 * tc * D_MODEL * 4 / 2**20 + 8),
        name="moe_combine_ln3",
    )(eid, rank, row_off, ys, wts, x, g, b)


def _moe(x, x_rows, sel, wts, cnt, w_gate, w_up, w_down, g, b, *, n_first, tc):
    n = x.shape[0]
    ng, ne = N_EXPERT_GROUPS, N_EXPERTS

    tm = MOE_TILE
    n_tiles = (2 * n) // tm + ne
    counts = cnt[0, ng:ng + ne]
    tiles_per = (counts + tm - 1) // tm
    tile_end = jnp.cumsum(tiles_per)
    row_off = (tile_end - tiles_per) * tm
    nact = tile_end[-1]
    a_eid, a_rank = sel[:, 0:2].reshape(-1), sel[:, 2:4].reshape(-1)
    row_off = row_off.astype(I32)
    tile_ids = jnp.minimum(jnp.arange(n_tiles, dtype=I32), nact - 1)
    tile_expert = jnp.sum((tile_end[None, :] <= tile_ids[:, None]).astype(I32), axis=1)
    used = tiles_per > 0
    eid = jnp.arange(ne, dtype=I32)
    ordinal = jnp.cumsum(used.astype(I32)) - 1
    later = jnp.where(jnp.logical_and(used[None, :], eid[None, :] > eid[:, None]), eid[None, :], ne)
    nxt = jnp.min(later, axis=1)
    nxt = jnp.where(nxt == ne, -1, nxt)

    nact = nact.reshape(1).astype(I32)
    x_sorted = _moe_dispatch(x_rows, a_eid, a_rank, row_off, (row_off + counts).astype(I32),
                             (tiles_per * tm - counts).astype(I32), nact, td=tc, n_tiles=n_tiles)
    ys = _moe_experts(x_sorted, w_gate, w_up, w_down, tile_expert, ordinal[tile_expert],
                      nxt[tile_expert], nact, n_tiles=n_tiles)
    return _moe_combine(ys, a_eid, a_rank, row_off, wts, x, g, b, tc=tc, n_first=n_first)


def _row_tile(m, cap):
    best = SUBLANES
    for t in range(SUBLANES, cap + 1, SUBLANES):
        if m % t == 0:
            best = t
    return best


def kernel(x_prompt, x_sample, cache_win_k, cache_win_v, state_ssm_re, state_ssm_im, cache_mem_k, cache_mem_v, mem_prompt, w_in, ssm_lam_re, ssm_lam_im, ssm_log_dt, ssm_b_re, ssm_b_im, ssm_c_re, ssm_c_im, ssm_d, w_glu, g_attn, g_ssm, w_out, ln1_g, ln1_b, w_mq, w_mk, w_mv, w_mo, ln2_g, ln2_b, w_r1, b_r1, w_r2, b_r2, w_gate, w_up, w_down, ln3_g, ln3_b):
    nb, seq, d = x_prompt.shape
    ns, dseq, _ = x_sample.shape
    n_p, n_s = nb * seq, ns * dseq
    n = n_p + n_s
    l = 0
    row2 = lambda v: v[l].reshape(1, -1)

    x_p, x_s = x_prompt.reshape(n_p, d), x_sample.reshape(n_s, d)
    tm_p = _row_tile(n_p, 1024)
    tm_ln = _row_tile(n_p, 512)
    assert n_p % n_s == 0 and n_s % SUBLANES == 0

    proj_p = _matmul(x_p, w_in[l], tm=tm_p, tn=1024, name="proj_in_prompt")
    proj_s = _matmul(x_s, w_in[l], tm=n_s, tn=1024, name="proj_in_sample")

    attn_p = _attn_prompt(proj_p, n_batch=nb, seq=seq)
    attn_s = _attn_sample(proj_s, cache_win_k[l], cache_win_v[l], row0=0, n_seq=ns, n_new=dseq)

    seg_len = seq // SSM_SEGMENTS
    prm = _ssm_params(ssm_lam_re[l], ssm_lam_im[l], ssm_log_dt[l], ssm_b_re[l], ssm_b_im[l],
                      ssm_c_re[l], ssm_c_im[l], ssm_d[l], seg_len)
    zeros = jnp.zeros((nb * SSM_SEGMENTS, N_SSM_GROUPS * SSM_STATE), F32)
    tl = _row_tile(seg_len, 32)
    end_re, end_im = _ssm_scan(proj_p, prm, zeros, zeros, seq_len=seg_len, tl=tl, nseg=1,
                               emit_y=False, exact_in=False, name="ssm_state_prompt")
    yg_p, fin_re, fin_im = _ssm_scan(proj_p, prm, end_re, end_im, seq_len=seg_len, tl=tl,
                                     nseg=SSM_SEGMENTS, emit_y=True, exact_in=False,
                                     name="ssm_scan_prompt")
    last = SSM_SEGMENTS - 1
    ssm_re_p = fin_re.reshape(nb, SSM_SEGMENTS, N_SSM_GROUPS, SSM_STATE)[:, last]
    ssm_im_p = fin_im.reshape(nb, SSM_SEGMENTS, N_SSM_GROUPS, SSM_STATE)[:, last]

    h0_re = state_ssm_re[l].reshape(ns, -1)
    h0_im = state_ssm_im[l].reshape(ns, -1)
    yg_s, ssm_re_s, ssm_im_s = _ssm_scan(proj_s, prm, h0_re, h0_im, seq_len=dseq, tl=dseq, nseg=1,
                                         emit_y=True, exact_in=True, name="ssm_scan_sample")
    w_glu_b = w_glu[l].astype(BF16)
    ssm_out_p = _glu(yg_p, w_glu_b, tm=tm_p, name="ssm_glu_prompt")
    ssm_out_s = _glu(yg_s, w_glu_b, tm=n_s, name="ssm_glu_sample")

    mix_args = (row2(g_attn), row2(g_ssm), w_out[l].astype(BF16))
    ln1 = (row2(ln1_g), row2(ln1_b))
    x1_p = _mix(attn_p, ssm_out_p, *mix_args, x_p, *ln1, tm=tm_ln, name="mix_out_ln1_prompt")
    x1_s = _mix(attn_s, ssm_out_s, *mix_args, x_s, *ln1, tm=n_s, name="mix_out_ln1_sample")

    mem_rows = mem_prompt.reshape(nb * N_MEM, d)
    mem_k = _matmul(mem_rows, w_mk[l], tm=nb * N_MEM, tn=1024, name="mem_k")
    mem_v = _matmul(mem_rows, w_mv[l], tm=nb * N_MEM, tn=1024, name="mem_v")
    q_p = _matmul(x1_p, w_mq[l], tm=tm_p, tn=1024, name="mem_q_prompt", out_dtype=BF16)
    q_s = _matmul(x1_s, w_mq[l], tm=n_s, tn=1024, name="mem_q_sample")
    o_p = _memattn(q_p, mem_k.reshape(nb, N_MEM, d), mem_v.reshape(nb, N_MEM, d),
                   row0=0, n_seq=nb, seq=seq, tq=_row_tile(seq, 512), name="memattn_prompt")
    o_s = _memattn_heads(q_s, cache_mem_k[l], cache_mem_v[l], row0=0, n_seq=ns, seq=dseq,
                         name="memattn_sample")
    w_r, b_r = _router_weights(w_r1[l], b_r1[l], w_r2[l], b_r2[l])
    x2, x2_rows, sel, wts, cnt = _mm_ln(o_p, o_s, w_mo[l].astype(BF16), x1_p, x1_s, row2(ln2_g),
                                        row2(ln2_b), w_r, b_r, name="mem_out_ln2_route")

    y_p, y_s = _moe(x2, x2_rows, sel, wts, cnt, w_gate[l], w_up[l], w_down[l], row2(ln3_g),
                    row2(ln3_b), n_first=n_p, tc=_row_tile(n_s, 128))

    y_p = y_p.reshape(nb, seq, d)
    y_s = y_s.reshape(ns, dseq, d)
    wp = min(max(w for w, _ in DILATIONS), seq)
    k_p, v_p = _kv_window(proj_p, n_batch=nb, seq=seq, window=wp, tr=_row_tile(wp, 512))
    k_s = proj_s[:, D_ATT:2 * D_ATT].reshape(ns, dseq, ATT_HEADS, ATT_HD)
    v_s = proj_s[:, 2 * D_ATT:3 * D_ATT].reshape(ns, dseq, ATT_HEADS, ATT_HD)
    state = lambda v, b_: v.reshape(1, b_, N_SSM_GROUPS, SSM_STATE)
    return (y_p, y_s, k_p[None], v_p[None], k_s[None], v_s[None],
            state(ssm_re_p, nb), state(ssm_im_p, nb), state(ssm_re_s, ns), state(ssm_im_s, ns),
            mem_k.reshape(1, nb, N_MEM, MEM_HEADS, MEM_HD),
            mem_v.reshape(1, nb, N_MEM, MEM_HEADS, MEM_HD))
```

```python
import functools
import math

import numpy as np
import jax
import jax.numpy as jnp
from jax import lax
from jax.experimental import pallas as pl
from jax.experimental.pallas import tpu as pltpu

F32 = jnp.float32
BF16 = jnp.bfloat16
I32 = jnp.int32

D_MODEL = 2048
PAST_LEN = 8192
D_ATT = D_MODEL // 2
ATT_HEADS = 8
ATT_HD = D_ATT // ATT_HEADS
DILATIONS = ((128, 1), (512, 4), (2048, 16))
D_SSM = D_MODEL - D_ATT
SSM_GROUP_CH = 16
N_SSM_GROUPS = D_SSM // SSM_GROUP_CH
SSM_STATE = 64
N_MEM = 256
MEM_HEADS = 4
MEM_HD = D_MODEL // MEM_HEADS
N_EXPERT_GROUPS = 4
EXPERTS_PER_GROUP = 8
N_EXPERTS = N_EXPERT_GROUPS * EXPERTS_PER_GROUP
D_EXPERT = D_MODEL // 4
DEPTH = 1
DEEPNORM_ALPHA = (2.0 * DEPTH) ** 0.25
LN_EPS = 1e-5
RMS_EPS = 1e-6

LANES = 128
SUBLANES = 8
ROW_CHUNKS = D_MODEL // LANES
ROW_PITCH = ROW_CHUNKS + 1
Q_BLOCK = 128
ATTN_GROUP = 8
SSM_LANE_TILE = 128
SSM_GROUPS_PER_TILE = SSM_LANE_TILE // SSM_GROUP_CH
SSM_STATES_PER_TILE = SSM_GROUPS_PER_TILE * SSM_STATE
SSM_SEGMENTS = 8
MOE_TILE = 256
GATHER_SLOTS = 3
DISPATCH_SLOTS = 3
ROW_SPLIT = 2
ROUTER_LANES = 128
NEG_INF = float("-inf")


def _cparams(semantics, vmem_mib):
    return pltpu.CompilerParams(dimension_semantics=semantics,
                                vmem_limit_bytes=int(vmem_mib) << 20)


def _layernorm(y, g, b):
    mu = jnp.mean(y, axis=-1, keepdims=True)
    yc = y - mu
    var = jnp.mean(yc * yc, axis=-1, keepdims=True)
    return yc * lax.rsqrt(var + LN_EPS) * g + b


def _rmsnorm(v, g):
    return v * lax.rsqrt(jnp.mean(v * v, axis=-1, keepdims=True) + RMS_EPS) * g


def _dot(a, b):
    return jnp.dot(a, b, preferred_element_type=F32)


def _dot_nt(a, b):
    return lax.dot_general(a, b, (((1,), (1,)), ((), ())), preferred_element_type=F32)


def _mm_body(x_ref, w_ref, o_ref, wb_s):
    @pl.when(pl.program_id(1) == 0)
    def _():
        wb_s[...] = w_ref[...].astype(BF16)

    o_ref[...] = _dot(x_ref[...].astype(BF16), wb_s[...]).astype(o_ref.dtype)


def _matmul(x, w, *, tm, tn, name, out_dtype=F32):
    m, k = x.shape
    n = w.shape[1]
    vmem = (2 * (tm * k * 4 + k * tn * 4 + tm * tn * 4) + k * tn * 2 + tm * k * 2) / 2**20 + 8
    return pl.pallas_call(
        _mm_body,
        out_shape=jax.ShapeDtypeStruct((m, n), out_dtype),
        grid=(n // tn, m // tm),
        in_specs=[pl.BlockSpec((tm, k), lambda j, i: (i, 0)),
                  pl.BlockSpec((k, tn), lambda j, i: (0, j))],
        out_specs=pl.BlockSpec((tm, tn), lambda j, i: (i, j)),
        scratch_shapes=[pltpu.VMEM((k, tn), BF16)],
        compiler_params=_cparams(("parallel", "arbitrary"), vmem),
        name=name,
    )(x, w)


def _kv_window_body(k_ref, v_ref, ko_ref, vo_ref):
    for h in range(ATT_HEADS):
        sl = slice(h * ATT_HD, (h + 1) * ATT_HD)
        ko_ref[:, h, :] = k_ref[:, sl]
        vo_ref[:, h, :] = v_ref[:, sl]


def _kv_window(proj, *, n_batch, seq, window, tr):
    assert window % tr == 0 and seq % tr == 0
    per, first = seq // tr, (seq - window) // tr
    col = lambda c: pl.BlockSpec((tr, D_ATT), lambda b, t: (b * per + first + t, c))
    out_spec = pl.BlockSpec((None, tr, ATT_HEADS, ATT_HD), lambda b, t: (b, t, 0, 0))
    out_shape = jax.ShapeDtypeStruct((n_batch, window, ATT_HEADS, ATT_HD), F32)
    return pl.pallas_call(
        _kv_window_body,
        out_shape=[out_shape, out_shape],
        grid=(n_batch, window // tr),
        in_specs=[col(1), col(2)],
        out_specs=[out_spec, out_spec],
        compiler_params=_cparams(("parallel", "parallel"), 8 * tr * D_ATT * 4 / 2**20 + 8),
        name="kv_window",
    )(proj, proj)


def _attn_prompt_body(q_ref, k_ref, v_ref, o_ref, kt_s, va_s, on_s, lse_s, *, seq, dilations):
    scale = ATT_HD ** -0.5
    nblk = seq // Q_BLOCK
    qi = lax.broadcasted_iota(I32, (Q_BLOCK, Q_BLOCK), 0)
    kj = lax.broadcasted_iota(I32, (Q_BLOCK, Q_BLOCK), 1)
    cur_ok = kj <= qi
    prev_ok = kj >= qi
    va_s[:, :, ATT_HD:] = jnp.ones((nblk, Q_BLOCK, ATT_HD), BF16)

    for br, (_, d) in enumerate(dilations):
        span = d * Q_BLOCK
        nb = seq // span

        def stream_rows(t, d=d, span=span, nb=nb):
            r = t // nb
            ib = t % nb
            return r, ib, pl.ds(r + ib * span, Q_BLOCK, stride=d)

        def prep(g, carry, stream_rows=stream_rows):
            loaded = []
            for j in range(ATTN_GROUP):
                t = g * ATTN_GROUP + j
                _, _, rows = stream_rows(t)
                loaded.append((t, k_ref[rows, :], v_ref[rows, :]))
            for t, kk, vv in loaded:
                kt_s[t] = jnp.transpose(kk).astype(BF16)
                va_s[t, :, 0:ATT_HD] = vv.astype(BF16)
            return carry

        lax.fori_loop(0, nblk // ATTN_GROUP, prep, 0)

        def group(g, carry, br=br, nb=nb, stream_rows=stream_rows):
            scores = []
            for j in range(ATTN_GROUP):
                t = g * ATTN_GROUP + j
                r, ib, rows = stream_rows(t)
                tp = jnp.maximum(t - 1, r * nb)
                q = (q_ref[rows, :] * scale).astype(BF16)
                s = _dot(q, jnp.concatenate([kt_s[tp], kt_s[t]], axis=1))
                scores.append((t, tp, ib, rows, s))
            probs = []
            for t, tp, ib, rows, s in scores:
                ok = jnp.concatenate([jnp.logical_and(prev_ok, ib > 0), cur_ok], axis=1)
                s = jnp.where(ok, s, NEG_INF)
                m = jnp.max(s, axis=-1, keepdims=True)
                probs.append((t, tp, rows, m, jnp.exp(s - m).astype(BF16)))
            outs = [(rows, m, _dot(p, jnp.concatenate([va_s[tp], va_s[t]], axis=0)))
                    for t, tp, rows, m, p in probs]
            for rows, m, al in outs:
                l = al[:, ATT_HD:]
                on_s[br, rows, :] = al[:, :ATT_HD] / l
                lse_s[br, rows, :] = m + jnp.log(l)
            return carry

        lax.fori_loop(0, nblk // ATTN_GROUP, group, 0)

    chunk = 256
    nbr = len(dilations)

    def merge(c, carry):
        rows = pl.ds(pl.multiple_of(c * chunk, chunk), chunk)
        ls = [lse_s[b, rows, :] for b in range(nbr)]
        mx = functools.reduce(jnp.maximum, ls)
        es = [jnp.exp(li - mx) for li in ls]
        num = sum(es[b] * on_s[b, rows, :] for b in range(nbr))
        o_ref[rows, :] = num / sum(es)
        return carry

    lax.fori_loop(0, seq // chunk, merge, 0)


def _attn_prompt(proj, *, n_batch, seq, dilations=DILATIONS):
    for w, d in dilations:
        assert w // d == Q_BLOCK and seq % (d * Q_BLOCK) == 0
    nbr = len(dilations)
    nblk = seq // Q_BLOCK
    assert nblk % ATTN_GROUP == 0
    blk = lambda off: pl.BlockSpec((seq, ATT_HD), lambda b, h, off=off: (b, off + h))
    vmem = ((4 * 2 + 2 * nbr) * seq * ATT_HD * 4 + 3 * seq * ATT_HD * 2) / 2**20 + 8
    return pl.pallas_call(
        functools.partial(_attn_prompt_body, seq=seq, dilations=dilations),
        out_shape=jax.ShapeDtypeStruct((n_batch * seq, D_ATT), F32),
        grid=(n_batch, ATT_HEADS),
        in_specs=[blk(0), blk(ATT_HEADS), blk(2 * ATT_HEADS)],
        out_specs=pl.BlockSpec((seq, ATT_HD), lambda b, h: (b, h)),
        scratch_shapes=[pltpu.VMEM((nblk, ATT_HD, Q_BLOCK), BF16),
                        pltpu.VMEM((nblk, Q_BLOCK, 2 * ATT_HD), BF16),
                        pltpu.VMEM((nbr, seq, ATT_HD), F32),
                        pltpu.VMEM((nbr, seq, ATT_HD), F32)],
        compiler_params=_cparams(("parallel", "parallel"), vmem),
        name="attn_prompt",
    )(proj, proj, proj)


def _sample_key_multiplicity(n_new, n_cache, past_len, dilations):
    d_max = max(d for _, d in dilations)
    tail = max(w for w, d in dilations if d != d_max)
    assert past_len % d_max == 0 and n_cache % d_max == 0 and n_new <= d_max // 2
    assert tail % d_max == 0 and tail <= n_cache
    half = d_max // 2
    n_grid = (n_cache - tail) // d_max
    kv_start = past_len - n_cache
    grid_rows = (np.arange(n_grid)[:, None] * d_max + np.arange(half)[None, :]).reshape(-1)
    tail_rows = n_cache - tail + np.arange(tail)
    new_rows = n_cache + np.arange(n_new)
    qpos = past_len + np.arange(n_new)

    def mult(rows):
        kpos = kv_start + rows
        delta = qpos[:, None] - kpos[None, :]
        c = np.zeros(delta.shape, np.float32)
        for w, d in dilations:
            c += ((delta >= 0) & (delta <= w) & (delta % d == 0) & (kpos[None, :] >= kv_start))
        return c

    fetched = np.zeros(n_cache + n_new, bool)
    fetched[grid_rows] = True
    fetched[tail_rows] = True
    fetched[new_rows] = True
    assert not mult(np.nonzero(~fetched)[0]).any()
    return mult(grid_rows), mult(tail_rows), mult(new_rows), n_grid, tail, half, d_max


def _attn_sample_body(q_ref, kn_ref, vn_ref, kg_ref, kt_ref, vg_ref, vt_ref,
                      cg_ref, ct_ref, cn_ref, o_ref):
    scale = ATT_HD ** -0.5
    heads = lambda ref: jnp.concatenate(
        [ref[:, h * ATT_HD:(h + 1) * ATT_HD] for h in range(ATT_HEADS)], axis=0)
    q = (heads(q_ref) * scale).astype(BF16)
    kn = heads(kn_ref).astype(BF16)
    vn = heads(vn_ref).astype(BF16)
    flat = lambda ref: ref[...].reshape(-1, ATT_HD).astype(BF16)
    cg, ct, cn = cg_ref[...], ct_ref[...], cn_ref[...]
    sg = jnp.where(cg > 0, _dot_nt(q, flat(kg_ref)), NEG_INF)
    st = jnp.where(ct > 0, _dot_nt(q, flat(kt_ref)), NEG_INF)
    sn = jnp.where(cn > 0, _dot_nt(q, kn), NEG_INF)
    m = jnp.maximum(jnp.maximum(jnp.max(sg, axis=-1, keepdims=True),
                                jnp.max(st, axis=-1, keepdims=True)),
                    jnp.max(sn, axis=-1, keepdims=True))
    pg = cg * jnp.exp(sg - m)
    pt = ct * jnp.exp(st - m)
    pn = cn * jnp.exp(sn - m)
    l = (jnp.sum(pg, axis=-1, keepdims=True) + jnp.sum(pt, axis=-1, keepdims=True)
         + jnp.sum(pn, axis=-1, keepdims=True))
    acc = (_dot(pg.astype(BF16), flat(vg_ref)) + _dot(pt.astype(BF16), flat(vt_ref))
           + _dot(pn.astype(BF16), vn))
    out = acc / l
    n_new = q_ref.shape[0]
    for h in range(ATT_HEADS):
        o_ref[:, h * ATT_HD:(h + 1) * ATT_HD] = out[h * n_new:(h + 1) * n_new, :]


def _attn_sample(proj, win_k, win_v, *, row0, n_seq, n_new, past_len=PAST_LEN,
                 dilations=DILATIONS):
    n_cache = win_k.shape[1]
    cg, ct, cn, n_grid, tail, half, d_max = _sample_key_multiplicity(
        n_new, n_cache, past_len, dilations)
    assert row0 % n_new == 0 and n_new % SUBLANES == 0 and n_cache % tail == 0
    eye = np.eye(ATT_HEADS, dtype=np.float32)
    key_major = lambda c: np.einsum("tk,hg->htkg", c, eye).reshape(ATT_HEADS * n_new, -1)
    head_major = lambda c: np.einsum("tk,hg->htgk", c, eye).reshape(ATT_HEADS * n_new, -1)
    cg, ct, cn = key_major(cg), key_major(ct), head_major(cn)
    rb = row0 // n_new
    n_groups = n_cache // d_max
    kgv = win_k.reshape(n_seq, n_groups, d_max, ATT_HEADS, ATT_HD)
    vgv = win_v.reshape(n_seq, n_groups, d_max, ATT_HEADS, ATT_HD)
    ktv = win_k.reshape(n_seq, n_cache // tail, tail, ATT_HEADS, ATT_HD)
    vtv = win_v.reshape(n_seq, n_cache // tail, tail, ATT_HEADS, ATT_HD)
    new = lambda off: pl.BlockSpec((n_new, D_ATT), lambda b, off=off: (rb + b, off))
    grid_spec = pl.BlockSpec((None, n_grid, half, ATT_HEADS, ATT_HD), lambda b: (b, 0, 0, 0, 0))
    tail_spec = pl.BlockSpec((None, None, tail, ATT_HEADS, ATT_HD),
                             lambda b: (b, n_cache // tail - 1, 0, 0, 0))
    const = lambda a: pl.BlockSpec(a.shape, lambda b: (0, 0))
    vmem = (2 * 2 * (n_grid * half + tail) * D_ATT * 4 + 4 * cg.size * 4 * 3) / 2**20 + 12
    return pl.pallas_call(
        _attn_sample_body,
        out_shape=jax.ShapeDtypeStruct((n_seq * n_new, D_ATT), F32),
        grid=(n_seq,),
        in_specs=[new(0), new(1), new(2), grid_spec, tail_spec, grid_spec, tail_spec,
                  const(cg), const(ct), const(cn)],
        out_specs=pl.BlockSpec((n_new, D_ATT), lambda b: (b, 0)),
        compiler_params=_cparams(("parallel",), vmem),
        name="attn_sample",
    )(proj, proj, proj, kgv, ktv, vgv, vtv, jnp.asarray(cg), jnp.asarray(ct), jnp.asarray(cn))


def _gelu_tanh(x):
    return 0.5 * x * (1.0 + jnp.tanh(math.sqrt(2.0 / math.pi) * (x + 0.044715 * (x * x * x))))


def _ssm_body(u_ref, bb_ref, cst_ref, a_ref, ap_ref, d_ref, hre_ref, him_ref, *rest,
              tl, npar, seq_len, nseg, emit_y, exact_in):
    if emit_y:
        y_ref, fre_ref, fim_ref, h_s = rest
    else:
        fre_ref, fim_ref, h_s = rest
    ns = SSM_STATES_PER_TILE
    c = pl.program_id(1)
    ngrp = npar // SUBLANES

    def step_rows(i, g):
        return pl.ds(c * tl + i + g * SUBLANES * seq_len, SUBLANES, stride=seq_len)

    @pl.when(c == 0)
    def _init():
        if nseg == 1:
            h_s[0] = hre_ref[...]
            h_s[1] = him_ref[...]
        else:
            pr, pi = ap_ref[0:1, :], ap_ref[1:2, :]
            for b in range(npar // nseg):
                sr = jnp.zeros((1, ns), F32)
                si = jnp.zeros((1, ns), F32)
                for j in range(nseg):
                    row = b * nseg + j
                    h_s[0, row:row + 1, :] = sr
                    h_s[1, row:row + 1, :] = si
                    er, ei = hre_ref[row:row + 1, :], him_ref[row:row + 1, :]
                    sr, si = pr * sr - pi * si + er, pr * si + pi * sr + ei

    ar = jnp.broadcast_to(a_ref[0:1, :], (SUBLANES, ns))
    ai = jnp.broadcast_to(a_ref[1:2, :], (SUBLANES, ns))
    us = [jnp.concatenate([u_ref[step_rows(i, g), :] for i in range(tl)], axis=0)
          for g in range(ngrp)]
    if exact_in:
        xs = [jnp.dot(u, bb_ref[...], precision=lax.Precision.HIGHEST, preferred_element_type=F32)
              for u in us]
    else:
        xs = [_dot(u.astype(BF16), bb_ref[...]) for u in us]

    hs = []
    for g in range(ngrp):
        gs = slice(g * SUBLANES, (g + 1) * SUBLANES)
        hr, hi = h_s[0, gs, :], h_s[1, gs, :]
        states = []
        for i in range(tl):
            xr = xs[g][i * SUBLANES:(i + 1) * SUBLANES, 0:ns]
            xi = xs[g][i * SUBLANES:(i + 1) * SUBLANES, ns:2 * ns]
            hr, hi = ar * hr - ai * hi + xr, ar * hi + ai * hr + xi
            if emit_y:
                states.append(jnp.concatenate([hr, hi], axis=1))
        h_s[0, gs, :] = hr
        h_s[1, gs, :] = hi
        if emit_y:
            hs.append(jnp.concatenate(states, axis=0))

    if emit_y:
        for g in range(ngrp):
            y = _gelu_tanh(_dot(hs[g].astype(BF16), cst_ref[...]) + d_ref[...] * us[g])
            for i in range(tl):
                y_ref[step_rows(i, g), :] = y[i * SUBLANES:(i + 1) * SUBLANES, :]

    @pl.when(c == pl.num_programs(1) - 1)
    def _fin():
        fre_ref[...] = h_s[0]
        fim_ref[...] = h_s[1]


def _ssm_scan(proj, prm, hin_re, hin_im, *, seq_len, tl, nseg, emit_y, exact_in, name):
    rows = proj.shape[0]
    npar = rows // seq_len
    assert npar % SUBLANES == 0 and seq_len % tl == 0
    ns = SSM_STATES_PER_TILE
    nk = D_SSM // SSM_LANE_TILE
    col0 = (proj.shape[1] - D_SSM) // SSM_LANE_TILE
    bb = prm["bb_f32"] if exact_in else prm["bb_bf16"]
    in_specs = [
        pl.BlockSpec((rows, SSM_LANE_TILE), lambda k, c: (0, col0 + k)),
        pl.BlockSpec((None, SSM_LANE_TILE, 2 * ns), lambda k, c: (k, 0, 0)),
        pl.BlockSpec((None, 2 * ns, SSM_LANE_TILE), lambda k, c: (k, 0, 0)),
        pl.BlockSpec((None, 2, ns), lambda k, c: (k, 0, 0)),
        pl.BlockSpec((None, 2, ns), lambda k, c: (k, 0, 0)),
        pl.BlockSpec((1, SSM_LANE_TILE), lambda k, c: (0, k)),
        pl.BlockSpec((npar, ns), lambda k, c: (0, k)),
        pl.BlockSpec((npar, ns), lambda k, c: (0, k)),
    ]
    state_shape = jax.ShapeDtypeStruct((npar, nk * ns), F32)
    state_spec = pl.BlockSpec((npar, ns), lambda k, c: (0, k))
    out_shape = [state_shape, state_shape]
    out_specs = [state_spec, state_spec]
    if emit_y:
        out_shape = [jax.ShapeDtypeStruct((rows, D_SSM), F32)] + out_shape
        out_specs = [pl.BlockSpec((rows, SSM_LANE_TILE), lambda k, c: (0, k))] + out_specs
    vmem = (4 * rows * SSM_LANE_TILE * 4 + tl * npar * 2 * ns * 4) / 2**20 + 16
    return pl.pallas_call(
        functools.partial(_ssm_body, tl=tl, npar=npar, seq_len=seq_len, nseg=nseg, emit_y=emit_y,
                          exact_in=exact_in),
        out_shape=out_shape,
        grid=(nk, seq_len // tl),
        in_specs=in_specs,
        out_specs=out_specs,
        scratch_shapes=[pltpu.VMEM((2, npar, ns), F32)],
        compiler_params=_cparams(("parallel", "arbitrary"), vmem),
        name=name,
    )(proj, bb, prm["cst"], prm["a"], prm["apow"], prm["d"], hin_re, hin_im)


def _ssm_params(lam_re, lam_im, log_dt, b_re, b_im, c_re, c_im, d_skip, seg_len):
    g, p, c = N_SSM_GROUPS, SSM_STATE, SSM_GROUP_CH
    nk, gt = g // SSM_GROUPS_PER_TILE, SSM_GROUPS_PER_TILE
    dt = jnp.exp(log_dt.astype(F32))[:, None]
    lr, li = lam_re.astype(F32), lam_im.astype(F32)
    mag = jnp.exp(lr * dt)
    a_re, a_im = mag * jnp.cos(li * dt), mag * jnp.sin(li * dt)
    magp = jnp.exp(lr * dt * seg_len)
    p_re, p_im = magp * jnp.cos(li * dt * seg_len), magp * jnp.sin(li * dt * seg_len)
    den = lr * lr + li * li
    nr, ni = a_re - 1.0, a_im
    f_re, f_im = (nr * lr + ni * li) / den, (ni * lr - nr * li) / den
    br, bi = b_re.astype(F32), b_im.astype(F32)
    bb_re = f_re[..., None] * br - f_im[..., None] * bi
    bb_im = f_re[..., None] * bi + f_im[..., None] * br
    eye = jnp.eye(gt, dtype=F32)

    def pack_b(m):
        return jnp.einsum("kgpc,gh->kgchp", m.reshape(nk, gt, p, c), eye).reshape(nk, gt * c, gt * p)

    def pack_c(m):
        return jnp.einsum("kgcp,gh->kgphc", m.reshape(nk, gt, c, p), eye).reshape(nk, gt * p, gt * c)

    bb = jnp.concatenate([pack_b(bb_re), pack_b(bb_im)], axis=2)
    cst = jnp.concatenate([pack_c(c_re.astype(F32)), -pack_c(c_im.astype(F32))], axis=1)
    tile = lambda v: v.reshape(nk, 1, gt * p)
    return {
        "bb_f32": bb, "bb_bf16": bb.astype(BF16), "cst": cst.astype(BF16),
        "a": jnp.concatenate([tile(a_re), tile(a_im)], axis=1),
        "apow": jnp.concatenate([tile(p_re), tile(p_im)], axis=1),
        "d": d_skip.astype(F32).reshape(1, g * c),
    }


def _glu_body(y_ref, w_ref, o_ref):
    yg = y_ref[...]
    z = _dot(yg.astype(BF16), w_ref[...])
    o_ref[...] = yg * (1.0 / (1.0 + jnp.exp(-z)))


def _glu(yg, w, *, tm, name):
    m, n = yg.shape
    return pl.pallas_call(
        _glu_body,
        out_shape=jax.ShapeDtypeStruct((m, n), F32),
        grid=(m // tm,),
        in_specs=[pl.BlockSpec((tm, n), lambda i: (i, 0)), pl.BlockSpec((n, n), lambda i: (0, 0))],
        out_specs=pl.BlockSpec((tm, n), lambda i: (i, 0)),
        compiler_params=_cparams(("parallel",), 4 * tm * n * 4 / 2**20 + 12),
        name=name,
    )(yg, w)


def _mix_body(attn_ref, ssm_ref, ga_ref, gs_ref, w_ref, x_ref, g_ref, b_ref, o_ref):
    half = attn_ref.shape[0] // ROW_SPLIT
    for r in range(ROW_SPLIT):
        rows = slice(r * half, (r + 1) * half)
        a = _rmsnorm(attn_ref[rows, :], ga_ref[...]).astype(BF16)
        s = _rmsnorm(ssm_ref[rows, :], gs_ref[...]).astype(BF16)
        mix = _dot(jnp.concatenate([a, s], axis=1), w_ref[...])
        o_ref[rows, :] = _layernorm(DEEPNORM_ALPHA * x_ref[rows, :] + mix, g_ref[...], b_ref[...])


def _mix(attn, ssm, ga, gs, w, x, g, b, *, tm, name):
    m = x.shape[0]
    row = lambda n: pl.BlockSpec((tm, n), lambda i: (i, 0))
    const = lambda a: pl.BlockSpec(a.shape, lambda i: (0, 0))
    return pl.pallas_call(
        _mix_body,
        out_shape=jax.ShapeDtypeStruct((m, D_MODEL), F32),
        grid=(m // tm,),
        in_specs=[row(D_ATT), row(D_SSM), const(ga), const(gs), const(w), row(D_MODEL),
                  const(g), const(b)],
        out_specs=row(D_MODEL),
        compiler_params=_cparams(("parallel",), 6 * tm * D_MODEL * 4 / 2**20 + 24),
        name=name,
    )(attn, ssm, ga, gs, w, x, g, b)


def _store_gatherable(o_ref, y):
    rows = y.shape[0]
    for c in range(ROW_CHUNKS):
        o_ref[pl.ds(c, rows, stride=ROW_PITCH), :] = y[:, c * LANES:(c + 1) * LANES]
    for c in range(ROW_CHUNKS, ROW_PITCH):
        o_ref[pl.ds(c, rows, stride=ROW_PITCH), :] = jnp.zeros((rows, LANES), F32)


def _load_gathered(buf, rows):
    return jnp.concatenate([buf[pl.ds(c, rows, stride=ROW_PITCH), :] for c in range(ROW_CHUNKS)],
                           axis=1)


def _start_row_gather(src_hbm, idx, buf, r, sem):
    pltpu.make_async_copy(src_hbm.at[pl.ds(idx * ROW_PITCH, ROW_CHUNKS), :],
                          buf.at[pl.ds(r * ROW_PITCH, ROW_CHUNKS), :], sem).start()


def _wait_row_gathers(buf, other, rows, sem):
    span = pl.ds(0, rows * ROW_CHUNKS)
    pltpu.make_async_copy(other.at[span, :], buf.at[span, :], sem).wait()


def _mm_ln_body(a1_ref, a2_ref, w_ref, x1_ref, x2_ref, g_ref, b_ref, wr_ref, br_ref,
                o_ref, rows_ref, sel_ref, wts_ref, cnt_ref, run_s, *, tiles1):
    first = pl.program_id(0) < tiles1
    half = a1_ref.shape[0] // ROW_SPLIT
    outs = []
    for r in range(ROW_SPLIT):
        rows = slice(r * half, (r + 1) * half)
        a = jnp.where(first, a1_ref[rows, :].astype(BF16), a2_ref[rows, :].astype(BF16))
        x = jnp.where(first, x1_ref[rows, :], x2_ref[rows, :])
        y = _dot(a, w_ref[...])
        out = _layernorm(DEEPNORM_ALPHA * x + y, g_ref[...], b_ref[...])
        o_ref[rows, :] = out
        outs.append(out)
    out = jnp.concatenate(outs, axis=0)
    _store_gatherable(rows_ref, out)
    _route_tile(out, wr_ref, br_ref, sel_ref, wts_ref, cnt_ref, run_s)


def _mm_ln(a1, a2, w, x1, x2, g, b, w_r, b_r, *, name):
    tm = a2.shape[0]
    assert a1.shape[0] % tm == 0
    tiles1 = a1.shape[0] // tm
    m = a1.shape[0] + tm
    row1 = lambda n: pl.BlockSpec((tm, n), lambda i: (jnp.minimum(i, tiles1 - 1), 0))
    row2 = lambda n: pl.BlockSpec((tm, n), lambda i: (0, 0))
    const = lambda v: pl.BlockSpec(v.shape, lambda i: (0, 0))
    lanes = pl.BlockSpec((tm, ROUTER_LANES), lambda i: (i, 0))
    return pl.pallas_call(
        functools.partial(_mm_ln_body, tiles1=tiles1),
        out_shape=[jax.ShapeDtypeStruct((m, D_MODEL), F32),
                   jax.ShapeDtypeStruct((m * ROW_PITCH, LANES), F32),
                   jax.ShapeDtypeStruct((m, ROUTER_LANES), I32),
                   jax.ShapeDtypeStruct((m, ROUTER_LANES), F32),
                   jax.ShapeDtypeStruct((1, ROUTER_LANES), I32)],
        grid=(tiles1 + 1,),
        in_specs=[row1(a1.shape[1]), row2(a2.shape[1]), const(w), row1(D_MODEL), row2(D_MODEL),
                  const(g), const(b), const(w_r), const(b_r)],
        out_specs=[pl.BlockSpec((tm, D_MODEL), lambda i: (i, 0)),
                   pl.BlockSpec((tm * ROW_PITCH, LANES), lambda i: (i, 0)),
                   lanes, lanes, pl.BlockSpec((1, ROUTER_LANES), lambda i: (0, 0))],
        scratch_shapes=[pltpu.VMEM((1, ROUTER_LANES), F32)],
        compiler_params=_cparams(("arbitrary",), 12 * tm * D_MODEL * 4 / 2**20 + 24),
        name=name,
    )(a1, a2, w, x1, x2, g, b, w_r, b_r)


def _memattn_body(q_ref, k_ref, v_ref, o_ref):
    scale = MEM_HD ** -0.5
    for h in range(MEM_HEADS):
        sl = slice(h * MEM_HD, (h + 1) * MEM_HD)
        s = _dot_nt(q_ref[:, sl].astype(BF16), k_ref[:, sl].astype(BF16)) * scale
        m = jnp.max(s, axis=-1, keepdims=True)
        p = jnp.exp(s - m)
        l = jnp.sum(p, axis=-1, keepdims=True)
        o_ref[:, sl] = (_dot(p.astype(BF16), v_ref[:, sl].astype(BF16)) / l).astype(o_ref.dtype)


def _memattn(q, mem_k, mem_v, *, row0, n_seq, seq, tq, name):
    assert seq % tq == 0 and row0 % tq == 0
    nq = seq // tq
    rb = row0 // tq
    mem_spec = pl.BlockSpec((None, N_MEM, D_MODEL), lambda b, i: (b, 0, 0))
    return pl.pallas_call(
        _memattn_body,
        out_shape=jax.ShapeDtypeStruct((n_seq * seq, D_MODEL), q.dtype),
        grid=(n_seq, nq),
        in_specs=[pl.BlockSpec((tq, D_MODEL), lambda b, i: (rb + b * nq + i, 0)),
                  mem_spec, mem_spec],
        out_specs=pl.BlockSpec((tq, D_MODEL), lambda b, i: (b * nq + i, 0)),
        compiler_params=_cparams(("parallel", "parallel"),
                                 4 * (tq + N_MEM) * D_MODEL * 4 / 2**20 + 8),
        name=name,
    )(q, mem_k, mem_v)


def _memattn_heads_body(q_ref, k_ref, v_ref, c_ref, o_ref):
    scale = MEM_HD ** -0.5
    tq = q_ref.shape[0]
    q = jnp.concatenate([q_ref[:, h * MEM_HD:(h + 1) * MEM_HD] for h in range(MEM_HEADS)], axis=0)
    k = k_ref[...].reshape(N_MEM * MEM_HEADS, MEM_HD).astype(BF16)
    v = v_ref[...].reshape(N_MEM * MEM_HEADS, MEM_HD).astype(BF16)
    s = jnp.where(c_ref[...] > 0, _dot_nt(q.astype(BF16), k) * scale, NEG_INF)
    m = jnp.max(s, axis=-1, keepdims=True)
    p = jnp.exp(s - m)
    l = jnp.sum(p, axis=-1, keepdims=True)
    o = _dot(p.astype(BF16), v) / l
    for h in range(MEM_HEADS):
        o_ref[:, h * MEM_HD:(h + 1) * MEM_HD] = o[h * tq:(h + 1) * tq, :]


def _memattn_heads(q, mem_k, mem_v, *, row0, n_seq, seq, name):
    assert row0 % seq == 0 and seq % SUBLANES == 0
    rb = row0 // seq
    same_head = np.kron(np.eye(MEM_HEADS, dtype=np.float32), np.ones((seq, 1), np.float32))
    same_head = np.tile(same_head, (1, N_MEM))
    mem_spec = pl.BlockSpec((None, N_MEM, MEM_HEADS, MEM_HD), lambda b: (b, 0, 0, 0))
    return pl.pallas_call(
        _memattn_heads_body,
        out_shape=jax.ShapeDtypeStruct((n_seq * seq, D_MODEL), F32),
        grid=(n_seq,),
        in_specs=[pl.BlockSpec((seq, D_MODEL), lambda b: (rb + b, 0)), mem_spec, mem_spec,
                  pl.BlockSpec(same_head.shape, lambda b: (0, 0))],
        out_specs=pl.BlockSpec((seq, D_MODEL), lambda b: (b, 0)),
        compiler_params=_cparams(("parallel",), 8 * N_MEM * D_MODEL * 4 / 2**20 + 8),
        name=name,
    )(q, mem_k, mem_v, jnp.asarray(same_head))


def _route_tile(x, w_ref, b_ref, sel_ref, wts_ref, cnt_ref, run_s):
    tm = x.shape[0]

    @pl.when(pl.program_id(0) == 0)
    def _():
        run_s[...] = jnp.zeros_like(run_s)

    ng, epg = N_EXPERT_GROUPS, EXPERTS_PER_GROUP
    x_hi = x.astype(BF16)
    x_lo = (x - x_hi.astype(F32)).astype(BF16)
    parts = _dot(x_hi, w_ref[...]) + _dot(x_lo, w_ref[...])
    logits = parts + pltpu.roll(parts, shift=ROUTER_LANES // 2, axis=1) + b_ref[...]
    lane = lax.broadcasted_iota(I32, (tm, ROUTER_LANES), 1)
    big = ROUTER_LANES

    def first_argmax(vals):
        mx = jnp.max(vals, axis=-1, keepdims=True)
        idx = jnp.min(jnp.where(vals == mx, lane, big), axis=-1, keepdims=True)
        return mx, idx

    gl = jnp.where(lane < ng, logits, NEG_INF)
    gmax, gsel = first_argmax(gl)
    g_w = 1.0 / jnp.sum(jnp.exp(gl - gmax), axis=-1, keepdims=True)
    lo = ng + gsel * epg
    el = jnp.where(jnp.logical_and(lane >= lo, lane < lo + epg), logits, NEG_INF)
    v1, i1 = first_argmax(el)
    v2, i2 = first_argmax(jnp.where(lane == i1, NEG_INF, el))
    e21 = jnp.exp(v2 - v1)
    w1 = g_w / (1.0 + e21)
    w2 = g_w * e21 / (1.0 + e21)

    onehot = jnp.logical_or(lane == i1, lane == i2)
    r = lax.broadcasted_iota(I32, (tm, tm), 0)
    cc = lax.broadcasted_iota(I32, (tm, tm), 1)
    tri = (cc < r).astype(BF16)
    before = _dot(tri, onehot.astype(BF16)) + run_s[...]
    rank1 = jnp.sum(jnp.where(lane == i1, before, 0.0), axis=-1, keepdims=True).astype(I32)
    rank2 = jnp.sum(jnp.where(lane == i2, before, 0.0), axis=-1, keepdims=True).astype(I32)
    run_s[...] = run_s[...] + jnp.sum(onehot.astype(F32), axis=0, keepdims=True)

    sel = jnp.where(lane == 0, i1 - ng, jnp.where(lane == 1, i2 - ng,
                    jnp.where(lane == 2, rank1, jnp.where(lane == 3, rank2, 0))))
    sel_ref[...] = sel
    wts_ref[...] = jnp.where(lane == 0, w1, jnp.where(lane == 1, w2, 0.0))
    cnt_ref[...] = run_s[...].astype(I32)


def _router_weights(w_r1, b_r1, w_r2, b_r2):
    ng, ne = N_EXPERT_GROUPS, N_EXPERTS
    half = ROUTER_LANES // 2
    assert ng + ne <= half
    w_r = jnp.concatenate([w_r1, w_r2.reshape(D_MODEL, ne),
                           jnp.zeros((D_MODEL, half - ng - ne), F32)], axis=1)
    w_hi = w_r.astype(BF16)
    w_lo = (w_r - w_hi.astype(F32)).astype(BF16)
    b_r = jnp.concatenate([b_r1, b_r2.reshape(ne), jnp.zeros((half - ng - ne,), F32)])
    return jnp.concatenate([w_hi, w_lo], axis=1), jnp.concatenate([b_r, b_r]).reshape(1, ROUTER_LANES)


def _dispatch_body(eid_ref, rank_ref, off_ref, pad0_ref, npad_ref, nact_ref, x_hbm, xs_hbm, xbuf,
                   zero_s, in_sem, out_sem, pad_sem, *, td, n_tiles):
    i = pl.program_id(0)
    n = pl.num_programs(0)
    tile_rows = MOE_TILE * ROW_PITCH
    in_rows = td * ROW_PITCH

    def row_copy(src, dst_row, s):
        return pltpu.make_async_copy(src, xs_hbm.at[pl.ds(dst_row * ROW_PITCH, ROW_PITCH), :], s)

    def tile_load(t):
        s = t % DISPATCH_SLOTS
        return pltpu.make_async_copy(x_hbm.at[pl.ds(t * in_rows, in_rows), :], xbuf.at[s],
                                     in_sem.at[s])

    def wait_scatter(par):
        for _ in range(2):
            pltpu.make_async_copy(xbuf.at[0], xs_hbm.at[pl.ds(0, in_rows), :],
                                  out_sem.at[par]).wait()

    @pl.when(i == 0)
    def _():
        for t in range(DISPATCH_SLOTS - 1):
            @pl.when(t < n)
            def _(t=t):
                tile_load(t).start()
        zero_s[...] = jnp.zeros_like(zero_s)

        def pad_copies(e):
            out = []
            for bit in reversed(range(MOE_TILE.bit_length() - 1)):
                rows = (1 << bit) * ROW_PITCH
                first = (pad0_ref[e] + (npad_ref[e] >> (bit + 1) << (bit + 1))) * ROW_PITCH
                out.append((jnp.bitwise_and(npad_ref[e] >> bit, 1) == 1, pltpu.make_async_copy(
                    zero_s.at[pl.ds(0, rows), :], xs_hbm.at[pl.ds(first, rows), :], pad_sem.at[0])))
            return out

        for e in range(N_EXPERTS):
            for on, cp in pad_copies(e):
                pl.when(on)(cp.start)
        for e in range(N_EXPERTS):
            for on, cp in pad_copies(e):
                pl.when(on)(cp.wait)

        def zero_tile(t, carry):
            parts = [pltpu.make_async_copy(
                zero_s.at[pl.ds(0, MOE_TILE), :],
                xs_hbm.at[pl.ds(t * tile_rows + j * MOE_TILE, MOE_TILE), :],
                pad_sem.at[0]) for j in range(ROW_PITCH)]
            for cp in parts:
                cp.start()
            for cp in parts:
                cp.wait()
            return carry

        lax.fori_loop(nact_ref[0], n_tiles, zero_tile, 0)

    slot = i % DISPATCH_SLOTS
    par = i % 2
    tile_load(i).wait()
    base = i * td * 2
    for r in range(td):
        for k in range(2):
            j = base + 2 * r + k
            row_copy(xbuf.at[slot, pl.ds(r * ROW_PITCH, ROW_PITCH), :],
                     off_ref[eid_ref[j]] + rank_ref[j], out_sem.at[par]).start()

    @pl.when(i > 0)
    def _():
        wait_scatter(1 - par)

    @pl.when(i + DISPATCH_SLOTS - 1 < n)
    def _():
        tile_load(i + DISPATCH_SLOTS - 1).start()

    @pl.when(i == n - 1)
    def _():
        wait_scatter(par)


def _moe_dispatch(x_rows, eid, rank, row_off, pad_start, pad_count, nact, *, td, n_tiles):
    n = x_rows.shape[0] // ROW_PITCH
    grid_spec = pltpu.PrefetchScalarGridSpec(
        num_scalar_prefetch=6,
        grid=(n // td,),
        in_specs=[pl.BlockSpec(memory_space=pl.ANY)],
        out_specs=pl.BlockSpec(memory_space=pl.ANY),
        scratch_shapes=[pltpu.VMEM((DISPATCH_SLOTS, td * ROW_PITCH, LANES), F32),
                        pltpu.VMEM((MOE_TILE // 2 * ROW_PITCH, LANES), F32),
                        pltpu.SemaphoreType.DMA((DISPATCH_SLOTS,)),
                        pltpu.SemaphoreType.DMA((2,)),
                        pltpu.SemaphoreType.DMA((1,))],
    )
    return pl.pallas_call(
        functools.partial(_dispatch_body, td=td, n_tiles=n_tiles),
        out_shape=jax.ShapeDtypeStruct((n_tiles * MOE_TILE * ROW_PITCH, LANES), F32),
        grid_spec=grid_spec,
        compiler_params=_cparams(("arbitrary",), 16),
        name="moe_dispatch",
    )(eid, rank, row_off, pad_start, pad_count, nact, x_rows)


def _moe_body(te_ref, ord_ref, nxt_ref, nact_ref, x_ref, wg_hbm, wu_hbm, wd_hbm, o_ref,
              wg_f, wu_f, wd_f, wsem, wg_s, wu_s, wd_s):
    i = pl.program_id(0)
    nact = nact_ref[0]
    tm = MOE_TILE

    def weight_copies(expert, ws):
        return [pltpu.make_async_copy(hbm.at[expert], stage.at[ws], wsem.at[ws])
                for hbm, stage in ((wg_hbm, wg_f), (wu_hbm, wu_f), (wd_hbm, wd_f))]

    def ffn(wg, wu, wd):
        x = _load_gathered(x_ref, tm).astype(BF16)
        hg = _dot(x, wg)
        hu = _dot(x, wu)
        h = hg * (1.0 / (1.0 + jnp.exp(-hg))) * hu
        _store_gatherable(o_ref, _dot(h.astype(BF16), wd))

    def tile_step():
        first = jnp.logical_or(i == 0, te_ref[i] != te_ref[jnp.maximum(i - 1, 0)])

        @pl.when(first)
        def _():
            ws = ord_ref[i] % 2
            for cp in weight_copies(te_ref[i], ws):
                cp.wait()

            @pl.when(nxt_ref[i] >= 0)
            def _():
                for cp in weight_copies(nxt_ref[i], 1 - ws):
                    cp.start(priority=1)

            wg = wg_f[ws].astype(BF16)
            wu = wu_f[ws].astype(BF16)
            wd = wd_f[ws].astype(BF16)
            wg_s[...] = wg
            wu_s[...] = wu
            wd_s[...] = wd
            ffn(wg, wu, wd)

        @pl.when(jnp.logical_not(first))
        def _():
            ffn(wg_s[...], wu_s[...], wd_s[...])

    @pl.when(i == 0)
    def _():
        for cp in weight_copies(te_ref[0], 0):
            cp.start(priority=1)

    @pl.when(i < nact)
    def _():
        tile_step()

    @pl.when(i >= nact)
    def _():
        o_ref[...] = jnp.zeros_like(o_ref)


def _moe_experts(x_sorted, w_gate, w_up, w_down, tile_expert, tile_ord, tile_next, nact, *,
                 n_tiles):
    tm = MOE_TILE
    in_map = lambda i, te, od, nx, n: (jnp.minimum(i, n[0] - 1), 0)
    any_spec = pl.BlockSpec(memory_space=pl.ANY)
    grid_spec = pltpu.PrefetchScalarGridSpec(
        num_scalar_prefetch=4,
        grid=(n_tiles,),
        in_specs=[pl.BlockSpec((tm * ROW_PITCH, LANES), in_map), any_spec, any_spec, any_spec],
        out_specs=pl.BlockSpec((tm * ROW_PITCH, LANES), lambda i, te, od, nx, n: (i, 0)),
        scratch_shapes=[pltpu.VMEM((2, D_MODEL, D_EXPERT), F32),
                        pltpu.VMEM((2, D_MODEL, D_EXPERT), F32),
                        pltpu.VMEM((2, D_EXPERT, D_MODEL), F32),
                        pltpu.SemaphoreType.DMA((2,)),
                        pltpu.VMEM((D_MODEL, D_EXPERT), BF16),
                        pltpu.VMEM((D_MODEL, D_EXPERT), BF16),
                        pltpu.VMEM((D_EXPERT, D_MODEL), BF16)],
    )
    return pl.pallas_call(
        _moe_body,
        out_shape=jax.ShapeDtypeStruct((n_tiles * tm * ROW_PITCH, LANES), F32),
        grid_spec=grid_spec,
        compiler_params=_cparams(("arbitrary",), 48),
        name="moe_experts",
    )(tile_expert, tile_ord, tile_next, nact, x_sorted, w_gate, w_up, w_down)


def _combine_body(eid_ref, rank_ref, off_ref, ys_hbm, wts_ref, x_ref, g_ref, b_ref, o1_ref, o2_ref,
                  buf, sem, *, tc, tiles1):
    i = pl.program_id(0)
    n = pl.num_programs(0)
    slot = i % GATHER_SLOTS
    ahead = GATHER_SLOTS - 1

    def issue_gather(tile, slot_):
        base = tile * tc * 2
        for r in range(tc):
            for k in range(2):
                j = base + 2 * r + k
                _start_row_gather(ys_hbm, off_ref[eid_ref[j]] + rank_ref[j], buf.at[slot_, k], r,
                                  sem.at[slot_])

    @pl.when(i == 0)
    def _():
        for t in range(ahead):
            @pl.when(t < n)
            def _(t=t):
                issue_gather(t, t)

    for k in range(2):
        _wait_row_gathers(buf.at[slot, k], buf.at[(i + 1) % GATHER_SLOTS, k], tc, sem.at[slot])

    @pl.when(i + ahead < n)
    def _():
        issue_gather(i + ahead, (i + ahead) % GATHER_SLOTS)

    w = wts_ref[...]
    moe = (w[:, 0:1] * _load_gathered(buf.at[slot, 0], tc)
           + w[:, 1:2] * _load_gathered(buf.at[slot, 1], tc))
    out = _layernorm(DEEPNORM_ALPHA * x_ref[...] + moe, g_ref[...], b_ref[...])

    @pl.when(i < tiles1)
    def _():
        o1_ref[...] = out

    @pl.when(i >= tiles1)
    def _():
        o2_ref[...] = out


def _moe_combine(ys, eid, rank, row_off, wts, x, g, b, *, tc, n_first):
    m = x.shape[0]
    assert n_first % tc == 0 and (m - n_first) % tc == 0
    tiles1 = n_first // tc
    grid_spec = pltpu.PrefetchScalarGridSpec(
        num_scalar_prefetch=3,
        grid=(m // tc,),
        in_specs=[pl.BlockSpec(memory_space=pl.ANY),
                  pl.BlockSpec((tc, ROUTER_LANES), lambda i, *_: (i, 0)),
                  pl.BlockSpec((tc, D_MODEL), lambda i, *_: (i, 0)),
                  pl.BlockSpec((1, D_MODEL), lambda i, *_: (0, 0)),
                  pl.BlockSpec((1, D_MODEL), lambda i, *_: (0, 0))],
        out_specs=[pl.BlockSpec((tc, D_MODEL), lambda i, *_: (jnp.minimum(i, tiles1 - 1), 0)),
                   pl.BlockSpec((tc, D_MODEL), lambda i, *_: (jnp.maximum(i - tiles1, 0), 0))],
        scratch_shapes=[pltpu.VMEM((GATHER_SLOTS, 2, tc * ROW_PITCH, LANES), F32),
                        pltpu.SemaphoreType.DMA((GATHER_SLOTS,))],
    )
    return pl.pallas_call(
        functools.partial(_combine_body, tc=tc, tiles1=tiles1),
        out_shape=[jax.ShapeDtypeStruct((n_first, D_MODEL), F32),
                   jax.ShapeDtypeStruct((m - n_first, D_MODEL), F32)],
        grid_spec=grid_spec,
        compiler_params=_cparams(("arbitrary",), 16 * tc * D_MODEL * 4 / 2**20 + 8),
        name="moe_combine_ln3",
    )(eid, rank, row_off, ys, wts, x, g, b)


def _moe(x, x_rows, sel, wts, cnt, w_gate, w_up, w_down, g, b, *, n_first, tc):
    n = x.shape[0]
    ng, ne = N_EXPERT_GROUPS, N_EXPERTS

    tm = MOE_TILE
    n_tiles = (2 * n) // tm + ne
    counts = cnt[0, ng:ng + ne]
    tiles_per = (counts + tm - 1) // tm
    tile_end = jnp.cumsum(tiles_per)
    row_off = (tile_end - tiles_per) * tm
    nact = tile_end[-1]
    a_eid, a_rank = sel[:, 0:2].reshape(-1), sel[:, 2:4].reshape(-1)
    row_off = row_off.astype(I32)
    tile_ids = jnp.minimum(jnp.arange(n_tiles, dtype=I32), nact - 1)
    tile_expert = jnp.sum((tile_end[None, :] <= tile_ids[:, None]).astype(I32), axis=1)
    used = tiles_per > 0
    eid = jnp.arange(ne, dtype=I32)
    ordinal = jnp.cumsum(used.astype(I32)) - 1
    later = jnp.where(jnp.logical_and(used[None, :], eid[None, :] > eid[:, None]), eid[None, :], ne)
    nxt = jnp.min(later, axis=1)
    nxt = jnp.where(nxt == ne, -1, nxt)

    nact = nact.reshape(1).astype(I32)
    x_sorted = _moe_dispatch(x_rows, a_eid, a_rank, row_off, (row_off + counts).astype(I32),
                             (tiles_per * tm - counts).astype(I32), nact, td=tc, n_tiles=n_tiles)
    ys = _moe_experts(x_sorted, w_gate, w_up, w_down, tile_expert, ordinal[tile_expert],
                      nxt[tile_expert], nact, n_tiles=n_tiles)
    return _moe_combine(ys, a_eid, a_rank, row_off, wts, x, g, b, tc=tc, n_first=n_first)


def _row_tile(m, cap):
    best = SUBLANES
    for t in range(SUBLANES, cap + 1, SUBLANES):
        if m % t == 0:
            best = t
    return best


def kernel(x_prompt, x_sample, cache_win_k, cache_win_v, state_ssm_re, state_ssm_im, cache_mem_k, cache_mem_v, mem_prompt, w_in, ssm_lam_re, ssm_lam_im, ssm_log_dt, ssm_b_re, ssm_b_im, ssm_c_re, ssm_c_im, ssm_d, w_glu, g_attn, g_ssm, w_out, ln1_g, ln1_b, w_mq, w_mk, w_mv, w_mo, ln2_g, ln2_b, w_r1, b_r1, w_r2, b_r2, w_gate, w_up, w_down, ln3_g, ln3_b):
    nb, seq, d = x_prompt.shape
    ns, dseq, _ = x_sample.shape
    n_p, n_s = nb * seq, ns * dseq
    n = n_p + n_s
    l = 0
    row2 = lambda v: v[l].reshape(1, -1)

    x_p, x_s = x_prompt.reshape(n_p, d), x_sample.reshape(n_s, d)
    tm_p = _row_tile(n_p, 1024)
    tm_ln = _row_tile(n_p, 512)
    assert n_p % n_s == 0 and n_s % SUBLANES == 0

    proj_p = _matmul(x_p, w_in[l], tm=tm_p, tn=1024, name="proj_in_prompt")
    proj_s = _matmul(x_s, w_in[l], tm=n_s, tn=1024, name="proj_in_sample")

    attn_p = _attn_prompt(proj_p, n_batch=nb, seq=seq)
    attn_s = _attn_sample(proj_s, cache_win_k[l], cache_win_v[l], row0=0, n_seq=ns, n_new=dseq)

    seg_len = seq // SSM_SEGMENTS
    prm = _ssm_params(ssm_lam_re[l], ssm_lam_im[l], ssm_log_dt[l], ssm_b_re[l], ssm_b_im[l],
                      ssm_c_re[l], ssm_c_im[l], ssm_d[l], seg_len)
    zeros = jnp.zeros((nb * SSM_SEGMENTS, N_SSM_GROUPS * SSM_STATE), F32)
    tl = _row_tile(seg_len, 32)
    end_re, end_im = _ssm_scan(proj_p, prm, zeros, zeros, seq_len=seg_len, tl=tl, nseg=1,
                               emit_y=False, exact_in=False, name="ssm_state_prompt")
    yg_p, fin_re, fin_im = _ssm_scan(proj_p, prm, end_re, end_im, seq_len=seg_len, tl=tl,
                                     nseg=SSM_SEGMENTS, emit_y=True, exact_in=False,
                                     name="ssm_scan_prompt")
    last = SSM_SEGMENTS - 1
    ssm_re_p = fin_re.reshape(nb, SSM_SEGMENTS, N_SSM_GROUPS, SSM_STATE)[:, last]
    ssm_im_p = fin_im.reshape(nb, SSM_SEGMENTS, N_SSM_GROUPS, SSM_STATE)[:, last]

    h0_re = state_ssm_re[l].reshape(ns, -1)
    h0_im = state_ssm_im[l].reshape(ns, -1)
    yg_s, ssm_re_s, ssm_im_s = _ssm_scan(proj_s, prm, h0_re, h0_im, seq_len=dseq, tl=dseq, nseg=1,
                                         emit_y=True, exact_in=True, name="ssm_scan_sample")
    w_glu_b = w_glu[l].astype(BF16)
    ssm_out_p = _glu(yg_p, w_glu_b, tm=tm_p, name="ssm_glu_prompt")
    ssm_out_s = _glu(yg_s, w_glu_b, tm=n_s, name="ssm_glu_sample")

    mix_args = (row2(g_attn), row2(g_ssm), w_out[l].astype(BF16))
    ln1 = (row2(ln1_g), row2(ln1_b))
    x1_p = _mix(attn_p, ssm_out_p, *mix_args, x_p, *ln1, tm=tm_ln, name="mix_out_ln1_prompt")
    x1_s = _mix(attn_s, ssm_out_s, *mix_args, x_s, *ln1, tm=n_s, name="mix_out_ln1_sample")

    mem_rows = mem_prompt.reshape(nb * N_MEM, d)
    mem_k = _matmul(mem_rows, w_mk[l], tm=nb * N_MEM, tn=1024, name="mem_k")
    mem_v = _matmul(mem_rows, w_mv[l], tm=nb * N_MEM, tn=1024, name="mem_v")
    q_p = _matmul(x1_p, w_mq[l], tm=tm_p, tn=1024, name="mem_q_prompt", out_dtype=BF16)
    q_s = _matmul(x1_s, w_mq[l], tm=n_s, tn=1024, name="mem_q_sample")
    o_p = _memattn(q_p, mem_k.reshape(nb, N_MEM, d), mem_v.reshape(nb, N_MEM, d),
                   row0=0, n_seq=nb, seq=seq, tq=_row_tile(seq, 512), name="memattn_prompt")
    o_s = _memattn_heads(q_s, cache_mem_k[l], cache_mem_v[l], row0=0, n_seq=ns, seq=dseq,
                         name="memattn_sample")
    w_r, b_r = _router_weights(w_r1[l], b_r1[l], w_r2[l], b_r2[l])
    x2, x2_rows, sel, wts, cnt = _mm_ln(o_p, o_s, w_mo[l].astype(BF16), x1_p, x1_s, row2(ln2_g),
                                        row2(ln2_b), w_r, b_r, name="mem_out_ln2_route")

    y_p, y_s = _moe(x2, x2_rows, sel, wts, cnt, w_gate[l], w_up[l], w_down[l], row2(ln3_g),
                    row2(ln3_b), n_first=n_p, tc=_row_tile(n_s, 256))

    y_p = y_p.reshape(nb, seq, d)
    y_s = y_s.reshape(ns, dseq, d)
    wp = min(max(w for w, _ in DILATIONS), seq)
    k_p, v_p = _kv_window(proj_p, n_batch=nb, seq=seq, window=wp, tr=_row_tile(wp, 512))
    k_s = proj_s[:, D_ATT:2 * D_ATT].reshape(ns, dseq, ATT_HEADS, ATT_HD)
    v_s = proj_s[:, 2 * D_ATT:3 * D_ATT].reshape(ns, dseq, ATT_HEADS, ATT_HD)
    state = lambda v, b_: v.reshape(1, b_, N_SSM_GROUPS, SSM_STATE)
    return (y_p, y_s, k_p[None], v_p[None], k_s[None], v_s[None],
            state(ssm_re_p, nb), state(ssm_im_p, nb), state(ssm_re_s, ns), state(ssm_im_s, ns),
            mem_k.reshape(1, nb, N_MEM, MEM_HEADS, MEM_HD),
            mem_v.reshape(1, nb, N_MEM, MEM_HEADS, MEM_HD))
```

```python
import functools
import math

import numpy as np
import jax
import jax.numpy as jnp
from jax import lax
from jax.experimental import pallas as pl
from jax.experimental.pallas import tpu as pltpu

F32 = jnp.float32
BF16 = jnp.bfloat16
I32 = jnp.int32

D_MODEL = 2048
PAST_LEN = 8192
D_ATT = D_MODEL // 2
ATT_HEADS = 8
ATT_HD = D_ATT // ATT_HEADS
DILATIONS = ((128, 1), (512, 4), (2048, 16))
D_SSM = D_MODEL - D_ATT
SSM_GROUP_CH = 16
N_SSM_GROUPS = D_SSM // SSM_GROUP_CH
SSM_STATE = 64
N_MEM = 256
MEM_HEADS = 4
MEM_HD = D_MODEL // MEM_HEADS
N_EXPERT_GROUPS = 4
EXPERTS_PER_GROUP = 8
N_EXPERTS = N_EXPERT_GROUPS * EXPERTS_PER_GROUP
D_EXPERT = D_MODEL // 4
DEPTH = 1
DEEPNORM_ALPHA = (2.0 * DEPTH) ** 0.25
LN_EPS = 1e-5
RMS_EPS = 1e-6

LANES = 128
SUBLANES = 8
ROW_CHUNKS = D_MODEL // LANES
ROW_PITCH = ROW_CHUNKS + 1
Q_BLOCK = 128
ATTN_GROUP = 8
DEINTERLEAVE = 4
SSM_LANE_TILE = 128
SSM_GROUPS_PER_TILE = SSM_LANE_TILE // SSM_GROUP_CH
SSM_STATES_PER_TILE = SSM_GROUPS_PER_TILE * SSM_STATE
SSM_SEGMENTS = 8
MOE_TILE = 256
GATHER_SLOTS = 3
DISPATCH_SLOTS = 3
ROW_SPLIT = 2
ROUTER_LANES = 128
NEG_INF = float("-inf")


def _cparams(semantics, vmem_mib):
    return pltpu.CompilerParams(dimension_semantics=semantics,
                                vmem_limit_bytes=int(vmem_mib) << 20)


def _layernorm(y, g, b):
    mu = jnp.mean(y, axis=-1, keepdims=True)
    yc = y - mu
    var = jnp.mean(yc * yc, axis=-1, keepdims=True)
    return yc * lax.rsqrt(var + LN_EPS) * g + b


def _rmsnorm(v, g):
    return v * lax.rsqrt(jnp.mean(v * v, axis=-1, keepdims=True) + RMS_EPS) * g


def _dot(a, b):
    return jnp.dot(a, b, preferred_element_type=F32)


def _dot_nt(a, b):
    return lax.dot_general(a, b, (((1,), (1,)), ((), ())), preferred_element_type=F32)


def _mm_body(x_ref, w_ref, o_ref, wb_s):
    @pl.when(pl.program_id(1) == 0)
    def _():
        wb_s[...] = w_ref[...].astype(BF16)

    o_ref[...] = _dot(x_ref[...].astype(BF16), wb_s[...]).astype(o_ref.dtype)


def _matmul(x, w, *, tm, tn, name, out_dtype=F32):
    m, k = x.shape
    n = w.shape[1]
    vmem = (2 * (tm * k * 4 + k * tn * 4 + tm * tn * 4) + k * tn * 2 + tm * k * 2) / 2**20 + 8
    return pl.pallas_call(
        _mm_body,
        out_shape=jax.ShapeDtypeStruct((m, n), out_dtype),
        grid=(n // tn, m // tm),
        in_specs=[pl.BlockSpec((tm, k), lambda j, i: (i, 0)),
                  pl.BlockSpec((k, tn), lambda j, i: (0, j))],
        out_specs=pl.BlockSpec((tm, tn), lambda j, i: (i, j)),
        scratch_shapes=[pltpu.VMEM((k, tn), BF16)],
        compiler_params=_cparams(("parallel", "arbitrary"), vmem),
        name=name,
    )(x, w)


def _kv_window_body(k_ref, v_ref, ko_ref, vo_ref):
    for h in range(ATT_HEADS):
        sl = slice(h * ATT_HD, (h + 1) * ATT_HD)
        ko_ref[:, h, :] = k_ref[:, sl]
        vo_ref[:, h, :] = v_ref[:, sl]


def _kv_window(proj, *, n_batch, seq, window, tr):
    assert window % tr == 0 and seq % tr == 0
    per, first = seq // tr, (seq - window) // tr
    col = lambda c: pl.BlockSpec((tr, D_ATT), lambda b, t: (b * per + first + t, c))
    out_spec = pl.BlockSpec((None, tr, ATT_HEADS, ATT_HD), lambda b, t: (b, t, 0, 0))
    out_shape = jax.ShapeDtypeStruct((n_batch, window, ATT_HEADS, ATT_HD), F32)
    return pl.pallas_call(
        _kv_window_body,
        out_shape=[out_shape, out_shape],
        grid=(n_batch, window // tr),
        in_specs=[col(1), col(2)],
        out_specs=[out_spec, out_spec],
        compiler_params=_cparams(("parallel", "parallel"), 8 * tr * D_ATT * 4 / 2**20 + 8),
        name="kv_window",
    )(proj, proj)


def _attn_prompt_body(q_ref, k_ref, v_ref, o_ref, kt_s, va_s, on_s, lse_s, q4_s, k4_s, v4_s, *,
                      seq, dilations):
    scale = ATT_HD ** -0.5
    nblk = seq // Q_BLOCK
    qi = lax.broadcasted_iota(I32, (Q_BLOCK, Q_BLOCK), 0)
    kj = lax.broadcasted_iota(I32, (Q_BLOCK, Q_BLOCK), 1)
    cur_ok = kj <= qi
    prev_ok = kj >= qi
    va_s[:, :, ATT_HD:] = jnp.ones((nblk, Q_BLOCK, ATT_HD), BF16)

    quarter = seq // DEINTERLEAVE
    piece = 256

    def deinterleave(c, carry):
        for r in range(DEINTERLEAVE):
            src = pl.ds(r + c * piece * DEINTERLEAVE, piece, stride=DEINTERLEAVE)
            dst = pl.ds(pl.multiple_of(r * quarter + c * piece, piece), piece)
            q4_s[dst, :] = q_ref[src, :]
            k4_s[dst, :] = k_ref[src, :]
            v4_s[dst, :] = v_ref[src, :]
        return carry

    lax.fori_loop(0, quarter // piece, deinterleave, 0)

    for br, (_, d) in enumerate(dilations):
        span = d * Q_BLOCK
        nb = seq // span
        inner = d // DEINTERLEAVE if d % DEINTERLEAVE == 0 else 0
        qs, ks, vs = (q_ref, k_ref, v_ref) if inner == 0 else (q4_s, k4_s, v4_s)

        def stream_rows(t, d=d, span=span, nb=nb, inner=inner):
            r = t // nb
            ib = t % nb
            natural = pl.ds(r + ib * span, Q_BLOCK, stride=d)
            if inner == 0:
                return r, ib, natural, natural
            start = (r % DEINTERLEAVE) * quarter + r // DEINTERLEAVE + ib * Q_BLOCK * inner
            return r, ib, natural, pl.ds(start, Q_BLOCK, stride=inner)

        def prep(g, carry, stream_rows=stream_rows, ks=ks, vs=vs):
            loaded = []
            for j in range(ATTN_GROUP):
                t = g * ATTN_GROUP + j
                _, _, _, rows = stream_rows(t)
                loaded.append((t, ks[rows, :], vs[rows, :]))
            for t, kk, vv in loaded:
                kt_s[t] = jnp.transpose(kk).astype(BF16)
                va_s[t, :, 0:ATT_HD] = vv.astype(BF16)
            return carry

        lax.fori_loop(0, nblk // ATTN_GROUP, prep, 0)

        def group(g, carry, br=br, nb=nb, stream_rows=stream_rows, qs=qs):
            scores = []
            for j in range(ATTN_GROUP):
                t = g * ATTN_GROUP + j
                r, ib, rows, src_rows = stream_rows(t)
                tp = jnp.maximum(t - 1, r * nb)
                q = (qs[src_rows, :] * scale).astype(BF16)
                s = _dot(q, jnp.concatenate([kt_s[tp], kt_s[t]], axis=1))
                scores.append((t, tp, ib, rows, s))
            probs = []
            for t, tp, ib, rows, s in scores:
                ok = jnp.concatenate([jnp.logical_and(prev_ok, ib > 0), cur_ok], axis=1)
                s = jnp.where(ok, s, NEG_INF)
                m = jnp.max(s, axis=-1, keepdims=True)
                probs.append((t, tp, rows, m, jnp.exp(s - m).astype(BF16)))
            outs = [(rows, m, _dot(p, jnp.concatenate([va_s[tp], va_s[t]], axis=0)))
                    for t, tp, rows, m, p in probs]
            for rows, m, al in outs:
                l = al[:, ATT_HD:]
                on_s[br, rows, :] = al[:, :ATT_HD] / l
                lse_s[br, rows, :] = m + jnp.log(l)
            return carry

        lax.fori_loop(0, nblk // ATTN_GROUP, group, 0)

    chunk = 256
    nbr = len(dilations)

    def merge(c, carry):
        rows = pl.ds(pl.multiple_of(c * chunk, chunk), chunk)
        ls = [lse_s[b, rows, :] for b in range(nbr)]
        mx = functools.reduce(jnp.maximum, ls)
        es = [jnp.exp(li - mx) for li in ls]
        num = sum(es[b] * on_s[b, rows, :] for b in range(nbr))
        o_ref[rows, :] = num / sum(es)
        return carry

    lax.fori_loop(0, seq // chunk, merge, 0)


def _attn_prompt(proj, *, n_batch, seq, dilations=DILATIONS):
    for w, d in dilations:
        assert w // d == Q_BLOCK and seq % (d * Q_BLOCK) == 0
    nbr = len(dilations)
    nblk = seq // Q_BLOCK
    assert nblk % ATTN_GROUP == 0
    blk = lambda off: pl.BlockSpec((seq, ATT_HD), lambda b, h, off=off: (b, off + h))
    assert seq % (DEINTERLEAVE * 256) == 0
    vmem = ((4 * 2 + 2 * nbr + 3) * seq * ATT_HD * 4 + 3 * seq * ATT_HD * 2) / 2**20 + 8
    return pl.pallas_call(
        functools.partial(_attn_prompt_body, seq=seq, dilations=dilations),
        out_shape=jax.ShapeDtypeStruct((n_batch * seq, D_ATT), F32),
        grid=(n_batch, ATT_HEADS),
        in_specs=[blk(0), blk(ATT_HEADS), blk(2 * ATT_HEADS)],
        out_specs=pl.BlockSpec((seq, ATT_HD), lambda b, h: (b, h)),
        scratch_shapes=[pltpu.VMEM((nblk, ATT_HD, Q_BLOCK), BF16),
                        pltpu.VMEM((nblk, Q_BLOCK, 2 * ATT_HD), BF16),
                        pltpu.VMEM((nbr, seq, ATT_HD), F32),
                        pltpu.VMEM((nbr, seq, ATT_HD), F32)]
        + [pltpu.VMEM((seq, ATT_HD), F32)] * 3,
        compiler_params=_cparams(("parallel", "parallel"), vmem),
        name="attn_prompt",
    )(proj, proj, proj)


def _sample_key_multiplicity(n_new, n_cache, past_len, dilations):
    d_max = max(d for _, d in dilations)
    tail = max(w for w, d in dilations if d != d_max)
    assert past_len % d_max == 0 and n_cache % d_max == 0 and n_new <= d_max // 2
    assert tail % d_max == 0 and tail <= n_cache
    half = d_max // 2
    n_grid = (n_cache - tail) // d_max
    kv_start = past_len - n_cache
    grid_rows = (np.arange(n_grid)[:, None] * d_max + np.arange(half)[None, :]).reshape(-1)
    tail_rows = n_cache - tail + np.arange(tail)
    new_rows = n_cache + np.arange(n_new)
    qpos = past_len + np.arange(n_new)

    def mult(rows):
        kpos = kv_start + rows
        delta = qpos[:, None] - kpos[None, :]
        c = np.zeros(delta.shape, np.float32)
        for w, d in dilations:
            c += ((delta >= 0) & (delta <= w) & (delta % d == 0) & (kpos[None, :] >= kv_start))
        return c

    fetched = np.zeros(n_cache + n_new, bool)
    fetched[grid_rows] = True
    fetched[tail_rows] = True
    fetched[new_rows] = True
    assert not mult(np.nonzero(~fetched)[0]).any()
    return mult(grid_rows), mult(tail_rows), mult(new_rows), n_grid, tail, half, d_max


def _attn_sample_body(q_ref, kn_ref, vn_ref, kg_ref, kt_ref, vg_ref, vt_ref,
                      cg_ref, ct_ref, cn_ref, o_ref):
    scale = ATT_HD ** -0.5
    heads = lambda ref: jnp.concatenate(
        [ref[:, h * ATT_HD:(h + 1) * ATT_HD] for h in range(ATT_HEADS)], axis=0)
    q = (heads(q_ref) * scale).astype(BF16)
    kn = heads(kn_ref).astype(BF16)
    vn = heads(vn_ref).astype(BF16)
    flat = lambda ref: ref[...].reshape(-1, ATT_HD).astype(BF16)
    cg, ct, cn = cg_ref[...], ct_ref[...], cn_ref[...]
    sg = jnp.where(cg > 0, _dot_nt(q, flat(kg_ref)), NEG_INF)
    st = jnp.where(ct > 0, _dot_nt(q, flat(kt_ref)), NEG_INF)
    sn = jnp.where(cn > 0, _dot_nt(q, kn), NEG_INF)
    m = jnp.maximum(jnp.maximum(jnp.max(sg, axis=-1, keepdims=True),
                                jnp.max(st, axis=-1, keepdims=True)),
                    jnp.max(sn, axis=-1, keepdims=True))
    pg = cg * jnp.exp(sg - m)
    pt = ct * jnp.exp(st - m)
    pn = cn * jnp.exp(sn - m)
    l = (jnp.sum(pg, axis=-1, keepdims=True) + jnp.sum(pt, axis=-1, keepdims=True)
         + jnp.sum(pn, axis=-1, keepdims=True))
    acc = (_dot(pg.astype(BF16), flat(vg_ref)) + _dot(pt.astype(BF16), flat(vt_ref))
           + _dot(pn.astype(BF16), vn))
    out = acc / l
    n_new = q_ref.shape[0]
    for h in range(ATT_HEADS):
        o_ref[:, h * ATT_HD:(h + 1) * ATT_HD] = out[h * n_new:(h + 1) * n_new, :]


def _attn_sample(proj, win_k, win_v, *, row0, n_seq, n_new, past_len=PAST_LEN,
                 dilations=DILATIONS):
    n_cache = win_k.shape[1]
    cg, ct, cn, n_grid, tail, half, d_max = _sample_key_multiplicity(
        n_new, n_cache, past_len, dilations)
    assert row0 % n_new == 0 and n_new % SUBLANES == 0 and n_cache % tail == 0
    eye = np.eye(ATT_HEADS, dtype=np.float32)
    key_major = lambda c: np.einsum("tk,hg->htkg", c, eye).reshape(ATT_HEADS * n_new, -1)
    head_major = lambda c: np.einsum("tk,hg->htgk", c, eye).reshape(ATT_HEADS * n_new, -1)
    cg, ct, cn = key_major(cg), key_major(ct), head_major(cn)
    rb = row0 // n_new
    n_groups = n_cache // d_max
    kgv = win_k.reshape(n_seq, n_groups, d_max, ATT_HEADS, ATT_HD)
    vgv = win_v.reshape(n_seq, n_groups, d_max, ATT_HEADS, ATT_HD)
    ktv = win_k.reshape(n_seq, n_cache // tail, tail, ATT_HEADS, ATT_HD)
    vtv = win_v.reshape(n_seq, n_cache // tail, tail, ATT_HEADS, ATT_HD)
    new = lambda off: pl.BlockSpec((n_new, D_ATT), lambda b, off=off: (rb + b, off))
    grid_spec = pl.BlockSpec((None, n_grid, half, ATT_HEADS, ATT_HD), lambda b: (b, 0, 0, 0, 0))
    tail_spec = pl.BlockSpec((None, None, tail, ATT_HEADS, ATT_HD),
                             lambda b: (b, n_cache // tail - 1, 0, 0, 0))
    const = lambda a: pl.BlockSpec(a.shape, lambda b: (0, 0))
    vmem = (2 * 2 * (n_grid * half + tail) * D_ATT * 4 + 4 * cg.size * 4 * 3) / 2**20 + 12
    return pl.pallas_call(
        _attn_sample_body,
        out_shape=jax.ShapeDtypeStruct((n_seq * n_new, D_ATT), F32),
        grid=(n_seq,),
        in_specs=[new(0), new(1), new(2), grid_spec, tail_spec, grid_spec, tail_spec,
                  const(cg), const(ct), const(cn)],
        out_specs=pl.BlockSpec((n_new, D_ATT), lambda b: (b, 0)),
        compiler_params=_cparams(("parallel",), vmem),
        name="attn_sample",
    )(proj, proj, proj, kgv, ktv, vgv, vtv, jnp.asarray(cg), jnp.asarray(ct), jnp.asarray(cn))


def _gelu_tanh(x):
    return 0.5 * x * (1.0 + jnp.tanh(math.sqrt(2.0 / math.pi) * (x + 0.044715 * (x * x * x))))


def _ssm_body(u_ref, bb_ref, cst_ref, a_ref, ap_ref, d_ref, hre_ref, him_ref, *rest,
              tl, npar, seq_len, nseg, emit_y, exact_in):
    if emit_y:
        y_ref, fre_ref, fim_ref, h_s = rest
    else:
        fre_ref, fim_ref, h_s = rest
    ns = SSM_STATES_PER_TILE
    c = pl.program_id(1)
    ngrp = npar // SUBLANES

    def step_rows(i, g):
        return pl.ds(c * tl + i + g * SUBLANES * seq_len, SUBLANES, stride=seq_len)

    @pl.when(c == 0)
    def _init():
        if nseg == 1:
            h_s[0] = hre_ref[...]
            h_s[1] = him_ref[...]
        else:
            pr, pi = ap_ref[0:1, :], ap_ref[1:2, :]
            for b in range(npar // nseg):
                sr = jnp.zeros((1, ns), F32)
                si = jnp.zeros((1, ns), F32)
                for j in range(nseg):
                    row = b * nseg + j
                    h_s[0, row:row + 1, :] = sr
                    h_s[1, row:row + 1, :] = si
                    er, ei = hre_ref[row:row + 1, :], him_ref[row:row + 1, :]
                    sr, si = pr * sr - pi * si + er, pr * si + pi * sr + ei

    ar = jnp.broadcast_to(a_ref[0:1, :], (SUBLANES, ns))
    ai = jnp.broadcast_to(a_ref[1:2, :], (SUBLANES, ns))
    us = [jnp.concatenate([u_ref[step_rows(i, g), :] for i in range(tl)], axis=0)
          for g in range(ngrp)]
    if exact_in:
        xs = [jnp.dot(u, bb_ref[...], precision=lax.Precision.HIGHEST, preferred_element_type=F32)
              for u in us]
    else:
        xs = [_dot(u.astype(BF16), bb_ref[...]) for u in us]

    hs = []
    for g in range(ngrp):
        gs = slice(g * SUBLANES, (g + 1) * SUBLANES)
        hr, hi = h_s[0, gs, :], h_s[1, gs, :]
        states = []
        for i in range(tl):
            xr = xs[g][i * SUBLANES:(i + 1) * SUBLANES, 0:ns]
            xi = xs[g][i * SUBLANES:(i + 1) * SUBLANES, ns:2 * ns]
            hr, hi = ar * hr - ai * hi + xr, ar * hi + ai * hr + xi
            if emit_y:
                states.append(jnp.concatenate([hr, hi], axis=1))
        h_s[0, gs, :] = hr
        h_s[1, gs, :] = hi
        if emit_y:
            hs.append(jnp.concatenate(states, axis=0))

    if emit_y:
        for g in range(ngrp):
            y = _gelu_tanh(_dot(hs[g].astype(BF16), cst_ref[...]) + d_ref[...] * us[g])
            for i in range(tl):
                y_ref[step_rows(i, g), :] = y[i * SUBLANES:(i + 1) * SUBLANES, :]

    @pl.when(c == pl.num_programs(1) - 1)
    def _fin():
        fre_ref[...] = h_s[0]
        fim_ref[...] = h_s[1]


def _ssm_scan(proj, prm, hin_re, hin_im, *, seq_len, tl, nseg, emit_y, exact_in, name):
    rows = proj.shape[0]
    npar = rows // seq_len
    assert npar % SUBLANES == 0 and seq_len % tl == 0
    ns = SSM_STATES_PER_TILE
    nk = D_SSM // SSM_LANE_TILE
    col0 = (proj.shape[1] - D_SSM) // SSM_LANE_TILE
    bb = prm["bb_f32"] if exact_in else prm["bb_bf16"]
    in_specs = [
        pl.BlockSpec((rows, SSM_LANE_TILE), lambda k, c: (0, col0 + k)),
        pl.BlockSpec((None, SSM_LANE_TILE, 2 * ns), lambda k, c: (k, 0, 0)),
        pl.BlockSpec((None, 2 * ns, SSM_LANE_TILE), lambda k, c: (k, 0, 0)),
        pl.BlockSpec((None, 2, ns), lambda k, c: (k, 0, 0)),
        pl.BlockSpec((None, 2, ns), lambda k, c: (k, 0, 0)),
        pl.BlockSpec((1, SSM_LANE_TILE), lambda k, c: (0, k)),
        pl.BlockSpec((npar, ns), lambda k, c: (0, k)),
        pl.BlockSpec((npar, ns), lambda k, c: (0, k)),
    ]
    state_shape = jax.ShapeDtypeStruct((npar, nk * ns), F32)
    state_spec = pl.BlockSpec((npar, ns), lambda k, c: (0, k))
    out_shape = [state_shape, state_shape]
    out_specs = [state_spec, state_spec]
    if emit_y:
        out_shape = [jax.ShapeDtypeStruct((rows, D_SSM), F32)] + out_shape
        out_specs = [pl.BlockSpec((rows, SSM_LANE_TILE), lambda k, c: (0, k))] + out_specs
    vmem = (4 * rows * SSM_LANE_TILE * 4 + tl * npar * 2 * ns * 4) / 2**20 + 16
    return pl.pallas_call(
        functools.partial(_ssm_body, tl=tl, npar=npar, seq_len=seq_len, nseg=nseg, emit_y=emit_y,
                          exact_in=exact_in),
        out_shape=out_shape,
        grid=(nk, seq_len // tl),
        in_specs=in_specs,
        out_specs=out_specs,
        scratch_shapes=[pltpu.VMEM((2, npar, ns), F32)],
        compiler_params=_cparams(("parallel", "arbitrary"), vmem),
        name=name,
    )(proj, bb, prm["cst"], prm["a"], prm["apow"], prm["d"], hin_re, hin_im)


def _ssm_params(lam_re, lam_im, log_dt, b_re, b_im, c_re, c_im, d_skip, seg_len):
    g, p, c = N_SSM_GROUPS, SSM_STATE, SSM_GROUP_CH
    nk, gt = g // SSM_GROUPS_PER_TILE, SSM_GROUPS_PER_TILE
    dt = jnp.exp(log_dt.astype(F32))[:, None]
    lr, li = lam_re.astype(F32), lam_im.astype(F32)
    mag = jnp.exp(lr * dt)
    a_re, a_im = mag * jnp.cos(li * dt), mag * jnp.sin(li * dt)
    magp = jnp.exp(lr * dt * seg_len)
    p_re, p_im = magp * jnp.cos(li * dt * seg_len), magp * jnp.sin(li * dt * seg_len)
    den = lr * lr + li * li
    nr, ni = a_re - 1.0, a_im
    f_re, f_im = (nr * lr + ni * li) / den, (ni * lr - nr * li) / den
    br, bi = b_re.astype(F32), b_im.astype(F32)
    bb_re = f_re[..., None] * br - f_im[..., None] * bi
    bb_im = f_re[..., None] * bi + f_im[..., None] * br
    eye = jnp.eye(gt, dtype=F32)

    def pack_b(m):
        return jnp.einsum("kgpc,gh->kgchp", m.reshape(nk, gt, p, c), eye).reshape(nk, gt * c, gt * p)

    def pack_c(m):
        return jnp.einsum("kgcp,gh->kgphc", m.reshape(nk, gt, c, p), eye).reshape(nk, gt * p, gt * c)

    bb = jnp.concatenate([pack_b(bb_re), pack_b(bb_im)], axis=2)
    cst = jnp.concatenate([pack_c(c_re.astype(F32)), -pack_c(c_im.astype(F32))], axis=1)
    tile = lambda v: v.reshape(nk, 1, gt * p)
    return {
        "bb_f32": bb, "bb_bf16": bb.astype(BF16), "cst": cst.astype(BF16),
        "a": jnp.concatenate([tile(a_re), tile(a_im)], axis=1),
        "apow": jnp.concatenate([tile(p_re), tile(p_im)], axis=1),
        "d": d_skip.astype(F32).reshape(1, g * c),
    }


def _glu_body(y_ref, w_ref, o_ref):
    yg = y_ref[...]
    z = _dot(yg.astype(BF16), w_ref[...])
    o_ref[...] = yg * (1.0 / (1.0 + jnp.exp(-z)))


def _glu(yg, w, *, tm, name):
    m, n = yg.shape
    return pl.pallas_call(
        _glu_body,
        out_shape=jax.ShapeDtypeStruct((m, n), F32),
        grid=(m // tm,),
        in_specs=[pl.BlockSpec((tm, n), lambda i: (i, 0)), pl.BlockSpec((n, n), lambda i: (0, 0))],
        out_specs=pl.BlockSpec((tm, n), lambda i: (i, 0)),
        compiler_params=_cparams(("parallel",), 4 * tm * n * 4 / 2**20 + 12),
        name=name,
    )(yg, w)


def _mix_body(attn_ref, ssm_ref, ga_ref, gs_ref, w_ref, x_ref, g_ref, b_ref, o_ref):
    half = attn_ref.shape[0] // ROW_SPLIT
    for r in range(ROW_SPLIT):
        rows = slice(r * half, (r + 1) * half)
        a = _rmsnorm(attn_ref[rows, :], ga_ref[...]).astype(BF16)
        s = _rmsnorm(ssm_ref[rows, :], gs_ref[...]).astype(BF16)
        mix = _dot(jnp.concatenate([a, s], axis=1), w_ref[...])
        o_ref[rows, :] = _layernorm(DEEPNORM_ALPHA * x_ref[rows, :] + mix, g_ref[...], b_ref[...])


def _mix(attn, ssm, ga, gs, w, x, g, b, *, tm, name):
    m = x.shape[0]
    row = lambda n: pl.BlockSpec((tm, n), lambda i: (i, 0))
    const = lambda a: pl.BlockSpec(a.shape, lambda i: (0, 0))
    return pl.pallas_call(
        _mix_body,
        out_shape=jax.ShapeDtypeStruct((m, D_MODEL), F32),
        grid=(m // tm,),
        in_specs=[row(D_ATT), row(D_SSM), const(ga), const(gs), const(w), row(D_MODEL),
                  const(g), const(b)],
        out_specs=row(D_MODEL),
        compiler_params=_cparams(("parallel",), 6 * tm * D_MODEL * 4 / 2**20 + 24),
        name=name,
    )(attn, ssm, ga, gs, w, x, g, b)


def _store_gatherable(o_ref, y):
    rows = y.shape[0]
    for c in range(ROW_CHUNKS):
        o_ref[pl.ds(c, rows, stride=ROW_PITCH), :] = y[:, c * LANES:(c + 1) * LANES]
    for c in range(ROW_CHUNKS, ROW_PITCH):
        o_ref[pl.ds(c, rows, stride=ROW_PITCH), :] = jnp.zeros((rows, LANES), F32)


def _load_gathered(buf, rows):
    return jnp.concatenate([buf[pl.ds(c, rows, stride=ROW_PITCH), :] for c in range(ROW_CHUNKS)],
                           axis=1)


def _start_row_gather(src_hbm, idx, buf, r, sem):
    pltpu.make_async_copy(src_hbm.at[pl.ds(idx * ROW_PITCH, ROW_CHUNKS), :],
                          buf.at[pl.ds(r * ROW_PITCH, ROW_CHUNKS), :], sem).start()


def _wait_row_gathers(buf, other, rows, sem):
    span = pl.ds(0, rows * ROW_CHUNKS)
    pltpu.make_async_copy(other.at[span, :], buf.at[span, :], sem).wait()


def _mm_ln_body(a1_ref, a2_ref, w_ref, x1_ref, x2_ref, g_ref, b_ref, wr_ref, br_ref,
                o_ref, rows_ref, sel_ref, wts_ref, cnt_ref, run_s, *, tiles1):
    first = pl.program_id(0) < tiles1
    half = a1_ref.shape[0] // ROW_SPLIT
    outs = []
    for r in range(ROW_SPLIT):
        rows = slice(r * half, (r + 1) * half)
        a = jnp.where(first, a1_ref[rows, :].astype(BF16), a2_ref[rows, :].astype(BF16))
        x = jnp.where(first, x1_ref[rows, :], x2_ref[rows, :])
        y = _dot(a, w_ref[...])
        out = _layernorm(DEEPNORM_ALPHA * x + y, g_ref[...], b_ref[...])
        o_ref[rows, :] = out
        outs.append(out)
    out = jnp.concatenate(outs, axis=0)
    _store_gatherable(rows_ref, out)
    _route_tile(out, wr_ref, br_ref, sel_ref, wts_ref, cnt_ref, run_s)


def _mm_ln(a1, a2, w, x1, x2, g, b, w_r, b_r, *, name):
    tm = a2.shape[0]
    assert a1.shape[0] % tm == 0
    tiles1 = a1.shape[0] // tm
    m = a1.shape[0] + tm
    row1 = lambda n: pl.BlockSpec((tm, n), lambda i: (jnp.minimum(i, tiles1 - 1), 0))
    row2 = lambda n: pl.BlockSpec((tm, n), lambda i: (0, 0))
    const = lambda v: pl.BlockSpec(v.shape, lambda i: (0, 0))
    lanes = pl.BlockSpec((tm, ROUTER_LANES), lambda i: (i, 0))
    return pl.pallas_call(
        functools.partial(_mm_ln_body, tiles1=tiles1),
        out_shape=[jax.ShapeDtypeStruct((m, D_MODEL), F32),
                   jax.ShapeDtypeStruct((m * ROW_PITCH, LANES), F32),
                   jax.ShapeDtypeStruct((m, ROUTER_LANES), I32),
                   jax.ShapeDtypeStruct((m, ROUTER_LANES), F32),
                   jax.ShapeDtypeStruct((1, ROUTER_LANES), I32)],
        grid=(tiles1 + 1,),
        in_specs=[row1(a1.shape[1]), row2(a2.shape[1]), const(w), row1(D_MODEL), row2(D_MODEL),
                  const(g), const(b), const(w_r), const(b_r)],
        out_specs=[pl.BlockSpec((tm, D_MODEL), lambda i: (i, 0)),
                   pl.BlockSpec((tm * ROW_PITCH, LANES), lambda i: (i, 0)),
                   lanes, lanes, pl.BlockSpec((1, ROUTER_LANES), lambda i: (0, 0))],
        scratch_shapes=[pltpu.VMEM((1, ROUTER_LANES), F32)],
        compiler_params=_cparams(("arbitrary",), 12 * tm * D_MODEL * 4 / 2**20 + 24),
        name=name,
    )(a1, a2, w, x1, x2, g, b, w_r, b_r)


def _memattn_body(q_ref, k_ref, v_ref, o_ref):
    scale = MEM_HD ** -0.5
    for h in range(MEM_HEADS):
        sl = slice(h * MEM_HD, (h + 1) * MEM_HD)
        s = _dot_nt(q_ref[:, sl].astype(BF16), k_ref[:, sl].astype(BF16)) * scale
        m = jnp.max(s, axis=-1, keepdims=True)
        p = jnp.exp(s - m)
        l = jnp.sum(p, axis=-1, keepdims=True)
        o_ref[:, sl] = (_dot(p.astype(BF16), v_ref[:, sl].astype(BF16)) / l).astype(o_ref.dtype)


def _memattn(q, mem_k, mem_v, *, row0, n_seq, seq, tq, name):
    assert seq % tq == 0 and row0 % tq == 0
    nq = seq // tq
    rb = row0 // tq
    mem_spec = pl.BlockSpec((None, N_MEM, D_MODEL), lambda b, i: (b, 0, 0))
    return pl.pallas_call(
        _memattn_body,
        out_shape=jax.ShapeDtypeStruct((n_seq * seq, D_MODEL), q.dtype),
        grid=(n_seq, nq),
        in_specs=[pl.BlockSpec((tq, D_MODEL), lambda b, i: (rb + b * nq + i, 0)),
                  mem_spec, mem_spec],
        out_specs=pl.BlockSpec((tq, D_MODEL), lambda b, i: (b * nq + i, 0)),
        compiler_params=_cparams(("parallel", "parallel"),
                                 4 * (tq + N_MEM) * D_MODEL * 4 / 2**20 + 8),
        name=name,
    )(q, mem_k, mem_v)


def _memattn_heads_body(q_ref, k_ref, v_ref, c_ref, o_ref):
    scale = MEM_HD ** -0.5
    tq = q_ref.shape[0]
    q = jnp.concatenate([q_ref[:, h * MEM_HD:(h + 1) * MEM_HD] for h in range(MEM_HEADS)], axis=0)
    k = k_ref[...].reshape(N_MEM * MEM_HEADS, MEM_HD).astype(BF16)
    v = v_ref[...].reshape(N_MEM * MEM_HEADS, MEM_HD).astype(BF16)
    s = jnp.where(c_ref[...] > 0, _dot_nt(q.astype(BF16), k) * scale, NEG_INF)
    m = jnp.max(s, axis=-1, keepdims=True)
    p = jnp.exp(s - m)
    l = jnp.sum(p, axis=-1, keepdims=True)
    o = _dot(p.astype(BF16), v) / l
    for h in range(MEM_HEADS):
        o_ref[:, h * MEM_HD:(h + 1) * MEM_HD] = o[h * tq:(h + 1) * tq, :]


def _memattn_heads(q, mem_k, mem_v, *, row0, n_seq, seq, name):
    assert row0 % seq == 0 and seq % SUBLANES == 0
    rb = row0 // seq
    same_head = np.kron(np.eye(MEM_HEADS, dtype=np.float32), np.ones((seq, 1), np.float32))
    same_head = np.tile(same_head, (1, N_MEM))
    mem_spec = pl.BlockSpec((None, N_MEM, MEM_HEADS, MEM_HD), lambda b: (b, 0, 0, 0))
    return pl.pallas_call(
        _memattn_heads_body,
        out_shape=jax.ShapeDtypeStruct((n_seq * seq, D_MODEL), F32),
        grid=(n_seq,),
        in_specs=[pl.BlockSpec((seq, D_MODEL), lambda b: (rb + b, 0)), mem_spec, mem_spec,
                  pl.BlockSpec(same_head.shape, lambda b: (0, 0))],
        out_specs=pl.BlockSpec((seq, D_MODEL), lambda b: (b, 0)),
        compiler_params=_cparams(("parallel",), 8 * N_MEM * D_MODEL * 4 / 2**20 + 8),
        name=name,
    )(q, mem_k, mem_v, jnp.asarray(same_head))


def _route_tile(x, w_ref, b_ref, sel_ref, wts_ref, cnt_ref, run_s):
    tm = x.shape[0]

    @pl.when(pl.program_id(0) == 0)
    def _():
        run_s[...] = jnp.zeros_like(run_s)

    ng, epg = N_EXPERT_GROUPS, EXPERTS_PER_GROUP
    x_hi = x.astype(BF16)
    x_lo = (x - x_hi.astype(F32)).astype(BF16)
    parts = _dot(x_hi, w_ref[...]) + _dot(x_lo, w_ref[...])
    logits = parts + pltpu.roll(parts, shift=ROUTER_LANES // 2, axis=1) + b_ref[...]
    lane = lax.broadcasted_iota(I32, (tm, ROUTER_LANES), 1)
    big = ROUTER_LANES

    def first_argmax(vals):
        mx = jnp.max(vals, axis=-1, keepdims=True)
        idx = jnp.min(jnp.where(vals == mx, lane, big), axis=-1, keepdims=True)
        return mx, idx

    gl = jnp.where(lane < ng, logits, NEG_INF)
    gmax, gsel = first_argmax(gl)
    g_w = 1.0 / jnp.sum(jnp.exp(gl - gmax), axis=-1, keepdims=True)
    lo = ng + gsel * epg
    el = jnp.where(jnp.logical_and(lane >= lo, lane < lo + epg), logits, NEG_INF)
    v1, i1 = first_argmax(el)
    v2, i2 = first_argmax(jnp.where(lane == i1, NEG_INF, el))
    e21 = jnp.exp(v2 - v1)
    w1 = g_w / (1.0 + e21)
    w2 = g_w * e21 / (1.0 + e21)

    onehot = jnp.logical_or(lane == i1, lane == i2)
    r = lax.broadcasted_iota(I32, (tm, tm), 0)
    cc = lax.broadcasted_iota(I32, (tm, tm), 1)
    tri = (cc < r).astype(BF16)
    before = _dot(tri, onehot.astype(BF16)) + run_s[...]
    rank1 = jnp.sum(jnp.where(lane == i1, before, 0.0), axis=-1, keepdims=True).astype(I32)
    rank2 = jnp.sum(jnp.where(lane == i2, before, 0.0), axis=-1, keepdims=True).astype(I32)
    run_s[...] = run_s[...] + jnp.sum(onehot.astype(F32), axis=0, keepdims=True)

    sel = jnp.where(lane == 0, i1 - ng, jnp.where(lane == 1, i2 - ng,
                    jnp.where(lane == 2, rank1, jnp.where(lane == 3, rank2, 0))))
    sel_ref[...] = sel
    wts_ref[...] = jnp.where(lane == 0, w1, jnp.where(lane == 1, w2, 0.0))
    cnt_ref[...] = run_s[...].astype(I32)


def _router_weights(w_r1, b_r1, w_r2, b_r2):
    ng, ne = N_EXPERT_GROUPS, N_EXPERTS
    half = ROUTER_LANES // 2
    assert ng + ne <= half
    w_r = jnp.concatenate([w_r1, w_r2.reshape(D_MODEL, ne),
                           jnp.zeros((D_MODEL, half - ng - ne), F32)], axis=1)
    w_hi = w_r.astype(BF16)
    w_lo = (w_r - w_hi.astype(F32)).astype(BF16)
    b_r = jnp.concatenate([b_r1, b_r2.reshape(ne), jnp.zeros((half - ng - ne,), F32)])
    return jnp.concatenate([w_hi, w_lo], axis=1), jnp.concatenate([b_r, b_r]).reshape(1, ROUTER_LANES)


def _dispatch_body(eid_ref, rank_ref, off_ref, pad0_ref, npad_ref, nact_ref, x_hbm, xs_hbm, xbuf,
                   zero_s, in_sem, out_sem, pad_sem, *, td, n_tiles):
    i = pl.program_id(0)
    n = pl.num_programs(0)
    tile_rows = MOE_TILE * ROW_PITCH
    in_rows = td * ROW_PITCH

    def row_copy(src, dst_row, s):
        return pltpu.make_async_copy(src, xs_hbm.at[pl.ds(dst_row * ROW_PITCH, ROW_PITCH), :], s)

    def tile_load(t):
        s = t % DISPATCH_SLOTS
        return pltpu.make_async_copy(x_hbm.at[pl.ds(t * in_rows, in_rows), :], xbuf.at[s],
                                     in_sem.at[s])

    def wait_scatter(par):
        for _ in range(2):
            pltpu.make_async_copy(xbuf.at[0], xs_hbm.at[pl.ds(0, in_rows), :],
                                  out_sem.at[par]).wait()

    @pl.when(i == 0)
    def _():
        for t in range(DISPATCH_SLOTS - 1):
            @pl.when(t < n)
            def _(t=t):
                tile_load(t).start()
        zero_s[...] = jnp.zeros_like(zero_s)

        def pad_copies(e):
            out = []
            for bit in reversed(range(MOE_TILE.bit_length() - 1)):
                rows = (1 << bit) * ROW_PITCH
                first = (pad0_ref[e] + (npad_ref[e] >> (bit + 1) << (bit + 1))) * ROW_PITCH
                out.append((jnp.bitwise_and(npad_ref[e] >> bit, 1) == 1, pltpu.make_async_copy(
                    zero_s.at[pl.ds(0, rows), :], xs_hbm.at[pl.ds(first, rows), :], pad_sem.at[0])))
            return out

        for e in range(N_EXPERTS):
            for on, cp in pad_copies(e):
                pl.when(on)(cp.start)
        for e in range(N_EXPERTS):
            for on, cp in pad_copies(e):
                pl.when(on)(cp.wait)

        def zero_tile(t, carry):
            parts = [pltpu.make_async_copy(
                zero_s.at[pl.ds(0, MOE_TILE), :],
                xs_hbm.at[pl.ds(t * tile_rows + j * MOE_TILE, MOE_TILE), :],
                pad_sem.at[0]) for j in range(ROW_PITCH)]
            for cp in parts:
                cp.start()
            for cp in parts:
                cp.wait()
            return carry

        lax.fori_loop(nact_ref[0], n_tiles, zero_tile, 0)

    slot = i % DISPATCH_SLOTS
    par = i % 2
    tile_load(i).wait()
    base = i * td * 2
    for r in range(td):
        for k in range(2):
            j = base + 2 * r + k
            row_copy(xbuf.at[slot, pl.ds(r * ROW_PITCH, ROW_PITCH), :],
                     off_ref[eid_ref[j]] + rank_ref[j], out_sem.at[par]).start()

    @pl.when(i > 0)
    def _():
        wait_scatter(1 - par)

    @pl.when(i + DISPATCH_SLOTS - 1 < n)
    def _():
        tile_load(i + DISPATCH_SLOTS - 1).start()

    @pl.when(i == n - 1)
    def _():
        wait_scatter(par)


def _moe_dispatch(x_rows, eid, rank, row_off, pad_start, pad_count, nact, *, td, n_tiles):
    n = x_rows.shape[0] // ROW_PITCH
    grid_spec = pltpu.PrefetchScalarGridSpec(
        num_scalar_prefetch=6,
        grid=(n // td,),
        in_specs=[pl.BlockSpec(memory_space=pl.ANY)],
        out_specs=pl.BlockSpec(memory_space=pl.ANY),
        scratch_shapes=[pltpu.VMEM((DISPATCH_SLOTS, td * ROW_PITCH, LANES), F32),
                        pltpu.VMEM((MOE_TILE // 2 * ROW_PITCH, LANES), F32),
                        pltpu.SemaphoreType.DMA((DISPATCH_SLOTS,)),
                        pltpu.SemaphoreType.DMA((2,)),
                        pltpu.SemaphoreType.DMA((1,))],
    )
    return pl.pallas_call(
        functools.partial(_dispatch_body, td=td, n_tiles=n_tiles),
        out_shape=jax.ShapeDtypeStruct((n_tiles * MOE_TILE * ROW_PITCH, LANES), F32),
        grid_spec=grid_spec,
        compiler_params=_cparams(("arbitrary",), 16),
        name="moe_dispatch",
    )(eid, rank, row_off, pad_start, pad_count, nact, x_rows)


def _moe_body(te_ref, ord_ref, nxt_ref, nact_ref, x_ref, wg_hbm, wu_hbm, wd_hbm, o_ref,
              wg_f, wu_f, wd_f, wsem, wg_s, wu_s, wd_s):
    i = pl.program_id(0)
    nact = nact_ref[0]
    tm = MOE_TILE

    def weight_copies(expert, ws):
        return [pltpu.make_async_copy(hbm.at[expert], stage.at[ws], wsem.at[ws])
                for hbm, stage in ((wg_hbm, wg_f), (wu_hbm, wu_f), (wd_hbm, wd_f))]

    def ffn(wg, wu, wd):
        x = _load_gathered(x_ref, tm).astype(BF16)
        hg = _dot(x, wg)
        hu = _dot(x, wu)
        h = hg * (1.0 / (1.0 + jnp.exp(-hg))) * hu
        _store_gatherable(o_ref, _dot(h.astype(BF16), wd))

    def tile_step():
        first = jnp.logical_or(i == 0, te_ref[i] != te_ref[jnp.maximum(i - 1, 0)])

        @pl.when(first)
        def _():
            ws = ord_ref[i] % 2
            for cp in weight_copies(te_ref[i], ws):
                cp.wait()

            @pl.when(nxt_ref[i] >= 0)
            def _():
                for cp in weight_copies(nxt_ref[i], 1 - ws):
                    cp.start(priority=1)

            wg = wg_f[ws].astype(BF16)
            wu = wu_f[ws].astype(BF16)
            wd = wd_f[ws].astype(BF16)
            wg_s[...] = wg
            wu_s[...] = wu
            wd_s[...] = wd
            ffn(wg, wu, wd)

        @pl.when(jnp.logical_not(first))
        def _():
            ffn(wg_s[...], wu_s[...], wd_s[...])

    @pl.when(i == 0)
    def _():
        for cp in weight_copies(te_ref[0], 0):
            cp.start(priority=1)

    @pl.when(i < nact)
    def _():
        tile_step()

    @pl.when(i >= nact)
    def _():
        o_ref[...] = jnp.zeros_like(o_ref)


def _moe_experts(x_sorted, w_gate, w_up, w_down, tile_expert, tile_ord, tile_next, nact, *,
                 n_tiles):
    tm = MOE_TILE
    in_map = lambda i, te, od, nx, n: (jnp.minimum(i, n[0] - 1), 0)
    any_spec = pl.BlockSpec(memory_space=pl.ANY)
    grid_spec = pltpu.PrefetchScalarGridSpec(
        num_scalar_prefetch=4,
        grid=(n_tiles,),
        in_specs=[pl.BlockSpec((tm * ROW_PITCH, LANES), in_map), any_spec, any_spec, any_spec],
        out_specs=pl.BlockSpec((tm * ROW_PITCH, LANES), lambda i, te, od, nx, n: (i, 0)),
        scratch_shapes=[pltpu.VMEM((2, D_MODEL, D_EXPERT), F32),
                        pltpu.VMEM((2, D_MODEL, D_EXPERT), F32),
                        pltpu.VMEM((2, D_EXPERT, D_MODEL), F32),
                        pltpu.SemaphoreType.DMA((2,)),
                        pltpu.VMEM((D_MODEL, D_EXPERT), BF16),
                        pltpu.VMEM((D_MODEL, D_EXPERT), BF16),
                        pltpu.VMEM((D_EXPERT, D_MODEL), BF16)],
    )
    return pl.pallas_call(
        _moe_body,
        out_shape=jax.ShapeDtypeStruct((n_tiles * tm * ROW_PITCH, LANES), F32),
        grid_spec=grid_spec,
        compiler_params=_cparams(("arbitrary",), 48),
        name="moe_experts",
    )(tile_expert, tile_ord, tile_next, nact, x_sorted, w_gate, w_up, w_down)


def _combine_body(eid_ref, rank_ref, off_ref, ys_hbm, wts_ref, x_ref, g_ref, b_ref, o1_ref, o2_ref,
                  buf, sem, *, tc, tiles1):
    i = pl.program_id(0)
    n = pl.num_programs(0)
    slot = i % GATHER_SLOTS
    ahead = GATHER_SLOTS - 1

    def issue_gather(tile, slot_):
        base = tile * tc * 2
        for r in range(tc):
            for k in range(2):
                j = base + 2 * r + k
                _start_row_gather(ys_hbm, off_ref[eid_ref[j]] + rank_ref[j], buf.at[slot_, k], r,
                                  sem.at[slot_])

    @pl.when(i == 0)
    def _():
        for t in range(ahead):
            @pl.when(t < n)
            def _(t=t):
                issue_gather(t, t)

    for k in range(2):
        _wait_row_gathers(buf.at[slot, k], buf.at[(i + 1) % GATHER_SLOTS, k], tc, sem.at[slot])

    @pl.when(i + ahead < n)
    def _():
        issue_gather(i + ahead, (i + ahead) % GATHER_SLOTS)

    w = wts_ref[...]
    moe = (w[:, 0:1] * _load_gathered(buf.at[slot, 0], tc)
           + w[:, 1:2] * _load_gathered(buf.at[slot, 1], tc))
    out = _layernorm(DEEPNORM_ALPHA * x_ref[...] + moe, g_ref[...], b_ref[...])

    @pl.when(i < tiles1)
    def _():
        o1_ref[...] = out

    @pl.when(i >= tiles1)
    def _():
        o2_ref[...] = out


def _moe_combine(ys, eid, rank, row_off, wts, x, g, b, *, tc, n_first):
    m = x.shape[0]
    assert n_first % tc == 0 and (m - n_first) % tc == 0
    tiles1 = n_first // tc
    grid_spec = pltpu.PrefetchScalarGridSpec(
        num_scalar_prefetch=3,
        grid=(m // tc,),
        in_specs=[pl.BlockSpec(memory_space=pl.ANY),
                  pl.BlockSpec((tc, ROUTER_LANES), lambda i, *_: (i, 0)),
                  pl.BlockSpec((tc, D_MODEL), lambda i, *_: (i, 0)),
                  pl.BlockSpec((1, D_MODEL), lambda i, *_: (0, 0)),
                  pl.BlockSpec((1, D_MODEL), lambda i, *_: (0, 0))],
        out_specs=[pl.BlockSpec((tc, D_MODEL), lambda i, *_: (jnp.minimum(i, tiles1 - 1), 0)),
                   pl.BlockSpec((tc, D_MODEL), lambda i, *_: (jnp.maximum(i - tiles1, 0), 0))],
        scratch_shapes=[pltpu.VMEM((GATHER_SLOTS, 2, tc * ROW_PITCH, LANES), F32),
                        pltpu.SemaphoreType.DMA((GATHER_SLOTS,))],
    )
    return pl.pallas_call(
        functools.partial(_combine_body, tc=tc, tiles1=tiles1),
        out_shape=[jax.ShapeDtypeStruct((n_first, D_MODEL), F32),
                   jax.ShapeDtypeStruct((m - n_first, D_MODEL), F32)],
        grid_spec=grid_spec,
        compiler_params=_cparams(("arbitrary",), 16 * tc * D_MODEL * 4 / 2**20 + 8),
        name="moe_combine_ln3",
    )(eid, rank, row_off, ys, wts, x, g, b)


def _moe(x, x_rows, sel, wts, cnt, w_gate, w_up, w_down, g, b, *, n_first, tc):
    n = x.shape[0]
    ng, ne = N_EXPERT_GROUPS, N_EXPERTS

    tm = MOE_TILE
    n_tiles = (2 * n) // tm + ne
    counts = cnt[0, ng:ng + ne]
    tiles_per = (counts + tm - 1) // tm
    tile_end = jnp.cumsum(tiles_per)
    row_off = (tile_end - tiles_per) * tm
    nact = tile_end[-1]
    a_eid, a_rank = sel[:, 0:2].reshape(-1), sel[:, 2:4].reshape(-1)
    row_off = row_off.astype(I32)
    tile_ids = jnp.minimum(jnp.arange(n_tiles, dtype=I32), nact - 1)
    tile_expert = jnp.sum((tile_end[None, :] <= tile_ids[:, None]).astype(I32), axis=1)
    used = tiles_per > 0
    eid = jnp.arange(ne, dtype=I32)
    ordinal = jnp.cumsum(used.astype(I32)) - 1
    later = jnp.where(jnp.logical_and(used[None, :], eid[None, :] > eid[:, None]), eid[None, :], ne)
    nxt = jnp.min(later, axis=1)
    nxt = jnp.where(nxt == ne, -1, nxt)

    nact = nact.reshape(1).astype(I32)
    x_sorted = _moe_dispatch(x_rows, a_eid, a_rank, row_off, (row_off + counts).astype(I32),
                             (tiles_per * tm - counts).astype(I32), nact, td=tc, n_tiles=n_tiles)
    ys = _moe_experts(x_sorted, w_gate, w_up, w_down, tile_expert, ordinal[tile_expert],
                      nxt[tile_expert], nact, n_tiles=n_tiles)
    return _moe_combine(ys, a_eid, a_rank, row_off, wts, x, g, b, tc=tc, n_first=n_first)


def _row_tile(m, cap):
    best = SUBLANES
    for t in range(SUBLANES, cap + 1, SUBLANES):
        if m % t == 0:
            best = t
    return best


def kernel(x_prompt, x_sample, cache_win_k, cache_win_v, state_ssm_re, state_ssm_im, cache_mem_k, cache_mem_v, mem_prompt, w_in, ssm_lam_re, ssm_lam_im, ssm_log_dt, ssm_b_re, ssm_b_im, ssm_c_re, ssm_c_im, ssm_d, w_glu, g_attn, g_ssm, w_out, ln1_g, ln1_b, w_mq, w_mk, w_mv, w_mo, ln2_g, ln2_b, w_r1, b_r1, w_r2, b_r2, w_gate, w_up, w_down, ln3_g, ln3_b):
    nb, seq, d = x_prompt.shape
    ns, dseq, _ = x_sample.shape
    n_p, n_s = nb * seq, ns * dseq
    n = n_p + n_s
    l = 0
    row2 = lambda v: v[l].reshape(1, -1)

    x_p, x_s = x_prompt.reshape(n_p, d), x_sample.reshape(n_s, d)
    tm_p = _row_tile(n_p, 1024)
    tm_ln = _row_tile(n_p, 512)
    assert n_p % n_s == 0 and n_s % SUBLANES == 0

    proj_p = _matmul(x_p, w_in[l], tm=tm_p, tn=1024, name="proj_in_prompt")
    proj_s = _matmul(x_s, w_in[l], tm=n_s, tn=1024, name="proj_in_sample")

    attn_p = _attn_prompt(proj_p, n_batch=nb, seq=seq)
    attn_s = _attn_sample(proj_s, cache_win_k[l], cache_win_v[l], row0=0, n_seq=ns, n_new=dseq)

    seg_len = seq // SSM_SEGMENTS
    prm = _ssm_params(ssm_lam_re[l], ssm_lam_im[l], ssm_log_dt[l], ssm_b_re[l], ssm_b_im[l],
                      ssm_c_re[l], ssm_c_im[l], ssm_d[l], seg_len)
    zeros = jnp.zeros((nb * SSM_SEGMENTS, N_SSM_GROUPS * SSM_STATE), F32)
    tl = _row_tile(seg_len, 32)
    end_re, end_im = _ssm_scan(proj_p, prm, zeros, zeros, seq_len=seg_len, tl=tl, nseg=1,
                               emit_y=False, exact_in=False, name="ssm_state_prompt")
    yg_p, fin_re, fin_im = _ssm_scan(proj_p, prm, end_re, end_im, seq_len=seg_len, tl=tl,
                                     nseg=SSM_SEGMENTS, emit_y=True, exact_in=False,
                                     name="ssm_scan_prompt")
    last = SSM_SEGMENTS - 1
    ssm_re_p = fin_re.reshape(nb, SSM_SEGMENTS, N_SSM_GROUPS, SSM_STATE)[:, last]
    ssm_im_p = fin_im.reshape(nb, SSM_SEGMENTS, N_SSM_GROUPS, SSM_STATE)[:, last]

    h0_re = state_ssm_re[l].reshape(ns, -1)
    h0_im = state_ssm_im[l].reshape(ns, -1)
    yg_s, ssm_re_s, ssm_im_s = _ssm_scan(proj_s, prm, h0_re, h0_im, seq_len=dseq, tl=dseq, nseg=1,
                                         emit_y=True, exact_in=True, name="ssm_scan_sample")
    w_glu_b = w_glu[l].astype(BF16)
    ssm_out_p = _glu(yg_p, w_glu_b, tm=tm_p, name="ssm_glu_prompt")
    ssm_out_s = _glu(yg_s, w_glu_b, tm=n_s, name="ssm_glu_sample")

    mix_args = (row2(g_attn), row2(g_ssm), w_out[l].astype(BF16))
    ln1 = (row2(ln1_g), row2(ln1_b))
    x1_p = _mix(attn_p, ssm_out_p, *mix_args, x_p, *ln1, tm=tm_ln, name="mix_out_ln1_prompt")
    x1_s = _mix(attn_s, ssm_out_s, *mix_args, x_s, *ln1, tm=n_s, name="mix_out_ln1_sample")

    mem_rows = mem_prompt.reshape(nb * N_MEM, d)
    mem_k = _matmul(mem_rows, w_mk[l], tm=nb * N_MEM, tn=1024, name="mem_k")
    mem_v = _matmul(mem_rows, w_mv[l], tm=nb * N_MEM, tn=1024, name="mem_v")
    q_p = _matmul(x1_p, w_mq[l], tm=tm_p, tn=1024, name="mem_q_prompt", out_dtype=BF16)
    q_s = _matmul(x1_s, w_mq[l], tm=n_s, tn=1024, name="mem_q_sample")
    o_p = _memattn(q_p, mem_k.reshape(nb, N_MEM, d), mem_v.reshape(nb, N_MEM, d),
                   row0=0, n_seq=nb, seq=seq, tq=_row_tile(seq, 512), name="memattn_prompt")
    o_s = _memattn_heads(q_s, cache_mem_k[l], cache_mem_v[l], row0=0, n_seq=ns, seq=dseq,
                         name="memattn_sample")
    w_r, b_r = _router_weights(w_r1[l], b_r1[l], w_r2[l], b_r2[l])
    x2, x2_rows, sel, wts, cnt = _mm_ln(o_p, o_s, w_mo[l].astype(BF16), x1_p, x1_s, row2(ln2_g),
                                        row2(ln2_b), w_r, b_r, name="mem_out_ln2_route")

    y_p, y_s = _moe(x2, x2_rows, sel, wts, cnt, w_gate[l], w_up[l], w_down[l], row2(ln3_g),
                    row2(ln3_b), n_first=n_p, tc=_row_tile(n_s, 256))

    y_p = y_p.reshape(nb, seq, d)
    y_s = y_s.reshape(ns, dseq, d)
    wp = min(max(w for w, _ in DILATIONS), seq)
    k_p, v_p = _kv_window(proj_p, n_batch=nb, seq=seq, window=wp, tr=_row_tile(wp, 512))
    k_s = proj_s[:, D_ATT:2 * D_ATT].reshape(ns, dseq, ATT_HEADS, ATT_HD)
    v_s = proj_s[:, 2 * D_ATT:3 * D_ATT].reshape(ns, dseq, ATT_HEADS, ATT_HD)
    state = lambda v, b_: v.reshape(1, b_, N_SSM_GROUPS, SSM_STATE)
    return (y_p, y_s, k_p[None], v_p[None], k_s[None], v_s[None],
            state(ssm_re_p, nb), state(ssm_im_p, nb), state(ssm_re_s, ns), state(ssm_im_s, ns),
            mem_k.reshape(1, nb, N_MEM, MEM_HEADS, MEM_HD),
            mem_v.reshape(1, nb, N_MEM, MEM_HEADS, MEM_HD))
```

```python
import functools
import math

import numpy as np
import jax
import jax.numpy as jnp
from jax import lax
from jax.experimental import pallas as pl
from jax.experimental.pallas import tpu as pltpu

F32 = jnp.float32
BF16 = jnp.bfloat16
I32 = jnp.int32

D_MODEL = 2048
PAST_LEN = 8192
D_ATT = D_MODEL // 2
ATT_HEADS = 8
ATT_HD = D_ATT // ATT_HEADS
DILATIONS = ((128, 1), (512, 4), (2048, 16))
D_SSM = D_MODEL - D_ATT
SSM_GROUP_CH = 16
N_SSM_GROUPS = D_SSM // SSM_GROUP_CH
SSM_STATE = 64
N_MEM = 256
MEM_HEADS = 4
MEM_HD = D_MODEL // MEM_HEADS
N_EXPERT_GROUPS = 4
EXPERTS_PER_GROUP = 8
N_EXPERTS = N_EXPERT_GROUPS * EXPERTS_PER_GROUP
D_EXPERT = D_MODEL // 4
DEPTH = 1
DEEPNORM_ALPHA = (2.0 * DEPTH) ** 0.25
LN_EPS = 1e-5
RMS_EPS = 1e-6

LANES = 128
SUBLANES = 8
ROW_CHUNKS = D_MODEL // LANES
ROW_PITCH = ROW_CHUNKS + 1
Q_BLOCK = 128
ATTN_GROUP = 8
DEINTERLEAVE = 4
SSM_LANE_TILE = 128
SSM_GROUPS_PER_TILE = SSM_LANE_TILE // SSM_GROUP_CH
SSM_STATES_PER_TILE = SSM_GROUPS_PER_TILE * SSM_STATE
SSM_SEGMENTS = 8
MOE_TILE = 256
GATHER_SLOTS = 3
DISPATCH_SLOTS = 3
ROW_SPLIT = 2
ROUTER_LANES = 128
NEG_INF = float("-inf")


def _cparams(semantics, vmem_mib):
    return pltpu.CompilerParams(dimension_semantics=semantics,
                                vmem_limit_bytes=int(vmem_mib) << 20)


def _layernorm(y, g, b):
    mu = jnp.mean(y, axis=-1, keepdims=True)
    yc = y - mu
    var = jnp.mean(yc * yc, axis=-1, keepdims=True)
    return yc * lax.rsqrt(var + LN_EPS) * g + b


def _rmsnorm(v, g):
    return v * lax.rsqrt(jnp.mean(v * v, axis=-1, keepdims=True) + RMS_EPS) * g


def _dot(a, b):
    return jnp.dot(a, b, preferred_element_type=F32)


def _dot_nt(a, b):
    return lax.dot_general(a, b, (((1,), (1,)), ((), ())), preferred_element_type=F32)


def _mm_body(x_ref, w_ref, o_ref, wb_s):
    @pl.when(pl.program_id(1) == 0)
    def _():
        wb_s[...] = w_ref[...].astype(BF16)

    o_ref[...] = _dot(x_ref[...].astype(BF16), wb_s[...]).astype(o_ref.dtype)


def _matmul(x, w, *, tm, tn, name, out_dtype=F32):
    m, k = x.shape
    n = w.shape[1]
    vmem = (2 * (tm * k * 4 + k * tn * 4 + tm * tn * 4) + k * tn * 2 + tm * k * 2) / 2**20 + 8
    return pl.pallas_call(
        _mm_body,
        out_shape=jax.ShapeDtypeStruct((m, n), out_dtype),
        grid=(n // tn, m // tm),
        in_specs=[pl.BlockSpec((tm, k), lambda j, i: (i, 0)),
                  pl.BlockSpec((k, tn), lambda j, i: (0, j))],
        out_specs=pl.BlockSpec((tm, tn), lambda j, i: (i, j)),
        scratch_shapes=[pltpu.VMEM((k, tn), BF16)],
        compiler_params=_cparams(("parallel", "arbitrary"), vmem),
        name=name,
    )(x, w)


def _kv_window_body(k_ref, v_ref, ko_ref, vo_ref):
    for h in range(ATT_HEADS):
        sl = slice(h * ATT_HD, (h + 1) * ATT_HD)
        ko_ref[:, h, :] = k_ref[:, sl]
        vo_ref[:, h, :] = v_ref[:, sl]


def _kv_window(proj, *, n_batch, seq, window, tr):
    assert window % tr == 0 and seq % tr == 0
    per, first = seq // tr, (seq - window) // tr
    col = lambda c: pl.BlockSpec((tr, D_ATT), lambda b, t: (b * per + first + t, c))
    out_spec = pl.BlockSpec((None, tr, ATT_HEADS, ATT_HD), lambda b, t: (b, t, 0, 0))
    out_shape = jax.ShapeDtypeStruct((n_batch, window, ATT_HEADS, ATT_HD), F32)
    return pl.pallas_call(
        _kv_window_body,
        out_shape=[out_shape, out_shape],
        grid=(n_batch, window // tr),
        in_specs=[col(1), col(2)],
        out_specs=[out_spec, out_spec],
        compiler_params=_cparams(("parallel", "parallel"), 8 * tr * D_ATT * 4 / 2**20 + 8),
        name="kv_window",
    )(proj, proj)


def _attn_prompt_body(q_ref, k_ref, v_ref, o_ref, kt_s, va_s, on_s, lse_s, q4_s, k4_s, v4_s, *,
                      seq, dilations):
    scale = ATT_HD ** -0.5
    nblk = seq // Q_BLOCK
    qi = lax.broadcasted_iota(I32, (Q_BLOCK, Q_BLOCK), 0)
    kj = lax.broadcasted_iota(I32, (Q_BLOCK, Q_BLOCK), 1)
    cur_ok = kj <= qi
    prev_ok = kj >= qi
    va_s[:, :, ATT_HD:] = jnp.ones((nblk, Q_BLOCK, ATT_HD), BF16)

    quarter = seq // DEINTERLEAVE
    piece = 256

    def deinterleave(c, carry):
        for r in range(DEINTERLEAVE):
            src = pl.ds(r + c * piece * DEINTERLEAVE, piece, stride=DEINTERLEAVE)
            dst = pl.ds(pl.multiple_of(r * quarter + c * piece, piece), piece)
            q4_s[dst, :] = q_ref[src, :]
            k4_s[dst, :] = k_ref[src, :]
            v4_s[dst, :] = v_ref[src, :]
        return carry

    lax.fori_loop(0, quarter // piece, deinterleave, 0)

    for br, (_, d) in enumerate(dilations):
        span = d * Q_BLOCK
        nb = seq // span
        inner = d // DEINTERLEAVE if d % DEINTERLEAVE == 0 else 0
        qs, ks, vs = (q_ref, k_ref, v_ref) if inner == 0 else (q4_s, k4_s, v4_s)

        def stream_rows(t, d=d, span=span, nb=nb, inner=inner):
            r = t // nb
            ib = t % nb
            natural = pl.ds(r + ib * span, Q_BLOCK, stride=d)
            if inner == 0:
                return r, ib, natural, natural
            start = (r % DEINTERLEAVE) * quarter + r // DEINTERLEAVE + ib * Q_BLOCK * inner
            return r, ib, natural, pl.ds(start, Q_BLOCK, stride=inner)

        def prep(g, carry, stream_rows=stream_rows, ks=ks, vs=vs):
            loaded = []
            for j in range(ATTN_GROUP):
                t = g * ATTN_GROUP + j
                _, _, _, rows = stream_rows(t)
                loaded.append((t, ks[rows, :], vs[rows, :]))
            for t, kk, vv in loaded:
                kt_s[t] = jnp.transpose(kk).astype(BF16)
                va_s[t, :, 0:ATT_HD] = vv.astype(BF16)
            return carry

        lax.fori_loop(0, nblk // ATTN_GROUP, prep, 0)

        def group(g, carry, br=br, nb=nb, stream_rows=stream_rows, qs=qs):
            scores = []
            for j in range(ATTN_GROUP):
                t = g * ATTN_GROUP + j
                r, ib, rows, src_rows = stream_rows(t)
                tp = jnp.maximum(t - 1, r * nb)
                q = (qs[src_rows, :] * scale).astype(BF16)
                s = _dot(q, jnp.concatenate([kt_s[tp], kt_s[t]], axis=1))
                scores.append((t, tp, ib, src_rows, s))
            probs = []
            for t, tp, ib, rows, s in scores:
                ok = jnp.concatenate([jnp.logical_and(prev_ok, ib > 0), cur_ok], axis=1)
                s = jnp.where(ok, s, NEG_INF)
                m = jnp.max(s, axis=-1, keepdims=True)
                probs.append((t, tp, rows, m, jnp.exp(s - m).astype(BF16)))
            outs = [(rows, m, _dot(p, jnp.concatenate([va_s[tp], va_s[t]], axis=0)))
                    for t, tp, rows, m, p in probs]
            for rows, m, al in outs:
                l = al[:, ATT_HD:]
                on_s[br, rows, :] = al[:, :ATT_HD] / l
                lse_s[br, rows, :] = m + jnp.log(l)
            return carry

        lax.fori_loop(0, nblk // ATTN_GROUP, group, 0)

    nbr = len(dilations)
    copied = [d % DEINTERLEAVE == 0 for _, d in dilations]

    def merge(c, carry):
        for r in range(DEINTERLEAVE):
            natural = pl.ds(r + c * piece * DEINTERLEAVE, piece, stride=DEINTERLEAVE)
            packed = pl.ds(pl.multiple_of(r * quarter + c * piece, piece), piece)
            rows = [packed if copied[b] else natural for b in range(nbr)]
            ls = [lse_s[b, rows[b], :] for b in range(nbr)]
            mx = functools.reduce(jnp.maximum, ls)
            es = [jnp.exp(li - mx) for li in ls]
            num = sum(es[b] * on_s[b, rows[b], :] for b in range(nbr))
            o_ref[natural, :] = num / sum(es)
        return carry

    lax.fori_loop(0, quarter // piece, merge, 0)


def _attn_prompt(proj, *, n_batch, seq, dilations=DILATIONS):
    for w, d in dilations:
        assert w // d == Q_BLOCK and seq % (d * Q_BLOCK) == 0
    nbr = len(dilations)
    nblk = seq // Q_BLOCK
    assert nblk % ATTN_GROUP == 0
    blk = lambda off: pl.BlockSpec((seq, ATT_HD), lambda b, h, off=off: (b, off + h))
    assert seq % (DEINTERLEAVE * 256) == 0
    vmem = ((4 * 2 + 2 * nbr + 3) * seq * ATT_HD * 4 + 3 * seq * ATT_HD * 2) / 2**20 + 8
    return pl.pallas_call(
        functools.partial(_attn_prompt_body, seq=seq, dilations=dilations),
        out_shape=jax.ShapeDtypeStruct((n_batch * seq, D_ATT), F32),
        grid=(n_batch, ATT_HEADS),
        in_specs=[blk(0), blk(ATT_HEADS), blk(2 * ATT_HEADS)],
        out_specs=pl.BlockSpec((seq, ATT_HD), lambda b, h: (b, h)),
        scratch_shapes=[pltpu.VMEM((nblk, ATT_HD, Q_BLOCK), BF16),
                        pltpu.VMEM((nblk, Q_BLOCK, 2 * ATT_HD), BF16),
                        pltpu.VMEM((nbr, seq, ATT_HD), F32),
                        pltpu.VMEM((nbr, seq, ATT_HD), F32)]
        + [pltpu.VMEM((seq, ATT_HD), F32)] * 3,
        compiler_params=_cparams(("parallel", "parallel"), vmem),
        name="attn_prompt",
    )(proj, proj, proj)


def _sample_key_multiplicity(n_new, n_cache, past_len, dilations):
    d_max = max(d for _, d in dilations)
    tail = max(w for w, d in dilations if d != d_max)
    assert past_len % d_max == 0 and n_cache % d_max == 0 and n_new <= d_max // 2
    assert tail % d_max == 0 and tail <= n_cache
    half = d_max // 2
    n_grid = (n_cache - tail) // d_max
    kv_start = past_len - n_cache
    grid_rows = (np.arange(n_grid)[:, None] * d_max + np.arange(half)[None, :]).reshape(-1)
    tail_rows = n_cache - tail + np.arange(tail)
    new_rows = n_cache + np.arange(n_new)
    qpos = past_len + np.arange(n_new)

    def mult(rows):
        kpos = kv_start + rows
        delta = qpos[:, None] - kpos[None, :]
        c = np.zeros(delta.shape, np.float32)
        for w, d in dilations:
            c += ((delta >= 0) & (delta <= w) & (delta % d == 0) & (kpos[None, :] >= kv_start))
        return c

    fetched = np.zeros(n_cache + n_new, bool)
    fetched[grid_rows] = True
    fetched[tail_rows] = True
    fetched[new_rows] = True
    assert not mult(np.nonzero(~fetched)[0]).any()
    return mult(grid_rows), mult(tail_rows), mult(new_rows), n_grid, tail, half, d_max


def _attn_sample_body(q_ref, kn_ref, vn_ref, kg_ref, kt_ref, vg_ref, vt_ref,
                      cg_ref, ct_ref, cn_ref, o_ref):
    scale = ATT_HD ** -0.5
    heads = lambda ref: jnp.concatenate(
        [ref[:, h * ATT_HD:(h + 1) * ATT_HD] for h in range(ATT_HEADS)], axis=0)
    q = (heads(q_ref) * scale).astype(BF16)
    kn = heads(kn_ref).astype(BF16)
    vn = heads(vn_ref).astype(BF16)
    flat = lambda ref: ref[...].reshape(-1, ATT_HD).astype(BF16)
    cg, ct, cn = cg_ref[...], ct_ref[...], cn_ref[...]
    sg = jnp.where(cg > 0, _dot_nt(q, flat(kg_ref)), NEG_INF)
    st = jnp.where(ct > 0, _dot_nt(q, flat(kt_ref)), NEG_INF)
    sn = jnp.where(cn > 0, _dot_nt(q, kn), NEG_INF)
    m = jnp.maximum(jnp.maximum(jnp.max(sg, axis=-1, keepdims=True),
                                jnp.max(st, axis=-1, keepdims=True)),
                    jnp.max(sn, axis=-1, keepdims=True))
    pg = cg * jnp.exp(sg - m)
    pt = ct * jnp.exp(st - m)
    pn = cn * jnp.exp(sn - m)
    l = (jnp.sum(pg, axis=-1, keepdims=True) + jnp.sum(pt, axis=-1, keepdims=True)
         + jnp.sum(pn, axis=-1, keepdims=True))
    acc = (_dot(pg.astype(BF16), flat(vg_ref)) + _dot(pt.astype(BF16), flat(vt_ref))
           + _dot(pn.astype(BF16), vn))
    out = acc / l
    n_new = q_ref.shape[0]
    for h in range(ATT_HEADS):
        o_ref[:, h * ATT_HD:(h + 1) * ATT_HD] = out[h * n_new:(h + 1) * n_new, :]


def _attn_sample(proj, win_k, win_v, *, row0, n_seq, n_new, past_len=PAST_LEN,
                 dilations=DILATIONS):
    n_cache = win_k.shape[1]
    cg, ct, cn, n_grid, tail, half, d_max = _sample_key_multiplicity(
        n_new, n_cache, past_len, dilations)
    assert row0 % n_new == 0 and n_new % SUBLANES == 0 and n_cache % tail == 0
    eye = np.eye(ATT_HEADS, dtype=np.float32)
    key_major = lambda c: np.einsum("tk,hg->htkg", c, eye).reshape(ATT_HEADS * n_new, -1)
    head_major = lambda c: np.einsum("tk,hg->htgk", c, eye).reshape(ATT_HEADS * n_new, -1)
    cg, ct, cn = key_major(cg), key_major(ct), head_major(cn)
    rb = row0 // n_new
    n_groups = n_cache // d_max
    kgv = win_k.reshape(n_seq, n_groups, d_max, ATT_HEADS, ATT_HD)
    vgv = win_v.reshape(n_seq, n_groups, d_max, ATT_HEADS, ATT_HD)
    ktv = win_k.reshape(n_seq, n_cache // tail, tail, ATT_HEADS, ATT_HD)
    vtv = win_v.reshape(n_seq, n_cache // tail, tail, ATT_HEADS, ATT_HD)
    new = lambda off: pl.BlockSpec((n_new, D_ATT), lambda b, off=off: (rb + b, off))
    grid_spec = pl.BlockSpec((None, n_grid, half, ATT_HEADS, ATT_HD), lambda b: (b, 0, 0, 0, 0))
    tail_spec = pl.BlockSpec((None, None, tail, ATT_HEADS, ATT_HD),
                             lambda b: (b, n_cache // tail - 1, 0, 0, 0))
    const = lambda a: pl.BlockSpec(a.shape, lambda b: (0, 0))
    vmem = (2 * 2 * (n_grid * half + tail) * D_ATT * 4 + 4 * cg.size * 4 * 3) / 2**20 + 12
    return pl.pallas_call(
        _attn_sample_body,
        out_shape=jax.ShapeDtypeStruct((n_seq * n_new, D_ATT), F32),
        grid=(n_seq,),
        in_specs=[new(0), new(1), new(2), grid_spec, tail_spec, grid_spec, tail_spec,
                  const(cg), const(ct), const(cn)],
        out_specs=pl.BlockSpec((n_new, D_ATT), lambda b: (b, 0)),
        compiler_params=_cparams(("parallel",), vmem),
        name="attn_sample",
    )(proj, proj, proj, kgv, ktv, vgv, vtv, jnp.asarray(cg), jnp.asarray(ct), jnp.asarray(cn))


def _gelu_tanh(x):
    return 0.5 * x * (1.0 + jnp.tanh(math.sqrt(2.0 / math.pi) * (x + 0.044715 * (x * x * x))))


def _ssm_body(u_ref, bb_ref, cst_ref, a_ref, ap_ref, d_ref, hre_ref, him_ref, *rest,
              tl, npar, seq_len, nseg, emit_y, exact_in):
    if emit_y:
        y_ref, fre_ref, fim_ref, h_s = rest
    else:
        fre_ref, fim_ref, h_s = rest
    ns = SSM_STATES_PER_TILE
    c = pl.program_id(1)
    ngrp = npar // SUBLANES

    def step_rows(i, g):
        return pl.ds(c * tl + i + g * SUBLANES * seq_len, SUBLANES, stride=seq_len)

    @pl.when(c == 0)
    def _init():
        if nseg == 1:
            h_s[0] = hre_ref[...]
            h_s[1] = him_ref[...]
        else:
            pr, pi = ap_ref[0:1, :], ap_ref[1:2, :]
            for b in range(npar // nseg):
                sr = jnp.zeros((1, ns), F32)
                si = jnp.zeros((1, ns), F32)
                for j in range(nseg):
                    row = b * nseg + j
                    h_s[0, row:row + 1, :] = sr
                    h_s[1, row:row + 1, :] = si
                    er, ei = hre_ref[row:row + 1, :], him_ref[row:row + 1, :]
                    sr, si = pr * sr - pi * si + er, pr * si + pi * sr + ei

    ar = jnp.broadcast_to(a_ref[0:1, :], (SUBLANES, ns))
    ai = jnp.broadcast_to(a_ref[1:2, :], (SUBLANES, ns))
    us = [jnp.concatenate([u_ref[step_rows(i, g), :] for i in range(tl)], axis=0)
          for g in range(ngrp)]
    if exact_in:
        xs = [jnp.dot(u, bb_ref[...], precision=lax.Precision.HIGHEST, preferred_element_type=F32)
              for u in us]
    else:
        xs = [_dot(u.astype(BF16), bb_ref[...]) for u in us]

    hs = []
    for g in range(ngrp):
        gs = slice(g * SUBLANES, (g + 1) * SUBLANES)
        hr, hi = h_s[0, gs, :], h_s[1, gs, :]
        states = []
        for i in range(tl):
            xr = xs[g][i * SUBLANES:(i + 1) * SUBLANES, 0:ns]
            xi = xs[g][i * SUBLANES:(i + 1) * SUBLANES, ns:2 * ns]
            hr, hi = ar * hr - ai * hi + xr, ar * hi + ai * hr + xi
            if emit_y:
                states.append(jnp.concatenate([hr, hi], axis=1))
        h_s[0, gs, :] = hr
        h_s[1, gs, :] = hi
        if emit_y:
            hs.append(jnp.concatenate(states, axis=0))

    if emit_y:
        for g in range(ngrp):
            y = _gelu_tanh(_dot(hs[g].astype(BF16), cst_ref[...]) + d_ref[...] * us[g])
            for i in range(tl):
                y_ref[step_rows(i, g), :] = y[i * SUBLANES:(i + 1) * SUBLANES, :]

    @pl.when(c == pl.num_programs(1) - 1)
    def _fin():
        fre_ref[...] = h_s[0]
        fim_ref[...] = h_s[1]


def _ssm_scan(proj, prm, hin_re, hin_im, *, seq_len, tl, nseg, emit_y, exact_in, name):
    rows = proj.shape[0]
    npar = rows // seq_len
    assert npar % SUBLANES == 0 and seq_len % tl == 0
    ns = SSM_STATES_PER_TILE
    nk = D_SSM // SSM_LANE_TILE
    col0 = (proj.shape[1] - D_SSM) // SSM_LANE_TILE
    bb = prm["bb_f32"] if exact_in else prm["bb_bf16"]
    in_specs = [
        pl.BlockSpec((rows, SSM_LANE_TILE), lambda k, c: (0, col0 + k)),
        pl.BlockSpec((None, SSM_LANE_TILE, 2 * ns), lambda k, c: (k, 0, 0)),
        pl.BlockSpec((None, 2 * ns, SSM_LANE_TILE), lambda k, c: (k, 0, 0)),
        pl.BlockSpec((None, 2, ns), lambda k, c: (k, 0, 0)),
        pl.BlockSpec((None, 2, ns), lambda k, c: (k, 0, 0)),
        pl.BlockSpec((1, SSM_LANE_TILE), lambda k, c: (0, k)),
        pl.BlockSpec((npar, ns), lambda k, c: (0, k)),
        pl.BlockSpec((npar, ns), lambda k, c: (0, k)),
    ]
    state_shape = jax.ShapeDtypeStruct((npar, nk * ns), F32)
    state_spec = pl.BlockSpec((npar, ns), lambda k, c: (0, k))
    out_shape = [state_shape, state_shape]
    out_specs = [state_spec, state_spec]
    if emit_y:
        out_shape = [jax.ShapeDtypeStruct((rows, D_SSM), F32)] + out_shape
        out_specs = [pl.BlockSpec((rows, SSM_LANE_TILE), lambda k, c: (0, k))] + out_specs
    vmem = (4 * rows * SSM_LANE_TILE * 4 + tl * npar * 2 * ns * 4) / 2**20 + 16
    return pl.pallas_call(
        functools.partial(_ssm_body, tl=tl, npar=npar, seq_len=seq_len, nseg=nseg, emit_y=emit_y,
                          exact_in=exact_in),
        out_shape=out_shape,
        grid=(nk, seq_len // tl),
        in_specs=in_specs,
        out_specs=out_specs,
        scratch_shapes=[pltpu.VMEM((2, npar, ns), F32)],
        compiler_params=_cparams(("parallel", "arbitrary"), vmem),
        name=name,
    )(proj, bb, prm["cst"], prm["a"], prm["apow"], prm["d"], hin_re, hin_im)


def _ssm_params(lam_re, lam_im, log_dt, b_re, b_im, c_re, c_im, d_skip, seg_len):
    g, p, c = N_SSM_GROUPS, SSM_STATE, SSM_GROUP_CH
    nk, gt = g // SSM_GROUPS_PER_TILE, SSM_GROUPS_PER_TILE
    dt = jnp.exp(log_dt.astype(F32))[:, None]
    lr, li = lam_re.astype(F32), lam_im.astype(F32)
    mag = jnp.exp(lr * dt)
    a_re, a_im = mag * jnp.cos(li * dt), mag * jnp.sin(li * dt)
    magp = jnp.exp(lr * dt * seg_len)
    p_re, p_im = magp * jnp.cos(li * dt * seg_len), magp * jnp.sin(li * dt * seg_len)
    den = lr * lr + li * li
    nr, ni = a_re - 1.0, a_im
    f_re, f_im = (nr * lr + ni * li) / den, (ni * lr - nr * li) / den
    br, bi = b_re.astype(F32), b_im.astype(F32)
    bb_re = f_re[..., None] * br - f_im[..., None] * bi
    bb_im = f_re[..., None] * bi + f_im[..., None] * br
    eye = jnp.eye(gt, dtype=F32)

    def pack_b(m):
        return jnp.einsum("kgpc,gh->kgchp", m.reshape(nk, gt, p, c), eye).reshape(nk, gt * c, gt * p)

    def pack_c(m):
        return jnp.einsum("kgcp,gh->kgphc", m.reshape(nk, gt, c, p), eye).reshape(nk, gt * p, gt * c)

    bb = jnp.concatenate([pack_b(bb_re), pack_b(bb_im)], axis=2)
    cst = jnp.concatenate([pack_c(c_re.astype(F32)), -pack_c(c_im.astype(F32))], axis=1)
    tile = lambda v: v.reshape(nk, 1, gt * p)
    return {
        "bb_f32": bb, "bb_bf16": bb.astype(BF16), "cst": cst.astype(BF16),
        "a": jnp.concatenate([tile(a_re), tile(a_im)], axis=1),
        "apow": jnp.concatenate([tile(p_re), tile(p_im)], axis=1),
        "d": d_skip.astype(F32).reshape(1, g * c),
    }


def _glu_body(y_ref, w_ref, o_ref):
    yg = y_ref[...]
    z = _dot(yg.astype(BF16), w_ref[...])
    o_ref[...] = yg * (1.0 / (1.0 + jnp.exp(-z)))


def _glu(yg, w, *, tm, name):
    m, n = yg.shape
    return pl.pallas_call(
        _glu_body,
        out_shape=jax.ShapeDtypeStruct((m, n), F32),
        grid=(m // tm,),
        in_specs=[pl.BlockSpec((tm, n), lambda i: (i, 0)), pl.BlockSpec((n, n), lambda i: (0, 0))],
        out_specs=pl.BlockSpec((tm, n), lambda i: (i, 0)),
        compiler_params=_cparams(("parallel",), 4 * tm * n * 4 / 2**20 + 12),
        name=name,
    )(yg, w)


def _mix_body(attn_ref, ssm_ref, ga_ref, gs_ref, w_ref, x_ref, g_ref, b_ref, o_ref):
    half = attn_ref.shape[0] // ROW_SPLIT
    for r in range(ROW_SPLIT):
        rows = slice(r * half, (r + 1) * half)
        a = _rmsnorm(attn_ref[rows, :], ga_ref[...]).astype(BF16)
        s = _rmsnorm(ssm_ref[rows, :], gs_ref[...]).astype(BF16)
        mix = _dot(jnp.concatenate([a, s], axis=1), w_ref[...])
        o_ref[rows, :] = _layernorm(DEEPNORM_ALPHA * x_ref[rows, :] + mix, g_ref[...], b_ref[...])


def _mix(attn, ssm, ga, gs, w, x, g, b, *, tm, name):
    m = x.shape[0]
    row = lambda n: pl.BlockSpec((tm, n), lambda i: (i, 0))
    const = lambda a: pl.BlockSpec(a.shape, lambda i: (0, 0))
    return pl.pallas_call(
        _mix_body,
        out_shape=jax.ShapeDtypeStruct((m, D_MODEL), F32),
        grid=(m // tm,),
        in_specs=[row(D_ATT), row(D_SSM), const(ga), const(gs), const(w), row(D_MODEL),
                  const(g), const(b)],
        out_specs=row(D_MODEL),
        compiler_params=_cparams(("parallel",), 6 * tm * D_MODEL * 4 / 2**20 + 24),
        name=name,
    )(attn, ssm, ga, gs, w, x, g, b)


def _store_gatherable(o_ref, y):
    rows = y.shape[0]
    for c in range(ROW_CHUNKS):
        o_ref[pl.ds(c, rows, stride=ROW_PITCH), :] = y[:, c * LANES:(c + 1) * LANES]
    for c in range(ROW_CHUNKS, ROW_PITCH):
        o_ref[pl.ds(c, rows, stride=ROW_PITCH), :] = jnp.zeros((rows, LANES), F32)


def _load_gathered(buf, rows):
    return jnp.concatenate([buf[pl.ds(c, rows, stride=ROW_PITCH), :] for c in range(ROW_CHUNKS)],
                           axis=1)


def _start_row_gather(src_hbm, idx, buf, r, sem):
    pltpu.make_async_copy(src_hbm.at[pl.ds(idx * ROW_PITCH, ROW_CHUNKS), :],
                          buf.at[pl.ds(r * ROW_PITCH, ROW_CHUNKS), :], sem).start()


def _wait_row_gathers(buf, other, rows, sem):
    span = pl.ds(0, rows * ROW_CHUNKS)
    pltpu.make_async_copy(other.at[span, :], buf.at[span, :], sem).wait()


def _mm_ln_body(a1_ref, a2_ref, w_ref, x1_ref, x2_ref, g_ref, b_ref, wr_ref, br_ref,
                o_ref, rows_ref, sel_ref, wts_ref, cnt_ref, run_s, *, tiles1):
    first = pl.program_id(0) < tiles1
    half = a1_ref.shape[0] // ROW_SPLIT
    outs = []
    for r in range(ROW_SPLIT):
        rows = slice(r * half, (r + 1) * half)
        a = jnp.where(first, a1_ref[rows, :].astype(BF16), a2_ref[rows, :].astype(BF16))
        x = jnp.where(first, x1_ref[rows, :], x2_ref[rows, :])
        y = _dot(a, w_ref[...])
        out = _layernorm(DEEPNORM_ALPHA * x + y, g_ref[...], b_ref[...])
        o_ref[rows, :] = out
        outs.append(out)
    out = jnp.concatenate(outs, axis=0)
    _store_gatherable(rows_ref, out)
    _route_tile(out, wr_ref, br_ref, sel_ref, wts_ref, cnt_ref, run_s)


def _mm_ln(a1, a2, w, x1, x2, g, b, w_r, b_r, *, name):
    tm = a2.shape[0]
    assert a1.shape[0] % tm == 0
    tiles1 = a1.shape[0] // tm
    m = a1.shape[0] + tm
    row1 = lambda n: pl.BlockSpec((tm, n), lambda i: (jnp.minimum(i, tiles1 - 1), 0))
    row2 = lambda n: pl.BlockSpec((tm, n), lambda i: (0, 0))
    const = lambda v: pl.BlockSpec(v.shape, lambda i: (0, 0))
    lanes = pl.BlockSpec((tm, ROUTER_LANES), lambda i: (i, 0))
    return pl.pallas_call(
        functools.partial(_mm_ln_body, tiles1=tiles1),
        out_shape=[jax.ShapeDtypeStruct((m, D_MODEL), F32),
                   jax.ShapeDtypeStruct((m * ROW_PITCH, LANES), F32),
                   jax.ShapeDtypeStruct((m, ROUTER_LANES), I32),
                   jax.ShapeDtypeStruct((m, ROUTER_LANES), F32),
                   jax.ShapeDtypeStruct((1, ROUTER_LANES), I32)],
        grid=(tiles1 + 1,),
        in_specs=[row1(a1.shape[1]), row2(a2.shape[1]), const(w), row1(D_MODEL), row2(D_MODEL),
                  const(g), const(b), const(w_r), const(b_r)],
        out_specs=[pl.BlockSpec((tm, D_MODEL), lambda i: (i, 0)),
                   pl.BlockSpec((tm * ROW_PITCH, LANES), lambda i: (i, 0)),
                   lanes, lanes, pl.BlockSpec((1, ROUTER_LANES), lambda i: (0, 0))],
        scratch_shapes=[pltpu.VMEM((1, ROUTER_LANES), F32)],
        compiler_params=_cparams(("arbitrary",), 12 * tm * D_MODEL * 4 / 2**20 + 24),
        name=name,
    )(a1, a2, w, x1, x2, g, b, w_r, b_r)


def _memattn_body(q_ref, k_ref, v_ref, o_ref):
    scale = MEM_HD ** -0.5
    for h in range(MEM_HEADS):
        sl = slice(h * MEM_HD, (h + 1) * MEM_HD)
        s = _dot_nt(q_ref[:, sl].astype(BF16), k_ref[:, sl].astype(BF16)) * scale
        m = jnp.max(s, axis=-1, keepdims=True)
        p = jnp.exp(s - m)
        l = jnp.sum(p, axis=-1, keepdims=True)
        o_ref[:, sl] = (_dot(p.astype(BF16), v_ref[:, sl].astype(BF16)) / l).astype(o_ref.dtype)


def _memattn(q, mem_k, mem_v, *, row0, n_seq, seq, tq, name):
    assert seq % tq == 0 and row0 % tq == 0
    nq = seq // tq
    rb = row0 // tq
    mem_spec = pl.BlockSpec((None, N_MEM, D_MODEL), lambda b, i: (b, 0, 0))
    return pl.pallas_call(
        _memattn_body,
        out_shape=jax.ShapeDtypeStruct((n_seq * seq, D_MODEL), q.dtype),
        grid=(n_seq, nq),
        in_specs=[pl.BlockSpec((tq, D_MODEL), lambda b, i: (rb + b * nq + i, 0)),
                  mem_spec, mem_spec],
        out_specs=pl.BlockSpec((tq, D_MODEL), lambda b, i: (b * nq + i, 0)),
        compiler_params=_cparams(("parallel", "parallel"),
                                 4 * (tq + N_MEM) * D_MODEL * 4 / 2**20 + 8),
        name=name,
    )(q, mem_k, mem_v)


def _memattn_heads_body(q_ref, k_ref, v_ref, c_ref, o_ref):
    scale = MEM_HD ** -0.5
    tq = q_ref.shape[0]
    q = jnp.concatenate([q_ref[:, h * MEM_HD:(h + 1) * MEM_HD] for h in range(MEM_HEADS)], axis=0)
    k = k_ref[...].reshape(N_MEM * MEM_HEADS, MEM_HD).astype(BF16)
    v = v_ref[...].reshape(N_MEM * MEM_HEADS, MEM_HD).astype(BF16)
    s = jnp.where(c_ref[...] > 0, _dot_nt(q.astype(BF16), k) * scale, NEG_INF)
    m = jnp.max(s, axis=-1, keepdims=True)
    p = jnp.exp(s - m)
    l = jnp.sum(p, axis=-1, keepdims=True)
    o = _dot(p.astype(BF16), v) / l
    for h in range(MEM_HEADS):
        o_ref[:, h * MEM_HD:(h + 1) * MEM_HD] = o[h * tq:(h + 1) * tq, :]


def _memattn_heads(q, mem_k, mem_v, *, row0, n_seq, seq, name):
    assert row0 % seq == 0 and seq % SUBLANES == 0
    rb = row0 // seq
    same_head = np.kron(np.eye(MEM_HEADS, dtype=np.float32), np.ones((seq, 1), np.float32))
    same_head = np.tile(same_head, (1, N_MEM))
    mem_spec = pl.BlockSpec((None, N_MEM, MEM_HEADS, MEM_HD), lambda b: (b, 0, 0, 0))
    return pl.pallas_call(
        _memattn_heads_body,
        out_shape=jax.ShapeDtypeStruct((n_seq * seq, D_MODEL), F32),
        grid=(n_seq,),
        in_specs=[pl.BlockSpec((seq, D_MODEL), lambda b: (rb + b, 0)), mem_spec, mem_spec,
                  pl.BlockSpec(same_head.shape, lambda b: (0, 0))],
        out_specs=pl.BlockSpec((seq, D_MODEL), lambda b: (b, 0)),
        compiler_params=_cparams(("parallel",), 8 * N_MEM * D_MODEL * 4 / 2**20 + 8),
        name=name,
    )(q, mem_k, mem_v, jnp.asarray(same_head))


def _route_tile(x, w_ref, b_ref, sel_ref, wts_ref, cnt_ref, run_s):
    tm = x.shape[0]

    @pl.when(pl.program_id(0) == 0)
    def _():
        run_s[...] = jnp.zeros_like(run_s)

    ng, epg = N_EXPERT_GROUPS, EXPERTS_PER_GROUP
    x_hi = x.astype(BF16)
    x_lo = (x - x_hi.astype(F32)).astype(BF16)
    parts = _dot(x_hi, w_ref[...]) + _dot(x_lo, w_ref[...])
    logits = parts + pltpu.roll(parts, shift=ROUTER_LANES // 2, axis=1) + b_ref[...]
    lane = lax.broadcasted_iota(I32, (tm, ROUTER_LANES), 1)
    big = ROUTER_LANES

    def first_argmax(vals):
        mx = jnp.max(vals, axis=-1, keepdims=True)
        idx = jnp.min(jnp.where(vals == mx, lane, big), axis=-1, keepdims=True)
        return mx, idx

    gl = jnp.where(lane < ng, logits, NEG_INF)
    gmax, gsel = first_argmax(gl)
    g_w = 1.0 / jnp.sum(jnp.exp(gl - gmax), axis=-1, keepdims=True)
    lo = ng + gsel * epg
    el = jnp.where(jnp.logical_and(lane >= lo, lane < lo + epg), logits, NEG_INF)
    v1, i1 = first_argmax(el)
    v2, i2 = first_argmax(jnp.where(lane == i1, NEG_INF, el))
    e21 = jnp.exp(v2 - v1)
    w1 = g_w / (1.0 + e21)
    w2 = g_w * e21 / (1.0 + e21)

    onehot = jnp.logical_or(lane == i1, lane == i2)
    r = lax.broadcasted_iota(I32, (tm, tm), 0)
    cc = lax.broadcasted_iota(I32, (tm, tm), 1)
    tri = (cc < r).astype(BF16)
    before = _dot(tri, onehot.astype(BF16)) + run_s[...]
    rank1 = jnp.sum(jnp.where(lane == i1, before, 0.0), axis=-1, keepdims=True).astype(I32)
    rank2 = jnp.sum(jnp.where(lane == i2, before, 0.0), axis=-1, keepdims=True).astype(I32)
    run_s[...] = run_s[...] + jnp.sum(onehot.astype(F32), axis=0, keepdims=True)

    sel = jnp.where(lane == 0, i1 - ng, jnp.where(lane == 1, i2 - ng,
                    jnp.where(lane == 2, rank1, jnp.where(lane == 3, rank2, 0))))
    sel_ref[...] = sel
    wts_ref[...] = jnp.where(lane == 0, w1, jnp.where(lane == 1, w2, 0.0))
    cnt_ref[...] = run_s[...].astype(I32)


def _router_weights(w_r1, b_r1, w_r2, b_r2):
    ng, ne = N_EXPERT_GROUPS, N_EXPERTS
    half = ROUTER_LANES // 2
    assert ng + ne <= half
    w_r = jnp.concatenate([w_r1, w_r2.reshape(D_MODEL, ne),
                           jnp.zeros((D_MODEL, half - ng - ne), F32)], axis=1)
    w_hi = w_r.astype(BF16)
    w_lo = (w_r - w_hi.astype(F32)).astype(BF16)
    b_r = jnp.concatenate([b_r1, b_r2.reshape(ne), jnp.zeros((half - ng - ne,), F32)])
    return jnp.concatenate([w_hi, w_lo], axis=1), jnp.concatenate([b_r, b_r]).reshape(1, ROUTER_LANES)


def _dispatch_body(eid_ref, rank_ref, off_ref, pad0_ref, npad_ref, nact_ref, x_hbm, xs_hbm, xbuf,
                   zero_s, in_sem, out_sem, pad_sem, *, td, n_tiles):
    i = pl.program_id(0)
    n = pl.num_programs(0)
    tile_rows = MOE_TILE * ROW_PITCH
    in_rows = td * ROW_PITCH

    def row_copy(src, dst_row, s):
        return pltpu.make_async_copy(src, xs_hbm.at[pl.ds(dst_row * ROW_PITCH, ROW_PITCH), :], s)

    def tile_load(t):
        s = t % DISPATCH_SLOTS
        return pltpu.make_async_copy(x_hbm.at[pl.ds(t * in_rows, in_rows), :], xbuf.at[s],
                                     in_sem.at[s])

    def wait_scatter(par):
        for _ in range(2):
            pltpu.make_async_copy(xbuf.at[0], xs_hbm.at[pl.ds(0, in_rows), :],
                                  out_sem.at[par]).wait()

    @pl.when(i == 0)
    def _():
        for t in range(DISPATCH_SLOTS - 1):
            @pl.when(t < n)
            def _(t=t):
                tile_load(t).start()
        zero_s[...] = jnp.zeros_like(zero_s)

        def pad_copies(e):
            out = []
            for bit in reversed(range(MOE_TILE.bit_length() - 1)):
                rows = (1 << bit) * ROW_PITCH
                first = (pad0_ref[e] + (npad_ref[e] >> (bit + 1) << (bit + 1))) * ROW_PITCH
                out.append((jnp.bitwise_and(npad_ref[e] >> bit, 1) == 1, pltpu.make_async_copy(
                    zero_s.at[pl.ds(0, rows), :], xs_hbm.at[pl.ds(first, rows), :], pad_sem.at[0])))
            return out

        for e in range(N_EXPERTS):
            for on, cp in pad_copies(e):
                pl.when(on)(cp.start)
        for e in range(N_EXPERTS):
            for on, cp in pad_copies(e):
                pl.when(on)(cp.wait)

        def zero_tile(t, carry):
            parts = [pltpu.make_async_copy(
                zero_s.at[pl.ds(0, MOE_TILE), :],
                xs_hbm.at[pl.ds(t * tile_rows + j * MOE_TILE, MOE_TILE), :],
                pad_sem.at[0]) for j in range(ROW_PITCH)]
            for cp in parts:
                cp.start()
            for cp in parts:
                cp.wait()
            return carry

        lax.fori_loop(nact_ref[0], n_tiles, zero_tile, 0)

    slot = i % DISPATCH_SLOTS
    par = i % 2
    tile_load(i).wait()
    base = i * td * 2
    for r in range(td):
        for k in range(2):
            j = base + 2 * r + k
            row_copy(xbuf.at[slot, pl.ds(r * ROW_PITCH, ROW_PITCH), :],
                     off_ref[eid_ref[j]] + rank_ref[j], out_sem.at[par]).start()

    @pl.when(i > 0)
    def _():
        wait_scatter(1 - par)

    @pl.when(i + DISPATCH_SLOTS - 1 < n)
    def _():
        tile_load(i + DISPATCH_SLOTS - 1).start()

    @pl.when(i == n - 1)
    def _():
        wait_scatter(par)


def _moe_dispatch(x_rows, eid, rank, row_off, pad_start, pad_count, nact, *, td, n_tiles):
    n = x_rows.shape[0] // ROW_PITCH
    grid_spec = pltpu.PrefetchScalarGridSpec(
        num_scalar_prefetch=6,
        grid=(n // td,),
        in_specs=[pl.BlockSpec(memory_space=pl.ANY)],
        out_specs=pl.BlockSpec(memory_space=pl.ANY),
        scratch_shapes=[pltpu.VMEM((DISPATCH_SLOTS, td * ROW_PITCH, LANES), F32),
                        pltpu.VMEM((MOE_TILE // 2 * ROW_PITCH, LANES), F32),
                        pltpu.SemaphoreType.DMA((DISPATCH_SLOTS,)),
                        pltpu.SemaphoreType.DMA((2,)),
                        pltpu.SemaphoreType.DMA((1,))],
    )
    return pl.pallas_call(
        functools.partial(_dispatch_body, td=td, n_tiles=n_tiles),
        out_shape=jax.ShapeDtypeStruct((n_tiles * MOE_TILE * ROW_PITCH, LANES), F32),
        grid_spec=grid_spec,
        compiler_params=_cparams(("arbitrary",), 16),
        name="moe_dispatch",
    )(eid, rank, row_off, pad_start, pad_count, nact, x_rows)


def _moe_body(te_ref, ord_ref, nxt_ref, nact_ref, x_ref, wg_hbm, wu_hbm, wd_hbm, o_ref,
              wg_f, wu_f, wd_f, wsem, wg_s, wu_s, wd_s):
    i = pl.program_id(0)
    nact = nact_ref[0]
    tm = MOE_TILE

    def weight_copies(expert, ws):
        return [pltpu.make_async_copy(hbm.at[expert], stage.at[ws], wsem.at[ws])
                for hbm, stage in ((wg_hbm, wg_f), (wu_hbm, wu_f), (wd_hbm, wd_f))]

    def ffn(wg, wu, wd):
        x = _load_gathered(x_ref, tm).astype(BF16)
        hg = _dot(x, wg)
        hu = _dot(x, wu)
        h = hg * (1.0 / (1.0 + jnp.exp(-hg))) * hu
        _store_gatherable(o_ref, _dot(h.astype(BF16), wd))

    def tile_step():
        first = jnp.logical_or(i == 0, te_ref[i] != te_ref[jnp.maximum(i - 1, 0)])

        @pl.when(first)
        def _():
            ws = ord_ref[i] % 2
            for cp in weight_copies(te_ref[i], ws):
                cp.wait()

            @pl.when(nxt_ref[i] >= 0)
            def _():
                for cp in weight_copies(nxt_ref[i], 1 - ws):
                    cp.start(priority=1)

            wg = wg_f[ws].astype(BF16)
            wu = wu_f[ws].astype(BF16)
            wd = wd_f[ws].astype(BF16)
            wg_s[...] = wg
            wu_s[...] = wu
            wd_s[...] = wd
            ffn(wg, wu, wd)

        @pl.when(jnp.logical_not(first))
        def _():
            ffn(wg_s[...], wu_s[...], wd_s[...])

    @pl.when(i == 0)
    def _():
        for cp in weight_copies(te_ref[0], 0):
            cp.start(priority=1)

    @pl.when(i < nact)
    def _():
        tile_step()

    @pl.when(i >= nact)
    def _():
        o_ref[...] = jnp.zeros_like(o_ref)


def _moe_experts(x_sorted, w_gate, w_up, w_down, tile_expert, tile_ord, tile_next, nact, *,
                 n_tiles):
    tm = MOE_TILE
    in_map = lambda i, te, od, nx, n: (jnp.minimum(i, n[0] - 1), 0)
    any_spec = pl.BlockSpec(memory_space=pl.ANY)
    grid_spec = pltpu.PrefetchScalarGridSpec(
        num_scalar_prefetch=4,
        grid=(n_tiles,),
        in_specs=[pl.BlockSpec((tm * ROW_PITCH, LANES), in_map), any_spec, any_spec, any_spec],
        out_specs=pl.BlockSpec((tm * ROW_PITCH, LANES), lambda i, te, od, nx, n: (i, 0)),
        scratch_shapes=[pltpu.VMEM((2, D_MODEL, D_EXPERT), F32),
                        pltpu.VMEM((2, D_MODEL, D_EXPERT), F32),
                        pltpu.VMEM((2, D_EXPERT, D_MODEL), F32),
                        pltpu.SemaphoreType.DMA((2,)),
                        pltpu.VMEM((D_MODEL, D_EXPERT), BF16),
                        pltpu.VMEM((D_MODEL, D_EXPERT), BF16),
                        pltpu.VMEM((D_EXPERT, D_MODEL), BF16)],
    )
    return pl.pallas_call(
        _moe_body,
        out_shape=jax.ShapeDtypeStruct((n_tiles * tm * ROW_PITCH, LANES), F32),
        grid_spec=grid_spec,
        compiler_params=_cparams(("arbitrary",), 48),
        name="moe_experts",
    )(tile_expert, tile_ord, tile_next, nact, x_sorted, w_gate, w_up, w_down)


def _combine_body(eid_ref, rank_ref, off_ref, ys_hbm, wts_ref, x_ref, g_ref, b_ref, o1_ref, o2_ref,
                  buf, sem, *, tc, tiles1):
    i = pl.program_id(0)
    n = pl.num_programs(0)
    slot = i % GATHER_SLOTS
    ahead = GATHER_SLOTS - 1

    def issue_gather(tile, slot_):
        base = tile * tc * 2
        for r in range(tc):
            for k in range(2):
                j = base + 2 * r + k
                _start_row_gather(ys_hbm, off_ref[eid_ref[j]] + rank_ref[j], buf.at[slot_, k], r,
                                  sem.at[slot_])

    @pl.when(i == 0)
    def _():
        for t in range(ahead):
            @pl.when(t < n)
            def _(t=t):
                issue_gather(t, t)

    for k in range(2):
        _wait_row_gathers(buf.at[slot, k], buf.at[(i + 1) % GATHER_SLOTS, k], tc, sem.at[slot])

    @pl.when(i + ahead < n)
    def _():
        issue_gather(i + ahead, (i + ahead) % GATHER_SLOTS)

    w = wts_ref[...]
    moe = (w[:, 0:1] * _load_gathered(buf.at[slot, 0], tc)
           + w[:, 1:2] * _load_gathered(buf.at[slot, 1], tc))
    out = _layernorm(DEEPNORM_ALPHA * x_ref[...] + moe, g_ref[...], b_ref[...])

    @pl.when(i < tiles1)
    def _():
        o1_ref[...] = out

    @pl.when(i >= tiles1)
    def _():
        o2_ref[...] = out


def _moe_combine(ys, eid, rank, row_off, wts, x, g, b, *, tc, n_first):
    m = x.shape[0]
    assert n_first % tc == 0 and (m - n_first) % tc == 0
    tiles1 = n_first // tc
    grid_spec = pltpu.PrefetchScalarGridSpec(
        num_scalar_prefetch=3,
        grid=(m // tc,),
        in_specs=[pl.BlockSpec(memory_space=pl.ANY),
                  pl.BlockSpec((tc, ROUTER_LANES), lambda i, *_: (i, 0)),
                  pl.BlockSpec((tc, D_MODEL), lambda i, *_: (i, 0)),
                  pl.BlockSpec((1, D_MODEL), lambda i, *_: (0, 0)),
                  pl.BlockSpec((1, D_MODEL), lambda i, *_: (0, 0))],
        out_specs=[pl.BlockSpec((tc, D_MODEL), lambda i, *_: (jnp.minimum(i, tiles1 - 1), 0)),
                   pl.BlockSpec((tc, D_MODEL), lambda i, *_: (jnp.maximum(i - tiles1, 0), 0))],
        scratch_shapes=[pltpu.VMEM((GATHER_SLOTS, 2, tc * ROW_PITCH, LANES), F32),
                        pltpu.SemaphoreType.DMA((GATHER_SLOTS,))],
    )
    return pl.pallas_call(
        functools.partial(_combine_body, tc=tc, tiles1=tiles1),
        out_shape=[jax.ShapeDtypeStruct((n_first, D_MODEL), F32),
                   jax.ShapeDtypeStruct((m - n_first, D_MODEL), F32)],
        grid_spec=grid_spec,
        compiler_params=_cparams(("arbitrary",), 16 * tc * D_MODEL * 4 / 2**20 + 8),
        name="moe_combine_ln3",
    )(eid, rank, row_off, ys, wts, x, g, b)


def _moe(x, x_rows, sel, wts, cnt, w_gate, w_up, w_down, g, b, *, n_first, tc):
    n = x.shape[0]
    ng, ne = N_EXPERT_GROUPS, N_EXPERTS

    tm = MOE_TILE
    n_tiles = (2 * n) // tm + ne
    counts = cnt[0, ng:ng + ne]
    tiles_per = (counts + tm - 1) // tm
    tile_end = jnp.cumsum(tiles_per)
    row_off = (tile_end - tiles_per) * tm
    nact = tile_end[-1]
    a_eid, a_rank = sel[:, 0:2].reshape(-1), sel[:, 2:4].reshape(-1)
    row_off = row_off.astype(I32)
    tile_ids = jnp.minimum(jnp.arange(n_tiles, dtype=I32), nact - 1)
    tile_expert = jnp.sum((tile_end[None, :] <= tile_ids[:, None]).astype(I32), axis=1)
    used = tiles_per > 0
    eid = jnp.arange(ne, dtype=I32)
    ordinal = jnp.cumsum(used.astype(I32)) - 1
    later = jnp.where(jnp.logical_and(used[None, :], eid[None, :] > eid[:, None]), eid[None, :], ne)
    nxt = jnp.min(later, axis=1)
    nxt = jnp.where(nxt == ne, -1, nxt)

    nact = nact.reshape(1).astype(I32)
    x_sorted = _moe_dispatch(x_rows, a_eid, a_rank, row_off, (row_off + counts).astype(I32),
                             (tiles_per * tm - counts).astype(I32), nact, td=tc, n_tiles=n_tiles)
    ys = _moe_experts(x_sorted, w_gate, w_up, w_down, tile_expert, ordinal[tile_expert],
                      nxt[tile_expert], nact, n_tiles=n_tiles)
    return _moe_combine(ys, a_eid, a_rank, row_off, wts, x, g, b, tc=tc, n_first=n_first)


def _row_tile(m, cap):
    best = SUBLANES
    for t in range(SUBLANES, cap + 1, SUBLANES):
        if m % t == 0:
            best = t
    return best


def kernel(x_prompt, x_sample, cache_win_k, cache_win_v, state_ssm_re, state_ssm_im, cache_mem_k, cache_mem_v, mem_prompt, w_in, ssm_lam_re, ssm_lam_im, ssm_log_dt, ssm_b_re, ssm_b_im, ssm_c_re, ssm_c_im, ssm_d, w_glu, g_attn, g_ssm, w_out, ln1_g, ln1_b, w_mq, w_mk, w_mv, w_mo, ln2_g, ln2_b, w_r1, b_r1, w_r2, b_r2, w_gate, w_up, w_down, ln3_g, ln3_b):
    nb, seq, d = x_prompt.shape
    ns, dseq, _ = x_sample.shape
    n_p, n_s = nb * seq, ns * dseq
    n = n_p + n_s
    l = 0
    row2 = lambda v: v[l].reshape(1, -1)

    x_p, x_s = x_prompt.reshape(n_p, d), x_sample.reshape(n_s, d)
    tm_p = _row_tile(n_p, 1024)
    tm_ln = _row_tile(n_p, 512)
    assert n_p % n_s == 0 and n_s % SUBLANES == 0

    proj_p = _matmul(x_p, w_in[l], tm=tm_p, tn=1024, name="proj_in_prompt")
    proj_s = _matmul(x_s, w_in[l], tm=n_s, tn=1024, name="proj_in_sample")

    attn_p = _attn_prompt(proj_p, n_batch=nb, seq=seq)
    attn_s = _attn_sample(proj_s, cache_win_k[l], cache_win_v[l], row0=0, n_seq=ns, n_new=dseq)

    seg_len = seq // SSM_SEGMENTS
    prm = _ssm_params(ssm_lam_re[l], ssm_lam_im[l], ssm_log_dt[l], ssm_b_re[l], ssm_b_im[l],
                      ssm_c_re[l], ssm_c_im[l], ssm_d[l], seg_len)
    zeros = jnp.zeros((nb * SSM_SEGMENTS, N_SSM_GROUPS * SSM_STATE), F32)
    tl = _row_tile(seg_len, 32)
    end_re, end_im = _ssm_scan(proj_p, prm, zeros, zeros, seq_len=seg_len, tl=tl, nseg=1,
                               emit_y=False, exact_in=False, name="ssm_state_prompt")
    yg_p, fin_re, fin_im = _ssm_scan(proj_p, prm, end_re, end_im, seq_len=seg_len, tl=tl,
                                     nseg=SSM_SEGMENTS, emit_y=True, exact_in=False,
                                     name="ssm_scan_prompt")
    last = SSM_SEGMENTS - 1
    ssm_re_p = fin_re.reshape(nb, SSM_SEGMENTS, N_SSM_GROUPS, SSM_STATE)[:, last]
    ssm_im_p = fin_im.reshape(nb, SSM_SEGMENTS, N_SSM_GROUPS, SSM_STATE)[:, last]

    h0_re = state_ssm_re[l].reshape(ns, -1)
    h0_im = state_ssm_im[l].reshape(ns, -1)
    yg_s, ssm_re_s, ssm_im_s = _ssm_scan(proj_s, prm, h0_re, h0_im, seq_len=dseq, tl=dseq, nseg=1,
                                         emit_y=True, exact_in=True, name="ssm_scan_sample")
    w_glu_b = w_glu[l].astype(BF16)
    ssm_out_p = _glu(yg_p, w_glu_b, tm=tm_p, name="ssm_glu_prompt")
    ssm_out_s = _glu(yg_s, w_glu_b, tm=n_s, name="ssm_glu_sample")

    mix_args = (row2(g_attn), row2(g_ssm), w_out[l].astype(BF16))
    ln1 = (row2(ln1_g), row2(ln1_b))
    x1_p = _mix(attn_p, ssm_out_p, *mix_args, x_p, *ln1, tm=tm_ln, name="mix_out_ln1_prompt")
    x1_s = _mix(attn_s, ssm_out_s, *mix_args, x_s, *ln1, tm=n_s, name="mix_out_ln1_sample")

    mem_rows = mem_prompt.reshape(nb * N_MEM, d)
    mem_k = _matmul(mem_rows, w_mk[l], tm=nb * N_MEM, tn=1024, name="mem_k")
    mem_v = _matmul(mem_rows, w_mv[l], tm=nb * N_MEM, tn=1024, name="mem_v")
    q_p = _matmul(x1_p, w_mq[l], tm=tm_p, tn=1024, name="mem_q_prompt", out_dtype=BF16)
    q_s = _matmul(x1_s, w_mq[l], tm=n_s, tn=1024, name="mem_q_sample")
    o_p = _memattn(q_p, mem_k.reshape(nb, N_MEM, d), mem_v.reshape(nb, N_MEM, d),
                   row0=0, n_seq=nb, seq=seq, tq=_row_tile(seq, 512), name="memattn_prompt")
    o_s = _memattn_heads(q_s, cache_mem_k[l], cache_mem_v[l], row0=0, n_seq=ns, seq=dseq,
                         name="memattn_sample")
    w_r, b_r = _router_weights(w_r1[l], b_r1[l], w_r2[l], b_r2[l])
    x2, x2_rows, sel, wts, cnt = _mm_ln(o_p, o_s, w_mo[l].astype(BF16), x1_p, x1_s, row2(ln2_g),
                                        row2(ln2_b), w_r, b_r, name="mem_out_ln2_route")

    y_p, y_s = _moe(x2, x2_rows, sel, wts, cnt, w_gate[l], w_up[l], w_down[l], row2(ln3_g),
                    row2(ln3_b), n_first=n_p, tc=_row_tile(n_s, 256))

    y_p = y_p.reshape(nb, seq, d)
    y_s = y_s.reshape(ns, dseq, d)
    wp = min(max(w for w, _ in DILATIONS), seq)
    k_p, v_p = _kv_window(proj_p, n_batch=nb, seq=seq, window=wp, tr=_row_tile(wp, 512))
    k_s = proj_s[:, D_ATT:2 * D_ATT].reshape(ns, dseq, ATT_HEADS, ATT_HD)
    v_s = proj_s[:, 2 * D_ATT:3 * D_ATT].reshape(ns, dseq, ATT_HEADS, ATT_HD)
    state = lambda v, b_: v.reshape(1, b_, N_SSM_GROUPS, SSM_STATE)
    return (y_p, y_s, k_p[None], v_p[None], k_s[None], v_s[None],
            state(ssm_re_p, nb), state(ssm_im_p, nb), state(ssm_re_s, ns), state(ssm_im_s, ns),
            mem_k.reshape(1, nb, N_MEM, MEM_HEADS, MEM_HD),
            mem_v.reshape(1, nb, N_MEM, MEM_HEADS, MEM_HD))
```

```python
import functools
import math

import numpy as np
import jax
import jax.numpy as jnp
from jax import lax
from jax.experimental import pallas as pl
from jax.experimental.pallas import tpu as pltpu

F32 = jnp.float32
BF16 = jnp.bfloat16
I32 = jnp.int32

D_MODEL = 2048
PAST_LEN = 8192
D_ATT = D_MODEL // 2
ATT_HEADS = 8
ATT_HD = D_ATT // ATT_HEADS
DILATIONS = ((128, 1), (512, 4), (2048, 16))
D_SSM = D_MODEL - D_ATT
SSM_GROUP_CH = 16
N_SSM_GROUPS = D_SSM // SSM_GROUP_CH
SSM_STATE = 64
N_MEM = 256
MEM_HEADS = 4
MEM_HD = D_MODEL // MEM_HEADS
N_EXPERT_GROUPS = 4
EXPERTS_PER_GROUP = 8
N_EXPERTS = N_EXPERT_GROUPS * EXPERTS_PER_GROUP
D_EXPERT = D_MODEL // 4
DEPTH = 1
DEEPNORM_ALPHA = (2.0 * DEPTH) ** 0.25
LN_EPS = 1e-5
RMS_EPS = 1e-6

LANES = 128
SUBLANES = 8
ROW_CHUNKS = D_MODEL // LANES
ROW_PITCH = ROW_CHUNKS + 1
Q_BLOCK = 128
ATTN_GROUP = 8
DEINTERLEAVE = 4
SSM_LANE_TILE = 128
SSM_GROUPS_PER_TILE = SSM_LANE_TILE // SSM_GROUP_CH
SSM_STATES_PER_TILE = SSM_GROUPS_PER_TILE * SSM_STATE
SSM_SEGMENTS = 16
MOE_TILE = 256
GATHER_SLOTS = 3
DISPATCH_SLOTS = 3
ROW_SPLIT = 2
ROUTER_LANES = 128
NEG_INF = float("-inf")


def _cparams(semantics, vmem_mib):
    return pltpu.CompilerParams(dimension_semantics=semantics,
                                vmem_limit_bytes=int(vmem_mib) << 20)


def _layernorm(y, g, b):
    mu = jnp.mean(y, axis=-1, keepdims=True)
    yc = y - mu
    var = jnp.mean(yc * yc, axis=-1, keepdims=True)
    return yc * lax.rsqrt(var + LN_EPS) * g + b


def _rmsnorm(v, g):
    return v * lax.rsqrt(jnp.mean(v * v, axis=-1, keepdims=True) + RMS_EPS) * g


def _dot(a, b):
    return jnp.dot(a, b, preferred_element_type=F32)


def _dot_nt(a, b):
    return lax.dot_general(a, b, (((1,), (1,)), ((), ())), preferred_element_type=F32)


def _mm_body(x_ref, w_ref, o_ref, wb_s):
    @pl.when(pl.program_id(1) == 0)
    def _():
        wb_s[...] = w_ref[...].astype(BF16)

    o_ref[...] = _dot(x_ref[...].astype(BF16), wb_s[...]).astype(o_ref.dtype)


def _matmul(x, w, *, tm, tn, name, out_dtype=F32):
    m, k = x.shape
    n = w.shape[1]
    vmem = (2 * (tm * k * 4 + k * tn * 4 + tm * tn * 4) + k * tn * 2 + tm * k * 2) / 2**20 + 8
    return pl.pallas_call(
        _mm_body,
        out_shape=jax.ShapeDtypeStruct((m, n), out_dtype),
        grid=(n // tn, m // tm),
        in_specs=[pl.BlockSpec((tm, k), lambda j, i: (i, 0)),
                  pl.BlockSpec((k, tn), lambda j, i: (0, j))],
        out_specs=pl.BlockSpec((tm, tn), lambda j, i: (i, j)),
        scratch_shapes=[pltpu.VMEM((k, tn), BF16)],
        compiler_params=_cparams(("parallel", "arbitrary"), vmem),
        name=name,
    )(x, w)


def _kv_window_body(k_ref, v_ref, ko_ref, vo_ref):
    for h in range(ATT_HEADS):
        sl = slice(h * ATT_HD, (h + 1) * ATT_HD)
        ko_ref[:, h, :] = k_ref[:, sl]
        vo_ref[:, h, :] = v_ref[:, sl]


def _kv_window(proj, *, n_batch, seq, window, tr):
    assert window % tr == 0 and seq % tr == 0
    per, first = seq // tr, (seq - window) // tr
    col = lambda c: pl.BlockSpec((tr, D_ATT), lambda b, t: (b * per + first + t, c))
    out_spec = pl.BlockSpec((None, tr, ATT_HEADS, ATT_HD), lambda b, t: (b, t, 0, 0))
    out_shape = jax.ShapeDtypeStruct((n_batch, window, ATT_HEADS, ATT_HD), F32)
    return pl.pallas_call(
        _kv_window_body,
        out_shape=[out_shape, out_shape],
        grid=(n_batch, window // tr),
        in_specs=[col(1), col(2)],
        out_specs=[out_spec, out_spec],
        compiler_params=_cparams(("parallel", "parallel"), 8 * tr * D_ATT * 4 / 2**20 + 8),
        name="kv_window",
    )(proj, proj)


def _attn_prompt_body(q_ref, k_ref, v_ref, o_ref, kt_s, va_s, on_s, lse_s, q4_s, k4_s, v4_s, *,
                      seq, dilations):
    scale = ATT_HD ** -0.5
    nblk = seq // Q_BLOCK
    qi = lax.broadcasted_iota(I32, (Q_BLOCK, Q_BLOCK), 0)
    kj = lax.broadcasted_iota(I32, (Q_BLOCK, Q_BLOCK), 1)
    cur_ok = kj <= qi
    prev_ok = kj >= qi
    va_s[:, :, ATT_HD:] = jnp.ones((nblk, Q_BLOCK, ATT_HD), BF16)

    quarter = seq // DEINTERLEAVE
    piece = 256

    def deinterleave(c, carry):
        for r in range(DEINTERLEAVE):
            src = pl.ds(r + c * piece * DEINTERLEAVE, piece, stride=DEINTERLEAVE)
            dst = pl.ds(pl.multiple_of(r * quarter + c * piece, piece), piece)
            q4_s[dst, :] = q_ref[src, :]
            k4_s[dst, :] = k_ref[src, :]
            v4_s[dst, :] = v_ref[src, :]
        return carry

    lax.fori_loop(0, quarter // piece, deinterleave, 0)

    for br, (_, d) in enumerate(dilations):
        span = d * Q_BLOCK
        nb = seq // span
        inner = d // DEINTERLEAVE if d % DEINTERLEAVE == 0 else 0
        qs, ks, vs = (q_ref, k_ref, v_ref) if inner == 0 else (q4_s, k4_s, v4_s)

        def stream_rows(t, d=d, span=span, nb=nb, inner=inner):
            r = t // nb
            ib = t % nb
            natural = pl.ds(r + ib * span, Q_BLOCK, stride=d)
            if inner == 0:
                return r, ib, natural, natural
            start = (r % DEINTERLEAVE) * quarter + r // DEINTERLEAVE + ib * Q_BLOCK * inner
            return r, ib, natural, pl.ds(start, Q_BLOCK, stride=inner)

        def prep(g, carry, stream_rows=stream_rows, ks=ks, vs=vs):
            loaded = []
            for j in range(ATTN_GROUP):
                t = g * ATTN_GROUP + j
                _, _, _, rows = stream_rows(t)
                loaded.append((t, ks[rows, :], vs[rows, :]))
            for t, kk, vv in loaded:
                kt_s[t] = jnp.transpose(kk).astype(BF16)
                va_s[t, :, 0:ATT_HD] = vv.astype(BF16)
            return carry

        lax.fori_loop(0, nblk // ATTN_GROUP, prep, 0)

        def group(g, carry, br=br, nb=nb, stream_rows=stream_rows, qs=qs):
            scores = []
            for j in range(ATTN_GROUP):
                t = g * ATTN_GROUP + j
                r, ib, rows, src_rows = stream_rows(t)
                tp = jnp.maximum(t - 1, r * nb)
                q = (qs[src_rows, :] * scale).astype(BF16)
                s = _dot(q, jnp.concatenate([kt_s[tp], kt_s[t]], axis=1))
                scores.append((t, tp, ib, src_rows, s))
            probs = []
            for t, tp, ib, rows, s in scores:
                ok = jnp.concatenate([jnp.logical_and(prev_ok, ib > 0), cur_ok], axis=1)
                s = jnp.where(ok, s, NEG_INF)
                m = jnp.max(s, axis=-1, keepdims=True)
                probs.append((t, tp, rows, m, jnp.exp(s - m).astype(BF16)))
            outs = [(rows, m, _dot(p, jnp.concatenate([va_s[tp], va_s[t]], axis=0)))
                    for t, tp, rows, m, p in probs]
            for rows, m, al in outs:
                l = al[:, ATT_HD:]
                on_s[br, rows, :] = al[:, :ATT_HD] / l
                lse_s[br, rows, :] = m + jnp.log(l)
            return carry

        lax.fori_loop(0, nblk // ATTN_GROUP, group, 0)

    nbr = len(dilations)
    copied = [d % DEINTERLEAVE == 0 for _, d in dilations]

    def merge(c, carry):
        for r in range(DEINTERLEAVE):
            natural = pl.ds(r + c * piece * DEINTERLEAVE, piece, stride=DEINTERLEAVE)
            packed = pl.ds(pl.multiple_of(r * quarter + c * piece, piece), piece)
            rows = [packed if copied[b] else natural for b in range(nbr)]
            ls = [lse_s[b, rows[b], :] for b in range(nbr)]
            mx = functools.reduce(jnp.maximum, ls)
            es = [jnp.exp(li - mx) for li in ls]
            num = sum(es[b] * on_s[b, rows[b], :] for b in range(nbr))
            o_ref[natural, :] = num / sum(es)
        return carry

    lax.fori_loop(0, quarter // piece, merge, 0)


def _attn_prompt(proj, *, n_batch, seq, dilations=DILATIONS):
    for w, d in dilations:
        assert w // d == Q_BLOCK and seq % (d * Q_BLOCK) == 0
    nbr = len(dilations)
    nblk = seq // Q_BLOCK
    assert nblk % ATTN_GROUP == 0
    blk = lambda off: pl.BlockSpec((seq, ATT_HD), lambda b, h, off=off: (b, off + h))
    assert seq % (DEINTERLEAVE * 256) == 0
    vmem = ((4 * 2 + 2 * nbr + 3) * seq * ATT_HD * 4 + 3 * seq * ATT_HD * 2) / 2**20 + 8
    return pl.pallas_call(
        functools.partial(_attn_prompt_body, seq=seq, dilations=dilations),
        out_shape=jax.ShapeDtypeStruct((n_batch * seq, D_ATT), F32),
        grid=(n_batch, ATT_HEADS),
        in_specs=[blk(0), blk(ATT_HEADS), blk(2 * ATT_HEADS)],
        out_specs=pl.BlockSpec((seq, ATT_HD), lambda b, h: (b, h)),
        scratch_shapes=[pltpu.VMEM((nblk, ATT_HD, Q_BLOCK), BF16),
                        pltpu.VMEM((nblk, Q_BLOCK, 2 * ATT_HD), BF16),
                        pltpu.VMEM((nbr, seq, ATT_HD), F32),
                        pltpu.VMEM((nbr, seq, ATT_HD), F32)]
        + [pltpu.VMEM((seq, ATT_HD), F32)] * 3,
        compiler_params=_cparams(("parallel", "parallel"), vmem),
        name="attn_prompt",
    )(proj, proj, proj)


def _sample_key_multiplicity(n_new, n_cache, past_len, dilations):
    d_max = max(d for _, d in dilations)
    tail = max(w for w, d in dilations if d != d_max)
    assert past_len % d_max == 0 and n_cache % d_max == 0 and n_new <= d_max // 2
    assert tail % d_max == 0 and tail <= n_cache
    half = d_max // 2
    n_grid = (n_cache - tail) // d_max
    kv_start = past_len - n_cache
    grid_rows = (np.arange(n_grid)[:, None] * d_max + np.arange(half)[None, :]).reshape(-1)
    tail_rows = n_cache - tail + np.arange(tail)
    new_rows = n_cache + np.arange(n_new)
    qpos = past_len + np.arange(n_new)

    def mult(rows):
        kpos = kv_start + rows
        delta = qpos[:, None] - kpos[None, :]
        c = np.zeros(delta.shape, np.float32)
        for w, d in dilations:
            c += ((delta >= 0) & (delta <= w) & (delta % d == 0) & (kpos[None, :] >= kv_start))
        return c

    fetched = np.zeros(n_cache + n_new, bool)
    fetched[grid_rows] = True
    fetched[tail_rows] = True
    fetched[new_rows] = True
    assert not mult(np.nonzero(~fetched)[0]).any()
    return mult(grid_rows), mult(tail_rows), mult(new_rows), n_grid, tail, half, d_max


def _attn_sample_body(q_ref, kn_ref, vn_ref, kg_ref, kt_ref, vg_ref, vt_ref,
                      cg_ref, ct_ref, cn_ref, o_ref):
    scale = ATT_HD ** -0.5
    heads = lambda ref: jnp.concatenate(
        [ref[:, h * ATT_HD:(h + 1) * ATT_HD] for h in range(ATT_HEADS)], axis=0)
    q = (heads(q_ref) * scale).astype(BF16)
    kn = heads(kn_ref).astype(BF16)
    vn = heads(vn_ref).astype(BF16)
    flat = lambda ref: ref[...].reshape(-1, ATT_HD).astype(BF16)
    cg, ct, cn = cg_ref[...], ct_ref[...], cn_ref[...]
    sg = jnp.where(cg > 0, _dot_nt(q, flat(kg_ref)), NEG_INF)
    st = jnp.where(ct > 0, _dot_nt(q, flat(kt_ref)), NEG_INF)
    sn = jnp.where(cn > 0, _dot_nt(q, kn), NEG_INF)
    m = jnp.maximum(jnp.maximum(jnp.max(sg, axis=-1, keepdims=True),
                                jnp.max(st, axis=-1, keepdims=True)),
                    jnp.max(sn, axis=-1, keepdims=True))
    pg = cg * jnp.exp(sg - m)
    pt = ct * jnp.exp(st - m)
    pn = cn * jnp.exp(sn - m)
    l = (jnp.sum(pg, axis=-1, keepdims=True) + jnp.sum(pt, axis=-1, keepdims=True)
         + jnp.sum(pn, axis=-1, keepdims=True))
    acc = (_dot(pg.astype(BF16), flat(vg_ref)) + _dot(pt.astype(BF16), flat(vt_ref))
           + _dot(pn.astype(BF16), vn))
    out = acc / l
    n_new = q_ref.shape[0]
    for h in range(ATT_HEADS):
        o_ref[:, h * ATT_HD:(h + 1) * ATT_HD] = out[h * n_new:(h + 1) * n_new, :]


def _attn_sample(proj, win_k, win_v, *, row0, n_seq, n_new, past_len=PAST_LEN,
                 dilations=DILATIONS):
    n_cache = win_k.shape[1]
    cg, ct, cn, n_grid, tail, half, d_max = _sample_key_multiplicity(
        n_new, n_cache, past_len, dilations)
    assert row0 % n_new == 0 and n_new % SUBLANES == 0 and n_cache % tail == 0
    eye = np.eye(ATT_HEADS, dtype=np.float32)
    key_major = lambda c: np.einsum("tk,hg->htkg", c, eye).reshape(ATT_HEADS * n_new, -1)
    head_major = lambda c: np.einsum("tk,hg->htgk", c, eye).reshape(ATT_HEADS * n_new, -1)
    cg, ct, cn = key_major(cg), key_major(ct), head_major(cn)
    rb = row0 // n_new
    n_groups = n_cache // d_max
    kgv = win_k.reshape(n_seq, n_groups, d_max, ATT_HEADS, ATT_HD)
    vgv = win_v.reshape(n_seq, n_groups, d_max, ATT_HEADS, ATT_HD)
    ktv = win_k.reshape(n_seq, n_cache // tail, tail, ATT_HEADS, ATT_HD)
    vtv = win_v.reshape(n_seq, n_cache // tail, tail, ATT_HEADS, ATT_HD)
    new = lambda off: pl.BlockSpec((n_new, D_ATT), lambda b, off=off: (rb + b, off))
    grid_spec = pl.BlockSpec((None, n_grid, half, ATT_HEADS, ATT_HD), lambda b: (b, 0, 0, 0, 0))
    tail_spec = pl.BlockSpec((None, None, tail, ATT_HEADS, ATT_HD),
                             lambda b: (b, n_cache // tail - 1, 0, 0, 0))
    const = lambda a: pl.BlockSpec(a.shape, lambda b: (0, 0))
    vmem = (2 * 2 * (n_grid * half + tail) * D_ATT * 4 + 4 * cg.size * 4 * 3) / 2**20 + 12
    return pl.pallas_call(
        _attn_sample_body,
        out_shape=jax.ShapeDtypeStruct((n_seq * n_new, D_ATT), F32),
        grid=(n_seq,),
        in_specs=[new(0), new(1), new(2), grid_spec, tail_spec, grid_spec, tail_spec,
                  const(cg), const(ct), const(cn)],
        out_specs=pl.BlockSpec((n_new, D_ATT), lambda b: (b, 0)),
        compiler_params=_cparams(("parallel",), vmem),
        name="attn_sample",
    )(proj, proj, proj, kgv, ktv, vgv, vtv, jnp.asarray(cg), jnp.asarray(ct), jnp.asarray(cn))


def _gelu_tanh(x):
    return 0.5 * x * (1.0 + jnp.tanh(math.sqrt(2.0 / math.pi) * (x + 0.044715 * (x * x * x))))


def _ssm_body(u_ref, bb_ref, cst_ref, a_ref, ap_ref, d_ref, hre_ref, him_ref, *rest,
              tl, npar, seq_len, nseg, emit_y, exact_in):
    if emit_y:
        y_ref, fre_ref, fim_ref, h_s = rest
    else:
        fre_ref, fim_ref, h_s = rest
    ns = SSM_STATES_PER_TILE
    c = pl.program_id(1)
    ngrp = npar // SUBLANES

    def step_rows(i, g):
        return pl.ds(c * tl + i + g * SUBLANES * seq_len, SUBLANES, stride=seq_len)

    @pl.when(c == 0)
    def _init():
        if nseg == 1:
            h_s[0] = hre_ref[...]
            h_s[1] = him_ref[...]
        else:
            pr, pi = ap_ref[0:1, :], ap_ref[1:2, :]
            for b in range(npar // nseg):
                sr = jnp.zeros((1, ns), F32)
                si = jnp.zeros((1, ns), F32)
                for j in range(nseg):
                    row = b * nseg + j
                    h_s[0, row:row + 1, :] = sr
                    h_s[1, row:row + 1, :] = si
                    er, ei = hre_ref[row:row + 1, :], him_ref[row:row + 1, :]
                    sr, si = pr * sr - pi * si + er, pr * si + pi * sr + ei

    ar = jnp.broadcast_to(a_ref[0:1, :], (SUBLANES, ns))
    ai = jnp.broadcast_to(a_ref[1:2, :], (SUBLANES, ns))
    us = [jnp.concatenate([u_ref[step_rows(i, g), :] for i in range(tl)], axis=0)
          for g in range(ngrp)]
    if exact_in:
        xs = [jnp.dot(u, bb_ref[...], precision=lax.Precision.HIGHEST, preferred_element_type=F32)
              for u in us]
    else:
        xs = [_dot(u.astype(BF16), bb_ref[...]) for u in us]

    hs = []
    for g in range(ngrp):
        gs = slice(g * SUBLANES, (g + 1) * SUBLANES)
        hr, hi = h_s[0, gs, :], h_s[1, gs, :]
        states = []
        for i in range(tl):
            xr = xs[g][i * SUBLANES:(i + 1) * SUBLANES, 0:ns]
            xi = xs[g][i * SUBLANES:(i + 1) * SUBLANES, ns:2 * ns]
            hr, hi = ar * hr - ai * hi + xr, ar * hi + ai * hr + xi
            if emit_y:
                states.append(jnp.concatenate([hr, hi], axis=1))
        h_s[0, gs, :] = hr
        h_s[1, gs, :] = hi
        if emit_y:
            hs.append(jnp.concatenate(states, axis=0))

    if emit_y:
        for g in range(ngrp):
            y = _gelu_tanh(_dot(hs[g].astype(BF16), cst_ref[...]) + d_ref[...] * us[g])
            for i in range(tl):
                y_ref[step_rows(i, g), :] = y[i * SUBLANES:(i + 1) * SUBLANES, :]

    @pl.when(c == pl.num_programs(1) - 1)
    def _fin():
        fre_ref[...] = h_s[0]
        fim_ref[...] = h_s[1]


def _ssm_scan(proj, prm, hin_re, hin_im, *, seq_len, tl, nseg, emit_y, exact_in, name):
    rows = proj.shape[0]
    npar = rows // seq_len
    assert npar % SUBLANES == 0 and seq_len % tl == 0
    ns = SSM_STATES_PER_TILE
    nk = D_SSM // SSM_LANE_TILE
    col0 = (proj.shape[1] - D_SSM) // SSM_LANE_TILE
    bb = prm["bb_f32"] if exact_in else prm["bb_bf16"]
    in_specs = [
        pl.BlockSpec((rows, SSM_LANE_TILE), lambda k, c: (0, col0 + k)),
        pl.BlockSpec((None, SSM_LANE_TILE, 2 * ns), lambda k, c: (k, 0, 0)),
        pl.BlockSpec((None, 2 * ns, SSM_LANE_TILE), lambda k, c: (k, 0, 0)),
        pl.BlockSpec((None, 2, ns), lambda k, c: (k, 0, 0)),
        pl.BlockSpec((None, 2, ns), lambda k, c: (k, 0, 0)),
        pl.BlockSpec((1, SSM_LANE_TILE), lambda k, c: (0, k)),
        pl.BlockSpec((npar, ns), lambda k, c: (0, k)),
        pl.BlockSpec((npar, ns), lambda k, c: (0, k)),
    ]
    state_shape = jax.ShapeDtypeStruct((npar, nk * ns), F32)
    state_spec = pl.BlockSpec((npar, ns), lambda k, c: (0, k))
    out_shape = [state_shape, state_shape]
    out_specs = [state_spec, state_spec]
    if emit_y:
        out_shape = [jax.ShapeDtypeStruct((rows, D_SSM), F32)] + out_shape
        out_specs = [pl.BlockSpec((rows, SSM_LANE_TILE), lambda k, c: (0, k))] + out_specs
    vmem = (4 * rows * SSM_LANE_TILE * 4 + tl * npar * 2 * ns * 4) / 2**20 + 16
    return pl.pallas_call(
        functools.partial(_ssm_body, tl=tl, npar=npar, seq_len=seq_len, nseg=nseg, emit_y=emit_y,
                          exact_in=exact_in),
        out_shape=out_shape,
        grid=(nk, seq_len // tl),
        in_specs=in_specs,
        out_specs=out_specs,
        scratch_shapes=[pltpu.VMEM((2, npar, ns), F32)],
        compiler_params=_cparams(("parallel", "arbitrary"), vmem),
        name=name,
    )(proj, bb, prm["cst"], prm["a"], prm["apow"], prm["d"], hin_re, hin_im)


def _ssm_params(lam_re, lam_im, log_dt, b_re, b_im, c_re, c_im, d_skip, seg_len):
    g, p, c = N_SSM_GROUPS, SSM_STATE, SSM_GROUP_CH
    nk, gt = g // SSM_GROUPS_PER_TILE, SSM_GROUPS_PER_TILE
    dt = jnp.exp(log_dt.astype(F32))[:, None]
    lr, li = lam_re.astype(F32), lam_im.astype(F32)
    mag = jnp.exp(lr * dt)
    a_re, a_im = mag * jnp.cos(li * dt), mag * jnp.sin(li * dt)
    magp = jnp.exp(lr * dt * seg_len)
    p_re, p_im = magp * jnp.cos(li * dt * seg_len), magp * jnp.sin(li * dt * seg_len)
    den = lr * lr + li * li
    nr, ni = a_re - 1.0, a_im
    f_re, f_im = (nr * lr + ni * li) / den, (ni * lr - nr * li) / den
    br, bi = b_re.astype(F32), b_im.astype(F32)
    bb_re = f_re[..., None] * br - f_im[..., None] * bi
    bb_im = f_re[..., None] * bi + f_im[..., None] * br
    eye = jnp.eye(gt, dtype=F32)

    def pack_b(m):
        return jnp.einsum("kgpc,gh->kgchp", m.reshape(nk, gt, p, c), eye).reshape(nk, gt * c, gt * p)

    def pack_c(m):
        return jnp.einsum("kgcp,gh->kgphc", m.reshape(nk, gt, c, p), eye).reshape(nk, gt * p, gt * c)

    bb = jnp.concatenate([pack_b(bb_re), pack_b(bb_im)], axis=2)
    cst = jnp.concatenate([pack_c(c_re.astype(F32)), -pack_c(c_im.astype(F32))], axis=1)
    tile = lambda v: v.reshape(nk, 1, gt * p)
    return {
        "bb_f32": bb, "bb_bf16": bb.astype(BF16), "cst": cst.astype(BF16),
        "a": jnp.concatenate([tile(a_re), tile(a_im)], axis=1),
        "apow": jnp.concatenate([tile(p_re), tile(p_im)], axis=1),
        "d": d_skip.astype(F32).reshape(1, g * c),
    }


def _glu_body(y_ref, w_ref, o_ref):
    yg = y_ref[...]
    z = _dot(yg.astype(BF16), w_ref[...])
    o_ref[...] = yg * (1.0 / (1.0 + jnp.exp(-z)))


def _glu(yg, w, *, tm, name):
    m, n = yg.shape
    return pl.pallas_call(
        _glu_body,
        out_shape=jax.ShapeDtypeStruct((m, n), F32),
        grid=(m // tm,),
        in_specs=[pl.BlockSpec((tm, n), lambda i: (i, 0)), pl.BlockSpec((n, n), lambda i: (0, 0))],
        out_specs=pl.BlockSpec((tm, n), lambda i: (i, 0)),
        compiler_params=_cparams(("parallel",), 4 * tm * n * 4 / 2**20 + 12),
        name=name,
    )(yg, w)


def _mix_body(attn_ref, ssm_ref, ga_ref, gs_ref, w_ref, x_ref, g_ref, b_ref, o_ref):
    half = attn_ref.shape[0] // ROW_SPLIT
    for r in range(ROW_SPLIT):
        rows = slice(r * half, (r + 1) * half)
        a = _rmsnorm(attn_ref[rows, :], ga_ref[...]).astype(BF16)
        s = _rmsnorm(ssm_ref[rows, :], gs_ref[...]).astype(BF16)
        mix = _dot(jnp.concatenate([a, s], axis=1), w_ref[...])
        o_ref[rows, :] = _layernorm(DEEPNORM_ALPHA * x_ref[rows, :] + mix, g_ref[...], b_ref[...])


def _mix(attn, ssm, ga, gs, w, x, g, b, *, tm, name):
    m = x.shape[0]
    row = lambda n: pl.BlockSpec((tm, n), lambda i: (i, 0))
    const = lambda a: pl.BlockSpec(a.shape, lambda i: (0, 0))
    return pl.pallas_call(
        _mix_body,
        out_shape=jax.ShapeDtypeStruct((m, D_MODEL), F32),
        grid=(m // tm,),
        in_specs=[row(D_ATT), row(D_SSM), const(ga), const(gs), const(w), row(D_MODEL),
                  const(g), const(b)],
        out_specs=row(D_MODEL),
        compiler_params=_cparams(("parallel",), 6 * tm * D_MODEL * 4 / 2**20 + 24),
        name=name,
    )(attn, ssm, ga, gs, w, x, g, b)


def _store_gatherable(o_ref, y):
    rows = y.shape[0]
    for c in range(ROW_CHUNKS):
        o_ref[pl.ds(c, rows, stride=ROW_PITCH), :] = y[:, c * LANES:(c + 1) * LANES]
    for c in range(ROW_CHUNKS, ROW_PITCH):
        o_ref[pl.ds(c, rows, stride=ROW_PITCH), :] = jnp.zeros((rows, LANES), F32)


def _load_gathered(buf, rows):
    return jnp.concatenate([buf[pl.ds(c, rows, stride=ROW_PITCH), :] for c in range(ROW_CHUNKS)],
                           axis=1)


def _start_row_gather(src_hbm, idx, buf, r, sem):
    pltpu.make_async_copy(src_hbm.at[pl.ds(idx * ROW_PITCH, ROW_CHUNKS), :],
                          buf.at[pl.ds(r * ROW_PITCH, ROW_CHUNKS), :], sem).start()


def _wait_row_gathers(buf, other, rows, sem):
    span = pl.ds(0, rows * ROW_CHUNKS)
    pltpu.make_async_copy(other.at[span, :], buf.at[span, :], sem).wait()


def _mm_ln_body(a1_ref, a2_ref, w_ref, x1_ref, x2_ref, g_ref, b_ref, wr_ref, br_ref,
                o_ref, rows_ref, sel_ref, wts_ref, cnt_ref, run_s, *, tiles1):
    first = pl.program_id(0) < tiles1
    half = a1_ref.shape[0] // ROW_SPLIT
    outs = []
    for r in range(ROW_SPLIT):
        rows = slice(r * half, (r + 1) * half)
        a = jnp.where(first, a1_ref[rows, :].astype(BF16), a2_ref[rows, :].astype(BF16))
        x = jnp.where(first, x1_ref[rows, :], x2_ref[rows, :])
        y = _dot(a, w_ref[...])
        out = _layernorm(DEEPNORM_ALPHA * x + y, g_ref[...], b_ref[...])
        o_ref[rows, :] = out
        outs.append(out)
    out = jnp.concatenate(outs, axis=0)
    _store_gatherable(rows_ref, out)
    _route_tile(out, wr_ref, br_ref, sel_ref, wts_ref, cnt_ref, run_s)


def _mm_ln(a1, a2, w, x1, x2, g, b, w_r, b_r, *, name):
    tm = a2.shape[0]
    assert a1.shape[0] % tm == 0
    tiles1 = a1.shape[0] // tm
    m = a1.shape[0] + tm
    row1 = lambda n: pl.BlockSpec((tm, n), lambda i: (jnp.minimum(i, tiles1 - 1), 0))
    row2 = lambda n: pl.BlockSpec((tm, n), lambda i: (0, 0))
    const = lambda v: pl.BlockSpec(v.shape, lambda i: (0, 0))
    lanes = pl.BlockSpec((tm, ROUTER_LANES), lambda i: (i, 0))
    return pl.pallas_call(
        functools.partial(_mm_ln_body, tiles1=tiles1),
        out_shape=[jax.ShapeDtypeStruct((m, D_MODEL), F32),
                   jax.ShapeDtypeStruct((m * ROW_PITCH, LANES), F32),
                   jax.ShapeDtypeStruct((m, ROUTER_LANES), I32),
                   jax.ShapeDtypeStruct((m, ROUTER_LANES), F32),
                   jax.ShapeDtypeStruct((1, ROUTER_LANES), I32)],
        grid=(tiles1 + 1,),
        in_specs=[row1(a1.shape[1]), row2(a2.shape[1]), const(w), row1(D_MODEL), row2(D_MODEL),
                  const(g), const(b), const(w_r), const(b_r)],
        out_specs=[pl.BlockSpec((tm, D_MODEL), lambda i: (i, 0)),
                   pl.BlockSpec((tm * ROW_PITCH, LANES), lambda i: (i, 0)),
                   lanes, lanes, pl.BlockSpec((1, ROUTER_LANES), lambda i: (0, 0))],
        scratch_shapes=[pltpu.VMEM((1, ROUTER_LANES), F32)],
        compiler_params=_cparams(("arbitrary",), 12 * tm * D_MODEL * 4 / 2**20 + 24),
        name=name,
    )(a1, a2, w, x1, x2, g, b, w_r, b_r)


def _memattn_body(q_ref, k_ref, v_ref, o_ref):
    scale = MEM_HD ** -0.5
    for h in range(MEM_HEADS):
        sl = slice(h * MEM_HD, (h + 1) * MEM_HD)
        s = _dot_nt(q_ref[:, sl].astype(BF16), k_ref[:, sl].astype(BF16)) * scale
        m = jnp.max(s, axis=-1, keepdims=True)
        p = jnp.exp(s - m)
        l = jnp.sum(p, axis=-1, keepdims=True)
        o_ref[:, sl] = (_dot(p.astype(BF16), v_ref[:, sl].astype(BF16)) / l).astype(o_ref.dtype)


def _memattn(q, mem_k, mem_v, *, row0, n_seq, seq, tq, name):
    assert seq % tq == 0 and row0 % tq == 0
    nq = seq // tq
    rb = row0 // tq
    mem_spec = pl.BlockSpec((None, N_MEM, D_MODEL), lambda b, i: (b, 0, 0))
    return pl.pallas_call(
        _memattn_body,
        out_shape=jax.ShapeDtypeStruct((n_seq * seq, D_MODEL), q.dtype),
        grid=(n_seq, nq),
        in_specs=[pl.BlockSpec((tq, D_MODEL), lambda b, i: (rb + b * nq + i, 0)),
                  mem_spec, mem_spec],
        out_specs=pl.BlockSpec((tq, D_MODEL), lambda b, i: (b * nq + i, 0)),
        compiler_params=_cparams(("parallel", "parallel"),
                                 4 * (tq + N_MEM) * D_MODEL * 4 / 2**20 + 8),
        name=name,
    )(q, mem_k, mem_v)


def _memattn_heads_body(q_ref, k_ref, v_ref, c_ref, o_ref):
    scale = MEM_HD ** -0.5
    tq = q_ref.shape[0]
    q = jnp.concatenate([q_ref[:, h * MEM_HD:(h + 1) * MEM_HD] for h in range(MEM_HEADS)], axis=0)
    k = k_ref[...].reshape(N_MEM * MEM_HEADS, MEM_HD).astype(BF16)
    v = v_ref[...].reshape(N_MEM * MEM_HEADS, MEM_HD).astype(BF16)
    s = jnp.where(c_ref[...] > 0, _dot_nt(q.astype(BF16), k) * scale, NEG_INF)
    m = jnp.max(s, axis=-1, keepdims=True)
    p = jnp.exp(s - m)
    l = jnp.sum(p, axis=-1, keepdims=True)
    o = _dot(p.astype(BF16), v) / l
    for h in range(MEM_HEADS):
        o_ref[:, h * MEM_HD:(h + 1) * MEM_HD] = o[h * tq:(h + 1) * tq, :]


def _memattn_heads(q, mem_k, mem_v, *, row0, n_seq, seq, name):
    assert row0 % seq == 0 and seq % SUBLANES == 0
    rb = row0 // seq
    same_head = np.kron(np.eye(MEM_HEADS, dtype=np.float32), np.ones((seq, 1), np.float32))
    same_head = np.tile(same_head, (1, N_MEM))
    mem_spec = pl.BlockSpec((None, N_MEM, MEM_HEADS, MEM_HD), lambda b: (b, 0, 0, 0))
    return pl.pallas_call(
        _memattn_heads_body,
        out_shape=jax.ShapeDtypeStruct((n_seq * seq, D_MODEL), F32),
        grid=(n_seq,),
        in_specs=[pl.BlockSpec((seq, D_MODEL), lambda b: (rb + b, 0)), mem_spec, mem_spec,
                  pl.BlockSpec(same_head.shape, lambda b: (0, 0))],
        out_specs=pl.BlockSpec((seq, D_MODEL), lambda b: (b, 0)),
        compiler_params=_cparams(("parallel",), 8 * N_MEM * D_MODEL * 4 / 2**20 + 8),
        name=name,
    )(q, mem_k, mem_v, jnp.asarray(same_head))


def _route_tile(x, w_ref, b_ref, sel_ref, wts_ref, cnt_ref, run_s):
    tm = x.shape[0]

    @pl.when(pl.program_id(0) == 0)
    def _():
        run_s[...] = jnp.zeros_like(run_s)

    ng, epg = N_EXPERT_GROUPS, EXPERTS_PER_GROUP
    x_hi = x.astype(BF16)
    x_lo = (x - x_hi.astype(F32)).astype(BF16)
    parts = _dot(x_hi, w_ref[...]) + _dot(x_lo, w_ref[...])
    logits = parts + pltpu.roll(parts, shift=ROUTER_LANES // 2, axis=1) + b_ref[...]
    lane = lax.broadcasted_iota(I32, (tm, ROUTER_LANES), 1)
    big = ROUTER_LANES

    def first_argmax(vals):
        mx = jnp.max(vals, axis=-1, keepdims=True)
        idx = jnp.min(jnp.where(vals == mx, lane, big), axis=-1, keepdims=True)
        return mx, idx

    gl = jnp.where(lane < ng, logits, NEG_INF)
    gmax, gsel = first_argmax(gl)
    g_w = 1.0 / jnp.sum(jnp.exp(gl - gmax), axis=-1, keepdims=True)
    lo = ng + gsel * epg
    el = jnp.where(jnp.logical_and(lane >= lo, lane < lo + epg), logits, NEG_INF)
    v1, i1 = first_argmax(el)
    v2, i2 = first_argmax(jnp.where(lane == i1, NEG_INF, el))
    e21 = jnp.exp(v2 - v1)
    w1 = g_w / (1.0 + e21)
    w2 = g_w * e21 / (1.0 + e21)

    onehot = jnp.logical_or(lane == i1, lane == i2)
    r = lax.broadcasted_iota(I32, (tm, tm), 0)
    cc = lax.broadcasted_iota(I32, (tm, tm), 1)
    tri = (cc < r).astype(BF16)
    before = _dot(tri, onehot.astype(BF16)) + run_s[...]
    rank1 = jnp.sum(jnp.where(lane == i1, before, 0.0), axis=-1, keepdims=True).astype(I32)
    rank2 = jnp.sum(jnp.where(lane == i2, before, 0.0), axis=-1, keepdims=True).astype(I32)
    run_s[...] = run_s[...] + jnp.sum(onehot.astype(F32), axis=0, keepdims=True)

    sel = jnp.where(lane == 0, i1 - ng, jnp.where(lane == 1, i2 - ng,
                    jnp.where(lane == 2, rank1, jnp.where(lane == 3, rank2, 0))))
    sel_ref[...] = sel
    wts_ref[...] = jnp.where(lane == 0, w1, jnp.where(lane == 1, w2, 0.0))
    cnt_ref[...] = run_s[...].astype(I32)


def _router_weights(w_r1, b_r1, w_r2, b_r2):
    ng, ne = N_EXPERT_GROUPS, N_EXPERTS
    half = ROUTER_LANES // 2
    assert ng + ne <= half
    w_r = jnp.concatenate([w_r1, w_r2.reshape(D_MODEL, ne),
                           jnp.zeros((D_MODEL, half - ng - ne), F32)], axis=1)
    w_hi = w_r.astype(BF16)
    w_lo = (w_r - w_hi.astype(F32)).astype(BF16)
    b_r = jnp.concatenate([b_r1, b_r2.reshape(ne), jnp.zeros((half - ng - ne,), F32)])
    return jnp.concatenate([w_hi, w_lo], axis=1), jnp.concatenate([b_r, b_r]).reshape(1, ROUTER_LANES)


def _dispatch_body(eid_ref, rank_ref, off_ref, pad0_ref, npad_ref, nact_ref, x_hbm, xs_hbm, xbuf,
                   zero_s, in_sem, out_sem, pad_sem, *, td, n_tiles):
    i = pl.program_id(0)
    n = pl.num_programs(0)
    tile_rows = MOE_TILE * ROW_PITCH
    in_rows = td * ROW_PITCH

    def row_copy(src, dst_row, s):
        return pltpu.make_async_copy(src, xs_hbm.at[pl.ds(dst_row * ROW_PITCH, ROW_PITCH), :], s)

    def tile_load(t):
        s = t % DISPATCH_SLOTS
        return pltpu.make_async_copy(x_hbm.at[pl.ds(t * in_rows, in_rows), :], xbuf.at[s],
                                     in_sem.at[s])

    def wait_scatter(par):
        for _ in range(2):
            pltpu.make_async_copy(xbuf.at[0], xs_hbm.at[pl.ds(0, in_rows), :],
                                  out_sem.at[par]).wait()

    @pl.when(i == 0)
    def _():
        for t in range(DISPATCH_SLOTS - 1):
            @pl.when(t < n)
            def _(t=t):
                tile_load(t).start()
        zero_s[...] = jnp.zeros_like(zero_s)

        def pad_copies(e):
            out = []
            for bit in reversed(range(MOE_TILE.bit_length() - 1)):
                rows = (1 << bit) * ROW_PITCH
                first = (pad0_ref[e] + (npad_ref[e] >> (bit + 1) << (bit + 1))) * ROW_PITCH
                out.append((jnp.bitwise_and(npad_ref[e] >> bit, 1) == 1, pltpu.make_async_copy(
                    zero_s.at[pl.ds(0, rows), :], xs_hbm.at[pl.ds(first, rows), :], pad_sem.at[0])))
            return out

        for e in range(N_EXPERTS):
            for on, cp in pad_copies(e):
                pl.when(on)(cp.start)
        for e in range(N_EXPERTS):
            for on, cp in pad_copies(e):
                pl.when(on)(cp.wait)

        def zero_tile(t, carry):
            parts = [pltpu.make_async_copy(
                zero_s.at[pl.ds(0, MOE_TILE), :],
                xs_hbm.at[pl.ds(t * tile_rows + j * MOE_TILE, MOE_TILE), :],
                pad_sem.at[0]) for j in range(ROW_PITCH)]
            for cp in parts:
                cp.start()
            for cp in parts:
                cp.wait()
            return carry

        lax.fori_loop(nact_ref[0], n_tiles, zero_tile, 0)

    slot = i % DISPATCH_SLOTS
    par = i % 2
    tile_load(i).wait()
    base = i * td * 2
    for r in range(td):
        for k in range(2):
            j = base + 2 * r + k
            row_copy(xbuf.at[slot, pl.ds(r * ROW_PITCH, ROW_PITCH), :],
                     off_ref[eid_ref[j]] + rank_ref[j], out_sem.at[par]).start()

    @pl.when(i > 0)
    def _():
        wait_scatter(1 - par)

    @pl.when(i + DISPATCH_SLOTS - 1 < n)
    def _():
        tile_load(i + DISPATCH_SLOTS - 1).start()

    @pl.when(i == n - 1)
    def _():
        wait_scatter(par)


def _moe_dispatch(x_rows, eid, rank, row_off, pad_start, pad_count, nact, *, td, n_tiles):
    n = x_rows.shape[0] // ROW_PITCH
    grid_spec = pltpu.PrefetchScalarGridSpec(
        num_scalar_prefetch=6,
        grid=(n // td,),
        in_specs=[pl.BlockSpec(memory_space=pl.ANY)],
        out_specs=pl.BlockSpec(memory_space=pl.ANY),
        scratch_shapes=[pltpu.VMEM((DISPATCH_SLOTS, td * ROW_PITCH, LANES), F32),
                        pltpu.VMEM((MOE_TILE // 2 * ROW_PITCH, LANES), F32),
                        pltpu.SemaphoreType.DMA((DISPATCH_SLOTS,)),
                        pltpu.SemaphoreType.DMA((2,)),
                        pltpu.SemaphoreType.DMA((1,))],
    )
    return pl.pallas_call(
        functools.partial(_dispatch_body, td=td, n_tiles=n_tiles),
        out_shape=jax.ShapeDtypeStruct((n_tiles * MOE_TILE * ROW_PITCH, LANES), F32),
        grid_spec=grid_spec,
        compiler_params=_cparams(("arbitrary",), 16),
        name="moe_dispatch",
    )(eid, rank, row_off, pad_start, pad_count, nact, x_rows)


def _moe_body(te_ref, ord_ref, nxt_ref, nact_ref, x_ref, wg_hbm, wu_hbm, wd_hbm, o_ref,
              wg_f, wu_f, wd_f, wsem, wg_s, wu_s, wd_s):
    i = pl.program_id(0)
    nact = nact_ref[0]
    tm = MOE_TILE

    def weight_copies(expert, ws):
        return [pltpu.make_async_copy(hbm.at[expert], stage.at[ws], wsem.at[ws])
                for hbm, stage in ((wg_hbm, wg_f), (wu_hbm, wu_f), (wd_hbm, wd_f))]

    def ffn(wg, wu, wd):
        x = _load_gathered(x_ref, tm).astype(BF16)
        hg = _dot(x, wg)
        hu = _dot(x, wu)
        h = hg * (1.0 / (1.0 + jnp.exp(-hg))) * hu
        _store_gatherable(o_ref, _dot(h.astype(BF16), wd))

    def tile_step():
        first = jnp.logical_or(i == 0, te_ref[i] != te_ref[jnp.maximum(i - 1, 0)])

        @pl.when(first)
        def _():
            ws = ord_ref[i] % 2
            for cp in weight_copies(te_ref[i], ws):
                cp.wait()

            @pl.when(nxt_ref[i] >= 0)
            def _():
                for cp in weight_copies(nxt_ref[i], 1 - ws):
                    cp.start(priority=1)

            wg = wg_f[ws].astype(BF16)
            wu = wu_f[ws].astype(BF16)
            wd = wd_f[ws].astype(BF16)
            wg_s[...] = wg
            wu_s[...] = wu
            wd_s[...] = wd
            ffn(wg, wu, wd)

        @pl.when(jnp.logical_not(first))
        def _():
            ffn(wg_s[...], wu_s[...], wd_s[...])

    @pl.when(i == 0)
    def _():
        for cp in weight_copies(te_ref[0], 0):
            cp.start(priority=1)

    @pl.when(i < nact)
    def _():
        tile_step()

    @pl.when(i >= nact)
    def _():
        o_ref[...] = jnp.zeros_like(o_ref)


def _moe_experts(x_sorted, w_gate, w_up, w_down, tile_expert, tile_ord, tile_next, nact, *,
                 n_tiles):
    tm = MOE_TILE
    in_map = lambda i, te, od, nx, n: (jnp.minimum(i, n[0] - 1), 0)
    any_spec = pl.BlockSpec(memory_space=pl.ANY)
    grid_spec = pltpu.PrefetchScalarGridSpec(
        num_scalar_prefetch=4,
        grid=(n_tiles,),
        in_specs=[pl.BlockSpec((tm * ROW_PITCH, LANES), in_map), any_spec, any_spec, any_spec],
        out_specs=pl.BlockSpec((tm * ROW_PITCH, LANES), lambda i, te, od, nx, n: (i, 0)),
        scratch_shapes=[pltpu.VMEM((2, D_MODEL, D_EXPERT), F32),
                        pltpu.VMEM((2, D_MODEL, D_EXPERT), F32),
                        pltpu.VMEM((2, D_EXPERT, D_MODEL), F32),
                        pltpu.SemaphoreType.DMA((2,)),
                        pltpu.VMEM((D_MODEL, D_EXPERT), BF16),
                        pltpu.VMEM((D_MODEL, D_EXPERT), BF16),
                        pltpu.VMEM((D_EXPERT, D_MODEL), BF16)],
    )
    return pl.pallas_call(
        _moe_body,
        out_shape=jax.ShapeDtypeStruct((n_tiles * tm * ROW_PITCH, LANES), F32),
        grid_spec=grid_spec,
        compiler_params=_cparams(("arbitrary",), 48),
        name="moe_experts",
    )(tile_expert, tile_ord, tile_next, nact, x_sorted, w_gate, w_up, w_down)


def _combine_body(eid_ref, rank_ref, off_ref, ys_hbm, wts_ref, x_ref, g_ref, b_ref, o1_ref, o2_ref,
                  buf, sem, *, tc, tiles1):
    i = pl.program_id(0)
    n = pl.num_programs(0)
    slot = i % GATHER_SLOTS
    ahead = GATHER_SLOTS - 1

    def issue_gather(tile, slot_):
        base = tile * tc * 2
        for r in range(tc):
            for k in range(2):
                j = base + 2 * r + k
                _start_row_gather(ys_hbm, off_ref[eid_ref[j]] + rank_ref[j], buf.at[slot_, k], r,
                                  sem.at[slot_])

    @pl.when(i == 0)
    def _():
        for t in range(ahead):
            @pl.when(t < n)
            def _(t=t):
                issue_gather(t, t)

    for k in range(2):
        _wait_row_gathers(buf.at[slot, k], buf.at[(i + 1) % GATHER_SLOTS, k], tc, sem.at[slot])

    @pl.when(i + ahead < n)
    def _():
        issue_gather(i + ahead, (i + ahead) % GATHER_SLOTS)

    w = wts_ref[...]
    moe = (w[:, 0:1] * _load_gathered(buf.at[slot, 0], tc)
           + w[:, 1:2] * _load_gathered(buf.at[slot, 1], tc))
    out = _layernorm(DEEPNORM_ALPHA * x_ref[...] + moe, g_ref[...], b_ref[...])

    @pl.when(i < tiles1)
    def _():
        o1_ref[...] = out

    @pl.when(i >= tiles1)
    def _():
        o2_ref[...] = out


def _moe_combine(ys, eid, rank, row_off, wts, x, g, b, *, tc, n_first):
    m = x.shape[0]
    assert n_first % tc == 0 and (m - n_first) % tc == 0
    tiles1 = n_first // tc
    grid_spec = pltpu.PrefetchScalarGridSpec(
        num_scalar_prefetch=3,
        grid=(m // tc,),
        in_specs=[pl.BlockSpec(memory_space=pl.ANY),
                  pl.BlockSpec((tc, ROUTER_LANES), lambda i, *_: (i, 0)),
                  pl.BlockSpec((tc, D_MODEL), lambda i, *_: (i, 0)),
                  pl.BlockSpec((1, D_MODEL), lambda i, *_: (0, 0)),
                  pl.BlockSpec((1, D_MODEL), lambda i, *_: (0, 0))],
        out_specs=[pl.BlockSpec((tc, D_MODEL), lambda i, *_: (jnp.minimum(i, tiles1 - 1), 0)),
                   pl.BlockSpec((tc, D_MODEL), lambda i, *_: (jnp.maximum(i - tiles1, 0), 0))],
        scratch_shapes=[pltpu.VMEM((GATHER_SLOTS, 2, tc * ROW_PITCH, LANES), F32),
                        pltpu.SemaphoreType.DMA((GATHER_SLOTS,))],
    )
    return pl.pallas_call(
        functools.partial(_combine_body, tc=tc, tiles1=tiles1),
        out_shape=[jax.ShapeDtypeStruct((n_first, D_MODEL), F32),
                   jax.ShapeDtypeStruct((m - n_first, D_MODEL), F32)],
        grid_spec=grid_spec,
        compiler_params=_cparams(("arbitrary",), 16 * tc * D_MODEL * 4 / 2**20 + 8),
        name="moe_combine_ln3",
    )(eid, rank, row_off, ys, wts, x, g, b)


def _moe(x, x_rows, sel, wts, cnt, w_gate, w_up, w_down, g, b, *, n_first, tc):
    n = x.shape[0]
    ng, ne = N_EXPERT_GROUPS, N_EXPERTS

    tm = MOE_TILE
    n_tiles = (2 * n) // tm + ne
    counts = cnt[0, ng:ng + ne]
    tiles_per = (counts + tm - 1) // tm
    tile_end = jnp.cumsum(tiles_per)
    row_off = (tile_end - tiles_per) * tm
    nact = tile_end[-1]
    a_eid, a_rank = sel[:, 0:2].reshape(-1), sel[:, 2:4].reshape(-1)
    row_off = row_off.astype(I32)
    tile_ids = jnp.minimum(jnp.arange(n_tiles, dtype=I32), nact - 1)
    tile_expert = jnp.sum((tile_end[None, :] <= tile_ids[:, None]).astype(I32), axis=1)
    used = tiles_per > 0
    eid = jnp.arange(ne, dtype=I32)
    ordinal = jnp.cumsum(used.astype(I32)) - 1
    later = jnp.where(jnp.logical_and(used[None, :], eid[None, :] > eid[:, None]), eid[None, :], ne)
    nxt = jnp.min(later, axis=1)
    nxt = jnp.where(nxt == ne, -1, nxt)

    nact = nact.reshape(1).astype(I32)
    x_sorted = _moe_dispatch(x_rows, a_eid, a_rank, row_off, (row_off + counts).astype(I32),
                             (tiles_per * tm - counts).astype(I32), nact, td=tc, n_tiles=n_tiles)
    ys = _moe_experts(x_sorted, w_gate, w_up, w_down, tile_expert, ordinal[tile_expert],
                      nxt[tile_expert], nact, n_tiles=n_tiles)
    return _moe_combine(ys, a_eid, a_rank, row_off, wts, x, g, b, tc=tc, n_first=n_first)


def _row_tile(m, cap):
    best = SUBLANES
    for t in range(SUBLANES, cap + 1, SUBLANES):
        if m % t == 0:
            best = t
    return best


def kernel(x_prompt, x_sample, cache_win_k, cache_win_v, state_ssm_re, state_ssm_im, cache_mem_k, cache_mem_v, mem_prompt, w_in, ssm_lam_re, ssm_lam_im, ssm_log_dt, ssm_b_re, ssm_b_im, ssm_c_re, ssm_c_im, ssm_d, w_glu, g_attn, g_ssm, w_out, ln1_g, ln1_b, w_mq, w_mk, w_mv, w_mo, ln2_g, ln2_b, w_r1, b_r1, w_r2, b_r2, w_gate, w_up, w_down, ln3_g, ln3_b):
    nb, seq, d = x_prompt.shape
    ns, dseq, _ = x_sample.shape
    n_p, n_s = nb * seq, ns * dseq
    n = n_p + n_s
    l = 0
    row2 = lambda v: v[l].reshape(1, -1)

    x_p, x_s = x_prompt.reshape(n_p, d), x_sample.reshape(n_s, d)
    tm_p = _row_tile(n_p, 1024)
    tm_ln = _row_tile(n_p, 512)
    assert n_p % n_s == 0 and n_s % SUBLANES == 0

    proj_p = _matmul(x_p, w_in[l], tm=tm_p, tn=1024, name="proj_in_prompt")
    proj_s = _matmul(x_s, w_in[l], tm=n_s, tn=1024, name="proj_in_sample")

    attn_p = _attn_prompt(proj_p, n_batch=nb, seq=seq)
    attn_s = _attn_sample(proj_s, cache_win_k[l], cache_win_v[l], row0=0, n_seq=ns, n_new=dseq)

    seg_len = seq // SSM_SEGMENTS
    prm = _ssm_params(ssm_lam_re[l], ssm_lam_im[l], ssm_log_dt[l], ssm_b_re[l], ssm_b_im[l],
                      ssm_c_re[l], ssm_c_im[l], ssm_d[l], seg_len)
    zeros = jnp.zeros((nb * SSM_SEGMENTS, N_SSM_GROUPS * SSM_STATE), F32)
    tl = _row_tile(seg_len, 32)
    end_re, end_im = _ssm_scan(proj_p, prm, zeros, zeros, seq_len=seg_len, tl=tl, nseg=1,
                               emit_y=False, exact_in=False, name="ssm_state_prompt")
    yg_p, fin_re, fin_im = _ssm_scan(proj_p, prm, end_re, end_im, seq_len=seg_len, tl=tl,
                                     nseg=SSM_SEGMENTS, emit_y=True, exact_in=False,
                                     name="ssm_scan_prompt")
    last = SSM_SEGMENTS - 1
    ssm_re_p = fin_re.reshape(nb, SSM_SEGMENTS, N_SSM_GROUPS, SSM_STATE)[:, last]
    ssm_im_p = fin_im.reshape(nb, SSM_SEGMENTS, N_SSM_GROUPS, SSM_STATE)[:, last]

    h0_re = state_ssm_re[l].reshape(ns, -1)
    h0_im = state_ssm_im[l].reshape(ns, -1)
    yg_s, ssm_re_s, ssm_im_s = _ssm_scan(proj_s, prm, h0_re, h0_im, seq_len=dseq, tl=dseq, nseg=1,
                                         emit_y=True, exact_in=True, name="ssm_scan_sample")
    w_glu_b = w_glu[l].astype(BF16)
    ssm_out_p = _glu(yg_p, w_glu_b, tm=tm_p, name="ssm_glu_prompt")
    ssm_out_s = _glu(yg_s, w_glu_b, tm=n_s, name="ssm_glu_sample")

    mix_args = (row2(g_attn), row2(g_ssm), w_out[l].astype(BF16))
    ln1 = (row2(ln1_g), row2(ln1_b))
    x1_p = _mix(attn_p, ssm_out_p, *mix_args, x_p, *ln1, tm=tm_ln, name="mix_out_ln1_prompt")
    x1_s = _mix(attn_s, ssm_out_s, *mix_args, x_s, *ln1, tm=n_s, name="mix_out_ln1_sample")

    mem_rows = mem_prompt.reshape(nb * N_MEM, d)
    mem_k = _matmul(mem_rows, w_mk[l], tm=nb * N_MEM, tn=1024, name="mem_k")
    mem_v = _matmul(mem_rows, w_mv[l], tm=nb * N_MEM, tn=1024, name="mem_v")
    q_p = _matmul(x1_p, w_mq[l], tm=tm_p, tn=1024, name="mem_q_prompt", out_dtype=BF16)
    q_s = _matmul(x1_s, w_mq[l], tm=n_s, tn=1024, name="mem_q_sample")
    o_p = _memattn(q_p, mem_k.reshape(nb, N_MEM, d), mem_v.reshape(nb, N_MEM, d),
                   row0=0, n_seq=nb, seq=seq, tq=_row_tile(seq, 512), name="memattn_prompt")
    o_s = _memattn_heads(q_s, cache_mem_k[l], cache_mem_v[l], row0=0, n_seq=ns, seq=dseq,
                         name="memattn_sample")
    w_r, b_r = _router_weights(w_r1[l], b_r1[l], w_r2[l], b_r2[l])
    x2, x2_rows, sel, wts, cnt = _mm_ln(o_p, o_s, w_mo[l].astype(BF16), x1_p, x1_s, row2(ln2_g),
                                        row2(ln2_b), w_r, b_r, name="mem_out_ln2_route")

    y_p, y_s = _moe(x2, x2_rows, sel, wts, cnt, w_gate[l], w_up[l], w_down[l], row2(ln3_g),
                    row2(ln3_b), n_first=n_p, tc=_row_tile(n_s, 256))

    y_p = y_p.reshape(nb, seq, d)
    y_s = y_s.reshape(ns, dseq, d)
    wp = min(max(w for w, _ in DILATIONS), seq)
    k_p, v_p = _kv_window(proj_p, n_batch=nb, seq=seq, window=wp, tr=_row_tile(wp, 512))
    k_s = proj_s[:, D_ATT:2 * D_ATT].reshape(ns, dseq, ATT_HEADS, ATT_HD)
    v_s = proj_s[:, 2 * D_ATT:3 * D_ATT].reshape(ns, dseq, ATT_HEADS, ATT_HD)
    state = lambda v, b_: v.reshape(1, b_, N_SSM_GROUPS, SSM_STATE)
    return (y_p, y_s, k_p[None], v_p[None], k_s[None], v_s[None],
            state(ssm_re_p, nb), state(ssm_im_p, nb), state(ssm_re_s, ns), state(ssm_im_s, ns),
            mem_k.reshape(1, nb, N_MEM, MEM_HEADS, MEM_HD),
            mem_v.reshape(1, nb, N_MEM, MEM_HEADS, MEM_HD))
```

```python
import functools
import math

import numpy as np
import jax
import jax.numpy as jnp
from jax import lax
from jax.experimental import pallas as pl
from jax.experimental.pallas import tpu as pltpu

F32 = jnp.float32
BF16 = jnp.bfloat16
I32 = jnp.int32

D_MODEL = 2048
PAST_LEN = 8192
D_ATT = D_MODEL // 2
ATT_HEADS = 8
ATT_HD = D_ATT // ATT_HEADS
DILATIONS = ((128, 1), (512, 4), (2048, 16))
D_SSM = D_MODEL - D_ATT
SSM_GROUP_CH = 16
N_SSM_GROUPS = D_SSM // SSM_GROUP_CH
SSM_STATE = 64
N_MEM = 256
MEM_HEADS = 4
MEM_HD = D_MODEL // MEM_HEADS
N_EXPERT_GROUPS = 4
EXPERTS_PER_GROUP = 8
N_EXPERTS = N_EXPERT_GROUPS * EXPERTS_PER_GROUP
D_EXPERT = D_MODEL // 4
DEPTH = 1
DEEPNORM_ALPHA = (2.0 * DEPTH) ** 0.25
LN_EPS = 1e-5
RMS_EPS = 1e-6

LANES = 128
SUBLANES = 8
ROW_CHUNKS = D_MODEL // LANES
ROW_PITCH = ROW_CHUNKS + 1
Q_BLOCK = 128
ATTN_GROUP = 8
DEINTERLEAVE = 4
SSM_LANE_TILE = 128
SSM_GROUPS_PER_TILE = SSM_LANE_TILE // SSM_GROUP_CH
SSM_STATES_PER_TILE = SSM_GROUPS_PER_TILE * SSM_STATE
SSM_SEGMENTS = 32
MOE_TILE = 256
GATHER_SLOTS = 3
DISPATCH_SLOTS = 3
ROW_SPLIT = 2
ROUTER_LANES = 128
NEG_INF = float("-inf")


def _cparams(semantics, vmem_mib):
    return pltpu.CompilerParams(dimension_semantics=semantics,
                                vmem_limit_bytes=int(vmem_mib) << 20)


def _layernorm(y, g, b):
    mu = jnp.mean(y, axis=-1, keepdims=True)
    yc = y - mu
    var = jnp.mean(yc * yc, axis=-1, keepdims=True)
    return yc * lax.rsqrt(var + LN_EPS) * g + b


def _rmsnorm(v, g):
    return v * lax.rsqrt(jnp.mean(v * v, axis=-1, keepdims=True) + RMS_EPS) * g


def _dot(a, b):
    return jnp.dot(a, b, preferred_element_type=F32)


def _dot_nt(a, b):
    return lax.dot_general(a, b, (((1,), (1,)), ((), ())), preferred_element_type=F32)


def _mm_body(x_ref, w_ref, o_ref, wb_s):
    @pl.when(pl.program_id(1) == 0)
    def _():
        wb_s[...] = w_ref[...].astype(BF16)

    o_ref[...] = _dot(x_ref[...].astype(BF16), wb_s[...]).astype(o_ref.dtype)


def _matmul(x, w, *, tm, tn, name, out_dtype=F32):
    m, k = x.shape
    n = w.shape[1]
    vmem = (2 * (tm * k * 4 + k * tn * 4 + tm * tn * 4) + k * tn * 2 + tm * k * 2) / 2**20 + 8
    return pl.pallas_call(
        _mm_body,
        out_shape=jax.ShapeDtypeStruct((m, n), out_dtype),
        grid=(n // tn, m // tm),
        in_specs=[pl.BlockSpec((tm, k), lambda j, i: (i, 0)),
                  pl.BlockSpec((k, tn), lambda j, i: (0, j))],
        out_specs=pl.BlockSpec((tm, tn), lambda j, i: (i, j)),
        scratch_shapes=[pltpu.VMEM((k, tn), BF16)],
        compiler_params=_cparams(("parallel", "arbitrary"), vmem),
        name=name,
    )(x, w)


def _kv_window_body(k_ref, v_ref, ko_ref, vo_ref):
    for h in range(ATT_HEADS):
        sl = slice(h * ATT_HD, (h + 1) * ATT_HD)
        ko_ref[:, h, :] = k_ref[:, sl]
        vo_ref[:, h, :] = v_ref[:, sl]


def _kv_window(proj, *, n_batch, seq, window, tr):
    assert window % tr == 0 and seq % tr == 0
    per, first = seq // tr, (seq - window) // tr
    col = lambda c: pl.BlockSpec((tr, D_ATT), lambda b, t: (b * per + first + t, c))
    out_spec = pl.BlockSpec((None, tr, ATT_HEADS, ATT_HD), lambda b, t: (b, t, 0, 0))
    out_shape = jax.ShapeDtypeStruct((n_batch, window, ATT_HEADS, ATT_HD), F32)
    return pl.pallas_call(
        _kv_window_body,
        out_shape=[out_shape, out_shape],
        grid=(n_batch, window // tr),
        in_specs=[col(1), col(2)],
        out_specs=[out_spec, out_spec],
        compiler_params=_cparams(("parallel", "parallel"), 8 * tr * D_ATT * 4 / 2**20 + 8),
        name="kv_window",
    )(proj, proj)


def _attn_prompt_body(q_ref, k_ref, v_ref, o_ref, kt_s, va_s, on_s, lse_s, q4_s, k4_s, v4_s, *,
                      seq, dilations):
    scale = ATT_HD ** -0.5
    nblk = seq // Q_BLOCK
    qi = lax.broadcasted_iota(I32, (Q_BLOCK, Q_BLOCK), 0)
    kj = lax.broadcasted_iota(I32, (Q_BLOCK, Q_BLOCK), 1)
    cur_ok = kj <= qi
    prev_ok = kj >= qi
    va_s[:, :, ATT_HD:] = jnp.ones((nblk, Q_BLOCK, ATT_HD), BF16)

    quarter = seq // DEINTERLEAVE
    piece = 256

    def deinterleave(c, carry):
        for r in range(DEINTERLEAVE):
            src = pl.ds(r + c * piece * DEINTERLEAVE, piece, stride=DEINTERLEAVE)
            dst = pl.ds(pl.multiple_of(r * quarter + c * piece, piece), piece)
            q4_s[dst, :] = q_ref[src, :]
            k4_s[dst, :] = k_ref[src, :]
            v4_s[dst, :] = v_ref[src, :]
        return carry

    lax.fori_loop(0, quarter // piece, deinterleave, 0)

    for br, (_, d) in enumerate(dilations):
        span = d * Q_BLOCK
        nb = seq // span
        inner = d // DEINTERLEAVE if d % DEINTERLEAVE == 0 else 0
        qs, ks, vs = (q_ref, k_ref, v_ref) if inner == 0 else (q4_s, k4_s, v4_s)

        def stream_rows(t, d=d, span=span, nb=nb, inner=inner):
            r = t // nb
            ib = t % nb
            natural = pl.ds(r + ib * span, Q_BLOCK, stride=d)
            if inner == 0:
                return r, ib, natural, natural
            start = (r % DEINTERLEAVE) * quarter + r // DEINTERLEAVE + ib * Q_BLOCK * inner
            return r, ib, natural, pl.ds(start, Q_BLOCK, stride=inner)

        def prep(g, carry, stream_rows=stream_rows, ks=ks, vs=vs):
            loaded = []
            for j in range(ATTN_GROUP):
                t = g * ATTN_GROUP + j
                _, _, _, rows = stream_rows(t)
                loaded.append((t, ks[rows, :], vs[rows, :]))
            for t, kk, vv in loaded:
                kt_s[t] = jnp.transpose(kk).astype(BF16)
                va_s[t, :, 0:ATT_HD] = vv.astype(BF16)
            return carry

        lax.fori_loop(0, nblk // ATTN_GROUP, prep, 0)

        def group(g, carry, br=br, nb=nb, stream_rows=stream_rows, qs=qs):
            scores = []
            for j in range(ATTN_GROUP):
                t = g * ATTN_GROUP + j
                r, ib, rows, src_rows = stream_rows(t)
                tp = jnp.maximum(t - 1, r * nb)
                q = (qs[src_rows, :] * scale).astype(BF16)
                s = _dot(q, jnp.concatenate([kt_s[tp], kt_s[t]], axis=1))
                scores.append((t, tp, ib, src_rows, s))
            probs = []
            for t, tp, ib, rows, s in scores:
                ok = jnp.concatenate([jnp.logical_and(prev_ok, ib > 0), cur_ok], axis=1)
                s = jnp.where(ok, s, NEG_INF)
                m = jnp.max(s, axis=-1, keepdims=True)
                probs.append((t, tp, rows, m, jnp.exp(s - m).astype(BF16)))
            outs = [(rows, m, _dot(p, jnp.concatenate([va_s[tp], va_s[t]], axis=0)))
                    for t, tp, rows, m, p in probs]
            for rows, m, al in outs:
                l = al[:, ATT_HD:]
                on_s[br, rows, :] = al[:, :ATT_HD] / l
                lse_s[br, rows, :] = m + jnp.log(l)
            return carry

        lax.fori_loop(0, nblk // ATTN_GROUP, group, 0)

    nbr = len(dilations)
    copied = [d % DEINTERLEAVE == 0 for _, d in dilations]

    def merge(c, carry):
        for r in range(DEINTERLEAVE):
            natural = pl.ds(r + c * piece * DEINTERLEAVE, piece, stride=DEINTERLEAVE)
            packed = pl.ds(pl.multiple_of(r * quarter + c * piece, piece), piece)
            rows = [packed if copied[b] else natural for b in range(nbr)]
            ls = [lse_s[b, rows[b], :] for b in range(nbr)]
            mx = functools.reduce(jnp.maximum, ls)
            es = [jnp.exp(li - mx) for li in ls]
            num = sum(es[b] * on_s[b, rows[b], :] for b in range(nbr))
            o_ref[natural, :] = num / sum(es)
        return carry

    lax.fori_loop(0, quarter // piece, merge, 0)


def _attn_prompt(proj, *, n_batch, seq, dilations=DILATIONS):
    for w, d in dilations:
        assert w // d == Q_BLOCK and seq % (d * Q_BLOCK) == 0
    nbr = len(dilations)
    nblk = seq // Q_BLOCK
    assert nblk % ATTN_GROUP == 0
    blk = lambda off: pl.BlockSpec((seq, ATT_HD), lambda b, h, off=off: (b, off + h))
    assert seq % (DEINTERLEAVE * 256) == 0
    vmem = ((4 * 2 + 2 * nbr + 3) * seq * ATT_HD * 4 + 3 * seq * ATT_HD * 2) / 2**20 + 8
    return pl.pallas_call(
        functools.partial(_attn_prompt_body, seq=seq, dilations=dilations),
        out_shape=jax.ShapeDtypeStruct((n_batch * seq, D_ATT), F32),
        grid=(n_batch, ATT_HEADS),
        in_specs=[blk(0), blk(ATT_HEADS), blk(2 * ATT_HEADS)],
        out_specs=pl.BlockSpec((seq, ATT_HD), lambda b, h: (b, h)),
        scratch_shapes=[pltpu.VMEM((nblk, ATT_HD, Q_BLOCK), BF16),
                        pltpu.VMEM((nblk, Q_BLOCK, 2 * ATT_HD), BF16),
                        pltpu.VMEM((nbr, seq, ATT_HD), F32),
                        pltpu.VMEM((nbr, seq, ATT_HD), F32)]
        + [pltpu.VMEM((seq, ATT_HD), F32)] * 3,
        compiler_params=_cparams(("parallel", "parallel"), vmem),
        name="attn_prompt",
    )(proj, proj, proj)


def _sample_key_multiplicity(n_new, n_cache, past_len, dilations):
    d_max = max(d for _, d in dilations)
    tail = max(w for w, d in dilations if d != d_max)
    assert past_len % d_max == 0 and n_cache % d_max == 0 and n_new <= d_max // 2
    assert tail % d_max == 0 and tail <= n_cache
    half = d_max // 2
    n_grid = (n_cache - tail) // d_max
    kv_start = past_len - n_cache
    grid_rows = (np.arange(n_grid)[:, None] * d_max + np.arange(half)[None, :]).reshape(-1)
    tail_rows = n_cache - tail + np.arange(tail)
    new_rows = n_cache + np.arange(n_new)
    qpos = past_len + np.arange(n_new)

    def mult(rows):
        kpos = kv_start + rows
        delta = qpos[:, None] - kpos[None, :]
        c = np.zeros(delta.shape, np.float32)
        for w, d in dilations:
            c += ((delta >= 0) & (delta <= w) & (delta % d == 0) & (kpos[None, :] >= kv_start))
        return c

    fetched = np.zeros(n_cache + n_new, bool)
    fetched[grid_rows] = True
    fetched[tail_rows] = True
    fetched[new_rows] = True
    assert not mult(np.nonzero(~fetched)[0]).any()
    return mult(grid_rows), mult(tail_rows), mult(new_rows), n_grid, tail, half, d_max


def _attn_sample_body(q_ref, kn_ref, vn_ref, kg_ref, kt_ref, vg_ref, vt_ref,
                      cg_ref, ct_ref, cn_ref, o_ref):
    scale = ATT_HD ** -0.5
    heads = lambda ref: jnp.concatenate(
        [ref[:, h * ATT_HD:(h + 1) * ATT_HD] for h in range(ATT_HEADS)], axis=0)
    q = (heads(q_ref) * scale).astype(BF16)
    kn = heads(kn_ref).astype(BF16)
    vn = heads(vn_ref).astype(BF16)
    flat = lambda ref: ref[...].reshape(-1, ATT_HD).astype(BF16)
    cg, ct, cn = cg_ref[...], ct_ref[...], cn_ref[...]
    sg = jnp.where(cg > 0, _dot_nt(q, flat(kg_ref)), NEG_INF)
    st = jnp.where(ct > 0, _dot_nt(q, flat(kt_ref)), NEG_INF)
    sn = jnp.where(cn > 0, _dot_nt(q, kn), NEG_INF)
    m = jnp.maximum(jnp.maximum(jnp.max(sg, axis=-1, keepdims=True),
                                jnp.max(st, axis=-1, keepdims=True)),
                    jnp.max(sn, axis=-1, keepdims=True))
    pg = cg * jnp.exp(sg - m)
    pt = ct * jnp.exp(st - m)
    pn = cn * jnp.exp(sn - m)
    l = (jnp.sum(pg, axis=-1, keepdims=True) + jnp.sum(pt, axis=-1, keepdims=True)
         + jnp.sum(pn, axis=-1, keepdims=True))
    acc = (_dot(pg.astype(BF16), flat(vg_ref)) + _dot(pt.astype(BF16), flat(vt_ref))
           + _dot(pn.astype(BF16), vn))
    out = acc / l
    n_new = q_ref.shape[0]
    for h in range(ATT_HEADS):
        o_ref[:, h * ATT_HD:(h + 1) * ATT_HD] = out[h * n_new:(h + 1) * n_new, :]


def _attn_sample(proj, win_k, win_v, *, row0, n_seq, n_new, past_len=PAST_LEN,
                 dilations=DILATIONS):
    n_cache = win_k.shape[1]
    cg, ct, cn, n_grid, tail, half, d_max = _sample_key_multiplicity(
        n_new, n_cache, past_len, dilations)
    assert row0 % n_new == 0 and n_new % SUBLANES == 0 and n_cache % tail == 0
    eye = np.eye(ATT_HEADS, dtype=np.float32)
    key_major = lambda c: np.einsum("tk,hg->htkg", c, eye).reshape(ATT_HEADS * n_new, -1)
    head_major = lambda c: np.einsum("tk,hg->htgk", c, eye).reshape(ATT_HEADS * n_new, -1)
    cg, ct, cn = key_major(cg), key_major(ct), head_major(cn)
    rb = row0 // n_new
    n_groups = n_cache // d_max
    kgv = win_k.reshape(n_seq, n_groups, d_max, ATT_HEADS, ATT_HD)
    vgv = win_v.reshape(n_seq, n_groups, d_max, ATT_HEADS, ATT_HD)
    ktv = win_k.reshape(n_seq, n_cache // tail, tail, ATT_HEADS, ATT_HD)
    vtv = win_v.reshape(n_seq, n_cache // tail, tail, ATT_HEADS, ATT_HD)
    new = lambda off: pl.BlockSpec((n_new, D_ATT), lambda b, off=off: (rb + b, off))
    grid_spec = pl.BlockSpec((None, n_grid, half, ATT_HEADS, ATT_HD), lambda b: (b, 0, 0, 0, 0))
    tail_spec = pl.BlockSpec((None, None, tail, ATT_HEADS, ATT_HD),
                             lambda b: (b, n_cache // tail - 1, 0, 0, 0))
    const = lambda a: pl.BlockSpec(a.shape, lambda b: (0, 0))
    vmem = (2 * 2 * (n_grid * half + tail) * D_ATT * 4 + 4 * cg.size * 4 * 3) / 2**20 + 12
    return pl.pallas_call(
        _attn_sample_body,
        out_shape=jax.ShapeDtypeStruct((n_seq * n_new, D_ATT), F32),
        grid=(n_seq,),
        in_specs=[new(0), new(1), new(2), grid_spec, tail_spec, grid_spec, tail_spec,
                  const(cg), const(ct), const(cn)],
        out_specs=pl.BlockSpec((n_new, D_ATT), lambda b: (b, 0)),
        compiler_params=_cparams(("parallel",), vmem),
        name="attn_sample",
    )(proj, proj, proj, kgv, ktv, vgv, vtv, jnp.asarray(cg), jnp.asarray(ct), jnp.asarray(cn))


def _gelu_tanh(x):
    return 0.5 * x * (1.0 + jnp.tanh(math.sqrt(2.0 / math.pi) * (x + 0.044715 * (x * x * x))))


def _ssm_body(u_ref, bb_ref, cst_ref, a_ref, ap_ref, d_ref, hre_ref, him_ref, *rest,
              tl, npar, seq_len, nseg, emit_y, exact_in):
    if emit_y:
        y_ref, fre_ref, fim_ref, h_s = rest
    else:
        fre_ref, fim_ref, h_s = rest
    ns = SSM_STATES_PER_TILE
    c = pl.program_id(1)
    ngrp = npar // SUBLANES

    def step_rows(i, g):
        return pl.ds(c * tl + i + g * SUBLANES * seq_len, SUBLANES, stride=seq_len)

    @pl.when(c == 0)
    def _init():
        if nseg == 1:
            h_s[0] = hre_ref[...]
            h_s[1] = him_ref[...]
        else:
            pr, pi = ap_ref[0:1, :], ap_ref[1:2, :]
            for b in range(npar // nseg):
                sr = jnp.zeros((1, ns), F32)
                si = jnp.zeros((1, ns), F32)
                for j in range(nseg):
                    row = b * nseg + j
                    h_s[0, row:row + 1, :] = sr
                    h_s[1, row:row + 1, :] = si
                    er, ei = hre_ref[row:row + 1, :], him_ref[row:row + 1, :]
                    sr, si = pr * sr - pi * si + er, pr * si + pi * sr + ei

    ar = jnp.broadcast_to(a_ref[0:1, :], (SUBLANES, ns))
    ai = jnp.broadcast_to(a_ref[1:2, :], (SUBLANES, ns))
    us = [jnp.concatenate([u_ref[step_rows(i, g), :] for i in range(tl)], axis=0)
          for g in range(ngrp)]
    if exact_in:
        xs = [jnp.dot(u, bb_ref[...], precision=lax.Precision.HIGHEST, preferred_element_type=F32)
              for u in us]
    else:
        xs = [_dot(u.astype(BF16), bb_ref[...]) for u in us]

    hs = []
    for g in range(ngrp):
        gs = slice(g * SUBLANES, (g + 1) * SUBLANES)
        hr, hi = h_s[0, gs, :], h_s[1, gs, :]
        states = []
        for i in range(tl):
            xr = xs[g][i * SUBLANES:(i + 1) * SUBLANES, 0:ns]
            xi = xs[g][i * SUBLANES:(i + 1) * SUBLANES, ns:2 * ns]
            hr, hi = ar * hr - ai * hi + xr, ar * hi + ai * hr + xi
            if emit_y:
                states.append(jnp.concatenate([hr, hi], axis=1))
        h_s[0, gs, :] = hr
        h_s[1, gs, :] = hi
        if emit_y:
            hs.append(jnp.concatenate(states, axis=0))

    if emit_y:
        for g in range(ngrp):
            y = _gelu_tanh(_dot(hs[g].astype(BF16), cst_ref[...]) + d_ref[...] * us[g])
            for i in range(tl):
                y_ref[step_rows(i, g), :] = y[i * SUBLANES:(i + 1) * SUBLANES, :]

    @pl.when(c == pl.num_programs(1) - 1)
    def _fin():
        fre_ref[...] = h_s[0]
        fim_ref[...] = h_s[1]


def _ssm_scan(proj, prm, hin_re, hin_im, *, seq_len, tl, nseg, emit_y, exact_in, name):
    rows = proj.shape[0]
    npar = rows // seq_len
    assert npar % SUBLANES == 0 and seq_len % tl == 0
    ns = SSM_STATES_PER_TILE
    nk = D_SSM // SSM_LANE_TILE
    col0 = (proj.shape[1] - D_SSM) // SSM_LANE_TILE
    bb = prm["bb_f32"] if exact_in else prm["bb_bf16"]
    in_specs = [
        pl.BlockSpec((rows, SSM_LANE_TILE), lambda k, c: (0, col0 + k)),
        pl.BlockSpec((None, SSM_LANE_TILE, 2 * ns), lambda k, c: (k, 0, 0)),
        pl.BlockSpec((None, 2 * ns, SSM_LANE_TILE), lambda k, c: (k, 0, 0)),
        pl.BlockSpec((None, 2, ns), lambda k, c: (k, 0, 0)),
        pl.BlockSpec((None, 2, ns), lambda k, c: (k, 0, 0)),
        pl.BlockSpec((1, SSM_LANE_TILE), lambda k, c: (0, k)),
        pl.BlockSpec((npar, ns), lambda k, c: (0, k)),
        pl.BlockSpec((npar, ns), lambda k, c: (0, k)),
    ]
    state_shape = jax.ShapeDtypeStruct((npar, nk * ns), F32)
    state_spec = pl.BlockSpec((npar, ns), lambda k, c: (0, k))
    out_shape = [state_shape, state_shape]
    out_specs = [state_spec, state_spec]
    if emit_y:
        out_shape = [jax.ShapeDtypeStruct((rows, D_SSM), F32)] + out_shape
        out_specs = [pl.BlockSpec((rows, SSM_LANE_TILE), lambda k, c: (0, k))] + out_specs
    vmem = (4 * rows * SSM_LANE_TILE * 4 + tl * npar * 2 * ns * 4) / 2**20 + 16
    return pl.pallas_call(
        functools.partial(_ssm_body, tl=tl, npar=npar, seq_len=seq_len, nseg=nseg, emit_y=emit_y,
                          exact_in=exact_in),
        out_shape=out_shape,
        grid=(nk, seq_len // tl),
        in_specs=in_specs,
        out_specs=out_specs,
        scratch_shapes=[pltpu.VMEM((2, npar, ns), F32)],
        compiler_params=_cparams(("parallel", "arbitrary"), vmem),
        name=name,
    )(proj, bb, prm["cst"], prm["a"], prm["apow"], prm["d"], hin_re, hin_im)


def _ssm_params(lam_re, lam_im, log_dt, b_re, b_im, c_re, c_im, d_skip, seg_len):
    g, p, c = N_SSM_GROUPS, SSM_STATE, SSM_GROUP_CH
    nk, gt = g // SSM_GROUPS_PER_TILE, SSM_GROUPS_PER_TILE
    dt = jnp.exp(log_dt.astype(F32))[:, None]
    lr, li = lam_re.astype(F32), lam_im.astype(F32)
    mag = jnp.exp(lr * dt)
    a_re, a_im = mag * jnp.cos(li * dt), mag * jnp.sin(li * dt)
    magp = jnp.exp(lr * dt * seg_len)
    p_re, p_im = magp * jnp.cos(li * dt * seg_len), magp * jnp.sin(li * dt * seg_len)
    den = lr * lr + li * li
    nr, ni = a_re - 1.0, a_im
    f_re, f_im = (nr * lr + ni * li) / den, (ni * lr - nr * li) / den
    br, bi = b_re.astype(F32), b_im.astype(F32)
    bb_re = f_re[..., None] * br - f_im[..., None] * bi
    bb_im = f_re[..., None] * bi + f_im[..., None] * br
    eye = jnp.eye(gt, dtype=F32)

    def pack_b(m):
        return jnp.einsum("kgpc,gh->kgchp", m.reshape(nk, gt, p, c), eye).reshape(nk, gt * c, gt * p)

    def pack_c(m):
        return jnp.einsum("kgcp,gh->kgphc", m.reshape(nk, gt, c, p), eye).reshape(nk, gt * p, gt * c)

    bb = jnp.concatenate([pack_b(bb_re), pack_b(bb_im)], axis=2)
    cst = jnp.concatenate([pack_c(c_re.astype(F32)), -pack_c(c_im.astype(F32))], axis=1)
    tile = lambda v: v.reshape(nk, 1, gt * p)
    return {
        "bb_f32": bb, "bb_bf16": bb.astype(BF16), "cst": cst.astype(BF16),
        "a": jnp.concatenate([tile(a_re), tile(a_im)], axis=1),
        "apow": jnp.concatenate([tile(p_re), tile(p_im)], axis=1),
        "d": d_skip.astype(F32).reshape(1, g * c),
    }


def _glu_body(y_ref, w_ref, o_ref):
    yg = y_ref[...]
    z = _dot(yg.astype(BF16), w_ref[...])
    o_ref[...] = yg * (1.0 / (1.0 + jnp.exp(-z)))


def _glu(yg, w, *, tm, name):
    m, n = yg.shape
    return pl.pallas_call(
        _glu_body,
        out_shape=jax.ShapeDtypeStruct((m, n), F32),
        grid=(m // tm,),
        in_specs=[pl.BlockSpec((tm, n), lambda i: (i, 0)), pl.BlockSpec((n, n), lambda i: (0, 0))],
        out_specs=pl.BlockSpec((tm, n), lambda i: (i, 0)),
        compiler_params=_cparams(("parallel",), 4 * tm * n * 4 / 2**20 + 12),
        name=name,
    )(yg, w)


def _mix_body(attn_ref, ssm_ref, ga_ref, gs_ref, w_ref, x_ref, g_ref, b_ref, o_ref):
    half = attn_ref.shape[0] // ROW_SPLIT
    for r in range(ROW_SPLIT):
        rows = slice(r * half, (r + 1) * half)
        a = _rmsnorm(attn_ref[rows, :], ga_ref[...]).astype(BF16)
        s = _rmsnorm(ssm_ref[rows, :], gs_ref[...]).astype(BF16)
        mix = _dot(jnp.concatenate([a, s], axis=1), w_ref[...])
        o_ref[rows, :] = _layernorm(DEEPNORM_ALPHA * x_ref[rows, :] + mix, g_ref[...], b_ref[...])


def _mix(attn, ssm, ga, gs, w, x, g, b, *, tm, name):
    m = x.shape[0]
    row = lambda n: pl.BlockSpec((tm, n), lambda i: (i, 0))
    const = lambda a: pl.BlockSpec(a.shape, lambda i: (0, 0))
    return pl.pallas_call(
        _mix_body,
        out_shape=jax.ShapeDtypeStruct((m, D_MODEL), F32),
        grid=(m // tm,),
        in_specs=[row(D_ATT), row(D_SSM), const(ga), const(gs), const(w), row(D_MODEL),
                  const(g), const(b)],
        out_specs=row(D_MODEL),
        compiler_params=_cparams(("parallel",), 6 * tm * D_MODEL * 4 / 2**20 + 24),
        name=name,
    )(attn, ssm, ga, gs, w, x, g, b)


def _store_gatherable(o_ref, y):
    rows = y.shape[0]
    for c in range(ROW_CHUNKS):
        o_ref[pl.ds(c, rows, stride=ROW_PITCH), :] = y[:, c * LANES:(c + 1) * LANES]
    for c in range(ROW_CHUNKS, ROW_PITCH):
        o_ref[pl.ds(c, rows, stride=ROW_PITCH), :] = jnp.zeros((rows, LANES), F32)


def _load_gathered(buf, rows):
    return jnp.concatenate([buf[pl.ds(c, rows, stride=ROW_PITCH), :] for c in range(ROW_CHUNKS)],
                           axis=1)


def _start_row_gather(src_hbm, idx, buf, r, sem):
    pltpu.make_async_copy(src_hbm.at[pl.ds(idx * ROW_PITCH, ROW_CHUNKS), :],
                          buf.at[pl.ds(r * ROW_PITCH, ROW_CHUNKS), :], sem).start()


def _wait_row_gathers(buf, other, rows, sem):
    span = pl.ds(0, rows * ROW_CHUNKS)
    pltpu.make_async_copy(other.at[span, :], buf.at[span, :], sem).wait()


def _mm_ln_body(a1_ref, a2_ref, w_ref, x1_ref, x2_ref, g_ref, b_ref, wr_ref, br_ref,
                o_ref, rows_ref, sel_ref, wts_ref, cnt_ref, run_s, *, tiles1):
    first = pl.program_id(0) < tiles1
    half = a1_ref.shape[0] // ROW_SPLIT
    outs = []
    for r in range(ROW_SPLIT):
        rows = slice(r * half, (r + 1) * half)
        a = jnp.where(first, a1_ref[rows, :].astype(BF16), a2_ref[rows, :].astype(BF16))
        x = jnp.where(first, x1_ref[rows, :], x2_ref[rows, :])
        y = _dot(a, w_ref[...])
        out = _layernorm(DEEPNORM_ALPHA * x + y, g_ref[...], b_ref[...])
        o_ref[rows, :] = out
        outs.append(out)
    out = jnp.concatenate(outs, axis=0)
    _store_gatherable(rows_ref, out)
    _route_tile(out, wr_ref, br_ref, sel_ref, wts_ref, cnt_ref, run_s)


def _mm_ln(a1, a2, w, x1, x2, g, b, w_r, b_r, *, name):
    tm = a2.shape[0]
    assert a1.shape[0] % tm == 0
    tiles1 = a1.shape[0] // tm
    m = a1.shape[0] + tm
    row1 = lambda n: pl.BlockSpec((tm, n), lambda i: (jnp.minimum(i, tiles1 - 1), 0))
    row2 = lambda n: pl.BlockSpec((tm, n), lambda i: (0, 0))
    const = lambda v: pl.BlockSpec(v.shape, lambda i: (0, 0))
    lanes = pl.BlockSpec((tm, ROUTER_LANES), lambda i: (i, 0))
    return pl.pallas_call(
        functools.partial(_mm_ln_body, tiles1=tiles1),
        out_shape=[jax.ShapeDtypeStruct((m, D_MODEL), F32),
                   jax.ShapeDtypeStruct((m * ROW_PITCH, LANES), F32),
                   jax.ShapeDtypeStruct((m, ROUTER_LANES), I32),
                   jax.ShapeDtypeStruct((m, ROUTER_LANES), F32),
                   jax.ShapeDtypeStruct((1, ROUTER_LANES), I32)],
        grid=(tiles1 + 1,),
        in_specs=[row1(a1.shape[1]), row2(a2.shape[1]), const(w), row1(D_MODEL), row2(D_MODEL),
                  const(g), const(b), const(w_r), const(b_r)],
        out_specs=[pl.BlockSpec((tm, D_MODEL), lambda i: (i, 0)),
                   pl.BlockSpec((tm * ROW_PITCH, LANES), lambda i: (i, 0)),
                   lanes, lanes, pl.BlockSpec((1, ROUTER_LANES), lambda i: (0, 0))],
        scratch_shapes=[pltpu.VMEM((1, ROUTER_LANES), F32)],
        compiler_params=_cparams(("arbitrary",), 12 * tm * D_MODEL * 4 / 2**20 + 24),
        name=name,
    )(a1, a2, w, x1, x2, g, b, w_r, b_r)


def _memattn_body(q_ref, k_ref, v_ref, o_ref):
    scale = MEM_HD ** -0.5
    for h in range(MEM_HEADS):
        sl = slice(h * MEM_HD, (h + 1) * MEM_HD)
        s = _dot_nt(q_ref[:, sl].astype(BF16), k_ref[:, sl].astype(BF16)) * scale
        m = jnp.max(s, axis=-1, keepdims=True)
        p = jnp.exp(s - m)
        l = jnp.sum(p, axis=-1, keepdims=True)
        o_ref[:, sl] = (_dot(p.astype(BF16), v_ref[:, sl].astype(BF16)) / l).astype(o_ref.dtype)


def _memattn(q, mem_k, mem_v, *, row0, n_seq, seq, tq, name):
    assert seq % tq == 0 and row0 % tq == 0
    nq = seq // tq
    rb = row0 // tq
    mem_spec = pl.BlockSpec((None, N_MEM, D_MODEL), lambda b, i: (b, 0, 0))
    return pl.pallas_call(
        _memattn_body,
        out_shape=jax.ShapeDtypeStruct((n_seq * seq, D_MODEL), q.dtype),
        grid=(n_seq, nq),
        in_specs=[pl.BlockSpec((tq, D_MODEL), lambda b, i: (rb + b * nq + i, 0)),
                  mem_spec, mem_spec],
        out_specs=pl.BlockSpec((tq, D_MODEL), lambda b, i: (b * nq + i, 0)),
        compiler_params=_cparams(("parallel", "parallel"),
                                 4 * (tq + N_MEM) * D_MODEL * 4 / 2**20 + 8),
        name=name,
    )(q, mem_k, mem_v)


def _memattn_heads_body(q_ref, k_ref, v_ref, c_ref, o_ref):
    scale = MEM_HD ** -0.5
    tq = q_ref.shape[0]
    q = jnp.concatenate([q_ref[:, h * MEM_HD:(h + 1) * MEM_HD] for h in range(MEM_HEADS)], axis=0)
    k = k_ref[...].reshape(N_MEM * MEM_HEADS, MEM_HD).astype(BF16)
    v = v_ref[...].reshape(N_MEM * MEM_HEADS, MEM_HD).astype(BF16)
    s = jnp.where(c_ref[...] > 0, _dot_nt(q.astype(BF16), k) * scale, NEG_INF)
    m = jnp.max(s, axis=-1, keepdims=True)
    p = jnp.exp(s - m)
    l = jnp.sum(p, axis=-1, keepdims=True)
    o = _dot(p.astype(BF16), v) / l
    for h in range(MEM_HEADS):
        o_ref[:, h * MEM_HD:(h + 1) * MEM_HD] = o[h * tq:(h + 1) * tq, :]


def _memattn_heads(q, mem_k, mem_v, *, row0, n_seq, seq, name):
    assert row0 % seq == 0 and seq % SUBLANES == 0
    rb = row0 // seq
    same_head = np.kron(np.eye(MEM_HEADS, dtype=np.float32), np.ones((seq, 1), np.float32))
    same_head = np.tile(same_head, (1, N_MEM))
    mem_spec = pl.BlockSpec((None, N_MEM, MEM_HEADS, MEM_HD), lambda b: (b, 0, 0, 0))
    return pl.pallas_call(
        _memattn_heads_body,
        out_shape=jax.ShapeDtypeStruct((n_seq * seq, D_MODEL), F32),
        grid=(n_seq,),
        in_specs=[pl.BlockSpec((seq, D_MODEL), lambda b: (rb + b, 0)), mem_spec, mem_spec,
                  pl.BlockSpec(same_head.shape, lambda b: (0, 0))],
        out_specs=pl.BlockSpec((seq, D_MODEL), lambda b: (b, 0)),
        compiler_params=_cparams(("parallel",), 8 * N_MEM * D_MODEL * 4 / 2**20 + 8),
        name=name,
    )(q, mem_k, mem_v, jnp.asarray(same_head))


def _route_tile(x, w_ref, b_ref, sel_ref, wts_ref, cnt_ref, run_s):
    tm = x.shape[0]

    @pl.when(pl.program_id(0) == 0)
    def _():
        run_s[...] = jnp.zeros_like(run_s)

    ng, epg = N_EXPERT_GROUPS, EXPERTS_PER_GROUP
    x_hi = x.astype(BF16)
    x_lo = (x - x_hi.astype(F32)).astype(BF16)
    parts = _dot(x_hi, w_ref[...]) + _dot(x_lo, w_ref[...])
    logits = parts + pltpu.roll(parts, shift=ROUTER_LANES // 2, axis=1) + b_ref[...]
    lane = lax.broadcasted_iota(I32, (tm, ROUTER_LANES), 1)
    big = ROUTER_LANES

    def first_argmax(vals):
        mx = jnp.max(vals, axis=-1, keepdims=True)
        idx = jnp.min(jnp.where(vals == mx, lane, big), axis=-1, keepdims=True)
        return mx, idx

    gl = jnp.where(lane < ng, logits, NEG_INF)
    gmax, gsel = first_argmax(gl)
    g_w = 1.0 / jnp.sum(jnp.exp(gl - gmax), axis=-1, keepdims=True)
    lo = ng + gsel * epg
    el = jnp.where(jnp.logical_and(lane >= lo, lane < lo + epg), logits, NEG_INF)
    v1, i1 = first_argmax(el)
    v2, i2 = first_argmax(jnp.where(lane == i1, NEG_INF, el))
    e21 = jnp.exp(v2 - v1)
    w1 = g_w / (1.0 + e21)
    w2 = g_w * e21 / (1.0 + e21)

    onehot = jnp.logical_or(lane == i1, lane == i2)
    r = lax.broadcasted_iota(I32, (tm, tm), 0)
    cc = lax.broadcasted_iota(I32, (tm, tm), 1)
    tri = (cc < r).astype(BF16)
    before = _dot(tri, onehot.astype(BF16)) + run_s[...]
    rank1 = jnp.sum(jnp.where(lane == i1, before, 0.0), axis=-1, keepdims=True).astype(I32)
    rank2 = jnp.sum(jnp.where(lane == i2, before, 0.0), axis=-1, keepdims=True).astype(I32)
    run_s[...] = run_s[...] + jnp.sum(onehot.astype(F32), axis=0, keepdims=True)

    sel = jnp.where(lane == 0, i1 - ng, jnp.where(lane == 1, i2 - ng,
                    jnp.where(lane == 2, rank1, jnp.where(lane == 3, rank2, 0))))
    sel_ref[...] = sel
    wts_ref[...] = jnp.where(lane == 0, w1, jnp.where(lane == 1, w2, 0.0))
    cnt_ref[...] = run_s[...].astype(I32)


def _router_weights(w_r1, b_r1, w_r2, b_r2):
    ng, ne = N_EXPERT_GROUPS, N_EXPERTS
    half = ROUTER_LANES // 2
    assert ng + ne <= half
    w_r = jnp.concatenate([w_r1, w_r2.reshape(D_MODEL, ne),
                           jnp.zeros((D_MODEL, half - ng - ne), F32)], axis=1)
    w_hi = w_r.astype(BF16)
    w_lo = (w_r - w_hi.astype(F32)).astype(BF16)
    b_r = jnp.concatenate([b_r1, b_r2.reshape(ne), jnp.zeros((half - ng - ne,), F32)])
    return jnp.concatenate([w_hi, w_lo], axis=1), jnp.concatenate([b_r, b_r]).reshape(1, ROUTER_LANES)


def _dispatch_body(eid_ref, rank_ref, off_ref, pad0_ref, npad_ref, nact_ref, x_hbm, xs_hbm, xbuf,
                   zero_s, in_sem, out_sem, pad_sem, *, td, n_tiles):
    i = pl.program_id(0)
    n = pl.num_programs(0)
    tile_rows = MOE_TILE * ROW_PITCH
    in_rows = td * ROW_PITCH

    def row_copy(src, dst_row, s):
        return pltpu.make_async_copy(src, xs_hbm.at[pl.ds(dst_row * ROW_PITCH, ROW_PITCH), :], s)

    def tile_load(t):
        s = t % DISPATCH_SLOTS
        return pltpu.make_async_copy(x_hbm.at[pl.ds(t * in_rows, in_rows), :], xbuf.at[s],
                                     in_sem.at[s])

    def wait_scatter(par):
        for _ in range(2):
            pltpu.make_async_copy(xbuf.at[0], xs_hbm.at[pl.ds(0, in_rows), :],
                                  out_sem.at[par]).wait()

    @pl.when(i == 0)
    def _():
        for t in range(DISPATCH_SLOTS - 1):
            @pl.when(t < n)
            def _(t=t):
                tile_load(t).start()
        zero_s[...] = jnp.zeros_like(zero_s)

        def pad_copies(e):
            out = []
            for bit in reversed(range(MOE_TILE.bit_length() - 1)):
                rows = (1 << bit) * ROW_PITCH
                first = (pad0_ref[e] + (npad_ref[e] >> (bit + 1) << (bit + 1))) * ROW_PITCH
                out.append((jnp.bitwise_and(npad_ref[e] >> bit, 1) == 1, pltpu.make_async_copy(
                    zero_s.at[pl.ds(0, rows), :], xs_hbm.at[pl.ds(first, rows), :], pad_sem.at[0])))
            return out

        for e in range(N_EXPERTS):
            for on, cp in pad_copies(e):
                pl.when(on)(cp.start)
        for e in range(N_EXPERTS):
            for on, cp in pad_copies(e):
                pl.when(on)(cp.wait)

        def zero_tile(t, carry):
            parts = [pltpu.make_async_copy(
                zero_s.at[pl.ds(0, MOE_TILE), :],
                xs_hbm.at[pl.ds(t * tile_rows + j * MOE_TILE, MOE_TILE), :],
                pad_sem.at[0]) for j in range(ROW_PITCH)]
            for cp in parts:
                cp.start()
            for cp in parts:
                cp.wait()
            return carry

        lax.fori_loop(nact_ref[0], n_tiles, zero_tile, 0)

    slot = i % DISPATCH_SLOTS
    par = i % 2
    tile_load(i).wait()
    base = i * td * 2
    for r in range(td):
        for k in range(2):
            j = base + 2 * r + k
            row_copy(xbuf.at[slot, pl.ds(r * ROW_PITCH, ROW_PITCH), :],
                     off_ref[eid_ref[j]] + rank_ref[j], out_sem.at[par]).start()

    @pl.when(i > 0)
    def _():
        wait_scatter(1 - par)

    @pl.when(i + DISPATCH_SLOTS - 1 < n)
    def _():
        tile_load(i + DISPATCH_SLOTS - 1).start()

    @pl.when(i == n - 1)
    def _():
        wait_scatter(par)


def _moe_dispatch(x_rows, eid, rank, row_off, pad_start, pad_count, nact, *, td, n_tiles):
    n = x_rows.shape[0] // ROW_PITCH
    grid_spec = pltpu.PrefetchScalarGridSpec(
        num_scalar_prefetch=6,
        grid=(n // td,),
        in_specs=[pl.BlockSpec(memory_space=pl.ANY)],
        out_specs=pl.BlockSpec(memory_space=pl.ANY),
        scratch_shapes=[pltpu.VMEM((DISPATCH_SLOTS, td * ROW_PITCH, LANES), F32),
                        pltpu.VMEM((MOE_TILE // 2 * ROW_PITCH, LANES), F32),
                        pltpu.SemaphoreType.DMA((DISPATCH_SLOTS,)),
                        pltpu.SemaphoreType.DMA((2,)),
                        pltpu.SemaphoreType.DMA((1,))],
    )
    return pl.pallas_call(
        functools.partial(_dispatch_body, td=td, n_tiles=n_tiles),
        out_shape=jax.ShapeDtypeStruct((n_tiles * MOE_TILE * ROW_PITCH, LANES), F32),
        grid_spec=grid_spec,
        compiler_params=_cparams(("arbitrary",), 16),
        name="moe_dispatch",
    )(eid, rank, row_off, pad_start, pad_count, nact, x_rows)


def _moe_body(te_ref, ord_ref, nxt_ref, nact_ref, x_ref, wg_hbm, wu_hbm, wd_hbm, o_ref,
              wg_f, wu_f, wd_f, wsem, wg_s, wu_s, wd_s):
    i = pl.program_id(0)
    nact = nact_ref[0]
    tm = MOE_TILE

    def weight_copies(expert, ws):
        return [pltpu.make_async_copy(hbm.at[expert], stage.at[ws], wsem.at[ws])
                for hbm, stage in ((wg_hbm, wg_f), (wu_hbm, wu_f), (wd_hbm, wd_f))]

    def ffn(wg, wu, wd):
        x = _load_gathered(x_ref, tm).astype(BF16)
        hg = _dot(x, wg)
        hu = _dot(x, wu)
        h = hg * (1.0 / (1.0 + jnp.exp(-hg))) * hu
        _store_gatherable(o_ref, _dot(h.astype(BF16), wd))

    def tile_step():
        first = jnp.logical_or(i == 0, te_ref[i] != te_ref[jnp.maximum(i - 1, 0)])

        @pl.when(first)
        def _():
            ws = ord_ref[i] % 2
            for cp in weight_copies(te_ref[i], ws):
                cp.wait()

            @pl.when(nxt_ref[i] >= 0)
            def _():
                for cp in weight_copies(nxt_ref[i], 1 - ws):
                    cp.start(priority=1)

            wg = wg_f[ws].astype(BF16)
            wu = wu_f[ws].astype(BF16)
            wd = wd_f[ws].astype(BF16)
            wg_s[...] = wg
            wu_s[...] = wu
            wd_s[...] = wd
            ffn(wg, wu, wd)

        @pl.when(jnp.logical_not(first))
        def _():
            ffn(wg_s[...], wu_s[...], wd_s[...])

    @pl.when(i == 0)
    def _():
        for cp in weight_copies(te_ref[0], 0):
            cp.start(priority=1)

    @pl.when(i < nact)
    def _():
        tile_step()

    @pl.when(i >= nact)
    def _():
        o_ref[...] = jnp.zeros_like(o_ref)


def _moe_experts(x_sorted, w_gate, w_up, w_down, tile_expert, tile_ord, tile_next, nact, *,
                 n_tiles):
    tm = MOE_TILE
    in_map = lambda i, te, od, nx, n: (jnp.minimum(i, n[0] - 1), 0)
    any_spec = pl.BlockSpec(memory_space=pl.ANY)
    grid_spec = pltpu.PrefetchScalarGridSpec(
        num_scalar_prefetch=4,
        grid=(n_tiles,),
        in_specs=[pl.BlockSpec((tm * ROW_PITCH, LANES), in_map), any_spec, any_spec, any_spec],
        out_specs=pl.BlockSpec((tm * ROW_PITCH, LANES), lambda i, te, od, nx, n: (i, 0)),
        scratch_shapes=[pltpu.VMEM((2, D_MODEL, D_EXPERT), F32),
                        pltpu.VMEM((2, D_MODEL, D_EXPERT), F32),
                        pltpu.VMEM((2, D_EXPERT, D_MODEL), F32),
                        pltpu.SemaphoreType.DMA((2,)),
                        pltpu.VMEM((D_MODEL, D_EXPERT), BF16),
                        pltpu.VMEM((D_MODEL, D_EXPERT), BF16),
                        pltpu.VMEM((D_EXPERT, D_MODEL), BF16)],
    )
    return pl.pallas_call(
        _moe_body,
        out_shape=jax.ShapeDtypeStruct((n_tiles * tm * ROW_PITCH, LANES), F32),
        grid_spec=grid_spec,
        compiler_params=_cparams(("arbitrary",), 48),
        name="moe_experts",
    )(tile_expert, tile_ord, tile_next, nact, x_sorted, w_gate, w_up, w_down)


def _combine_body(eid_ref, rank_ref, off_ref, ys_hbm, wts_ref, x_ref, g_ref, b_ref, o1_ref, o2_ref,
                  buf, sem, *, tc, tiles1):
    i = pl.program_id(0)
    n = pl.num_programs(0)
    slot = i % GATHER_SLOTS
    ahead = GATHER_SLOTS - 1

    def issue_gather(tile, slot_):
        base = tile * tc * 2
        for r in range(tc):
            for k in range(2):
                j = base + 2 * r + k
                _start_row_gather(ys_hbm, off_ref[eid_ref[j]] + rank_ref[j], buf.at[slot_, k], r,
                                  sem.at[slot_])

    @pl.when(i == 0)
    def _():
        for t in range(ahead):
            @pl.when(t < n)
            def _(t=t):
                issue_gather(t, t)

    for k in range(2):
        _wait_row_gathers(buf.at[slot, k], buf.at[(i + 1) % GATHER_SLOTS, k], tc, sem.at[slot])

    @pl.when(i + ahead < n)
    def _():
        issue_gather(i + ahead, (i + ahead) % GATHER_SLOTS)

    w = wts_ref[...]
    moe = (w[:, 0:1] * _load_gathered(buf.at[slot, 0], tc)
           + w[:, 1:2] * _load_gathered(buf.at[slot, 1], tc))
    out = _layernorm(DEEPNORM_ALPHA * x_ref[...] + moe, g_ref[...], b_ref[...])

    @pl.when(i < tiles1)
    def _():
        o1_ref[...] = out

    @pl.when(i >= tiles1)
    def _():
        o2_ref[...] = out


def _moe_combine(ys, eid, rank, row_off, wts, x, g, b, *, tc, n_first):
    m = x.shape[0]
    assert n_first % tc == 0 and (m - n_first) % tc == 0
    tiles1 = n_first // tc
    grid_spec = pltpu.PrefetchScalarGridSpec(
        num_scalar_prefetch=3,
        grid=(m // tc,),
        in_specs=[pl.BlockSpec(memory_space=pl.ANY),
                  pl.BlockSpec((tc, ROUTER_LANES), lambda i, *_: (i, 0)),
                  pl.BlockSpec((tc, D_MODEL), lambda i, *_: (i, 0)),
                  pl.BlockSpec((1, D_MODEL), lambda i, *_: (0, 0)),
                  pl.BlockSpec((1, D_MODEL), lambda i, *_: (0, 0))],
        out_specs=[pl.BlockSpec((tc, D_MODEL), lambda i, *_: (jnp.minimum(i, tiles1 - 1), 0)),
                   pl.BlockSpec((tc, D_MODEL), lambda i, *_: (jnp.maximum(i - tiles1, 0), 0))],
        scratch_shapes=[pltpu.VMEM((GATHER_SLOTS, 2, tc * ROW_PITCH, LANES), F32),
                        pltpu.SemaphoreType.DMA((GATHER_SLOTS,))],
    )
    return pl.pallas_call(
        functools.partial(_combine_body, tc=tc, tiles1=tiles1),
        out_shape=[jax.ShapeDtypeStruct((n_first, D_MODEL), F32),
                   jax.ShapeDtypeStruct((m - n_first, D_MODEL), F32)],
        grid_spec=grid_spec,
        compiler_params=_cparams(("arbitrary",), 16 * tc * D_MODEL * 4 / 2**20 + 8),
        name="moe_combine_ln3",
    )(eid, rank, row_off, ys, wts, x, g, b)


def _moe(x, x_rows, sel, wts, cnt, w_gate, w_up, w_down, g, b, *, n_first, tc):
    n = x.shape[0]
    ng, ne = N_EXPERT_GROUPS, N_EXPERTS

    tm = MOE_TILE
    n_tiles = (2 * n) // tm + ne
    counts = cnt[0, ng:ng + ne]
    tiles_per = (counts + tm - 1) // tm
    tile_end = jnp.cumsum(tiles_per)
    row_off = (tile_end - tiles_per) * tm
    nact = tile_end[-1]
    a_eid, a_rank = sel[:, 0:2].reshape(-1), sel[:, 2:4].reshape(-1)
    row_off = row_off.astype(I32)
    tile_ids = jnp.minimum(jnp.arange(n_tiles, dtype=I32), nact - 1)
    tile_expert = jnp.sum((tile_end[None, :] <= tile_ids[:, None]).astype(I32), axis=1)
    used = tiles_per > 0
    eid = jnp.arange(ne, dtype=I32)
    ordinal = jnp.cumsum(used.astype(I32)) - 1
    later = jnp.where(jnp.logical_and(used[None, :], eid[None, :] > eid[:, None]), eid[None, :], ne)
    nxt = jnp.min(later, axis=1)
    nxt = jnp.where(nxt == ne, -1, nxt)

    nact = nact.reshape(1).astype(I32)
    x_sorted = _moe_dispatch(x_rows, a_eid, a_rank, row_off, (row_off + counts).astype(I32),
                             (tiles_per * tm - counts).astype(I32), nact, td=tc, n_tiles=n_tiles)
    ys = _moe_experts(x_sorted, w_gate, w_up, w_down, tile_expert, ordinal[tile_expert],
                      nxt[tile_expert], nact, n_tiles=n_tiles)
    return _moe_combine(ys, a_eid, a_rank, row_off, wts, x, g, b, tc=tc, n_first=n_first)


def _row_tile(m, cap):
    best = SUBLANES
    for t in range(SUBLANES, cap + 1, SUBLANES):
        if m % t == 0:
            best = t
    return best


def kernel(x_prompt, x_sample, cache_win_k, cache_win_v, state_ssm_re, state_ssm_im, cache_mem_k, cache_mem_v, mem_prompt, w_in, ssm_lam_re, ssm_lam_im, ssm_log_dt, ssm_b_re, ssm_b_im, ssm_c_re, ssm_c_im, ssm_d, w_glu, g_attn, g_ssm, w_out, ln1_g, ln1_b, w_mq, w_mk, w_mv, w_mo, ln2_g, ln2_b, w_r1, b_r1, w_r2, b_r2, w_gate, w_up, w_down, ln3_g, ln3_b):
    nb, seq, d = x_prompt.shape
    ns, dseq, _ = x_sample.shape
    n_p, n_s = nb * seq, ns * dseq
    n = n_p + n_s
    l = 0
    row2 = lambda v: v[l].reshape(1, -1)

    x_p, x_s = x_prompt.reshape(n_p, d), x_sample.reshape(n_s, d)
    tm_p = _row_tile(n_p, 1024)
    tm_ln = _row_tile(n_p, 512)
    assert n_p % n_s == 0 and n_s % SUBLANES == 0

    proj_p = _matmul(x_p, w_in[l], tm=tm_p, tn=1024, name="proj_in_prompt")
    proj_s = _matmul(x_s, w_in[l], tm=n_s, tn=1024, name="proj_in_sample")

    attn_p = _attn_prompt(proj_p, n_batch=nb, seq=seq)
    attn_s = _attn_sample(proj_s, cache_win_k[l], cache_win_v[l], row0=0, n_seq=ns, n_new=dseq)

    seg_len = seq // SSM_SEGMENTS
    prm = _ssm_params(ssm_lam_re[l], ssm_lam_im[l], ssm_log_dt[l], ssm_b_re[l], ssm_b_im[l],
                      ssm_c_re[l], ssm_c_im[l], ssm_d[l], seg_len)
    zeros = jnp.zeros((nb * SSM_SEGMENTS, N_SSM_GROUPS * SSM_STATE), F32)
    tl = _row_tile(seg_len, 32)
    end_re, end_im = _ssm_scan(proj_p, prm, zeros, zeros, seq_len=seg_len, tl=tl, nseg=1,
                               emit_y=False, exact_in=False, name="ssm_state_prompt")
    yg_p, fin_re, fin_im = _ssm_scan(proj_p, prm, end_re, end_im, seq_len=seg_len, tl=tl,
                                     nseg=SSM_SEGMENTS, emit_y=True, exact_in=False,
                                     name="ssm_scan_prompt")
    last = SSM_SEGMENTS - 1
    ssm_re_p = fin_re.reshape(nb, SSM_SEGMENTS, N_SSM_GROUPS, SSM_STATE)[:, last]
    ssm_im_p = fin_im.reshape(nb, SSM_SEGMENTS, N_SSM_GROUPS, SSM_STATE)[:, last]

    h0_re = state_ssm_re[l].reshape(ns, -1)
    h0_im = state_ssm_im[l].reshape(ns, -1)
    yg_s, ssm_re_s, ssm_im_s = _ssm_scan(proj_s, prm, h0_re, h0_im, seq_len=dseq, tl=dseq, nseg=1,
                                         emit_y=True, exact_in=True, name="ssm_scan_sample")
    w_glu_b = w_glu[l].astype(BF16)
    ssm_out_p = _glu(yg_p, w_glu_b, tm=tm_p, name="ssm_glu_prompt")
    ssm_out_s = _glu(yg_s, w_glu_b, tm=n_s, name="ssm_glu_sample")

    mix_args = (row2(g_attn), row2(g_ssm), w_out[l].astype(BF16))
    ln1 = (row2(ln1_g), row2(ln1_b))
    x1_p = _mix(attn_p, ssm_out_p, *mix_args, x_p, *ln1, tm=tm_ln, name="mix_out_ln1_prompt")
    x1_s = _mix(attn_s, ssm_out_s, *mix_args, x_s, *ln1, tm=n_s, name="mix_out_ln1_sample")

    mem_rows = mem_prompt.reshape(nb * N_MEM, d)
    mem_k = _matmul(mem_rows, w_mk[l], tm=nb * N_MEM, tn=1024, name="mem_k")
    mem_v = _matmul(mem_rows, w_mv[l], tm=nb * N_MEM, tn=1024, name="mem_v")
    q_p = _matmul(x1_p, w_mq[l], tm=tm_p, tn=1024, name="mem_q_prompt", out_dtype=BF16)
    q_s = _matmul(x1_s, w_mq[l], tm=n_s, tn=1024, name="mem_q_sample")
    o_p = _memattn(q_p, mem_k.reshape(nb, N_MEM, d), mem_v.reshape(nb, N_MEM, d),
                   row0=0, n_seq=nb, seq=seq, tq=_row_tile(seq, 512), name="memattn_prompt")
    o_s = _memattn_heads(q_s, cache_mem_k[l], cache_mem_v[l], row0=0, n_seq=ns, seq=dseq,
                         name="memattn_sample")
    w_r, b_r = _router_weights(w_r1[l], b_r1[l], w_r2[l], b_r2[l])
    x2, x2_rows, sel, wts, cnt = _mm_ln(o_p, o_s, w_mo[l].astype(BF16), x1_p, x1_s, row2(ln2_g),
                                        row2(ln2_b), w_r, b_r, name="mem_out_ln2_route")

    y_p, y_s = _moe(x2, x2_rows, sel, wts, cnt, w_gate[l], w_up[l], w_down[l], row2(ln3_g),
                    row2(ln3_b), n_first=n_p, tc=_row_tile(n_s, 256))

    y_p = y_p.reshape(nb, seq, d)
    y_s = y_s.reshape(ns, dseq, d)
    wp = min(max(w for w, _ in DILATIONS), seq)
    k_p, v_p = _kv_window(proj_p, n_batch=nb, seq=seq, window=wp, tr=_row_tile(wp, 512))
    k_s = proj_s[:, D_ATT:2 * D_ATT].reshape(ns, dseq, ATT_HEADS, ATT_HD)
    v_s = proj_s[:, 2 * D_ATT:3 * D_ATT].reshape(ns, dseq, ATT_HEADS, ATT_HD)
    state = lambda v, b_: v.reshape(1, b_, N_SSM_GROUPS, SSM_STATE)
    return (y_p, y_s, k_p[None], v_p[None], k_s[None], v_s[None],
            state(ssm_re_p, nb), state(ssm_im_p, nb), state(ssm_re_s, ns), state(ssm_im_s, ns),
            mem_k.reshape(1, nb, N_MEM, MEM_HEADS, MEM_HD),
            mem_v.reshape(1, nb, N_MEM, MEM_HEADS, MEM_HD))
```
